```python
import jax, jax.numpy as jnp
from jax import lax
import numpy as np

D_MODEL = 1024
BATCH = 8
SEQ = 4096
DEPTH = 1

CHUNK = 64
N_META = 16
Q_BLOCK = 128
D_CONV = 512
CONV_WIDTH = 31
N_HEADS = 8
QK_NOPE = 64
QK_ROPE = 32
V_HEAD = 64
D_ATTN = N_HEADS * V_HEAD
Q_LORA = 384
KV_LORA = 256
ROPE_THETA = 10000.0
D_MIX = D_CONV + D_ATTN
D_IN = 2 * D_CONV + Q_LORA + KV_LORA + QK_ROPE
D_FF = 2816
FFN_CONV_WIDTH = 3
EPS = 1e-6
NEG = -1e30

kernel_name = 'hymba_conformer_mla_convffn_block'


def rms_norm(x, g):
    xf = x.astype(jnp.float32)
    y = xf * lax.rsqrt(jnp.mean(xf * xf, axis=-1, keepdims=True) + EPS)
    return (y * g.astype(jnp.float32)).astype(x.dtype)


def layer_norm(x, g, b):
    xf = x.astype(jnp.float32)
    mu = jnp.mean(xf, axis=-1, keepdims=True)
    var = jnp.mean(jnp.square(xf - mu), axis=-1, keepdims=True)
    y = (xf - mu) * lax.rsqrt(var + EPS)
    return (y * g.astype(jnp.float32) + b.astype(jnp.float32)).astype(x.dtype)


def causal_depthwise_conv(x, w, b):
    k = w.shape[0]
    y = lax.conv_general_dilated(
        x, w[:, None, :].astype(x.dtype), window_strides=(1,), padding=[(k - 1, 0)],
        dimension_numbers=('NWC', 'WIO', 'NWC'), feature_group_count=x.shape[-1])
    return y + b.astype(x.dtype)


def rope(x, cos, sin):
    half = x.shape[-1] // 2
    x1, x2 = x[..., :half], x[..., half:]
    return jnp.concatenate([x1 * cos - x2 * sin, x2 * cos + x1 * sin], axis=-1)


def block_causal_attention(q, k, v, chunk_id):
    b, l, h, dqk = q.shape
    dv = v.shape[-1]
    nblk = l // Q_BLOCK
    qb = q.reshape(b, nblk, Q_BLOCK, h, dqk).transpose(1, 0, 2, 3, 4)
    cb = chunk_id.reshape(nblk, Q_BLOCK)
    scale = dqk ** -0.5

    def one_block(args):
        qi, ci = args
        s = jnp.einsum('bqhd,bkhd->bhqk', qi, k, preferred_element_type=jnp.float32) * scale
        visible = ci[:, None] >= chunk_id[None, :]
        s = jnp.where(visible[None, None], s, NEG)
        p = jax.nn.softmax(s, axis=-1)
        return jnp.einsum('bhqk,bkhd->bqhd', p.astype(v.dtype), v)

    o = lax.map(one_block, (qb, cb))
    return o.transpose(1, 0, 2, 3, 4).reshape(b, l, h * dv)


def hybrid_layer(h, cos, sin, chunk_id, mix_norm_g, w_in, q_norm_g, w_uq, kv_norm_g, w_ukv,
                 conv_w, conv_b, conv_ln_g, conv_ln_b, conv_out_g, attn_out_g, w_out,
                 ffn_norm_g, w_ffn_up, ffn_conv_w, ffn_conv_b, w_ffn_down):
    b, l, _ = h.shape
    n = rms_norm(h, mix_norm_g)
    z = n @ w_in.astype(h.dtype)
    a, gate, c_q, c_kv, k_r = jnp.split(
        z, [D_CONV, 2 * D_CONV, 2 * D_CONV + Q_LORA, 2 * D_CONV + Q_LORA + KV_LORA], axis=-1)

    u = a * jax.nn.sigmoid(gate)
    u = causal_depthwise_conv(u, conv_w, conv_b)
    u = jax.nn.silu(layer_norm(u, conv_ln_g, conv_ln_b))

    q = (rms_norm(c_q, q_norm_g) @ w_uq.astype(h.dtype)).reshape(b, l, N_HEADS, QK_NOPE + QK_ROPE)
    kv = (rms_norm(c_kv, kv_norm_g) @ w_ukv.astype(h.dtype)).reshape(b, l, N_HEADS, QK_NOPE + V_HEAD)
    q_nope, q_rot = q[..., :QK_NOPE], q[..., QK_NOPE:]
    k_nope, v = kv[..., :QK_NOPE], kv[..., QK_NOPE:]
    q_rot = rope(q_rot, cos[:, None, :], sin[:, None, :])
    k_rot = rope(k_r, cos, sin)
    qf = jnp.concatenate([q_nope, q_rot], axis=-1)
    kf = jnp.concatenate(
        [k_nope, jnp.broadcast_to(k_rot[:, :, None, :], (b, l, N_HEADS, QK_ROPE))], axis=-1)
    o = block_causal_attention(qf, kf, v, chunk_id)

    mix = jnp.concatenate([rms_norm(u, conv_out_g), rms_norm(o, attn_out_g)], axis=-1)
    h = h + mix @ w_out.astype(h.dtype)

    n2 = rms_norm(h, ffn_norm_g)
    up = causal_depthwise_conv(n2 @ w_ffn_up.astype(h.dtype), ffn_conv_w, ffn_conv_b)
    g, val = up[..., :D_FF], up[..., D_FF:]
    return h + (jax.nn.silu(g) * val) @ w_ffn_down.astype(h.dtype)


def _fwd_setup_inputs(seed: int = 0) -> dict:
    key = jax.random.key(seed)
    ks = jax.random.split(key, 24)
    f32 = jnp.float32

    def nrm(k, shape, scale):
        return jax.random.normal(k, shape, f32) * scale

    def gain(k, shape):
        return 1.0 + 0.02 * jax.random.normal(k, shape, f32)

    L = DEPTH
    return {
        'x': jax.random.normal(ks[0], (BATCH, SEQ, D_MODEL), f32),
        'meta_tokens': nrm(ks[1], (N_META, D_MODEL), 1.0),
        'mix_norm_g': gain(ks[2], (L, D_MODEL)),
        'w_in': nrm(ks[3], (L, D_MODEL, D_IN), D_MODEL ** -0.5),
        'q_norm_g': gain(ks[4], (L, Q_LORA)),
        'w_uq': nrm(ks[5], (L, Q_LORA, N_HEADS * (QK_NOPE + QK_ROPE)), Q_LORA ** -0.5),
        'kv_norm_g': gain(ks[6], (L, KV_LORA)),
        'w_ukv': nrm(ks[7], (L, KV_LORA, N_HEADS * (QK_NOPE + V_HEAD)), KV_LORA ** -0.5),
        'conv_w': nrm(ks[8], (L, CONV_WIDTH, D_CONV), CONV_WIDTH ** -0.5),
        'conv_b': nrm(ks[9], (L, D_CONV), 0.02),
        'conv_ln_g': gain(ks[10], (L, D_CONV)),
        'conv_ln_b': nrm(ks[11], (L, D_CONV), 0.02),
        'conv_out_g': gain(ks[12], (L, D_CONV)),
        'attn_out_g': gain(ks[13], (L, D_ATTN)),
        'w_out': nrm(ks[14], (L, D_MIX, D_MODEL), D_MIX ** -0.5),
        'ffn_norm_g': gain(ks[15], (L, D_MODEL)),
        'w_ffn_up': nrm(ks[16], (L, D_MODEL, 2 * D_FF), D_MODEL ** -0.5),
        'ffn_conv_w': nrm(ks[17], (L, FFN_CONV_WIDTH, 2 * D_FF), FFN_CONV_WIDTH ** -0.5),
        'ffn_conv_b': nrm(ks[18], (L, 2 * D_FF), 0.02),
        'w_ffn_down': nrm(ks[19], (L, D_FF, D_MODEL), D_FF ** -0.5),
        'final_norm_g': gain(ks[20], (D_MODEL,)),
    }


def _fwd_reference(x, meta_tokens, mix_norm_g, w_in, q_norm_g, w_uq, kv_norm_g, w_ukv,
              conv_w, conv_b, conv_ln_g, conv_ln_b, conv_out_g, attn_out_g, w_out,
              ffn_norm_g, w_ffn_up, ffn_conv_w, ffn_conv_b, w_ffn_down, final_norm_g):
    b, s, d = x.shape
    l_real = N_META + s
    l_pad = ((l_real + Q_BLOCK - 1) // Q_BLOCK) * Q_BLOCK
    meta = jnp.broadcast_to(meta_tokens[None].astype(x.dtype), (b, N_META, d))
    pad = jnp.zeros((b, l_pad - l_real, d), x.dtype)
    h = jnp.concatenate([meta, x, pad], axis=1)

    pos = jnp.arange(l_pad, dtype=jnp.int32)
    chunk_id = jnp.where(pos < N_META, 0, 1 + (pos - N_META) // CHUNK).astype(jnp.int32)
    inv_freq = 1.0 / (ROPE_THETA ** (jnp.arange(0, QK_ROPE, 2, dtype=jnp.float32) / QK_ROPE))
    ang = pos.astype(jnp.float32)[:, None] * inv_freq[None, :]
    cos = jnp.cos(ang).astype(x.dtype)
    sin = jnp.sin(ang).astype(x.dtype)

    for i in range(DEPTH):
        h = hybrid_layer(h, cos, sin, chunk_id, mix_norm_g[i], w_in[i], q_norm_g[i], w_uq[i],
                         kv_norm_g[i], w_ukv[i], conv_w[i], conv_b[i], conv_ln_g[i], conv_ln_b[i],
                         conv_out_g[i], attn_out_g[i], w_out[i], ffn_norm_g[i], w_ffn_up[i],
                         ffn_conv_w[i], ffn_conv_b[i], w_ffn_down[i])

    h = rms_norm(h, final_norm_g)
    return h[:, N_META:N_META + s]


import jax as _jax
import jax.numpy as _jnp

TWIN_FORMAT = 'train_step'
FWD_PARAMS = ['x', 'meta_tokens', 'mix_norm_g', 'w_in', 'q_norm_g', 'w_uq', 'kv_norm_g', 'w_ukv', 'conv_w', 'conv_b', 'conv_ln_g', 'conv_ln_b', 'conv_out_g', 'attn_out_g', 'w_out', 'ffn_norm_g', 'w_ffn_up', 'ffn_conv_w', 'ffn_conv_b', 'w_ffn_down', 'final_norm_g']
TWIN_WEIGHTS = ['meta_tokens', 'mix_norm_g', 'w_in', 'q_norm_g', 'w_uq', 'kv_norm_g', 'w_ukv', 'conv_w', 'conv_b', 'conv_ln_g', 'conv_ln_b', 'conv_out_g', 'attn_out_g', 'w_out', 'ffn_norm_g', 'w_ffn_up', 'ffn_conv_w', 'ffn_conv_b', 'w_ffn_down', 'final_norm_g']
TWIN_DIFF_INPUT = 'x'
TWIN_INPUTS = ['x', 'meta_tokens', 'mix_norm_g', 'w_in', 'q_norm_g', 'w_uq', 'kv_norm_g', 'w_ukv', 'conv_w', 'conv_b', 'conv_ln_g', 'conv_ln_b', 'conv_out_g', 'attn_out_g', 'w_out', 'ffn_norm_g', 'w_ffn_up', 'ffn_conv_w', 'ffn_conv_b', 'w_ffn_down', 'final_norm_g', 'loss_target', 'm_meta_tokens', 'm_mix_norm_g', 'm_w_in', 'm_q_norm_g', 'm_w_uq', 'm_kv_norm_g', 'm_w_ukv', 'm_conv_w', 'm_conv_b', 'm_conv_ln_g', 'm_conv_ln_b', 'm_conv_out_g', 'm_attn_out_g', 'm_w_out', 'm_ffn_norm_g', 'm_w_ffn_up', 'm_ffn_conv_w', 'm_ffn_conv_b', 'm_w_ffn_down', 'm_final_norm_g', 'v_meta_tokens', 'v_mix_norm_g', 'v_w_in', 'v_q_norm_g', 'v_w_uq', 'v_kv_norm_g', 'v_w_ukv', 'v_conv_w', 'v_conv_b', 'v_conv_ln_g', 'v_conv_ln_b', 'v_conv_out_g', 'v_attn_out_g', 'v_w_out', 'v_ffn_norm_g', 'v_w_ffn_up', 'v_ffn_conv_w', 'v_ffn_conv_b', 'v_w_ffn_down', 'v_final_norm_g']
TWIN_OUTPUTS = ['loss', 'grad_x', 'grad_meta_tokens', 'grad_mix_norm_g', 'grad_w_in', 'grad_q_norm_g', 'grad_w_uq', 'grad_kv_norm_g', 'grad_w_ukv', 'grad_conv_w', 'grad_conv_b', 'grad_conv_ln_g', 'grad_conv_ln_b', 'grad_conv_out_g', 'grad_attn_out_g', 'grad_w_out', 'grad_ffn_norm_g', 'grad_w_ffn_up', 'grad_ffn_conv_w', 'grad_ffn_conv_b', 'grad_w_ffn_down', 'grad_final_norm_g', 'delta_meta_tokens', 'delta_mix_norm_g', 'delta_w_in', 'delta_q_norm_g', 'delta_w_uq', 'delta_kv_norm_g', 'delta_w_ukv', 'delta_conv_w', 'delta_conv_b', 'delta_conv_ln_g', 'delta_conv_ln_b', 'delta_conv_out_g', 'delta_attn_out_g', 'delta_w_out', 'delta_ffn_norm_g', 'delta_w_ffn_up', 'delta_ffn_conv_w', 'delta_ffn_conv_b', 'delta_w_ffn_down', 'delta_final_norm_g', 'new_m_meta_tokens', 'new_m_mix_norm_g', 'new_m_w_in', 'new_m_q_norm_g', 'new_m_w_uq', 'new_m_kv_norm_g', 'new_m_w_ukv', 'new_m_conv_w', 'new_m_conv_b', 'new_m_conv_ln_g', 'new_m_conv_ln_b', 'new_m_conv_out_g', 'new_m_attn_out_g', 'new_m_w_out', 'new_m_ffn_norm_g', 'new_m_w_ffn_up', 'new_m_ffn_conv_w', 'new_m_ffn_conv_b', 'new_m_w_ffn_down', 'new_m_final_norm_g', 'new_v_meta_tokens', 'new_v_mix_norm_g', 'new_v_w_in', 'new_v_q_norm_g', 'new_v_w_uq', 'new_v_kv_norm_g', 'new_v_w_ukv', 'new_v_conv_w', 'new_v_conv_b', 'new_v_conv_ln_g', 'new_v_conv_ln_b', 'new_v_conv_out_g', 'new_v_attn_out_g', 'new_v_w_out', 'new_v_ffn_norm_g', 'new_v_w_ffn_up', 'new_v_ffn_conv_w', 'new_v_ffn_conv_b', 'new_v_w_ffn_down', 'new_v_final_norm_g']
TWIN_LEAF_KINDS = {'loss': 'loss', 'grad_x': 'grad_x', 'grad_meta_tokens': 'grad_w', 'grad_mix_norm_g': 'grad_w', 'grad_w_in': 'grad_w', 'grad_q_norm_g': 'grad_w', 'grad_w_uq': 'grad_w', 'grad_kv_norm_g': 'grad_w', 'grad_w_ukv': 'grad_w', 'grad_conv_w': 'grad_w', 'grad_conv_b': 'grad_w', 'grad_conv_ln_g': 'grad_w', 'grad_conv_ln_b': 'grad_w', 'grad_conv_out_g': 'grad_w', 'grad_attn_out_g': 'grad_w', 'grad_w_out': 'grad_w', 'grad_ffn_norm_g': 'grad_w', 'grad_w_ffn_up': 'grad_w', 'grad_ffn_conv_w': 'grad_w', 'grad_ffn_conv_b': 'grad_w', 'grad_w_ffn_down': 'grad_w', 'grad_final_norm_g': 'grad_w', 'delta_meta_tokens': 'delta_w', 'delta_mix_norm_g': 'delta_w', 'delta_w_in': 'delta_w', 'delta_q_norm_g': 'delta_w', 'delta_w_uq': 'delta_w', 'delta_kv_norm_g': 'delta_w', 'delta_w_ukv': 'delta_w', 'delta_conv_w': 'delta_w', 'delta_conv_b': 'delta_w', 'delta_conv_ln_g': 'delta_w', 'delta_conv_ln_b': 'delta_w', 'delta_conv_out_g': 'delta_w', 'delta_attn_out_g': 'delta_w', 'delta_w_out': 'delta_w', 'delta_ffn_norm_g': 'delta_w', 'delta_w_ffn_up': 'delta_w', 'delta_ffn_conv_w': 'delta_w', 'delta_ffn_conv_b': 'delta_w', 'delta_w_ffn_down': 'delta_w', 'delta_final_norm_g': 'delta_w', 'new_m_meta_tokens': 'new_m', 'new_m_mix_norm_g': 'new_m', 'new_m_w_in': 'new_m', 'new_m_q_norm_g': 'new_m', 'new_m_w_uq': 'new_m', 'new_m_kv_norm_g': 'new_m', 'new_m_w_ukv': 'new_m', 'new_m_conv_w': 'new_m', 'new_m_conv_b': 'new_m', 'new_m_conv_ln_g': 'new_m', 'new_m_conv_ln_b': 'new_m', 'new_m_conv_out_g': 'new_m', 'new_m_attn_out_g': 'new_m', 'new_m_w_out': 'new_m', 'new_m_ffn_norm_g': 'new_m', 'new_m_w_ffn_up': 'new_m', 'new_m_ffn_conv_w': 'new_m', 'new_m_ffn_conv_b': 'new_m', 'new_m_w_ffn_down': 'new_m', 'new_m_final_norm_g': 'new_m', 'new_v_meta_tokens': 'new_v', 'new_v_mix_norm_g': 'new_v', 'new_v_w_in': 'new_v', 'new_v_q_norm_g': 'new_v', 'new_v_w_uq': 'new_v', 'new_v_kv_norm_g': 'new_v', 'new_v_w_ukv': 'new_v', 'new_v_conv_w': 'new_v', 'new_v_conv_b': 'new_v', 'new_v_conv_ln_g': 'new_v', 'new_v_conv_ln_b': 'new_v', 'new_v_conv_out_g': 'new_v', 'new_v_attn_out_g': 'new_v', 'new_v_w_out': 'new_v', 'new_v_ffn_norm_g': 'new_v', 'new_v_w_ffn_up': 'new_v', 'new_v_ffn_conv_w': 'new_v', 'new_v_ffn_conv_b': 'new_v', 'new_v_w_ffn_down': 'new_v', 'new_v_final_norm_g': 'new_v'}


def _forward(args):
    return _fwd_reference(*[args[k] for k in FWD_PARAMS])


def _output_shape():
    def fwd():
        inp = _fwd_setup_inputs(0)
        return _fwd_reference(*[inp[k] for k in FWD_PARAMS])
    out = _jax.eval_shape(fwd)
    return out.shape, out.dtype

N_MICROBATCH = 1
ADAM_LR = 0.001
ADAM_B1 = 0.9
ADAM_B2 = 0.999
ADAM_EPS = 1e-08
ADAM_WD = 0.01
ADAM_STEP = 10
PER_EXAMPLE_BATCH_AXIS = {'x': 0, 'loss_target': 0}
SHARED_INPUTS = []
_WEIGHT_DTYPES = {'meta_tokens': _jnp.float32, 'mix_norm_g': _jnp.float32, 'w_in': _jnp.float32, 'q_norm_g': _jnp.float32, 'w_uq': _jnp.float32, 'kv_norm_g': _jnp.float32, 'w_ukv': _jnp.float32, 'conv_w': _jnp.float32, 'conv_b': _jnp.float32, 'conv_ln_g': _jnp.float32, 'conv_ln_b': _jnp.float32, 'conv_out_g': _jnp.float32, 'attn_out_g': _jnp.float32, 'w_out': _jnp.float32, 'ffn_norm_g': _jnp.float32, 'w_ffn_up': _jnp.float32, 'ffn_conv_w': _jnp.float32, 'ffn_conv_b': _jnp.float32, 'w_ffn_down': _jnp.float32, 'final_norm_g': _jnp.float32}
MOMENT_SCALE = {'meta_tokens': 1.221558e-02, 'mix_norm_g': 1.879607e-01, 'w_in': 1.457090e-01, 'q_norm_g': 1.673937e-01, 'w_uq': 1.117250e-01, 'kv_norm_g': 2.937765e-01, 'w_ukv': 1.342945e-01, 'conv_w': 1.427767e-01, 'conv_b': 3.481158e-01, 'conv_ln_g': 2.004934e-01, 'conv_ln_b': 1.879708e-01, 'conv_out_g': 1.662678e-01, 'attn_out_g': 1.427623e-01, 'w_out': 1.393659e-01, 'ffn_norm_g': 1.025388e-01, 'w_ffn_up': 4.316502e-02, 'ffn_conv_w': 4.346030e-02, 'ffn_conv_b': 4.511582e-02, 'w_ffn_down': 7.071206e-02, 'final_norm_g': 3.209620e+01}


def _to_microbatches(a, axis):
    t = _jnp.moveaxis(a, axis, 0)
    t = t.reshape((N_MICROBATCH, t.shape[0] // N_MICROBATCH) + t.shape[1:])
    return _jnp.moveaxis(t, 1, axis + 1)


def setup_inputs(seed: int = 0) -> dict:
    inp = _fwd_setup_inputs(seed)
    key = _jax.random.fold_in(_jax.random.key(seed), 7919)
    shape, _ = _output_shape()
    out = dict(inp)
    out["loss_target"] = _jax.random.normal(_jax.random.fold_in(key, 0), shape, _jnp.float32)
    for i, name in enumerate(TWIN_WEIGHTS):
        w = inp[name].astype(_jnp.float32)
        if MOMENT_SCALE is None:
            s = _jnp.sqrt(_jnp.mean(_jnp.square(w)) + 1e-30)
        else:
            s = MOMENT_SCALE[name]
        km, kv = _jax.random.split(_jax.random.fold_in(key, i + 1))
        out[name] = w
        out["m_" + name] = s * _jax.random.normal(km, w.shape, _jnp.float32)
        out["v_" + name] = (s * s) * _jax.random.uniform(kv, w.shape, _jnp.float32, 0.5, 1.5)
    if N_MICROBATCH > 1:
        for name, axis in PER_EXAMPLE_BATCH_AXIS.items():
            out[name] = _to_microbatches(out[name], axis)
    return {'x': out['x'], 'meta_tokens': out['meta_tokens'], 'mix_norm_g': out['mix_norm_g'], 'w_in': out['w_in'], 'q_norm_g': out['q_norm_g'], 'w_uq': out['w_uq'], 'kv_norm_g': out['kv_norm_g'], 'w_ukv': out['w_ukv'], 'conv_w': out['conv_w'], 'conv_b': out['conv_b'], 'conv_ln_g': out['conv_ln_g'], 'conv_ln_b': out['conv_ln_b'], 'conv_out_g': out['conv_out_g'], 'attn_out_g': out['attn_out_g'], 'w_out': out['w_out'], 'ffn_norm_g': out['ffn_norm_g'], 'w_ffn_up': out['w_ffn_up'], 'ffn_conv_w': out['ffn_conv_w'], 'ffn_conv_b': out['ffn_conv_b'], 'w_ffn_down': out['w_ffn_down'], 'final_norm_g': out['final_norm_g'], 'loss_target': out['loss_target'], 'm_meta_tokens': out['m_meta_tokens'], 'm_mix_norm_g': out['m_mix_norm_g'], 'm_w_in': out['m_w_in'], 'm_q_norm_g': out['m_q_norm_g'], 'm_w_uq': out['m_w_uq'], 'm_kv_norm_g': out['m_kv_norm_g'], 'm_w_ukv': out['m_w_ukv'], 'm_conv_w': out['m_conv_w'], 'm_conv_b': out['m_conv_b'], 'm_conv_ln_g': out['m_conv_ln_g'], 'm_conv_ln_b': out['m_conv_ln_b'], 'm_conv_out_g': out['m_conv_out_g'], 'm_attn_out_g': out['m_attn_out_g'], 'm_w_out': out['m_w_out'], 'm_ffn_norm_g': out['m_ffn_norm_g'], 'm_w_ffn_up': out['m_w_ffn_up'], 'm_ffn_conv_w': out['m_ffn_conv_w'], 'm_ffn_conv_b': out['m_ffn_conv_b'], 'm_w_ffn_down': out['m_w_ffn_down'], 'm_final_norm_g': out['m_final_norm_g'], 'v_meta_tokens': out['v_meta_tokens'], 'v_mix_norm_g': out['v_mix_norm_g'], 'v_w_in': out['v_w_in'], 'v_q_norm_g': out['v_q_norm_g'], 'v_w_uq': out['v_w_uq'], 'v_kv_norm_g': out['v_kv_norm_g'], 'v_w_ukv': out['v_w_ukv'], 'v_conv_w': out['v_conv_w'], 'v_conv_b': out['v_conv_b'], 'v_conv_ln_g': out['v_conv_ln_g'], 'v_conv_ln_b': out['v_conv_ln_b'], 'v_conv_out_g': out['v_conv_out_g'], 'v_attn_out_g': out['v_attn_out_g'], 'v_w_out': out['v_w_out'], 'v_ffn_norm_g': out['v_ffn_norm_g'], 'v_w_ffn_up': out['v_w_ffn_up'], 'v_ffn_conv_w': out['v_ffn_conv_w'], 'v_ffn_conv_b': out['v_ffn_conv_b'], 'v_w_ffn_down': out['v_w_ffn_down'], 'v_final_norm_g': out['v_final_norm_g']}


def _loss(weights, diff, rest, loss_target):
    with _jax.named_scope("forward"):
        args = {**rest, TWIN_DIFF_INPUT: diff, **{k: w.astype(_WEIGHT_DTYPES[k]) for k, w in weights.items()}}
        y = _forward(args)
    with _jax.named_scope("loss_head"):
        err = _jnp.square(y.astype(_jnp.float32) - loss_target)
        return 0.5 * _jnp.sum(_jnp.mean(err, axis=-1)) if err.ndim else 0.5 * err


def _adamw(w, g, m, v):
    m = ADAM_B1 * m + (1.0 - ADAM_B1) * g
    v = ADAM_B2 * v + (1.0 - ADAM_B2) * _jnp.square(g)
    m_hat = m / (1.0 - ADAM_B1 ** ADAM_STEP)
    v_hat = v / (1.0 - ADAM_B2 ** ADAM_STEP)
    delta = -ADAM_LR * (m_hat / (_jnp.sqrt(v_hat) + ADAM_EPS) + ADAM_WD * w)
    return delta, m, v


def reference(x, meta_tokens, mix_norm_g, w_in, q_norm_g, w_uq, kv_norm_g, w_ukv, conv_w, conv_b, conv_ln_g, conv_ln_b, conv_out_g, attn_out_g, w_out, ffn_norm_g, w_ffn_up, ffn_conv_w, ffn_conv_b, w_ffn_down, final_norm_g, loss_target, m_meta_tokens, m_mix_norm_g, m_w_in, m_q_norm_g, m_w_uq, m_kv_norm_g, m_w_ukv, m_conv_w, m_conv_b, m_conv_ln_g, m_conv_ln_b, m_conv_out_g, m_attn_out_g, m_w_out, m_ffn_norm_g, m_w_ffn_up, m_ffn_conv_w, m_ffn_conv_b, m_w_ffn_down, m_final_norm_g, v_meta_tokens, v_mix_norm_g, v_w_in, v_q_norm_g, v_w_uq, v_kv_norm_g, v_w_ukv, v_conv_w, v_conv_b, v_conv_ln_g, v_conv_ln_b, v_conv_out_g, v_attn_out_g, v_w_out, v_ffn_norm_g, v_w_ffn_up, v_ffn_conv_w, v_ffn_conv_b, v_w_ffn_down, v_final_norm_g):
    given = dict(x=x, meta_tokens=meta_tokens, mix_norm_g=mix_norm_g, w_in=w_in, q_norm_g=q_norm_g, w_uq=w_uq, kv_norm_g=kv_norm_g, w_ukv=w_ukv, conv_w=conv_w, conv_b=conv_b, conv_ln_g=conv_ln_g, conv_ln_b=conv_ln_b, conv_out_g=conv_out_g, attn_out_g=attn_out_g, w_out=w_out, ffn_norm_g=ffn_norm_g, w_ffn_up=w_ffn_up, ffn_conv_w=ffn_conv_w, ffn_conv_b=ffn_conv_b, w_ffn_down=w_ffn_down, final_norm_g=final_norm_g, loss_target=loss_target, m_meta_tokens=m_meta_tokens, m_mix_norm_g=m_mix_norm_g, m_w_in=m_w_in, m_q_norm_g=m_q_norm_g, m_w_uq=m_w_uq, m_kv_norm_g=m_kv_norm_g, m_w_ukv=m_w_ukv, m_conv_w=m_conv_w, m_conv_b=m_conv_b, m_conv_ln_g=m_conv_ln_g, m_conv_ln_b=m_conv_ln_b, m_conv_out_g=m_conv_out_g, m_attn_out_g=m_attn_out_g, m_w_out=m_w_out, m_ffn_norm_g=m_ffn_norm_g, m_w_ffn_up=m_w_ffn_up, m_ffn_conv_w=m_ffn_conv_w, m_ffn_conv_b=m_ffn_conv_b, m_w_ffn_down=m_w_ffn_down, m_final_norm_g=m_final_norm_g, v_meta_tokens=v_meta_tokens, v_mix_norm_g=v_mix_norm_g, v_w_in=v_w_in, v_q_norm_g=v_q_norm_g, v_w_uq=v_w_uq, v_kv_norm_g=v_kv_norm_g, v_w_ukv=v_w_ukv, v_conv_w=v_conv_w, v_conv_b=v_conv_b, v_conv_ln_g=v_conv_ln_g, v_conv_ln_b=v_conv_ln_b, v_conv_out_g=v_conv_out_g, v_attn_out_g=v_attn_out_g, v_w_out=v_w_out, v_ffn_norm_g=v_ffn_norm_g, v_w_ffn_up=v_w_ffn_up, v_ffn_conv_w=v_ffn_conv_w, v_ffn_conv_b=v_ffn_conv_b, v_w_ffn_down=v_w_ffn_down, v_final_norm_g=v_final_norm_g)
    weights = {n: given[n] for n in TWIN_WEIGHTS}
    shared = {n: given[n] for n in SHARED_INPUTS}
    per_example = {n: given[n] for n in ['x']}
    grad_fn = _jax.value_and_grad(_loss, argnums=(0, 1))

    def one_microbatch(ex, loss_target):
        ex = dict(ex)
        diff = ex.pop(TWIN_DIFF_INPUT)
        return grad_fn(weights, diff, {**shared, **ex}, loss_target)

    if N_MICROBATCH == 1:
        loss, (grad_w, grad_x) = one_microbatch(per_example, given["loss_target"])
    else:
        def body(carry, xs):
            loss_sum, grad_sum = carry
            l_k, (gw_k, gx_k) = one_microbatch(xs[0], xs[1])
            with _jax.named_scope("update"):
                return (loss_sum + l_k, _jax.tree.map(_jnp.add, grad_sum, gw_k)), gx_k

        init = (_jnp.zeros((), _jnp.float32), _jax.tree.map(_jnp.zeros_like, weights))
        (loss, grad_w), grad_x = _jax.lax.scan(body, init, (per_example, given["loss_target"]))
    with _jax.named_scope("update"):
        delta_w, new_m, new_v = {}, {}, {}
        for n in TWIN_WEIGHTS:
            delta_w[n], new_m[n], new_v[n] = _adamw(weights[n], grad_w[n], given["m_" + n], given["v_" + n])
    return (loss, grad_x, *[grad_w[n] for n in TWIN_WEIGHTS], *[delta_w[n] for n in TWIN_WEIGHTS],
            *[new_m[n] for n in TWIN_WEIGHTS], *[new_v[n] for n in TWIN_WEIGHTS])
```

```python
import functools

import jax
import jax.numpy as jnp
from jax import lax
from jax.experimental import pallas as pl
from jax.experimental.pallas import tpu as pltpu

F32 = jnp.float32
BF16 = jnp.bfloat16

N_DEV = 8
D_MODEL = 1024
CHUNK = 64
CHUNK_SHIFT = 6
N_META = 16
D_CONV = 512
CONV_WIDTH = 31
N_HEADS = 8
QK_NOPE = 64
QK_ROPE = 32
QK_DIM = QK_NOPE + QK_ROPE
V_HEAD = 64
KV_HEAD = QK_NOPE + V_HEAD
D_ATTN = N_HEADS * V_HEAD
Q_LORA = 384
KV_LORA = 256
ROPE_THETA = 10000.0
D_IN = 2 * D_CONV + Q_LORA + KV_LORA + QK_ROPE
D_FF = 2816
D_UP = 2 * D_FF
FFN_CONV_WIDTH = 3
UP_SLAB = D_UP // N_DEV
N_ACT_SLAB = D_FF // UP_SLAB
EPS = 1e-6
NEG = -1e30
ADAM_LR = 0.001
ADAM_B1 = 0.9
ADAM_B2 = 0.999
ADAM_EPS = 1e-08
ADAM_WD = 0.01
ADAM_STEP = 10

ROW_TILE = 256
DEAD = ROW_TILE - N_META
CONV_HALO = 32
FFN_HALO = 16
VMEM_LIMIT = 56 * 1024 * 1024

MESH = pl.DeviceIdType.MESH


def _dot(a, b):
    return jnp.dot(a, b, preferred_element_type=F32)


def _dot_nt(a, b):
    return lax.dot_general(a, b, (((1,), (1,)), ((), ())), preferred_element_type=F32)


def _dot_tn(a, b):
    return lax.dot_general(a, b, (((0,), (0,)), ((), ())), preferred_element_type=F32)


def _sigmoid(x):
    return 1.0 / (1.0 + jnp.exp(-x))


def _rms_fwd(x, g):
    r = lax.rsqrt(jnp.mean(x * x, axis=-1, keepdims=True) + EPS)
    return x * r * g


def _rms_bwd(dy, x, g):
    r = lax.rsqrt(jnp.mean(x * x, axis=-1, keepdims=True) + EPS)
    w = dy * g
    dx = r * w - x * (r * r * r) * jnp.mean(w * x, axis=-1, keepdims=True)
    return dx, jnp.sum(dy * x * r, axis=0, keepdims=True)


def _rope(x, cos, sin):
    half = QK_ROPE // 2
    x1, x2 = x[:, :half], x[:, half:]
    return jnp.concatenate([x1 * cos - x2 * sin, x2 * cos + x1 * sin], axis=-1)


def _rope_t(dy, cos, sin):
    half = QK_ROPE // 2
    d1, d2 = dy[:, :half], dy[:, half:]
    return jnp.concatenate([d1 * cos + d2 * sin, d2 * cos - d1 * sin], axis=-1)


def _row_ids(i, rows):
    return i * rows + lax.broadcasted_iota(jnp.int32, (rows, 1), 0)


def _accumulate(ref, first, value):
    @pl.when(first)
    def _():
        ref[...] = value

    @pl.when(jnp.logical_not(first))
    def _():
        ref[...] += value


def _tile_spec(shape):
    nd = len(shape)
    if nd == 2:
        return pl.BlockSpec((ROW_TILE, shape[1]), lambda i: (i, 0))
    return pl.BlockSpec((shape[0], ROW_TILE, shape[2]), lambda i: (0, i, 0))


def _whole_spec(shape):
    nd = len(shape)
    return pl.BlockSpec(tuple(shape), lambda i: (0,) * nd, pipeline_mode=pl.Buffered(1))


def _acc_spec(shape):
    nd = len(shape)
    return pl.BlockSpec(tuple(shape), lambda i: (0,) * nd)


def _real_spec(width):
    return pl.BlockSpec((ROW_TILE, width), lambda i: (jnp.maximum(i - 1, 0), 0))


def _params(*semantics):
    return pltpu.CompilerParams(dimension_semantics=semantics, vmem_limit_bytes=VMEM_LIMIT)


def _fwd_in(x, meta_pad, g1, w_in, n_rows):
    nt = n_rows // ROW_TILE

    def body(x_ref, meta_ref, g_ref, w_ref, nb_ref, ag_ref, cq_ref, ckv_ref, kr_ref):
        i = pl.program_id(0)
        h0 = jnp.where(i == 0, meta_ref[...], x_ref[...])
        nb = _rms_fwd(h0, g_ref[...]).astype(BF16)
        nb_ref[...] = nb
        z = _dot(nb, w_ref[...])
        ag_ref[...] = z[:, :2 * D_CONV]
        cq_ref[...] = z[:, 2 * D_CONV:2 * D_CONV + Q_LORA]
        ckv_ref[...] = z[:, 2 * D_CONV + Q_LORA:2 * D_CONV + Q_LORA + KV_LORA]
        kr_ref[...] = z[:, 2 * D_CONV + Q_LORA + KV_LORA:]

    out_shapes = [
        jax.ShapeDtypeStruct((n_rows, D_MODEL), BF16),
        jax.ShapeDtypeStruct((n_rows, 2 * D_CONV), F32),
        jax.ShapeDtypeStruct((n_rows, Q_LORA), F32),
        jax.ShapeDtypeStruct((n_rows, KV_LORA), F32),
        jax.ShapeDtypeStruct((n_rows, QK_ROPE), F32),
    ]
    return pl.pallas_call(
        body, name="fwd_in", grid=(nt,),
        in_specs=[_real_spec(D_MODEL), _whole_spec(meta_pad.shape), _whole_spec(g1.shape), _whole_spec(w_in.shape)],
        out_specs=[_tile_spec(s.shape) for s in out_shapes],
        out_shape=out_shapes,
        compiler_params=_params("parallel"),
    )(x, meta_pad, g1, w_in)


def _conv_chain(u1, ln_g, ln_b):
    mu = jnp.mean(u1, axis=-1, keepdims=True)
    xc = u1 - mu
    rstd = lax.rsqrt(jnp.mean(xc * xc, axis=-1, keepdims=True) + EPS)
    xh = xc * rstd
    u2 = xh * ln_g + ln_b
    return xh, u2, u2 * _sigmoid(u2), rstd


def _fwd_conv(ag, conv_w, conv_b, ln_g, ln_b, out_g, n_rows):
    nt = n_rows // ROW_TILE

    def body(ag_ref, w_ref, b_ref, lg_ref, lb_ref, og_ref, mix_ref, u1_ref, ext_ref):
        i = pl.program_id(0)

        @pl.when(i == 0)
        def _():
            ext_ref[0:CONV_HALO, :] = jnp.zeros((CONV_HALO, D_CONV), F32)

        ag_t = ag_ref[...]
        live = _row_ids(i, ROW_TILE) >= DEAD
        u0 = jnp.where(live, ag_t[:, :D_CONV] * _sigmoid(ag_t[:, D_CONV:]), 0.0)
        ext_ref[CONV_HALO:, :] = u0
        first = CONV_HALO - (CONV_WIDTH - 1)
        acc = jnp.zeros((ROW_TILE, D_CONV), F32)
        for k in range(CONV_WIDTH):
            acc = acc + w_ref[k:k + 1, :] * ext_ref[first + k:first + k + ROW_TILE, :]
        u1 = acc + b_ref[...]
        ext_ref[0:CONV_HALO, :] = ext_ref[ROW_TILE:ROW_TILE + CONV_HALO, :]
        u1_ref[...] = u1
        _, _, u3, _ = _conv_chain(u1, lg_ref[...], lb_ref[...])
        mix_ref[...] = _rms_fwd(u3, og_ref[...]).astype(BF16)

    out_shapes = [jax.ShapeDtypeStruct((n_rows, D_CONV), BF16), jax.ShapeDtypeStruct((n_rows, D_CONV), F32)]
    small = [conv_w, conv_b, ln_g, ln_b, out_g]
    return pl.pallas_call(
        body, name="fwd_conv", grid=(nt,),
        in_specs=[_tile_spec(ag.shape)] + [_whole_spec(a.shape) for a in small],
        out_specs=[_tile_spec(s.shape) for s in out_shapes],
        out_shape=out_shapes,
        scratch_shapes=[pltpu.VMEM((ROW_TILE + CONV_HALO, D_CONV), F32)],
        compiler_params=_params("arbitrary"),
    )(ag, *small)


def _fwd_qkv(cq, ckv, kr, gq, gkv, w_uq, w_ukv, cos, sin, n_rows):
    nt = n_rows // ROW_TILE

    def body(cq_ref, ckv_ref, kr_ref, gq_ref, gkv_ref, wq_ref, wkv_ref, cos_ref, sin_ref,
             q_ref, k_ref, v_ref, cqn_ref, ckvn_ref):
        cos_t, sin_t = cos_ref[...], sin_ref[...]
        cqn = _rms_fwd(cq_ref[...], gq_ref[...]).astype(BF16)
        ckvn = _rms_fwd(ckv_ref[...], gkv_ref[...]).astype(BF16)
        cqn_ref[...] = cqn
        ckvn_ref[...] = ckvn
        k_rot = _rope(kr_ref[...], cos_t, sin_t)
        for h in range(N_HEADS):
            q_raw = _dot(cqn, wq_ref[h])
            q_ref[h] = jnp.concatenate([q_raw[:, :QK_NOPE], _rope(q_raw[:, QK_NOPE:], cos_t, sin_t)], axis=-1).astype(BF16)
            kv = _dot(ckvn, wkv_ref[h])
            k_ref[h] = jnp.concatenate([kv[:, :QK_NOPE], k_rot], axis=-1).astype(BF16)
            v_ref[h] = kv[:, QK_NOPE:].astype(BF16)

    out_shapes = [
        jax.ShapeDtypeStruct((N_HEADS, n_rows, QK_DIM), BF16),
        jax.ShapeDtypeStruct((N_HEADS, n_rows, QK_DIM), BF16),
        jax.ShapeDtypeStruct((N_HEADS, n_rows, V_HEAD), BF16),
        jax.ShapeDtypeStruct((n_rows, Q_LORA), BF16),
        jax.ShapeDtypeStruct((n_rows, KV_LORA), BF16),
    ]
    tiles = [cq, ckv, kr]
    whole = [gq, gkv, w_uq, w_ukv]
    return pl.pallas_call(
        body, name="fwd_qkv", grid=(nt,),
        in_specs=[_tile_spec(a.shape) for a in tiles] + [_whole_spec(a.shape) for a in whole]
        + [_tile_spec(cos.shape), _tile_spec(sin.shape)],
        out_specs=[_tile_spec(s.shape) for s in out_shapes],
        out_shape=out_shapes,
        compiler_params=_params("parallel"),
    )(*tiles, *whole, cos, sin)


def _chunk_of(rows):
    return jnp.where(rows >= ROW_TILE, lax.shift_right_arithmetic(rows - ROW_TILE, CHUNK_SHIFT) + 1, 0)


def _visible(i, j):
    q_rows = i * ROW_TILE + lax.broadcasted_iota(jnp.int32, (ROW_TILE, 1), 0)
    k_rows = j * ROW_TILE + lax.broadcasted_iota(jnp.int32, (1, ROW_TILE), 1)
    return jnp.logical_and(_chunk_of(q_rows) >= _chunk_of(k_rows), k_rows >= DEAD)


def _attn_fwd(q, k, v, n_rows):
    nt = n_rows // ROW_TILE
    scale = QK_DIM ** -0.5

    def body(q_ref, k_ref, v_ref, o_ref, lse_ref):
        i = pl.program_id(1)
        q_t = q_ref[...]

        def step(j, carry):
            m, l, acc = carry
            rows = pl.ds(pl.multiple_of(j * ROW_TILE, ROW_TILE), ROW_TILE)
            s = _dot_nt(q_t, k_ref[rows, :]) * scale
            s = jnp.where(_visible(i, j), s, NEG)
            m_new = jnp.maximum(m, jnp.max(s, axis=-1, keepdims=True))
            alpha = jnp.exp(m - m_new)
            p = jnp.exp(s - m_new)
            l = alpha * l + jnp.sum(p, axis=-1, keepdims=True)
            acc = alpha * acc + _dot(p.astype(BF16), v_ref[rows, :])
            return m_new, l, acc

        init = (jnp.full((ROW_TILE, 1), NEG, F32), jnp.zeros((ROW_TILE, 1), F32), jnp.zeros((ROW_TILE, V_HEAD), F32))
        m, l, acc = lax.fori_loop(0, i + 1, step, init)
        o_ref[...] = acc / l
        lse_ref[...] = m + jnp.log(l)

    head_tile = lambda w: pl.BlockSpec((None, ROW_TILE, w), lambda h, i: (h, i, 0))
    head_all = lambda w: pl.BlockSpec((None, n_rows, w), lambda h, i: (h, 0, 0))
    out_shapes = [jax.ShapeDtypeStruct((N_HEADS, n_rows, V_HEAD), F32), jax.ShapeDtypeStruct((N_HEADS, n_rows, 1), F32)]
    return pl.pallas_call(
        body, name="attn_fwd", grid=(N_HEADS, nt),
        in_specs=[head_tile(QK_DIM), head_all(QK_DIM), head_all(V_HEAD)],
        out_specs=[head_tile(V_HEAD), head_tile(1)],
        out_shape=out_shapes,
        compiler_params=_params("parallel", "parallel"),
    )(q, k, v)


def _heads_to_lanes(ref):
    return jnp.concatenate([ref[h] for h in range(N_HEADS)], axis=-1)


def _fwd_out(x, meta_pad, mix_a, o, gb, w_out, n_rows):
    nt = n_rows // ROW_TILE

    def body(x_ref, meta_ref, mixa_ref, o_ref, gb_ref, w_ref, mix_ref, h1_ref):
        i = pl.program_id(0)
        h0 = jnp.where(i == 0, meta_ref[...], x_ref[...])
        mix_b = _rms_fwd(_heads_to_lanes(o_ref), gb_ref[...]).astype(BF16)
        mix = jnp.concatenate([mixa_ref[...], mix_b], axis=-1)
        mix_ref[...] = mix
        h1_ref[...] = h0 + _dot(mix, w_ref[...])

    out_shapes = [jax.ShapeDtypeStruct((n_rows, D_MODEL), BF16), jax.ShapeDtypeStruct((n_rows, D_MODEL), F32)]
    return pl.pallas_call(
        body, name="fwd_out", grid=(nt,),
        in_specs=[_real_spec(D_MODEL), _whole_spec(meta_pad.shape), _tile_spec(mix_a.shape), _tile_spec(o.shape),
                  _whole_spec(gb.shape), _whole_spec(w_out.shape)],
        out_specs=[_tile_spec(s.shape) for s in out_shapes],
        out_shape=out_shapes,
        compiler_params=_params("parallel"),
    )(x, meta_pad, mix_a, o, gb, w_out)


def _ffn_conv(ext_ref, w_ref, b_ref, s):
    first = FFN_HALO - (FFN_CONV_WIDTH - 1)
    acc = b_ref[s]
    for k in range(FFN_CONV_WIDTH):
        acc = acc + w_ref[s, k:k + 1, :] * ext_ref[s, first + k:first + k + ROW_TILE, :]
    return acc


def _fwd_ffn(h1, target, g2, w_up, fw, fb, w_down, gf, n_rows):
    nt = n_rows // ROW_TILE

    def body(h1_ref, t_ref, g2_ref, wup_ref, fw_ref, fb_ref, wdn_ref, gf_ref,
             n2_ref, up0_ref, act_ref, dh2_ref, loss_ref, dgf_ref, ext_ref):
        i = pl.program_id(0)

        @pl.when(i == 0)
        def _():
            ext_ref[:, 0:FFN_HALO, :] = jnp.zeros((N_DEV, FFN_HALO, UP_SLAB), F32)

        h1_t = h1_ref[...]
        n2 = _rms_fwd(h1_t, g2_ref[...]).astype(BF16)
        n2_ref[...] = n2
        live = _row_ids(i, ROW_TILE) >= DEAD
        for s in range(N_DEV):
            up0 = jnp.where(live, _dot(n2, wup_ref[s]), 0.0).astype(BF16)
            up0_ref[s] = up0
            ext_ref[s, FFN_HALO:, :] = up0.astype(F32)
        h2 = h1_t
        for s in range(N_ACT_SLAB):
            gate = _ffn_conv(ext_ref, fw_ref, fb_ref, s)
            val = _ffn_conv(ext_ref, fw_ref, fb_ref, s + N_ACT_SLAB)
            act = (gate * _sigmoid(gate) * val).astype(BF16)
            act_ref[s] = act
            h2 = h2 + _dot(act, wdn_ref[s])
        ext_ref[:, 0:FFN_HALO, :] = ext_ref[:, ROW_TILE:ROW_TILE + FFN_HALO, :]

        gf_t = gf_ref[...]
        y = _rms_fwd(h2, gf_t)
        diff = jnp.where(i >= 1, y - t_ref[...], 0.0)
        tile_loss = 0.5 * jnp.sum(jnp.sum(diff * diff, axis=-1, keepdims=True), axis=0, keepdims=True) / D_MODEL
        dh2, dgf = _rms_bwd(diff / D_MODEL, h2, gf_t)
        dh2_ref[...] = dh2
        _accumulate(loss_ref, i == 0, jnp.broadcast_to(tile_loss, loss_ref.shape))
        _accumulate(dgf_ref, i == 0, dgf)

    out_shapes = [
        jax.ShapeDtypeStruct((n_rows, D_MODEL), BF16),
        jax.ShapeDtypeStruct((N_DEV, n_rows, UP_SLAB), BF16),
        jax.ShapeDtypeStruct((N_ACT_SLAB, n_rows, UP_SLAB), BF16),
        jax.ShapeDtypeStruct((n_rows, D_MODEL), F32),
        jax.ShapeDtypeStruct((8, 128), F32),
        jax.ShapeDtypeStruct((1, D_MODEL), F32),
    ]
    whole = [g2, w_up, fw, fb, w_down, gf]
    return pl.pallas_call(
        body, name="fwd_ffn", grid=(nt,),
        in_specs=[_tile_spec(h1.shape), _real_spec(D_MODEL)] + [_whole_spec(a.shape) for a in whole],
        out_specs=[_tile_spec(s.shape) for s in out_shapes[:4]] + [_acc_spec(s.shape) for s in out_shapes[4:]],
        out_shape=out_shapes,
        scratch_shapes=[pltpu.VMEM((N_DEV, ROW_TILE + FFN_HALO, UP_SLAB), F32)],
        compiler_params=_params("arbitrary"),
    )(h1, target, *whole)


def _rope_tables(n_rows):
    pos = jnp.maximum(jnp.arange(n_rows, dtype=jnp.int32) - DEAD, 0)
    inv_freq = 1.0 / (ROPE_THETA ** (jnp.arange(0, QK_ROPE, 2, dtype=F32) / QK_ROPE))
    ang = pos.astype(F32)[:, None] * inv_freq[None, :]
    return jnp.cos(ang), jnp.sin(ang)


def _halo_after(shape, halo, n_rows):
    last = n_rows // halo - 1
    step = ROW_TILE // halo
    if len(shape) == 2:
        return pl.BlockSpec((halo, shape[1]), lambda i: (jnp.minimum((i + 1) * step, last), 0))
    return pl.BlockSpec((shape[0], halo, shape[2]), lambda i: (0, jnp.minimum((i + 1) * step, last), 0))


def _halo_before(shape, halo):
    step = ROW_TILE // halo
    if len(shape) == 2:
        return pl.BlockSpec((halo, shape[1]), lambda i: (jnp.maximum(i * step - 1, 0), 0))
    return pl.BlockSpec((shape[0], halo, shape[2]), lambda i: (0, jnp.maximum(i * step - 1, 0), 0))


def _bwd_ffn_act(dh2, up0, w_down, fw, fb, n_rows):
    nt = n_rows // ROW_TILE

    def body(dh2_ref, up0_ref, wdn_ref, fw_ref, fb_ref, dup_ref, dfb_ref, ext_ref):
        i = pl.program_id(0)

        @pl.when(i == 0)
        def _():
            ext_ref[:, 0:FFN_HALO, :] = jnp.zeros((N_DEV, FFN_HALO, UP_SLAB), F32)
            dfb_ref[...] = jnp.zeros_like(dfb_ref)

        for s in range(N_DEV):
            ext_ref[s, FFN_HALO:, :] = up0_ref[s].astype(F32)
        dh2_b = dh2_ref[...].astype(BF16)
        for s in range(N_ACT_SLAB):
            gate = _ffn_conv(ext_ref, fw_ref, fb_ref, s)
            val = _ffn_conv(ext_ref, fw_ref, fb_ref, s + N_ACT_SLAB)
            d_act = _dot_nt(dh2_b, wdn_ref[s])
            sg = _sigmoid(gate)
            d_gate = d_act * val * sg * (1.0 + gate * (1.0 - sg))
            d_val = d_act * gate * sg
            dup_ref[s] = d_gate.astype(BF16)
            dup_ref[s + N_ACT_SLAB] = d_val.astype(BF16)
            dfb_ref[s] += jnp.sum(d_gate, axis=0, keepdims=True)
            dfb_ref[s + N_ACT_SLAB] += jnp.sum(d_val, axis=0, keepdims=True)
        ext_ref[:, 0:FFN_HALO, :] = ext_ref[:, ROW_TILE:ROW_TILE + FFN_HALO, :]

    out_shapes = [jax.ShapeDtypeStruct((N_DEV, n_rows, UP_SLAB), BF16), jax.ShapeDtypeStruct((N_DEV, 1, UP_SLAB), F32)]
    whole = [w_down, fw, fb]
    return pl.pallas_call(
        body, name="bwd_ffn_act", grid=(nt,),
        in_specs=[_tile_spec(dh2.shape), _tile_spec(up0.shape)] + [_whole_spec(a.shape) for a in whole],
        out_specs=[_tile_spec(out_shapes[0].shape), _acc_spec(out_shapes[1].shape)],
        out_shape=out_shapes,
        scratch_shapes=[pltpu.VMEM((N_DEV, ROW_TILE + FFN_HALO, UP_SLAB), F32)],
        compiler_params=_params("arbitrary"),
    )(dh2, up0, *whole)


def _bwd_ffn_up(dup, up0, h1, dh2, g2, w_up, fw, n_rows):
    nt = n_rows // ROW_TILE

    def body(dup_ref, dnext_ref, up0_ref, uprev_ref, h1_ref, dh2_ref, g2_ref, wup_ref, fw_ref,
             dup0_ref, dh1_ref, dfw_ref, dg2_ref, dext_ref, uext_ref):
        i = pl.program_id(0)

        @pl.when(i == 0)
        def _():
            dfw_ref[...] = jnp.zeros_like(dfw_ref)

        live = _row_ids(i, ROW_TILE) >= DEAD
        dn2 = jnp.zeros((ROW_TILE, D_MODEL), F32)
        for s in range(N_DEV):
            d = dup_ref[s].astype(F32)
            dext_ref[0:ROW_TILE, :] = d
            dext_ref[ROW_TILE:, :] = jnp.where(i == nt - 1, 0.0, dnext_ref[s].astype(F32))
            uext_ref[0:FFN_HALO, :] = jnp.where(i == 0, 0.0, uprev_ref[s].astype(F32))
            uext_ref[FFN_HALO:, :] = up0_ref[s].astype(F32)
            dup0 = jnp.zeros((ROW_TILE, UP_SLAB), F32)
            for k in range(FFN_CONV_WIDTH):
                back = FFN_CONV_WIDTH - 1 - k
                dup0 = dup0 + fw_ref[s, k:k + 1, :] * dext_ref[back:back + ROW_TILE, :]
                first = FFN_HALO - back
                dfw_ref[s, k:k + 1, :] += jnp.sum(d * uext_ref[first:first + ROW_TILE, :], axis=0, keepdims=True)
            dup0_b = jnp.where(live, dup0, 0.0).astype(BF16)
            dup0_ref[s] = dup0_b
            dn2 = dn2 + _dot_nt(dup0_b, wup_ref[s])
        dx, dg2 = _rms_bwd(dn2, h1_ref[...], g2_ref[...])
        dh1_ref[...] = dh2_ref[...] + dx
        _accumulate(dg2_ref, i == 0, dg2)

    out_shapes = [
        jax.ShapeDtypeStruct((N_DEV, n_rows, UP_SLAB), BF16),
        jax.ShapeDtypeStruct((n_rows, D_MODEL), F32),
        jax.ShapeDtypeStruct((N_DEV, FFN_CONV_WIDTH, UP_SLAB), F32),
        jax.ShapeDtypeStruct((1, D_MODEL), F32),
    ]
    return pl.pallas_call(
        body, name="bwd_ffn_up", grid=(nt,),
        in_specs=[_tile_spec(dup.shape), _halo_after(dup.shape, FFN_HALO, n_rows), _tile_spec(up0.shape),
                  _halo_before(up0.shape, FFN_HALO), _tile_spec(h1.shape), _tile_spec(dh2.shape),
                  _whole_spec(g2.shape), _whole_spec(w_up.shape), _whole_spec(fw.shape)],
        out_specs=[_tile_spec(s.shape) for s in out_shapes[:2]] + [_acc_spec(s.shape) for s in out_shapes[2:]],
        out_shape=out_shapes,
        scratch_shapes=[pltpu.VMEM((ROW_TILE + FFN_HALO, UP_SLAB), F32), pltpu.VMEM((ROW_TILE + FFN_HALO, UP_SLAB), F32)],
        compiler_params=_params("arbitrary"),
    )(dup, dup, up0, up0, h1, dh2, g2, w_up, fw)


def _bwd_out(dh1, o, u1, w_out, gb, ln_g, ln_b, ga, n_rows):
    nt = n_rows // ROW_TILE

    def body(dh1_ref, o_ref, u1_ref, w_ref, gb_ref, lg_ref, lb_ref, ga_ref,
             do_ref, delta_ref, du1_ref, dgb_ref, dga_ref, dlg_ref, dlb_ref, dcb_ref):
        i = pl.program_id(0)
        dmix = _dot_nt(dh1_ref[...].astype(BF16), w_ref[...])
        do, dgb = _rms_bwd(dmix[:, D_CONV:], _heads_to_lanes(o_ref), gb_ref[...])
        for h in range(N_HEADS):
            do_h = do[:, h * V_HEAD:(h + 1) * V_HEAD]
            do_ref[h] = do_h.astype(BF16)
            delta_ref[h] = jnp.sum(do_h * o_ref[h], axis=-1, keepdims=True)
        lg = lg_ref[...]
        xh, u2, u3, rstd = _conv_chain(u1_ref[...], lg, lb_ref[...])
        du3, dga = _rms_bwd(dmix[:, :D_CONV], u3, ga_ref[...])
        sg = _sigmoid(u2)
        du2 = du3 * sg * (1.0 + u2 * (1.0 - sg))
        dxh = du2 * lg
        du1 = rstd * (dxh - jnp.mean(dxh, axis=-1, keepdims=True) - xh * jnp.mean(dxh * xh, axis=-1, keepdims=True))
        du1_ref[...] = du1
        first = i == 0
        _accumulate(dgb_ref, first, dgb)
        _accumulate(dga_ref, first, dga)
        _accumulate(dlg_ref, first, jnp.sum(du2 * xh, axis=0, keepdims=True))
        _accumulate(dlb_ref, first, jnp.sum(du2, axis=0, keepdims=True))
        _accumulate(dcb_ref, first, jnp.sum(du1, axis=0, keepdims=True))

    out_shapes = [
        jax.ShapeDtypeStruct((N_HEADS, n_rows, V_HEAD), BF16),
        jax.ShapeDtypeStruct((N_HEADS, n_rows, 1), F32),
        jax.ShapeDtypeStruct((n_rows, D_CONV), F32),
    ] + [jax.ShapeDtypeStruct((1, D_CONV), F32)] * 5
    whole = [w_out, gb, ln_g, ln_b, ga]
    return pl.pallas_call(
        body, name="bwd_out", grid=(nt,),
        in_specs=[_tile_spec(dh1.shape), _tile_spec(o.shape), _tile_spec(u1.shape)] + [_whole_spec(a.shape) for a in whole],
        out_specs=[_tile_spec(s.shape) for s in out_shapes[:3]] + [_acc_spec(s.shape) for s in out_shapes[3:]],
        out_shape=out_shapes,
        compiler_params=_params("arbitrary"),
    )(dh1, o, u1, *whole)


def _attn_bwd(q, k, v, do, lse, delta, n_rows):
    nt = n_rows // ROW_TILE
    scale = QK_DIM ** -0.5

    def body(k_ref, v_ref, q_ref, do_ref, lse_ref, delta_ref, dq_ref, dk_ref, dv_ref):
        j = pl.program_id(1)

        @pl.when(j == 0)
        def _():
            dq_ref[...] = jnp.zeros_like(dq_ref)

        k_t, v_t = k_ref[...], v_ref[...]

        def step(i, carry):
            dk, dv = carry
            rows = pl.ds(pl.multiple_of(i * ROW_TILE, ROW_TILE), ROW_TILE)
            q_i, do_i = q_ref[rows, :], do_ref[rows, :]
            s = _dot_nt(q_i, k_t) * scale
            s = jnp.where(_visible(i, j), s, NEG)
            p = jnp.exp(s - lse_ref[rows, :])
            dp = _dot_nt(do_i, v_t)
            ds = (p * (dp - delta_ref[rows, :]) * scale).astype(BF16)
            dv = dv + _dot_tn(p.astype(BF16), do_i)
            dk = dk + _dot_tn(ds, q_i)
            dq_ref[rows, :] += _dot(ds, k_t)
            return dk, dv

        dk, dv = lax.fori_loop(j, nt, step, (jnp.zeros((ROW_TILE, QK_DIM), F32), jnp.zeros((ROW_TILE, V_HEAD), F32)))
        dk_ref[...] = dk
        dv_ref[...] = dv

    head_tile = lambda w: pl.BlockSpec((None, ROW_TILE, w), lambda h, j: (h, j, 0))
    head_all = lambda w: pl.BlockSpec((None, n_rows, w), lambda h, j: (h, 0, 0))
    out_shapes = [
        jax.ShapeDtypeStruct((N_HEADS, n_rows, QK_DIM), F32),
        jax.ShapeDtypeStruct((N_HEADS, n_rows, QK_DIM), F32),
        jax.ShapeDtypeStruct((N_HEADS, n_rows, V_HEAD), F32),
    ]
    return pl.pallas_call(
        body, name="attn_bwd", grid=(N_HEADS, nt),
        in_specs=[head_tile(QK_DIM), head_tile(V_HEAD), head_all(QK_DIM), head_all(V_HEAD), head_all(1), head_all(1)],
        out_specs=[head_all(QK_DIM), head_tile(QK_DIM), head_tile(V_HEAD)],
        out_shape=out_shapes,
        compiler_params=_params("parallel", "arbitrary"),
    )(k, v, q, do, lse, delta)


def _bwd_qkv(dq, dk, dv, cq, ckv, gq, gkv, w_uq, w_ukv, cos, sin, n_rows):
    nt = n_rows // ROW_TILE

    def body(dq_ref, dk_ref, dv_ref, cq_ref, ckv_ref, gq_ref, gkv_ref, wq_ref, wkv_ref, cos_ref, sin_ref,
             dqraw_ref, dkv_ref, dcq_ref, dckv_ref, dkr_ref, dgq_ref, dgkv_ref):
        i = pl.program_id(0)
        cos_t, sin_t = cos_ref[...], sin_ref[...]
        dcqn = jnp.zeros((ROW_TILE, Q_LORA), F32)
        dckvn = jnp.zeros((ROW_TILE, KV_LORA), F32)
        dk_rot = jnp.zeros((ROW_TILE, QK_ROPE), F32)
        for h in range(N_HEADS):
            dq_h, dk_h = dq_ref[h], dk_ref[h]
            dq_raw = jnp.concatenate([dq_h[:, :QK_NOPE], _rope_t(dq_h[:, QK_NOPE:], cos_t, sin_t)], axis=-1).astype(BF16)
            dqraw_ref[h] = dq_raw
            dcqn = dcqn + _dot_nt(dq_raw, wq_ref[h])
            dkv = jnp.concatenate([dk_h[:, :QK_NOPE], dv_ref[h]], axis=-1).astype(BF16)
            dkv_ref[h] = dkv
            dckvn = dckvn + _dot_nt(dkv, wkv_ref[h])
            dk_rot = dk_rot + dk_h[:, QK_NOPE:]
        dkr_ref[...] = _rope_t(dk_rot, cos_t, sin_t).astype(BF16)
        dcq, dgq = _rms_bwd(dcqn, cq_ref[...], gq_ref[...])
        dckv, dgkv = _rms_bwd(dckvn, ckv_ref[...], gkv_ref[...])
        dcq_ref[...] = dcq.astype(BF16)
        dckv_ref[...] = dckv.astype(BF16)
        _accumulate(dgq_ref, i == 0, dgq)
        _accumulate(dgkv_ref, i == 0, dgkv)

    out_shapes = [
        jax.ShapeDtypeStruct((N_HEADS, n_rows, QK_DIM), BF16),
        jax.ShapeDtypeStruct((N_HEADS, n_rows, KV_HEAD), BF16),
        jax.ShapeDtypeStruct((n_rows, Q_LORA), BF16),
        jax.ShapeDtypeStruct((n_rows, KV_LORA), BF16),
        jax.ShapeDtypeStruct((n_rows, QK_ROPE), BF16),
        jax.ShapeDtypeStruct((1, Q_LORA), F32),
        jax.ShapeDtypeStruct((1, KV_LORA), F32),
    ]
    tiles = [dq, dk, dv, cq, ckv]
    whole = [gq, gkv, w_uq, w_ukv]
    return pl.pallas_call(
        body, name="bwd_qkv", grid=(nt,),
        in_specs=[_tile_spec(a.shape) for a in tiles] + [_whole_spec(a.shape) for a in whole]
        + [_tile_spec(cos.shape), _tile_spec(sin.shape)],
        out_specs=[_tile_spec(s.shape) for s in out_shapes[:5]] + [_acc_spec(s.shape) for s in out_shapes[5:]],
        out_shape=out_shapes,
        compiler_params=_params("arbitrary"),
    )(*tiles, *whole, cos, sin)


def _bwd_conv(du1, ag, conv_w, n_rows):
    nt = n_rows // ROW_TILE

    def glu(ag_t, rows):
        sg = _sigmoid(ag_t[:, D_CONV:])
        return jnp.where(rows >= DEAD, ag_t[:, :D_CONV] * sg, 0.0), sg

    def body(du1_ref, dnext_ref, ag_ref, agprev_ref, w_ref, dag_ref, dw_ref, dext_ref, uext_ref):
        i = pl.program_id(0)

        @pl.when(i == 0)
        def _():
            dw_ref[...] = jnp.zeros_like(dw_ref)

        du1_t = du1_ref[...]
        dext_ref[0:ROW_TILE, :] = du1_t
        dext_ref[ROW_TILE:, :] = jnp.where(i == nt - 1, 0.0, dnext_ref[...])
        ag_t = ag_ref[...]
        rows = _row_ids(i, ROW_TILE)
        u0, sg = glu(ag_t, rows)
        prev_rows = i * ROW_TILE - CONV_HALO + lax.broadcasted_iota(jnp.int32, (CONV_HALO, 1), 0)
        u0_prev, _ = glu(agprev_ref[...], prev_rows)
        uext_ref[0:CONV_HALO, :] = jnp.where(i == 0, 0.0, u0_prev)
        uext_ref[CONV_HALO:, :] = u0
        du0 = jnp.zeros((ROW_TILE, D_CONV), F32)
        for k in range(CONV_WIDTH):
            back = CONV_WIDTH - 1 - k
            du0 = du0 + w_ref[k:k + 1, :] * dext_ref[back:back + ROW_TILE, :]
            first = CONV_HALO - back
            dw_ref[k:k + 1, :] += jnp.sum(du1_t * uext_ref[first:first + ROW_TILE, :], axis=0, keepdims=True)
        du0 = jnp.where(rows >= DEAD, du0, 0.0)
        da = du0 * sg
        dgate = du0 * ag_t[:, :D_CONV] * sg * (1.0 - sg)
        dag_ref[...] = jnp.concatenate([da, dgate], axis=-1).astype(BF16)

    out_shapes = [jax.ShapeDtypeStruct((n_rows, 2 * D_CONV), BF16), jax.ShapeDtypeStruct((CONV_WIDTH, D_CONV), F32)]
    return pl.pallas_call(
        body, name="bwd_conv", grid=(nt,),
        in_specs=[_tile_spec(du1.shape), _halo_after(du1.shape, CONV_HALO, n_rows), _tile_spec(ag.shape),
                  _halo_before(ag.shape, CONV_HALO), _whole_spec(conv_w.shape)],
        out_specs=[_tile_spec(out_shapes[0].shape), _acc_spec(out_shapes[1].shape)],
        out_shape=out_shapes,
        scratch_shapes=[pltpu.VMEM((ROW_TILE + CONV_HALO, D_CONV), F32), pltpu.VMEM((ROW_TILE + CONV_HALO, D_CONV), F32)],
        compiler_params=_params("arbitrary"),
    )(du1, du1, ag, ag, conv_w)


def _bwd_in(dag, dcq, dckv, dkr, x, meta_pad, dh1, g1, w_in, n_rows):
    nt = n_rows // ROW_TILE

    def body(dag_ref, dcq_ref, dckv_ref, dkr_ref, x_ref, meta_ref, dh1_ref, g_ref, w_ref,
             dz_ref, gx_ref, gmeta_ref, dg1_ref):
        i = pl.program_id(0)
        dz = jnp.concatenate([dag_ref[...], dcq_ref[...], dckv_ref[...], dkr_ref[...]], axis=-1)
        dz_ref[...] = dz
        h0 = jnp.where(i == 0, meta_ref[...], x_ref[...])
        dx, dg1 = _rms_bwd(_dot_nt(dz, w_ref[...]), h0, g_ref[...])
        dh0 = dh1_ref[...] + dx
        gx_ref[...] = dh0

        @pl.when(i == 0)
        def _():
            gmeta_ref[...] = dh0

        _accumulate(dg1_ref, i == 0, dg1)

    out_shapes = [
        jax.ShapeDtypeStruct((n_rows, D_IN), BF16),
        jax.ShapeDtypeStruct((n_rows - ROW_TILE, D_MODEL), F32),
        jax.ShapeDtypeStruct((ROW_TILE, D_MODEL), F32),
        jax.ShapeDtypeStruct((1, D_MODEL), F32),
    ]
    tiles = [dag, dcq, dckv, dkr]
    return pl.pallas_call(
        body, name="bwd_in", grid=(nt,),
        in_specs=[_tile_spec(a.shape) for a in tiles]
        + [_real_spec(D_MODEL), _whole_spec(meta_pad.shape), _tile_spec(dh1.shape), _whole_spec(g1.shape), _whole_spec(w_in.shape)],
        out_specs=[_tile_spec(out_shapes[0].shape), _real_spec(D_MODEL), _acc_spec(out_shapes[2].shape), _acc_spec(out_shapes[3].shape)],
        out_shape=out_shapes,
        compiler_params=_params("arbitrary"),
    )(*tiles, x, meta_pad, dh1, g1, w_in)


def _xty(a, b, name):
    groups = max(a.shape[0] if a.ndim == 3 else 1, b.shape[0] if b.ndim == 3 else 1)
    n_rows, m, n = a.shape[-2], a.shape[-1], b.shape[-1]
    nt = n_rows // ROW_TILE

    def body(a_ref, b_ref, out_ref, acc_ref):
        i = pl.program_id(1)
        part = _dot_tn(a_ref[...].astype(BF16), b_ref[...].astype(BF16))
        _accumulate(acc_ref, i == 0, part)

        @pl.when(i == nt - 1)
        def _():
            out_ref[...] = acc_ref[...].astype(out_ref.dtype)

    def spec(arr):
        if arr.ndim == 3:
            return pl.BlockSpec((None, ROW_TILE, arr.shape[-1]), lambda g, i: (g, i, 0))
        return pl.BlockSpec((ROW_TILE, arr.shape[-1]), lambda g, i: (i, 0))

    return pl.pallas_call(
        body, name=name, grid=(groups, nt),
        in_specs=[spec(a), spec(b)],
        out_specs=pl.BlockSpec((None, m, n), lambda g, i: (g, 0, 0)),
        out_shape=jax.ShapeDtypeStruct((groups, m, n), BF16),
        scratch_shapes=[pltpu.VMEM((m, n), F32)],
        compiler_params=_params("parallel", "arbitrary"),
    )(a, b)


def _my_index():
    return 4 * lax.axis_index("x") + 2 * lax.axis_index("y") + lax.axis_index("c")


def _peer(k):
    flip = lambda v, bit: 1 - v if bit else v
    px = flip(lax.axis_index("x"), k & 4)
    py = flip(lax.axis_index("y"), k & 2)
    pc = flip(lax.axis_index("c"), k & 1)
    return (px, py, pc), 4 * px + 2 * py + pc


def _all_gather(shards, dtypes):
    n = len(shards)

    def body(*refs):
        ins, outs, stages = refs[:n], refs[n:2 * n], refs[2 * n:3 * n]
        send_sems, recv_sems, local_sems = refs[3 * n:]
        me = _my_index()
        for a in range(n):
            stages[a][...] = ins[a][...].astype(stages[a].dtype)
        local = [pltpu.make_async_copy(stages[a], outs[a].at[me], local_sems.at[a]) for a in range(n)]
        for cp in local:
            cp.start()

        def copy(a, k, slot):
            peer, _ = _peer(k)
            return pltpu.make_async_remote_copy(
                src_ref=stages[a], dst_ref=outs[a].at[slot], send_sem=send_sems.at[a, k - 1],
                recv_sem=recv_sems.at[a, k - 1], device_id=peer, device_id_type=MESH)

        for k in range(1, N_DEV):
            for a in range(n):
                copy(a, k, me).start()
        for k in range(1, N_DEV):
            for a in range(n):
                copy(a, k, _peer(k)[1]).wait()
        for cp in local:
            cp.wait()

    return pl.pallas_call(
        body, name="gather_weights",
        in_specs=[pl.BlockSpec(memory_space=pltpu.VMEM)] * n,
        out_specs=[pl.BlockSpec(memory_space=pl.ANY)] * n,
        out_shape=[jax.ShapeDtypeStruct((N_DEV,) + s.shape, dt) for s, dt in zip(shards, dtypes)],
        scratch_shapes=[pltpu.VMEM(s.shape, dt) for s, dt in zip(shards, dtypes)]
        + [pltpu.SemaphoreType.DMA((n, N_DEV - 1)), pltpu.SemaphoreType.DMA((n, N_DEV - 1)), pltpu.SemaphoreType.DMA((n,))],
        compiler_params=pltpu.CompilerParams(vmem_limit_bytes=VMEM_LIMIT),
    )(*shards)


def _exchange(parts, whole):
    n = len(parts)

    def body(*refs):
        ins, outs = refs[:n], refs[n:2 * n]
        send_sems, recv_sems, local_sems = refs[2 * n:]
        me = _my_index()

        def src(a, slab):
            return ins[a] if whole[a] else ins[a].at[slab]

        local = [pltpu.make_async_copy(src(a, me), outs[a].at[me], local_sems.at[a]) for a in range(n)]
        for cp in local:
            cp.start()

        def copy(a, k, slab, slot):
            peer, _ = _peer(k)
            return pltpu.make_async_remote_copy(
                src_ref=src(a, slab), dst_ref=outs[a].at[slot], send_sem=send_sems.at[a, k - 1],
                recv_sem=recv_sems.at[a, k - 1], device_id=peer, device_id_type=MESH)

        for k in range(1, N_DEV):
            for a in range(n):
                copy(a, k, _peer(k)[1], me).start()
        for k in range(1, N_DEV):
            for a in range(n):
                copy(a, k, _peer(k)[1], _peer(k)[1]).wait()
        for cp in local:
            cp.wait()

    return pl.pallas_call(
        body, name="exchange_grads",
        in_specs=[pl.BlockSpec(memory_space=pl.ANY)] * n,
        out_specs=[pl.BlockSpec(memory_space=pl.ANY)] * n,
        out_shape=[jax.ShapeDtypeStruct(((N_DEV,) + p.shape) if w else p.shape, p.dtype) for p, w in zip(parts, whole)],
        scratch_shapes=[pltpu.SemaphoreType.DMA((n, N_DEV - 1)), pltpu.SemaphoreType.DMA((n, N_DEV - 1)),
                        pltpu.SemaphoreType.DMA((n,))],
    )(*parts)


def _row_block(rows):
    if rows <= ROW_TILE:
        return rows
    return next(rb for rb in range(ROW_TILE, 0, -16) if rows % rb == 0)


def _adamw(landing, w, m, v, name):
    rows, cols = w.shape
    rb = _row_block(rows)

    def body(l_ref, w_ref, m_ref, v_ref, g_ref, d_ref, m2_ref, v2_ref):
        g = l_ref[0].astype(F32)
        for p in range(1, N_DEV):
            g = g + l_ref[p].astype(F32)
        m2 = ADAM_B1 * m_ref[...] + (1.0 - ADAM_B1) * g
        v2 = ADAM_B2 * v_ref[...] + (1.0 - ADAM_B2) * (g * g)
        m_hat = m2 / (1.0 - ADAM_B1 ** ADAM_STEP)
        v_hat = v2 / (1.0 - ADAM_B2 ** ADAM_STEP)
        g_ref[...] = g
        d_ref[...] = -ADAM_LR * (m_hat / (jnp.sqrt(v_hat) + ADAM_EPS) + ADAM_WD * w_ref[...])
        m2_ref[...] = m2
        v2_ref[...] = v2

    flat = pl.BlockSpec((rb, cols), lambda i: (i, 0))
    return pl.pallas_call(
        body, name=name, grid=(rows // rb,),
        in_specs=[pl.BlockSpec((N_DEV, rb, cols), lambda i: (0, i, 0)), flat, flat, flat],
        out_specs=[flat] * 4,
        out_shape=[jax.ShapeDtypeStruct((rows, cols), F32)] * 4,
        compiler_params=_params("parallel"),
    )(landing, w, m, v)


_REPLICATED = (
    ("mix_norm_g", D_MODEL), ("q_norm_g", Q_LORA), ("kv_norm_g", KV_LORA), ("conv_b", D_CONV), ("conv_ln_g", D_CONV),
    ("conv_ln_b", D_CONV), ("conv_out_g", D_CONV), ("attn_out_g", D_CONV), ("ffn_norm_g", D_MODEL),
    ("ffn_conv_b", D_UP), ("final_norm_g", D_MODEL),
)
_LANES = 128
_PACK_ROWS = 104

_WEIGHT_ORDER = (
    "meta_tokens", "mix_norm_g", "w_in", "q_norm_g", "w_uq", "kv_norm_g", "w_ukv", "conv_w", "conv_b", "conv_ln_g",
    "conv_ln_b", "conv_out_g", "attn_out_g", "w_out", "ffn_norm_g", "w_ffn_up", "ffn_conv_w", "ffn_conv_b",
    "w_ffn_down", "final_norm_g",
)


def _pack(vectors):
    flat = jnp.concatenate([vectors[name].reshape(-1) for name, _ in _REPLICATED])
    return jnp.pad(flat, (0, _PACK_ROWS * _LANES - flat.shape[0])).reshape(_PACK_ROWS, _LANES)


def _unpack(packed, like):
    flat, out, at = packed.reshape(-1), {}, 0
    for name, size in _REPLICATED:
        out[name] = flat[at:at + size].reshape(like[name].shape)
        at += size
    return out


def _pad_rows(a, rows):
    return jnp.pad(a, ((0, rows - a.shape[0]), (0, 0)))


def _slabs(a):
    r, c = a.shape
    return a.reshape(r, N_DEV, c // N_DEV).transpose(1, 0, 2)


def _unslab(a):
    g, r, c = a.shape
    return a.transpose(1, 0, 2).reshape(r, g * c)


def _local_step(x, target, w, n_rows):
    cos, sin = _rope_tables(n_rows)
    meta_pad, g1, gf = w["meta_pad"], w["mix_norm_g"], w["final_norm_g"]
    nb, ag, cq, ckv, kr = _fwd_in(x, meta_pad, g1, w["w_in"], n_rows)
    mix_a, u1 = _fwd_conv(ag, w["conv_w"], w["conv_b"], w["conv_ln_g"], w["conv_ln_b"], w["conv_out_g"], n_rows)
    q, k, v, cqn, ckvn = _fwd_qkv(cq, ckv, kr, w["q_norm_g"], w["kv_norm_g"], w["w_uq"], w["w_ukv"], cos, sin, n_rows)
    o, lse = _attn_fwd(q, k, v, n_rows)
    mix, h1 = _fwd_out(x, meta_pad, mix_a, o, w["attn_out_g"], w["w_out"], n_rows)
    n2, up0, act, dh2, loss, dgf = _fwd_ffn(h1, target, w["ffn_norm_g"], w["w_up"], w["fw"], w["fb"], w["w_down"], gf, n_rows)

    dup, dfb = _bwd_ffn_act(dh2, up0, w["w_down"], w["fw"], w["fb"], n_rows)
    dup0, dh1, dfw, dg2 = _bwd_ffn_up(dup, up0, h1, dh2, w["ffn_norm_g"], w["w_up"], w["fw"], n_rows)
    do, delta, du1, dgb, dga, dlg, dlb, dcb = _bwd_out(
        dh1, o, u1, w["w_out"], w["attn_out_g"], w["conv_ln_g"], w["conv_ln_b"], w["conv_out_g"], n_rows)
    dq, dk, dv = _attn_bwd(q, k, v, do, lse, delta, n_rows)
    dqraw, dkv, dcq, dckv, dkr, dgq, dgkv = _bwd_qkv(
        dq, dk, dv, cq, ckv, w["q_norm_g"], w["kv_norm_g"], w["w_uq"], w["w_ukv"], cos, sin, n_rows)
    dag, dcw = _bwd_conv(du1, ag, w["conv_w"], n_rows)
    dz, gx, gmeta, dg1 = _bwd_in(dag, dcq, dckv, dkr, x, meta_pad, dh1, g1, w["w_in"], n_rows)

    sharded = {
        "w_in": _slabs(_xty(nb, dz, "grad_w_in")[0]),
        "w_uq": _xty(cqn, dqraw, "grad_w_uq"),
        "w_ukv": _xty(ckvn, dkv, "grad_w_ukv"),
        "w_out": _xty(mix, dh1, "grad_w_out")[0].reshape(N_DEV, D_MODEL // N_DEV, D_MODEL),
        "w_ffn_up": _xty(n2, dup0, "grad_w_ffn_up"),
        "w_ffn_down": _xty(act, dh2, "grad_w_ffn_down").reshape(N_DEV, D_FF // N_DEV, D_MODEL),
        "conv_w": _slabs(dcw),
        "ffn_conv_w": dfw,
        "meta_tokens": _slabs(gmeta[DEAD:]),
    }
    replicated = {
        "mix_norm_g": dg1, "q_norm_g": dgq, "kv_norm_g": dgkv, "conv_b": dcb, "conv_ln_g": dlg, "conv_ln_b": dlb,
        "conv_out_g": dga, "attn_out_g": dgb, "ffn_norm_g": dg2, "ffn_conv_b": dfb, "final_norm_g": dgf,
    }
    return loss[0, 0], gx, sharded, replicated


_SHARDED = (
    ("w_in", None, BF16), ("w_uq", None, BF16), ("w_ukv", None, BF16), ("w_out", None, BF16), ("w_ffn_up", None, BF16),
    ("w_ffn_down", None, BF16), ("conv_w", 32, F32), ("ffn_conv_w", 8, F32), ("meta_tokens", None, F32),
)


def kernel(x, meta_tokens, mix_norm_g, w_in, q_norm_g, w_uq, kv_norm_g, w_ukv, conv_w, conv_b, conv_ln_g, conv_ln_b, conv_out_g, attn_out_g, w_out, ffn_norm_g, w_ffn_up, ffn_conv_w, ffn_conv_b, w_ffn_down, final_norm_g, loss_target, m_meta_tokens, m_mix_norm_g, m_w_in, m_q_norm_g, m_w_uq, m_kv_norm_g, m_w_ukv, m_conv_w, m_conv_b, m_conv_ln_g, m_conv_ln_b, m_conv_out_g, m_attn_out_g, m_w_out, m_ffn_norm_g, m_w_ffn_up, m_ffn_conv_w, m_ffn_conv_b, m_w_ffn_down, m_final_norm_g, v_meta_tokens, v_mix_norm_g, v_w_in, v_q_norm_g, v_w_uq, v_kv_norm_g, v_w_ukv, v_conv_w, v_conv_b, v_conv_ln_g, v_conv_ln_b, v_conv_out_g, v_attn_out_g, v_w_out, v_ffn_norm_g, v_w_ffn_up, v_ffn_conv_w, v_ffn_conv_b, v_w_ffn_down, v_final_norm_g):
    given = dict(locals())
    weights = {name: given[name] for name in _WEIGHT_ORDER}
    moments_m = {name: given["m_" + name] for name in _WEIGHT_ORDER}
    moments_v = {name: given["v_" + name] for name in _WEIGHT_ORDER}
    seq = x.shape[1]
    n_rows = ROW_TILE + seq

    def shard2d(a):
        return a.reshape(a.shape[-2], a.shape[-1])

    shards = []
    for name, pad_to, _ in _SHARDED:
        s = shard2d(weights[name])
        shards.append(s if pad_to is None else _pad_rows(s, pad_to))
    gathered = dict(zip([name for name, _, _ in _SHARDED], _all_gather(shards, [dt for _, _, dt in _SHARDED])))
    meta_full = _unslab(gathered["meta_tokens"])
    full = {
        "meta_pad": jnp.concatenate([jnp.zeros((DEAD, D_MODEL), F32), meta_full], axis=0),
        "w_in": _unslab(gathered["w_in"]),
        "w_uq": gathered["w_uq"],
        "w_ukv": gathered["w_ukv"],
        "w_out": gathered["w_out"].reshape(D_MODEL, D_MODEL),
        "w_up": gathered["w_ffn_up"],
        "w_down": gathered["w_ffn_down"].reshape(N_ACT_SLAB, UP_SLAB, D_MODEL),
        "conv_w": _unslab(gathered["conv_w"][:, :CONV_WIDTH]),
        "fw": gathered["ffn_conv_w"][:, :FFN_CONV_WIDTH],
        "fb": ffn_conv_b.reshape(N_DEV, 1, UP_SLAB),
        "final_norm_g": final_norm_g.reshape(1, D_MODEL),
    }
    for name in ("mix_norm_g", "q_norm_g", "kv_norm_g", "conv_b", "conv_ln_g", "conv_ln_b", "conv_out_g", "attn_out_g",
                 "ffn_norm_g"):
        full[name] = weights[name]

    loss, gx, sharded, replicated = _local_step(x[0], loss_target[0], full, n_rows)
    loss = lax.psum(loss, ("x", "y", "c"))

    parts, whole = [], []
    for name, pad_to, dt in _SHARDED:
        p = sharded[name].astype(dt)
        parts.append(p if pad_to is None else jnp.pad(p, ((0, 0), (0, pad_to - p.shape[1]), (0, 0))))
        whole.append(False)
    parts.append(_pack(replicated))
    whole.append(True)
    landed = _exchange(parts, whole)

    grad, delta, new_m, new_v = {}, {}, {}, {}
    for (name, pad_to, _), land in zip(_SHARDED, landed[:-1]):
        ws, ms, vs = shard2d(weights[name]), shard2d(moments_m[name]), shard2d(moments_v[name])
        rows = ws.shape[0]
        if pad_to is not None:
            ws, ms, vs = _pad_rows(ws, pad_to), _pad_rows(ms, pad_to), _pad_rows(vs, pad_to)
        outs = _adamw(land, ws, ms, vs, "adamw_" + name)
        shape = weights[name].shape
        grad[name], delta[name], new_m[name], new_v[name] = (o[:rows].reshape(shape) for o in outs)
    outs = _adamw(landed[-1], _pack(weights), _pack(moments_m), _pack(moments_v), "adamw_replicated")
    for store, packed in zip((grad, delta, new_m, new_v), outs):
        store.update(_unpack(packed, weights))

    return (loss, gx[None], *[grad[n] for n in _WEIGHT_ORDER], *[delta[n] for n in _WEIGHT_ORDER],
            *[new_m[n] for n in _WEIGHT_ORDER], *[new_v[n] for n in _WEIGHT_ORDER])
```

```python
import functools

import jax
import jax.numpy as jnp
from jax import lax
from jax.experimental import pallas as pl
from jax.experimental.pallas import tpu as pltpu

F32 = jnp.float32
BF16 = jnp.bfloat16

N_DEV = 8
D_MODEL = 1024
CHUNK = 64
CHUNK_SHIFT = 6
N_META = 16
D_CONV = 512
CONV_WIDTH = 31
N_HEADS = 8
QK_NOPE = 64
QK_ROPE = 32
QK_DIM = QK_NOPE + QK_ROPE
V_HEAD = 64
KV_HEAD = QK_NOPE + V_HEAD
D_ATTN = N_HEADS * V_HEAD
Q_LORA = 384
KV_LORA = 256
ROPE_THETA = 10000.0
D_IN = 2 * D_CONV + Q_LORA + KV_LORA + QK_ROPE
D_FF = 2816
D_UP = 2 * D_FF
FFN_CONV_WIDTH = 3
UP_SLAB = D_UP // N_DEV
N_ACT_SLAB = D_FF // UP_SLAB
EPS = 1e-6
NEG = -1e30
ADAM_LR = 0.001
ADAM_B1 = 0.9
ADAM_B2 = 0.999
ADAM_EPS = 1e-08
ADAM_WD = 0.01
ADAM_STEP = 10

ROW_TILE = 256
DEAD = ROW_TILE - N_META
CONV_HALO = 32
FFN_HALO = 16
VMEM_LIMIT = 56 * 1024 * 1024
_LANES = 128

MESH = pl.DeviceIdType.MESH


def _dot(a, b):
    return jnp.dot(a, b, preferred_element_type=F32)


def _dot_nt(a, b):
    return lax.dot_general(a, b, (((1,), (1,)), ((), ())), preferred_element_type=F32)


def _dot_tn(a, b):
    return lax.dot_general(a, b, (((0,), (0,)), ((), ())), preferred_element_type=F32)


def _sigmoid(x):
    return 1.0 / (1.0 + jnp.exp(-x))


def _rms_fwd(x, g):
    r = lax.rsqrt(jnp.mean(x * x, axis=-1, keepdims=True) + EPS)
    return x * r * g


def _rms_bwd(dy, x, g):
    r = lax.rsqrt(jnp.mean(x * x, axis=-1, keepdims=True) + EPS)
    w = dy * g
    dx = r * w - x * (r * r * r) * jnp.mean(w * x, axis=-1, keepdims=True)
    return dx, jnp.sum(dy * x * r, axis=0, keepdims=True)


def _rope(x, cos, sin):
    half = QK_ROPE // 2
    x1, x2 = x[:, :half], x[:, half:]
    return jnp.concatenate([x1 * cos - x2 * sin, x2 * cos + x1 * sin], axis=-1)


def _rope_t(dy, cos, sin):
    half = QK_ROPE // 2
    d1, d2 = dy[:, :half], dy[:, half:]
    return jnp.concatenate([d1 * cos + d2 * sin, d2 * cos - d1 * sin], axis=-1)


def _row_ids(i, rows):
    return i * rows + lax.broadcasted_iota(jnp.int32, (rows, 1), 0)


def _accumulate(ref, first, value):
    @pl.when(first)
    def _():
        ref[...] = value

    @pl.when(jnp.logical_not(first))
    def _():
        ref[...] += value


def _tile_spec(shape):
    nd = len(shape)
    if nd == 2:
        return pl.BlockSpec((ROW_TILE, shape[1]), lambda i: (i, 0))
    return pl.BlockSpec((shape[0], ROW_TILE, shape[2]), lambda i: (0, i, 0))


def _whole_spec(shape):
    nd = len(shape)
    return pl.BlockSpec(tuple(shape), lambda i: (0,) * nd, pipeline_mode=pl.Buffered(1))


def _acc_spec(shape):
    nd = len(shape)
    return pl.BlockSpec(tuple(shape), lambda i: (0,) * nd)


def _real_spec(width):
    return pl.BlockSpec((ROW_TILE, width), lambda i: (jnp.maximum(i - 1, 0), 0))


def _params(*semantics):
    return pltpu.CompilerParams(dimension_semantics=semantics, vmem_limit_bytes=VMEM_LIMIT)


def _fwd_in(x, meta_pad, g1, w_in, n_rows):
    nt = n_rows // ROW_TILE

    def body(x_ref, meta_ref, g_ref, w_ref, nb_ref, ag_ref, cq_ref, ckv_ref, kr_ref):
        i = pl.program_id(0)
        h0 = jnp.where(i == 0, meta_ref[...], x_ref[...])
        nb = _rms_fwd(h0, g_ref[...]).astype(BF16)
        nb_ref[...] = nb
        z = _dot(nb, w_ref[...])
        ag_ref[...] = z[:, :2 * D_CONV]
        cq_ref[...] = z[:, 2 * D_CONV:2 * D_CONV + Q_LORA]
        ckv_ref[...] = z[:, 2 * D_CONV + Q_LORA:2 * D_CONV + Q_LORA + KV_LORA]
        kr_ref[...] = z[:, 2 * D_CONV + Q_LORA + KV_LORA:]

    out_shapes = [
        jax.ShapeDtypeStruct((n_rows, D_MODEL), BF16),
        jax.ShapeDtypeStruct((n_rows, 2 * D_CONV), F32),
        jax.ShapeDtypeStruct((n_rows, Q_LORA), F32),
        jax.ShapeDtypeStruct((n_rows, KV_LORA), F32),
        jax.ShapeDtypeStruct((n_rows, QK_ROPE), F32),
    ]
    return pl.pallas_call(
        body, name="fwd_in", grid=(nt,),
        in_specs=[_real_spec(D_MODEL), _whole_spec(meta_pad.shape), _whole_spec(g1.shape), _whole_spec(w_in.shape)],
        out_specs=[_tile_spec(s.shape) for s in out_shapes],
        out_shape=out_shapes,
        compiler_params=_params("parallel"),
    )(x, meta_pad, g1, w_in)


def _conv_chain(u1, ln_g, ln_b):
    mu = jnp.mean(u1, axis=-1, keepdims=True)
    xc = u1 - mu
    rstd = lax.rsqrt(jnp.mean(xc * xc, axis=-1, keepdims=True) + EPS)
    xh = xc * rstd
    u2 = xh * ln_g + ln_b
    return xh, u2, u2 * _sigmoid(u2), rstd


def _fwd_conv(ag, conv_w, conv_b, ln_g, ln_b, out_g, n_rows):
    nt = n_rows // ROW_TILE

    def body(ag_ref, w_ref, b_ref, lg_ref, lb_ref, og_ref, mix_ref, u1_ref, ext_ref):
        i = pl.program_id(0)

        @pl.when(i == 0)
        def _():
            ext_ref[0:CONV_HALO, :] = jnp.zeros((CONV_HALO, D_CONV), F32)

        ag_t = ag_ref[...]
        live = _row_ids(i, ROW_TILE) >= DEAD
        u0 = jnp.where(live, ag_t[:, :D_CONV] * _sigmoid(ag_t[:, D_CONV:]), 0.0)
        ext_ref[CONV_HALO:, :] = u0
        first = CONV_HALO - (CONV_WIDTH - 1)
        acc = jnp.zeros((ROW_TILE, D_CONV), F32)
        for k in range(CONV_WIDTH):
            acc = acc + w_ref[k:k + 1, :] * ext_ref[first + k:first + k + ROW_TILE, :]
        u1 = acc + b_ref[...]
        ext_ref[0:CONV_HALO, :] = ext_ref[ROW_TILE:ROW_TILE + CONV_HALO, :]
        u1_ref[...] = u1
        _, _, u3, _ = _conv_chain(u1, lg_ref[...], lb_ref[...])
        mix_ref[...] = _rms_fwd(u3, og_ref[...]).astype(BF16)

    out_shapes = [jax.ShapeDtypeStruct((n_rows, D_CONV), BF16), jax.ShapeDtypeStruct((n_rows, D_CONV), F32)]
    small = [conv_w, conv_b, ln_g, ln_b, out_g]
    return pl.pallas_call(
        body, name="fwd_conv", grid=(nt,),
        in_specs=[_tile_spec(ag.shape)] + [_whole_spec(a.shape) for a in small],
        out_specs=[_tile_spec(s.shape) for s in out_shapes],
        out_shape=out_shapes,
        scratch_shapes=[pltpu.VMEM((ROW_TILE + CONV_HALO, D_CONV), F32)],
        compiler_params=_params("arbitrary"),
    )(ag, *small)


def _lane_tile(shape):
    if len(shape) == 2:
        return pl.BlockSpec((shape[0], ROW_TILE), lambda i: (0, i))
    return pl.BlockSpec((shape[0], shape[1], ROW_TILE), lambda i: (0, 0, i))


def _rope_rows(x, cos, sin):
    half = QK_ROPE // 2
    x1, x2 = x[:half], x[half:]
    return jnp.concatenate([x1 * cos - x2 * sin, x2 * cos + x1 * sin], axis=0)


def _rope_rows_t(dy, cos, sin):
    half = QK_ROPE // 2
    d1, d2 = dy[:half], dy[half:]
    return jnp.concatenate([d1 * cos + d2 * sin, d2 * cos - d1 * sin], axis=0)


def _fwd_qkv(cq, ckv, kr, gq, gkv, wq_t, w_ukv, wv_t, cos, sin, cos_t, sin_t, n_rows):
    nt = n_rows // ROW_TILE

    def body(cq_ref, ckv_ref, kr_ref, gq_ref, gkv_ref, wqt_ref, wkv_ref, wvt_ref, cos_ref, sin_ref, cost_ref, sint_ref,
             qt_ref, k_ref, v_ref, vt_ref, cqn_ref, ckvn_ref):
        cqn = _rms_fwd(cq_ref[...], gq_ref[...]).astype(BF16)
        ckvn = _rms_fwd(ckv_ref[...], gkv_ref[...]).astype(BF16)
        cqn_ref[...] = cqn
        ckvn_ref[...] = ckvn
        k_rot = _rope(kr_ref[...], cos_ref[...], sin_ref[...])
        cos_rows, sin_rows = cost_ref[...], sint_ref[...]
        for h in range(N_HEADS):
            q_raw = _dot_nt(wqt_ref[h], cqn)
            qt_ref[h] = jnp.concatenate(
                [q_raw[:QK_NOPE], _rope_rows(q_raw[QK_NOPE:], cos_rows, sin_rows)], axis=0).astype(BF16)
            kv = _dot(ckvn, wkv_ref[h])
            k_ref[h] = jnp.concatenate([kv[:, :QK_NOPE], k_rot], axis=-1).astype(BF16)
            v_ref[h] = kv[:, QK_NOPE:].astype(BF16)
            vt_ref[h] = _dot_nt(wvt_ref[h], ckvn).astype(BF16)

    out_shapes = [
        jax.ShapeDtypeStruct((N_HEADS, QK_DIM, n_rows), BF16),
        jax.ShapeDtypeStruct((N_HEADS, n_rows, QK_DIM), BF16),
        jax.ShapeDtypeStruct((N_HEADS, n_rows, V_HEAD), BF16),
        jax.ShapeDtypeStruct((N_HEADS, V_HEAD, n_rows), BF16),
        jax.ShapeDtypeStruct((n_rows, Q_LORA), BF16),
        jax.ShapeDtypeStruct((n_rows, KV_LORA), BF16),
    ]
    tiles = [cq, ckv, kr]
    whole = [gq, gkv, wq_t, w_ukv, wv_t]
    out_specs = [_lane_tile(out_shapes[0].shape), _tile_spec(out_shapes[1].shape), _tile_spec(out_shapes[2].shape),
                 _lane_tile(out_shapes[3].shape), _tile_spec(out_shapes[4].shape), _tile_spec(out_shapes[5].shape)]
    return pl.pallas_call(
        body, name="fwd_qkv", grid=(nt,),
        in_specs=[_tile_spec(a.shape) for a in tiles] + [_whole_spec(a.shape) for a in whole]
        + [_tile_spec(cos.shape), _tile_spec(sin.shape), _lane_tile(cos_t.shape), _lane_tile(sin_t.shape)],
        out_specs=out_specs,
        out_shape=out_shapes,
        compiler_params=_params("parallel"),
    )(*tiles, *whole, cos, sin, cos_t, sin_t)


def _chunk_of(rows):
    return jnp.where(rows >= ROW_TILE, lax.shift_right_arithmetic(rows - ROW_TILE, CHUNK_SHIFT) + 1, 0)


def _visible(i, j):
    k_rows = j * ROW_TILE + lax.broadcasted_iota(jnp.int32, (ROW_TILE, 1), 0)
    q_rows = i * ROW_TILE + lax.broadcasted_iota(jnp.int32, (1, ROW_TILE), 1)
    return jnp.logical_and(_chunk_of(q_rows) >= _chunk_of(k_rows), k_rows >= DEAD)


def _attn_fwd(q_t, k, v_t, n_rows):
    nt = n_rows // ROW_TILE
    scale = QK_DIM ** -0.5

    def body(qt_ref, k_ref, vt_ref, ot_ref, lse_ref):
        i = pl.program_id(0)
        q_ts = [qt_ref[h] for h in range(N_HEADS)]

        def make_step(masked):
            def step(j, carry):
                rows = pl.ds(pl.multiple_of(j * ROW_TILE, ROW_TILE), ROW_TILE)
                scores = [_dot(k_ref[h, rows, :], q_ts[h]) for h in range(N_HEADS)]
                visible = _visible(i, j) if masked else None
                probs, state = [], []
                for h in range(N_HEADS):
                    m, l, _ = carry[h]
                    s = scores[h] * scale
                    if masked:
                        s = jnp.where(visible, s, NEG)
                    m_new = jnp.maximum(m, jnp.max(s, axis=0, keepdims=True))
                    alpha = jnp.exp(m - m_new)
                    p = jnp.exp(s - m_new)
                    probs.append(p.astype(BF16))
                    state.append((m_new, alpha * l + jnp.sum(p, axis=0, keepdims=True), alpha))
                outs = [_dot(vt_ref[h, :, rows], probs[h]) for h in range(N_HEADS)]
                return tuple((state[h][0], state[h][1], state[h][2] * carry[h][2] + outs[h]) for h in range(N_HEADS))
            return step

        init = tuple((jnp.full((1, ROW_TILE), NEG, F32), jnp.zeros((1, ROW_TILE), F32),
                      jnp.zeros((V_HEAD, ROW_TILE), F32)) for _ in range(N_HEADS))
        carry = make_step(True)(0, init)
        carry = lax.fori_loop(1, i, make_step(False), carry)
        carry = lax.fori_loop(jnp.maximum(i, 1), i + 1, make_step(True), carry)
        for h in range(N_HEADS):
            m, l, acc = carry[h]
            ot_ref[h] = acc / l
            lse_ref[h] = m + jnp.log(l)

    out_shapes = [jax.ShapeDtypeStruct((N_HEADS, V_HEAD, n_rows), F32), jax.ShapeDtypeStruct((N_HEADS, 1, n_rows), F32)]
    return pl.pallas_call(
        body, name="attn_fwd", grid=(nt,),
        in_specs=[_lane_tile(q_t.shape), _whole_spec(k.shape), _whole_spec(v_t.shape)],
        out_specs=[_lane_tile(s.shape) for s in out_shapes],
        out_shape=out_shapes,
        compiler_params=_params("parallel"),
    )(q_t, k, v_t)


def _heads_to_rows(ref):
    return jnp.concatenate([ref[h] for h in range(N_HEADS)], axis=0)


def _rms_cols(x, g_col):
    r = lax.rsqrt(jnp.mean(x * x, axis=0, keepdims=True) + EPS)
    return x * r * g_col


def _fwd_out(x, meta_pad, mix_a, o_t, gb_col, w_out, n_rows):
    nt = n_rows // ROW_TILE

    def body(x_ref, meta_ref, mixa_ref, ot_ref, gb_ref, w_ref, mixbt_ref, h1_ref):
        i = pl.program_id(0)
        h0 = jnp.where(i == 0, meta_ref[...], x_ref[...])
        mix_bt = _rms_cols(_heads_to_rows(ot_ref), gb_ref[...]).astype(BF16)
        mixbt_ref[...] = mix_bt
        h1_ref[...] = h0 + _dot(mixa_ref[...], w_ref[:D_CONV, :]) + _dot_tn(mix_bt, w_ref[D_CONV:, :])

    out_shapes = [jax.ShapeDtypeStruct((D_ATTN, n_rows), BF16), jax.ShapeDtypeStruct((n_rows, D_MODEL), F32)]
    return pl.pallas_call(
        body, name="fwd_out", grid=(nt,),
        in_specs=[_real_spec(D_MODEL), _whole_spec(meta_pad.shape), _tile_spec(mix_a.shape), _lane_tile(o_t.shape),
                  _whole_spec(gb_col.shape), _whole_spec(w_out.shape)],
        out_specs=[_lane_tile(out_shapes[0].shape), _tile_spec(out_shapes[1].shape)],
        out_shape=out_shapes,
        compiler_params=_params("parallel"),
    )(x, meta_pad, mix_a, o_t, gb_col, w_out)


def _ffn_conv(ext_ref, w_ref, b_ref, s):
    first = FFN_HALO - (FFN_CONV_WIDTH - 1)
    acc = b_ref[s]
    for k in range(FFN_CONV_WIDTH):
        acc = acc + w_ref[s, k:k + 1, :] * ext_ref[s, first + k:first + k + ROW_TILE, :]
    return acc


def _fwd_ffn(h1, target, g2, w_up, fw, fb, w_down, gf, n_rows):
    nt = n_rows // ROW_TILE

    def body(h1_ref, t_ref, g2_ref, wup_ref, fw_ref, fb_ref, wdn_ref, gf_ref,
             n2_ref, up0_ref, act_ref, dh2_ref, loss_ref, dgf_ref, ext_ref):
        i = pl.program_id(0)

        @pl.when(i == 0)
        def _():
            ext_ref[:, 0:FFN_HALO, :] = jnp.zeros((N_DEV, FFN_HALO, UP_SLAB), F32)

        h1_t = h1_ref[...]
        n2 = _rms_fwd(h1_t, g2_ref[...]).astype(BF16)
        n2_ref[...] = n2
        live = _row_ids(i, ROW_TILE) >= DEAD
        for s in range(N_DEV):
            up0 = jnp.where(live, _dot(n2, wup_ref[s]), 0.0).astype(BF16)
            up0_ref[s] = up0
            ext_ref[s, FFN_HALO:, :] = up0.astype(F32)
        h2 = h1_t
        for s in range(N_ACT_SLAB):
            gate = _ffn_conv(ext_ref, fw_ref, fb_ref, s)
            val = _ffn_conv(ext_ref, fw_ref, fb_ref, s + N_ACT_SLAB)
            act = (gate * _sigmoid(gate) * val).astype(BF16)
            act_ref[s] = act
            h2 = h2 + _dot(act, wdn_ref[s])
        ext_ref[:, 0:FFN_HALO, :] = ext_ref[:, ROW_TILE:ROW_TILE + FFN_HALO, :]

        gf_t = gf_ref[...]
        y = _rms_fwd(h2, gf_t)
        diff = jnp.where(i >= 1, y - t_ref[...], 0.0)
        tile_loss = 0.5 * jnp.sum(jnp.sum(diff * diff, axis=-1, keepdims=True), axis=0, keepdims=True) / D_MODEL
        dh2, dgf = _rms_bwd(diff / D_MODEL, h2, gf_t)
        dh2_ref[...] = dh2
        _accumulate(loss_ref, i == 0, jnp.broadcast_to(tile_loss, loss_ref.shape))
        _accumulate(dgf_ref, i == 0, dgf)

    out_shapes = [
        jax.ShapeDtypeStruct((n_rows, D_MODEL), BF16),
        jax.ShapeDtypeStruct((N_DEV, n_rows, UP_SLAB), BF16),
        jax.ShapeDtypeStruct((N_ACT_SLAB, n_rows, UP_SLAB), BF16),
        jax.ShapeDtypeStruct((n_rows, D_MODEL), F32),
        jax.ShapeDtypeStruct((8, 128), F32),
        jax.ShapeDtypeStruct((1, D_MODEL), F32),
    ]
    whole = [g2, w_up, fw, fb, w_down, gf]
    return pl.pallas_call(
        body, name="fwd_ffn", grid=(nt,),
        in_specs=[_tile_spec(h1.shape), _real_spec(D_MODEL)] + [_whole_spec(a.shape) for a in whole],
        out_specs=[_tile_spec(s.shape) for s in out_shapes[:4]] + [_acc_spec(s.shape) for s in out_shapes[4:]],
        out_shape=out_shapes,
        scratch_shapes=[pltpu.VMEM((N_DEV, ROW_TILE + FFN_HALO, UP_SLAB), F32)],
        compiler_params=_params("arbitrary"),
    )(h1, target, *whole)


def _rope_tables(n_rows):
    pos = jnp.maximum(jnp.arange(n_rows, dtype=jnp.int32) - DEAD, 0)
    inv_freq = 1.0 / (ROPE_THETA ** (jnp.arange(0, QK_ROPE, 2, dtype=F32) / QK_ROPE))
    ang = pos.astype(F32)[:, None] * inv_freq[None, :]
    return jnp.cos(ang), jnp.sin(ang)


def _halo_after(shape, halo, n_rows):
    last = n_rows // halo - 1
    step = ROW_TILE // halo
    if len(shape) == 2:
        return pl.BlockSpec((halo, shape[1]), lambda i: (jnp.minimum((i + 1) * step, last), 0))
    return pl.BlockSpec((shape[0], halo, shape[2]), lambda i: (0, jnp.minimum((i + 1) * step, last), 0))


def _halo_before(shape, halo):
    step = ROW_TILE // halo
    if len(shape) == 2:
        return pl.BlockSpec((halo, shape[1]), lambda i: (jnp.maximum(i * step - 1, 0), 0))
    return pl.BlockSpec((shape[0], halo, shape[2]), lambda i: (0, jnp.maximum(i * step - 1, 0), 0))


def _bwd_ffn_act(dh2, up0, w_down, fw, fb, n_rows):
    nt = n_rows // ROW_TILE

    def body(dh2_ref, up0_ref, wdn_ref, fw_ref, fb_ref, dup_ref, dfb_ref, ext_ref):
        i = pl.program_id(0)

        @pl.when(i == 0)
        def _():
            ext_ref[:, 0:FFN_HALO, :] = jnp.zeros((N_DEV, FFN_HALO, UP_SLAB), F32)
            dfb_ref[...] = jnp.zeros_like(dfb_ref)

        for s in range(N_DEV):
            ext_ref[s, FFN_HALO:, :] = up0_ref[s].astype(F32)
        dh2_b = dh2_ref[...].astype(BF16)
        for s in range(N_ACT_SLAB):
            gate = _ffn_conv(ext_ref, fw_ref, fb_ref, s)
            val = _ffn_conv(ext_ref, fw_ref, fb_ref, s + N_ACT_SLAB)
            d_act = _dot_nt(dh2_b, wdn_ref[s])
            sg = _sigmoid(gate)
            d_gate = d_act * val * sg * (1.0 + gate * (1.0 - sg))
            d_val = d_act * gate * sg
            dup_ref[s] = d_gate.astype(BF16)
            dup_ref[s + N_ACT_SLAB] = d_val.astype(BF16)
            dfb_ref[s] += jnp.sum(d_gate, axis=0, keepdims=True)
            dfb_ref[s + N_ACT_SLAB] += jnp.sum(d_val, axis=0, keepdims=True)
        ext_ref[:, 0:FFN_HALO, :] = ext_ref[:, ROW_TILE:ROW_TILE + FFN_HALO, :]

    out_shapes = [jax.ShapeDtypeStruct((N_DEV, n_rows, UP_SLAB), BF16), jax.ShapeDtypeStruct((N_DEV, 1, UP_SLAB), F32)]
    whole = [w_down, fw, fb]
    return pl.pallas_call(
        body, name="bwd_ffn_act", grid=(nt,),
        in_specs=[_tile_spec(dh2.shape), _tile_spec(up0.shape)] + [_whole_spec(a.shape) for a in whole],
        out_specs=[_tile_spec(out_shapes[0].shape), _acc_spec(out_shapes[1].shape)],
        out_shape=out_shapes,
        scratch_shapes=[pltpu.VMEM((N_DEV, ROW_TILE + FFN_HALO, UP_SLAB), F32)],
        compiler_params=_params("arbitrary"),
    )(dh2, up0, *whole)


def _bwd_ffn_up(dup, up0, h1, dh2, g2, w_up, fw, n_rows):
    nt = n_rows // ROW_TILE

    def body(dup_ref, dnext_ref, up0_ref, uprev_ref, h1_ref, dh2_ref, g2_ref, wup_ref, fw_ref,
             dup0_ref, dh1_ref, dfw_ref, dg2_ref, dext_ref, uext_ref):
        i = pl.program_id(0)

        @pl.when(i == 0)
        def _():
            dfw_ref[...] = jnp.zeros_like(dfw_ref)

        live = _row_ids(i, ROW_TILE) >= DEAD
        dn2 = jnp.zeros((ROW_TILE, D_MODEL), F32)
        for s in range(N_DEV):
            d = dup_ref[s].astype(F32)
            dext_ref[0:ROW_TILE, :] = d
            dext_ref[ROW_TILE:, :] = jnp.where(i == nt - 1, 0.0, dnext_ref[s].astype(F32))
            uext_ref[0:FFN_HALO, :] = jnp.where(i == 0, 0.0, uprev_ref[s].astype(F32))
            uext_ref[FFN_HALO:, :] = up0_ref[s].astype(F32)
            dup0 = jnp.zeros((ROW_TILE, UP_SLAB), F32)
            for k in range(FFN_CONV_WIDTH):
                back = FFN_CONV_WIDTH - 1 - k
                dup0 = dup0 + fw_ref[s, k:k + 1, :] * dext_ref[back:back + ROW_TILE, :]
                first = FFN_HALO - back
                dfw_ref[s, k:k + 1, :] += jnp.sum(d * uext_ref[first:first + ROW_TILE, :], axis=0, keepdims=True)
            dup0_b = jnp.where(live, dup0, 0.0).astype(BF16)
            dup0_ref[s] = dup0_b
            dn2 = dn2 + _dot_nt(dup0_b, wup_ref[s])
        dx, dg2 = _rms_bwd(dn2, h1_ref[...], g2_ref[...])
        dh1_ref[...] = dh2_ref[...] + dx
        _accumulate(dg2_ref, i == 0, dg2)

    out_shapes = [
        jax.ShapeDtypeStruct((N_DEV, n_rows, UP_SLAB), BF16),
        jax.ShapeDtypeStruct((n_rows, D_MODEL), F32),
        jax.ShapeDtypeStruct((N_DEV, FFN_CONV_WIDTH, UP_SLAB), F32),
        jax.ShapeDtypeStruct((1, D_MODEL), F32),
    ]
    return pl.pallas_call(
        body, name="bwd_ffn_up", grid=(nt,),
        in_specs=[_tile_spec(dup.shape), _halo_after(dup.shape, FFN_HALO, n_rows), _tile_spec(up0.shape),
                  _halo_before(up0.shape, FFN_HALO), _tile_spec(h1.shape), _tile_spec(dh2.shape),
                  _whole_spec(g2.shape), _whole_spec(w_up.shape), _whole_spec(fw.shape)],
        out_specs=[_tile_spec(s.shape) for s in out_shapes[:2]] + [_acc_spec(s.shape) for s in out_shapes[2:]],
        out_shape=out_shapes,
        scratch_shapes=[pltpu.VMEM((ROW_TILE + FFN_HALO, UP_SLAB), F32), pltpu.VMEM((ROW_TILE + FFN_HALO, UP_SLAB), F32)],
        compiler_params=_params("arbitrary"),
    )(dup, dup, up0, up0, h1, dh2, g2, w_up, fw)


def _bwd_out(dh1, o_t, u1, w_out, gb_col, ln_g, ln_b, ga, n_rows):
    nt = n_rows // ROW_TILE

    def body(dh1_ref, ot_ref, u1_ref, w_ref, gb_ref, lg_ref, lb_ref, ga_ref,
             dot_ref, delta_ref, du1_ref, dgb_ref, dga_ref, dlg_ref, dlb_ref, dcb_ref):
        i = pl.program_id(0)
        dh1_b = dh1_ref[...].astype(BF16)
        o_t = _heads_to_rows(ot_ref)
        gb = gb_ref[...]
        r = lax.rsqrt(jnp.mean(o_t * o_t, axis=0, keepdims=True) + EPS)
        dmix_bt = _dot_nt(w_ref[D_CONV:, :], dh1_b)
        wgt = dmix_bt * gb
        do_t = r * wgt - o_t * (r * r * r) * jnp.mean(wgt * o_t, axis=0, keepdims=True)
        dgb = jnp.sum(dmix_bt * o_t * r, axis=1, keepdims=True)
        for h in range(N_HEADS):
            do_h = do_t[h * V_HEAD:(h + 1) * V_HEAD]
            dot_ref[h] = do_h.astype(BF16)
            delta_ref[h] = jnp.sum(do_h * ot_ref[h], axis=0, keepdims=True)
        lg = lg_ref[...]
        xh, u2, u3, rstd = _conv_chain(u1_ref[...], lg, lb_ref[...])
        du3, dga = _rms_bwd(_dot_nt(dh1_b, w_ref[:D_CONV, :]), u3, ga_ref[...])
        sg = _sigmoid(u2)
        du2 = du3 * sg * (1.0 + u2 * (1.0 - sg))
        dxh = du2 * lg
        du1 = rstd * (dxh - jnp.mean(dxh, axis=-1, keepdims=True) - xh * jnp.mean(dxh * xh, axis=-1, keepdims=True))
        du1_ref[...] = du1
        first = i == 0
        _accumulate(dgb_ref, first, dgb)
        _accumulate(dga_ref, first, dga)
        _accumulate(dlg_ref, first, jnp.sum(du2 * xh, axis=0, keepdims=True))
        _accumulate(dlb_ref, first, jnp.sum(du2, axis=0, keepdims=True))
        _accumulate(dcb_ref, first, jnp.sum(du1, axis=0, keepdims=True))

    out_shapes = [
        jax.ShapeDtypeStruct((N_HEADS, V_HEAD, n_rows), BF16),
        jax.ShapeDtypeStruct((N_HEADS, 1, n_rows), F32),
        jax.ShapeDtypeStruct((n_rows, D_CONV), F32),
        jax.ShapeDtypeStruct((D_ATTN, 1), F32),
    ] + [jax.ShapeDtypeStruct((1, D_CONV), F32)] * 4
    whole = [w_out, gb_col, ln_g, ln_b, ga]
    return pl.pallas_call(
        body, name="bwd_out", grid=(nt,),
        in_specs=[_tile_spec(dh1.shape), _lane_tile(o_t.shape), _tile_spec(u1.shape)] + [_whole_spec(a.shape) for a in whole],
        out_specs=[_lane_tile(out_shapes[0].shape), _lane_tile(out_shapes[1].shape), _tile_spec(out_shapes[2].shape)]
        + [_acc_spec(s.shape) for s in out_shapes[3:]],
        out_shape=out_shapes,
        compiler_params=_params("arbitrary"),
    )(dh1, o_t, u1, *whole)


ATTN_BWD_HEADS = 4


def _attn_bwd(q_t, k, v, do_t, lse, delta, n_rows):
    nt = n_rows // ROW_TILE
    scale = QK_DIM ** -0.5
    hp = ATTN_BWD_HEADS

    def body(k_ref, v_ref, qt_ref, dot_ref, lse_ref, delta_ref, dqt_ref, dk_ref, dv_ref):
        j = pl.program_id(1)

        @pl.when(j == 0)
        def _():
            dqt_ref[...] = jnp.zeros_like(dqt_ref)

        k_ts = [k_ref[h] for h in range(hp)]
        v_ts = [v_ref[h] for h in range(hp)]

        def make_step(masked):
            def step(i, carry):
                cols = pl.ds(pl.multiple_of(i * ROW_TILE, ROW_TILE), ROW_TILE)
                q_is = [qt_ref[h, :, cols] for h in range(hp)]
                do_is = [dot_ref[h, :, cols] for h in range(hp)]
                scores = [_dot(k_ts[h], q_is[h]) for h in range(hp)]
                dps = [_dot(v_ts[h], do_is[h]) for h in range(hp)]
                visible = _visible(i, j) if masked else None
                probs, dss = [], []
                for h in range(hp):
                    s = scores[h] * scale
                    if masked:
                        s = jnp.where(visible, s, NEG)
                    p = jnp.exp(s - lse_ref[h, :, cols])
                    probs.append(p.astype(BF16))
                    dss.append((p * (dps[h] - delta_ref[h, :, cols]) * scale).astype(BF16))
                out = []
                for h in range(hp):
                    dk, dv = carry[h]
                    dv = dv + _dot_nt(probs[h], do_is[h])
                    dk = dk + _dot_nt(dss[h], q_is[h])
                    dqt_ref[h, :, cols] += _dot_tn(k_ts[h], dss[h])
                    out.append((dk, dv))
                return tuple(out)
            return step

        init = tuple((jnp.zeros((ROW_TILE, QK_DIM), F32), jnp.zeros((ROW_TILE, V_HEAD), F32)) for _ in range(hp))
        carry = make_step(True)(j, init)
        carry = lax.fori_loop(jnp.where(j == 0, j + 1, nt), nt, make_step(True), carry)
        carry = lax.fori_loop(jnp.where(j == 0, nt, j + 1), nt, make_step(False), carry)
        for h in range(hp):
            dk_ref[h], dv_ref[h] = carry[h]

    key_tile = lambda w: pl.BlockSpec((hp, ROW_TILE, w), lambda g, j: (g, j, 0))
    all_cols = lambda w: pl.BlockSpec((hp, w, n_rows), lambda g, j: (g, 0, 0))
    out_shapes = [
        jax.ShapeDtypeStruct((N_HEADS, QK_DIM, n_rows), F32),
        jax.ShapeDtypeStruct((N_HEADS, n_rows, QK_DIM), F32),
        jax.ShapeDtypeStruct((N_HEADS, n_rows, V_HEAD), F32),
    ]
    return pl.pallas_call(
        body, name="attn_bwd", grid=(N_HEADS // hp, nt),
        in_specs=[key_tile(QK_DIM), key_tile(V_HEAD), all_cols(QK_DIM), all_cols(V_HEAD), all_cols(1), all_cols(1)],
        out_specs=[all_cols(QK_DIM), key_tile(QK_DIM), key_tile(V_HEAD)],
        out_shape=out_shapes,
        compiler_params=_params("parallel", "arbitrary"),
    )(k, v, q_t, do_t, lse, delta)


def _bwd_qkv(dq_t, dk, dv, cq, ckv, gq, gkv, wq_t, w_ukv, cos, sin, cos_t, sin_t, n_rows):
    nt = n_rows // ROW_TILE

    def body(dqt_ref, dk_ref, dv_ref, cq_ref, ckv_ref, gq_ref, gkv_ref, wqt_ref, wkv_ref, cos_ref, sin_ref,
             cost_ref, sint_ref, dqraw_ref, dkv_ref, dcq_ref, dckv_ref, dkr_ref, dgq_ref, dgkv_ref):
        i = pl.program_id(0)
        cos_rows, sin_rows = cost_ref[...], sint_ref[...]
        dcqn = jnp.zeros((ROW_TILE, Q_LORA), F32)
        dckvn = jnp.zeros((ROW_TILE, KV_LORA), F32)
        dk_rot = jnp.zeros((ROW_TILE, QK_ROPE), F32)
        for h in range(N_HEADS):
            dq_h, dk_h = dqt_ref[h], dk_ref[h]
            dq_raw = jnp.concatenate(
                [dq_h[:QK_NOPE], _rope_rows_t(dq_h[QK_NOPE:], cos_rows, sin_rows)], axis=0).astype(BF16)
            dqraw_ref[h] = dq_raw
            dcqn = dcqn + _dot_tn(dq_raw, wqt_ref[h])
            dkv = jnp.concatenate([dk_h[:, :QK_NOPE], dv_ref[h]], axis=-1).astype(BF16)
            dkv_ref[h] = dkv
            dckvn = dckvn + _dot_nt(dkv, wkv_ref[h])
            dk_rot = dk_rot + dk_h[:, QK_NOPE:]
        dkr_ref[...] = _rope_t(dk_rot, cos_ref[...], sin_ref[...]).astype(BF16)
        dcq, dgq = _rms_bwd(dcqn, cq_ref[...], gq_ref[...])
        dckv, dgkv = _rms_bwd(dckvn, ckv_ref[...], gkv_ref[...])
        dcq_ref[...] = dcq.astype(BF16)
        dckv_ref[...] = dckv.astype(BF16)
        _accumulate(dgq_ref, i == 0, dgq)
        _accumulate(dgkv_ref, i == 0, dgkv)

    out_shapes = [
        jax.ShapeDtypeStruct((N_HEADS, QK_DIM, n_rows), BF16),
        jax.ShapeDtypeStruct((N_HEADS, n_rows, KV_HEAD), BF16),
        jax.ShapeDtypeStruct((n_rows, Q_LORA), BF16),
        jax.ShapeDtypeStruct((n_rows, KV_LORA), BF16),
        jax.ShapeDtypeStruct((n_rows, QK_ROPE), BF16),
        jax.ShapeDtypeStruct((1, Q_LORA), F32),
        jax.ShapeDtypeStruct((1, KV_LORA), F32),
    ]
    tiles = [dk, dv, cq, ckv]
    whole = [gq, gkv, wq_t, w_ukv]
    return pl.pallas_call(
        body, name="bwd_qkv", grid=(nt,),
        in_specs=[_lane_tile(dq_t.shape)] + [_tile_spec(a.shape) for a in tiles] + [_whole_spec(a.shape) for a in whole]
        + [_tile_spec(cos.shape), _tile_spec(sin.shape), _lane_tile(cos_t.shape), _lane_tile(sin_t.shape)],
        out_specs=[_lane_tile(out_shapes[0].shape)] + [_tile_spec(s.shape) for s in out_shapes[1:5]]
        + [_acc_spec(s.shape) for s in out_shapes[5:]],
        out_shape=out_shapes,
        compiler_params=_params("arbitrary"),
    )(dq_t, *tiles, *whole, cos, sin, cos_t, sin_t)


def _bwd_conv(du1, ag, conv_w, n_rows):
    nt = n_rows // ROW_TILE

    def glu(ag_t, rows):
        sg = _sigmoid(ag_t[:, D_CONV:])
        return jnp.where(rows >= DEAD, ag_t[:, :D_CONV] * sg, 0.0), sg

    def body(du1_ref, dnext_ref, ag_ref, agprev_ref, w_ref, dag_ref, dw_ref, dext_ref, uext_ref):
        i = pl.program_id(0)

        @pl.when(i == 0)
        def _():
            dw_ref[...] = jnp.zeros_like(dw_ref)

        du1_t = du1_ref[...]
        dext_ref[0:ROW_TILE, :] = du1_t
        dext_ref[ROW_TILE:, :] = jnp.where(i == nt - 1, 0.0, dnext_ref[...])
        ag_t = ag_ref[...]
        rows = _row_ids(i, ROW_TILE)
        u0, sg = glu(ag_t, rows)
        prev_rows = i * ROW_TILE - CONV_HALO + lax.broadcasted_iota(jnp.int32, (CONV_HALO, 1), 0)
        u0_prev, _ = glu(agprev_ref[...], prev_rows)
        uext_ref[0:CONV_HALO, :] = jnp.where(i == 0, 0.0, u0_prev)
        uext_ref[CONV_HALO:, :] = u0
        du0 = jnp.zeros((ROW_TILE, D_CONV), F32)
        for k in range(CONV_WIDTH):
            back = CONV_WIDTH - 1 - k
            du0 = du0 + w_ref[k:k + 1, :] * dext_ref[back:back + ROW_TILE, :]
            first = CONV_HALO - back
            dw_ref[k:k + 1, :] += jnp.sum(du1_t * uext_ref[first:first + ROW_TILE, :], axis=0, keepdims=True)
        du0 = jnp.where(rows >= DEAD, du0, 0.0)
        da = du0 * sg
        dgate = du0 * ag_t[:, :D_CONV] * sg * (1.0 - sg)
        dag_ref[...] = jnp.concatenate([da, dgate], axis=-1).astype(BF16)

    out_shapes = [jax.ShapeDtypeStruct((n_rows, 2 * D_CONV), BF16), jax.ShapeDtypeStruct((CONV_WIDTH, D_CONV), F32)]
    return pl.pallas_call(
        body, name="bwd_conv", grid=(nt,),
        in_specs=[_tile_spec(du1.shape), _halo_after(du1.shape, CONV_HALO, n_rows), _tile_spec(ag.shape),
                  _halo_before(ag.shape, CONV_HALO), _whole_spec(conv_w.shape)],
        out_specs=[_tile_spec(out_shapes[0].shape), _acc_spec(out_shapes[1].shape)],
        out_shape=out_shapes,
        scratch_shapes=[pltpu.VMEM((ROW_TILE + CONV_HALO, D_CONV), F32), pltpu.VMEM((ROW_TILE + CONV_HALO, D_CONV), F32)],
        compiler_params=_params("arbitrary"),
    )(du1, du1, ag, ag, conv_w)


def _bwd_in(dag, dcq, dckv, dkr, x, meta_pad, dh1, g1, w_in, n_rows):
    nt = n_rows // ROW_TILE

    def body(dag_ref, dcq_ref, dckv_ref, dkr_ref, x_ref, meta_ref, dh1_ref, g_ref, w_ref,
             dz_ref, gx_ref, gmeta_ref, dg1_ref):
        i = pl.program_id(0)
        dz = jnp.concatenate([dag_ref[...], dcq_ref[...], dckv_ref[...], dkr_ref[...]], axis=-1)
        dz_ref[...] = dz
        h0 = jnp.where(i == 0, meta_ref[...], x_ref[...])
        dx, dg1 = _rms_bwd(_dot_nt(dz, w_ref[...]), h0, g_ref[...])
        dh0 = dh1_ref[...] + dx
        gx_ref[...] = dh0

        @pl.when(i == 0)
        def _():
            gmeta_ref[...] = dh0

        _accumulate(dg1_ref, i == 0, dg1)

    out_shapes = [
        jax.ShapeDtypeStruct((n_rows, D_IN), BF16),
        jax.ShapeDtypeStruct((n_rows - ROW_TILE, D_MODEL), F32),
        jax.ShapeDtypeStruct((ROW_TILE, D_MODEL), F32),
        jax.ShapeDtypeStruct((1, D_MODEL), F32),
    ]
    tiles = [dag, dcq, dckv, dkr]
    return pl.pallas_call(
        body, name="bwd_in", grid=(nt,),
        in_specs=[_tile_spec(a.shape) for a in tiles]
        + [_real_spec(D_MODEL), _whole_spec(meta_pad.shape), _tile_spec(dh1.shape), _whole_spec(g1.shape), _whole_spec(w_in.shape)],
        out_specs=[_tile_spec(out_shapes[0].shape), _real_spec(D_MODEL), _acc_spec(out_shapes[2].shape), _acc_spec(out_shapes[3].shape)],
        out_shape=out_shapes,
        compiler_params=_params("arbitrary"),
    )(*tiles, x, meta_pad, dh1, g1, w_in)


def _contraction_tile(n_rows):
    return next(t for t in range(n_rows // 2 // _LANES * _LANES, 0, -_LANES) if n_rows % t == 0)


def _weight_grad(a, b, name, a_transposed=False):
    groups = max(a.shape[0] if a.ndim == 3 else 1, b.shape[0] if b.ndim == 3 else 1)
    n_rows, n = b.shape[-2], b.shape[-1]
    m = a.shape[-2] if a_transposed else a.shape[-1]
    kt = _contraction_tile(n_rows)
    steps = n_rows // kt

    def body(a_ref, b_ref, out_ref, acc_ref):
        i = pl.program_id(1)
        a_t, b_t = a_ref[...].astype(BF16), b_ref[...].astype(BF16)
        part = _dot(a_t, b_t) if a_transposed else _dot_tn(a_t, b_t)
        _accumulate(acc_ref, i == 0, part)

        @pl.when(i == steps - 1)
        def _():
            out_ref[...] = acc_ref[...].astype(out_ref.dtype)

    def spec(arr, rows_last):
        block = (arr.shape[-2], kt) if rows_last else (kt, arr.shape[-1])
        at = (lambda i: (0, i)) if rows_last else (lambda i: (i, 0))
        if arr.ndim == 3:
            return pl.BlockSpec((None,) + block, lambda g, i: (g,) + at(i))
        return pl.BlockSpec(block, lambda g, i: at(i))

    return pl.pallas_call(
        body, name=name, grid=(groups, steps),
        in_specs=[spec(a, a_transposed), spec(b, False)],
        out_specs=pl.BlockSpec((None, m, n), lambda g, i: (g, 0, 0)),
        out_shape=jax.ShapeDtypeStruct((groups, m, n), BF16),
        scratch_shapes=[pltpu.VMEM((m, n), F32)],
        compiler_params=_params("parallel", "arbitrary"),
    )(a, b)


def _my_index():
    return 4 * lax.axis_index("x") + 2 * lax.axis_index("y") + lax.axis_index("c")


def _peer(k):
    flip = lambda v, bit: 1 - v if bit else v
    px = flip(lax.axis_index("x"), k & 4)
    py = flip(lax.axis_index("y"), k & 2)
    pc = flip(lax.axis_index("c"), k & 1)
    return (px, py, pc), 4 * px + 2 * py + pc


def _all_gather(shards, dtypes):
    n = len(shards)

    def body(*refs):
        ins, outs, stages = refs[:n], refs[n:2 * n], refs[2 * n:3 * n]
        send_sems, recv_sems, local_sems = refs[3 * n:]
        me = _my_index()
        for a in range(n):
            stages[a][...] = ins[a][...].astype(stages[a].dtype)
        local = [pltpu.make_async_copy(stages[a], outs[a].at[me], local_sems.at[a]) for a in range(n)]
        for cp in local:
            cp.start()

        def copy(a, k, slot):
            peer, _ = _peer(k)
            return pltpu.make_async_remote_copy(
                src_ref=stages[a], dst_ref=outs[a].at[slot], send_sem=send_sems.at[a, k - 1],
                recv_sem=recv_sems.at[a, k - 1], device_id=peer, device_id_type=MESH)

        for k in range(1, N_DEV):
            for a in range(n):
                copy(a, k, me).start()
        for k in range(1, N_DEV):
            for a in range(n):
                copy(a, k, _peer(k)[1]).wait()
        for cp in local:
            cp.wait()

    return pl.pallas_call(
        body, name="gather_weights",
        in_specs=[pl.BlockSpec(memory_space=pltpu.VMEM)] * n,
        out_specs=[pl.BlockSpec(memory_space=pl.ANY)] * n,
        out_shape=[jax.ShapeDtypeStruct((N_DEV,) + s.shape, dt) for s, dt in zip(shards, dtypes)],
        scratch_shapes=[pltpu.VMEM(s.shape, dt) for s, dt in zip(shards, dtypes)]
        + [pltpu.SemaphoreType.DMA((n, N_DEV - 1)), pltpu.SemaphoreType.DMA((n, N_DEV - 1)), pltpu.SemaphoreType.DMA((n,))],
        compiler_params=pltpu.CompilerParams(vmem_limit_bytes=VMEM_LIMIT),
    )(*shards)


def _exchange(parts, whole):
    n = len(parts)

    def body(*refs):
        ins, outs = refs[:n], refs[n:2 * n]
        send_sems, recv_sems, local_sems = refs[2 * n:]
        me = _my_index()

        def src(a, slab):
            return ins[a] if whole[a] else ins[a].at[slab]

        local = [pltpu.make_async_copy(src(a, me), outs[a].at[me], local_sems.at[a]) for a in range(n)]
        for cp in local:
            cp.start()

        def copy(a, k, slab, slot):
            peer, _ = _peer(k)
            return pltpu.make_async_remote_copy(
                src_ref=src(a, slab), dst_ref=outs[a].at[slot], send_sem=send_sems.at[a, k - 1],
                recv_sem=recv_sems.at[a, k - 1], device_id=peer, device_id_type=MESH)

        for k in range(1, N_DEV):
            for a in range(n):
                copy(a, k, _peer(k)[1], me).start()
        for k in range(1, N_DEV):
            for a in range(n):
                copy(a, k, _peer(k)[1], _peer(k)[1]).wait()
        for cp in local:
            cp.wait()

    return pl.pallas_call(
        body, name="exchange_grads",
        in_specs=[pl.BlockSpec(memory_space=pl.ANY)] * n,
        out_specs=[pl.BlockSpec(memory_space=pl.ANY)] * n,
        out_shape=[jax.ShapeDtypeStruct(((N_DEV,) + p.shape) if w else p.shape, p.dtype) for p, w in zip(parts, whole)],
        scratch_shapes=[pltpu.SemaphoreType.DMA((n, N_DEV - 1)), pltpu.SemaphoreType.DMA((n, N_DEV - 1)),
                        pltpu.SemaphoreType.DMA((n,))],
    )(*parts)


def _row_block(rows):
    if rows <= ROW_TILE:
        return rows
    return next(rb for rb in range(ROW_TILE, 0, -16) if rows % rb == 0)


def _adamw(landing, w, m, v, name):
    rows, cols = w.shape
    rb = _row_block(rows)

    def body(l_ref, w_ref, m_ref, v_ref, g_ref, d_ref, m2_ref, v2_ref):
        g = l_ref[0].astype(F32)
        for p in range(1, N_DEV):
            g = g + l_ref[p].astype(F32)
        m2 = ADAM_B1 * m_ref[...] + (1.0 - ADAM_B1) * g
        v2 = ADAM_B2 * v_ref[...] + (1.0 - ADAM_B2) * (g * g)
        m_hat = m2 / (1.0 - ADAM_B1 ** ADAM_STEP)
        v_hat = v2 / (1.0 - ADAM_B2 ** ADAM_STEP)
        g_ref[...] = g
        d_ref[...] = -ADAM_LR * (m_hat / (jnp.sqrt(v_hat) + ADAM_EPS) + ADAM_WD * w_ref[...])
        m2_ref[...] = m2
        v2_ref[...] = v2

    flat = pl.BlockSpec((rb, cols), lambda i: (i, 0))
    return pl.pallas_call(
        body, name=name, grid=(rows // rb,),
        in_specs=[pl.BlockSpec((N_DEV, rb, cols), lambda i: (0, i, 0)), flat, flat, flat],
        out_specs=[flat] * 4,
        out_shape=[jax.ShapeDtypeStruct((rows, cols), F32)] * 4,
        compiler_params=_params("parallel"),
    )(landing, w, m, v)


_REPLICATED = (
    ("mix_norm_g", D_MODEL), ("q_norm_g", Q_LORA), ("kv_norm_g", KV_LORA), ("conv_b", D_CONV), ("conv_ln_g", D_CONV),
    ("conv_ln_b", D_CONV), ("conv_out_g", D_CONV), ("attn_out_g", D_CONV), ("ffn_norm_g", D_MODEL),
    ("ffn_conv_b", D_UP), ("final_norm_g", D_MODEL),
)
_PACK_ROWS = 104

_WEIGHT_ORDER = (
    "meta_tokens", "mix_norm_g", "w_in", "q_norm_g", "w_uq", "kv_norm_g", "w_ukv", "conv_w", "conv_b", "conv_ln_g",
    "conv_ln_b", "conv_out_g", "attn_out_g", "w_out", "ffn_norm_g", "w_ffn_up", "ffn_conv_w", "ffn_conv_b",
    "w_ffn_down", "final_norm_g",
)


def _pack(vectors):
    flat = jnp.concatenate([vectors[name].reshape(-1) for name, _ in _REPLICATED])
    return jnp.pad(flat, (0, _PACK_ROWS * _LANES - flat.shape[0])).reshape(_PACK_ROWS, _LANES)


def _unpack(packed, like):
    flat, out, at = packed.reshape(-1), {}, 0
    for name, size in _REPLICATED:
        out[name] = flat[at:at + size].reshape(like[name].shape)
        at += size
    return out


def _pad_rows(a, rows):
    return jnp.pad(a, ((0, rows - a.shape[0]), (0, 0)))


def _slabs(a):
    r, c = a.shape
    return a.reshape(r, N_DEV, c // N_DEV).transpose(1, 0, 2)


def _unslab(a):
    g, r, c = a.shape
    return a.transpose(1, 0, 2).reshape(r, g * c)


def _local_step(x, target, w, n_rows):
    cos, sin = _rope_tables(n_rows)
    cos_t, sin_t = cos.T, sin.T
    meta_pad, g1, gf = w["meta_pad"], w["mix_norm_g"], w["final_norm_g"]
    gq, gkv, gb_col = w["q_norm_g"], w["kv_norm_g"], w["attn_out_g"].reshape(D_ATTN, 1)
    nb, ag, cq, ckv, kr = _fwd_in(x, meta_pad, g1, w["w_in"], n_rows)
    mix_a, u1 = _fwd_conv(ag, w["conv_w"], w["conv_b"], w["conv_ln_g"], w["conv_ln_b"], w["conv_out_g"], n_rows)
    q_t, k, v, v_t, cqn, ckvn = _fwd_qkv(cq, ckv, kr, gq, gkv, w["wq_t"], w["w_ukv"], w["wv_t"], cos, sin, cos_t, sin_t, n_rows)
    o_t, lse = _attn_fwd(q_t, k, v_t, n_rows)
    mix_bt, h1 = _fwd_out(x, meta_pad, mix_a, o_t, gb_col, w["w_out"], n_rows)
    n2, up0, act, dh2, loss, dgf = _fwd_ffn(h1, target, w["ffn_norm_g"], w["w_up"], w["fw"], w["fb"], w["w_down"], gf, n_rows)

    dup, dfb = _bwd_ffn_act(dh2, up0, w["w_down"], w["fw"], w["fb"], n_rows)
    dup0, dh1, dfw, dg2 = _bwd_ffn_up(dup, up0, h1, dh2, w["ffn_norm_g"], w["w_up"], w["fw"], n_rows)
    do_t, delta, du1, dgb, dga, dlg, dlb, dcb = _bwd_out(
        dh1, o_t, u1, w["w_out"], gb_col, w["conv_ln_g"], w["conv_ln_b"], w["conv_out_g"], n_rows)
    dq_t, dk, dv = _attn_bwd(q_t, k, v, do_t, lse, delta, n_rows)
    dqraw_t, dkv, dcq, dckv, dkr, dgq, dgkv = _bwd_qkv(
        dq_t, dk, dv, cq, ckv, gq, gkv, w["wq_t"], w["w_ukv"], cos, sin, cos_t, sin_t, n_rows)
    dag, dcw = _bwd_conv(du1, ag, w["conv_w"], n_rows)
    dz, gx, gmeta, dg1 = _bwd_in(dag, dcq, dckv, dkr, x, meta_pad, dh1, g1, w["w_in"], n_rows)

    grad_w_out = jnp.concatenate([_weight_grad(mix_a, dh1, "grad_w_out_conv")[0],
                                  _weight_grad(mix_bt, dh1, "grad_w_out_attn", a_transposed=True)[0]], axis=0)
    sharded = {
        "w_in": _slabs(_weight_grad(nb, dz, "grad_w_in")[0]),
        "w_uq": _weight_grad(dqraw_t, cqn, "grad_w_uq", a_transposed=True).transpose(0, 2, 1),
        "w_ukv": _weight_grad(ckvn, dkv, "grad_w_ukv"),
        "w_out": grad_w_out.reshape(N_DEV, D_MODEL // N_DEV, D_MODEL),
        "w_ffn_up": _weight_grad(n2, dup0, "grad_w_ffn_up"),
        "w_ffn_down": _weight_grad(act, dh2, "grad_w_ffn_down").reshape(N_DEV, D_FF // N_DEV, D_MODEL),
        "conv_w": _slabs(dcw),
        "ffn_conv_w": dfw,
        "meta_tokens": _slabs(gmeta[DEAD:]),
    }
    replicated = {
        "mix_norm_g": dg1, "q_norm_g": dgq, "kv_norm_g": dgkv, "conv_b": dcb, "conv_ln_g": dlg, "conv_ln_b": dlb,
        "conv_out_g": dga, "attn_out_g": dgb, "ffn_norm_g": dg2, "ffn_conv_b": dfb, "final_norm_g": dgf,
    }
    return loss[0, 0], gx, sharded, replicated


_SHARDED = (
    ("w_in", None, BF16), ("w_uq", None, BF16), ("w_ukv", None, BF16), ("w_out", None, BF16), ("w_ffn_up", None, BF16),
    ("w_ffn_down", None, BF16), ("conv_w", 32, F32), ("ffn_conv_w", 8, F32), ("meta_tokens", None, F32),
)


def kernel(x, meta_tokens, mix_norm_g, w_in, q_norm_g, w_uq, kv_norm_g, w_ukv, conv_w, conv_b, conv_ln_g, conv_ln_b, conv_out_g, attn_out_g, w_out, ffn_norm_g, w_ffn_up, ffn_conv_w, ffn_conv_b, w_ffn_down, final_norm_g, loss_target, m_meta_tokens, m_mix_norm_g, m_w_in, m_q_norm_g, m_w_uq, m_kv_norm_g, m_w_ukv, m_conv_w, m_conv_b, m_conv_ln_g, m_conv_ln_b, m_conv_out_g, m_attn_out_g, m_w_out, m_ffn_norm_g, m_w_ffn_up, m_ffn_conv_w, m_ffn_conv_b, m_w_ffn_down, m_final_norm_g, v_meta_tokens, v_mix_norm_g, v_w_in, v_q_norm_g, v_w_uq, v_kv_norm_g, v_w_ukv, v_conv_w, v_conv_b, v_conv_ln_g, v_conv_ln_b, v_conv_out_g, v_attn_out_g, v_w_out, v_ffn_norm_g, v_w_ffn_up, v_ffn_conv_w, v_ffn_conv_b, v_w_ffn_down, v_final_norm_g):
    given = dict(locals())
    weights = {name: given[name] for name in _WEIGHT_ORDER}
    moments_m = {name: given["m_" + name] for name in _WEIGHT_ORDER}
    moments_v = {name: given["v_" + name] for name in _WEIGHT_ORDER}
    seq = x.shape[1]
    n_rows = ROW_TILE + seq

    def shard2d(a):
        return a.reshape(a.shape[-2], a.shape[-1])

    shards = []
    for name, pad_to, _ in _SHARDED:
        s = shard2d(weights[name])
        shards.append(s if pad_to is None else _pad_rows(s, pad_to))
    gathered = dict(zip([name for name, _, _ in _SHARDED], _all_gather(shards, [dt for _, _, dt in _SHARDED])))
    meta_full = _unslab(gathered["meta_tokens"])
    full = {
        "meta_pad": jnp.concatenate([jnp.zeros((DEAD, D_MODEL), F32), meta_full], axis=0),
        "w_in": _unslab(gathered["w_in"]),
        "wq_t": gathered["w_uq"].transpose(0, 2, 1),
        "w_ukv": gathered["w_ukv"],
        "wv_t": gathered["w_ukv"][:, :, QK_NOPE:].transpose(0, 2, 1),
        "w_out": gathered["w_out"].reshape(D_MODEL, D_MODEL),
        "w_up": gathered["w_ffn_up"],
        "w_down": gathered["w_ffn_down"].reshape(N_ACT_SLAB, UP_SLAB, D_MODEL),
        "conv_w": _unslab(gathered["conv_w"][:, :CONV_WIDTH]),
        "fw": gathered["ffn_conv_w"][:, :FFN_CONV_WIDTH],
        "fb": ffn_conv_b.reshape(N_DEV, 1, UP_SLAB),
        "final_norm_g": final_norm_g.reshape(1, D_MODEL),
    }
    for name in ("mix_norm_g", "q_norm_g", "kv_norm_g", "conv_b", "conv_ln_g", "conv_ln_b", "conv_out_g", "attn_out_g",
                 "ffn_norm_g"):
        full[name] = weights[name]

    loss, gx, sharded, replicated = _local_step(x[0], loss_target[0], full, n_rows)
    loss = lax.psum(loss, ("x", "y", "c"))

    parts, whole = [], []
    for name, pad_to, dt in _SHARDED:
        p = sharded[name].astype(dt)
        parts.append(p if pad_to is None else jnp.pad(p, ((0, 0), (0, pad_to - p.shape[1]), (0, 0))))
        whole.append(False)
    parts.append(_pack(replicated))
    whole.append(True)
    landed = _exchange(parts, whole)

    grad, delta, new_m, new_v = {}, {}, {}, {}
    for (name, pad_to, _), land in zip(_SHARDED, landed[:-1]):
        ws, ms, vs = shard2d(weights[name]), shard2d(moments_m[name]), shard2d(moments_v[name])
        rows = ws.shape[0]
        if pad_to is not None:
            ws, ms, vs = _pad_rows(ws, pad_to), _pad_rows(ms, pad_to), _pad_rows(vs, pad_to)
        outs = _adamw(land, ws, ms, vs, "adamw_" + name)
        shape = weights[name].shape
        grad[name], delta[name], new_m[name], new_v[name] = (o[:rows].reshape(shape) for o in outs)
    outs = _adamw(landed[-1], _pack(weights), _pack(moments_m), _pack(moments_v), "adamw_replicated")
    for store, packed in zip((grad, delta, new_m, new_v), outs):
        store.update(_unpack(packed, weights))

    return (loss, gx[None], *[grad[n] for n in _WEIGHT_ORDER], *[delta[n] for n in _WEIGHT_ORDER],
            *[new_m[n] for n in _WEIGHT_ORDER], *[new_v[n] for n in _WEIGHT_ORDER])
```

```python
import functools

import jax
import jax.numpy as jnp
from jax import lax
from jax.experimental import pallas as pl
from jax.experimental.pallas import tpu as pltpu

F32 = jnp.float32
BF16 = jnp.bfloat16

N_DEV = 8
D_MODEL = 1024
CHUNK = 64
CHUNK_SHIFT = 6
N_META = 16
D_CONV = 512
CONV_WIDTH = 31
N_HEADS = 8
QK_NOPE = 64
QK_ROPE = 32
QK_DIM = QK_NOPE + QK_ROPE
V_HEAD = 64
KV_HEAD = QK_NOPE + V_HEAD
D_ATTN = N_HEADS * V_HEAD
Q_LORA = 384
KV_LORA = 256
ROPE_THETA = 10000.0
D_IN = 2 * D_CONV + Q_LORA + KV_LORA + QK_ROPE
D_FF = 2816
D_UP = 2 * D_FF
FFN_CONV_WIDTH = 3
UP_SLAB = D_UP // N_DEV
N_ACT_SLAB = D_FF // UP_SLAB
EPS = 1e-6
NEG = -1e30
ADAM_LR = 0.001
ADAM_B1 = 0.9
ADAM_B2 = 0.999
ADAM_EPS = 1e-08
ADAM_WD = 0.01
ADAM_STEP = 10

ROW_TILE = 256
DEAD = ROW_TILE - N_META
CONV_HALO = 32
FFN_HALO = 16
VMEM_LIMIT = 56 * 1024 * 1024
_LANES = 128

MESH = pl.DeviceIdType.MESH


def _dot(a, b):
    return jnp.dot(a, b, preferred_element_type=F32)


def _dot_nt(a, b):
    return lax.dot_general(a, b, (((1,), (1,)), ((), ())), preferred_element_type=F32)


def _dot_tn(a, b):
    return lax.dot_general(a, b, (((0,), (0,)), ((), ())), preferred_element_type=F32)


def _sigmoid(x):
    return 1.0 / (1.0 + jnp.exp(-x))


def _rms_fwd(x, g):
    r = lax.rsqrt(jnp.mean(x * x, axis=-1, keepdims=True) + EPS)
    return x * r * g


def _rms_bwd(dy, x, g):
    r = lax.rsqrt(jnp.mean(x * x, axis=-1, keepdims=True) + EPS)
    w = dy * g
    dx = r * w - x * (r * r * r) * jnp.mean(w * x, axis=-1, keepdims=True)
    return dx, jnp.sum(dy * x * r, axis=0, keepdims=True)


def _rope(x, cos, sin):
    half = QK_ROPE // 2
    x1, x2 = x[:, :half], x[:, half:]
    return jnp.concatenate([x1 * cos - x2 * sin, x2 * cos + x1 * sin], axis=-1)


def _rope_t(dy, cos, sin):
    half = QK_ROPE // 2
    d1, d2 = dy[:, :half], dy[:, half:]
    return jnp.concatenate([d1 * cos + d2 * sin, d2 * cos - d1 * sin], axis=-1)


def _row_ids(i, rows):
    return i * rows + lax.broadcasted_iota(jnp.int32, (rows, 1), 0)


def _accumulate(ref, first, value):
    @pl.when(first)
    def _():
        ref[...] = value

    @pl.when(jnp.logical_not(first))
    def _():
        ref[...] += value


def _tile_spec(shape):
    nd = len(shape)
    if nd == 2:
        return pl.BlockSpec((ROW_TILE, shape[1]), lambda i: (i, 0))
    return pl.BlockSpec((shape[0], ROW_TILE, shape[2]), lambda i: (0, i, 0))


def _whole_spec(shape):
    nd = len(shape)
    return pl.BlockSpec(tuple(shape), lambda i: (0,) * nd, pipeline_mode=pl.Buffered(1))


def _acc_spec(shape):
    nd = len(shape)
    return pl.BlockSpec(tuple(shape), lambda i: (0,) * nd)


def _real_spec(width):
    return pl.BlockSpec((ROW_TILE, width), lambda i: (jnp.maximum(i - 1, 0), 0))


def _params(*semantics):
    return pltpu.CompilerParams(dimension_semantics=semantics, vmem_limit_bytes=VMEM_LIMIT)


def _fwd_in(x, meta_pad, g1, w_in, n_rows):
    nt = n_rows // ROW_TILE

    def body(x_ref, meta_ref, g_ref, w_ref, nb_ref, ag_ref, cq_ref, ckv_ref, kr_ref):
        i = pl.program_id(0)
        h0 = jnp.where(i == 0, meta_ref[...], x_ref[...])
        nb = _rms_fwd(h0, g_ref[...]).astype(BF16)
        nb_ref[...] = nb
        z = _dot(nb, w_ref[...])
        ag_ref[...] = z[:, :2 * D_CONV]
        cq_ref[...] = z[:, 2 * D_CONV:2 * D_CONV + Q_LORA]
        ckv_ref[...] = z[:, 2 * D_CONV + Q_LORA:2 * D_CONV + Q_LORA + KV_LORA]
        kr_ref[...] = z[:, 2 * D_CONV + Q_LORA + KV_LORA:]

    out_shapes = [
        jax.ShapeDtypeStruct((n_rows, D_MODEL), BF16),
        jax.ShapeDtypeStruct((n_rows, 2 * D_CONV), F32),
        jax.ShapeDtypeStruct((n_rows, Q_LORA), F32),
        jax.ShapeDtypeStruct((n_rows, KV_LORA), F32),
        jax.ShapeDtypeStruct((n_rows, QK_ROPE), F32),
    ]
    return pl.pallas_call(
        body, name="fwd_in", grid=(nt,),
        in_specs=[_real_spec(D_MODEL), _whole_spec(meta_pad.shape), _whole_spec(g1.shape), _whole_spec(w_in.shape)],
        out_specs=[_tile_spec(s.shape) for s in out_shapes],
        out_shape=out_shapes,
        compiler_params=_params("parallel"),
    )(x, meta_pad, g1, w_in)


def _conv_chain(u1, ln_g, ln_b):
    mu = jnp.mean(u1, axis=-1, keepdims=True)
    xc = u1 - mu
    rstd = lax.rsqrt(jnp.mean(xc * xc, axis=-1, keepdims=True) + EPS)
    xh = xc * rstd
    u2 = xh * ln_g + ln_b
    return xh, u2, u2 * _sigmoid(u2), rstd


def _fwd_conv(ag, conv_w, conv_b, ln_g, ln_b, out_g, n_rows):
    nt = n_rows // ROW_TILE

    def body(ag_ref, w_ref, b_ref, lg_ref, lb_ref, og_ref, mix_ref, u1_ref, ext_ref):
        i = pl.program_id(0)

        @pl.when(i == 0)
        def _():
            ext_ref[0:CONV_HALO, :] = jnp.zeros((CONV_HALO, D_CONV), F32)

        ag_t = ag_ref[...]
        live = _row_ids(i, ROW_TILE) >= DEAD
        u0 = jnp.where(live, ag_t[:, :D_CONV] * _sigmoid(ag_t[:, D_CONV:]), 0.0)
        ext_ref[CONV_HALO:, :] = u0
        first = CONV_HALO - (CONV_WIDTH - 1)
        acc = jnp.zeros((ROW_TILE, D_CONV), F32)
        for k in range(CONV_WIDTH):
            acc = acc + w_ref[k:k + 1, :] * ext_ref[first + k:first + k + ROW_TILE, :]
        u1 = acc + b_ref[...]
        ext_ref[0:CONV_HALO, :] = ext_ref[ROW_TILE:ROW_TILE + CONV_HALO, :]
        u1_ref[...] = u1
        _, _, u3, _ = _conv_chain(u1, lg_ref[...], lb_ref[...])
        mix_ref[...] = _rms_fwd(u3, og_ref[...]).astype(BF16)

    out_shapes = [jax.ShapeDtypeStruct((n_rows, D_CONV), BF16), jax.ShapeDtypeStruct((n_rows, D_CONV), F32)]
    small = [conv_w, conv_b, ln_g, ln_b, out_g]
    return pl.pallas_call(
        body, name="fwd_conv", grid=(nt,),
        in_specs=[_tile_spec(ag.shape)] + [_whole_spec(a.shape) for a in small],
        out_specs=[_tile_spec(s.shape) for s in out_shapes],
        out_shape=out_shapes,
        scratch_shapes=[pltpu.VMEM((ROW_TILE + CONV_HALO, D_CONV), F32)],
        compiler_params=_params("arbitrary"),
    )(ag, *small)


def _lane_tile(shape):
    if len(shape) == 2:
        return pl.BlockSpec((shape[0], ROW_TILE), lambda i: (0, i))
    return pl.BlockSpec((shape[0], shape[1], ROW_TILE), lambda i: (0, 0, i))


def _rope_rows(x, cos, sin):
    half = QK_ROPE // 2
    x1, x2 = x[:half], x[half:]
    return jnp.concatenate([x1 * cos - x2 * sin, x2 * cos + x1 * sin], axis=0)


def _rope_rows_t(dy, cos, sin):
    half = QK_ROPE // 2
    d1, d2 = dy[:half], dy[half:]
    return jnp.concatenate([d1 * cos + d2 * sin, d2 * cos - d1 * sin], axis=0)


def _fwd_qkv(cq, ckv, kr, gq, gkv, wq_t, w_ukv, wv_t, cos, sin, cos_t, sin_t, n_rows):
    nt = n_rows // ROW_TILE

    def body(cq_ref, ckv_ref, kr_ref, gq_ref, gkv_ref, wqt_ref, wkv_ref, wvt_ref, cos_ref, sin_ref, cost_ref, sint_ref,
             qt_ref, k_ref, v_ref, vt_ref, cqn_ref, ckvn_ref):
        cqn = _rms_fwd(cq_ref[...], gq_ref[...]).astype(BF16)
        ckvn = _rms_fwd(ckv_ref[...], gkv_ref[...]).astype(BF16)
        cqn_ref[...] = cqn
        ckvn_ref[...] = ckvn
        k_rot = _rope(kr_ref[...], cos_ref[...], sin_ref[...])
        cos_rows, sin_rows = cost_ref[...], sint_ref[...]
        for h in range(N_HEADS):
            q_raw = _dot_nt(wqt_ref[h], cqn)
            qt_ref[h] = jnp.concatenate(
                [q_raw[:QK_NOPE], _rope_rows(q_raw[QK_NOPE:], cos_rows, sin_rows)], axis=0).astype(BF16)
            kv = _dot(ckvn, wkv_ref[h])
            k_ref[h] = jnp.concatenate([kv[:, :QK_NOPE], k_rot], axis=-1).astype(BF16)
            v_ref[h] = kv[:, QK_NOPE:].astype(BF16)
            vt_ref[h] = _dot_nt(wvt_ref[h], ckvn).astype(BF16)

    out_shapes = [
        jax.ShapeDtypeStruct((N_HEADS, QK_DIM, n_rows), BF16),
        jax.ShapeDtypeStruct((N_HEADS, n_rows, QK_DIM), BF16),
        jax.ShapeDtypeStruct((N_HEADS, n_rows, V_HEAD), BF16),
        jax.ShapeDtypeStruct((N_HEADS, V_HEAD, n_rows), BF16),
        jax.ShapeDtypeStruct((n_rows, Q_LORA), BF16),
        jax.ShapeDtypeStruct((n_rows, KV_LORA), BF16),
    ]
    tiles = [cq, ckv, kr]
    whole = [gq, gkv, wq_t, w_ukv, wv_t]
    out_specs = [_lane_tile(out_shapes[0].shape), _tile_spec(out_shapes[1].shape), _tile_spec(out_shapes[2].shape),
                 _lane_tile(out_shapes[3].shape), _tile_spec(out_shapes[4].shape), _tile_spec(out_shapes[5].shape)]
    return pl.pallas_call(
        body, name="fwd_qkv", grid=(nt,),
        in_specs=[_tile_spec(a.shape) for a in tiles] + [_whole_spec(a.shape) for a in whole]
        + [_tile_spec(cos.shape), _tile_spec(sin.shape), _lane_tile(cos_t.shape), _lane_tile(sin_t.shape)],
        out_specs=out_specs,
        out_shape=out_shapes,
        compiler_params=_params("parallel"),
    )(*tiles, *whole, cos, sin, cos_t, sin_t)


def _chunk_of(rows):
    return jnp.where(rows >= ROW_TILE, lax.shift_right_arithmetic(rows - ROW_TILE, CHUNK_SHIFT) + 1, 0)


def _visible(i, j):
    k_rows = j * ROW_TILE + lax.broadcasted_iota(jnp.int32, (ROW_TILE, 1), 0)
    q_rows = i * ROW_TILE + lax.broadcasted_iota(jnp.int32, (1, ROW_TILE), 1)
    return jnp.logical_and(_chunk_of(q_rows) >= _chunk_of(k_rows), k_rows >= DEAD)


def _attn_fwd(q_t, k, v_t, n_rows):
    nt = n_rows // ROW_TILE
    scale = QK_DIM ** -0.5

    def body(qt_ref, k_ref, vt_ref, ot_ref, lse_ref):
        i = pl.program_id(0)
        q_ts = [qt_ref[h] for h in range(N_HEADS)]

        def make_step(masked):
            def step(j, carry):
                rows = pl.ds(pl.multiple_of(j * ROW_TILE, ROW_TILE), ROW_TILE)
                scores = [_dot(k_ref[h, rows, :], q_ts[h]) for h in range(N_HEADS)]
                visible = _visible(i, j) if masked else None
                probs, state = [], []
                for h in range(N_HEADS):
                    m, l, _ = carry[h]
                    s = scores[h] * scale
                    if masked:
                        s = jnp.where(visible, s, NEG)
                    m_new = jnp.maximum(m, jnp.max(s, axis=0, keepdims=True))
                    alpha = jnp.exp(m - m_new)
                    p = jnp.exp(s - m_new)
                    probs.append(p.astype(BF16))
                    state.append((m_new, alpha * l + jnp.sum(p, axis=0, keepdims=True), alpha))
                outs = [_dot(vt_ref[h, :, rows], probs[h]) for h in range(N_HEADS)]
                return tuple((state[h][0], state[h][1], state[h][2] * carry[h][2] + outs[h]) for h in range(N_HEADS))
            return step

        init = tuple((jnp.full((1, ROW_TILE), NEG, F32), jnp.zeros((1, ROW_TILE), F32),
                      jnp.zeros((V_HEAD, ROW_TILE), F32)) for _ in range(N_HEADS))
        carry = make_step(True)(0, init)
        carry = lax.fori_loop(1, i, make_step(False), carry)
        carry = lax.fori_loop(jnp.maximum(i, 1), i + 1, make_step(True), carry)
        for h in range(N_HEADS):
            m, l, acc = carry[h]
            ot_ref[h] = acc / l
            lse_ref[h] = m + jnp.log(l)

    out_shapes = [jax.ShapeDtypeStruct((N_HEADS, V_HEAD, n_rows), F32), jax.ShapeDtypeStruct((N_HEADS, 1, n_rows), F32)]
    return pl.pallas_call(
        body, name="attn_fwd", grid=(nt,),
        in_specs=[_lane_tile(q_t.shape), _whole_spec(k.shape), _whole_spec(v_t.shape)],
        out_specs=[_lane_tile(s.shape) for s in out_shapes],
        out_shape=out_shapes,
        compiler_params=_params("parallel"),
    )(q_t, k, v_t)


def _heads_to_rows(ref):
    return jnp.concatenate([ref[h] for h in range(N_HEADS)], axis=0)


def _rms_cols(x, g_col):
    r = lax.rsqrt(jnp.mean(x * x, axis=0, keepdims=True) + EPS)
    return x * r * g_col


def _fwd_out(x, meta_pad, mix_a, o_t, gb_col, w_out, n_rows):
    nt = n_rows // ROW_TILE

    def body(x_ref, meta_ref, mixa_ref, ot_ref, gb_ref, w_ref, mixbt_ref, h1_ref):
        i = pl.program_id(0)
        h0 = jnp.where(i == 0, meta_ref[...], x_ref[...])
        mix_bt = _rms_cols(_heads_to_rows(ot_ref), gb_ref[...]).astype(BF16)
        mixbt_ref[...] = mix_bt
        h1_ref[...] = h0 + _dot(mixa_ref[...], w_ref[:D_CONV, :]) + _dot_tn(mix_bt, w_ref[D_CONV:, :])

    out_shapes = [jax.ShapeDtypeStruct((D_ATTN, n_rows), BF16), jax.ShapeDtypeStruct((n_rows, D_MODEL), F32)]
    return pl.pallas_call(
        body, name="fwd_out", grid=(nt,),
        in_specs=[_real_spec(D_MODEL), _whole_spec(meta_pad.shape), _tile_spec(mix_a.shape), _lane_tile(o_t.shape),
                  _whole_spec(gb_col.shape), _whole_spec(w_out.shape)],
        out_specs=[_lane_tile(out_shapes[0].shape), _tile_spec(out_shapes[1].shape)],
        out_shape=out_shapes,
        compiler_params=_params("parallel"),
    )(x, meta_pad, mix_a, o_t, gb_col, w_out)


def _ffn_conv(ext_ref, w_ref, b_ref, s):
    first = FFN_HALO - (FFN_CONV_WIDTH - 1)
    acc = b_ref[s]
    for k in range(FFN_CONV_WIDTH):
        acc = acc + w_ref[s, k:k + 1, :] * ext_ref[s, first + k:first + k + ROW_TILE, :]
    return acc


def _fwd_ffn(h1, target, g2, w_up, fw, fb, w_down, gf, n_rows):
    nt = n_rows // ROW_TILE

    def body(h1_ref, t_ref, g2_ref, wup_ref, fw_ref, fb_ref, wdn_ref, gf_ref,
             n2_ref, up0_ref, act_ref, dh2_ref, loss_ref, dgf_ref, ext_ref):
        i = pl.program_id(0)

        @pl.when(i == 0)
        def _():
            ext_ref[:, 0:FFN_HALO, :] = jnp.zeros((N_DEV, FFN_HALO, UP_SLAB), F32)

        h1_t = h1_ref[...]
        n2 = _rms_fwd(h1_t, g2_ref[...]).astype(BF16)
        n2_ref[...] = n2
        live = _row_ids(i, ROW_TILE) >= DEAD
        for s in range(N_DEV):
            up0 = jnp.where(live, _dot(n2, wup_ref[s]), 0.0).astype(BF16)
            up0_ref[s] = up0
            ext_ref[s, FFN_HALO:, :] = up0.astype(F32)
        h2 = h1_t
        for s in range(N_ACT_SLAB):
            gate = _ffn_conv(ext_ref, fw_ref, fb_ref, s)
            val = _ffn_conv(ext_ref, fw_ref, fb_ref, s + N_ACT_SLAB)
            act = (gate * _sigmoid(gate) * val).astype(BF16)
            act_ref[s] = act
            h2 = h2 + _dot(act, wdn_ref[s])
        ext_ref[:, 0:FFN_HALO, :] = ext_ref[:, ROW_TILE:ROW_TILE + FFN_HALO, :]

        gf_t = gf_ref[...]
        y = _rms_fwd(h2, gf_t)
        diff = jnp.where(i >= 1, y - t_ref[...], 0.0)
        tile_loss = 0.5 * jnp.sum(jnp.sum(diff * diff, axis=-1, keepdims=True), axis=0, keepdims=True) / D_MODEL
        dh2, dgf = _rms_bwd(diff / D_MODEL, h2, gf_t)
        dh2_ref[...] = dh2
        _accumulate(loss_ref, i == 0, jnp.broadcast_to(tile_loss, loss_ref.shape))
        _accumulate(dgf_ref, i == 0, dgf)

    out_shapes = [
        jax.ShapeDtypeStruct((n_rows, D_MODEL), BF16),
        jax.ShapeDtypeStruct((N_DEV, n_rows, UP_SLAB), BF16),
        jax.ShapeDtypeStruct((N_ACT_SLAB, n_rows, UP_SLAB), BF16),
        jax.ShapeDtypeStruct((n_rows, D_MODEL), F32),
        jax.ShapeDtypeStruct((8, 128), F32),
        jax.ShapeDtypeStruct((1, D_MODEL), F32),
    ]
    whole = [g2, w_up, fw, fb, w_down, gf]
    return pl.pallas_call(
        body, name="fwd_ffn", grid=(nt,),
        in_specs=[_tile_spec(h1.shape), _real_spec(D_MODEL)] + [_whole_spec(a.shape) for a in whole],
        out_specs=[_tile_spec(s.shape) for s in out_shapes[:4]] + [_acc_spec(s.shape) for s in out_shapes[4:]],
        out_shape=out_shapes,
        scratch_shapes=[pltpu.VMEM((N_DEV, ROW_TILE + FFN_HALO, UP_SLAB), F32)],
        compiler_params=_params("arbitrary"),
    )(h1, target, *whole)


def _rope_tables(n_rows):
    pos = jnp.maximum(jnp.arange(n_rows, dtype=jnp.int32) - DEAD, 0)
    inv_freq = 1.0 / (ROPE_THETA ** (jnp.arange(0, QK_ROPE, 2, dtype=F32) / QK_ROPE))
    ang = pos.astype(F32)[:, None] * inv_freq[None, :]
    return jnp.cos(ang), jnp.sin(ang)


def _halo_after(shape, halo, n_rows):
    last = n_rows // halo - 1
    step = ROW_TILE // halo
    if len(shape) == 2:
        return pl.BlockSpec((halo, shape[1]), lambda i: (jnp.minimum((i + 1) * step, last), 0))
    return pl.BlockSpec((shape[0], halo, shape[2]), lambda i: (0, jnp.minimum((i + 1) * step, last), 0))


def _halo_before(shape, halo):
    step = ROW_TILE // halo
    if len(shape) == 2:
        return pl.BlockSpec((halo, shape[1]), lambda i: (jnp.maximum(i * step - 1, 0), 0))
    return pl.BlockSpec((shape[0], halo, shape[2]), lambda i: (0, jnp.maximum(i * step - 1, 0), 0))


def _bwd_ffn_act(dh2, up0, w_down, fw, fb, n_rows):
    nt = n_rows // ROW_TILE

    def body(dh2_ref, up0_ref, wdn_ref, fw_ref, fb_ref, dup_ref, dfb_ref, ext_ref):
        i = pl.program_id(0)

        @pl.when(i == 0)
        def _():
            ext_ref[:, 0:FFN_HALO, :] = jnp.zeros((N_DEV, FFN_HALO, UP_SLAB), F32)
            dfb_ref[...] = jnp.zeros_like(dfb_ref)

        for s in range(N_DEV):
            ext_ref[s, FFN_HALO:, :] = up0_ref[s].astype(F32)
        dh2_b = dh2_ref[...].astype(BF16)
        for s in range(N_ACT_SLAB):
            gate = _ffn_conv(ext_ref, fw_ref, fb_ref, s)
            val = _ffn_conv(ext_ref, fw_ref, fb_ref, s + N_ACT_SLAB)
            d_act = _dot_nt(dh2_b, wdn_ref[s])
            sg = _sigmoid(gate)
            d_gate = d_act * val * sg * (1.0 + gate * (1.0 - sg))
            d_val = d_act * gate * sg
            dup_ref[s] = d_gate.astype(BF16)
            dup_ref[s + N_ACT_SLAB] = d_val.astype(BF16)
            dfb_ref[s] += jnp.sum(d_gate, axis=0, keepdims=True)
            dfb_ref[s + N_ACT_SLAB] += jnp.sum(d_val, axis=0, keepdims=True)
        ext_ref[:, 0:FFN_HALO, :] = ext_ref[:, ROW_TILE:ROW_TILE + FFN_HALO, :]

    out_shapes = [jax.ShapeDtypeStruct((N_DEV, n_rows, UP_SLAB), BF16), jax.ShapeDtypeStruct((N_DEV, 1, UP_SLAB), F32)]
    whole = [w_down, fw, fb]
    return pl.pallas_call(
        body, name="bwd_ffn_act", grid=(nt,),
        in_specs=[_tile_spec(dh2.shape), _tile_spec(up0.shape)] + [_whole_spec(a.shape) for a in whole],
        out_specs=[_tile_spec(out_shapes[0].shape), _acc_spec(out_shapes[1].shape)],
        out_shape=out_shapes,
        scratch_shapes=[pltpu.VMEM((N_DEV, ROW_TILE + FFN_HALO, UP_SLAB), F32)],
        compiler_params=_params("arbitrary"),
    )(dh2, up0, *whole)


def _bwd_ffn_up(dup, up0, h1, dh2, g2, w_up, fw, n_rows):
    nt = n_rows // ROW_TILE

    def body(dup_ref, dnext_ref, up0_ref, uprev_ref, h1_ref, dh2_ref, g2_ref, wup_ref, fw_ref,
             dup0_ref, dh1_ref, dfw_ref, dg2_ref, dext_ref, uext_ref):
        i = pl.program_id(0)

        @pl.when(i == 0)
        def _():
            dfw_ref[...] = jnp.zeros_like(dfw_ref)

        live = _row_ids(i, ROW_TILE) >= DEAD
        dn2 = jnp.zeros((ROW_TILE, D_MODEL), F32)
        for s in range(N_DEV):
            d = dup_ref[s].astype(F32)
            dext_ref[0:ROW_TILE, :] = d
            dext_ref[ROW_TILE:, :] = jnp.where(i == nt - 1, 0.0, dnext_ref[s].astype(F32))
            uext_ref[0:FFN_HALO, :] = jnp.where(i == 0, 0.0, uprev_ref[s].astype(F32))
            uext_ref[FFN_HALO:, :] = up0_ref[s].astype(F32)
            dup0 = jnp.zeros((ROW_TILE, UP_SLAB), F32)
            for k in range(FFN_CONV_WIDTH):
                back = FFN_CONV_WIDTH - 1 - k
                dup0 = dup0 + fw_ref[s, k:k + 1, :] * dext_ref[back:back + ROW_TILE, :]
                first = FFN_HALO - back
                dfw_ref[s, k:k + 1, :] += jnp.sum(d * uext_ref[first:first + ROW_TILE, :], axis=0, keepdims=True)
            dup0_b = jnp.where(live, dup0, 0.0).astype(BF16)
            dup0_ref[s] = dup0_b
            dn2 = dn2 + _dot_nt(dup0_b, wup_ref[s])
        dx, dg2 = _rms_bwd(dn2, h1_ref[...], g2_ref[...])
        dh1_ref[...] = dh2_ref[...] + dx
        _accumulate(dg2_ref, i == 0, dg2)

    out_shapes = [
        jax.ShapeDtypeStruct((N_DEV, n_rows, UP_SLAB), BF16),
        jax.ShapeDtypeStruct((n_rows, D_MODEL), F32),
        jax.ShapeDtypeStruct((N_DEV, FFN_CONV_WIDTH, UP_SLAB), F32),
        jax.ShapeDtypeStruct((1, D_MODEL), F32),
    ]
    return pl.pallas_call(
        body, name="bwd_ffn_up", grid=(nt,),
        in_specs=[_tile_spec(dup.shape), _halo_after(dup.shape, FFN_HALO, n_rows), _tile_spec(up0.shape),
                  _halo_before(up0.shape, FFN_HALO), _tile_spec(h1.shape), _tile_spec(dh2.shape),
                  _whole_spec(g2.shape), _whole_spec(w_up.shape), _whole_spec(fw.shape)],
        out_specs=[_tile_spec(s.shape) for s in out_shapes[:2]] + [_acc_spec(s.shape) for s in out_shapes[2:]],
        out_shape=out_shapes,
        scratch_shapes=[pltpu.VMEM((ROW_TILE + FFN_HALO, UP_SLAB), F32), pltpu.VMEM((ROW_TILE + FFN_HALO, UP_SLAB), F32)],
        compiler_params=_params("arbitrary"),
    )(dup, dup, up0, up0, h1, dh2, g2, w_up, fw)


def _bwd_out(dh1, o_t, u1, w_out, gb_col, ln_g, ln_b, ga, n_rows):
    nt = n_rows // ROW_TILE

    def body(dh1_ref, ot_ref, u1_ref, w_ref, gb_ref, lg_ref, lb_ref, ga_ref,
             dot_ref, delta_ref, du1_ref, dgb_ref, dga_ref, dlg_ref, dlb_ref, dcb_ref):
        i = pl.program_id(0)
        dh1_b = dh1_ref[...].astype(BF16)
        o_t = _heads_to_rows(ot_ref)
        gb = gb_ref[...]
        r = lax.rsqrt(jnp.mean(o_t * o_t, axis=0, keepdims=True) + EPS)
        dmix_bt = _dot_nt(w_ref[D_CONV:, :], dh1_b)
        wgt = dmix_bt * gb
        do_t = r * wgt - o_t * (r * r * r) * jnp.mean(wgt * o_t, axis=0, keepdims=True)
        dgb = jnp.sum(dmix_bt * o_t * r, axis=1, keepdims=True)
        for h in range(N_HEADS):
            do_h = do_t[h * V_HEAD:(h + 1) * V_HEAD]
            dot_ref[h] = do_h.astype(BF16)
            delta_ref[h] = jnp.sum(do_h * ot_ref[h], axis=0, keepdims=True)
        lg = lg_ref[...]
        xh, u2, u3, rstd = _conv_chain(u1_ref[...], lg, lb_ref[...])
        du3, dga = _rms_bwd(_dot_nt(dh1_b, w_ref[:D_CONV, :]), u3, ga_ref[...])
        sg = _sigmoid(u2)
        du2 = du3 * sg * (1.0 + u2 * (1.0 - sg))
        dxh = du2 * lg
        du1 = rstd * (dxh - jnp.mean(dxh, axis=-1, keepdims=True) - xh * jnp.mean(dxh * xh, axis=-1, keepdims=True))
        du1_ref[...] = du1
        first = i == 0
        _accumulate(dgb_ref, first, dgb)
        _accumulate(dga_ref, first, dga)
        _accumulate(dlg_ref, first, jnp.sum(du2 * xh, axis=0, keepdims=True))
        _accumulate(dlb_ref, first, jnp.sum(du2, axis=0, keepdims=True))
        _accumulate(dcb_ref, first, jnp.sum(du1, axis=0, keepdims=True))

    out_shapes = [
        jax.ShapeDtypeStruct((N_HEADS, V_HEAD, n_rows), BF16),
        jax.ShapeDtypeStruct((N_HEADS, 1, n_rows), F32),
        jax.ShapeDtypeStruct((n_rows, D_CONV), F32),
        jax.ShapeDtypeStruct((D_ATTN, 1), F32),
    ] + [jax.ShapeDtypeStruct((1, D_CONV), F32)] * 4
    whole = [w_out, gb_col, ln_g, ln_b, ga]
    return pl.pallas_call(
        body, name="bwd_out", grid=(nt,),
        in_specs=[_tile_spec(dh1.shape), _lane_tile(o_t.shape), _tile_spec(u1.shape)] + [_whole_spec(a.shape) for a in whole],
        out_specs=[_lane_tile(out_shapes[0].shape), _lane_tile(out_shapes[1].shape), _tile_spec(out_shapes[2].shape)]
        + [_acc_spec(s.shape) for s in out_shapes[3:]],
        out_shape=out_shapes,
        compiler_params=_params("arbitrary"),
    )(dh1, o_t, u1, *whole)


ATTN_BWD_HEADS = 4


def _attn_bwd(q_t, k, v, do_t, lse, delta, n_rows):
    nt = n_rows // ROW_TILE
    scale = QK_DIM ** -0.5
    hp = ATTN_BWD_HEADS

    def body(k_ref, v_ref, qt_ref, dot_ref, lse_ref, delta_ref, dqt_ref, dk_ref, dv_ref):
        j = pl.program_id(1)

        @pl.when(j == 0)
        def _():
            dqt_ref[...] = jnp.zeros_like(dqt_ref)

        k_ts = [k_ref[h] for h in range(hp)]
        v_ts = [v_ref[h] for h in range(hp)]

        def make_step(masked):
            def step(i, carry):
                cols = pl.ds(pl.multiple_of(i * ROW_TILE, ROW_TILE), ROW_TILE)
                q_is = [qt_ref[h, :, cols] for h in range(hp)]
                do_is = [dot_ref[h, :, cols] for h in range(hp)]
                scores = [_dot(k_ts[h], q_is[h]) for h in range(hp)]
                dps = [_dot(v_ts[h], do_is[h]) for h in range(hp)]
                visible = _visible(i, j) if masked else None
                probs, dss = [], []
                for h in range(hp):
                    s = scores[h] * scale
                    if masked:
                        s = jnp.where(visible, s, NEG)
                    p = jnp.exp(s - lse_ref[h, :, cols])
                    probs.append(p.astype(BF16))
                    dss.append((p * (dps[h] - delta_ref[h, :, cols]) * scale).astype(BF16))
                out = []
                for h in range(hp):
                    dk, dv = carry[h]
                    dv = dv + _dot_nt(probs[h], do_is[h])
                    dk = dk + _dot_nt(dss[h], q_is[h])
                    dqt_ref[h, :, cols] += _dot_tn(k_ts[h], dss[h])
                    out.append((dk, dv))
                return tuple(out)
            return step

        init = tuple((jnp.zeros((ROW_TILE, QK_DIM), F32), jnp.zeros((ROW_TILE, V_HEAD), F32)) for _ in range(hp))
        carry = make_step(True)(j, init)
        carry = lax.fori_loop(jnp.where(j == 0, j + 1, nt), nt, make_step(True), carry)
        carry = lax.fori_loop(jnp.where(j == 0, nt, j + 1), nt, make_step(False), carry)
        for h in range(hp):
            dk_ref[h], dv_ref[h] = carry[h]

    key_tile = lambda w: pl.BlockSpec((hp, ROW_TILE, w), lambda g, j: (g, j, 0))
    all_cols = lambda w: pl.BlockSpec((hp, w, n_rows), lambda g, j: (g, 0, 0))
    out_shapes = [
        jax.ShapeDtypeStruct((N_HEADS, QK_DIM, n_rows), F32),
        jax.ShapeDtypeStruct((N_HEADS, n_rows, QK_DIM), F32),
        jax.ShapeDtypeStruct((N_HEADS, n_rows, V_HEAD), F32),
    ]
    return pl.pallas_call(
        body, name="attn_bwd", grid=(N_HEADS // hp, nt),
        in_specs=[key_tile(QK_DIM), key_tile(V_HEAD), all_cols(QK_DIM), all_cols(V_HEAD), all_cols(1), all_cols(1)],
        out_specs=[all_cols(QK_DIM), key_tile(QK_DIM), key_tile(V_HEAD)],
        out_shape=out_shapes,
        compiler_params=_params("parallel", "arbitrary"),
    )(k, v, q_t, do_t, lse, delta)


def _bwd_qkv(dq_t, dk, dv, cq, ckv, gq, gkv, wq_t, w_ukv, cos, sin, cos_t, sin_t, n_rows):
    nt = n_rows // ROW_TILE

    def body(dqt_ref, dk_ref, dv_ref, cq_ref, ckv_ref, gq_ref, gkv_ref, wqt_ref, wkv_ref, cos_ref, sin_ref,
             cost_ref, sint_ref, dqraw_ref, dkv_ref, dcq_ref, dckv_ref, dkr_ref, dgq_ref, dgkv_ref):
        i = pl.program_id(0)
        cos_rows, sin_rows = cost_ref[...], sint_ref[...]
        dcqn = jnp.zeros((ROW_TILE, Q_LORA), F32)
        dckvn = jnp.zeros((ROW_TILE, KV_LORA), F32)
        dk_rot = jnp.zeros((ROW_TILE, QK_ROPE), F32)
        for h in range(N_HEADS):
            dq_h, dk_h = dqt_ref[h], dk_ref[h]
            dq_raw = jnp.concatenate(
                [dq_h[:QK_NOPE], _rope_rows_t(dq_h[QK_NOPE:], cos_rows, sin_rows)], axis=0).astype(BF16)
            dqraw_ref[h] = dq_raw
            dcqn = dcqn + _dot_tn(dq_raw, wqt_ref[h])
            dkv = jnp.concatenate([dk_h[:, :QK_NOPE], dv_ref[h]], axis=-1).astype(BF16)
            dkv_ref[h] = dkv
            dckvn = dckvn + _dot_nt(dkv, wkv_ref[h])
            dk_rot = dk_rot + dk_h[:, QK_NOPE:]
        dkr_ref[...] = _rope_t(dk_rot, cos_ref[...], sin_ref[...]).astype(BF16)
        dcq, dgq = _rms_bwd(dcqn, cq_ref[...], gq_ref[...])
        dckv, dgkv = _rms_bwd(dckvn, ckv_ref[...], gkv_ref[...])
        dcq_ref[...] = dcq.astype(BF16)
        dckv_ref[...] = dckv.astype(BF16)
        _accumulate(dgq_ref, i == 0, dgq)
        _accumulate(dgkv_ref, i == 0, dgkv)

    out_shapes = [
        jax.ShapeDtypeStruct((N_HEADS, QK_DIM, n_rows), BF16),
        jax.ShapeDtypeStruct((N_HEADS, n_rows, KV_HEAD), BF16),
        jax.ShapeDtypeStruct((n_rows, Q_LORA), BF16),
        jax.ShapeDtypeStruct((n_rows, KV_LORA), BF16),
        jax.ShapeDtypeStruct((n_rows, QK_ROPE), BF16),
        jax.ShapeDtypeStruct((1, Q_LORA), F32),
        jax.ShapeDtypeStruct((1, KV_LORA), F32),
    ]
    tiles = [dk, dv, cq, ckv]
    whole = [gq, gkv, wq_t, w_ukv]
    return pl.pallas_call(
        body, name="bwd_qkv", grid=(nt,),
        in_specs=[_lane_tile(dq_t.shape)] + [_tile_spec(a.shape) for a in tiles] + [_whole_spec(a.shape) for a in whole]
        + [_tile_spec(cos.shape), _tile_spec(sin.shape), _lane_tile(cos_t.shape), _lane_tile(sin_t.shape)],
        out_specs=[_lane_tile(out_shapes[0].shape)] + [_tile_spec(s.shape) for s in out_shapes[1:5]]
        + [_acc_spec(s.shape) for s in out_shapes[5:]],
        out_shape=out_shapes,
        compiler_params=_params("arbitrary"),
    )(dq_t, *tiles, *whole, cos, sin, cos_t, sin_t)


def _bwd_conv(du1, ag, conv_w, n_rows):
    nt = n_rows // ROW_TILE

    def glu(ag_t, rows):
        sg = _sigmoid(ag_t[:, D_CONV:])
        return jnp.where(rows >= DEAD, ag_t[:, :D_CONV] * sg, 0.0), sg

    def body(du1_ref, dnext_ref, ag_ref, agprev_ref, w_ref, dag_ref, dw_ref, dext_ref, uext_ref):
        i = pl.program_id(0)

        @pl.when(i == 0)
        def _():
            dw_ref[...] = jnp.zeros_like(dw_ref)

        du1_t = du1_ref[...]
        dext_ref[0:ROW_TILE, :] = du1_t
        dext_ref[ROW_TILE:, :] = jnp.where(i == nt - 1, 0.0, dnext_ref[...])
        ag_t = ag_ref[...]
        rows = _row_ids(i, ROW_TILE)
        u0, sg = glu(ag_t, rows)
        prev_rows = i * ROW_TILE - CONV_HALO + lax.broadcasted_iota(jnp.int32, (CONV_HALO, 1), 0)
        u0_prev, _ = glu(agprev_ref[...], prev_rows)
        uext_ref[0:CONV_HALO, :] = jnp.where(i == 0, 0.0, u0_prev)
        uext_ref[CONV_HALO:, :] = u0
        du0 = jnp.zeros((ROW_TILE, D_CONV), F32)
        for k in range(CONV_WIDTH):
            back = CONV_WIDTH - 1 - k
            du0 = du0 + w_ref[k:k + 1, :] * dext_ref[back:back + ROW_TILE, :]
            first = CONV_HALO - back
            dw_ref[k:k + 1, :] += jnp.sum(du1_t * uext_ref[first:first + ROW_TILE, :], axis=0, keepdims=True)
        du0 = jnp.where(rows >= DEAD, du0, 0.0)
        da = du0 * sg
        dgate = du0 * ag_t[:, :D_CONV] * sg * (1.0 - sg)
        dag_ref[...] = jnp.concatenate([da, dgate], axis=-1).astype(BF16)

    out_shapes = [jax.ShapeDtypeStruct((n_rows, 2 * D_CONV), BF16), jax.ShapeDtypeStruct((CONV_WIDTH, D_CONV), F32)]
    return pl.pallas_call(
        body, name="bwd_conv", grid=(nt,),
        in_specs=[_tile_spec(du1.shape), _halo_after(du1.shape, CONV_HALO, n_rows), _tile_spec(ag.shape),
                  _halo_before(ag.shape, CONV_HALO), _whole_spec(conv_w.shape)],
        out_specs=[_tile_spec(out_shapes[0].shape), _acc_spec(out_shapes[1].shape)],
        out_shape=out_shapes,
        scratch_shapes=[pltpu.VMEM((ROW_TILE + CONV_HALO, D_CONV), F32), pltpu.VMEM((ROW_TILE + CONV_HALO, D_CONV), F32)],
        compiler_params=_params("arbitrary"),
    )(du1, du1, ag, ag, conv_w)


def _bwd_in(dag, dcq, dckv, dkr, x, meta_pad, dh1, g1, w_in, n_rows):
    nt = n_rows // ROW_TILE

    def body(dag_ref, dcq_ref, dckv_ref, dkr_ref, x_ref, meta_ref, dh1_ref, g_ref, w_ref,
             dz_ref, gx_ref, gmeta_ref, dg1_ref):
        i = pl.program_id(0)
        dz = jnp.concatenate([dag_ref[...], dcq_ref[...], dckv_ref[...], dkr_ref[...]], axis=-1)
        dz_ref[...] = dz
        h0 = jnp.where(i == 0, meta_ref[...], x_ref[...])
        dx, dg1 = _rms_bwd(_dot_nt(dz, w_ref[...]), h0, g_ref[...])
        dh0 = dh1_ref[...] + dx
        gx_ref[...] = dh0

        @pl.when(i == 0)
        def _():
            gmeta_ref[...] = dh0

        _accumulate(dg1_ref, i == 0, dg1)

    out_shapes = [
        jax.ShapeDtypeStruct((n_rows, D_IN), BF16),
        jax.ShapeDtypeStruct((n_rows - ROW_TILE, D_MODEL), F32),
        jax.ShapeDtypeStruct((ROW_TILE, D_MODEL), F32),
        jax.ShapeDtypeStruct((1, D_MODEL), F32),
    ]
    tiles = [dag, dcq, dckv, dkr]
    return pl.pallas_call(
        body, name="bwd_in", grid=(nt,),
        in_specs=[_tile_spec(a.shape) for a in tiles]
        + [_real_spec(D_MODEL), _whole_spec(meta_pad.shape), _tile_spec(dh1.shape), _whole_spec(g1.shape), _whole_spec(w_in.shape)],
        out_specs=[_tile_spec(out_shapes[0].shape), _real_spec(D_MODEL), _acc_spec(out_shapes[2].shape), _acc_spec(out_shapes[3].shape)],
        out_shape=out_shapes,
        compiler_params=_params("arbitrary"),
    )(*tiles, x, meta_pad, dh1, g1, w_in)


def _contraction_tile(n_rows):
    return next(t for t in range(n_rows // 2 // _LANES * _LANES, 0, -_LANES) if n_rows % t == 0)


def _weight_grad(a, b, name, a_transposed=False):
    groups = max(a.shape[0] if a.ndim == 3 else 1, b.shape[0] if b.ndim == 3 else 1)
    n_rows, n = b.shape[-2], b.shape[-1]
    m = a.shape[-2] if a_transposed else a.shape[-1]
    kt = _contraction_tile(n_rows)
    steps = n_rows // kt

    def body(a_ref, b_ref, out_ref, acc_ref):
        i = pl.program_id(1)
        a_t, b_t = a_ref[...].astype(BF16), b_ref[...].astype(BF16)
        part = _dot(a_t, b_t) if a_transposed else _dot_tn(a_t, b_t)
        _accumulate(acc_ref, i == 0, part)

        @pl.when(i == steps - 1)
        def _():
            out_ref[...] = acc_ref[...].astype(out_ref.dtype)

    def spec(arr, rows_last):
        block = (arr.shape[-2], kt) if rows_last else (kt, arr.shape[-1])
        at = (lambda i: (0, i)) if rows_last else (lambda i: (i, 0))
        if arr.ndim == 3:
            return pl.BlockSpec((None,) + block, lambda g, i: (g,) + at(i))
        return pl.BlockSpec(block, lambda g, i: at(i))

    return pl.pallas_call(
        body, name=name, grid=(groups, steps),
        in_specs=[spec(a, a_transposed), spec(b, False)],
        out_specs=pl.BlockSpec((None, m, n), lambda g, i: (g, 0, 0)),
        out_shape=jax.ShapeDtypeStruct((groups, m, n), BF16),
        scratch_shapes=[pltpu.VMEM((m, n), F32)],
        compiler_params=_params("parallel", "arbitrary"),
    )(a, b)


def _my_index():
    return 4 * lax.axis_index("x") + 2 * lax.axis_index("y") + lax.axis_index("c")


def _peer(k):
    flip = lambda v, bit: 1 - v if bit else v
    px = flip(lax.axis_index("x"), k & 4)
    py = flip(lax.axis_index("y"), k & 2)
    pc = flip(lax.axis_index("c"), k & 1)
    return (px, py, pc), 4 * px + 2 * py + pc


def _all_gather(shards, dtypes):
    n = len(shards)

    def body(*refs):
        ins, outs, stages = refs[:n], refs[n:2 * n], refs[2 * n:3 * n]
        send_sems, recv_sems, local_sems = refs[3 * n:]
        me = _my_index()
        for a in range(n):
            stages[a][...] = ins[a][...].astype(stages[a].dtype)
        local = [pltpu.make_async_copy(stages[a], outs[a].at[me], local_sems.at[a]) for a in range(n)]
        for cp in local:
            cp.start()

        def copy(a, k, slot):
            peer, _ = _peer(k)
            return pltpu.make_async_remote_copy(
                src_ref=stages[a], dst_ref=outs[a].at[slot], send_sem=send_sems.at[a, k - 1],
                recv_sem=recv_sems.at[a, k - 1], device_id=peer, device_id_type=MESH)

        for k in range(1, N_DEV):
            for a in range(n):
                copy(a, k, me).start()
        for k in range(1, N_DEV):
            for a in range(n):
                copy(a, k, _peer(k)[1]).wait()
        for cp in local:
            cp.wait()

    return pl.pallas_call(
        body, name="gather_weights",
        in_specs=[pl.BlockSpec(memory_space=pltpu.VMEM)] * n,
        out_specs=[pl.BlockSpec(memory_space=pl.ANY)] * n,
        out_shape=[jax.ShapeDtypeStruct((N_DEV,) + s.shape, dt) for s, dt in zip(shards, dtypes)],
        scratch_shapes=[pltpu.VMEM(s.shape, dt) for s, dt in zip(shards, dtypes)]
        + [pltpu.SemaphoreType.DMA((n, N_DEV - 1)), pltpu.SemaphoreType.DMA((n, N_DEV - 1)), pltpu.SemaphoreType.DMA((n,))],
        compiler_params=pltpu.CompilerParams(vmem_limit_bytes=VMEM_LIMIT),
    )(*shards)


def _exchange(parts, whole):
    n = len(parts)

    def body(*refs):
        ins, outs = refs[:n], refs[n:2 * n]
        send_sems, recv_sems, local_sems = refs[2 * n:]
        me = _my_index()

        def src(a, slab):
            return ins[a] if whole[a] else ins[a].at[slab]

        local = [pltpu.make_async_copy(src(a, me), outs[a].at[me], local_sems.at[a]) for a in range(n)]
        for cp in local:
            cp.start()

        def copy(a, k, slab, slot):
            peer, _ = _peer(k)
            return pltpu.make_async_remote_copy(
                src_ref=src(a, slab), dst_ref=outs[a].at[slot], send_sem=send_sems.at[a, k - 1],
                recv_sem=recv_sems.at[a, k - 1], device_id=peer, device_id_type=MESH)

        for k in range(1, N_DEV):
            for a in range(n):
                copy(a, k, _peer(k)[1], me).start()
        for k in range(1, N_DEV):
            for a in range(n):
                copy(a, k, _peer(k)[1], _peer(k)[1]).wait()
        for cp in local:
            cp.wait()

    return pl.pallas_call(
        body, name="exchange_grads",
        in_specs=[pl.BlockSpec(memory_space=pl.ANY)] * n,
        out_specs=[pl.BlockSpec(memory_space=pl.ANY)] * n,
        out_shape=[jax.ShapeDtypeStruct(((N_DEV,) + p.shape) if w else p.shape, p.dtype) for p, w in zip(parts, whole)],
        scratch_shapes=[pltpu.SemaphoreType.DMA((n, N_DEV - 1)), pltpu.SemaphoreType.DMA((n, N_DEV - 1)),
                        pltpu.SemaphoreType.DMA((n,))],
    )(*parts)


_HBM = pl.BlockSpec(memory_space=pltpu.HBM)
_SEM = pl.BlockSpec(memory_space=pltpu.SEMAPHORE)
_DATAFLOW = pltpu.SideEffectType.DATAFLOW_SIDE_EFFECTING


def _exchange_start(parts, whole, name):
    n = len(parts)
    n_copies = n * (N_DEV - 1)

    def body(*refs):
        ins, lands = refs[:n], refs[n:2 * n]
        send_sems = refs[2 * n:2 * n + n_copies]
        recv_sems = refs[2 * n + n_copies:2 * n + 2 * n_copies]
        token, local_sems = refs[-2], refs[-1]
        me = _my_index()

        def src(a, slab):
            return ins[a] if whole[a] else ins[a].at[slab]

        local = [pltpu.make_async_copy(src(a, me), lands[a].at[me], local_sems.at[a]) for a in range(n)]
        for cp in local:
            cp.start()
        for k in range(1, N_DEV):
            peer, peer_index = _peer(k)
            for a in range(n):
                at = a * (N_DEV - 1) + k - 1
                pltpu.make_async_remote_copy(
                    src_ref=src(a, peer_index), dst_ref=lands[a].at[me], send_sem=send_sems[at],
                    recv_sem=recv_sems[at], device_id=peer, device_id_type=MESH).start()
        for cp in local:
            cp.wait()
        token[...] = jnp.zeros_like(token)

    hbm = lambda a: pltpu.with_memory_space_constraint(a, pltpu.HBM)
    lands = [lax.empty(((N_DEV,) + p.shape) if w else p.shape, p.dtype) for p, w in zip(parts, whole)]
    outs = pl.pallas_call(
        body, name=name,
        in_specs=[_HBM] * (2 * n),
        out_specs=[_SEM] * (2 * n_copies) + [_HBM] * (2 * n) + [pl.BlockSpec(memory_space=pltpu.VMEM)],
        out_shape=[pltpu.SemaphoreType.DMA(())] * (2 * n_copies)
        + [pltpu.HBM(p.shape, p.dtype) for p in parts] + [pltpu.HBM(l.shape, l.dtype) for l in lands]
        + [jax.ShapeDtypeStruct((8, _LANES), F32)],
        input_output_aliases={a: 2 * n_copies + a for a in range(2 * n)},
        scratch_shapes=[pltpu.SemaphoreType.DMA((n,))],
        compiler_params=pltpu.CompilerParams(has_side_effects=_DATAFLOW),
    )(*[hbm(p) for p in parts], *[hbm(l) for l in lands])
    sems, rest = list(outs[:2 * n_copies]), outs[2 * n_copies:]
    return sems[:n_copies], sems[n_copies:], list(rest[:n]), list(rest[n:2 * n]), rest[-1]


def _exchange_wait(send_sems, recv_sems, parts, lands, whole, after, name):
    n = len(parts)
    n_copies = n * (N_DEV - 1)

    def body(*refs):
        ins, zones = refs[:n], refs[n:2 * n]
        send_refs = refs[2 * n:2 * n + n_copies]
        recv_refs = refs[2 * n + n_copies:2 * n + 2 * n_copies]
        for k in range(1, N_DEV):
            peer, peer_index = _peer(k)
            for a in range(n):
                at = a * (N_DEV - 1) + k - 1
                copy = pltpu.make_async_remote_copy(
                    src_ref=ins[a] if whole[a] else ins[a].at[peer_index], dst_ref=zones[a].at[peer_index],
                    send_sem=send_refs[at], recv_sem=recv_refs[at], device_id=peer, device_id_type=MESH)
                copy.wait_send()
                copy.wait_recv()

    outs = pl.pallas_call(
        body, name=name,
        in_specs=[_HBM] * (2 * n) + [_SEM] * (2 * n_copies) + [pl.BlockSpec(memory_space=pl.ANY)],
        out_specs=[_HBM] * (2 * n),
        out_shape=[pltpu.HBM(p.shape, p.dtype) for p in parts] + [pltpu.HBM(l.shape, l.dtype) for l in lands],
        input_output_aliases={a: a for a in range(2 * n)},
        compiler_params=pltpu.CompilerParams(has_side_effects=_DATAFLOW),
    )(*parts, *lands, *send_sems, *recv_sems, after)
    return list(outs[n:])


def _row_block(rows):
    if rows <= ROW_TILE:
        return rows
    return next(rb for rb in range(ROW_TILE, 0, -16) if rows % rb == 0)


def _adamw(landing, w, m, v, name):
    rows, cols = w.shape
    rb = _row_block(rows)

    def body(l_ref, w_ref, m_ref, v_ref, g_ref, d_ref, m2_ref, v2_ref):
        g = l_ref[0].astype(F32)
        for p in range(1, N_DEV):
            g = g + l_ref[p].astype(F32)
        m2 = ADAM_B1 * m_ref[...] + (1.0 - ADAM_B1) * g
        v2 = ADAM_B2 * v_ref[...] + (1.0 - ADAM_B2) * (g * g)
        m_hat = m2 / (1.0 - ADAM_B1 ** ADAM_STEP)
        v_hat = v2 / (1.0 - ADAM_B2 ** ADAM_STEP)
        g_ref[...] = g
        d_ref[...] = -ADAM_LR * (m_hat / (jnp.sqrt(v_hat) + ADAM_EPS) + ADAM_WD * w_ref[...])
        m2_ref[...] = m2
        v2_ref[...] = v2

    flat = pl.BlockSpec((rb, cols), lambda i: (i, 0))
    return pl.pallas_call(
        body, name=name, grid=(rows // rb,),
        in_specs=[pl.BlockSpec((N_DEV, rb, cols), lambda i: (0, i, 0)), flat, flat, flat],
        out_specs=[flat] * 4,
        out_shape=[jax.ShapeDtypeStruct((rows, cols), F32)] * 4,
        compiler_params=_params("parallel"),
    )(landing, w, m, v)


_REPLICATED = (
    ("mix_norm_g", D_MODEL), ("q_norm_g", Q_LORA), ("kv_norm_g", KV_LORA), ("conv_b", D_CONV), ("conv_ln_g", D_CONV),
    ("conv_ln_b", D_CONV), ("conv_out_g", D_CONV), ("attn_out_g", D_CONV), ("ffn_norm_g", D_MODEL),
    ("ffn_conv_b", D_UP), ("final_norm_g", D_MODEL),
)
_PACK_ROWS = 104

_WEIGHT_ORDER = (
    "meta_tokens", "mix_norm_g", "w_in", "q_norm_g", "w_uq", "kv_norm_g", "w_ukv", "conv_w", "conv_b", "conv_ln_g",
    "conv_ln_b", "conv_out_g", "attn_out_g", "w_out", "ffn_norm_g", "w_ffn_up", "ffn_conv_w", "ffn_conv_b",
    "w_ffn_down", "final_norm_g",
)


def _pack(vectors):
    flat = jnp.concatenate([vectors[name].reshape(-1) for name, _ in _REPLICATED])
    return jnp.pad(flat, (0, _PACK_ROWS * _LANES - flat.shape[0])).reshape(_PACK_ROWS, _LANES)


def _unpack(packed, like):
    flat, out, at = packed.reshape(-1), {}, 0
    for name, size in _REPLICATED:
        out[name] = flat[at:at + size].reshape(like[name].shape)
        at += size
    return out


def _pad_rows(a, rows):
    return jnp.pad(a, ((0, rows - a.shape[0]), (0, 0)))


def _slabs(a):
    r, c = a.shape
    return a.reshape(r, N_DEV, c // N_DEV).transpose(1, 0, 2)


def _unslab(a):
    g, r, c = a.shape
    return a.transpose(1, 0, 2).reshape(r, g * c)


def _local_step(x, target, w, n_rows, ffn_weights, send_ffn_grads):
    cos, sin = _rope_tables(n_rows)
    cos_t, sin_t = cos.T, sin.T
    meta_pad, g1, gf = w["meta_pad"], w["mix_norm_g"], w["final_norm_g"]
    gq, gkv, gb_col = w["q_norm_g"], w["kv_norm_g"], w["attn_out_g"].reshape(D_ATTN, 1)
    nb, ag, cq, ckv, kr = _fwd_in(x, meta_pad, g1, w["w_in"], n_rows)
    mix_a, u1 = _fwd_conv(ag, w["conv_w"], w["conv_b"], w["conv_ln_g"], w["conv_ln_b"], w["conv_out_g"], n_rows)
    q_t, k, v, v_t, cqn, ckvn = _fwd_qkv(cq, ckv, kr, gq, gkv, w["wq_t"], w["w_ukv"], w["wv_t"], cos, sin, cos_t, sin_t, n_rows)
    o_t, lse = _attn_fwd(q_t, k, v_t, n_rows)
    w_out, w_up, w_down = ffn_weights(o_t)
    mix_bt, h1 = _fwd_out(x, meta_pad, mix_a, o_t, gb_col, w_out, n_rows)
    n2, up0, act, dh2, loss, dgf = _fwd_ffn(h1, target, w["ffn_norm_g"], w_up, w["fw"], w["fb"], w_down, gf, n_rows)

    dup, dfb = _bwd_ffn_act(dh2, up0, w_down, w["fw"], w["fb"], n_rows)
    dup0, dh1, dfw, dg2 = _bwd_ffn_up(dup, up0, h1, dh2, w["ffn_norm_g"], w_up, w["fw"], n_rows)
    token = send_ffn_grads(_weight_grad(n2, dup0, "grad_w_ffn_up"),
                           _weight_grad(act, dh2, "grad_w_ffn_down").reshape(N_DEV, D_FF // N_DEV, D_MODEL))
    do_t, delta, du1, dgb, dga, dlg, dlb, dcb = _bwd_out(
        dh1, o_t, u1, w_out, gb_col + token[0:1, 0:1], w["conv_ln_g"], w["conv_ln_b"], w["conv_out_g"], n_rows)
    dq_t, dk, dv = _attn_bwd(q_t, k, v, do_t, lse, delta, n_rows)
    dqraw_t, dkv, dcq, dckv, dkr, dgq, dgkv = _bwd_qkv(
        dq_t, dk, dv, cq, ckv, gq, gkv, w["wq_t"], w["w_ukv"], cos, sin, cos_t, sin_t, n_rows)
    dag, dcw = _bwd_conv(du1, ag, w["conv_w"], n_rows)
    dz, gx, gmeta, dg1 = _bwd_in(dag, dcq, dckv, dkr, x, meta_pad, dh1, g1, w["w_in"], n_rows)

    grad_w_out = jnp.concatenate([_weight_grad(mix_a, dh1, "grad_w_out_conv")[0],
                                  _weight_grad(mix_bt, dh1, "grad_w_out_attn", a_transposed=True)[0]], axis=0)
    sharded = {
        "w_in": _slabs(_weight_grad(nb, dz, "grad_w_in")[0]),
        "w_uq": _weight_grad(dqraw_t, cqn, "grad_w_uq", a_transposed=True).transpose(0, 2, 1),
        "w_ukv": _weight_grad(ckvn, dkv, "grad_w_ukv"),
        "w_out": grad_w_out.reshape(N_DEV, D_MODEL // N_DEV, D_MODEL),
        "conv_w": _slabs(dcw),
        "ffn_conv_w": dfw,
        "meta_tokens": _slabs(gmeta[DEAD:]),
    }
    replicated = {
        "mix_norm_g": dg1, "q_norm_g": dgq, "kv_norm_g": dgkv, "conv_b": dcb, "conv_ln_g": dlg, "conv_ln_b": dlb,
        "conv_out_g": dga, "attn_out_g": dgb, "ffn_norm_g": dg2, "ffn_conv_b": dfb, "final_norm_g": dgf,
    }
    return loss[0, 0], gx, sharded, replicated


_SHARDED = (
    ("w_in", None, BF16), ("w_uq", None, BF16), ("w_ukv", None, BF16), ("w_out", None, BF16), ("w_ffn_up", None, BF16),
    ("w_ffn_down", None, BF16), ("conv_w", 32, F32), ("ffn_conv_w", 8, F32), ("meta_tokens", None, F32),
)
_LATE_WEIGHTS = ("w_out", "w_ffn_up", "w_ffn_down")
_EARLY_GRADS = ("w_ffn_up", "w_ffn_down")


def kernel(x, meta_tokens, mix_norm_g, w_in, q_norm_g, w_uq, kv_norm_g, w_ukv, conv_w, conv_b, conv_ln_g, conv_ln_b, conv_out_g, attn_out_g, w_out, ffn_norm_g, w_ffn_up, ffn_conv_w, ffn_conv_b, w_ffn_down, final_norm_g, loss_target, m_meta_tokens, m_mix_norm_g, m_w_in, m_q_norm_g, m_w_uq, m_kv_norm_g, m_w_ukv, m_conv_w, m_conv_b, m_conv_ln_g, m_conv_ln_b, m_conv_out_g, m_attn_out_g, m_w_out, m_ffn_norm_g, m_w_ffn_up, m_ffn_conv_w, m_ffn_conv_b, m_w_ffn_down, m_final_norm_g, v_meta_tokens, v_mix_norm_g, v_w_in, v_q_norm_g, v_w_uq, v_kv_norm_g, v_w_ukv, v_conv_w, v_conv_b, v_conv_ln_g, v_conv_ln_b, v_conv_out_g, v_attn_out_g, v_w_out, v_ffn_norm_g, v_w_ffn_up, v_ffn_conv_w, v_ffn_conv_b, v_w_ffn_down, v_final_norm_g):
    given = dict(locals())
    weights = {name: given[name] for name in _WEIGHT_ORDER}
    moments_m = {name: given["m_" + name] for name in _WEIGHT_ORDER}
    moments_v = {name: given["v_" + name] for name in _WEIGHT_ORDER}
    seq = x.shape[1]
    n_rows = ROW_TILE + seq

    def shard2d(a):
        return a.reshape(a.shape[-2], a.shape[-1])

    early = [entry for entry in _SHARDED if entry[0] not in _LATE_WEIGHTS]
    shards = []
    for name, pad_to, _ in early:
        s = shard2d(weights[name])
        shards.append(s if pad_to is None else _pad_rows(s, pad_to))
    gathered = dict(zip([name for name, _, _ in early], _all_gather(shards, [dt for _, _, dt in early])))
    behind = gathered["meta_tokens"][0, 0, 0] * 0.0
    late_parts = [(shard2d(weights[name]) + behind).astype(BF16) for name in _LATE_WEIGHTS]
    late = _exchange_start(late_parts, [True] * len(late_parts), "gather_late_start")
    meta_full = _unslab(gathered["meta_tokens"])
    full = {
        "meta_pad": jnp.concatenate([jnp.zeros((DEAD, D_MODEL), F32), meta_full], axis=0),
        "w_in": _unslab(gathered["w_in"]),
        "wq_t": gathered["w_uq"].transpose(0, 2, 1),
        "w_ukv": gathered["w_ukv"],
        "wv_t": gathered["w_ukv"][:, :, QK_NOPE:].transpose(0, 2, 1),
        "conv_w": _unslab(gathered["conv_w"][:, :CONV_WIDTH]),
        "fw": gathered["ffn_conv_w"][:, :FFN_CONV_WIDTH],
        "fb": ffn_conv_b.reshape(N_DEV, 1, UP_SLAB),
        "final_norm_g": final_norm_g.reshape(1, D_MODEL),
    }
    for name in ("mix_norm_g", "q_norm_g", "kv_norm_g", "conv_b", "conv_ln_g", "conv_ln_b", "conv_out_g", "attn_out_g",
                 "ffn_norm_g"):
        full[name] = weights[name]
    full["mix_norm_g"] = mix_norm_g + late[4][0:1, 0:1]

    def ffn_weights(after):
        send_sems, recv_sems, parts, lands, _ = late
        w_out_all, w_up_all, w_down_all = _exchange_wait(
            send_sems, recv_sems, parts, lands, [True] * len(parts), after, "gather_late_wait")
        return (w_out_all.reshape(D_MODEL, D_MODEL), w_up_all, w_down_all.reshape(N_ACT_SLAB, UP_SLAB, D_MODEL))

    early_grads = []

    def send_ffn_grads(*grads):
        early_grads.append(_exchange_start(list(grads), [False] * len(grads), "exchange_early_start"))
        return early_grads[0][4]

    loss, gx, sharded, replicated = _local_step(x[0], loss_target[0], full, n_rows, ffn_weights, send_ffn_grads)
    loss = lax.psum(loss, ("x", "y", "c"))

    rest = [entry for entry in _SHARDED if entry[0] not in _EARLY_GRADS]
    parts, whole = [], []
    for name, pad_to, dt in rest:
        p = sharded[name].astype(dt)
        parts.append(p if pad_to is None else jnp.pad(p, ((0, 0), (0, pad_to - p.shape[1]), (0, 0))))
        whole.append(False)
    parts.append(_pack(replicated))
    whole.append(True)
    landed = _exchange(parts, whole)
    landing = dict(zip([name for name, _, _ in rest], landed[:-1]))
    send_sems, recv_sems, sent, lands, _ = early_grads[0]
    landing.update(zip(_EARLY_GRADS, _exchange_wait(
        send_sems, recv_sems, sent, lands, [False] * len(sent), landed[-1], "exchange_early_wait")))

    grad, delta, new_m, new_v = {}, {}, {}, {}
    for name, pad_to, _ in _SHARDED:
        land = landing[name]
        ws, ms, vs = shard2d(weights[name]), shard2d(moments_m[name]), shard2d(moments_v[name])
        rows = ws.shape[0]
        if pad_to is not None:
            ws, ms, vs = _pad_rows(ws, pad_to), _pad_rows(ms, pad_to), _pad_rows(vs, pad_to)
        outs = _adamw(land, ws, ms, vs, "adamw_" + name)
        shape = weights[name].shape
        grad[name], delta[name], new_m[name], new_v[name] = (o[:rows].reshape(shape) for o in outs)
    outs = _adamw(landed[-1], _pack(weights), _pack(moments_m), _pack(moments_v), "adamw_replicated")
    for store, packed in zip((grad, delta, new_m, new_v), outs):
        store.update(_unpack(packed, weights))

    return (loss, gx[None], *[grad[n] for n in _WEIGHT_ORDER], *[delta[n] for n in _WEIGHT_ORDER],
            *[new_m[n] for n in _WEIGHT_ORDER], *[new_v[n] for n in _WEIGHT_ORDER])
```

```python
import functools

import jax
import jax.numpy as jnp
from jax import lax
from jax.experimental import pallas as pl
from jax.experimental.pallas import tpu as pltpu
from jax.experimental.pallas import tpu_sc as plsc

F32 = jnp.float32
BF16 = jnp.bfloat16

N_DEV = 8
D_MODEL = 1024
CHUNK = 64
CHUNK_SHIFT = 6
N_META = 16
D_CONV = 512
CONV_WIDTH = 31
N_HEADS = 8
QK_NOPE = 64
QK_ROPE = 32
QK_DIM = QK_NOPE + QK_ROPE
V_HEAD = 64
KV_HEAD = QK_NOPE + V_HEAD
D_ATTN = N_HEADS * V_HEAD
Q_LORA = 384
KV_LORA = 256
ROPE_THETA = 10000.0
D_IN = 2 * D_CONV + Q_LORA + KV_LORA + QK_ROPE
D_FF = 2816
D_UP = 2 * D_FF
FFN_CONV_WIDTH = 3
UP_SLAB = D_UP // N_DEV
N_ACT_SLAB = D_FF // UP_SLAB
EPS = 1e-6
NEG = -1e30
ADAM_LR = 0.001
ADAM_B1 = 0.9
ADAM_B2 = 0.999
ADAM_EPS = 1e-08
ADAM_WD = 0.01
ADAM_STEP = 10

ROW_TILE = 256
DEAD = ROW_TILE - N_META
CONV_HALO = 32
FFN_HALO = 16
VMEM_LIMIT = 56 * 1024 * 1024
_LANES = 128

MESH = pl.DeviceIdType.MESH


def _dot(a, b):
    return jnp.dot(a, b, preferred_element_type=F32)


def _dot_nt(a, b):
    return lax.dot_general(a, b, (((1,), (1,)), ((), ())), preferred_element_type=F32)


def _dot_tn(a, b):
    return lax.dot_general(a, b, (((0,), (0,)), ((), ())), preferred_element_type=F32)


def _sigmoid(x):
    return 1.0 / (1.0 + jnp.exp(-x))


def _rms_fwd(x, g):
    r = lax.rsqrt(jnp.mean(x * x, axis=-1, keepdims=True) + EPS)
    return x * r * g


def _rms_bwd(dy, x, g):
    r = lax.rsqrt(jnp.mean(x * x, axis=-1, keepdims=True) + EPS)
    w = dy * g
    dx = r * w - x * (r * r * r) * jnp.mean(w * x, axis=-1, keepdims=True)
    return dx, jnp.sum(dy * x * r, axis=0, keepdims=True)


def _rope(x, cos, sin):
    half = QK_ROPE // 2
    x1, x2 = x[:, :half], x[:, half:]
    return jnp.concatenate([x1 * cos - x2 * sin, x2 * cos + x1 * sin], axis=-1)


def _rope_t(dy, cos, sin):
    half = QK_ROPE // 2
    d1, d2 = dy[:, :half], dy[:, half:]
    return jnp.concatenate([d1 * cos + d2 * sin, d2 * cos - d1 * sin], axis=-1)


def _row_ids(i, rows):
    return i * rows + lax.broadcasted_iota(jnp.int32, (rows, 1), 0)


def _accumulate(ref, first, value):
    @pl.when(first)
    def _():
        ref[...] = value

    @pl.when(jnp.logical_not(first))
    def _():
        ref[...] += value


def _tile_spec(shape):
    nd = len(shape)
    if nd == 2:
        return pl.BlockSpec((ROW_TILE, shape[1]), lambda i: (i, 0))
    return pl.BlockSpec((shape[0], ROW_TILE, shape[2]), lambda i: (0, i, 0))


def _whole_spec(shape):
    nd = len(shape)
    return pl.BlockSpec(tuple(shape), lambda i: (0,) * nd, pipeline_mode=pl.Buffered(1))


def _acc_spec(shape):
    nd = len(shape)
    return pl.BlockSpec(tuple(shape), lambda i: (0,) * nd)


def _real_spec(width):
    return pl.BlockSpec((ROW_TILE, width), lambda i: (jnp.maximum(i - 1, 0), 0))


def _params(*semantics):
    return pltpu.CompilerParams(dimension_semantics=semantics, vmem_limit_bytes=VMEM_LIMIT)


def _fwd_in(x, meta_pad, g1, w_in, n_rows):
    nt = n_rows // ROW_TILE

    def body(x_ref, meta_ref, g_ref, w_ref, nb_ref, ag_ref, cq_ref, ckv_ref, kr_ref):
        i = pl.program_id(0)
        h0 = jnp.where(i == 0, meta_ref[...], x_ref[...])
        nb = _rms_fwd(h0, g_ref[...]).astype(BF16)
        nb_ref[...] = nb
        z = _dot(nb, w_ref[...])
        ag_ref[...] = z[:, :2 * D_CONV]
        cq_ref[...] = z[:, 2 * D_CONV:2 * D_CONV + Q_LORA]
        ckv_ref[...] = z[:, 2 * D_CONV + Q_LORA:2 * D_CONV + Q_LORA + KV_LORA]
        kr_ref[...] = z[:, 2 * D_CONV + Q_LORA + KV_LORA:]

    out_shapes = [
        jax.ShapeDtypeStruct((n_rows, D_MODEL), BF16),
        jax.ShapeDtypeStruct((n_rows, 2 * D_CONV), F32),
        jax.ShapeDtypeStruct((n_rows, Q_LORA), F32),
        jax.ShapeDtypeStruct((n_rows, KV_LORA), F32),
        jax.ShapeDtypeStruct((n_rows, QK_ROPE), F32),
    ]
    return pl.pallas_call(
        body, name="fwd_in", grid=(nt,),
        in_specs=[_real_spec(D_MODEL), _whole_spec(meta_pad.shape), _whole_spec(g1.shape), _whole_spec(w_in.shape)],
        out_specs=[_tile_spec(s.shape) for s in out_shapes],
        out_shape=out_shapes,
        compiler_params=_params("parallel"),
    )(x, meta_pad, g1, w_in)


def _conv_chain(u1, ln_g, ln_b):
    mu = jnp.mean(u1, axis=-1, keepdims=True)
    xc = u1 - mu
    rstd = lax.rsqrt(jnp.mean(xc * xc, axis=-1, keepdims=True) + EPS)
    xh = xc * rstd
    u2 = xh * ln_g + ln_b
    return xh, u2, u2 * _sigmoid(u2), rstd


def _fwd_conv(ag, conv_w, conv_b, ln_g, ln_b, out_g, n_rows):
    nt = n_rows // ROW_TILE

    def body(ag_ref, w_ref, b_ref, lg_ref, lb_ref, og_ref, mix_ref, u1_ref, ext_ref):
        i = pl.program_id(0)

        @pl.when(i == 0)
        def _():
            ext_ref[0:CONV_HALO, :] = jnp.zeros((CONV_HALO, D_CONV), F32)

        ag_t = ag_ref[...]
        live = _row_ids(i, ROW_TILE) >= DEAD
        u0 = jnp.where(live, ag_t[:, :D_CONV] * _sigmoid(ag_t[:, D_CONV:]), 0.0)
        ext_ref[CONV_HALO:, :] = u0
        first = CONV_HALO - (CONV_WIDTH - 1)
        acc = jnp.zeros((ROW_TILE, D_CONV), F32)
        for k in range(CONV_WIDTH):
            acc = acc + w_ref[k:k + 1, :] * ext_ref[first + k:first + k + ROW_TILE, :]
        u1 = acc + b_ref[...]
        ext_ref[0:CONV_HALO, :] = ext_ref[ROW_TILE:ROW_TILE + CONV_HALO, :]
        u1_ref[...] = u1
        _, _, u3, _ = _conv_chain(u1, lg_ref[...], lb_ref[...])
        mix_ref[...] = _rms_fwd(u3, og_ref[...]).astype(BF16)

    out_shapes = [jax.ShapeDtypeStruct((n_rows, D_CONV), BF16), jax.ShapeDtypeStruct((n_rows, D_CONV), F32)]
    small = [conv_w, conv_b, ln_g, ln_b, out_g]
    return pl.pallas_call(
        body, name="fwd_conv", grid=(nt,),
        in_specs=[_tile_spec(ag.shape)] + [_whole_spec(a.shape) for a in small],
        out_specs=[_tile_spec(s.shape) for s in out_shapes],
        out_shape=out_shapes,
        scratch_shapes=[pltpu.VMEM((ROW_TILE + CONV_HALO, D_CONV), F32)],
        compiler_params=_params("arbitrary"),
    )(ag, *small)


def _lane_tile(shape):
    if len(shape) == 2:
        return pl.BlockSpec((shape[0], ROW_TILE), lambda i: (0, i))
    return pl.BlockSpec((shape[0], shape[1], ROW_TILE), lambda i: (0, 0, i))


def _rope_rows(x, cos, sin):
    half = QK_ROPE // 2
    x1, x2 = x[:half], x[half:]
    return jnp.concatenate([x1 * cos - x2 * sin, x2 * cos + x1 * sin], axis=0)


def _rope_rows_t(dy, cos, sin):
    half = QK_ROPE // 2
    d1, d2 = dy[:half], dy[half:]
    return jnp.concatenate([d1 * cos + d2 * sin, d2 * cos - d1 * sin], axis=0)


def _fwd_qkv(cq, ckv, kr, gq, gkv, wq_t, w_ukv, wv_t, cos, sin, cos_t, sin_t, n_rows):
    nt = n_rows // ROW_TILE

    def body(cq_ref, ckv_ref, kr_ref, gq_ref, gkv_ref, wqt_ref, wkv_ref, wvt_ref, cos_ref, sin_ref, cost_ref, sint_ref,
             qt_ref, k_ref, v_ref, vt_ref, cqn_ref, ckvn_ref):
        cqn = _rms_fwd(cq_ref[...], gq_ref[...]).astype(BF16)
        ckvn = _rms_fwd(ckv_ref[...], gkv_ref[...]).astype(BF16)
        cqn_ref[...] = cqn
        ckvn_ref[...] = ckvn
        k_rot = _rope(kr_ref[...], cos_ref[...], sin_ref[...])
        cos_rows, sin_rows = cost_ref[...], sint_ref[...]
        for h in range(N_HEADS):
            q_raw = _dot_nt(wqt_ref[h], cqn)
            qt_ref[h] = jnp.concatenate(
                [q_raw[:QK_NOPE], _rope_rows(q_raw[QK_NOPE:], cos_rows, sin_rows)], axis=0).astype(BF16)
            kv = _dot(ckvn, wkv_ref[h])
            k_ref[h] = jnp.concatenate([kv[:, :QK_NOPE], k_rot], axis=-1).astype(BF16)
            v_ref[h] = kv[:, QK_NOPE:].astype(BF16)
            vt_ref[h] = _dot_nt(wvt_ref[h], ckvn).astype(BF16)

    out_shapes = [
        jax.ShapeDtypeStruct((N_HEADS, QK_DIM, n_rows), BF16),
        jax.ShapeDtypeStruct((N_HEADS, n_rows, QK_DIM), BF16),
        jax.ShapeDtypeStruct((N_HEADS, n_rows, V_HEAD), BF16),
        jax.ShapeDtypeStruct((N_HEADS, V_HEAD, n_rows), BF16),
        jax.ShapeDtypeStruct((n_rows, Q_LORA), BF16),
        jax.ShapeDtypeStruct((n_rows, KV_LORA), BF16),
    ]
    tiles = [cq, ckv, kr]
    whole = [gq, gkv, wq_t, w_ukv, wv_t]
    out_specs = [_lane_tile(out_shapes[0].shape), _tile_spec(out_shapes[1].shape), _tile_spec(out_shapes[2].shape),
                 _lane_tile(out_shapes[3].shape), _tile_spec(out_shapes[4].shape), _tile_spec(out_shapes[5].shape)]
    return pl.pallas_call(
        body, name="fwd_qkv", grid=(nt,),
        in_specs=[_tile_spec(a.shape) for a in tiles] + [_whole_spec(a.shape) for a in whole]
        + [_tile_spec(cos.shape), _tile_spec(sin.shape), _lane_tile(cos_t.shape), _lane_tile(sin_t.shape)],
        out_specs=out_specs,
        out_shape=out_shapes,
        compiler_params=_params("parallel"),
    )(*tiles, *whole, cos, sin, cos_t, sin_t)


def _chunk_of(rows):
    return jnp.where(rows >= ROW_TILE, lax.shift_right_arithmetic(rows - ROW_TILE, CHUNK_SHIFT) + 1, 0)


def _visible(i, j):
    k_rows = j * ROW_TILE + lax.broadcasted_iota(jnp.int32, (ROW_TILE, 1), 0)
    q_rows = i * ROW_TILE + lax.broadcasted_iota(jnp.int32, (1, ROW_TILE), 1)
    return jnp.logical_and(_chunk_of(q_rows) >= _chunk_of(k_rows), k_rows >= DEAD)


def _attn_fwd(q_t, k, v_t, n_rows):
    nt = n_rows // ROW_TILE
    scale = QK_DIM ** -0.5

    def body(qt_ref, k_ref, vt_ref, ot_ref, lse_ref):
        i = pl.program_id(0)
        q_ts = [qt_ref[h] for h in range(N_HEADS)]

        def make_step(masked):
            def step(j, carry):
                rows = pl.ds(pl.multiple_of(j * ROW_TILE, ROW_TILE), ROW_TILE)
                scores = [_dot(k_ref[h, rows, :], q_ts[h]) for h in range(N_HEADS)]
                visible = _visible(i, j) if masked else None
                probs, state = [], []
                for h in range(N_HEADS):
                    m, l, _ = carry[h]
                    s = scores[h] * scale
                    if masked:
                        s = jnp.where(visible, s, NEG)
                    m_new = jnp.maximum(m, jnp.max(s, axis=0, keepdims=True))
                    alpha = jnp.exp(m - m_new)
                    p = jnp.exp(s - m_new)
                    probs.append(p.astype(BF16))
                    state.append((m_new, alpha * l + jnp.sum(p, axis=0, keepdims=True), alpha))
                outs = [_dot(vt_ref[h, :, rows], probs[h]) for h in range(N_HEADS)]
                return tuple((state[h][0], state[h][1], state[h][2] * carry[h][2] + outs[h]) for h in range(N_HEADS))
            return step

        init = tuple((jnp.full((1, ROW_TILE), NEG, F32), jnp.zeros((1, ROW_TILE), F32),
                      jnp.zeros((V_HEAD, ROW_TILE), F32)) for _ in range(N_HEADS))
        carry = make_step(True)(0, init)
        carry = lax.fori_loop(1, i, make_step(False), carry)
        carry = lax.fori_loop(jnp.maximum(i, 1), i + 1, make_step(True), carry)
        for h in range(N_HEADS):
            m, l, acc = carry[h]
            ot_ref[h] = acc / l
            lse_ref[h] = m + jnp.log(l)

    out_shapes = [jax.ShapeDtypeStruct((N_HEADS, V_HEAD, n_rows), F32), jax.ShapeDtypeStruct((N_HEADS, 1, n_rows), F32)]
    return pl.pallas_call(
        body, name="attn_fwd", grid=(nt,),
        in_specs=[_lane_tile(q_t.shape), _whole_spec(k.shape), _whole_spec(v_t.shape)],
        out_specs=[_lane_tile(s.shape) for s in out_shapes],
        out_shape=out_shapes,
        compiler_params=_params("parallel"),
    )(q_t, k, v_t)


def _heads_to_rows(ref):
    return jnp.concatenate([ref[h] for h in range(N_HEADS)], axis=0)


def _rms_cols(x, g_col):
    r = lax.rsqrt(jnp.mean(x * x, axis=0, keepdims=True) + EPS)
    return x * r * g_col


def _fwd_out(x, meta_pad, mix_a, o_t, gb_col, w_out, n_rows):
    nt = n_rows // ROW_TILE

    def body(x_ref, meta_ref, mixa_ref, ot_ref, gb_ref, w_ref, mixbt_ref, h1_ref):
        i = pl.program_id(0)
        h0 = jnp.where(i == 0, meta_ref[...], x_ref[...])
        mix_bt = _rms_cols(_heads_to_rows(ot_ref), gb_ref[...]).astype(BF16)
        mixbt_ref[...] = mix_bt
        h1_ref[...] = h0 + _dot(mixa_ref[...], w_ref[:D_CONV, :]) + _dot_tn(mix_bt, w_ref[D_CONV:, :])

    out_shapes = [jax.ShapeDtypeStruct((D_ATTN, n_rows), BF16), jax.ShapeDtypeStruct((n_rows, D_MODEL), F32)]
    return pl.pallas_call(
        body, name="fwd_out", grid=(nt,),
        in_specs=[_real_spec(D_MODEL), _whole_spec(meta_pad.shape), _tile_spec(mix_a.shape), _lane_tile(o_t.shape),
                  _whole_spec(gb_col.shape), _whole_spec(w_out.shape)],
        out_specs=[_lane_tile(out_shapes[0].shape), _tile_spec(out_shapes[1].shape)],
        out_shape=out_shapes,
        compiler_params=_params("parallel"),
    )(x, meta_pad, mix_a, o_t, gb_col, w_out)


def _ffn_conv(ext_ref, w_ref, b_ref, s):
    first = FFN_HALO - (FFN_CONV_WIDTH - 1)
    acc = b_ref[s]
    for k in range(FFN_CONV_WIDTH):
        acc = acc + w_ref[s, k:k + 1, :] * ext_ref[s, first + k:first + k + ROW_TILE, :]
    return acc


def _fwd_ffn(h1, target, g2, w_up, fw, fb, w_down, gf, n_rows):
    nt = n_rows // ROW_TILE

    def body(h1_ref, t_ref, g2_ref, wup_ref, fw_ref, fb_ref, wdn_ref, gf_ref,
             n2_ref, up0_ref, act_ref, dh2_ref, loss_ref, dgf_ref, ext_ref):
        i = pl.program_id(0)

        @pl.when(i == 0)
        def _():
            ext_ref[:, 0:FFN_HALO, :] = jnp.zeros((N_DEV, FFN_HALO, UP_SLAB), F32)

        h1_t = h1_ref[...]
        n2 = _rms_fwd(h1_t, g2_ref[...]).astype(BF16)
        n2_ref[...] = n2
        live = _row_ids(i, ROW_TILE) >= DEAD
        for s in range(N_DEV):
            up0 = jnp.where(live, _dot(n2, wup_ref[s]), 0.0).astype(BF16)
            up0_ref[s] = up0
            ext_ref[s, FFN_HALO:, :] = up0.astype(F32)
        h2 = h1_t
        for s in range(N_ACT_SLAB):
            gate = _ffn_conv(ext_ref, fw_ref, fb_ref, s)
            val = _ffn_conv(ext_ref, fw_ref, fb_ref, s + N_ACT_SLAB)
            act = (gate * _sigmoid(gate) * val).astype(BF16)
            act_ref[s] = act
            h2 = h2 + _dot(act, wdn_ref[s])
        ext_ref[:, 0:FFN_HALO, :] = ext_ref[:, ROW_TILE:ROW_TILE + FFN_HALO, :]

        gf_t = gf_ref[...]
        y = _rms_fwd(h2, gf_t)
        diff = jnp.where(i >= 1, y - t_ref[...], 0.0)
        tile_loss = 0.5 * jnp.sum(jnp.sum(diff * diff, axis=-1, keepdims=True), axis=0, keepdims=True) / D_MODEL
        dh2, dgf = _rms_bwd(diff / D_MODEL, h2, gf_t)
        dh2_ref[...] = dh2
        _accumulate(loss_ref, i == 0, jnp.broadcast_to(tile_loss, loss_ref.shape))
        _accumulate(dgf_ref, i == 0, dgf)

    out_shapes = [
        jax.ShapeDtypeStruct((n_rows, D_MODEL), BF16),
        jax.ShapeDtypeStruct((N_DEV, n_rows, UP_SLAB), BF16),
        jax.ShapeDtypeStruct((N_ACT_SLAB, n_rows, UP_SLAB), BF16),
        jax.ShapeDtypeStruct((n_rows, D_MODEL), F32),
        jax.ShapeDtypeStruct((8, 128), F32),
        jax.ShapeDtypeStruct((1, D_MODEL), F32),
    ]
    whole = [g2, w_up, fw, fb, w_down, gf]
    return pl.pallas_call(
        body, name="fwd_ffn", grid=(nt,),
        in_specs=[_tile_spec(h1.shape), _real_spec(D_MODEL)] + [_whole_spec(a.shape) for a in whole],
        out_specs=[_tile_spec(s.shape) for s in out_shapes[:4]] + [_acc_spec(s.shape) for s in out_shapes[4:]],
        out_shape=out_shapes,
        scratch_shapes=[pltpu.VMEM((N_DEV, ROW_TILE + FFN_HALO, UP_SLAB), F32)],
        compiler_params=_params("arbitrary"),
    )(h1, target, *whole)


def _rope_tables(n_rows):
    pos = jnp.maximum(jnp.arange(n_rows, dtype=jnp.int32) - DEAD, 0)
    inv_freq = 1.0 / (ROPE_THETA ** (jnp.arange(0, QK_ROPE, 2, dtype=F32) / QK_ROPE))
    ang = pos.astype(F32)[:, None] * inv_freq[None, :]
    return jnp.cos(ang), jnp.sin(ang)


def _halo_after(shape, halo, n_rows):
    last = n_rows // halo - 1
    step = ROW_TILE // halo
    if len(shape) == 2:
        return pl.BlockSpec((halo, shape[1]), lambda i: (jnp.minimum((i + 1) * step, last), 0))
    return pl.BlockSpec((shape[0], halo, shape[2]), lambda i: (0, jnp.minimum((i + 1) * step, last), 0))


def _halo_before(shape, halo):
    step = ROW_TILE // halo
    if len(shape) == 2:
        return pl.BlockSpec((halo, shape[1]), lambda i: (jnp.maximum(i * step - 1, 0), 0))
    return pl.BlockSpec((shape[0], halo, shape[2]), lambda i: (0, jnp.maximum(i * step - 1, 0), 0))


def _bwd_ffn_act(dh2, up0, w_down, fw, fb, n_rows):
    nt = n_rows // ROW_TILE

    def body(dh2_ref, up0_ref, wdn_ref, fw_ref, fb_ref, dup_ref, dfb_ref, ext_ref):
        i = pl.program_id(0)

        @pl.when(i == 0)
        def _():
            ext_ref[:, 0:FFN_HALO, :] = jnp.zeros((N_DEV, FFN_HALO, UP_SLAB), F32)
            dfb_ref[...] = jnp.zeros_like(dfb_ref)

        for s in range(N_DEV):
            ext_ref[s, FFN_HALO:, :] = up0_ref[s].astype(F32)
        dh2_b = dh2_ref[...].astype(BF16)
        for s in range(N_ACT_SLAB):
            gate = _ffn_conv(ext_ref, fw_ref, fb_ref, s)
            val = _ffn_conv(ext_ref, fw_ref, fb_ref, s + N_ACT_SLAB)
            d_act = _dot_nt(dh2_b, wdn_ref[s])
            sg = _sigmoid(gate)
            d_gate = d_act * val * sg * (1.0 + gate * (1.0 - sg))
            d_val = d_act * gate * sg
            dup_ref[s] = d_gate.astype(BF16)
            dup_ref[s + N_ACT_SLAB] = d_val.astype(BF16)
            dfb_ref[s] += jnp.sum(d_gate, axis=0, keepdims=True)
            dfb_ref[s + N_ACT_SLAB] += jnp.sum(d_val, axis=0, keepdims=True)
        ext_ref[:, 0:FFN_HALO, :] = ext_ref[:, ROW_TILE:ROW_TILE + FFN_HALO, :]

    out_shapes = [jax.ShapeDtypeStruct((N_DEV, n_rows, UP_SLAB), BF16), jax.ShapeDtypeStruct((N_DEV, 1, UP_SLAB), F32)]
    whole = [w_down, fw, fb]
    return pl.pallas_call(
        body, name="bwd_ffn_act", grid=(nt,),
        in_specs=[_tile_spec(dh2.shape), _tile_spec(up0.shape)] + [_whole_spec(a.shape) for a in whole],
        out_specs=[_tile_spec(out_shapes[0].shape), _acc_spec(out_shapes[1].shape)],
        out_shape=out_shapes,
        scratch_shapes=[pltpu.VMEM((N_DEV, ROW_TILE + FFN_HALO, UP_SLAB), F32)],
        compiler_params=_params("arbitrary"),
    )(dh2, up0, *whole)


def _bwd_ffn_up(dup, up0, h1, dh2, g2, w_up, fw, n_rows):
    nt = n_rows // ROW_TILE

    def body(dup_ref, dnext_ref, up0_ref, uprev_ref, h1_ref, dh2_ref, g2_ref, wup_ref, fw_ref,
             dup0_ref, dh1_ref, dfw_ref, dg2_ref, dext_ref, uext_ref):
        i = pl.program_id(0)

        @pl.when(i == 0)
        def _():
            dfw_ref[...] = jnp.zeros_like(dfw_ref)

        live = _row_ids(i, ROW_TILE) >= DEAD
        dn2 = jnp.zeros((ROW_TILE, D_MODEL), F32)
        for s in range(N_DEV):
            d = dup_ref[s].astype(F32)
            dext_ref[0:ROW_TILE, :] = d
            dext_ref[ROW_TILE:, :] = jnp.where(i == nt - 1, 0.0, dnext_ref[s].astype(F32))
            uext_ref[0:FFN_HALO, :] = jnp.where(i == 0, 0.0, uprev_ref[s].astype(F32))
            uext_ref[FFN_HALO:, :] = up0_ref[s].astype(F32)
            dup0 = jnp.zeros((ROW_TILE, UP_SLAB), F32)
            for k in range(FFN_CONV_WIDTH):
                back = FFN_CONV_WIDTH - 1 - k
                dup0 = dup0 + fw_ref[s, k:k + 1, :] * dext_ref[back:back + ROW_TILE, :]
                first = FFN_HALO - back
                dfw_ref[s, k:k + 1, :] += jnp.sum(d * uext_ref[first:first + ROW_TILE, :], axis=0, keepdims=True)
            dup0_b = jnp.where(live, dup0, 0.0).astype(BF16)
            dup0_ref[s] = dup0_b
            dn2 = dn2 + _dot_nt(dup0_b, wup_ref[s])
        dx, dg2 = _rms_bwd(dn2, h1_ref[...], g2_ref[...])
        dh1_ref[...] = dh2_ref[...] + dx
        _accumulate(dg2_ref, i == 0, dg2)

    out_shapes = [
        jax.ShapeDtypeStruct((N_DEV, n_rows, UP_SLAB), BF16),
        jax.ShapeDtypeStruct((n_rows, D_MODEL), F32),
        jax.ShapeDtypeStruct((N_DEV, FFN_CONV_WIDTH, UP_SLAB), F32),
        jax.ShapeDtypeStruct((1, D_MODEL), F32),
    ]
    return pl.pallas_call(
        body, name="bwd_ffn_up", grid=(nt,),
        in_specs=[_tile_spec(dup.shape), _halo_after(dup.shape, FFN_HALO, n_rows), _tile_spec(up0.shape),
                  _halo_before(up0.shape, FFN_HALO), _tile_spec(h1.shape), _tile_spec(dh2.shape),
                  _whole_spec(g2.shape), _whole_spec(w_up.shape), _whole_spec(fw.shape)],
        out_specs=[_tile_spec(s.shape) for s in out_shapes[:2]] + [_acc_spec(s.shape) for s in out_shapes[2:]],
        out_shape=out_shapes,
        scratch_shapes=[pltpu.VMEM((ROW_TILE + FFN_HALO, UP_SLAB), F32), pltpu.VMEM((ROW_TILE + FFN_HALO, UP_SLAB), F32)],
        compiler_params=_params("arbitrary"),
    )(dup, dup, up0, up0, h1, dh2, g2, w_up, fw)


def _bwd_out(dh1, o_t, u1, w_out, gb_col, ln_g, ln_b, ga, n_rows):
    nt = n_rows // ROW_TILE

    def body(dh1_ref, ot_ref, u1_ref, w_ref, gb_ref, lg_ref, lb_ref, ga_ref,
             dot_ref, delta_ref, du1_ref, dgb_ref, dga_ref, dlg_ref, dlb_ref, dcb_ref):
        i = pl.program_id(0)
        dh1_b = dh1_ref[...].astype(BF16)
        o_t = _heads_to_rows(ot_ref)
        gb = gb_ref[...]
        r = lax.rsqrt(jnp.mean(o_t * o_t, axis=0, keepdims=True) + EPS)
        dmix_bt = _dot_nt(w_ref[D_CONV:, :], dh1_b)
        wgt = dmix_bt * gb
        do_t = r * wgt - o_t * (r * r * r) * jnp.mean(wgt * o_t, axis=0, keepdims=True)
        dgb = jnp.sum(dmix_bt * o_t * r, axis=1, keepdims=True)
        for h in range(N_HEADS):
            do_h = do_t[h * V_HEAD:(h + 1) * V_HEAD]
            dot_ref[h] = do_h.astype(BF16)
            delta_ref[h] = jnp.sum(do_h * ot_ref[h], axis=0, keepdims=True)
        lg = lg_ref[...]
        xh, u2, u3, rstd = _conv_chain(u1_ref[...], lg, lb_ref[...])
        du3, dga = _rms_bwd(_dot_nt(dh1_b, w_ref[:D_CONV, :]), u3, ga_ref[...])
        sg = _sigmoid(u2)
        du2 = du3 * sg * (1.0 + u2 * (1.0 - sg))
        dxh = du2 * lg
        du1 = rstd * (dxh - jnp.mean(dxh, axis=-1, keepdims=True) - xh * jnp.mean(dxh * xh, axis=-1, keepdims=True))
        du1_ref[...] = du1
        first = i == 0
        _accumulate(dgb_ref, first, dgb)
        _accumulate(dga_ref, first, dga)
        _accumulate(dlg_ref, first, jnp.sum(du2 * xh, axis=0, keepdims=True))
        _accumulate(dlb_ref, first, jnp.sum(du2, axis=0, keepdims=True))
        _accumulate(dcb_ref, first, jnp.sum(du1, axis=0, keepdims=True))

    out_shapes = [
        jax.ShapeDtypeStruct((N_HEADS, V_HEAD, n_rows), BF16),
        jax.ShapeDtypeStruct((N_HEADS, 1, n_rows), F32),
        jax.ShapeDtypeStruct((n_rows, D_CONV), F32),
        jax.ShapeDtypeStruct((D_ATTN, 1), F32),
    ] + [jax.ShapeDtypeStruct((1, D_CONV), F32)] * 4
    whole = [w_out, gb_col, ln_g, ln_b, ga]
    return pl.pallas_call(
        body, name="bwd_out", grid=(nt,),
        in_specs=[_tile_spec(dh1.shape), _lane_tile(o_t.shape), _tile_spec(u1.shape)] + [_whole_spec(a.shape) for a in whole],
        out_specs=[_lane_tile(out_shapes[0].shape), _lane_tile(out_shapes[1].shape), _tile_spec(out_shapes[2].shape)]
        + [_acc_spec(s.shape) for s in out_shapes[3:]],
        out_shape=out_shapes,
        compiler_params=_params("arbitrary"),
    )(dh1, o_t, u1, *whole)


ATTN_BWD_HEADS = 4


def _attn_bwd(q_t, k, v, do_t, lse, delta, n_rows):
    nt = n_rows // ROW_TILE
    scale = QK_DIM ** -0.5
    hp = ATTN_BWD_HEADS

    def body(k_ref, v_ref, qt_ref, dot_ref, lse_ref, delta_ref, dqt_ref, dk_ref, dv_ref):
        j = pl.program_id(1)

        @pl.when(j == 0)
        def _():
            dqt_ref[...] = jnp.zeros_like(dqt_ref)

        k_ts = [k_ref[h] for h in range(hp)]
        v_ts = [v_ref[h] for h in range(hp)]

        def make_step(masked):
            def step(i, carry):
                cols = pl.ds(pl.multiple_of(i * ROW_TILE, ROW_TILE), ROW_TILE)
                q_is = [qt_ref[h, :, cols] for h in range(hp)]
                do_is = [dot_ref[h, :, cols] for h in range(hp)]
                scores = [_dot(k_ts[h], q_is[h]) for h in range(hp)]
                dps = [_dot(v_ts[h], do_is[h]) for h in range(hp)]
                visible = _visible(i, j) if masked else None
                probs, dss = [], []
                for h in range(hp):
                    s = scores[h] * scale
                    if masked:
                        s = jnp.where(visible, s, NEG)
                    p = jnp.exp(s - lse_ref[h, :, cols])
                    probs.append(p.astype(BF16))
                    dss.append((p * (dps[h] - delta_ref[h, :, cols]) * scale).astype(BF16))
                out = []
                for h in range(hp):
                    dk, dv = carry[h]
                    dv = dv + _dot_nt(probs[h], do_is[h])
                    dk = dk + _dot_nt(dss[h], q_is[h])
                    dqt_ref[h, :, cols] += _dot_tn(k_ts[h], dss[h])
                    out.append((dk, dv))
                return tuple(out)
            return step

        init = tuple((jnp.zeros((ROW_TILE, QK_DIM), F32), jnp.zeros((ROW_TILE, V_HEAD), F32)) for _ in range(hp))
        carry = make_step(True)(j, init)
        carry = lax.fori_loop(jnp.where(j == 0, j + 1, nt), nt, make_step(True), carry)
        carry = lax.fori_loop(jnp.where(j == 0, nt, j + 1), nt, make_step(False), carry)
        for h in range(hp):
            dk_ref[h], dv_ref[h] = carry[h]

    key_tile = lambda w: pl.BlockSpec((hp, ROW_TILE, w), lambda g, j: (g, j, 0))
    all_cols = lambda w: pl.BlockSpec((hp, w, n_rows), lambda g, j: (g, 0, 0))
    out_shapes = [
        jax.ShapeDtypeStruct((N_HEADS, QK_DIM, n_rows), F32),
        jax.ShapeDtypeStruct((N_HEADS, n_rows, QK_DIM), F32),
        jax.ShapeDtypeStruct((N_HEADS, n_rows, V_HEAD), F32),
    ]
    return pl.pallas_call(
        body, name="attn_bwd", grid=(N_HEADS // hp, nt),
        in_specs=[key_tile(QK_DIM), key_tile(V_HEAD), all_cols(QK_DIM), all_cols(V_HEAD), all_cols(1), all_cols(1)],
        out_specs=[all_cols(QK_DIM), key_tile(QK_DIM), key_tile(V_HEAD)],
        out_shape=out_shapes,
        compiler_params=_params("parallel", "arbitrary"),
    )(k, v, q_t, do_t, lse, delta)


def _bwd_qkv(dq_t, dk, dv, cq, ckv, gq, gkv, wq_t, w_ukv, cos, sin, cos_t, sin_t, n_rows):
    nt = n_rows // ROW_TILE

    def body(dqt_ref, dk_ref, dv_ref, cq_ref, ckv_ref, gq_ref, gkv_ref, wqt_ref, wkv_ref, cos_ref, sin_ref,
             cost_ref, sint_ref, dqraw_ref, dkv_ref, dcq_ref, dckv_ref, dkr_ref, dgq_ref, dgkv_ref):
        i = pl.program_id(0)
        cos_rows, sin_rows = cost_ref[...], sint_ref[...]
        dcqn = jnp.zeros((ROW_TILE, Q_LORA), F32)
        dckvn = jnp.zeros((ROW_TILE, KV_LORA), F32)
        dk_rot = jnp.zeros((ROW_TILE, QK_ROPE), F32)
        for h in range(N_HEADS):
            dq_h, dk_h = dqt_ref[h], dk_ref[h]
            dq_raw = jnp.concatenate(
                [dq_h[:QK_NOPE], _rope_rows_t(dq_h[QK_NOPE:], cos_rows, sin_rows)], axis=0).astype(BF16)
            dqraw_ref[h] = dq_raw
            dcqn = dcqn + _dot_tn(dq_raw, wqt_ref[h])
            dkv = jnp.concatenate([dk_h[:, :QK_NOPE], dv_ref[h]], axis=-1).astype(BF16)
            dkv_ref[h] = dkv
            dckvn = dckvn + _dot_nt(dkv, wkv_ref[h])
            dk_rot = dk_rot + dk_h[:, QK_NOPE:]
        dkr_ref[...] = _rope_t(dk_rot, cos_ref[...], sin_ref[...]).astype(BF16)
        dcq, dgq = _rms_bwd(dcqn, cq_ref[...], gq_ref[...])
        dckv, dgkv = _rms_bwd(dckvn, ckv_ref[...], gkv_ref[...])
        dcq_ref[...] = dcq.astype(BF16)
        dckv_ref[...] = dckv.astype(BF16)
        _accumulate(dgq_ref, i == 0, dgq)
        _accumulate(dgkv_ref, i == 0, dgkv)

    out_shapes = [
        jax.ShapeDtypeStruct((N_HEADS, QK_DIM, n_rows), BF16),
        jax.ShapeDtypeStruct((N_HEADS, n_rows, KV_HEAD), BF16),
        jax.ShapeDtypeStruct((n_rows, Q_LORA), BF16),
        jax.ShapeDtypeStruct((n_rows, KV_LORA), BF16),
        jax.ShapeDtypeStruct((n_rows, QK_ROPE), BF16),
        jax.ShapeDtypeStruct((1, Q_LORA), F32),
        jax.ShapeDtypeStruct((1, KV_LORA), F32),
    ]
    tiles = [dk, dv, cq, ckv]
    whole = [gq, gkv, wq_t, w_ukv]
    return pl.pallas_call(
        body, name="bwd_qkv", grid=(nt,),
        in_specs=[_lane_tile(dq_t.shape)] + [_tile_spec(a.shape) for a in tiles] + [_whole_spec(a.shape) for a in whole]
        + [_tile_spec(cos.shape), _tile_spec(sin.shape), _lane_tile(cos_t.shape), _lane_tile(sin_t.shape)],
        out_specs=[_lane_tile(out_shapes[0].shape)] + [_tile_spec(s.shape) for s in out_shapes[1:5]]
        + [_acc_spec(s.shape) for s in out_shapes[5:]],
        out_shape=out_shapes,
        compiler_params=_params("arbitrary"),
    )(dq_t, *tiles, *whole, cos, sin, cos_t, sin_t)


def _bwd_conv(du1, ag, conv_w, n_rows):
    nt = n_rows // ROW_TILE

    def glu(ag_t, rows):
        sg = _sigmoid(ag_t[:, D_CONV:])
        return jnp.where(rows >= DEAD, ag_t[:, :D_CONV] * sg, 0.0), sg

    def body(du1_ref, dnext_ref, ag_ref, agprev_ref, w_ref, dag_ref, dw_ref, dext_ref, uext_ref):
        i = pl.program_id(0)

        @pl.when(i == 0)
        def _():
            dw_ref[...] = jnp.zeros_like(dw_ref)

        du1_t = du1_ref[...]
        dext_ref[0:ROW_TILE, :] = du1_t
        dext_ref[ROW_TILE:, :] = jnp.where(i == nt - 1, 0.0, dnext_ref[...])
        ag_t = ag_ref[...]
        rows = _row_ids(i, ROW_TILE)
        u0, sg = glu(ag_t, rows)
        prev_rows = i * ROW_TILE - CONV_HALO + lax.broadcasted_iota(jnp.int32, (CONV_HALO, 1), 0)
        u0_prev, _ = glu(agprev_ref[...], prev_rows)
        uext_ref[0:CONV_HALO, :] = jnp.where(i == 0, 0.0, u0_prev)
        uext_ref[CONV_HALO:, :] = u0
        du0 = jnp.zeros((ROW_TILE, D_CONV), F32)
        for k in range(CONV_WIDTH):
            back = CONV_WIDTH - 1 - k
            du0 = du0 + w_ref[k:k + 1, :] * dext_ref[back:back + ROW_TILE, :]
            first = CONV_HALO - back
            dw_ref[k:k + 1, :] += jnp.sum(du1_t * uext_ref[first:first + ROW_TILE, :], axis=0, keepdims=True)
        du0 = jnp.where(rows >= DEAD, du0, 0.0)
        da = du0 * sg
        dgate = du0 * ag_t[:, :D_CONV] * sg * (1.0 - sg)
        dag_ref[...] = jnp.concatenate([da, dgate], axis=-1).astype(BF16)

    out_shapes = [jax.ShapeDtypeStruct((n_rows, 2 * D_CONV), BF16), jax.ShapeDtypeStruct((CONV_WIDTH, D_CONV), F32)]
    return pl.pallas_call(
        body, name="bwd_conv", grid=(nt,),
        in_specs=[_tile_spec(du1.shape), _halo_after(du1.shape, CONV_HALO, n_rows), _tile_spec(ag.shape),
                  _halo_before(ag.shape, CONV_HALO), _whole_spec(conv_w.shape)],
        out_specs=[_tile_spec(out_shapes[0].shape), _acc_spec(out_shapes[1].shape)],
        out_shape=out_shapes,
        scratch_shapes=[pltpu.VMEM((ROW_TILE + CONV_HALO, D_CONV), F32), pltpu.VMEM((ROW_TILE + CONV_HALO, D_CONV), F32)],
        compiler_params=_params("arbitrary"),
    )(du1, du1, ag, ag, conv_w)


def _bwd_in(dag, dcq, dckv, dkr, x, meta_pad, dh1, g1, w_in, n_rows):
    nt = n_rows // ROW_TILE

    def body(dag_ref, dcq_ref, dckv_ref, dkr_ref, x_ref, meta_ref, dh1_ref, g_ref, w_ref,
             dz_ref, gx_ref, gmeta_ref, dg1_ref):
        i = pl.program_id(0)
        dz = jnp.concatenate([dag_ref[...], dcq_ref[...], dckv_ref[...], dkr_ref[...]], axis=-1)
        dz_ref[...] = dz
        h0 = jnp.where(i == 0, meta_ref[...], x_ref[...])
        dx, dg1 = _rms_bwd(_dot_nt(dz, w_ref[...]), h0, g_ref[...])
        dh0 = dh1_ref[...] + dx
        gx_ref[...] = dh0

        @pl.when(i == 0)
        def _():
            gmeta_ref[...] = dh0

        _accumulate(dg1_ref, i == 0, dg1)

    out_shapes = [
        jax.ShapeDtypeStruct((n_rows, D_IN), BF16),
        jax.ShapeDtypeStruct((n_rows - ROW_TILE, D_MODEL), F32),
        jax.ShapeDtypeStruct((ROW_TILE, D_MODEL), F32),
        jax.ShapeDtypeStruct((1, D_MODEL), F32),
    ]
    tiles = [dag, dcq, dckv, dkr]
    return pl.pallas_call(
        body, name="bwd_in", grid=(nt,),
        in_specs=[_tile_spec(a.shape) for a in tiles]
        + [_real_spec(D_MODEL), _whole_spec(meta_pad.shape), _tile_spec(dh1.shape), _whole_spec(g1.shape), _whole_spec(w_in.shape)],
        out_specs=[_tile_spec(out_shapes[0].shape), _real_spec(D_MODEL), _acc_spec(out_shapes[2].shape), _acc_spec(out_shapes[3].shape)],
        out_shape=out_shapes,
        compiler_params=_params("arbitrary"),
    )(*tiles, x, meta_pad, dh1, g1, w_in)


def _contraction_tile(n_rows):
    return next(t for t in range(n_rows // 2 // _LANES * _LANES, 0, -_LANES) if n_rows % t == 0)


def _weight_grad(a, b, name, a_transposed=False):
    groups = max(a.shape[0] if a.ndim == 3 else 1, b.shape[0] if b.ndim == 3 else 1)
    n_rows, n = b.shape[-2], b.shape[-1]
    m = a.shape[-2] if a_transposed else a.shape[-1]
    kt = _contraction_tile(n_rows)
    steps = n_rows // kt

    def body(a_ref, b_ref, out_ref, acc_ref):
        i = pl.program_id(1)
        a_t, b_t = a_ref[...].astype(BF16), b_ref[...].astype(BF16)
        part = _dot(a_t, b_t) if a_transposed else _dot_tn(a_t, b_t)
        _accumulate(acc_ref, i == 0, part)

        @pl.when(i == steps - 1)
        def _():
            out_ref[...] = acc_ref[...].astype(out_ref.dtype)

    def spec(arr, rows_last):
        block = (arr.shape[-2], kt) if rows_last else (kt, arr.shape[-1])
        at = (lambda i: (0, i)) if rows_last else (lambda i: (i, 0))
        if arr.ndim == 3:
            return pl.BlockSpec((None,) + block, lambda g, i: (g,) + at(i))
        return pl.BlockSpec(block, lambda g, i: at(i))

    return pl.pallas_call(
        body, name=name, grid=(groups, steps),
        in_specs=[spec(a, a_transposed), spec(b, False)],
        out_specs=pl.BlockSpec((None, m, n), lambda g, i: (g, 0, 0)),
        out_shape=jax.ShapeDtypeStruct((groups, m, n), BF16),
        scratch_shapes=[pltpu.VMEM((m, n), F32)],
        compiler_params=_params("parallel", "arbitrary"),
    )(a, b)


def _my_index():
    return 4 * lax.axis_index("x") + 2 * lax.axis_index("y") + lax.axis_index("c")


def _peer(k):
    flip = lambda v, bit: 1 - v if bit else v
    px = flip(lax.axis_index("x"), k & 4)
    py = flip(lax.axis_index("y"), k & 2)
    pc = flip(lax.axis_index("c"), k & 1)
    return (px, py, pc), 4 * px + 2 * py + pc


def _all_gather(shards, dtypes):
    n = len(shards)

    def body(*refs):
        ins, outs, stages = refs[:n], refs[n:2 * n], refs[2 * n:3 * n]
        send_sems, recv_sems, local_sems = refs[3 * n:]
        me = _my_index()
        for a in range(n):
            stages[a][...] = ins[a][...].astype(stages[a].dtype)
        local = [pltpu.make_async_copy(stages[a], outs[a].at[me], local_sems.at[a]) for a in range(n)]
        for cp in local:
            cp.start()

        def copy(a, k, slot):
            peer, _ = _peer(k)
            return pltpu.make_async_remote_copy(
                src_ref=stages[a], dst_ref=outs[a].at[slot], send_sem=send_sems.at[a, k - 1],
                recv_sem=recv_sems.at[a, k - 1], device_id=peer, device_id_type=MESH)

        for k in range(1, N_DEV):
            for a in range(n):
                copy(a, k, me).start()
        for k in range(1, N_DEV):
            for a in range(n):
                copy(a, k, _peer(k)[1]).wait()
        for cp in local:
            cp.wait()

    return pl.pallas_call(
        body, name="gather_weights",
        in_specs=[pl.BlockSpec(memory_space=pltpu.VMEM)] * n,
        out_specs=[pl.BlockSpec(memory_space=pl.ANY)] * n,
        out_shape=[jax.ShapeDtypeStruct((N_DEV,) + s.shape, dt) for s, dt in zip(shards, dtypes)],
        scratch_shapes=[pltpu.VMEM(s.shape, dt) for s, dt in zip(shards, dtypes)]
        + [pltpu.SemaphoreType.DMA((n, N_DEV - 1)), pltpu.SemaphoreType.DMA((n, N_DEV - 1)), pltpu.SemaphoreType.DMA((n,))],
        compiler_params=pltpu.CompilerParams(vmem_limit_bytes=VMEM_LIMIT),
    )(*shards)


def _exchange(parts, whole):
    n = len(parts)

    def body(*refs):
        ins, outs = refs[:n], refs[n:2 * n]
        send_sems, recv_sems, local_sems = refs[2 * n:]
        me = _my_index()

        def src(a, slab):
            return ins[a] if whole[a] else ins[a].at[slab]

        local = [pltpu.make_async_copy(src(a, me), outs[a].at[me], local_sems.at[a]) for a in range(n)]
        for cp in local:
            cp.start()

        def copy(a, k, slab, slot):
            peer, _ = _peer(k)
            return pltpu.make_async_remote_copy(
                src_ref=src(a, slab), dst_ref=outs[a].at[slot], send_sem=send_sems.at[a, k - 1],
                recv_sem=recv_sems.at[a, k - 1], device_id=peer, device_id_type=MESH)

        for k in range(1, N_DEV):
            for a in range(n):
                copy(a, k, _peer(k)[1], me).start()
        for k in range(1, N_DEV):
            for a in range(n):
                copy(a, k, _peer(k)[1], _peer(k)[1]).wait()
        for cp in local:
            cp.wait()

    return pl.pallas_call(
        body, name="exchange_grads",
        in_specs=[pl.BlockSpec(memory_space=pl.ANY)] * n,
        out_specs=[pl.BlockSpec(memory_space=pl.ANY)] * n,
        out_shape=[jax.ShapeDtypeStruct(((N_DEV,) + p.shape) if w else p.shape, p.dtype) for p, w in zip(parts, whole)],
        scratch_shapes=[pltpu.SemaphoreType.DMA((n, N_DEV - 1)), pltpu.SemaphoreType.DMA((n, N_DEV - 1)),
                        pltpu.SemaphoreType.DMA((n,))],
    )(*parts)


def _sequencer_exchange(parts, whole, name, collective_id):
    n = len(parts)
    srcs = [jax.new_ref(p, memory_space=pltpu.MemorySpace.HBM) for p in parts]
    lands = [jax.empty_ref(jax.ShapeDtypeStruct(((N_DEV,) + p.shape) if w else p.shape, p.dtype),
                           memory_space=pltpu.MemorySpace.HBM) for p, w in zip(parts, whole)]

    @pl.kernel(mesh=plsc.ScalarSubcoreMesh(axis_name="sequencer", num_cores=1), name=name,
               scratch_types=(pltpu.SemaphoreType.DMA((n, N_DEV - 1)), pltpu.SemaphoreType.DMA((n, N_DEV - 1)),
                              pltpu.SemaphoreType.DMA((n,))),
               compiler_params=pltpu.CompilerParams(collective_id=collective_id))
    def launch(send_sems, recv_sems, local_sems):
        barrier = pltpu.get_barrier_semaphore()
        for k in range(1, N_DEV):
            pl.semaphore_signal(barrier, inc=1, device_id=_peer(k)[0], device_id_type=MESH)
        pl.semaphore_wait(barrier, N_DEV - 1)
        me = _my_index()

        def src(a, slab):
            return srcs[a] if whole[a] else srcs[a].at[slab]

        local = [pltpu.make_async_copy(src(a, me), lands[a].at[me], local_sems.at[a]) for a in range(n)]
        for cp in local:
            cp.start()

        def copy(a, k, slab, slot):
            return pltpu.make_async_remote_copy(
                src_ref=src(a, slab), dst_ref=lands[a].at[slot], send_sem=send_sems.at[a, k - 1],
                recv_sem=recv_sems.at[a, k - 1], device_id=_peer(k)[0], device_id_type=MESH)

        for k in range(1, N_DEV):
            for a in range(n):
                copy(a, k, _peer(k)[1], me).start()
        for k in range(1, N_DEV):
            for a in range(n):
                copy(a, k, _peer(k)[1], _peer(k)[1]).wait()
        for cp in local:
            cp.wait()

    launch()
    return [land[...] for land in lands]


def _row_block(rows):
    if rows <= ROW_TILE:
        return rows
    return next(rb for rb in range(ROW_TILE, 0, -16) if rows % rb == 0)


def _adamw(landing, w, m, v, name):
    rows, cols = w.shape
    rb = _row_block(rows)

    def body(l_ref, w_ref, m_ref, v_ref, g_ref, d_ref, m2_ref, v2_ref):
        g = l_ref[0].astype(F32)
        for p in range(1, N_DEV):
            g = g + l_ref[p].astype(F32)
        m2 = ADAM_B1 * m_ref[...] + (1.0 - ADAM_B1) * g
        v2 = ADAM_B2 * v_ref[...] + (1.0 - ADAM_B2) * (g * g)
        m_hat = m2 / (1.0 - ADAM_B1 ** ADAM_STEP)
        v_hat = v2 / (1.0 - ADAM_B2 ** ADAM_STEP)
        g_ref[...] = g
        d_ref[...] = -ADAM_LR * (m_hat / (jnp.sqrt(v_hat) + ADAM_EPS) + ADAM_WD * w_ref[...])
        m2_ref[...] = m2
        v2_ref[...] = v2

    flat = pl.BlockSpec((rb, cols), lambda i: (i, 0))
    return pl.pallas_call(
        body, name=name, grid=(rows // rb,),
        in_specs=[pl.BlockSpec((N_DEV, rb, cols), lambda i: (0, i, 0)), flat, flat, flat],
        out_specs=[flat] * 4,
        out_shape=[jax.ShapeDtypeStruct((rows, cols), F32)] * 4,
        compiler_params=_params("parallel"),
    )(landing, w, m, v)


_REPLICATED = (
    ("mix_norm_g", D_MODEL), ("q_norm_g", Q_LORA), ("kv_norm_g", KV_LORA), ("conv_b", D_CONV), ("conv_ln_g", D_CONV),
    ("conv_ln_b", D_CONV), ("conv_out_g", D_CONV), ("attn_out_g", D_CONV), ("ffn_norm_g", D_MODEL),
    ("ffn_conv_b", D_UP), ("final_norm_g", D_MODEL),
)
_PACK_ROWS = 104

_WEIGHT_ORDER = (
    "meta_tokens", "mix_norm_g", "w_in", "q_norm_g", "w_uq", "kv_norm_g", "w_ukv", "conv_w", "conv_b", "conv_ln_g",
    "conv_ln_b", "conv_out_g", "attn_out_g", "w_out", "ffn_norm_g", "w_ffn_up", "ffn_conv_w", "ffn_conv_b",
    "w_ffn_down", "final_norm_g",
)


def _pack(vectors):
    flat = jnp.concatenate([vectors[name].reshape(-1) for name, _ in _REPLICATED])
    return jnp.pad(flat, (0, _PACK_ROWS * _LANES - flat.shape[0])).reshape(_PACK_ROWS, _LANES)


def _unpack(packed, like):
    flat, out, at = packed.reshape(-1), {}, 0
    for name, size in _REPLICATED:
        out[name] = flat[at:at + size].reshape(like[name].shape)
        at += size
    return out


def _pad_rows(a, rows):
    return jnp.pad(a, ((0, rows - a.shape[0]), (0, 0)))


def _slabs(a):
    r, c = a.shape
    return a.reshape(r, N_DEV, c // N_DEV).transpose(1, 0, 2)


def _unslab(a):
    g, r, c = a.shape
    return a.transpose(1, 0, 2).reshape(r, g * c)


def _local_step(x, target, w, n_rows, ffn_weights, send_early_grads):
    cos, sin = _rope_tables(n_rows)
    cos_t, sin_t = cos.T, sin.T
    meta_pad, g1, gf = w["meta_pad"], w["mix_norm_g"], w["final_norm_g"]
    gq, gkv, gb_col = w["q_norm_g"], w["kv_norm_g"], w["attn_out_g"].reshape(D_ATTN, 1)
    nb, ag, cq, ckv, kr = _fwd_in(x, meta_pad, g1, w["w_in"], n_rows)
    mix_a, u1 = _fwd_conv(ag, w["conv_w"], w["conv_b"], w["conv_ln_g"], w["conv_ln_b"], w["conv_out_g"], n_rows)
    q_t, k, v, v_t, cqn, ckvn = _fwd_qkv(cq, ckv, kr, gq, gkv, w["wq_t"], w["w_ukv"], w["wv_t"], cos, sin, cos_t, sin_t, n_rows)
    o_t, lse = _attn_fwd(q_t, k, v_t, n_rows)
    w_out, w_up, w_down = ffn_weights()
    mix_bt, h1 = _fwd_out(x, meta_pad, mix_a, o_t, gb_col, w_out, n_rows)
    n2, up0, act, dh2, loss, dgf = _fwd_ffn(h1, target, w["ffn_norm_g"], w_up, w["fw"], w["fb"], w_down, gf, n_rows)

    dup, dfb = _bwd_ffn_act(dh2, up0, w_down, w["fw"], w["fb"], n_rows)
    dup0, dh1, dfw, dg2 = _bwd_ffn_up(dup, up0, h1, dh2, w["ffn_norm_g"], w_up, w["fw"], n_rows)
    grad_w_out = jnp.concatenate([_weight_grad(mix_a, dh1, "grad_w_out_conv")[0],
                                  _weight_grad(mix_bt, dh1, "grad_w_out_attn", a_transposed=True)[0]], axis=0)
    send_early_grads(_weight_grad(n2, dup0, "grad_w_ffn_up"),
                     _weight_grad(act, dh2, "grad_w_ffn_down").reshape(N_DEV, D_FF // N_DEV, D_MODEL),
                     grad_w_out.reshape(N_DEV, D_MODEL // N_DEV, D_MODEL))
    do_t, delta, du1, dgb, dga, dlg, dlb, dcb = _bwd_out(
        dh1, o_t, u1, w_out, gb_col, w["conv_ln_g"], w["conv_ln_b"], w["conv_out_g"], n_rows)
    dq_t, dk, dv = _attn_bwd(q_t, k, v, do_t, lse, delta, n_rows)
    dqraw_t, dkv, dcq, dckv, dkr, dgq, dgkv = _bwd_qkv(
        dq_t, dk, dv, cq, ckv, gq, gkv, w["wq_t"], w["w_ukv"], cos, sin, cos_t, sin_t, n_rows)
    dag, dcw = _bwd_conv(du1, ag, w["conv_w"], n_rows)
    dz, gx, gmeta, dg1 = _bwd_in(dag, dcq, dckv, dkr, x, meta_pad, dh1, g1, w["w_in"], n_rows)

    sharded = {
        "w_in": _slabs(_weight_grad(nb, dz, "grad_w_in")[0]),
        "w_uq": _weight_grad(dqraw_t, cqn, "grad_w_uq", a_transposed=True).transpose(0, 2, 1),
        "w_ukv": _weight_grad(ckvn, dkv, "grad_w_ukv"),
        "conv_w": _slabs(dcw),
        "ffn_conv_w": dfw,
        "meta_tokens": _slabs(gmeta[DEAD:]),
    }
    replicated = {
        "mix_norm_g": dg1, "q_norm_g": dgq, "kv_norm_g": dgkv, "conv_b": dcb, "conv_ln_g": dlg, "conv_ln_b": dlb,
        "conv_out_g": dga, "attn_out_g": dgb, "ffn_norm_g": dg2, "ffn_conv_b": dfb, "final_norm_g": dgf,
    }
    return loss[0, 0], gx, sharded, replicated


_SHARDED = (
    ("w_in", None, BF16), ("w_uq", None, BF16), ("w_ukv", None, BF16), ("w_out", None, BF16), ("w_ffn_up", None, BF16),
    ("w_ffn_down", None, BF16), ("conv_w", 32, F32), ("ffn_conv_w", 8, F32), ("meta_tokens", None, F32),
)
GATHER_LATE_ID = 3
EXCHANGE_EARLY_ID = 4
_LATE_WEIGHTS = ("w_out", "w_ffn_up", "w_ffn_down")
_EARLY_GRADS = ("w_ffn_up", "w_ffn_down", "w_out")


def kernel(x, meta_tokens, mix_norm_g, w_in, q_norm_g, w_uq, kv_norm_g, w_ukv, conv_w, conv_b, conv_ln_g, conv_ln_b, conv_out_g, attn_out_g, w_out, ffn_norm_g, w_ffn_up, ffn_conv_w, ffn_conv_b, w_ffn_down, final_norm_g, loss_target, m_meta_tokens, m_mix_norm_g, m_w_in, m_q_norm_g, m_w_uq, m_kv_norm_g, m_w_ukv, m_conv_w, m_conv_b, m_conv_ln_g, m_conv_ln_b, m_conv_out_g, m_attn_out_g, m_w_out, m_ffn_norm_g, m_w_ffn_up, m_ffn_conv_w, m_ffn_conv_b, m_w_ffn_down, m_final_norm_g, v_meta_tokens, v_mix_norm_g, v_w_in, v_q_norm_g, v_w_uq, v_kv_norm_g, v_w_ukv, v_conv_w, v_conv_b, v_conv_ln_g, v_conv_ln_b, v_conv_out_g, v_attn_out_g, v_w_out, v_ffn_norm_g, v_w_ffn_up, v_ffn_conv_w, v_ffn_conv_b, v_w_ffn_down, v_final_norm_g):
    given = dict(locals())
    weights = {name: given[name] for name in _WEIGHT_ORDER}
    moments_m = {name: given["m_" + name] for name in _WEIGHT_ORDER}
    moments_v = {name: given["v_" + name] for name in _WEIGHT_ORDER}
    seq = x.shape[1]
    n_rows = ROW_TILE + seq

    def shard2d(a):
        return a.reshape(a.shape[-2], a.shape[-1])

    early = [entry for entry in _SHARDED if entry[0] not in _LATE_WEIGHTS]
    shards = []
    for name, pad_to, _ in early:
        s = shard2d(weights[name])
        shards.append(s if pad_to is None else _pad_rows(s, pad_to))
    gathered = dict(zip([name for name, _, _ in early], _all_gather(shards, [dt for _, _, dt in early])))
    behind = gathered["meta_tokens"][0, 0, 0] * 0.0
    late_parts = [(shard2d(weights[name]) + behind).astype(BF16) for name in _LATE_WEIGHTS]
    late = _sequencer_exchange(late_parts, [True] * len(late_parts), "gather_late", GATHER_LATE_ID)
    meta_full = _unslab(gathered["meta_tokens"])
    full = {
        "meta_pad": jnp.concatenate([jnp.zeros((DEAD, D_MODEL), F32), meta_full], axis=0),
        "w_in": _unslab(gathered["w_in"]),
        "wq_t": gathered["w_uq"].transpose(0, 2, 1),
        "w_ukv": gathered["w_ukv"],
        "wv_t": gathered["w_ukv"][:, :, QK_NOPE:].transpose(0, 2, 1),
        "conv_w": _unslab(gathered["conv_w"][:, :CONV_WIDTH]),
        "fw": gathered["ffn_conv_w"][:, :FFN_CONV_WIDTH],
        "fb": ffn_conv_b.reshape(N_DEV, 1, UP_SLAB),
        "final_norm_g": final_norm_g.reshape(1, D_MODEL),
    }
    for name in ("mix_norm_g", "q_norm_g", "kv_norm_g", "conv_b", "conv_ln_g", "conv_ln_b", "conv_out_g", "attn_out_g",
                 "ffn_norm_g"):
        full[name] = weights[name]

    def ffn_weights():
        w_out_all, w_up_all, w_down_all = late
        return (w_out_all.reshape(D_MODEL, D_MODEL), w_up_all, w_down_all.reshape(N_ACT_SLAB, UP_SLAB, D_MODEL))

    early_landed = []

    def send_early_grads(*grads):
        early_landed.extend(_sequencer_exchange(list(grads), [False] * len(grads), "exchange_early", EXCHANGE_EARLY_ID))

    loss, gx, sharded, replicated = _local_step(x[0], loss_target[0], full, n_rows, ffn_weights, send_early_grads)
    loss = lax.psum(loss, ("x", "y", "c"))

    rest = [entry for entry in _SHARDED if entry[0] not in _EARLY_GRADS]
    parts, whole = [], []
    for name, pad_to, dt in rest:
        p = sharded[name].astype(dt)
        parts.append(p if pad_to is None else jnp.pad(p, ((0, 0), (0, pad_to - p.shape[1]), (0, 0))))
        whole.append(False)
    parts.append(_pack(replicated))
    whole.append(True)
    landed = _exchange(parts, whole)
    landing = dict(zip([name for name, _, _ in rest], landed[:-1]))
    landing.update(zip(_EARLY_GRADS, early_landed))

    grad, delta, new_m, new_v = {}, {}, {}, {}
    for name, pad_to, _ in _SHARDED:
        land = landing[name]
        ws, ms, vs = shard2d(weights[name]), shard2d(moments_m[name]), shard2d(moments_v[name])
        rows = ws.shape[0]
        if pad_to is not None:
            ws, ms, vs = _pad_rows(ws, pad_to), _pad_rows(ms, pad_to), _pad_rows(vs, pad_to)
        outs = _adamw(land, ws, ms, vs, "adamw_" + name)
        shape = weights[name].shape
        grad[name], delta[name], new_m[name], new_v[name] = (o[:rows].reshape(shape) for o in outs)
    outs = _adamw(landed[-1], _pack(weights), _pack(moments_m), _pack(moments_v), "adamw_replicated")
    for store, packed in zip((grad, delta, new_m, new_v), outs):
        store.update(_unpack(packed, weights))

    return (loss, gx[None], *[grad[n] for n in _WEIGHT_ORDER], *[delta[n] for n in _WEIGHT_ORDER],
            *[new_m[n] for n in _WEIGHT_ORDER], *[new_v[n] for n in _WEIGHT_ORDER])
```

```python
import functools

import jax
import jax.numpy as jnp
from jax import lax
from jax.experimental import pallas as pl
from jax.experimental.pallas import tpu as pltpu
from jax.experimental.pallas import tpu_sc as plsc

F32 = jnp.float32
BF16 = jnp.bfloat16

N_DEV = 8
D_MODEL = 1024
CHUNK = 64
CHUNK_SHIFT = 6
N_META = 16
D_CONV = 512
CONV_WIDTH = 31
N_HEADS = 8
QK_NOPE = 64
QK_ROPE = 32
QK_DIM = QK_NOPE + QK_ROPE
V_HEAD = 64
KV_HEAD = QK_NOPE + V_HEAD
D_ATTN = N_HEADS * V_HEAD
Q_LORA = 384
KV_LORA = 256
ROPE_THETA = 10000.0
D_IN = 2 * D_CONV + Q_LORA + KV_LORA + QK_ROPE
D_FF = 2816
D_UP = 2 * D_FF
FFN_CONV_WIDTH = 3
UP_SLAB = D_UP // N_DEV
N_ACT_SLAB = D_FF // UP_SLAB
EPS = 1e-6
NEG = -1e30
ADAM_LR = 0.001
ADAM_B1 = 0.9
ADAM_B2 = 0.999
ADAM_EPS = 1e-08
ADAM_WD = 0.01
ADAM_STEP = 10

ROW_TILE = 256
DEAD = ROW_TILE - N_META
CONV_HALO = 32
FFN_HALO = 16
VMEM_LIMIT = 56 * 1024 * 1024
_LANES = 128

MESH = pl.DeviceIdType.MESH


def _dot(a, b):
    return jnp.dot(a, b, preferred_element_type=F32)


def _dot_nt(a, b):
    return lax.dot_general(a, b, (((1,), (1,)), ((), ())), preferred_element_type=F32)


def _dot_tn(a, b):
    return lax.dot_general(a, b, (((0,), (0,)), ((), ())), preferred_element_type=F32)


def _sigmoid(x):
    return 1.0 / (1.0 + jnp.exp(-x))


def _rms_fwd(x, g):
    r = lax.rsqrt(jnp.mean(x * x, axis=-1, keepdims=True) + EPS)
    return x * r * g


def _rms_bwd(dy, x, g):
    r = lax.rsqrt(jnp.mean(x * x, axis=-1, keepdims=True) + EPS)
    w = dy * g
    dx = r * w - x * (r * r * r) * jnp.mean(w * x, axis=-1, keepdims=True)
    return dx, jnp.sum(dy * x * r, axis=0, keepdims=True)


def _rope(x, cos, sin):
    half = QK_ROPE // 2
    x1, x2 = x[:, :half], x[:, half:]
    return jnp.concatenate([x1 * cos - x2 * sin, x2 * cos + x1 * sin], axis=-1)


def _rope_t(dy, cos, sin):
    half = QK_ROPE // 2
    d1, d2 = dy[:, :half], dy[:, half:]
    return jnp.concatenate([d1 * cos + d2 * sin, d2 * cos - d1 * sin], axis=-1)


def _row_ids(i, rows):
    return i * rows + lax.broadcasted_iota(jnp.int32, (rows, 1), 0)


def _accumulate(ref, first, value):
    @pl.when(first)
    def _():
        ref[...] = value

    @pl.when(jnp.logical_not(first))
    def _():
        ref[...] += value


def _tile_spec(shape):
    nd = len(shape)
    if nd == 2:
        return pl.BlockSpec((ROW_TILE, shape[1]), lambda i: (i, 0))
    return pl.BlockSpec((shape[0], ROW_TILE, shape[2]), lambda i: (0, i, 0))


def _whole_spec(shape):
    nd = len(shape)
    return pl.BlockSpec(tuple(shape), lambda i: (0,) * nd, pipeline_mode=pl.Buffered(1))


def _acc_spec(shape):
    nd = len(shape)
    return pl.BlockSpec(tuple(shape), lambda i: (0,) * nd)


def _real_spec(width):
    return pl.BlockSpec((ROW_TILE, width), lambda i: (jnp.maximum(i - 1, 0), 0))


def _params(*semantics):
    return pltpu.CompilerParams(dimension_semantics=semantics, vmem_limit_bytes=VMEM_LIMIT)


def _fwd_in(x, meta_pad, g1, w_in, n_rows):
    nt = n_rows // ROW_TILE

    def body(x_ref, meta_ref, g_ref, w_ref, nb_ref, ag_ref, cq_ref, ckv_ref, kr_ref):
        i = pl.program_id(0)
        h0 = jnp.where(i == 0, meta_ref[...], x_ref[...])
        nb = _rms_fwd(h0, g_ref[...]).astype(BF16)
        nb_ref[...] = nb
        z = _dot(nb, w_ref[...])
        ag_ref[...] = z[:, :2 * D_CONV]
        cq_ref[...] = z[:, 2 * D_CONV:2 * D_CONV + Q_LORA]
        ckv_ref[...] = z[:, 2 * D_CONV + Q_LORA:2 * D_CONV + Q_LORA + KV_LORA]
        kr_ref[...] = z[:, 2 * D_CONV + Q_LORA + KV_LORA:]

    out_shapes = [
        jax.ShapeDtypeStruct((n_rows, D_MODEL), BF16),
        jax.ShapeDtypeStruct((n_rows, 2 * D_CONV), F32),
        jax.ShapeDtypeStruct((n_rows, Q_LORA), F32),
        jax.ShapeDtypeStruct((n_rows, KV_LORA), F32),
        jax.ShapeDtypeStruct((n_rows, QK_ROPE), F32),
    ]
    return pl.pallas_call(
        body, name="fwd_in", grid=(nt,),
        in_specs=[_real_spec(D_MODEL), _whole_spec(meta_pad.shape), _whole_spec(g1.shape), _whole_spec(w_in.shape)],
        out_specs=[_tile_spec(s.shape) for s in out_shapes],
        out_shape=out_shapes,
        compiler_params=_params("parallel"),
    )(x, meta_pad, g1, w_in)


def _conv_chain(u1, ln_g, ln_b):
    mu = jnp.mean(u1, axis=-1, keepdims=True)
    xc = u1 - mu
    rstd = lax.rsqrt(jnp.mean(xc * xc, axis=-1, keepdims=True) + EPS)
    xh = xc * rstd
    u2 = xh * ln_g + ln_b
    return xh, u2, u2 * _sigmoid(u2), rstd


def _fwd_conv(ag, conv_w, conv_b, ln_g, ln_b, out_g, n_rows):
    nt = n_rows // ROW_TILE

    def body(ag_ref, w_ref, b_ref, lg_ref, lb_ref, og_ref, mix_ref, u1_ref, ext_ref, conv_ref):
        i = pl.program_id(0)

        @pl.when(i == 0)
        def _():
            ext_ref[:, 0:CONV_HALO, :] = jnp.zeros((CONV_PLANES, CONV_HALO, _LANES), F32)

        ag_t = ag_ref[...]
        live = _row_ids(i, ROW_TILE) >= DEAD
        u0 = jnp.where(live, ag_t[:, :D_CONV] * _sigmoid(ag_t[:, D_CONV:]), 0.0)
        _to_planes(ext_ref, (), slice(CONV_HALO, None), u0)
        first = CONV_HALO - (CONV_WIDTH - 1)
        for c in range(CONV_PLANES):
            taps = w_ref[:, c * _LANES:(c + 1) * _LANES]
            for p in range(PHASES):
                acc = jnp.zeros((PHASE_ROWS, _LANES), F32)
                for k in range(CONV_WIDTH):
                    acc = acc + taps[k:k + 1, :] * ext_ref[c, _phase(first + k + p), :]
                conv_ref[c, _phase(p), :] = acc
        ext_ref[:, 0:CONV_HALO, :] = ext_ref[:, ROW_TILE:ROW_TILE + CONV_HALO, :]
        u1 = _from_planes(conv_ref, (), D_CONV) + b_ref[...]
        u1_ref[...] = u1
        _, _, u3, _ = _conv_chain(u1, lg_ref[...], lb_ref[...])
        mix_ref[...] = _rms_fwd(u3, og_ref[...]).astype(BF16)

    out_shapes = [jax.ShapeDtypeStruct((n_rows, D_CONV), BF16), jax.ShapeDtypeStruct((n_rows, D_CONV), F32)]
    small = [conv_w, conv_b, ln_g, ln_b, out_g]
    return pl.pallas_call(
        body, name="fwd_conv", grid=(nt,),
        in_specs=[_tile_spec(ag.shape)] + [_whole_spec(a.shape) for a in small],
        out_specs=[_tile_spec(s.shape) for s in out_shapes],
        out_shape=out_shapes,
        scratch_shapes=[pltpu.VMEM((CONV_PLANES, ROW_TILE + CONV_HALO, _LANES), F32),
                        pltpu.VMEM((CONV_PLANES, ROW_TILE, _LANES), F32)],
        compiler_params=_params("arbitrary"),
    )(ag, *small)


def _lane_tile(shape):
    if len(shape) == 2:
        return pl.BlockSpec((shape[0], ROW_TILE), lambda i: (0, i))
    return pl.BlockSpec((shape[0], shape[1], ROW_TILE), lambda i: (0, 0, i))


def _rope_rows(x, cos, sin):
    half = QK_ROPE // 2
    x1, x2 = x[:half], x[half:]
    return jnp.concatenate([x1 * cos - x2 * sin, x2 * cos + x1 * sin], axis=0)


def _rope_rows_t(dy, cos, sin):
    half = QK_ROPE // 2
    d1, d2 = dy[:half], dy[half:]
    return jnp.concatenate([d1 * cos + d2 * sin, d2 * cos - d1 * sin], axis=0)


def _fwd_qkv(cq, ckv, kr, gq, gkv, wq_t, w_ukv, wv_t, cos, sin, cos_t, sin_t, n_rows):
    nt = n_rows // ROW_TILE

    def body(cq_ref, ckv_ref, kr_ref, gq_ref, gkv_ref, wqt_ref, wkv_ref, wvt_ref, cos_ref, sin_ref, cost_ref, sint_ref,
             qt_ref, k_ref, v_ref, vt_ref, cqn_ref, ckvn_ref):
        cqn = _rms_fwd(cq_ref[...], gq_ref[...]).astype(BF16)
        ckvn = _rms_fwd(ckv_ref[...], gkv_ref[...]).astype(BF16)
        cqn_ref[...] = cqn
        ckvn_ref[...] = ckvn
        k_rot = _rope(kr_ref[...], cos_ref[...], sin_ref[...])
        cos_rows, sin_rows = cost_ref[...], sint_ref[...]
        for h in range(N_HEADS):
            q_raw = _dot_nt(wqt_ref[h], cqn)
            qt_ref[h] = jnp.concatenate(
                [q_raw[:QK_NOPE], _rope_rows(q_raw[QK_NOPE:], cos_rows, sin_rows)], axis=0).astype(BF16)
            kv = _dot(ckvn, wkv_ref[h])
            k_ref[h] = jnp.concatenate([kv[:, :QK_NOPE], k_rot], axis=-1).astype(BF16)
            v_ref[h] = kv[:, QK_NOPE:].astype(BF16)
            vt_ref[h] = _dot_nt(wvt_ref[h], ckvn).astype(BF16)

    out_shapes = [
        jax.ShapeDtypeStruct((N_HEADS, QK_DIM, n_rows), BF16),
        jax.ShapeDtypeStruct((N_HEADS, n_rows, QK_DIM), BF16),
        jax.ShapeDtypeStruct((N_HEADS, n_rows, V_HEAD), BF16),
        jax.ShapeDtypeStruct((N_HEADS, V_HEAD, n_rows), BF16),
        jax.ShapeDtypeStruct((n_rows, Q_LORA), BF16),
        jax.ShapeDtypeStruct((n_rows, KV_LORA), BF16),
    ]
    tiles = [cq, ckv, kr]
    whole = [gq, gkv, wq_t, w_ukv, wv_t]
    out_specs = [_lane_tile(out_shapes[0].shape), _tile_spec(out_shapes[1].shape), _tile_spec(out_shapes[2].shape),
                 _lane_tile(out_shapes[3].shape), _tile_spec(out_shapes[4].shape), _tile_spec(out_shapes[5].shape)]
    return pl.pallas_call(
        body, name="fwd_qkv", grid=(nt,),
        in_specs=[_tile_spec(a.shape) for a in tiles] + [_whole_spec(a.shape) for a in whole]
        + [_tile_spec(cos.shape), _tile_spec(sin.shape), _lane_tile(cos_t.shape), _lane_tile(sin_t.shape)],
        out_specs=out_specs,
        out_shape=out_shapes,
        compiler_params=_params("parallel"),
    )(*tiles, *whole, cos, sin, cos_t, sin_t)


def _chunk_of(rows):
    return jnp.where(rows >= ROW_TILE, lax.shift_right_arithmetic(rows - ROW_TILE, CHUNK_SHIFT) + 1, 0)


def _visible(i, j):
    k_rows = j * ROW_TILE + lax.broadcasted_iota(jnp.int32, (ROW_TILE, 1), 0)
    q_rows = i * ROW_TILE + lax.broadcasted_iota(jnp.int32, (1, ROW_TILE), 1)
    return jnp.logical_and(_chunk_of(q_rows) >= _chunk_of(k_rows), k_rows >= DEAD)


def _attn_fwd(q_t, k, v_t, n_rows):
    nt = n_rows // ROW_TILE
    scale = QK_DIM ** -0.5

    def body(qt_ref, k_ref, vt_ref, ot_ref, lse_ref):
        i = pl.program_id(0)
        q_ts = [qt_ref[h] for h in range(N_HEADS)]

        def make_step(masked):
            def step(j, carry):
                rows = pl.ds(pl.multiple_of(j * ROW_TILE, ROW_TILE), ROW_TILE)
                scores = [_dot(k_ref[h, rows, :], q_ts[h]) for h in range(N_HEADS)]
                visible = _visible(i, j) if masked else None
                probs, state = [], []
                for h in range(N_HEADS):
                    m, l, _ = carry[h]
                    s = scores[h] * scale
                    if masked:
                        s = jnp.where(visible, s, NEG)
                    m_new = jnp.maximum(m, jnp.max(s, axis=0, keepdims=True))
                    alpha = jnp.exp(m - m_new)
                    p = jnp.exp(s - m_new)
                    probs.append(p.astype(BF16))
                    state.append((m_new, alpha * l + jnp.sum(p, axis=0, keepdims=True), alpha))
                outs = [_dot(vt_ref[h, :, rows], probs[h]) for h in range(N_HEADS)]
                return tuple((state[h][0], state[h][1], state[h][2] * carry[h][2] + outs[h]) for h in range(N_HEADS))
            return step

        init = tuple((jnp.full((1, ROW_TILE), NEG, F32), jnp.zeros((1, ROW_TILE), F32),
                      jnp.zeros((V_HEAD, ROW_TILE), F32)) for _ in range(N_HEADS))
        carry = make_step(True)(0, init)
        carry = lax.fori_loop(1, i, make_step(False), carry)
        carry = lax.fori_loop(jnp.maximum(i, 1), i + 1, make_step(True), carry)
        for h in range(N_HEADS):
            m, l, acc = carry[h]
            ot_ref[h] = acc / l
            lse_ref[h] = m + jnp.log(l)

    out_shapes = [jax.ShapeDtypeStruct((N_HEADS, V_HEAD, n_rows), F32), jax.ShapeDtypeStruct((N_HEADS, 1, n_rows), F32)]
    return pl.pallas_call(
        body, name="attn_fwd", grid=(nt,),
        in_specs=[_lane_tile(q_t.shape), _whole_spec(k.shape), _whole_spec(v_t.shape)],
        out_specs=[_lane_tile(s.shape) for s in out_shapes],
        out_shape=out_shapes,
        compiler_params=_params("parallel"),
    )(q_t, k, v_t)


def _heads_to_rows(ref):
    return jnp.concatenate([ref[h] for h in range(N_HEADS)], axis=0)


def _rms_cols(x, g_col):
    r = lax.rsqrt(jnp.mean(x * x, axis=0, keepdims=True) + EPS)
    return x * r * g_col


def _fwd_out(x, meta_pad, mix_a, o_t, gb_col, w_out, n_rows):
    nt = n_rows // ROW_TILE

    def body(x_ref, meta_ref, mixa_ref, ot_ref, gb_ref, w_ref, mixbt_ref, h1_ref):
        i = pl.program_id(0)
        h0 = jnp.where(i == 0, meta_ref[...], x_ref[...])
        mix_bt = _rms_cols(_heads_to_rows(ot_ref), gb_ref[...]).astype(BF16)
        mixbt_ref[...] = mix_bt
        h1_ref[...] = h0 + _dot(mixa_ref[...], w_ref[:D_CONV, :]) + _dot_tn(mix_bt, w_ref[D_CONV:, :])

    out_shapes = [jax.ShapeDtypeStruct((D_ATTN, n_rows), BF16), jax.ShapeDtypeStruct((n_rows, D_MODEL), F32)]
    return pl.pallas_call(
        body, name="fwd_out", grid=(nt,),
        in_specs=[_real_spec(D_MODEL), _whole_spec(meta_pad.shape), _tile_spec(mix_a.shape), _lane_tile(o_t.shape),
                  _whole_spec(gb_col.shape), _whole_spec(w_out.shape)],
        out_specs=[_lane_tile(out_shapes[0].shape), _tile_spec(out_shapes[1].shape)],
        out_shape=out_shapes,
        compiler_params=_params("parallel"),
    )(x, meta_pad, mix_a, o_t, gb_col, w_out)


PHASES = 8
PHASE_ROWS = ROW_TILE // PHASES
UP_PLANES = -(-UP_SLAB // _LANES)
UP_PAD = UP_PLANES * _LANES
CONV_PLANES = D_CONV // _LANES


def _phase(start):
    return pl.ds(start, PHASE_ROWS, stride=PHASES)


def _to_planes(ref, lead, rows, value):
    width = value.shape[-1]
    for c in range(-(-width // _LANES)):
        part = value[:, c * _LANES:min((c + 1) * _LANES, width)]
        if part.shape[-1] < _LANES:
            part = jnp.concatenate([part, jnp.zeros((part.shape[0], _LANES - part.shape[-1]), part.dtype)], axis=-1)
        ref[(*lead, c, rows, slice(None))] = part


def _from_planes(ref, lead, width):
    planes = [ref[(*lead, c)] for c in range(-(-width // _LANES))]
    last = width - (len(planes) - 1) * _LANES
    return jnp.concatenate(planes[:-1] + [planes[-1][:, :last]], axis=-1)


def _fwd_ffn(h1, target, g2, w_up, fw, fb, w_down, gf, n_rows):
    nt = n_rows // ROW_TILE

    def body(h1_ref, t_ref, g2_ref, wup_ref, fw_ref, fb_ref, wdn_ref, gf_ref,
             n2_ref, up0_ref, act_ref, da_ref, db_ref, dh2_ref, loss_ref, dgf_ref, ext_ref):
        i = pl.program_id(0)

        @pl.when(i == 0)
        def _():
            ext_ref[:, 0:FFN_HALO, :] = jnp.zeros((N_DEV, FFN_HALO, UP_SLAB), F32)

        h1_t = h1_ref[...]
        n2 = _rms_fwd(h1_t, g2_ref[...]).astype(BF16)
        n2_ref[...] = n2
        live = _row_ids(i, ROW_TILE) >= DEAD
        for s in range(N_DEV):
            up0 = jnp.where(live, _dot(n2, wup_ref[s]), 0.0).astype(BF16)
            up0_ref[s] = up0
            ext_ref[s, FFN_HALO:, :] = up0.astype(F32)
        first = FFN_HALO - (FFN_CONV_WIDTH - 1)

        def conv(s):
            acc = fb_ref[s, :, :UP_SLAB]
            for k in range(FFN_CONV_WIDTH):
                acc = acc + fw_ref[s, k:k + 1, :UP_SLAB] * ext_ref[s, first + k:first + k + ROW_TILE, :]
            return acc

        h2 = h1_t
        for s in range(N_ACT_SLAB):
            gate = conv(s)
            val = conv(s + N_ACT_SLAB)
            sg = _sigmoid(gate)
            silu = gate * sg
            act = (silu * val).astype(BF16)
            act_ref[s] = act
            da_ref[s] = (val * sg * (1.0 + gate * (1.0 - sg))).astype(BF16)
            db_ref[s] = silu.astype(BF16)
            h2 = h2 + _dot(act, wdn_ref[s])
        ext_ref[:, 0:FFN_HALO, :] = ext_ref[:, ROW_TILE:ROW_TILE + FFN_HALO, :]

        gf_t = gf_ref[...]
        y = _rms_fwd(h2, gf_t)
        diff = jnp.where(i >= 1, y - t_ref[...], 0.0)
        tile_loss = 0.5 * jnp.sum(jnp.sum(diff * diff, axis=-1, keepdims=True), axis=0, keepdims=True) / D_MODEL
        dh2, dgf = _rms_bwd(diff / D_MODEL, h2, gf_t)
        dh2_ref[...] = dh2
        _accumulate(loss_ref, i == 0, jnp.broadcast_to(tile_loss, loss_ref.shape))
        _accumulate(dgf_ref, i == 0, dgf)

    act_like = jax.ShapeDtypeStruct((N_ACT_SLAB, n_rows, UP_SLAB), BF16)
    out_shapes = [
        jax.ShapeDtypeStruct((n_rows, D_MODEL), BF16),
        jax.ShapeDtypeStruct((N_DEV, n_rows, UP_SLAB), BF16),
        act_like, act_like, act_like,
        jax.ShapeDtypeStruct((n_rows, D_MODEL), F32),
        jax.ShapeDtypeStruct((8, 128), F32),
        jax.ShapeDtypeStruct((1, D_MODEL), F32),
    ]
    whole = [g2, w_up, fw, fb, w_down, gf]
    return pl.pallas_call(
        body, name="fwd_ffn", grid=(nt,),
        in_specs=[_tile_spec(h1.shape), _real_spec(D_MODEL)] + [_whole_spec(a.shape) for a in whole],
        out_specs=[_tile_spec(s.shape) for s in out_shapes[:6]] + [_acc_spec(s.shape) for s in out_shapes[6:]],
        out_shape=out_shapes,
        scratch_shapes=[pltpu.VMEM((N_DEV, ROW_TILE + FFN_HALO, UP_SLAB), F32)],
        compiler_params=_params("arbitrary"),
    )(h1, target, *whole)


def _rope_tables(n_rows):
    pos = jnp.maximum(jnp.arange(n_rows, dtype=jnp.int32) - DEAD, 0)
    inv_freq = 1.0 / (ROPE_THETA ** (jnp.arange(0, QK_ROPE, 2, dtype=F32) / QK_ROPE))
    ang = pos.astype(F32)[:, None] * inv_freq[None, :]
    return jnp.cos(ang), jnp.sin(ang)


def _halo_after(shape, halo, n_rows):
    last = n_rows // halo - 1
    step = ROW_TILE // halo
    if len(shape) == 2:
        return pl.BlockSpec((halo, shape[1]), lambda i: (jnp.minimum((i + 1) * step, last), 0))
    return pl.BlockSpec((shape[0], halo, shape[2]), lambda i: (0, jnp.minimum((i + 1) * step, last), 0))


def _halo_before(shape, halo):
    step = ROW_TILE // halo
    if len(shape) == 2:
        return pl.BlockSpec((halo, shape[1]), lambda i: (jnp.maximum(i * step - 1, 0), 0))
    return pl.BlockSpec((shape[0], halo, shape[2]), lambda i: (0, jnp.maximum(i * step - 1, 0), 0))


def _bwd_ffn_act(dh2, da, db, w_down, n_rows):
    nt = n_rows // ROW_TILE

    def body(dh2_ref, da_ref, db_ref, wdn_ref, dup_ref, dfb_ref):
        i = pl.program_id(0)

        @pl.when(i == 0)
        def _():
            dfb_ref[...] = jnp.zeros_like(dfb_ref)

        dh2_b = dh2_ref[...].astype(BF16)
        for s in range(N_ACT_SLAB):
            d_act = _dot_nt(dh2_b, wdn_ref[s])
            d_gate = d_act * da_ref[s].astype(F32)
            d_val = d_act * db_ref[s].astype(F32)
            dup_ref[s] = d_gate.astype(BF16)
            dup_ref[s + N_ACT_SLAB] = d_val.astype(BF16)
            dfb_ref[s] += jnp.sum(d_gate, axis=0, keepdims=True)
            dfb_ref[s + N_ACT_SLAB] += jnp.sum(d_val, axis=0, keepdims=True)

    out_shapes = [jax.ShapeDtypeStruct((N_DEV, n_rows, UP_SLAB), BF16), jax.ShapeDtypeStruct((N_DEV, 1, UP_SLAB), F32)]
    return pl.pallas_call(
        body, name="bwd_ffn_act", grid=(nt,),
        in_specs=[_tile_spec(dh2.shape), _tile_spec(da.shape), _tile_spec(db.shape), _whole_spec(w_down.shape)],
        out_specs=[_tile_spec(out_shapes[0].shape), _acc_spec(out_shapes[1].shape)],
        out_shape=out_shapes,
        compiler_params=_params("arbitrary"),
    )(dh2, da, db, w_down)


def _bwd_ffn_up(dup, up0, h1, dh2, g2, w_up, fw, n_rows):
    nt = n_rows // ROW_TILE
    last_tap = FFN_CONV_WIDTH - 1

    def body(dup_ref, dnext_ref, up0_ref, h1_ref, dh2_ref, g2_ref, wup_ref, fw_ref,
             dup0_ref, dh1_ref, dfw_ref, dg2_ref, dext_ref, uext_ref, conv_ref):
        i = pl.program_id(0)

        @pl.when(i == 0)
        def _():
            dfw_ref[...] = jnp.zeros_like(dfw_ref)

        live = _row_ids(i, ROW_TILE) >= DEAD
        dn2 = jnp.zeros((ROW_TILE, D_MODEL), F32)
        for s in range(N_DEV):
            _to_planes(dext_ref, (), slice(0, ROW_TILE), dup_ref[s].astype(F32))
            _to_planes(dext_ref, (), slice(ROW_TILE, None), jnp.where(i == nt - 1, 0.0, dnext_ref[s].astype(F32)))
            _to_planes(uext_ref, (), slice(None), up0_ref[s].astype(F32))
            for c in range(UP_PLANES):
                lanes = slice(c * _LANES, (c + 1) * _LANES)
                taps = [fw_ref[s, k:k + 1, lanes] for k in range(FFN_CONV_WIDTH)]
                sums = [jnp.zeros((PHASE_ROWS, _LANES), F32) for _ in range(FFN_CONV_WIDTH)]
                for p in range(PHASES):
                    u = uext_ref[c, _phase(p), :]
                    acc = jnp.zeros((PHASE_ROWS, _LANES), F32)
                    for k in range(FFN_CONV_WIDTH):
                        shifted = dext_ref[c, _phase(p + last_tap - k), :]
                        acc = acc + taps[k] * shifted
                        sums[k] = sums[k] + shifted * u
                    conv_ref[c, _phase(p), :] = acc
                for k in range(FFN_CONV_WIDTH):
                    dfw_ref[s, k:k + 1, lanes] += jnp.sum(sums[k], axis=0, keepdims=True)
            dup0_b = jnp.where(live, _from_planes(conv_ref, (), UP_SLAB), 0.0).astype(BF16)
            dup0_ref[s] = dup0_b
            dn2 = dn2 + _dot_nt(dup0_b, wup_ref[s])
        dx, dg2 = _rms_bwd(dn2, h1_ref[...], g2_ref[...])
        dh1_ref[...] = dh2_ref[...] + dx
        _accumulate(dg2_ref, i == 0, dg2)

    out_shapes = [
        jax.ShapeDtypeStruct((N_DEV, n_rows, UP_SLAB), BF16),
        jax.ShapeDtypeStruct((n_rows, D_MODEL), F32),
        jax.ShapeDtypeStruct((N_DEV, FFN_CONV_WIDTH, UP_PAD), F32),
        jax.ShapeDtypeStruct((1, D_MODEL), F32),
    ]
    return pl.pallas_call(
        body, name="bwd_ffn_up", grid=(nt,),
        in_specs=[_tile_spec(dup.shape), _halo_after(dup.shape, FFN_HALO, n_rows), _tile_spec(up0.shape),
                  _tile_spec(h1.shape), _tile_spec(dh2.shape),
                  _whole_spec(g2.shape), _whole_spec(w_up.shape), _whole_spec(fw.shape)],
        out_specs=[_tile_spec(s.shape) for s in out_shapes[:2]] + [_acc_spec(s.shape) for s in out_shapes[2:]],
        out_shape=out_shapes,
        scratch_shapes=[pltpu.VMEM((UP_PLANES, ROW_TILE + FFN_HALO, _LANES), F32),
                        pltpu.VMEM((UP_PLANES, ROW_TILE, _LANES), F32), pltpu.VMEM((UP_PLANES, ROW_TILE, _LANES), F32)],
        compiler_params=_params("arbitrary"),
    )(dup, dup, up0, h1, dh2, g2, w_up, fw)


def _bwd_out(dh1, o_t, u1, w_out, gb_col, ln_g, ln_b, ga, n_rows):
    nt = n_rows // ROW_TILE

    def body(dh1_ref, ot_ref, u1_ref, w_ref, gb_ref, lg_ref, lb_ref, ga_ref,
             dot_ref, delta_ref, du1_ref, dgb_ref, dga_ref, dlg_ref, dlb_ref, dcb_ref):
        i = pl.program_id(0)
        dh1_b = dh1_ref[...].astype(BF16)
        o_t = _heads_to_rows(ot_ref)
        gb = gb_ref[...]
        r = lax.rsqrt(jnp.mean(o_t * o_t, axis=0, keepdims=True) + EPS)
        dmix_bt = _dot_nt(w_ref[D_CONV:, :], dh1_b)
        wgt = dmix_bt * gb
        do_t = r * wgt - o_t * (r * r * r) * jnp.mean(wgt * o_t, axis=0, keepdims=True)
        dgb = jnp.sum(dmix_bt * o_t * r, axis=1, keepdims=True)
        for h in range(N_HEADS):
            do_h = do_t[h * V_HEAD:(h + 1) * V_HEAD]
            dot_ref[h] = do_h.astype(BF16)
            delta_ref[h] = jnp.sum(do_h * ot_ref[h], axis=0, keepdims=True)
        lg = lg_ref[...]
        xh, u2, u3, rstd = _conv_chain(u1_ref[...], lg, lb_ref[...])
        du3, dga = _rms_bwd(_dot_nt(dh1_b, w_ref[:D_CONV, :]), u3, ga_ref[...])
        sg = _sigmoid(u2)
        du2 = du3 * sg * (1.0 + u2 * (1.0 - sg))
        dxh = du2 * lg
        du1 = rstd * (dxh - jnp.mean(dxh, axis=-1, keepdims=True) - xh * jnp.mean(dxh * xh, axis=-1, keepdims=True))
        du1_ref[...] = du1
        first = i == 0
        _accumulate(dgb_ref, first, dgb)
        _accumulate(dga_ref, first, dga)
        _accumulate(dlg_ref, first, jnp.sum(du2 * xh, axis=0, keepdims=True))
        _accumulate(dlb_ref, first, jnp.sum(du2, axis=0, keepdims=True))
        _accumulate(dcb_ref, first, jnp.sum(du1, axis=0, keepdims=True))

    out_shapes = [
        jax.ShapeDtypeStruct((N_HEADS, V_HEAD, n_rows), BF16),
        jax.ShapeDtypeStruct((N_HEADS, 1, n_rows), F32),
        jax.ShapeDtypeStruct((n_rows, D_CONV), F32),
        jax.ShapeDtypeStruct((D_ATTN, 1), F32),
    ] + [jax.ShapeDtypeStruct((1, D_CONV), F32)] * 4
    whole = [w_out, gb_col, ln_g, ln_b, ga]
    return pl.pallas_call(
        body, name="bwd_out", grid=(nt,),
        in_specs=[_tile_spec(dh1.shape), _lane_tile(o_t.shape), _tile_spec(u1.shape)] + [_whole_spec(a.shape) for a in whole],
        out_specs=[_lane_tile(out_shapes[0].shape), _lane_tile(out_shapes[1].shape), _tile_spec(out_shapes[2].shape)]
        + [_acc_spec(s.shape) for s in out_shapes[3:]],
        out_shape=out_shapes,
        compiler_params=_params("arbitrary"),
    )(dh1, o_t, u1, *whole)


ATTN_BWD_HEADS = 4


def _attn_bwd(q_t, k, v, do_t, lse, delta, n_rows):
    nt = n_rows // ROW_TILE
    scale = QK_DIM ** -0.5
    hp = ATTN_BWD_HEADS

    def body(k_ref, v_ref, qt_ref, dot_ref, lse_ref, delta_ref, dqt_ref, dk_ref, dv_ref):
        j = pl.program_id(1)

        @pl.when(j == 0)
        def _():
            dqt_ref[...] = jnp.zeros_like(dqt_ref)

        k_ts = [k_ref[h] for h in range(hp)]
        v_ts = [v_ref[h] for h in range(hp)]

        def make_step(masked):
            def step(i, carry):
                cols = pl.ds(pl.multiple_of(i * ROW_TILE, ROW_TILE), ROW_TILE)
                q_is = [qt_ref[h, :, cols] for h in range(hp)]
                do_is = [dot_ref[h, :, cols] for h in range(hp)]
                scores = [_dot(k_ts[h], q_is[h]) for h in range(hp)]
                dps = [_dot(v_ts[h], do_is[h]) for h in range(hp)]
                visible = _visible(i, j) if masked else None
                probs, dss = [], []
                for h in range(hp):
                    s = scores[h] * scale
                    if masked:
                        s = jnp.where(visible, s, NEG)
                    p = jnp.exp(s - lse_ref[h, :, cols])
                    probs.append(p.astype(BF16))
                    dss.append((p * (dps[h] - delta_ref[h, :, cols]) * scale).astype(BF16))
                out = []
                for h in range(hp):
                    dk, dv = carry[h]
                    dv = dv + _dot_nt(probs[h], do_is[h])
                    dk = dk + _dot_nt(dss[h], q_is[h])
                    dqt_ref[h, :, cols] += _dot_tn(k_ts[h], dss[h])
                    out.append((dk, dv))
                return tuple(out)
            return step

        init = tuple((jnp.zeros((ROW_TILE, QK_DIM), F32), jnp.zeros((ROW_TILE, V_HEAD), F32)) for _ in range(hp))
        carry = make_step(True)(j, init)
        carry = lax.fori_loop(jnp.where(j == 0, j + 1, nt), nt, make_step(True), carry)
        carry = lax.fori_loop(jnp.where(j == 0, nt, j + 1), nt, make_step(False), carry)
        for h in range(hp):
            dk_ref[h], dv_ref[h] = carry[h]

    key_tile = lambda w: pl.BlockSpec((hp, ROW_TILE, w), lambda g, j: (g, j, 0))
    all_cols = lambda w: pl.BlockSpec((hp, w, n_rows), lambda g, j: (g, 0, 0))
    out_shapes = [
        jax.ShapeDtypeStruct((N_HEADS, QK_DIM, n_rows), F32),
        jax.ShapeDtypeStruct((N_HEADS, n_rows, QK_DIM), F32),
        jax.ShapeDtypeStruct((N_HEADS, n_rows, V_HEAD), F32),
    ]
    return pl.pallas_call(
        body, name="attn_bwd", grid=(N_HEADS // hp, nt),
        in_specs=[key_tile(QK_DIM), key_tile(V_HEAD), all_cols(QK_DIM), all_cols(V_HEAD), all_cols(1), all_cols(1)],
        out_specs=[all_cols(QK_DIM), key_tile(QK_DIM), key_tile(V_HEAD)],
        out_shape=out_shapes,
        compiler_params=_params("parallel", "arbitrary"),
    )(k, v, q_t, do_t, lse, delta)


def _bwd_qkv(dq_t, dk, dv, cq, ckv, gq, gkv, wq_t, w_ukv, cos, sin, cos_t, sin_t, n_rows):
    nt = n_rows // ROW_TILE

    def body(dqt_ref, dk_ref, dv_ref, cq_ref, ckv_ref, gq_ref, gkv_ref, wqt_ref, wkv_ref, cos_ref, sin_ref,
             cost_ref, sint_ref, dqraw_ref, dkv_ref, dcq_ref, dckv_ref, dkr_ref, dgq_ref, dgkv_ref):
        i = pl.program_id(0)
        cos_rows, sin_rows = cost_ref[...], sint_ref[...]
        dcqn = jnp.zeros((ROW_TILE, Q_LORA), F32)
        dckvn = jnp.zeros((ROW_TILE, KV_LORA), F32)
        dk_rot = jnp.zeros((ROW_TILE, QK_ROPE), F32)
        for h in range(N_HEADS):
            dq_h, dk_h = dqt_ref[h], dk_ref[h]
            dq_raw = jnp.concatenate(
                [dq_h[:QK_NOPE], _rope_rows_t(dq_h[QK_NOPE:], cos_rows, sin_rows)], axis=0).astype(BF16)
            dqraw_ref[h] = dq_raw
            dcqn = dcqn + _dot_tn(dq_raw, wqt_ref[h])
            dkv = jnp.concatenate([dk_h[:, :QK_NOPE], dv_ref[h]], axis=-1).astype(BF16)
            dkv_ref[h] = dkv
            dckvn = dckvn + _dot_nt(dkv, wkv_ref[h])
            dk_rot = dk_rot + dk_h[:, QK_NOPE:]
        dkr_ref[...] = _rope_t(dk_rot, cos_ref[...], sin_ref[...]).astype(BF16)
        dcq, dgq = _rms_bwd(dcqn, cq_ref[...], gq_ref[...])
        dckv, dgkv = _rms_bwd(dckvn, ckv_ref[...], gkv_ref[...])
        dcq_ref[...] = dcq.astype(BF16)
        dckv_ref[...] = dckv.astype(BF16)
        _accumulate(dgq_ref, i == 0, dgq)
        _accumulate(dgkv_ref, i == 0, dgkv)

    out_shapes = [
        jax.ShapeDtypeStruct((N_HEADS, QK_DIM, n_rows), BF16),
        jax.ShapeDtypeStruct((N_HEADS, n_rows, KV_HEAD), BF16),
        jax.ShapeDtypeStruct((n_rows, Q_LORA), BF16),
        jax.ShapeDtypeStruct((n_rows, KV_LORA), BF16),
        jax.ShapeDtypeStruct((n_rows, QK_ROPE), BF16),
        jax.ShapeDtypeStruct((1, Q_LORA), F32),
        jax.ShapeDtypeStruct((1, KV_LORA), F32),
    ]
    tiles = [dk, dv, cq, ckv]
    whole = [gq, gkv, wq_t, w_ukv]
    return pl.pallas_call(
        body, name="bwd_qkv", grid=(nt,),
        in_specs=[_lane_tile(dq_t.shape)] + [_tile_spec(a.shape) for a in tiles] + [_whole_spec(a.shape) for a in whole]
        + [_tile_spec(cos.shape), _tile_spec(sin.shape), _lane_tile(cos_t.shape), _lane_tile(sin_t.shape)],
        out_specs=[_lane_tile(out_shapes[0].shape)] + [_tile_spec(s.shape) for s in out_shapes[1:5]]
        + [_acc_spec(s.shape) for s in out_shapes[5:]],
        out_shape=out_shapes,
        compiler_params=_params("arbitrary"),
    )(dq_t, *tiles, *whole, cos, sin, cos_t, sin_t)


def _bwd_conv(du1, ag, conv_w, n_rows):
    nt = n_rows // ROW_TILE

    last_tap = CONV_WIDTH - 1

    def body(du1_ref, dnext_ref, ag_ref, w_ref, dag_ref, dw_ref, dext_ref, uext_ref, conv_ref, sums_ref):
        i = pl.program_id(0)

        @pl.when(i == 0)
        def _():
            sums_ref[...] = jnp.zeros_like(sums_ref)

        _to_planes(dext_ref, (), slice(0, ROW_TILE), du1_ref[...])
        _to_planes(dext_ref, (), slice(ROW_TILE, None), jnp.where(i == nt - 1, 0.0, dnext_ref[...]))
        ag_t = ag_ref[...]
        live = _row_ids(i, ROW_TILE) >= DEAD
        sg = _sigmoid(ag_t[:, D_CONV:])
        _to_planes(uext_ref, (), slice(None), jnp.where(live, ag_t[:, :D_CONV] * sg, 0.0))
        for c in range(CONV_PLANES):
            taps = w_ref[:, c * _LANES:(c + 1) * _LANES]
            for p in range(PHASES):
                u = uext_ref[c, _phase(p), :]
                acc = jnp.zeros((PHASE_ROWS, _LANES), F32)
                for k in range(CONV_WIDTH):
                    shifted = dext_ref[c, _phase(p + last_tap - k), :]
                    acc = acc + taps[k:k + 1, :] * shifted
                    sums_ref[c, k] += shifted * u
                conv_ref[c, _phase(p), :] = acc
        du0 = jnp.where(live, _from_planes(conv_ref, (), D_CONV), 0.0)
        da = du0 * sg
        dgate = du0 * ag_t[:, :D_CONV] * sg * (1.0 - sg)
        dag_ref[...] = jnp.concatenate([da, dgate], axis=-1).astype(BF16)

        @pl.when(i == nt - 1)
        def _():
            for c in range(CONV_PLANES):
                for k in range(CONV_WIDTH):
                    dw_ref[k:k + 1, c * _LANES:(c + 1) * _LANES] = jnp.sum(sums_ref[c, k], axis=0, keepdims=True)

    out_shapes = [jax.ShapeDtypeStruct((n_rows, 2 * D_CONV), BF16), jax.ShapeDtypeStruct((CONV_WIDTH, D_CONV), F32)]
    return pl.pallas_call(
        body, name="bwd_conv", grid=(nt,),
        in_specs=[_tile_spec(du1.shape), _halo_after(du1.shape, CONV_HALO, n_rows), _tile_spec(ag.shape),
                  _whole_spec(conv_w.shape)],
        out_specs=[_tile_spec(out_shapes[0].shape), _acc_spec(out_shapes[1].shape)],
        out_shape=out_shapes,
        scratch_shapes=[pltpu.VMEM((CONV_PLANES, ROW_TILE + CONV_HALO, _LANES), F32),
                        pltpu.VMEM((CONV_PLANES, ROW_TILE, _LANES), F32), pltpu.VMEM((CONV_PLANES, ROW_TILE, _LANES), F32),
                        pltpu.VMEM((CONV_PLANES, CONV_WIDTH, PHASE_ROWS, _LANES), F32)],
        compiler_params=_params("arbitrary"),
    )(du1, du1, ag, conv_w)


def _bwd_in(dag, dcq, dckv, dkr, x, meta_pad, dh1, g1, w_in, n_rows):
    nt = n_rows // ROW_TILE

    def body(dag_ref, dcq_ref, dckv_ref, dkr_ref, x_ref, meta_ref, dh1_ref, g_ref, w_ref,
             dz_ref, gx_ref, gmeta_ref, dg1_ref):
        i = pl.program_id(0)
        dz = jnp.concatenate([dag_ref[...], dcq_ref[...], dckv_ref[...], dkr_ref[...]], axis=-1)
        dz_ref[...] = dz
        h0 = jnp.where(i == 0, meta_ref[...], x_ref[...])
        dx, dg1 = _rms_bwd(_dot_nt(dz, w_ref[...]), h0, g_ref[...])
        dh0 = dh1_ref[...] + dx
        gx_ref[...] = dh0

        @pl.when(i == 0)
        def _():
            gmeta_ref[...] = dh0

        _accumulate(dg1_ref, i == 0, dg1)

    out_shapes = [
        jax.ShapeDtypeStruct((n_rows, D_IN), BF16),
        jax.ShapeDtypeStruct((n_rows - ROW_TILE, D_MODEL), F32),
        jax.ShapeDtypeStruct((ROW_TILE, D_MODEL), F32),
        jax.ShapeDtypeStruct((1, D_MODEL), F32),
    ]
    tiles = [dag, dcq, dckv, dkr]
    return pl.pallas_call(
        body, name="bwd_in", grid=(nt,),
        in_specs=[_tile_spec(a.shape) for a in tiles]
        + [_real_spec(D_MODEL), _whole_spec(meta_pad.shape), _tile_spec(dh1.shape), _whole_spec(g1.shape), _whole_spec(w_in.shape)],
        out_specs=[_tile_spec(out_shapes[0].shape), _real_spec(D_MODEL), _acc_spec(out_shapes[2].shape), _acc_spec(out_shapes[3].shape)],
        out_shape=out_shapes,
        compiler_params=_params("arbitrary"),
    )(*tiles, x, meta_pad, dh1, g1, w_in)


def _contraction_tile(n_rows):
    return next(t for t in range(n_rows // 2 // _LANES * _LANES, 0, -_LANES) if n_rows % t == 0)


def _weight_grad(a, b, name, a_transposed=False):
    groups = max(a.shape[0] if a.ndim == 3 else 1, b.shape[0] if b.ndim == 3 else 1)
    n_rows, n = b.shape[-2], b.shape[-1]
    m = a.shape[-2] if a_transposed else a.shape[-1]
    kt = _contraction_tile(n_rows)
    steps = n_rows // kt

    def body(a_ref, b_ref, out_ref, acc_ref):
        i = pl.program_id(1)
        a_t, b_t = a_ref[...].astype(BF16), b_ref[...].astype(BF16)
        part = _dot(a_t, b_t) if a_transposed else _dot_tn(a_t, b_t)
        _accumulate(acc_ref, i == 0, part)

        @pl.when(i == steps - 1)
        def _():
            out_ref[...] = acc_ref[...].astype(out_ref.dtype)

    def spec(arr, rows_last):
        block = (arr.shape[-2], kt) if rows_last else (kt, arr.shape[-1])
        at = (lambda i: (0, i)) if rows_last else (lambda i: (i, 0))
        if arr.ndim == 3:
            return pl.BlockSpec((None,) + block, lambda g, i: (g,) + at(i))
        return pl.BlockSpec(block, lambda g, i: at(i))

    return pl.pallas_call(
        body, name=name, grid=(groups, steps),
        in_specs=[spec(a, a_transposed), spec(b, False)],
        out_specs=pl.BlockSpec((None, m, n), lambda g, i: (g, 0, 0)),
        out_shape=jax.ShapeDtypeStruct((groups, m, n), BF16),
        scratch_shapes=[pltpu.VMEM((m, n), F32)],
        compiler_params=_params("parallel", "arbitrary"),
    )(a, b)


def _my_index():
    return 4 * lax.axis_index("x") + 2 * lax.axis_index("y") + lax.axis_index("c")


def _peer(k):
    flip = lambda v, bit: 1 - v if bit else v
    px = flip(lax.axis_index("x"), k & 4)
    py = flip(lax.axis_index("y"), k & 2)
    pc = flip(lax.axis_index("c"), k & 1)
    return (px, py, pc), 4 * px + 2 * py + pc


def _all_gather(shards, dtypes):
    n = len(shards)

    def body(*refs):
        ins, outs, stages = refs[:n], refs[n:2 * n], refs[2 * n:3 * n]
        send_sems, recv_sems, local_sems = refs[3 * n:]
        me = _my_index()
        for a in range(n):
            stages[a][...] = ins[a][...].astype(stages[a].dtype)
        local = [pltpu.make_async_copy(stages[a], outs[a].at[me], local_sems.at[a]) for a in range(n)]
        for cp in local:
            cp.start()

        def copy(a, k, slot):
            peer, _ = _peer(k)
            return pltpu.make_async_remote_copy(
                src_ref=stages[a], dst_ref=outs[a].at[slot], send_sem=send_sems.at[a, k - 1],
                recv_sem=recv_sems.at[a, k - 1], device_id=peer, device_id_type=MESH)

        for k in range(1, N_DEV):
            for a in range(n):
                copy(a, k, me).start()
        for k in range(1, N_DEV):
            for a in range(n):
                copy(a, k, _peer(k)[1]).wait()
        for cp in local:
            cp.wait()

    return pl.pallas_call(
        body, name="gather_weights",
        in_specs=[pl.BlockSpec(memory_space=pltpu.VMEM)] * n,
        out_specs=[pl.BlockSpec(memory_space=pl.ANY)] * n,
        out_shape=[jax.ShapeDtypeStruct((N_DEV,) + s.shape, dt) for s, dt in zip(shards, dtypes)],
        scratch_shapes=[pltpu.VMEM(s.shape, dt) for s, dt in zip(shards, dtypes)]
        + [pltpu.SemaphoreType.DMA((n, N_DEV - 1)), pltpu.SemaphoreType.DMA((n, N_DEV - 1)), pltpu.SemaphoreType.DMA((n,))],
        compiler_params=pltpu.CompilerParams(vmem_limit_bytes=VMEM_LIMIT),
    )(*shards)


def _exchange(parts, whole):
    n = len(parts)

    def body(*refs):
        ins, outs = refs[:n], refs[n:2 * n]
        send_sems, recv_sems, local_sems = refs[2 * n:]
        me = _my_index()

        def src(a, slab):
            return ins[a] if whole[a] else ins[a].at[slab]

        local = [pltpu.make_async_copy(src(a, me), outs[a].at[me], local_sems.at[a]) for a in range(n)]
        for cp in local:
            cp.start()

        def copy(a, k, slab, slot):
            peer, _ = _peer(k)
            return pltpu.make_async_remote_copy(
                src_ref=src(a, slab), dst_ref=outs[a].at[slot], send_sem=send_sems.at[a, k - 1],
                recv_sem=recv_sems.at[a, k - 1], device_id=peer, device_id_type=MESH)

        for k in range(1, N_DEV):
            for a in range(n):
                copy(a, k, _peer(k)[1], me).start()
        for k in range(1, N_DEV):
            for a in range(n):
                copy(a, k, _peer(k)[1], _peer(k)[1]).wait()
        for cp in local:
            cp.wait()

    return pl.pallas_call(
        body, name="exchange_grads",
        in_specs=[pl.BlockSpec(memory_space=pl.ANY)] * n,
        out_specs=[pl.BlockSpec(memory_space=pl.ANY)] * n,
        out_shape=[jax.ShapeDtypeStruct(((N_DEV,) + p.shape) if w else p.shape, p.dtype) for p, w in zip(parts, whole)],
        scratch_shapes=[pltpu.SemaphoreType.DMA((n, N_DEV - 1)), pltpu.SemaphoreType.DMA((n, N_DEV - 1)),
                        pltpu.SemaphoreType.DMA((n,))],
    )(*parts)


def _sequencer_exchange(parts, whole, name, collective_id):
    n = len(parts)
    srcs = [jax.new_ref(p, memory_space=pltpu.MemorySpace.HBM) for p in parts]
    lands = [jax.empty_ref(jax.ShapeDtypeStruct(((N_DEV,) + p.shape) if w else p.shape, p.dtype),
                           memory_space=pltpu.MemorySpace.HBM) for p, w in zip(parts, whole)]

    @pl.kernel(mesh=plsc.ScalarSubcoreMesh(axis_name="sequencer", num_cores=1), name=name,
               scratch_types=(pltpu.SemaphoreType.DMA((n, N_DEV - 1)), pltpu.SemaphoreType.DMA((n, N_DEV - 1)),
                              pltpu.SemaphoreType.DMA((n,))),
               compiler_params=pltpu.CompilerParams(collective_id=collective_id))
    def launch(send_sems, recv_sems, local_sems):
        barrier = pltpu.get_barrier_semaphore()
        for k in range(1, N_DEV):
            pl.semaphore_signal(barrier, inc=1, device_id=_peer(k)[0], device_id_type=MESH)
        pl.semaphore_wait(barrier, N_DEV - 1)
        me = _my_index()

        def src(a, slab):
            return srcs[a] if whole[a] else srcs[a].at[slab]

        local = [pltpu.make_async_copy(src(a, me), lands[a].at[me], local_sems.at[a]) for a in range(n)]
        for cp in local:
            cp.start()

        def copy(a, k, slab, slot):
            return pltpu.make_async_remote_copy(
                src_ref=src(a, slab), dst_ref=lands[a].at[slot], send_sem=send_sems.at[a, k - 1],
                recv_sem=recv_sems.at[a, k - 1], device_id=_peer(k)[0], device_id_type=MESH)

        for k in range(1, N_DEV):
            for a in range(n):
                copy(a, k, _peer(k)[1], me).start()
        for k in range(1, N_DEV):
            for a in range(n):
                copy(a, k, _peer(k)[1], _peer(k)[1]).wait()
        for cp in local:
            cp.wait()

    launch()
    return [land[...] for land in lands]


def _row_block(rows):
    if rows <= ROW_TILE:
        return rows
    return next(rb for rb in range(ROW_TILE, 0, -16) if rows % rb == 0)


def _adamw(landing, w, m, v, name):
    rows, cols = w.shape
    rb = _row_block(rows)

    def body(l_ref, w_ref, m_ref, v_ref, g_ref, d_ref, m2_ref, v2_ref):
        g = l_ref[0].astype(F32)
        for p in range(1, N_DEV):
            g = g + l_ref[p].astype(F32)
        m2 = ADAM_B1 * m_ref[...] + (1.0 - ADAM_B1) * g
        v2 = ADAM_B2 * v_ref[...] + (1.0 - ADAM_B2) * (g * g)
        m_hat = m2 / (1.0 - ADAM_B1 ** ADAM_STEP)
        v_hat = v2 / (1.0 - ADAM_B2 ** ADAM_STEP)
        g_ref[...] = g
        d_ref[...] = -ADAM_LR * (m_hat / (jnp.sqrt(v_hat) + ADAM_EPS) + ADAM_WD * w_ref[...])
        m2_ref[...] = m2
        v2_ref[...] = v2

    flat = pl.BlockSpec((rb, cols), lambda i: (i, 0))
    return pl.pallas_call(
        body, name=name, grid=(rows // rb,),
        in_specs=[pl.BlockSpec((N_DEV, rb, cols), lambda i: (0, i, 0)), flat, flat, flat],
        out_specs=[flat] * 4,
        out_shape=[jax.ShapeDtypeStruct((rows, cols), F32)] * 4,
        compiler_params=_params("parallel"),
    )(landing, w, m, v)


_REPLICATED = (
    ("mix_norm_g", D_MODEL), ("q_norm_g", Q_LORA), ("kv_norm_g", KV_LORA), ("conv_b", D_CONV), ("conv_ln_g", D_CONV),
    ("conv_ln_b", D_CONV), ("conv_out_g", D_CONV), ("attn_out_g", D_CONV), ("ffn_norm_g", D_MODEL),
    ("ffn_conv_b", D_UP), ("final_norm_g", D_MODEL),
)
_PACK_ROWS = 104

_WEIGHT_ORDER = (
    "meta_tokens", "mix_norm_g", "w_in", "q_norm_g", "w_uq", "kv_norm_g", "w_ukv", "conv_w", "conv_b", "conv_ln_g",
    "conv_ln_b", "conv_out_g", "attn_out_g", "w_out", "ffn_norm_g", "w_ffn_up", "ffn_conv_w", "ffn_conv_b",
    "w_ffn_down", "final_norm_g",
)


def _pack(vectors):
    flat = jnp.concatenate([vectors[name].reshape(-1) for name, _ in _REPLICATED])
    return jnp.pad(flat, (0, _PACK_ROWS * _LANES - flat.shape[0])).reshape(_PACK_ROWS, _LANES)


def _unpack(packed, like):
    flat, out, at = packed.reshape(-1), {}, 0
    for name, size in _REPLICATED:
        out[name] = flat[at:at + size].reshape(like[name].shape)
        at += size
    return out


def _pad_rows(a, rows):
    return jnp.pad(a, ((0, rows - a.shape[0]), (0, 0)))


def _slabs(a):
    r, c = a.shape
    return a.reshape(r, N_DEV, c // N_DEV).transpose(1, 0, 2)


def _unslab(a):
    g, r, c = a.shape
    return a.transpose(1, 0, 2).reshape(r, g * c)


def _local_step(x, target, w, n_rows, ffn_weights, send_early_grads):
    cos, sin = _rope_tables(n_rows)
    cos_t, sin_t = cos.T, sin.T
    meta_pad, g1, gf = w["meta_pad"], w["mix_norm_g"], w["final_norm_g"]
    gq, gkv, gb_col = w["q_norm_g"], w["kv_norm_g"], w["attn_out_g"].reshape(D_ATTN, 1)
    nb, ag, cq, ckv, kr = _fwd_in(x, meta_pad, g1, w["w_in"], n_rows)
    mix_a, u1 = _fwd_conv(ag, w["conv_w"], w["conv_b"], w["conv_ln_g"], w["conv_ln_b"], w["conv_out_g"], n_rows)
    q_t, k, v, v_t, cqn, ckvn = _fwd_qkv(cq, ckv, kr, gq, gkv, w["wq_t"], w["w_ukv"], w["wv_t"], cos, sin, cos_t, sin_t, n_rows)
    o_t, lse = _attn_fwd(q_t, k, v_t, n_rows)
    w_out, w_up, w_down = ffn_weights()
    mix_bt, h1 = _fwd_out(x, meta_pad, mix_a, o_t, gb_col, w_out, n_rows)
    n2, up0, act, da, db, dh2, loss, dgf = _fwd_ffn(
        h1, target, w["ffn_norm_g"], w_up, w["fw"], w["fb"], w_down, gf, n_rows)

    dup, dfb = _bwd_ffn_act(dh2, da, db, w_down, n_rows)
    dup0, dh1, dfw, dg2 = _bwd_ffn_up(dup, up0, h1, dh2, w["ffn_norm_g"], w_up, w["fw"], n_rows)
    grad_w_out = jnp.concatenate([_weight_grad(mix_a, dh1, "grad_w_out_conv")[0],
                                  _weight_grad(mix_bt, dh1, "grad_w_out_attn", a_transposed=True)[0]], axis=0)
    send_early_grads(_weight_grad(n2, dup0, "grad_w_ffn_up"),
                     _weight_grad(act, dh2, "grad_w_ffn_down").reshape(N_DEV, D_FF // N_DEV, D_MODEL),
                     grad_w_out.reshape(N_DEV, D_MODEL // N_DEV, D_MODEL))
    do_t, delta, du1, dgb, dga, dlg, dlb, dcb = _bwd_out(
        dh1, o_t, u1, w_out, gb_col, w["conv_ln_g"], w["conv_ln_b"], w["conv_out_g"], n_rows)
    dq_t, dk, dv = _attn_bwd(q_t, k, v, do_t, lse, delta, n_rows)
    dqraw_t, dkv, dcq, dckv, dkr, dgq, dgkv = _bwd_qkv(
        dq_t, dk, dv, cq, ckv, gq, gkv, w["wq_t"], w["w_ukv"], cos, sin, cos_t, sin_t, n_rows)
    dag, dcw = _bwd_conv(du1, ag, w["conv_w"], n_rows)
    dz, gx, gmeta, dg1 = _bwd_in(dag, dcq, dckv, dkr, x, meta_pad, dh1, g1, w["w_in"], n_rows)

    sharded = {
        "w_in": _slabs(_weight_grad(nb, dz, "grad_w_in")[0]),
        "w_uq": _weight_grad(dqraw_t, cqn, "grad_w_uq", a_transposed=True).transpose(0, 2, 1),
        "w_ukv": _weight_grad(ckvn, dkv, "grad_w_ukv"),
        "conv_w": _slabs(dcw),
        "ffn_conv_w": dfw[:, :, :UP_SLAB],
        "meta_tokens": _slabs(gmeta[DEAD:]),
    }
    replicated = {
        "mix_norm_g": dg1, "q_norm_g": dgq, "kv_norm_g": dgkv, "conv_b": dcb, "conv_ln_g": dlg, "conv_ln_b": dlb,
        "conv_out_g": dga, "attn_out_g": dgb, "ffn_norm_g": dg2, "ffn_conv_b": dfb, "final_norm_g": dgf,
    }
    return loss[0, 0], gx, sharded, replicated


_SHARDED = (
    ("w_in", None, BF16), ("w_uq", None, BF16), ("w_ukv", None, BF16), ("w_out", None, BF16), ("w_ffn_up", None, BF16),
    ("w_ffn_down", None, BF16), ("conv_w", 32, F32), ("ffn_conv_w", 8, F32), ("meta_tokens", None, F32),
)
GATHER_LATE_ID = 3
EXCHANGE_EARLY_ID = 4
_LATE_WEIGHTS = ("w_out", "w_ffn_up", "w_ffn_down")
_EARLY_GRADS = ("w_ffn_up", "w_ffn_down", "w_out")


def kernel(x, meta_tokens, mix_norm_g, w_in, q_norm_g, w_uq, kv_norm_g, w_ukv, conv_w, conv_b, conv_ln_g, conv_ln_b, conv_out_g, attn_out_g, w_out, ffn_norm_g, w_ffn_up, ffn_conv_w, ffn_conv_b, w_ffn_down, final_norm_g, loss_target, m_meta_tokens, m_mix_norm_g, m_w_in, m_q_norm_g, m_w_uq, m_kv_norm_g, m_w_ukv, m_conv_w, m_conv_b, m_conv_ln_g, m_conv_ln_b, m_conv_out_g, m_attn_out_g, m_w_out, m_ffn_norm_g, m_w_ffn_up, m_ffn_conv_w, m_ffn_conv_b, m_w_ffn_down, m_final_norm_g, v_meta_tokens, v_mix_norm_g, v_w_in, v_q_norm_g, v_w_uq, v_kv_norm_g, v_w_ukv, v_conv_w, v_conv_b, v_conv_ln_g, v_conv_ln_b, v_conv_out_g, v_attn_out_g, v_w_out, v_ffn_norm_g, v_w_ffn_up, v_ffn_conv_w, v_ffn_conv_b, v_w_ffn_down, v_final_norm_g):
    given = dict(locals())
    weights = {name: given[name] for name in _WEIGHT_ORDER}
    moments_m = {name: given["m_" + name] for name in _WEIGHT_ORDER}
    moments_v = {name: given["v_" + name] for name in _WEIGHT_ORDER}
    seq = x.shape[1]
    n_rows = ROW_TILE + seq

    def shard2d(a):
        return a.reshape(a.shape[-2], a.shape[-1])

    early = [entry for entry in _SHARDED if entry[0] not in _LATE_WEIGHTS]
    shards = []
    for name, pad_to, _ in early:
        s = shard2d(weights[name])
        shards.append(s if pad_to is None else _pad_rows(s, pad_to))
    gathered = dict(zip([name for name, _, _ in early], _all_gather(shards, [dt for _, _, dt in early])))
    behind = gathered["meta_tokens"][0, 0, 0] * 0.0
    late_parts = [(shard2d(weights[name]) + behind).astype(BF16) for name in _LATE_WEIGHTS]
    late = _sequencer_exchange(late_parts, [True] * len(late_parts), "gather_late", GATHER_LATE_ID)
    meta_full = _unslab(gathered["meta_tokens"])
    full = {
        "meta_pad": jnp.concatenate([jnp.zeros((DEAD, D_MODEL), F32), meta_full], axis=0),
        "w_in": _unslab(gathered["w_in"]),
        "wq_t": gathered["w_uq"].transpose(0, 2, 1),
        "w_ukv": gathered["w_ukv"],
        "wv_t": gathered["w_ukv"][:, :, QK_NOPE:].transpose(0, 2, 1),
        "conv_w": _unslab(gathered["conv_w"][:, :CONV_WIDTH]),
        "fw": jnp.pad(gathered["ffn_conv_w"][:, :FFN_CONV_WIDTH], ((0, 0), (0, 0), (0, UP_PAD - UP_SLAB))),
        "fb": jnp.pad(ffn_conv_b.reshape(N_DEV, 1, UP_SLAB), ((0, 0), (0, 0), (0, UP_PAD - UP_SLAB))),
        "final_norm_g": final_norm_g.reshape(1, D_MODEL),
    }
    for name in ("mix_norm_g", "q_norm_g", "kv_norm_g", "conv_b", "conv_ln_g", "conv_ln_b", "conv_out_g", "attn_out_g",
                 "ffn_norm_g"):
        full[name] = weights[name]

    def ffn_weights():
        w_out_all, w_up_all, w_down_all = late
        return (w_out_all.reshape(D_MODEL, D_MODEL), w_up_all, w_down_all.reshape(N_ACT_SLAB, UP_SLAB, D_MODEL))

    early_landed = []

    def send_early_grads(*grads):
        early_landed.extend(_sequencer_exchange(list(grads), [False] * len(grads), "exchange_early", EXCHANGE_EARLY_ID))

    loss, gx, sharded, replicated = _local_step(x[0], loss_target[0], full, n_rows, ffn_weights, send_early_grads)
    loss = lax.psum(loss, ("x", "y", "c"))

    rest = [entry for entry in _SHARDED if entry[0] not in _EARLY_GRADS]
    parts, whole = [], []
    for name, pad_to, dt in rest:
        p = sharded[name].astype(dt)
        parts.append(p if pad_to is None else jnp.pad(p, ((0, 0), (0, pad_to - p.shape[1]), (0, 0))))
        whole.append(False)
    parts.append(_pack(replicated))
    whole.append(True)
    landed = _exchange(parts, whole)
    landing = dict(zip([name for name, _, _ in rest], landed[:-1]))
    landing.update(zip(_EARLY_GRADS, early_landed))

    grad, delta, new_m, new_v = {}, {}, {}, {}
    for name, pad_to, _ in _SHARDED:
        land = landing[name]
        ws, ms, vs = shard2d(weights[name]), shard2d(moments_m[name]), shard2d(moments_v[name])
        rows = ws.shape[0]
        if pad_to is not None:
            ws, ms, vs = _pad_rows(ws, pad_to), _pad_rows(ms, pad_to), _pad_rows(vs, pad_to)
        outs = _adamw(land, ws, ms, vs, "adamw_" + name)
        shape = weights[name].shape
        grad[name], delta[name], new_m[name], new_v[name] = (o[:rows].reshape(shape) for o in outs)
    outs = _adamw(landed[-1], _pack(weights), _pack(moments_m), _pack(moments_v), "adamw_replicated")
    for store, packed in zip((grad, delta, new_m, new_v), outs):
        store.update(_unpack(packed, weights))

    return (loss, gx[None], *[grad[n] for n in _WEIGHT_ORDER], *[delta[n] for n in _WEIGHT_ORDER],
            *[new_m[n] for n in _WEIGHT_ORDER], *[new_v[n] for n in _WEIGHT_ORDER])
```

```python
import functools

import jax
import jax.numpy as jnp
from jax import lax
from jax.experimental import pallas as pl
from jax.experimental.pallas import tpu as pltpu
from jax.experimental.pallas import tpu_sc as plsc

F32 = jnp.float32
BF16 = jnp.bfloat16

N_DEV = 8
D_MODEL = 1024
CHUNK = 64
CHUNK_SHIFT = 6
N_META = 16
D_CONV = 512
CONV_WIDTH = 31
N_HEADS = 8
QK_NOPE = 64
QK_ROPE = 32
QK_DIM = QK_NOPE + QK_ROPE
V_HEAD = 64
KV_HEAD = QK_NOPE + V_HEAD
D_ATTN = N_HEADS * V_HEAD
Q_LORA = 384
KV_LORA = 256
ROPE_THETA = 10000.0
D_IN = 2 * D_CONV + Q_LORA + KV_LORA + QK_ROPE
D_FF = 2816
D_UP = 2 * D_FF
FFN_CONV_WIDTH = 3
UP_SLAB = D_UP // N_DEV
N_ACT_SLAB = D_FF // UP_SLAB
EPS = 1e-6
NEG = -1e30
ADAM_LR = 0.001
ADAM_B1 = 0.9
ADAM_B2 = 0.999
ADAM_EPS = 1e-08
ADAM_WD = 0.01
ADAM_STEP = 10

ROW_TILE = 256
DEAD = ROW_TILE - N_META
CONV_HALO = 32
FFN_HALO = 16
VMEM_LIMIT = 56 * 1024 * 1024
_LANES = 128

MESH = pl.DeviceIdType.MESH


def _dot(a, b):
    return jnp.dot(a, b, preferred_element_type=F32)


def _dot_nt(a, b):
    return lax.dot_general(a, b, (((1,), (1,)), ((), ())), preferred_element_type=F32)


def _dot_tn(a, b):
    return lax.dot_general(a, b, (((0,), (0,)), ((), ())), preferred_element_type=F32)


def _sigmoid(x):
    return 1.0 / (1.0 + jnp.exp(-x))


def _rms_fwd(x, g):
    r = lax.rsqrt(jnp.mean(x * x, axis=-1, keepdims=True) + EPS)
    return x * r * g


def _rms_bwd(dy, x, g):
    r = lax.rsqrt(jnp.mean(x * x, axis=-1, keepdims=True) + EPS)
    w = dy * g
    dx = r * w - x * (r * r * r) * jnp.mean(w * x, axis=-1, keepdims=True)
    return dx, jnp.sum(dy * x * r, axis=0, keepdims=True)


def _rope(x, cos, sin):
    half = QK_ROPE // 2
    x1, x2 = x[:, :half], x[:, half:]
    return jnp.concatenate([x1 * cos - x2 * sin, x2 * cos + x1 * sin], axis=-1)


def _rope_t(dy, cos, sin):
    half = QK_ROPE // 2
    d1, d2 = dy[:, :half], dy[:, half:]
    return jnp.concatenate([d1 * cos + d2 * sin, d2 * cos - d1 * sin], axis=-1)


def _row_ids(i, rows):
    return i * rows + lax.broadcasted_iota(jnp.int32, (rows, 1), 0)


def _accumulate(ref, first, value):
    @pl.when(first)
    def _():
        ref[...] = value

    @pl.when(jnp.logical_not(first))
    def _():
        ref[...] += value


def _tile_spec(shape):
    nd = len(shape)
    if nd == 2:
        return pl.BlockSpec((ROW_TILE, shape[1]), lambda i: (i, 0))
    return pl.BlockSpec((shape[0], ROW_TILE, shape[2]), lambda i: (0, i, 0))


def _whole_spec(shape):
    nd = len(shape)
    return pl.BlockSpec(tuple(shape), lambda i: (0,) * nd, pipeline_mode=pl.Buffered(1))


def _acc_spec(shape):
    nd = len(shape)
    return pl.BlockSpec(tuple(shape), lambda i: (0,) * nd)


def _real_spec(width):
    return pl.BlockSpec((ROW_TILE, width), lambda i: (jnp.maximum(i - 1, 0), 0))


def _params(*semantics):
    return pltpu.CompilerParams(dimension_semantics=semantics, vmem_limit_bytes=VMEM_LIMIT)


def _fwd_in(x, meta_pad, g1, w_in, n_rows):
    nt = n_rows // ROW_TILE

    def body(x_ref, meta_ref, g_ref, w_ref, nb_ref, ag_ref, cq_ref, ckv_ref, kr_ref):
        i = pl.program_id(0)
        h0 = jnp.where(i == 0, meta_ref[...], x_ref[...])
        nb = _rms_fwd(h0, g_ref[...]).astype(BF16)
        nb_ref[...] = nb
        z = _dot_nt(nb, w_ref[...])
        ag_ref[...] = z[:, :2 * D_CONV]
        cq_ref[...] = z[:, 2 * D_CONV:2 * D_CONV + Q_LORA]
        ckv_ref[...] = z[:, 2 * D_CONV + Q_LORA:2 * D_CONV + Q_LORA + KV_LORA]
        kr_ref[...] = z[:, 2 * D_CONV + Q_LORA + KV_LORA:]

    out_shapes = [
        jax.ShapeDtypeStruct((n_rows, D_MODEL), BF16),
        jax.ShapeDtypeStruct((n_rows, 2 * D_CONV), F32),
        jax.ShapeDtypeStruct((n_rows, Q_LORA), F32),
        jax.ShapeDtypeStruct((n_rows, KV_LORA), F32),
        jax.ShapeDtypeStruct((n_rows, QK_ROPE), F32),
    ]
    return pl.pallas_call(
        body, name="fwd_in", grid=(nt,),
        in_specs=[_real_spec(D_MODEL), _whole_spec(meta_pad.shape), _whole_spec(g1.shape), _whole_spec(w_in.shape)],
        out_specs=[_tile_spec(s.shape) for s in out_shapes],
        out_shape=out_shapes,
        compiler_params=_params("parallel"),
    )(x, meta_pad, g1, w_in)


def _conv_chain(u1, ln_g, ln_b):
    mu = jnp.mean(u1, axis=-1, keepdims=True)
    xc = u1 - mu
    rstd = lax.rsqrt(jnp.mean(xc * xc, axis=-1, keepdims=True) + EPS)
    xh = xc * rstd
    u2 = xh * ln_g + ln_b
    return xh, u2, u2 * _sigmoid(u2), rstd


def _fwd_conv(ag, conv_w, conv_b, ln_g, ln_b, out_g, n_rows):
    nt = n_rows // ROW_TILE

    def body(ag_ref, w_ref, b_ref, lg_ref, lb_ref, og_ref, mix_ref, u1_ref, ext_ref, conv_ref):
        i = pl.program_id(0)

        @pl.when(i == 0)
        def _():
            ext_ref[:, 0:CONV_HALO, :] = jnp.zeros((CONV_PLANES, CONV_HALO, _LANES), F32)

        ag_t = ag_ref[...]
        live = _row_ids(i, ROW_TILE) >= DEAD
        u0 = jnp.where(live, ag_t[:, :D_CONV] * _sigmoid(ag_t[:, D_CONV:]), 0.0)
        _to_planes(ext_ref, (), slice(CONV_HALO, None), u0)
        first = CONV_HALO - (CONV_WIDTH - 1)
        for c in range(CONV_PLANES):
            taps = w_ref[:, c * _LANES:(c + 1) * _LANES]
            for p in range(PHASES):
                acc = jnp.zeros((PHASE_ROWS, _LANES), F32)
                for k in range(CONV_WIDTH):
                    acc = acc + taps[k:k + 1, :] * ext_ref[c, _phase(first + k + p), :]
                conv_ref[c, _phase(p), :] = acc
        ext_ref[:, 0:CONV_HALO, :] = ext_ref[:, ROW_TILE:ROW_TILE + CONV_HALO, :]
        u1 = _from_planes(conv_ref, (), D_CONV) + b_ref[...]
        u1_ref[...] = u1
        _, _, u3, _ = _conv_chain(u1, lg_ref[...], lb_ref[...])
        mix_ref[...] = _rms_fwd(u3, og_ref[...]).astype(BF16)

    out_shapes = [jax.ShapeDtypeStruct((n_rows, D_CONV), BF16), jax.ShapeDtypeStruct((n_rows, D_CONV), F32)]
    small = [conv_w, conv_b, ln_g, ln_b, out_g]
    return pl.pallas_call(
        body, name="fwd_conv", grid=(nt,),
        in_specs=[_tile_spec(ag.shape)] + [_whole_spec(a.shape) for a in small],
        out_specs=[_tile_spec(s.shape) for s in out_shapes],
        out_shape=out_shapes,
        scratch_shapes=[pltpu.VMEM((CONV_PLANES, ROW_TILE + CONV_HALO, _LANES), F32),
                        pltpu.VMEM((CONV_PLANES, ROW_TILE, _LANES), F32)],
        compiler_params=_params("arbitrary"),
    )(ag, *small)


def _lane_tile(shape):
    if len(shape) == 2:
        return pl.BlockSpec((shape[0], ROW_TILE), lambda i: (0, i))
    return pl.BlockSpec((shape[0], shape[1], ROW_TILE), lambda i: (0, 0, i))


def _rope_rows(x, cos, sin):
    half = QK_ROPE // 2
    x1, x2 = x[:half], x[half:]
    return jnp.concatenate([x1 * cos - x2 * sin, x2 * cos + x1 * sin], axis=0)


def _rope_rows_t(dy, cos, sin):
    half = QK_ROPE // 2
    d1, d2 = dy[:half], dy[half:]
    return jnp.concatenate([d1 * cos + d2 * sin, d2 * cos - d1 * sin], axis=0)


def _fwd_qkv(cq, ckv, kr, gq, gkv, wq_t, w_ukv, wv_t, cos, sin, cos_t, sin_t, n_rows):
    nt = n_rows // ROW_TILE

    def body(cq_ref, ckv_ref, kr_ref, gq_ref, gkv_ref, wqt_ref, wkv_ref, wvt_ref, cos_ref, sin_ref, cost_ref, sint_ref,
             qt_ref, k_ref, v_ref, vt_ref, cqn_ref, ckvn_ref):
        cqn = _rms_fwd(cq_ref[...], gq_ref[...]).astype(BF16)
        ckvn = _rms_fwd(ckv_ref[...], gkv_ref[...]).astype(BF16)
        cqn_ref[...] = cqn
        ckvn_ref[...] = ckvn
        k_rot = _rope(kr_ref[...], cos_ref[...], sin_ref[...])
        cos_rows, sin_rows = cost_ref[...], sint_ref[...]
        for h in range(N_HEADS):
            q_raw = _dot_nt(wqt_ref[h], cqn)
            qt_ref[h] = jnp.concatenate(
                [q_raw[:QK_NOPE], _rope_rows(q_raw[QK_NOPE:], cos_rows, sin_rows)], axis=0).astype(BF16)
            kv = _dot(ckvn, wkv_ref[h])
            k_ref[h] = jnp.concatenate([kv[:, :QK_NOPE], k_rot], axis=-1).astype(BF16)
            v_ref[h] = kv[:, QK_NOPE:].astype(BF16)
            vt_ref[h] = _dot_nt(wvt_ref[h], ckvn).astype(BF16)

    out_shapes = [
        jax.ShapeDtypeStruct((N_HEADS, QK_DIM, n_rows), BF16),
        jax.ShapeDtypeStruct((N_HEADS, n_rows, QK_DIM), BF16),
        jax.ShapeDtypeStruct((N_HEADS, n_rows, V_HEAD), BF16),
        jax.ShapeDtypeStruct((N_HEADS, V_HEAD, n_rows), BF16),
        jax.ShapeDtypeStruct((n_rows, Q_LORA), BF16),
        jax.ShapeDtypeStruct((n_rows, KV_LORA), BF16),
    ]
    tiles = [cq, ckv, kr]
    whole = [gq, gkv, wq_t, w_ukv, wv_t]
    out_specs = [_lane_tile(out_shapes[0].shape), _tile_spec(out_shapes[1].shape), _tile_spec(out_shapes[2].shape),
                 _lane_tile(out_shapes[3].shape), _tile_spec(out_shapes[4].shape), _tile_spec(out_shapes[5].shape)]
    return pl.pallas_call(
        body, name="fwd_qkv", grid=(nt,),
        in_specs=[_tile_spec(a.shape) for a in tiles] + [_whole_spec(a.shape) for a in whole]
        + [_tile_spec(cos.shape), _tile_spec(sin.shape), _lane_tile(cos_t.shape), _lane_tile(sin_t.shape)],
        out_specs=out_specs,
        out_shape=out_shapes,
        compiler_params=_params("parallel"),
    )(*tiles, *whole, cos, sin, cos_t, sin_t)


def _chunk_of(rows):
    return jnp.where(rows >= ROW_TILE, lax.shift_right_arithmetic(rows - ROW_TILE, CHUNK_SHIFT) + 1, 0)


def _visible(i, j):
    k_rows = j * ROW_TILE + lax.broadcasted_iota(jnp.int32, (ROW_TILE, 1), 0)
    q_rows = i * ROW_TILE + lax.broadcasted_iota(jnp.int32, (1, ROW_TILE), 1)
    return jnp.logical_and(_chunk_of(q_rows) >= _chunk_of(k_rows), k_rows >= DEAD)


def _attn_fwd(q_t, k, v_t, n_rows):
    nt = n_rows // ROW_TILE
    scale = QK_DIM ** -0.5

    def body(qt_ref, k_ref, vt_ref, ot_ref, lse_ref):
        i = pl.program_id(0)
        q_ts = [qt_ref[h] for h in range(N_HEADS)]

        def make_step(masked):
            def step(j, carry):
                rows = pl.ds(pl.multiple_of(j * ROW_TILE, ROW_TILE), ROW_TILE)
                scores = [_dot(k_ref[h, rows, :], q_ts[h]) for h in range(N_HEADS)]
                visible = _visible(i, j) if masked else None
                probs, state = [], []
                for h in range(N_HEADS):
                    m, l, _ = carry[h]
                    s = scores[h] * scale
                    if masked:
                        s = jnp.where(visible, s, NEG)
                    m_new = jnp.maximum(m, jnp.max(s, axis=0, keepdims=True))
                    alpha = jnp.exp(m - m_new)
                    p = jnp.exp(s - m_new)
                    probs.append(p.astype(BF16))
                    state.append((m_new, alpha * l + jnp.sum(p, axis=0, keepdims=True), alpha))
                outs = [_dot(vt_ref[h, :, rows], probs[h]) for h in range(N_HEADS)]
                return tuple((state[h][0], state[h][1], state[h][2] * carry[h][2] + outs[h]) for h in range(N_HEADS))
            return step

        init = tuple((jnp.full((1, ROW_TILE), NEG, F32), jnp.zeros((1, ROW_TILE), F32),
                      jnp.zeros((V_HEAD, ROW_TILE), F32)) for _ in range(N_HEADS))
        carry = make_step(True)(0, init)
        carry = lax.fori_loop(1, i, make_step(False), carry)
        carry = lax.fori_loop(jnp.maximum(i, 1), i + 1, make_step(True), carry)
        for h in range(N_HEADS):
            m, l, acc = carry[h]
            ot_ref[h] = acc / l
            lse_ref[h] = m + jnp.log(l)

    out_shapes = [jax.ShapeDtypeStruct((N_HEADS, V_HEAD, n_rows), F32), jax.ShapeDtypeStruct((N_HEADS, 1, n_rows), F32)]
    return pl.pallas_call(
        body, name="attn_fwd", grid=(nt,),
        in_specs=[_lane_tile(q_t.shape), _whole_spec(k.shape), _whole_spec(v_t.shape)],
        out_specs=[_lane_tile(s.shape) for s in out_shapes],
        out_shape=out_shapes,
        compiler_params=_params("parallel"),
    )(q_t, k, v_t)


def _heads_to_rows(ref):
    return jnp.concatenate([ref[h] for h in range(N_HEADS)], axis=0)


def _rms_cols(x, g_col):
    r = lax.rsqrt(jnp.mean(x * x, axis=0, keepdims=True) + EPS)
    return x * r * g_col


def _fwd_out(x, meta_pad, mix_a, o_t, gb_col, w_out, n_rows):
    nt = n_rows // ROW_TILE

    def body(x_ref, meta_ref, mixa_ref, ot_ref, gb_ref, w_ref, mixbt_ref, h1_ref):
        i = pl.program_id(0)
        h0 = jnp.where(i == 0, meta_ref[...], x_ref[...])
        mix_bt = _rms_cols(_heads_to_rows(ot_ref), gb_ref[...]).astype(BF16)
        mixbt_ref[...] = mix_bt
        h1_ref[...] = h0 + _dot(mixa_ref[...], w_ref[:D_CONV, :]) + _dot_tn(mix_bt, w_ref[D_CONV:, :])

    out_shapes = [jax.ShapeDtypeStruct((D_ATTN, n_rows), BF16), jax.ShapeDtypeStruct((n_rows, D_MODEL), F32)]
    return pl.pallas_call(
        body, name="fwd_out", grid=(nt,),
        in_specs=[_real_spec(D_MODEL), _whole_spec(meta_pad.shape), _tile_spec(mix_a.shape), _lane_tile(o_t.shape),
                  _whole_spec(gb_col.shape), _whole_spec(w_out.shape)],
        out_specs=[_lane_tile(out_shapes[0].shape), _tile_spec(out_shapes[1].shape)],
        out_shape=out_shapes,
        compiler_params=_params("parallel"),
    )(x, meta_pad, mix_a, o_t, gb_col, w_out)


PHASES = 8
PHASE_ROWS = ROW_TILE // PHASES
UP_PLANES = -(-UP_SLAB // _LANES)
UP_PAD = UP_PLANES * _LANES
CONV_PLANES = D_CONV // _LANES


def _phase(start):
    return pl.ds(start, PHASE_ROWS, stride=PHASES)


def _to_planes(ref, lead, rows, value):
    width = value.shape[-1]
    for c in range(-(-width // _LANES)):
        part = value[:, c * _LANES:min((c + 1) * _LANES, width)]
        if part.shape[-1] < _LANES:
            part = jnp.concatenate([part, jnp.zeros((part.shape[0], _LANES - part.shape[-1]), part.dtype)], axis=-1)
        ref[(*lead, c, rows, slice(None))] = part


def _from_planes(ref, lead, width):
    planes = [ref[(*lead, c)] for c in range(-(-width // _LANES))]
    last = width - (len(planes) - 1) * _LANES
    return jnp.concatenate(planes[:-1] + [planes[-1][:, :last]], axis=-1)


def _fwd_ffn(h1, target, g2, w_up, fw, fb, w_down, gf, n_rows):
    nt = n_rows // ROW_TILE

    def body(h1_ref, t_ref, g2_ref, wup_ref, fw_ref, fb_ref, wdn_ref, gf_ref,
             n2_ref, up0_ref, act_ref, da_ref, db_ref, dh2_ref, loss_ref, dgf_ref, ext_ref):
        i = pl.program_id(0)

        @pl.when(i == 0)
        def _():
            ext_ref[:, 0:FFN_HALO, :] = jnp.zeros((N_DEV, FFN_HALO, UP_SLAB), F32)

        h1_t = h1_ref[...]
        n2 = _rms_fwd(h1_t, g2_ref[...]).astype(BF16)
        n2_ref[...] = n2
        live = _row_ids(i, ROW_TILE) >= DEAD
        for s in range(N_DEV):
            up0 = jnp.where(live, _dot_nt(n2, wup_ref[s]), 0.0).astype(BF16)
            up0_ref[s] = up0
            ext_ref[s, FFN_HALO:, :] = up0.astype(F32)
        first = FFN_HALO - (FFN_CONV_WIDTH - 1)

        def conv(s):
            acc = fb_ref[s, :, :UP_SLAB]
            for k in range(FFN_CONV_WIDTH):
                acc = acc + fw_ref[s, k:k + 1, :UP_SLAB] * ext_ref[s, first + k:first + k + ROW_TILE, :]
            return acc

        h2 = h1_t
        for s in range(N_ACT_SLAB):
            gate = conv(s)
            val = conv(s + N_ACT_SLAB)
            sg = _sigmoid(gate)
            silu = gate * sg
            act = (silu * val).astype(BF16)
            act_ref[s] = act
            da_ref[s] = (val * sg * (1.0 + gate * (1.0 - sg))).astype(BF16)
            db_ref[s] = silu.astype(BF16)
            h2 = h2 + _dot(act, wdn_ref[s])
        ext_ref[:, 0:FFN_HALO, :] = ext_ref[:, ROW_TILE:ROW_TILE + FFN_HALO, :]

        gf_t = gf_ref[...]
        y = _rms_fwd(h2, gf_t)
        diff = jnp.where(i >= 1, y - t_ref[...], 0.0)
        tile_loss = 0.5 * jnp.sum(jnp.sum(diff * diff, axis=-1, keepdims=True), axis=0, keepdims=True) / D_MODEL
        dh2, dgf = _rms_bwd(diff / D_MODEL, h2, gf_t)
        dh2_ref[...] = dh2
        _accumulate(loss_ref, i == 0, jnp.broadcast_to(tile_loss, loss_ref.shape))
        _accumulate(dgf_ref, i == 0, dgf)

    act_like = jax.ShapeDtypeStruct((N_ACT_SLAB, n_rows, UP_SLAB), BF16)
    out_shapes = [
        jax.ShapeDtypeStruct((n_rows, D_MODEL), BF16),
        jax.ShapeDtypeStruct((N_DEV, n_rows, UP_SLAB), BF16),
        act_like, act_like, act_like,
        jax.ShapeDtypeStruct((n_rows, D_MODEL), F32),
        jax.ShapeDtypeStruct((8, 128), F32),
        jax.ShapeDtypeStruct((1, D_MODEL), F32),
    ]
    whole = [g2, w_up, fw, fb, w_down, gf]
    return pl.pallas_call(
        body, name="fwd_ffn", grid=(nt,),
        in_specs=[_tile_spec(h1.shape), _real_spec(D_MODEL)] + [_whole_spec(a.shape) for a in whole],
        out_specs=[_tile_spec(s.shape) for s in out_shapes[:6]] + [_acc_spec(s.shape) for s in out_shapes[6:]],
        out_shape=out_shapes,
        scratch_shapes=[pltpu.VMEM((N_DEV, ROW_TILE + FFN_HALO, UP_SLAB), F32)],
        compiler_params=_params("arbitrary"),
    )(h1, target, *whole)


def _rope_tables(n_rows):
    pos = jnp.maximum(jnp.arange(n_rows, dtype=jnp.int32) - DEAD, 0)
    inv_freq = 1.0 / (ROPE_THETA ** (jnp.arange(0, QK_ROPE, 2, dtype=F32) / QK_ROPE))
    ang_t = inv_freq[:, None] * pos.astype(F32)[None, :]
    return jnp.cos(ang_t), jnp.sin(ang_t)


def _halo_after(shape, halo, n_rows):
    last = n_rows // halo - 1
    step = ROW_TILE // halo
    if len(shape) == 2:
        return pl.BlockSpec((halo, shape[1]), lambda i: (jnp.minimum((i + 1) * step, last), 0))
    return pl.BlockSpec((shape[0], halo, shape[2]), lambda i: (0, jnp.minimum((i + 1) * step, last), 0))


def _halo_before(shape, halo):
    step = ROW_TILE // halo
    if len(shape) == 2:
        return pl.BlockSpec((halo, shape[1]), lambda i: (jnp.maximum(i * step - 1, 0), 0))
    return pl.BlockSpec((shape[0], halo, shape[2]), lambda i: (0, jnp.maximum(i * step - 1, 0), 0))


def _bwd_ffn_act(dh2, da, db, w_down, n_rows):
    nt = n_rows // ROW_TILE

    def body(dh2_ref, da_ref, db_ref, wdn_ref, dup_ref, dfb_ref):
        i = pl.program_id(0)

        @pl.when(i == 0)
        def _():
            dfb_ref[...] = jnp.zeros_like(dfb_ref)

        dh2_b = dh2_ref[...].astype(BF16)
        for s in range(N_ACT_SLAB):
            d_act = _dot_nt(dh2_b, wdn_ref[s])
            d_gate = d_act * da_ref[s].astype(F32)
            d_val = d_act * db_ref[s].astype(F32)
            dup_ref[s] = d_gate.astype(BF16)
            dup_ref[s + N_ACT_SLAB] = d_val.astype(BF16)
            dfb_ref[s] += jnp.sum(d_gate, axis=0, keepdims=True)
            dfb_ref[s + N_ACT_SLAB] += jnp.sum(d_val, axis=0, keepdims=True)

    out_shapes = [jax.ShapeDtypeStruct((N_DEV, n_rows, UP_SLAB), BF16), jax.ShapeDtypeStruct((N_DEV, 1, UP_SLAB), F32)]
    return pl.pallas_call(
        body, name="bwd_ffn_act", grid=(nt,),
        in_specs=[_tile_spec(dh2.shape), _tile_spec(da.shape), _tile_spec(db.shape), _whole_spec(w_down.shape)],
        out_specs=[_tile_spec(out_shapes[0].shape), _acc_spec(out_shapes[1].shape)],
        out_shape=out_shapes,
        compiler_params=_params("arbitrary"),
    )(dh2, da, db, w_down)


def _bwd_ffn_up(dup, up0, h1, dh2, g2, w_up, fw, n_rows):
    nt = n_rows // ROW_TILE
    last_tap = FFN_CONV_WIDTH - 1

    def body(dup_ref, dnext_ref, up0_ref, h1_ref, dh2_ref, g2_ref, wup_ref, fw_ref,
             dup0_ref, dh1_ref, dfw_ref, dg2_ref, dext_ref, uext_ref, conv_ref):
        i = pl.program_id(0)

        @pl.when(i == 0)
        def _():
            dfw_ref[...] = jnp.zeros_like(dfw_ref)

        live = _row_ids(i, ROW_TILE) >= DEAD
        dn2 = jnp.zeros((ROW_TILE, D_MODEL), F32)
        for s in range(N_DEV):
            _to_planes(dext_ref, (), slice(0, ROW_TILE), dup_ref[s].astype(F32))
            _to_planes(dext_ref, (), slice(ROW_TILE, None), jnp.where(i == nt - 1, 0.0, dnext_ref[s].astype(F32)))
            _to_planes(uext_ref, (), slice(None), up0_ref[s].astype(F32))
            for c in range(UP_PLANES):
                lanes = slice(c * _LANES, (c + 1) * _LANES)
                taps = [fw_ref[s, k:k + 1, lanes] for k in range(FFN_CONV_WIDTH)]
                sums = [jnp.zeros((PHASE_ROWS, _LANES), F32) for _ in range(FFN_CONV_WIDTH)]
                for p in range(PHASES):
                    u = uext_ref[c, _phase(p), :]
                    acc = jnp.zeros((PHASE_ROWS, _LANES), F32)
                    for k in range(FFN_CONV_WIDTH):
                        shifted = dext_ref[c, _phase(p + last_tap - k), :]
                        acc = acc + taps[k] * shifted
                        sums[k] = sums[k] + shifted * u
                    conv_ref[c, _phase(p), :] = acc
                for k in range(FFN_CONV_WIDTH):
                    dfw_ref[s, k:k + 1, lanes] += jnp.sum(sums[k], axis=0, keepdims=True)
            dup0_b = jnp.where(live, _from_planes(conv_ref, (), UP_SLAB), 0.0).astype(BF16)
            dup0_ref[s] = dup0_b
            dn2 = dn2 + _dot(dup0_b, wup_ref[s])
        dx, dg2 = _rms_bwd(dn2, h1_ref[...], g2_ref[...])
        dh1_ref[...] = dh2_ref[...] + dx
        _accumulate(dg2_ref, i == 0, dg2)

    out_shapes = [
        jax.ShapeDtypeStruct((N_DEV, n_rows, UP_SLAB), BF16),
        jax.ShapeDtypeStruct((n_rows, D_MODEL), F32),
        jax.ShapeDtypeStruct((N_DEV, FFN_CONV_WIDTH, UP_PAD), F32),
        jax.ShapeDtypeStruct((1, D_MODEL), F32),
    ]
    return pl.pallas_call(
        body, name="bwd_ffn_up", grid=(nt,),
        in_specs=[_tile_spec(dup.shape), _halo_after(dup.shape, FFN_HALO, n_rows), _tile_spec(up0.shape),
                  _tile_spec(h1.shape), _tile_spec(dh2.shape),
                  _whole_spec(g2.shape), _whole_spec(w_up.shape), _whole_spec(fw.shape)],
        out_specs=[_tile_spec(s.shape) for s in out_shapes[:2]] + [_acc_spec(s.shape) for s in out_shapes[2:]],
        out_shape=out_shapes,
        scratch_shapes=[pltpu.VMEM((UP_PLANES, ROW_TILE + FFN_HALO, _LANES), F32),
                        pltpu.VMEM((UP_PLANES, ROW_TILE, _LANES), F32), pltpu.VMEM((UP_PLANES, ROW_TILE, _LANES), F32)],
        compiler_params=_params("arbitrary"),
    )(dup, dup, up0, h1, dh2, g2, w_up, fw)


def _bwd_out(dh1, o_t, u1, w_out, gb_col, ln_g, ln_b, ga, n_rows):
    nt = n_rows // ROW_TILE

    def body(dh1_ref, ot_ref, u1_ref, w_ref, gb_ref, lg_ref, lb_ref, ga_ref,
             dot_ref, delta_ref, du1_ref, dgb_ref, dga_ref, dlg_ref, dlb_ref, dcb_ref):
        i = pl.program_id(0)
        dh1_b = dh1_ref[...].astype(BF16)
        o_t = _heads_to_rows(ot_ref)
        gb = gb_ref[...]
        r = lax.rsqrt(jnp.mean(o_t * o_t, axis=0, keepdims=True) + EPS)
        dmix_bt = _dot_nt(w_ref[D_CONV:, :], dh1_b)
        wgt = dmix_bt * gb
        do_t = r * wgt - o_t * (r * r * r) * jnp.mean(wgt * o_t, axis=0, keepdims=True)
        dgb = jnp.sum(dmix_bt * o_t * r, axis=1, keepdims=True)
        for h in range(N_HEADS):
            do_h = do_t[h * V_HEAD:(h + 1) * V_HEAD]
            dot_ref[h] = do_h.astype(BF16)
            delta_ref[h] = jnp.sum(do_h * ot_ref[h], axis=0, keepdims=True)
        lg = lg_ref[...]
        xh, u2, u3, rstd = _conv_chain(u1_ref[...], lg, lb_ref[...])
        du3, dga = _rms_bwd(_dot_nt(dh1_b, w_ref[:D_CONV, :]), u3, ga_ref[...])
        sg = _sigmoid(u2)
        du2 = du3 * sg * (1.0 + u2 * (1.0 - sg))
        dxh = du2 * lg
        du1 = rstd * (dxh - jnp.mean(dxh, axis=-1, keepdims=True) - xh * jnp.mean(dxh * xh, axis=-1, keepdims=True))
        du1_ref[...] = du1
        first = i == 0
        _accumulate(dgb_ref, first, dgb)
        _accumulate(dga_ref, first, dga)
        _accumulate(dlg_ref, first, jnp.sum(du2 * xh, axis=0, keepdims=True))
        _accumulate(dlb_ref, first, jnp.sum(du2, axis=0, keepdims=True))
        _accumulate(dcb_ref, first, jnp.sum(du1, axis=0, keepdims=True))

    out_shapes = [
        jax.ShapeDtypeStruct((N_HEADS, V_HEAD, n_rows), BF16),
        jax.ShapeDtypeStruct((N_HEADS, 1, n_rows), F32),
        jax.ShapeDtypeStruct((n_rows, D_CONV), F32),
        jax.ShapeDtypeStruct((D_ATTN, 1), F32),
    ] + [jax.ShapeDtypeStruct((1, D_CONV), F32)] * 4
    whole = [w_out, gb_col, ln_g, ln_b, ga]
    return pl.pallas_call(
        body, name="bwd_out", grid=(nt,),
        in_specs=[_tile_spec(dh1.shape), _lane_tile(o_t.shape), _tile_spec(u1.shape)] + [_whole_spec(a.shape) for a in whole],
        out_specs=[_lane_tile(out_shapes[0].shape), _lane_tile(out_shapes[1].shape), _tile_spec(out_shapes[2].shape)]
        + [_acc_spec(s.shape) for s in out_shapes[3:]],
        out_shape=out_shapes,
        compiler_params=_params("arbitrary"),
    )(dh1, o_t, u1, *whole)


ATTN_BWD_HEADS = 4


def _attn_bwd(q_t, k, v, do_t, lse, delta, n_rows):
    nt = n_rows // ROW_TILE
    scale = QK_DIM ** -0.5
    hp = ATTN_BWD_HEADS

    def body(k_ref, v_ref, qt_ref, dot_ref, lse_ref, delta_ref, dqt_ref, dk_ref, dv_ref):
        j = pl.program_id(1)

        @pl.when(j == 0)
        def _():
            dqt_ref[...] = jnp.zeros_like(dqt_ref)

        k_ts = [k_ref[h] for h in range(hp)]
        v_ts = [v_ref[h] for h in range(hp)]

        def make_step(masked):
            def step(i, carry):
                cols = pl.ds(pl.multiple_of(i * ROW_TILE, ROW_TILE), ROW_TILE)
                q_is = [qt_ref[h, :, cols] for h in range(hp)]
                do_is = [dot_ref[h, :, cols] for h in range(hp)]
                scores = [_dot(k_ts[h], q_is[h]) for h in range(hp)]
                dps = [_dot(v_ts[h], do_is[h]) for h in range(hp)]
                visible = _visible(i, j) if masked else None
                probs, dss = [], []
                for h in range(hp):
                    s = scores[h] * scale
                    if masked:
                        s = jnp.where(visible, s, NEG)
                    p = jnp.exp(s - lse_ref[h, :, cols])
                    probs.append(p.astype(BF16))
                    dss.append((p * (dps[h] - delta_ref[h, :, cols]) * scale).astype(BF16))
                out = []
                for h in range(hp):
                    dk, dv = carry[h]
                    dv = dv + _dot_nt(probs[h], do_is[h])
                    dk = dk + _dot_nt(dss[h], q_is[h])
                    dqt_ref[h, :, cols] += _dot_tn(k_ts[h], dss[h])
                    out.append((dk, dv))
                return tuple(out)
            return step

        init = tuple((jnp.zeros((ROW_TILE, QK_DIM), F32), jnp.zeros((ROW_TILE, V_HEAD), F32)) for _ in range(hp))
        carry = make_step(True)(j, init)
        carry = lax.fori_loop(jnp.where(j == 0, j + 1, nt), nt, make_step(True), carry)
        carry = lax.fori_loop(jnp.where(j == 0, nt, j + 1), nt, make_step(False), carry)
        for h in range(hp):
            dk_ref[h], dv_ref[h] = carry[h]

    key_tile = lambda w: pl.BlockSpec((hp, ROW_TILE, w), lambda g, j: (g, j, 0))
    all_cols = lambda w: pl.BlockSpec((hp, w, n_rows), lambda g, j: (g, 0, 0))
    out_shapes = [
        jax.ShapeDtypeStruct((N_HEADS, QK_DIM, n_rows), F32),
        jax.ShapeDtypeStruct((N_HEADS, n_rows, QK_DIM), F32),
        jax.ShapeDtypeStruct((N_HEADS, n_rows, V_HEAD), F32),
    ]
    return pl.pallas_call(
        body, name="attn_bwd", grid=(N_HEADS // hp, nt),
        in_specs=[key_tile(QK_DIM), key_tile(V_HEAD), all_cols(QK_DIM), all_cols(V_HEAD), all_cols(1), all_cols(1)],
        out_specs=[all_cols(QK_DIM), key_tile(QK_DIM), key_tile(V_HEAD)],
        out_shape=out_shapes,
        compiler_params=_params("parallel", "arbitrary"),
    )(k, v, q_t, do_t, lse, delta)


def _bwd_qkv(dq_t, dk, dv, cq, ckv, gq, gkv, wq_t, w_ukv, cos, sin, cos_t, sin_t, n_rows):
    nt = n_rows // ROW_TILE

    def body(dqt_ref, dk_ref, dv_ref, cq_ref, ckv_ref, gq_ref, gkv_ref, wqt_ref, wkv_ref, cos_ref, sin_ref,
             cost_ref, sint_ref, dqraw_ref, dkv_ref, dcq_ref, dckv_ref, dkr_ref, dgq_ref, dgkv_ref):
        i = pl.program_id(0)
        cos_rows, sin_rows = cost_ref[...], sint_ref[...]
        dcqn = jnp.zeros((ROW_TILE, Q_LORA), F32)
        dckvn = jnp.zeros((ROW_TILE, KV_LORA), F32)
        dk_rot = jnp.zeros((ROW_TILE, QK_ROPE), F32)
        for h in range(N_HEADS):
            dq_h, dk_h = dqt_ref[h], dk_ref[h]
            dq_raw = jnp.concatenate(
                [dq_h[:QK_NOPE], _rope_rows_t(dq_h[QK_NOPE:], cos_rows, sin_rows)], axis=0).astype(BF16)
            dqraw_ref[h] = dq_raw
            dcqn = dcqn + _dot_tn(dq_raw, wqt_ref[h])
            dkv = jnp.concatenate([dk_h[:, :QK_NOPE], dv_ref[h]], axis=-1).astype(BF16)
            dkv_ref[h] = dkv
            dckvn = dckvn + _dot_nt(dkv, wkv_ref[h])
            dk_rot = dk_rot + dk_h[:, QK_NOPE:]
        dkr_ref[...] = _rope_t(dk_rot, cos_ref[...], sin_ref[...]).astype(BF16)
        dcq, dgq = _rms_bwd(dcqn, cq_ref[...], gq_ref[...])
        dckv, dgkv = _rms_bwd(dckvn, ckv_ref[...], gkv_ref[...])
        dcq_ref[...] = dcq.astype(BF16)
        dckv_ref[...] = dckv.astype(BF16)
        _accumulate(dgq_ref, i == 0, dgq)
        _accumulate(dgkv_ref, i == 0, dgkv)

    out_shapes = [
        jax.ShapeDtypeStruct((N_HEADS, QK_DIM, n_rows), BF16),
        jax.ShapeDtypeStruct((N_HEADS, n_rows, KV_HEAD), BF16),
        jax.ShapeDtypeStruct((n_rows, Q_LORA), BF16),
        jax.ShapeDtypeStruct((n_rows, KV_LORA), BF16),
        jax.ShapeDtypeStruct((n_rows, QK_ROPE), BF16),
        jax.ShapeDtypeStruct((1, Q_LORA), F32),
        jax.ShapeDtypeStruct((1, KV_LORA), F32),
    ]
    tiles = [dk, dv, cq, ckv]
    whole = [gq, gkv, wq_t, w_ukv]
    return pl.pallas_call(
        body, name="bwd_qkv", grid=(nt,),
        in_specs=[_lane_tile(dq_t.shape)] + [_tile_spec(a.shape) for a in tiles] + [_whole_spec(a.shape) for a in whole]
        + [_tile_spec(cos.shape), _tile_spec(sin.shape), _lane_tile(cos_t.shape), _lane_tile(sin_t.shape)],
        out_specs=[_lane_tile(out_shapes[0].shape)] + [_tile_spec(s.shape) for s in out_shapes[1:5]]
        + [_acc_spec(s.shape) for s in out_shapes[5:]],
        out_shape=out_shapes,
        compiler_params=_params("arbitrary"),
    )(dq_t, *tiles, *whole, cos, sin, cos_t, sin_t)


def _bwd_conv(du1, ag, conv_w, n_rows):
    nt = n_rows // ROW_TILE

    last_tap = CONV_WIDTH - 1

    def body(du1_ref, dnext_ref, ag_ref, w_ref, dag_ref, dw_ref, dext_ref, uext_ref, conv_ref, sums_ref):
        i = pl.program_id(0)

        @pl.when(i == 0)
        def _():
            sums_ref[...] = jnp.zeros_like(sums_ref)

        _to_planes(dext_ref, (), slice(0, ROW_TILE), du1_ref[...])
        _to_planes(dext_ref, (), slice(ROW_TILE, None), jnp.where(i == nt - 1, 0.0, dnext_ref[...]))
        ag_t = ag_ref[...]
        live = _row_ids(i, ROW_TILE) >= DEAD
        sg = _sigmoid(ag_t[:, D_CONV:])
        _to_planes(uext_ref, (), slice(None), jnp.where(live, ag_t[:, :D_CONV] * sg, 0.0))
        for c in range(CONV_PLANES):
            taps = w_ref[:, c * _LANES:(c + 1) * _LANES]
            for p in range(PHASES):
                u = uext_ref[c, _phase(p), :]
                acc = jnp.zeros((PHASE_ROWS, _LANES), F32)
                for k in range(CONV_WIDTH):
                    shifted = dext_ref[c, _phase(p + last_tap - k), :]
                    acc = acc + taps[k:k + 1, :] * shifted
                    sums_ref[c, k] += shifted * u
                conv_ref[c, _phase(p), :] = acc
        du0 = jnp.where(live, _from_planes(conv_ref, (), D_CONV), 0.0)
        da = du0 * sg
        dgate = du0 * ag_t[:, :D_CONV] * sg * (1.0 - sg)
        dag_ref[...] = jnp.concatenate([da, dgate], axis=-1).astype(BF16)

        @pl.when(i == nt - 1)
        def _():
            for c in range(CONV_PLANES):
                for k in range(CONV_WIDTH):
                    dw_ref[k:k + 1, c * _LANES:(c + 1) * _LANES] = jnp.sum(sums_ref[c, k], axis=0, keepdims=True)

    out_shapes = [jax.ShapeDtypeStruct((n_rows, 2 * D_CONV), BF16), jax.ShapeDtypeStruct((CONV_WIDTH, D_CONV), F32)]
    return pl.pallas_call(
        body, name="bwd_conv", grid=(nt,),
        in_specs=[_tile_spec(du1.shape), _halo_after(du1.shape, CONV_HALO, n_rows), _tile_spec(ag.shape),
                  _whole_spec(conv_w.shape)],
        out_specs=[_tile_spec(out_shapes[0].shape), _acc_spec(out_shapes[1].shape)],
        out_shape=out_shapes,
        scratch_shapes=[pltpu.VMEM((CONV_PLANES, ROW_TILE + CONV_HALO, _LANES), F32),
                        pltpu.VMEM((CONV_PLANES, ROW_TILE, _LANES), F32), pltpu.VMEM((CONV_PLANES, ROW_TILE, _LANES), F32),
                        pltpu.VMEM((CONV_PLANES, CONV_WIDTH, PHASE_ROWS, _LANES), F32)],
        compiler_params=_params("arbitrary"),
    )(du1, du1, ag, conv_w)


def _bwd_in(dag, dcq, dckv, dkr, x, meta_pad, dh1, g1, w_in, n_rows):
    nt = n_rows // ROW_TILE

    def body(dag_ref, dcq_ref, dckv_ref, dkr_ref, x_ref, meta_ref, dh1_ref, g_ref, w_ref,
             dz_ref, gx_ref, gmeta_ref, dg1_ref):
        i = pl.program_id(0)
        dz = jnp.concatenate([dag_ref[...], dcq_ref[...], dckv_ref[...], dkr_ref[...]], axis=-1)
        dz_ref[...] = dz
        h0 = jnp.where(i == 0, meta_ref[...], x_ref[...])
        dx, dg1 = _rms_bwd(_dot(dz, w_ref[...]), h0, g_ref[...])
        dh0 = dh1_ref[...] + dx
        gx_ref[...] = dh0

        @pl.when(i == 0)
        def _():
            gmeta_ref[...] = dh0

        _accumulate(dg1_ref, i == 0, dg1)

    out_shapes = [
        jax.ShapeDtypeStruct((n_rows, D_IN), BF16),
        jax.ShapeDtypeStruct((n_rows - ROW_TILE, D_MODEL), F32),
        jax.ShapeDtypeStruct((ROW_TILE, D_MODEL), F32),
        jax.ShapeDtypeStruct((1, D_MODEL), F32),
    ]
    tiles = [dag, dcq, dckv, dkr]
    return pl.pallas_call(
        body, name="bwd_in", grid=(nt,),
        in_specs=[_tile_spec(a.shape) for a in tiles]
        + [_real_spec(D_MODEL), _whole_spec(meta_pad.shape), _tile_spec(dh1.shape), _whole_spec(g1.shape), _whole_spec(w_in.shape)],
        out_specs=[_tile_spec(out_shapes[0].shape), _real_spec(D_MODEL), _acc_spec(out_shapes[2].shape), _acc_spec(out_shapes[3].shape)],
        out_shape=out_shapes,
        compiler_params=_params("arbitrary"),
    )(*tiles, x, meta_pad, dh1, g1, w_in)


def _contraction_tile(n_rows):
    return next(t for t in range(n_rows // 2 // _LANES * _LANES, 0, -_LANES) if n_rows % t == 0)


def _weight_grad(a, b, name, a_transposed=False):
    groups = max(a.shape[0] if a.ndim == 3 else 1, b.shape[0] if b.ndim == 3 else 1)
    n_rows, n = b.shape[-2], b.shape[-1]
    m = a.shape[-2] if a_transposed else a.shape[-1]
    kt = _contraction_tile(n_rows)
    steps = n_rows // kt

    def body(a_ref, b_ref, out_ref, acc_ref):
        i = pl.program_id(1)
        a_t, b_t = a_ref[...].astype(BF16), b_ref[...].astype(BF16)
        part = _dot(a_t, b_t) if a_transposed else _dot_tn(a_t, b_t)
        _accumulate(acc_ref, i == 0, part)

        @pl.when(i == steps - 1)
        def _():
            out_ref[...] = acc_ref[...].astype(out_ref.dtype)

    def spec(arr, rows_last):
        block = (arr.shape[-2], kt) if rows_last else (kt, arr.shape[-1])
        at = (lambda i: (0, i)) if rows_last else (lambda i: (i, 0))
        if arr.ndim == 3:
            return pl.BlockSpec((None,) + block, lambda g, i: (g,) + at(i))
        return pl.BlockSpec(block, lambda g, i: at(i))

    return pl.pallas_call(
        body, name=name, grid=(groups, steps),
        in_specs=[spec(a, a_transposed), spec(b, False)],
        out_specs=pl.BlockSpec((None, m, n), lambda g, i: (g, 0, 0)),
        out_shape=jax.ShapeDtypeStruct((groups, m, n), BF16),
        scratch_shapes=[pltpu.VMEM((m, n), F32)],
        compiler_params=_params("parallel", "arbitrary"),
    )(a, b)


def _my_index():
    return 4 * lax.axis_index("x") + 2 * lax.axis_index("y") + lax.axis_index("c")


def _peer(k):
    flip = lambda v, bit: 1 - v if bit else v
    px = flip(lax.axis_index("x"), k & 4)
    py = flip(lax.axis_index("y"), k & 2)
    pc = flip(lax.axis_index("c"), k & 1)
    return (px, py, pc), 4 * px + 2 * py + pc


def _all_gather(shards, dtypes):
    n = len(shards)

    def body(*refs):
        ins, outs, stages = refs[:n], refs[n:2 * n], refs[2 * n:3 * n]
        send_sems, recv_sems, local_sems = refs[3 * n:]
        me = _my_index()
        for a in range(n):
            stages[a][...] = ins[a][...].astype(stages[a].dtype)
        local = [pltpu.make_async_copy(stages[a], outs[a].at[me], local_sems.at[a]) for a in range(n)]
        for cp in local:
            cp.start()

        def copy(a, k, slot):
            peer, _ = _peer(k)
            return pltpu.make_async_remote_copy(
                src_ref=stages[a], dst_ref=outs[a].at[slot], send_sem=send_sems.at[a, k - 1],
                recv_sem=recv_sems.at[a, k - 1], device_id=peer, device_id_type=MESH)

        for k in range(1, N_DEV):
            for a in range(n):
                copy(a, k, me).start()
        for k in range(1, N_DEV):
            for a in range(n):
                copy(a, k, _peer(k)[1]).wait()
        for cp in local:
            cp.wait()

    return pl.pallas_call(
        body, name="gather_weights",
        in_specs=[pl.BlockSpec(memory_space=pltpu.VMEM)] * n,
        out_specs=[pl.BlockSpec(memory_space=pl.ANY)] * n,
        out_shape=[jax.ShapeDtypeStruct((N_DEV,) + s.shape, dt) for s, dt in zip(shards, dtypes)],
        scratch_shapes=[pltpu.VMEM(s.shape, dt) for s, dt in zip(shards, dtypes)]
        + [pltpu.SemaphoreType.DMA((n, N_DEV - 1)), pltpu.SemaphoreType.DMA((n, N_DEV - 1)), pltpu.SemaphoreType.DMA((n,))],
        compiler_params=pltpu.CompilerParams(vmem_limit_bytes=VMEM_LIMIT),
    )(*shards)


def _exchange(parts, whole):
    n = len(parts)

    def body(*refs):
        ins, outs = refs[:n], refs[n:2 * n]
        send_sems, recv_sems, local_sems = refs[2 * n:]
        me = _my_index()

        def src(a, slab):
            return ins[a] if whole[a] else ins[a].at[slab]

        local = [pltpu.make_async_copy(src(a, me), outs[a].at[me], local_sems.at[a]) for a in range(n)]
        for cp in local:
            cp.start()

        def copy(a, k, slab, slot):
            peer, _ = _peer(k)
            return pltpu.make_async_remote_copy(
                src_ref=src(a, slab), dst_ref=outs[a].at[slot], send_sem=send_sems.at[a, k - 1],
                recv_sem=recv_sems.at[a, k - 1], device_id=peer, device_id_type=MESH)

        for k in range(1, N_DEV):
            for a in range(n):
                copy(a, k, _peer(k)[1], me).start()
        for k in range(1, N_DEV):
            for a in range(n):
                copy(a, k, _peer(k)[1], _peer(k)[1]).wait()
        for cp in local:
            cp.wait()

    return pl.pallas_call(
        body, name="exchange_grads",
        in_specs=[pl.BlockSpec(memory_space=pl.ANY)] * n,
        out_specs=[pl.BlockSpec(memory_space=pl.ANY)] * n,
        out_shape=[jax.ShapeDtypeStruct(((N_DEV,) + p.shape) if w else p.shape, p.dtype) for p, w in zip(parts, whole)],
        scratch_shapes=[pltpu.SemaphoreType.DMA((n, N_DEV - 1)), pltpu.SemaphoreType.DMA((n, N_DEV - 1)),
                        pltpu.SemaphoreType.DMA((n,))],
    )(*parts)


def _sequencer_exchange(parts, whole, name, collective_id):
    n = len(parts)
    srcs = [jax.new_ref(p, memory_space=pltpu.MemorySpace.HBM) for p in parts]
    lands = [jax.empty_ref(jax.ShapeDtypeStruct(((N_DEV,) + p.shape) if w else p.shape, p.dtype),
                           memory_space=pltpu.MemorySpace.HBM) for p, w in zip(parts, whole)]

    @pl.kernel(mesh=plsc.ScalarSubcoreMesh(axis_name="sequencer", num_cores=1), name=name,
               scratch_types=(pltpu.SemaphoreType.DMA((n, N_DEV - 1)), pltpu.SemaphoreType.DMA((n, N_DEV - 1)),
                              pltpu.SemaphoreType.DMA((n,))),
               compiler_params=pltpu.CompilerParams(collective_id=collective_id))
    def launch(send_sems, recv_sems, local_sems):
        barrier = pltpu.get_barrier_semaphore()
        for k in range(1, N_DEV):
            pl.semaphore_signal(barrier, inc=1, device_id=_peer(k)[0], device_id_type=MESH)
        pl.semaphore_wait(barrier, N_DEV - 1)
        me = _my_index()

        def src(a, slab):
            return srcs[a] if whole[a] else srcs[a].at[slab]

        local = [pltpu.make_async_copy(src(a, me), lands[a].at[me], local_sems.at[a]) for a in range(n)]
        for cp in local:
            cp.start()

        def copy(a, k, slab, slot):
            return pltpu.make_async_remote_copy(
                src_ref=src(a, slab), dst_ref=lands[a].at[slot], send_sem=send_sems.at[a, k - 1],
                recv_sem=recv_sems.at[a, k - 1], device_id=_peer(k)[0], device_id_type=MESH)

        for k in range(1, N_DEV):
            for a in range(n):
                copy(a, k, _peer(k)[1], me).start()
        for k in range(1, N_DEV):
            for a in range(n):
                copy(a, k, _peer(k)[1], _peer(k)[1]).wait()
        for cp in local:
            cp.wait()

    launch()
    return [land[...] for land in lands]


def _row_block(rows):
    if rows <= ROW_TILE:
        return rows
    return next(rb for rb in range(ROW_TILE, 0, -16) if rows % rb == 0)


def _adamw(landing, w, m, v, name):
    rows, cols = w.shape
    rb = _row_block(rows)

    def body(l_ref, w_ref, m_ref, v_ref, g_ref, d_ref, m2_ref, v2_ref):
        g = l_ref[0].astype(F32)
        for p in range(1, N_DEV):
            g = g + l_ref[p].astype(F32)
        g_ref[...] = g
        d_ref[...], m2_ref[...], v2_ref[...] = _adamw_step(g, w_ref[...], m_ref[...], v_ref[...])

    flat = pl.BlockSpec((rb, cols), lambda i: (i, 0))
    return pl.pallas_call(
        body, name=name, grid=(rows // rb,),
        in_specs=[pl.BlockSpec((N_DEV, rb, cols), lambda i: (0, i, 0)), flat, flat, flat],
        out_specs=[flat] * 4,
        out_shape=[jax.ShapeDtypeStruct((rows, cols), F32)] * 4,
        compiler_params=_params("parallel"),
    )(landing, w, m, v)


def _adamw_step(g, w, m, v):
    m2 = ADAM_B1 * m + (1.0 - ADAM_B1) * g
    v2 = ADAM_B2 * v + (1.0 - ADAM_B2) * (g * g)
    m_hat = m2 / (1.0 - ADAM_B1 ** ADAM_STEP)
    v_hat = v2 / (1.0 - ADAM_B2 ** ADAM_STEP)
    return -ADAM_LR * (m_hat / (jnp.sqrt(v_hat) + ADAM_EPS) + ADAM_WD * w), m2, v2


_REPLICATED = (
    ("mix_norm_g", D_MODEL), ("q_norm_g", Q_LORA), ("kv_norm_g", KV_LORA), ("conv_b", D_CONV), ("conv_ln_g", D_CONV),
    ("conv_ln_b", D_CONV), ("conv_out_g", D_CONV), ("attn_out_g", D_CONV), ("ffn_norm_g", D_MODEL),
    ("ffn_conv_b", D_UP), ("final_norm_g", D_MODEL),
)
_REPLICATED_WIDTH = sum(size for _, size in _REPLICATED) + _LANES

_WEIGHT_ORDER = (
    "meta_tokens", "mix_norm_g", "w_in", "q_norm_g", "w_uq", "kv_norm_g", "w_ukv", "conv_w", "conv_b", "conv_ln_g",
    "conv_ln_b", "conv_out_g", "attn_out_g", "w_out", "ffn_norm_g", "w_ffn_up", "ffn_conv_w", "ffn_conv_b",
    "w_ffn_down", "final_norm_g",
)


def _pack_replicated(grads, loss):
    rows = [grads[name].reshape(1, size) for name, size in _REPLICATED]
    return jnp.concatenate(rows + [jnp.broadcast_to(loss.reshape(1, 1), (1, _LANES))], axis=-1)


def _adamw_replicated(landing, weights, moments_m, moments_v):
    n = len(_REPLICATED)

    def body(*refs):
        l_ref, ins, outs = refs[0], refs[1:1 + 3 * n], refs[1 + 3 * n:]
        total = l_ref[0]
        for p in range(1, N_DEV):
            total = total + l_ref[p]
        at = 0
        for a, (_, size) in enumerate(_REPLICATED):
            g = total[:, at:at + size]
            w_ref, m_ref, v_ref = ins[3 * a:3 * a + 3]
            g_ref, d_ref, m2_ref, v2_ref = outs[4 * a:4 * a + 4]
            g_ref[...] = g
            d_ref[...], m2_ref[...], v2_ref[...] = _adamw_step(g, w_ref[...], m_ref[...], v_ref[...])
            at += size
        outs[-1][...] = total[:, at:at + _LANES]

    operands, out_shapes = [], []
    for name, size in _REPLICATED:
        operands += [weights[name].reshape(1, size), moments_m[name].reshape(1, size), moments_v[name].reshape(1, size)]
        out_shapes += [jax.ShapeDtypeStruct((1, size), F32)] * 4
    out_shapes.append(jax.ShapeDtypeStruct((1, _LANES), F32))
    outs = pl.pallas_call(body, name="adamw_replicated", out_shape=out_shapes)(landing, *operands)
    return outs[-1][0, 0], {name: outs[4 * a:4 * a + 4] for a, (name, _) in enumerate(_REPLICATED)}


def _pad_rows(a, rows):
    return jnp.pad(a, ((0, rows - a.shape[0]), (0, 0)))


def _slabs(a):
    r, c = a.shape
    return a.reshape(r, N_DEV, c // N_DEV).transpose(1, 0, 2)


def _unslab(a):
    g, r, c = a.shape
    return a.transpose(1, 0, 2).reshape(r, g * c)


def _local_step(x, target, w, n_rows, ffn_weights, send_early_grads):
    cos_t, sin_t = _rope_tables(n_rows)
    cos, sin = cos_t.T, sin_t.T
    meta_pad, g1, gf = w["meta_pad"], w["mix_norm_g"], w["final_norm_g"]
    gq, gkv, gb_col = w["q_norm_g"], w["kv_norm_g"], w["attn_out_g"].reshape(D_ATTN, 1)
    nb, ag, cq, ckv, kr = _fwd_in(x, meta_pad, g1, w["w_in"], n_rows)
    mix_a, u1 = _fwd_conv(ag, w["conv_w"], w["conv_b"], w["conv_ln_g"], w["conv_ln_b"], w["conv_out_g"], n_rows)
    q_t, k, v, v_t, cqn, ckvn = _fwd_qkv(cq, ckv, kr, gq, gkv, w["wq_t"], w["w_ukv"], w["wv_t"], cos, sin, cos_t, sin_t, n_rows)
    o_t, lse = _attn_fwd(q_t, k, v_t, n_rows)
    w_out, w_up, w_down = ffn_weights()
    mix_bt, h1 = _fwd_out(x, meta_pad, mix_a, o_t, gb_col, w_out, n_rows)
    n2, up0, act, da, db, dh2, loss, dgf = _fwd_ffn(
        h1, target, w["ffn_norm_g"], w_up, w["fw"], w["fb"], w_down, gf, n_rows)

    dup, dfb = _bwd_ffn_act(dh2, da, db, w_down, n_rows)
    dup0, dh1, dfw, dg2 = _bwd_ffn_up(dup, up0, h1, dh2, w["ffn_norm_g"], w_up, w["fw"], n_rows)
    grad_w_out = jnp.concatenate([_weight_grad(mix_a, dh1, "grad_w_out_conv")[0],
                                  _weight_grad(mix_bt, dh1, "grad_w_out_attn", a_transposed=True)[0]], axis=0)
    send_early_grads(_weight_grad(dup0, n2, "grad_w_ffn_up"),
                     _weight_grad(act, dh2, "grad_w_ffn_down").reshape(N_DEV, D_FF // N_DEV, D_MODEL),
                     grad_w_out.reshape(N_DEV, D_MODEL // N_DEV, D_MODEL))
    do_t, delta, du1, dgb, dga, dlg, dlb, dcb = _bwd_out(
        dh1, o_t, u1, w_out, gb_col, w["conv_ln_g"], w["conv_ln_b"], w["conv_out_g"], n_rows)
    dq_t, dk, dv = _attn_bwd(q_t, k, v, do_t, lse, delta, n_rows)
    dqraw_t, dkv, dcq, dckv, dkr, dgq, dgkv = _bwd_qkv(
        dq_t, dk, dv, cq, ckv, gq, gkv, w["wq_t"], w["w_ukv"], cos, sin, cos_t, sin_t, n_rows)
    dag, dcw = _bwd_conv(du1, ag, w["conv_w"], n_rows)
    dz, gx, gmeta, dg1 = _bwd_in(dag, dcq, dckv, dkr, x, meta_pad, dh1, g1, w["w_in"], n_rows)

    sharded = {
        "w_in": _weight_grad(dz, nb, "grad_w_in")[0].reshape(N_DEV, D_IN // N_DEV, D_MODEL),
        "w_uq": _weight_grad(dqraw_t, cqn, "grad_w_uq", a_transposed=True),
        "w_ukv": _weight_grad(ckvn, dkv, "grad_w_ukv"),
        "conv_w": _slabs(dcw),
        "ffn_conv_w": dfw[:, :, :UP_SLAB],
        "meta_tokens": _slabs(gmeta[DEAD:]),
    }
    replicated = {
        "mix_norm_g": dg1, "q_norm_g": dgq, "kv_norm_g": dgkv, "conv_b": dcb, "conv_ln_g": dlg, "conv_ln_b": dlb,
        "conv_out_g": dga, "attn_out_g": dgb, "ffn_norm_g": dg2, "ffn_conv_b": dfb, "final_norm_g": dgf,
    }
    return loss[0, 0], gx, sharded, replicated


_SHARDED = (
    ("w_in", None, BF16), ("w_uq", None, BF16), ("w_ukv", None, BF16), ("w_out", None, BF16), ("w_ffn_up", None, BF16),
    ("w_ffn_down", None, BF16), ("conv_w", 32, F32), ("ffn_conv_w", 8, F32), ("meta_tokens", None, F32),
)
GATHER_LATE_ID = 3
EXCHANGE_EARLY_ID = 4
_LATE_WEIGHTS = ("w_out", "w_ffn_up", "w_ffn_down")
_COLUMN_SHARDS = ("w_in", "w_uq", "w_ffn_up")
_EARLY_GRADS = ("w_ffn_up", "w_ffn_down", "w_out")


def kernel(x, meta_tokens, mix_norm_g, w_in, q_norm_g, w_uq, kv_norm_g, w_ukv, conv_w, conv_b, conv_ln_g, conv_ln_b, conv_out_g, attn_out_g, w_out, ffn_norm_g, w_ffn_up, ffn_conv_w, ffn_conv_b, w_ffn_down, final_norm_g, loss_target, m_meta_tokens, m_mix_norm_g, m_w_in, m_q_norm_g, m_w_uq, m_kv_norm_g, m_w_ukv, m_conv_w, m_conv_b, m_conv_ln_g, m_conv_ln_b, m_conv_out_g, m_attn_out_g, m_w_out, m_ffn_norm_g, m_w_ffn_up, m_ffn_conv_w, m_ffn_conv_b, m_w_ffn_down, m_final_norm_g, v_meta_tokens, v_mix_norm_g, v_w_in, v_q_norm_g, v_w_uq, v_kv_norm_g, v_w_ukv, v_conv_w, v_conv_b, v_conv_ln_g, v_conv_ln_b, v_conv_out_g, v_attn_out_g, v_w_out, v_ffn_norm_g, v_w_ffn_up, v_ffn_conv_w, v_ffn_conv_b, v_w_ffn_down, v_final_norm_g):
    given = dict(locals())
    weights = {name: given[name] for name in _WEIGHT_ORDER}
    moments_m = {name: given["m_" + name] for name in _WEIGHT_ORDER}
    moments_v = {name: given["v_" + name] for name in _WEIGHT_ORDER}
    seq = x.shape[1]
    n_rows = ROW_TILE + seq

    def shard2d(name, a):
        a = a.reshape(a.shape[-2], a.shape[-1])
        return a.T if name in _COLUMN_SHARDS else a

    early = [entry for entry in _SHARDED if entry[0] not in _LATE_WEIGHTS]
    shards = []
    for name, pad_to, _ in early:
        s = shard2d(name, weights[name])
        shards.append(s if pad_to is None else _pad_rows(s, pad_to))
    gathered = dict(zip([name for name, _, _ in early], _all_gather(shards, [dt for _, _, dt in early])))
    behind = gathered["meta_tokens"][0, 0, 0] * 0.0
    late_parts = [(shard2d(name, weights[name]) + behind).astype(BF16) for name in _LATE_WEIGHTS]
    late = _sequencer_exchange(late_parts, [True] * len(late_parts), "gather_late", GATHER_LATE_ID)
    meta_full = _unslab(gathered["meta_tokens"])
    full = {
        "meta_pad": jnp.concatenate([jnp.zeros((DEAD, D_MODEL), F32), meta_full], axis=0),
        "w_in": gathered["w_in"].reshape(D_IN, D_MODEL),
        "wq_t": gathered["w_uq"],
        "w_ukv": gathered["w_ukv"],
        "wv_t": gathered["w_ukv"][:, :, QK_NOPE:].transpose(0, 2, 1),
        "conv_w": _unslab(gathered["conv_w"][:, :CONV_WIDTH]),
        "fw": jnp.pad(gathered["ffn_conv_w"][:, :FFN_CONV_WIDTH], ((0, 0), (0, 0), (0, UP_PAD - UP_SLAB))),
        "fb": jnp.pad(ffn_conv_b.reshape(N_DEV, 1, UP_SLAB), ((0, 0), (0, 0), (0, UP_PAD - UP_SLAB))),
        "final_norm_g": final_norm_g.reshape(1, D_MODEL),
    }
    for name in ("mix_norm_g", "q_norm_g", "kv_norm_g", "conv_b", "conv_ln_g", "conv_ln_b", "conv_out_g", "attn_out_g",
                 "ffn_norm_g"):
        full[name] = weights[name]

    def ffn_weights():
        w_out_all, w_up_all, w_down_all = late
        return (w_out_all.reshape(D_MODEL, D_MODEL), w_up_all, w_down_all.reshape(N_ACT_SLAB, UP_SLAB, D_MODEL))

    early_landed = []

    def send_early_grads(*grads):
        early_landed.extend(_sequencer_exchange(list(grads), [False] * len(grads), "exchange_early", EXCHANGE_EARLY_ID))

    loss, gx, sharded, replicated = _local_step(x[0], loss_target[0], full, n_rows, ffn_weights, send_early_grads)

    rest = [entry for entry in _SHARDED if entry[0] not in _EARLY_GRADS]
    parts, whole = [], []
    for name, pad_to, dt in rest:
        p = sharded[name].astype(dt)
        parts.append(p if pad_to is None else jnp.pad(p, ((0, 0), (0, pad_to - p.shape[1]), (0, 0))))
        whole.append(False)
    parts.append(_pack_replicated(replicated, loss))
    whole.append(True)
    landed = _exchange(parts, whole)
    landing = dict(zip([name for name, _, _ in rest], landed[:-1]))
    landing.update(zip(_EARLY_GRADS, early_landed))

    grad, delta, new_m, new_v = {}, {}, {}, {}
    for name, pad_to, _ in _SHARDED:
        land = landing[name]
        ws, ms, vs = (shard2d(name, a[name]) for a in (weights, moments_m, moments_v))
        rows = ws.shape[0]
        if pad_to is not None:
            ws, ms, vs = _pad_rows(ws, pad_to), _pad_rows(ms, pad_to), _pad_rows(vs, pad_to)
        outs = _adamw(land, ws, ms, vs, "adamw_" + name)
        shape = weights[name].shape
        grad[name], delta[name], new_m[name], new_v[name] = (
            (o.T if name in _COLUMN_SHARDS else o[:rows]).reshape(shape) for o in outs)
    loss, updates = _adamw_replicated(landed[-1], weights, moments_m, moments_v)
    for name, outs in updates.items():
        grad[name], delta[name], new_m[name], new_v[name] = (o.reshape(weights[name].shape) for o in outs)

    return (loss, gx[None], *[grad[n] for n in _WEIGHT_ORDER], *[delta[n] for n in _WEIGHT_ORDER],
            *[new_m[n] for n in _WEIGHT_ORDER], *[new_v[n] for n in _WEIGHT_ORDER])
```

```python
import functools

import jax
import jax.numpy as jnp
from jax import lax
from jax.experimental import pallas as pl
from jax.experimental.pallas import tpu as pltpu
from jax.experimental.pallas import tpu_sc as plsc

F32 = jnp.float32
BF16 = jnp.bfloat16

N_DEV = 8
D_MODEL = 1024
CHUNK = 64
CHUNK_SHIFT = 6
N_META = 16
D_CONV = 512
CONV_WIDTH = 31
N_HEADS = 8
QK_NOPE = 64
QK_ROPE = 32
QK_DIM = QK_NOPE + QK_ROPE
V_HEAD = 64
KV_HEAD = QK_NOPE + V_HEAD
D_ATTN = N_HEADS * V_HEAD
Q_LORA = 384
KV_LORA = 256
ROPE_THETA = 10000.0
D_IN = 2 * D_CONV + Q_LORA + KV_LORA + QK_ROPE
D_FF = 2816
D_UP = 2 * D_FF
FFN_CONV_WIDTH = 3
UP_SLAB = D_UP // N_DEV
N_ACT_SLAB = D_FF // UP_SLAB
EPS = 1e-6
NEG = -1e30
_LN2 = 0.6931471805599453
QK_LOGIT_SCALE = QK_DIM ** -0.5 / _LN2
ADAM_LR = 0.001
ADAM_B1 = 0.9
ADAM_B2 = 0.999
ADAM_EPS = 1e-08
ADAM_WD = 0.01
ADAM_STEP = 10

ROW_TILE = 256
DEAD = ROW_TILE - N_META
CONV_HALO = 32
FFN_HALO = 16
VMEM_LIMIT = 56 * 1024 * 1024
_LANES = 128

MESH = pl.DeviceIdType.MESH


def _dot(a, b):
    return jnp.dot(a, b, preferred_element_type=F32)


def _dot_nt(a, b):
    return lax.dot_general(a, b, (((1,), (1,)), ((), ())), preferred_element_type=F32)


def _dot_tn(a, b):
    return lax.dot_general(a, b, (((0,), (0,)), ((), ())), preferred_element_type=F32)


def _sigmoid(x):
    return 1.0 / (1.0 + jnp.exp2(x * (-1.0 / _LN2)))


def _rms_fwd(x, g):
    r = lax.rsqrt(jnp.mean(x * x, axis=-1, keepdims=True) + EPS)
    return x * r * g


def _rms_bwd(dy, x, g):
    r = lax.rsqrt(jnp.mean(x * x, axis=-1, keepdims=True) + EPS)
    w = dy * g
    dx = r * w - x * (r * r * r) * jnp.mean(w * x, axis=-1, keepdims=True)
    return dx, jnp.sum(dy * x * r, axis=0, keepdims=True)


def _rope(x, cos, sin):
    half = QK_ROPE // 2
    x1, x2 = x[:, :half], x[:, half:]
    return jnp.concatenate([x1 * cos - x2 * sin, x2 * cos + x1 * sin], axis=-1)


def _rope_t(dy, cos, sin):
    half = QK_ROPE // 2
    d1, d2 = dy[:, :half], dy[:, half:]
    return jnp.concatenate([d1 * cos + d2 * sin, d2 * cos - d1 * sin], axis=-1)


def _row_ids(i, rows):
    return i * rows + lax.broadcasted_iota(jnp.int32, (rows, 1), 0)


def _accumulate(ref, first, value):
    @pl.when(first)
    def _():
        ref[...] = value

    @pl.when(jnp.logical_not(first))
    def _():
        ref[...] += value


def _tile_spec(shape):
    nd = len(shape)
    if nd == 2:
        return pl.BlockSpec((ROW_TILE, shape[1]), lambda i: (i, 0))
    return pl.BlockSpec((shape[0], ROW_TILE, shape[2]), lambda i: (0, i, 0))


def _whole_spec(shape):
    nd = len(shape)
    return pl.BlockSpec(tuple(shape), lambda i: (0,) * nd, pipeline_mode=pl.Buffered(1))


def _acc_spec(shape):
    nd = len(shape)
    return pl.BlockSpec(tuple(shape), lambda i: (0,) * nd)


def _real_spec(width):
    return pl.BlockSpec((ROW_TILE, width), lambda i: (jnp.maximum(i - 1, 0), 0))


def _params(*semantics):
    return pltpu.CompilerParams(dimension_semantics=semantics, vmem_limit_bytes=VMEM_LIMIT)


def _fwd_in(x, meta_pad, g1, w_in, n_rows):
    nt = n_rows // ROW_TILE

    def body(x_ref, meta_ref, g_ref, w_ref, nb_ref, ag_ref, cq_ref, ckv_ref, kr_ref):
        i = pl.program_id(0)
        h0 = jnp.where(i == 0, meta_ref[...], x_ref[...])
        nb = _rms_fwd(h0, g_ref[...]).astype(BF16)
        nb_ref[...] = nb
        z = _dot_nt(nb, w_ref[...])
        ag_ref[...] = z[:, :2 * D_CONV]
        cq_ref[...] = z[:, 2 * D_CONV:2 * D_CONV + Q_LORA]
        ckv_ref[...] = z[:, 2 * D_CONV + Q_LORA:2 * D_CONV + Q_LORA + KV_LORA]
        kr_ref[...] = z[:, 2 * D_CONV + Q_LORA + KV_LORA:]

    out_shapes = [
        jax.ShapeDtypeStruct((n_rows, D_MODEL), BF16),
        jax.ShapeDtypeStruct((n_rows, 2 * D_CONV), F32),
        jax.ShapeDtypeStruct((n_rows, Q_LORA), F32),
        jax.ShapeDtypeStruct((n_rows, KV_LORA), F32),
        jax.ShapeDtypeStruct((n_rows, QK_ROPE), F32),
    ]
    return pl.pallas_call(
        body, name="fwd_in", grid=(nt,),
        in_specs=[_real_spec(D_MODEL), _whole_spec(meta_pad.shape), _whole_spec(g1.shape), _whole_spec(w_in.shape)],
        out_specs=[_tile_spec(s.shape) for s in out_shapes],
        out_shape=out_shapes,
        compiler_params=_params("parallel"),
    )(x, meta_pad, g1, w_in)


def _conv_chain(u1, ln_g, ln_b):
    mu = jnp.mean(u1, axis=-1, keepdims=True)
    xc = u1 - mu
    rstd = lax.rsqrt(jnp.mean(xc * xc, axis=-1, keepdims=True) + EPS)
    xh = xc * rstd
    u2 = xh * ln_g + ln_b
    return xh, u2, u2 * _sigmoid(u2), rstd


def _fwd_conv(ag, conv_w, conv_b, ln_g, ln_b, out_g, n_rows):
    nt = n_rows // ROW_TILE

    def body(ag_ref, w_ref, b_ref, lg_ref, lb_ref, og_ref, mix_ref, u1_ref, ext_ref, conv_ref):
        i = pl.program_id(0)

        @pl.when(i == 0)
        def _():
            ext_ref[:, 0:CONV_HALO, :] = jnp.zeros((CONV_PLANES, CONV_HALO, _LANES), F32)

        ag_t = ag_ref[...]
        live = _row_ids(i, ROW_TILE) >= DEAD
        u0 = jnp.where(live, ag_t[:, :D_CONV] * _sigmoid(ag_t[:, D_CONV:]), 0.0)
        _to_planes(ext_ref, (), slice(CONV_HALO, None), u0)
        first = CONV_HALO - (CONV_WIDTH - 1)
        for c in range(CONV_PLANES):
            taps = w_ref[:, c * _LANES:(c + 1) * _LANES]
            for p in range(PHASES):
                acc = jnp.zeros((PHASE_ROWS, _LANES), F32)
                for k in range(CONV_WIDTH):
                    acc = acc + taps[k:k + 1, :] * ext_ref[c, _phase(first + k + p), :]
                conv_ref[c, _phase(p), :] = acc
        ext_ref[:, 0:CONV_HALO, :] = ext_ref[:, ROW_TILE:ROW_TILE + CONV_HALO, :]
        u1 = _from_planes(conv_ref, (), D_CONV) + b_ref[...]
        u1_ref[...] = u1
        _, _, u3, _ = _conv_chain(u1, lg_ref[...], lb_ref[...])
        mix_ref[...] = _rms_fwd(u3, og_ref[...]).astype(BF16)

    out_shapes = [jax.ShapeDtypeStruct((n_rows, D_CONV), BF16), jax.ShapeDtypeStruct((n_rows, D_CONV), F32)]
    small = [conv_w, conv_b, ln_g, ln_b, out_g]
    return pl.pallas_call(
        body, name="fwd_conv", grid=(nt,),
        in_specs=[_tile_spec(ag.shape)] + [_whole_spec(a.shape) for a in small],
        out_specs=[_tile_spec(s.shape) for s in out_shapes],
        out_shape=out_shapes,
        scratch_shapes=[pltpu.VMEM((CONV_PLANES, ROW_TILE + CONV_HALO, _LANES), F32),
                        pltpu.VMEM((CONV_PLANES, ROW_TILE, _LANES), F32)],
        compiler_params=_params("arbitrary"),
    )(ag, *small)


def _lane_tile(shape):
    if len(shape) == 2:
        return pl.BlockSpec((shape[0], ROW_TILE), lambda i: (0, i))
    return pl.BlockSpec((shape[0], shape[1], ROW_TILE), lambda i: (0, 0, i))


def _rope_rows(x, cos, sin):
    half = QK_ROPE // 2
    x1, x2 = x[:half], x[half:]
    return jnp.concatenate([x1 * cos - x2 * sin, x2 * cos + x1 * sin], axis=0)


def _rope_rows_t(dy, cos, sin):
    half = QK_ROPE // 2
    d1, d2 = dy[:half], dy[half:]
    return jnp.concatenate([d1 * cos + d2 * sin, d2 * cos - d1 * sin], axis=0)


def _fwd_qkv(cq, ckv, kr, gq, gkv, wq_t, w_ukv, wv_t, cos, sin, cos_t, sin_t, n_rows):
    nt = n_rows // ROW_TILE

    def body(cq_ref, ckv_ref, kr_ref, gq_ref, gkv_ref, wqt_ref, wkv_ref, wvt_ref, cos_ref, sin_ref, cost_ref, sint_ref,
             qt_ref, k_ref, v_ref, vt_ref, cqn_ref, ckvn_ref):
        cqn = _rms_fwd(cq_ref[...], gq_ref[...]).astype(BF16)
        ckvn = _rms_fwd(ckv_ref[...], gkv_ref[...]).astype(BF16)
        cqn_ref[...] = cqn
        ckvn_ref[...] = ckvn
        k_rot = _rope(kr_ref[...], cos_ref[...], sin_ref[...])
        cos_rows, sin_rows = cost_ref[...], sint_ref[...]
        for h in range(N_HEADS):
            q_raw = _dot_nt(wqt_ref[h], cqn)
            q_h = jnp.concatenate([q_raw[:QK_NOPE], _rope_rows(q_raw[QK_NOPE:], cos_rows, sin_rows)], axis=0)
            qt_ref[h] = (q_h * QK_LOGIT_SCALE).astype(BF16)
            kv = _dot(ckvn, wkv_ref[h])
            k_ref[h] = jnp.concatenate([kv[:, :QK_NOPE], k_rot], axis=-1).astype(BF16)
            v_ref[h] = kv[:, QK_NOPE:].astype(BF16)
            vt_ref[h] = _dot_nt(wvt_ref[h], ckvn).astype(BF16)

    out_shapes = [
        jax.ShapeDtypeStruct((N_HEADS, QK_DIM, n_rows), BF16),
        jax.ShapeDtypeStruct((N_HEADS, n_rows, QK_DIM), BF16),
        jax.ShapeDtypeStruct((N_HEADS, n_rows, V_HEAD), BF16),
        jax.ShapeDtypeStruct((N_HEADS, V_HEAD, n_rows), BF16),
        jax.ShapeDtypeStruct((n_rows, Q_LORA), BF16),
        jax.ShapeDtypeStruct((n_rows, KV_LORA), BF16),
    ]
    tiles = [cq, ckv, kr]
    whole = [gq, gkv, wq_t, w_ukv, wv_t]
    out_specs = [_lane_tile(out_shapes[0].shape), _tile_spec(out_shapes[1].shape), _tile_spec(out_shapes[2].shape),
                 _lane_tile(out_shapes[3].shape), _tile_spec(out_shapes[4].shape), _tile_spec(out_shapes[5].shape)]
    return pl.pallas_call(
        body, name="fwd_qkv", grid=(nt,),
        in_specs=[_tile_spec(a.shape) for a in tiles] + [_whole_spec(a.shape) for a in whole]
        + [_tile_spec(cos.shape), _tile_spec(sin.shape), _lane_tile(cos_t.shape), _lane_tile(sin_t.shape)],
        out_specs=out_specs,
        out_shape=out_shapes,
        compiler_params=_params("parallel"),
    )(*tiles, *whole, cos, sin, cos_t, sin_t)


def _chunk_of(rows):
    return jnp.where(rows >= ROW_TILE, lax.shift_right_arithmetic(rows - ROW_TILE, CHUNK_SHIFT) + 1, 0)


def _visible(i, j):
    k_rows = j * ROW_TILE + lax.broadcasted_iota(jnp.int32, (ROW_TILE, 1), 0)
    q_rows = i * ROW_TILE + lax.broadcasted_iota(jnp.int32, (1, ROW_TILE), 1)
    return jnp.logical_and(_chunk_of(q_rows) >= _chunk_of(k_rows), k_rows >= DEAD)


def _attn_fwd(q_t, k, v_t, n_rows):
    nt = n_rows // ROW_TILE

    def body(qt_ref, k_ref, vt_ref, ot_ref, lse_ref):
        i = pl.program_id(0)
        q_ts = [qt_ref[h] for h in range(N_HEADS)]

        def make_step(masked):
            def step(j, carry):
                rows = pl.ds(pl.multiple_of(j * ROW_TILE, ROW_TILE), ROW_TILE)
                scores = [_dot(k_ref[h, rows, :], q_ts[h]) for h in range(N_HEADS)]
                visible = _visible(i, j) if masked else None
                probs, state = [], []
                for h in range(N_HEADS):
                    m, l, _ = carry[h]
                    s = jnp.where(visible, scores[h], NEG) if masked else scores[h]
                    m_new = jnp.maximum(m, jnp.max(s, axis=0, keepdims=True))
                    alpha = jnp.exp2(m - m_new)
                    p = jnp.exp2(s - m_new)
                    probs.append(p.astype(BF16))
                    state.append((m_new, alpha * l + jnp.sum(p, axis=0, keepdims=True), alpha))
                outs = [_dot(vt_ref[h, :, rows], probs[h]) for h in range(N_HEADS)]
                return tuple((state[h][0], state[h][1], state[h][2] * carry[h][2] + outs[h]) for h in range(N_HEADS))
            return step

        init = tuple((jnp.full((1, ROW_TILE), NEG, F32), jnp.zeros((1, ROW_TILE), F32),
                      jnp.zeros((V_HEAD, ROW_TILE), F32)) for _ in range(N_HEADS))
        carry = make_step(True)(0, init)
        carry = lax.fori_loop(1, i, make_step(False), carry)
        carry = lax.fori_loop(jnp.maximum(i, 1), i + 1, make_step(True), carry)
        for h in range(N_HEADS):
            m, l, acc = carry[h]
            ot_ref[h] = acc / l
            lse_ref[h] = m + jnp.log2(l)

    out_shapes = [jax.ShapeDtypeStruct((N_HEADS, V_HEAD, n_rows), F32), jax.ShapeDtypeStruct((N_HEADS, 1, n_rows), F32)]
    return pl.pallas_call(
        body, name="attn_fwd", grid=(nt,),
        in_specs=[_lane_tile(q_t.shape), _whole_spec(k.shape), _whole_spec(v_t.shape)],
        out_specs=[_lane_tile(s.shape) for s in out_shapes],
        out_shape=out_shapes,
        compiler_params=_params("parallel"),
    )(q_t, k, v_t)


def _heads_to_rows(ref):
    return jnp.concatenate([ref[h] for h in range(N_HEADS)], axis=0)


def _rms_cols(x, g_col):
    r = lax.rsqrt(jnp.mean(x * x, axis=0, keepdims=True) + EPS)
    return x * r * g_col


def _fwd_out(x, meta_pad, mix_a, o_t, gb_col, w_out, n_rows):
    nt = n_rows // ROW_TILE

    def body(x_ref, meta_ref, mixa_ref, ot_ref, gb_ref, w_ref, mixbt_ref, h1_ref):
        i = pl.program_id(0)
        h0 = jnp.where(i == 0, meta_ref[...], x_ref[...])
        mix_bt = _rms_cols(_heads_to_rows(ot_ref), gb_ref[...]).astype(BF16)
        mixbt_ref[...] = mix_bt
        h1_ref[...] = h0 + _dot(mixa_ref[...], w_ref[:D_CONV, :]) + _dot_tn(mix_bt, w_ref[D_CONV:, :])

    out_shapes = [jax.ShapeDtypeStruct((D_ATTN, n_rows), BF16), jax.ShapeDtypeStruct((n_rows, D_MODEL), F32)]
    return pl.pallas_call(
        body, name="fwd_out", grid=(nt,),
        in_specs=[_real_spec(D_MODEL), _whole_spec(meta_pad.shape), _tile_spec(mix_a.shape), _lane_tile(o_t.shape),
                  _whole_spec(gb_col.shape), _whole_spec(w_out.shape)],
        out_specs=[_lane_tile(out_shapes[0].shape), _tile_spec(out_shapes[1].shape)],
        out_shape=out_shapes,
        compiler_params=_params("parallel"),
    )(x, meta_pad, mix_a, o_t, gb_col, w_out)


PHASES = 8
PHASE_ROWS = ROW_TILE // PHASES
UP_PLANES = -(-UP_SLAB // _LANES)
UP_PAD = UP_PLANES * _LANES
CONV_PLANES = D_CONV // _LANES


def _phase(start):
    return pl.ds(start, PHASE_ROWS, stride=PHASES)


def _to_planes(ref, lead, rows, value):
    width = value.shape[-1]
    for c in range(-(-width // _LANES)):
        part = value[:, c * _LANES:min((c + 1) * _LANES, width)]
        if part.shape[-1] < _LANES:
            part = jnp.concatenate([part, jnp.zeros((part.shape[0], _LANES - part.shape[-1]), part.dtype)], axis=-1)
        ref[(*lead, c, rows, slice(None))] = part


def _from_planes(ref, lead, width):
    planes = [ref[(*lead, c)] for c in range(-(-width // _LANES))]
    last = width - (len(planes) - 1) * _LANES
    return jnp.concatenate(planes[:-1] + [planes[-1][:, :last]], axis=-1)


def _fwd_ffn(h1, target, g2, w_up, fw, fb, w_down, gf, n_rows):
    nt = n_rows // ROW_TILE

    def body(h1_ref, t_ref, g2_ref, wup_ref, fw_ref, fb_ref, wdn_ref, gf_ref,
             n2_ref, up0_ref, act_ref, da_ref, db_ref, dh2_ref, loss_ref, dgf_ref, ext_ref):
        i = pl.program_id(0)

        @pl.when(i == 0)
        def _():
            ext_ref[:, 0:FFN_HALO, :] = jnp.zeros((N_DEV, FFN_HALO, UP_SLAB), F32)

        h1_t = h1_ref[...]
        n2 = _rms_fwd(h1_t, g2_ref[...]).astype(BF16)
        n2_ref[...] = n2
        for s in range(N_DEV):
            up0 = _dot_nt(n2, wup_ref[s])
            up0_ref[s] = up0.astype(BF16)
            ext_ref[s, FFN_HALO:, :] = up0

        @pl.when(i == 0)
        def _():
            up0_ref[:, 0:DEAD, :] = jnp.zeros((N_DEV, DEAD, UP_SLAB), BF16)
            ext_ref[:, FFN_HALO:FFN_HALO + DEAD, :] = jnp.zeros((N_DEV, DEAD, UP_SLAB), F32)

        first = FFN_HALO - (FFN_CONV_WIDTH - 1)

        def conv(s):
            acc = fb_ref[s, :, :UP_SLAB]
            for k in range(FFN_CONV_WIDTH):
                acc = acc + fw_ref[s, k:k + 1, :UP_SLAB] * ext_ref[s, first + k:first + k + ROW_TILE, :]
            return acc

        h2 = h1_t
        for s in range(N_ACT_SLAB):
            gate = conv(s)
            val = conv(s + N_ACT_SLAB)
            sg = _sigmoid(gate)
            silu = gate * sg
            act = (silu * val).astype(BF16)
            act_ref[s] = act
            da_ref[s] = (val * sg * (1.0 + gate * (1.0 - sg))).astype(BF16)
            db_ref[s] = silu.astype(BF16)
            h2 = h2 + _dot(act, wdn_ref[s])
        ext_ref[:, 0:FFN_HALO, :] = ext_ref[:, ROW_TILE:ROW_TILE + FFN_HALO, :]

        gf_t = gf_ref[...]
        y = _rms_fwd(h2, gf_t)
        diff = jnp.where(i >= 1, y - t_ref[...], 0.0)
        tile_loss = 0.5 * jnp.sum(jnp.sum(diff * diff, axis=-1, keepdims=True), axis=0, keepdims=True) / D_MODEL
        dh2, dgf = _rms_bwd(diff / D_MODEL, h2, gf_t)
        dh2_ref[...] = dh2
        _accumulate(loss_ref, i == 0, jnp.broadcast_to(tile_loss, loss_ref.shape))
        _accumulate(dgf_ref, i == 0, dgf)

    act_like = jax.ShapeDtypeStruct((N_ACT_SLAB, n_rows, UP_SLAB), BF16)
    out_shapes = [
        jax.ShapeDtypeStruct((n_rows, D_MODEL), BF16),
        jax.ShapeDtypeStruct((N_DEV, n_rows, UP_SLAB), BF16),
        act_like, act_like, act_like,
        jax.ShapeDtypeStruct((n_rows, D_MODEL), F32),
        jax.ShapeDtypeStruct((8, 128), F32),
        jax.ShapeDtypeStruct((1, D_MODEL), F32),
    ]
    whole = [g2, w_up, fw, fb, w_down, gf]
    return pl.pallas_call(
        body, name="fwd_ffn", grid=(nt,),
        in_specs=[_tile_spec(h1.shape), _real_spec(D_MODEL)] + [_whole_spec(a.shape) for a in whole],
        out_specs=[_tile_spec(s.shape) for s in out_shapes[:6]] + [_acc_spec(s.shape) for s in out_shapes[6:]],
        out_shape=out_shapes,
        scratch_shapes=[pltpu.VMEM((N_DEV, ROW_TILE + FFN_HALO, UP_SLAB), F32)],
        compiler_params=_params("arbitrary"),
    )(h1, target, *whole)


def _rope_tables(n_rows):
    pos = jnp.maximum(jnp.arange(n_rows, dtype=jnp.int32) - DEAD, 0)
    inv_freq = 1.0 / (ROPE_THETA ** (jnp.arange(0, QK_ROPE, 2, dtype=F32) / QK_ROPE))
    ang_t = inv_freq[:, None] * pos.astype(F32)[None, :]
    return jnp.cos(ang_t), jnp.sin(ang_t)


def _halo_after(shape, halo, n_rows):
    last = n_rows // halo - 1
    step = ROW_TILE // halo
    if len(shape) == 2:
        return pl.BlockSpec((halo, shape[1]), lambda i: (jnp.minimum((i + 1) * step, last), 0))
    return pl.BlockSpec((shape[0], halo, shape[2]), lambda i: (0, jnp.minimum((i + 1) * step, last), 0))


def _halo_before(shape, halo):
    step = ROW_TILE // halo
    if len(shape) == 2:
        return pl.BlockSpec((halo, shape[1]), lambda i: (jnp.maximum(i * step - 1, 0), 0))
    return pl.BlockSpec((shape[0], halo, shape[2]), lambda i: (0, jnp.maximum(i * step - 1, 0), 0))


def _bwd_ffn_act(dh2, da, db, w_down, n_rows):
    nt = n_rows // ROW_TILE

    def body(dh2_ref, da_ref, db_ref, wdn_ref, dup_ref, dfb_ref):
        i = pl.program_id(0)

        @pl.when(i == 0)
        def _():
            dfb_ref[...] = jnp.zeros_like(dfb_ref)

        dh2_b = dh2_ref[...].astype(BF16)
        for s in range(N_ACT_SLAB):
            d_act = _dot_nt(dh2_b, wdn_ref[s])
            d_gate = d_act * da_ref[s].astype(F32)
            d_val = d_act * db_ref[s].astype(F32)
            dup_ref[s] = d_gate.astype(BF16)
            dup_ref[s + N_ACT_SLAB] = d_val.astype(BF16)
            dfb_ref[s] += jnp.sum(d_gate, axis=0, keepdims=True)
            dfb_ref[s + N_ACT_SLAB] += jnp.sum(d_val, axis=0, keepdims=True)

    out_shapes = [jax.ShapeDtypeStruct((N_DEV, n_rows, UP_SLAB), BF16), jax.ShapeDtypeStruct((N_DEV, 1, UP_SLAB), F32)]
    return pl.pallas_call(
        body, name="bwd_ffn_act", grid=(nt,),
        in_specs=[_tile_spec(dh2.shape), _tile_spec(da.shape), _tile_spec(db.shape), _whole_spec(w_down.shape)],
        out_specs=[_tile_spec(out_shapes[0].shape), _acc_spec(out_shapes[1].shape)],
        out_shape=out_shapes,
        compiler_params=_params("arbitrary"),
    )(dh2, da, db, w_down)


def _bwd_ffn_up(dup, up0, h1, dh2, g2, w_up, fw, n_rows):
    nt = n_rows // ROW_TILE
    last_tap = FFN_CONV_WIDTH - 1

    def body(dup_ref, dnext_ref, up0_ref, h1_ref, dh2_ref, g2_ref, wup_ref, fw_ref,
             dup0_ref, dh1_ref, dfw_ref, dg2_ref, dext_ref, uext_ref, conv_ref):
        i = pl.program_id(0)

        @pl.when(i == 0)
        def _():
            dfw_ref[...] = jnp.zeros_like(dfw_ref)

        live = _row_ids(i, ROW_TILE) >= DEAD
        dn2 = jnp.zeros((ROW_TILE, D_MODEL), F32)
        for s in range(N_DEV):
            _to_planes(dext_ref, (), slice(0, ROW_TILE), dup_ref[s].astype(F32))
            _to_planes(dext_ref, (), slice(ROW_TILE, None), jnp.where(i == nt - 1, 0.0, dnext_ref[s].astype(F32)))
            _to_planes(uext_ref, (), slice(None), up0_ref[s].astype(F32))
            for c in range(UP_PLANES):
                lanes = slice(c * _LANES, (c + 1) * _LANES)
                taps = [fw_ref[s, k:k + 1, lanes] for k in range(FFN_CONV_WIDTH)]
                sums = [jnp.zeros((PHASE_ROWS, _LANES), F32) for _ in range(FFN_CONV_WIDTH)]
                for p in range(PHASES):
                    u = uext_ref[c, _phase(p), :]
                    acc = jnp.zeros((PHASE_ROWS, _LANES), F32)
                    for k in range(FFN_CONV_WIDTH):
                        shifted = dext_ref[c, _phase(p + last_tap - k), :]
                        acc = acc + taps[k] * shifted
                        sums[k] = sums[k] + shifted * u
                    conv_ref[c, _phase(p), :] = acc
                for k in range(FFN_CONV_WIDTH):
                    dfw_ref[s, k:k + 1, lanes] += jnp.sum(sums[k], axis=0, keepdims=True)
            dup0_b = jnp.where(live, _from_planes(conv_ref, (), UP_SLAB), 0.0).astype(BF16)
            dup0_ref[s] = dup0_b
            dn2 = dn2 + _dot(dup0_b, wup_ref[s])
        dx, dg2 = _rms_bwd(dn2, h1_ref[...], g2_ref[...])
        dh1_ref[...] = dh2_ref[...] + dx
        _accumulate(dg2_ref, i == 0, dg2)

    out_shapes = [
        jax.ShapeDtypeStruct((N_DEV, n_rows, UP_SLAB), BF16),
        jax.ShapeDtypeStruct((n_rows, D_MODEL), F32),
        jax.ShapeDtypeStruct((N_DEV, FFN_CONV_WIDTH, UP_PAD), F32),
        jax.ShapeDtypeStruct((1, D_MODEL), F32),
    ]
    return pl.pallas_call(
        body, name="bwd_ffn_up", grid=(nt,),
        in_specs=[_tile_spec(dup.shape), _halo_after(dup.shape, FFN_HALO, n_rows), _tile_spec(up0.shape),
                  _tile_spec(h1.shape), _tile_spec(dh2.shape),
                  _whole_spec(g2.shape), _whole_spec(w_up.shape), _whole_spec(fw.shape)],
        out_specs=[_tile_spec(s.shape) for s in out_shapes[:2]] + [_acc_spec(s.shape) for s in out_shapes[2:]],
        out_shape=out_shapes,
        scratch_shapes=[pltpu.VMEM((UP_PLANES, ROW_TILE + FFN_HALO, _LANES), F32),
                        pltpu.VMEM((UP_PLANES, ROW_TILE, _LANES), F32), pltpu.VMEM((UP_PLANES, ROW_TILE, _LANES), F32)],
        compiler_params=_params("arbitrary"),
    )(dup, dup, up0, h1, dh2, g2, w_up, fw)


def _bwd_out(dh1, o_t, u1, w_out, gb_col, ln_g, ln_b, ga, n_rows):
    nt = n_rows // ROW_TILE

    def body(dh1_ref, ot_ref, u1_ref, w_ref, gb_ref, lg_ref, lb_ref, ga_ref,
             dot_ref, delta_ref, du1_ref, dgb_ref, dga_ref, dlg_ref, dlb_ref, dcb_ref):
        i = pl.program_id(0)
        dh1_b = dh1_ref[...].astype(BF16)
        o_t = _heads_to_rows(ot_ref)
        gb = gb_ref[...]
        r = lax.rsqrt(jnp.mean(o_t * o_t, axis=0, keepdims=True) + EPS)
        dmix_bt = _dot_nt(w_ref[D_CONV:, :], dh1_b)
        wgt = dmix_bt * gb
        do_t = r * wgt - o_t * (r * r * r) * jnp.mean(wgt * o_t, axis=0, keepdims=True)
        dgb = jnp.sum(dmix_bt * o_t * r, axis=1, keepdims=True)
        for h in range(N_HEADS):
            do_h = do_t[h * V_HEAD:(h + 1) * V_HEAD]
            dot_ref[h] = do_h.astype(BF16)
            delta_ref[h] = jnp.sum(do_h * ot_ref[h], axis=0, keepdims=True)
        lg = lg_ref[...]
        xh, u2, u3, rstd = _conv_chain(u1_ref[...], lg, lb_ref[...])
        du3, dga = _rms_bwd(_dot_nt(dh1_b, w_ref[:D_CONV, :]), u3, ga_ref[...])
        sg = _sigmoid(u2)
        du2 = du3 * sg * (1.0 + u2 * (1.0 - sg))
        dxh = du2 * lg
        du1 = rstd * (dxh - jnp.mean(dxh, axis=-1, keepdims=True) - xh * jnp.mean(dxh * xh, axis=-1, keepdims=True))
        du1_ref[...] = du1
        first = i == 0
        _accumulate(dgb_ref, first, dgb)
        _accumulate(dga_ref, first, dga)
        _accumulate(dlg_ref, first, jnp.sum(du2 * xh, axis=0, keepdims=True))
        _accumulate(dlb_ref, first, jnp.sum(du2, axis=0, keepdims=True))
        _accumulate(dcb_ref, first, jnp.sum(du1, axis=0, keepdims=True))

    out_shapes = [
        jax.ShapeDtypeStruct((N_HEADS, V_HEAD, n_rows), BF16),
        jax.ShapeDtypeStruct((N_HEADS, 1, n_rows), F32),
        jax.ShapeDtypeStruct((n_rows, D_CONV), F32),
        jax.ShapeDtypeStruct((D_ATTN, 1), F32),
    ] + [jax.ShapeDtypeStruct((1, D_CONV), F32)] * 4
    whole = [w_out, gb_col, ln_g, ln_b, ga]
    return pl.pallas_call(
        body, name="bwd_out", grid=(nt,),
        in_specs=[_tile_spec(dh1.shape), _lane_tile(o_t.shape), _tile_spec(u1.shape)] + [_whole_spec(a.shape) for a in whole],
        out_specs=[_lane_tile(out_shapes[0].shape), _lane_tile(out_shapes[1].shape), _tile_spec(out_shapes[2].shape)]
        + [_acc_spec(s.shape) for s in out_shapes[3:]],
        out_shape=out_shapes,
        compiler_params=_params("arbitrary"),
    )(dh1, o_t, u1, *whole)


ATTN_BWD_HEADS = 4


def _attn_bwd(q_t, k, v, do_t, lse, delta, n_rows):
    nt = n_rows // ROW_TILE
    hp = ATTN_BWD_HEADS

    def body(k_ref, v_ref, qt_ref, dot_ref, lse_ref, delta_ref, dqt_ref, dk_ref, dv_ref):
        j = pl.program_id(1)

        @pl.when(j == 0)
        def _():
            dqt_ref[...] = jnp.zeros_like(dqt_ref)

        k_ts = [k_ref[h] for h in range(hp)]
        v_ts = [v_ref[h] for h in range(hp)]

        def make_step(masked):
            def step(i, carry):
                cols = pl.ds(pl.multiple_of(i * ROW_TILE, ROW_TILE), ROW_TILE)
                q_is = [qt_ref[h, :, cols] for h in range(hp)]
                do_is = [dot_ref[h, :, cols] for h in range(hp)]
                scores = [_dot(k_ts[h], q_is[h]) for h in range(hp)]
                dps = [_dot(v_ts[h], do_is[h]) for h in range(hp)]
                visible = _visible(i, j) if masked else None
                probs, dss = [], []
                for h in range(hp):
                    s = jnp.where(visible, scores[h], NEG) if masked else scores[h]
                    p = jnp.exp2(s - lse_ref[h, :, cols])
                    probs.append(p.astype(BF16))
                    dss.append((p * (dps[h] - delta_ref[h, :, cols])).astype(BF16))
                out = []
                for h in range(hp):
                    dk, dv = carry[h]
                    dv = dv + _dot_nt(probs[h], do_is[h])
                    dk = dk + _dot_nt(dss[h], q_is[h])
                    dqt_ref[h, :, cols] += _dot_tn(k_ts[h], dss[h])
                    out.append((dk, dv))
                return tuple(out)
            return step

        init = tuple((jnp.zeros((ROW_TILE, QK_DIM), F32), jnp.zeros((ROW_TILE, V_HEAD), F32)) for _ in range(hp))
        carry = make_step(True)(j, init)
        carry = lax.fori_loop(jnp.where(j == 0, j + 1, nt), nt, make_step(True), carry)
        carry = lax.fori_loop(jnp.where(j == 0, nt, j + 1), nt, make_step(False), carry)
        for h in range(hp):
            dk_ref[h] = carry[h][0] * _LN2
            dv_ref[h] = carry[h][1]

    key_tile = lambda w: pl.BlockSpec((hp, ROW_TILE, w), lambda g, j: (g, j, 0))
    all_cols = lambda w: pl.BlockSpec((hp, w, n_rows), lambda g, j: (g, 0, 0))
    out_shapes = [
        jax.ShapeDtypeStruct((N_HEADS, QK_DIM, n_rows), F32),
        jax.ShapeDtypeStruct((N_HEADS, n_rows, QK_DIM), F32),
        jax.ShapeDtypeStruct((N_HEADS, n_rows, V_HEAD), F32),
    ]
    return pl.pallas_call(
        body, name="attn_bwd", grid=(N_HEADS // hp, nt),
        in_specs=[key_tile(QK_DIM), key_tile(V_HEAD), all_cols(QK_DIM), all_cols(V_HEAD), all_cols(1), all_cols(1)],
        out_specs=[all_cols(QK_DIM), key_tile(QK_DIM), key_tile(V_HEAD)],
        out_shape=out_shapes,
        compiler_params=_params("parallel", "arbitrary"),
    )(k, v, q_t, do_t, lse, delta)


def _bwd_qkv(dq_t, dk, dv, cq, ckv, gq, gkv, wq_t, w_ukv, cos, sin, cos_t, sin_t, n_rows):
    nt = n_rows // ROW_TILE

    def body(dqt_ref, dk_ref, dv_ref, cq_ref, ckv_ref, gq_ref, gkv_ref, wqt_ref, wkv_ref, cos_ref, sin_ref,
             cost_ref, sint_ref, dqraw_ref, dkv_ref, dcq_ref, dckv_ref, dkr_ref, dgq_ref, dgkv_ref):
        i = pl.program_id(0)
        cos_rows, sin_rows = cost_ref[...], sint_ref[...]
        dcqn = jnp.zeros((ROW_TILE, Q_LORA), F32)
        dckvn = jnp.zeros((ROW_TILE, KV_LORA), F32)
        dk_rot = jnp.zeros((ROW_TILE, QK_ROPE), F32)
        for h in range(N_HEADS):
            dq_h, dk_h = dqt_ref[h] * QK_DIM ** -0.5, dk_ref[h]
            dq_raw = jnp.concatenate(
                [dq_h[:QK_NOPE], _rope_rows_t(dq_h[QK_NOPE:], cos_rows, sin_rows)], axis=0).astype(BF16)
            dqraw_ref[h] = dq_raw
            dcqn = dcqn + _dot_tn(dq_raw, wqt_ref[h])
            dkv = jnp.concatenate([dk_h[:, :QK_NOPE], dv_ref[h]], axis=-1).astype(BF16)
            dkv_ref[:, h * KV_HEAD:(h + 1) * KV_HEAD] = dkv
            dckvn = dckvn + _dot_nt(dkv, wkv_ref[h])
            dk_rot = dk_rot + dk_h[:, QK_NOPE:]
        dkr_ref[...] = _rope_t(dk_rot, cos_ref[...], sin_ref[...]).astype(BF16)
        dcq, dgq = _rms_bwd(dcqn, cq_ref[...], gq_ref[...])
        dckv, dgkv = _rms_bwd(dckvn, ckv_ref[...], gkv_ref[...])
        dcq_ref[...] = dcq.astype(BF16)
        dckv_ref[...] = dckv.astype(BF16)
        _accumulate(dgq_ref, i == 0, dgq)
        _accumulate(dgkv_ref, i == 0, dgkv)

    out_shapes = [
        jax.ShapeDtypeStruct((N_HEADS, QK_DIM, n_rows), BF16),
        jax.ShapeDtypeStruct((n_rows, N_HEADS * KV_HEAD), BF16),
        jax.ShapeDtypeStruct((n_rows, Q_LORA), BF16),
        jax.ShapeDtypeStruct((n_rows, KV_LORA), BF16),
        jax.ShapeDtypeStruct((n_rows, QK_ROPE), BF16),
        jax.ShapeDtypeStruct((1, Q_LORA), F32),
        jax.ShapeDtypeStruct((1, KV_LORA), F32),
    ]
    tiles = [dk, dv, cq, ckv]
    whole = [gq, gkv, wq_t, w_ukv]
    return pl.pallas_call(
        body, name="bwd_qkv", grid=(nt,),
        in_specs=[_lane_tile(dq_t.shape)] + [_tile_spec(a.shape) for a in tiles] + [_whole_spec(a.shape) for a in whole]
        + [_tile_spec(cos.shape), _tile_spec(sin.shape), _lane_tile(cos_t.shape), _lane_tile(sin_t.shape)],
        out_specs=[_lane_tile(out_shapes[0].shape)] + [_tile_spec(s.shape) for s in out_shapes[1:5]]
        + [_acc_spec(s.shape) for s in out_shapes[5:]],
        out_shape=out_shapes,
        compiler_params=_params("arbitrary"),
    )(dq_t, *tiles, *whole, cos, sin, cos_t, sin_t)


def _bwd_conv(du1, ag, conv_w, n_rows):
    nt = n_rows // ROW_TILE

    last_tap = CONV_WIDTH - 1

    def body(du1_ref, dnext_ref, ag_ref, w_ref, dag_ref, dw_ref, dext_ref, uext_ref, conv_ref, sums_ref):
        i = pl.program_id(0)

        @pl.when(i == 0)
        def _():
            sums_ref[...] = jnp.zeros_like(sums_ref)

        _to_planes(dext_ref, (), slice(0, ROW_TILE), du1_ref[...])
        _to_planes(dext_ref, (), slice(ROW_TILE, None), jnp.where(i == nt - 1, 0.0, dnext_ref[...]))
        ag_t = ag_ref[...]
        live = _row_ids(i, ROW_TILE) >= DEAD
        sg = _sigmoid(ag_t[:, D_CONV:])
        _to_planes(uext_ref, (), slice(None), jnp.where(live, ag_t[:, :D_CONV] * sg, 0.0))
        for c in range(CONV_PLANES):
            taps = w_ref[:, c * _LANES:(c + 1) * _LANES]
            for p in range(PHASES):
                u = uext_ref[c, _phase(p), :]
                acc = jnp.zeros((PHASE_ROWS, _LANES), F32)
                for k in range(CONV_WIDTH):
                    shifted = dext_ref[c, _phase(p + last_tap - k), :]
                    acc = acc + taps[k:k + 1, :] * shifted
                    sums_ref[c, k] += shifted * u
                conv_ref[c, _phase(p), :] = acc
        du0 = jnp.where(live, _from_planes(conv_ref, (), D_CONV), 0.0)
        da = du0 * sg
        dgate = du0 * ag_t[:, :D_CONV] * sg * (1.0 - sg)
        dag_ref[...] = jnp.concatenate([da, dgate], axis=-1).astype(BF16)

        @pl.when(i == nt - 1)
        def _():
            for c in range(CONV_PLANES):
                for k in range(CONV_WIDTH):
                    dw_ref[k:k + 1, c * _LANES:(c + 1) * _LANES] = jnp.sum(sums_ref[c, k], axis=0, keepdims=True)

    out_shapes = [jax.ShapeDtypeStruct((n_rows, 2 * D_CONV), BF16), jax.ShapeDtypeStruct((CONV_WIDTH, D_CONV), F32)]
    return pl.pallas_call(
        body, name="bwd_conv", grid=(nt,),
        in_specs=[_tile_spec(du1.shape), _halo_after(du1.shape, CONV_HALO, n_rows), _tile_spec(ag.shape),
                  _whole_spec(conv_w.shape)],
        out_specs=[_tile_spec(out_shapes[0].shape), _acc_spec(out_shapes[1].shape)],
        out_shape=out_shapes,
        scratch_shapes=[pltpu.VMEM((CONV_PLANES, ROW_TILE + CONV_HALO, _LANES), F32),
                        pltpu.VMEM((CONV_PLANES, ROW_TILE, _LANES), F32), pltpu.VMEM((CONV_PLANES, ROW_TILE, _LANES), F32),
                        pltpu.VMEM((CONV_PLANES, CONV_WIDTH, PHASE_ROWS, _LANES), F32)],
        compiler_params=_params("arbitrary"),
    )(du1, du1, ag, conv_w)


def _bwd_in(dag, dcq, dckv, dkr, x, meta_pad, dh1, g1, w_in, n_rows):
    nt = n_rows // ROW_TILE

    def body(dag_ref, dcq_ref, dckv_ref, dkr_ref, x_ref, meta_ref, dh1_ref, g_ref, w_ref,
             dz_ref, gx_ref, gmeta_ref, dg1_ref):
        i = pl.program_id(0)
        dz = jnp.concatenate([dag_ref[...], dcq_ref[...], dckv_ref[...], dkr_ref[...]], axis=-1)
        dz_ref[...] = dz
        h0 = jnp.where(i == 0, meta_ref[...], x_ref[...])
        dx, dg1 = _rms_bwd(_dot(dz, w_ref[...]), h0, g_ref[...])
        dh0 = dh1_ref[...] + dx
        gx_ref[...] = dh0

        @pl.when(i == 0)
        def _():
            gmeta_ref[...] = dh0

        _accumulate(dg1_ref, i == 0, dg1)

    out_shapes = [
        jax.ShapeDtypeStruct((n_rows, D_IN), BF16),
        jax.ShapeDtypeStruct((n_rows - ROW_TILE, D_MODEL), F32),
        jax.ShapeDtypeStruct((ROW_TILE, D_MODEL), F32),
        jax.ShapeDtypeStruct((1, D_MODEL), F32),
    ]
    tiles = [dag, dcq, dckv, dkr]
    return pl.pallas_call(
        body, name="bwd_in", grid=(nt,),
        in_specs=[_tile_spec(a.shape) for a in tiles]
        + [_real_spec(D_MODEL), _whole_spec(meta_pad.shape), _tile_spec(dh1.shape), _whole_spec(g1.shape), _whole_spec(w_in.shape)],
        out_specs=[_tile_spec(out_shapes[0].shape), _real_spec(D_MODEL), _acc_spec(out_shapes[2].shape), _acc_spec(out_shapes[3].shape)],
        out_shape=out_shapes,
        compiler_params=_params("arbitrary"),
    )(*tiles, x, meta_pad, dh1, g1, w_in)


def _contraction_tile(n_rows):
    return next(t for t in range(n_rows // 2 // _LANES * _LANES, 0, -_LANES) if n_rows % t == 0)


def _weight_grad(a, b, name, a_transposed=False):
    groups = max(a.shape[0] if a.ndim == 3 else 1, b.shape[0] if b.ndim == 3 else 1)
    n_rows, n = b.shape[-2], b.shape[-1]
    m = a.shape[-2] if a_transposed else a.shape[-1]
    kt = _contraction_tile(n_rows)
    steps = n_rows // kt

    def body(a_ref, b_ref, out_ref, acc_ref):
        i = pl.program_id(1)
        a_t, b_t = a_ref[...].astype(BF16), b_ref[...].astype(BF16)
        part = _dot(a_t, b_t) if a_transposed else _dot_tn(a_t, b_t)
        _accumulate(acc_ref, i == 0, part)

        @pl.when(i == steps - 1)
        def _():
            out_ref[...] = acc_ref[...].astype(out_ref.dtype)

    def spec(arr, rows_last):
        block = (arr.shape[-2], kt) if rows_last else (kt, arr.shape[-1])
        at = (lambda i: (0, i)) if rows_last else (lambda i: (i, 0))
        if arr.ndim == 3:
            return pl.BlockSpec((None,) + block, lambda g, i: (g,) + at(i))
        return pl.BlockSpec(block, lambda g, i: at(i))

    return pl.pallas_call(
        body, name=name, grid=(groups, steps),
        in_specs=[spec(a, a_transposed), spec(b, False)],
        out_specs=pl.BlockSpec((None, m, n), lambda g, i: (g, 0, 0)),
        out_shape=jax.ShapeDtypeStruct((groups, m, n), BF16),
        scratch_shapes=[pltpu.VMEM((m, n), F32)],
        compiler_params=_params("parallel", "arbitrary"),
    )(a, b)


def _my_index():
    return 4 * lax.axis_index("x") + 2 * lax.axis_index("y") + lax.axis_index("c")


def _peer(k):
    flip = lambda v, bit: 1 - v if bit else v
    px = flip(lax.axis_index("x"), k & 4)
    py = flip(lax.axis_index("y"), k & 2)
    pc = flip(lax.axis_index("c"), k & 1)
    return (px, py, pc), 4 * px + 2 * py + pc


def _all_gather(shards, dtypes):
    n = len(shards)

    def body(*refs):
        ins, outs, stages = refs[:n], refs[n:2 * n], refs[2 * n:3 * n]
        send_sems, recv_sems, local_sems = refs[3 * n:]
        me = _my_index()
        for a in range(n):
            stages[a][...] = ins[a][...].astype(stages[a].dtype)
        local = [pltpu.make_async_copy(stages[a], outs[a].at[me], local_sems.at[a]) for a in range(n)]
        for cp in local:
            cp.start()

        def copy(a, k, slot):
            peer, _ = _peer(k)
            return pltpu.make_async_remote_copy(
                src_ref=stages[a], dst_ref=outs[a].at[slot], send_sem=send_sems.at[a, k - 1],
                recv_sem=recv_sems.at[a, k - 1], device_id=peer, device_id_type=MESH)

        for k in range(1, N_DEV):
            for a in range(n):
                copy(a, k, me).start()
        for k in range(1, N_DEV):
            for a in range(n):
                copy(a, k, _peer(k)[1]).wait()
        for cp in local:
            cp.wait()

    return pl.pallas_call(
        body, name="gather_weights",
        in_specs=[pl.BlockSpec(memory_space=pltpu.VMEM)] * n,
        out_specs=[pl.BlockSpec(memory_space=pl.ANY)] * n,
        out_shape=[jax.ShapeDtypeStruct((N_DEV,) + s.shape, dt) for s, dt in zip(shards, dtypes)],
        scratch_shapes=[pltpu.VMEM(s.shape, dt) for s, dt in zip(shards, dtypes)]
        + [pltpu.SemaphoreType.DMA((n, N_DEV - 1)), pltpu.SemaphoreType.DMA((n, N_DEV - 1)), pltpu.SemaphoreType.DMA((n,))],
        compiler_params=pltpu.CompilerParams(vmem_limit_bytes=VMEM_LIMIT),
    )(*shards)


def _exchange(parts, whole):
    n = len(parts)

    def body(*refs):
        ins, outs = refs[:n], refs[n:2 * n]
        send_sems, recv_sems, local_sems = refs[2 * n:]
        me = _my_index()

        def src(a, slab):
            return ins[a] if whole[a] else ins[a].at[slab]

        local = [pltpu.make_async_copy(src(a, me), outs[a].at[me], local_sems.at[a]) for a in range(n)]
        for cp in local:
            cp.start()

        def copy(a, k, slab, slot):
            peer, _ = _peer(k)
            return pltpu.make_async_remote_copy(
                src_ref=src(a, slab), dst_ref=outs[a].at[slot], send_sem=send_sems.at[a, k - 1],
                recv_sem=recv_sems.at[a, k - 1], device_id=peer, device_id_type=MESH)

        for k in range(1, N_DEV):
            for a in range(n):
                copy(a, k, _peer(k)[1], me).start()
        for k in range(1, N_DEV):
            for a in range(n):
                copy(a, k, _peer(k)[1], _peer(k)[1]).wait()
        for cp in local:
            cp.wait()

    return pl.pallas_call(
        body, name="exchange_grads",
        in_specs=[pl.BlockSpec(memory_space=pl.ANY)] * n,
        out_specs=[pl.BlockSpec(memory_space=pl.ANY)] * n,
        out_shape=[jax.ShapeDtypeStruct(((N_DEV,) + p.shape) if w else p.shape, p.dtype) for p, w in zip(parts, whole)],
        scratch_shapes=[pltpu.SemaphoreType.DMA((n, N_DEV - 1)), pltpu.SemaphoreType.DMA((n, N_DEV - 1)),
                        pltpu.SemaphoreType.DMA((n,))],
    )(*parts)


def _sequencer_exchange(parts, whole, name, collective_id):
    n = len(parts)
    srcs = [jax.new_ref(p, memory_space=pltpu.MemorySpace.HBM) for p in parts]
    lands = [jax.empty_ref(jax.ShapeDtypeStruct(((N_DEV,) + p.shape) if w else p.shape, p.dtype),
                           memory_space=pltpu.MemorySpace.HBM) for p, w in zip(parts, whole)]

    @pl.kernel(mesh=plsc.ScalarSubcoreMesh(axis_name="sequencer", num_cores=1), name=name,
               scratch_types=(pltpu.SemaphoreType.DMA((n, N_DEV - 1)), pltpu.SemaphoreType.DMA((n, N_DEV - 1)),
                              pltpu.SemaphoreType.DMA((n,))),
               compiler_params=pltpu.CompilerParams(collective_id=collective_id))
    def launch(send_sems, recv_sems, local_sems):
        barrier = pltpu.get_barrier_semaphore()
        for k in range(1, N_DEV):
            pl.semaphore_signal(barrier, inc=1, device_id=_peer(k)[0], device_id_type=MESH)
        pl.semaphore_wait(barrier, N_DEV - 1)
        me = _my_index()

        def src(a, slab):
            return srcs[a] if whole[a] else srcs[a].at[slab]

        local = [pltpu.make_async_copy(src(a, me), lands[a].at[me], local_sems.at[a]) for a in range(n)]
        for cp in local:
            cp.start()

        def copy(a, k, slab, slot):
            return pltpu.make_async_remote_copy(
                src_ref=src(a, slab), dst_ref=lands[a].at[slot], send_sem=send_sems.at[a, k - 1],
                recv_sem=recv_sems.at[a, k - 1], device_id=_peer(k)[0], device_id_type=MESH)

        for k in range(1, N_DEV):
            for a in range(n):
                copy(a, k, _peer(k)[1], me).start()
        for k in range(1, N_DEV):
            for a in range(n):
                copy(a, k, _peer(k)[1], _peer(k)[1]).wait()
        for cp in local:
            cp.wait()

    launch()
    return [land[...] for land in lands]


def _row_block(rows):
    if rows <= ROW_TILE:
        return rows
    return next(rb for rb in range(ROW_TILE, 0, -16) if rows % rb == 0)


def _adamw(landing, w, m, v, name):
    rows, cols = w.shape
    rb = _row_block(rows)

    def body(l_ref, w_ref, m_ref, v_ref, g_ref, d_ref, m2_ref, v2_ref):
        g = l_ref[0].astype(F32)
        for p in range(1, N_DEV):
            g = g + l_ref[p].astype(F32)
        g_ref[...] = g
        d_ref[...], m2_ref[...], v2_ref[...] = _adamw_step(g, w_ref[...], m_ref[...], v_ref[...])

    flat = pl.BlockSpec((rb, cols), lambda i: (i, 0))
    return pl.pallas_call(
        body, name=name, grid=(rows // rb,),
        in_specs=[pl.BlockSpec((N_DEV, rb, cols), lambda i: (0, i, 0)), flat, flat, flat],
        out_specs=[flat] * 4,
        out_shape=[jax.ShapeDtypeStruct((rows, cols), F32)] * 4,
        compiler_params=_params("parallel"),
    )(landing, w, m, v)


def _adamw_step(g, w, m, v):
    m2 = ADAM_B1 * m + (1.0 - ADAM_B1) * g
    v2 = ADAM_B2 * v + (1.0 - ADAM_B2) * (g * g)
    m_hat = m2 / (1.0 - ADAM_B1 ** ADAM_STEP)
    v_hat = v2 / (1.0 - ADAM_B2 ** ADAM_STEP)
    return -ADAM_LR * (m_hat / (jnp.sqrt(v_hat) + ADAM_EPS) + ADAM_WD * w), m2, v2


_REPLICATED = (
    ("mix_norm_g", D_MODEL), ("q_norm_g", Q_LORA), ("kv_norm_g", KV_LORA), ("conv_b", D_CONV), ("conv_ln_g", D_CONV),
    ("conv_ln_b", D_CONV), ("conv_out_g", D_CONV), ("attn_out_g", D_CONV), ("ffn_norm_g", D_MODEL),
    ("ffn_conv_b", D_UP), ("final_norm_g", D_MODEL),
)
_REPLICATED_WIDTH = sum(size for _, size in _REPLICATED) + _LANES

_WEIGHT_ORDER = (
    "meta_tokens", "mix_norm_g", "w_in", "q_norm_g", "w_uq", "kv_norm_g", "w_ukv", "conv_w", "conv_b", "conv_ln_g",
    "conv_ln_b", "conv_out_g", "attn_out_g", "w_out", "ffn_norm_g", "w_ffn_up", "ffn_conv_w", "ffn_conv_b",
    "w_ffn_down", "final_norm_g",
)


def _pack_replicated(grads, loss):
    rows = [grads[name].reshape(1, size) for name, size in _REPLICATED]
    return jnp.concatenate(rows + [jnp.broadcast_to(loss.reshape(1, 1), (1, _LANES))], axis=-1)


def _adamw_replicated(landing, weights, moments_m, moments_v):
    n = len(_REPLICATED)

    def body(*refs):
        l_ref, ins, outs = refs[0], refs[1:1 + 3 * n], refs[1 + 3 * n:]
        total = l_ref[0]
        for p in range(1, N_DEV):
            total = total + l_ref[p]
        at = 0
        for a, (_, size) in enumerate(_REPLICATED):
            g = total[:, at:at + size]
            w_ref, m_ref, v_ref = ins[3 * a:3 * a + 3]
            g_ref, d_ref, m2_ref, v2_ref = outs[4 * a:4 * a + 4]
            g_ref[...] = g
            d_ref[...], m2_ref[...], v2_ref[...] = _adamw_step(g, w_ref[...], m_ref[...], v_ref[...])
            at += size
        outs[-1][...] = total[:, at:at + _LANES]

    operands, out_shapes = [], []
    for name, size in _REPLICATED:
        operands += [weights[name].reshape(1, size), moments_m[name].reshape(1, size), moments_v[name].reshape(1, size)]
        out_shapes += [jax.ShapeDtypeStruct((1, size), F32)] * 4
    out_shapes.append(jax.ShapeDtypeStruct((1, _LANES), F32))
    outs = pl.pallas_call(body, name="adamw_replicated", out_shape=out_shapes)(landing, *operands)
    return outs[-1][0, 0], {name: outs[4 * a:4 * a + 4] for a, (name, _) in enumerate(_REPLICATED)}


def _pad_rows(a, rows):
    return jnp.pad(a, ((0, rows - a.shape[0]), (0, 0)))


def _slabs(a):
    r, c = a.shape
    return a.reshape(r, N_DEV, c // N_DEV).transpose(1, 0, 2)


def _unslab(a):
    g, r, c = a.shape
    return a.transpose(1, 0, 2).reshape(r, g * c)


def _local_step(x, target, w, n_rows, ffn_weights, send_early_grads):
    cos_t, sin_t = _rope_tables(n_rows)
    cos, sin = cos_t.T, sin_t.T
    meta_pad, g1, gf = w["meta_pad"], w["mix_norm_g"], w["final_norm_g"]
    gq, gkv, gb_col = w["q_norm_g"], w["kv_norm_g"], w["attn_out_g"].reshape(D_ATTN, 1)
    nb, ag, cq, ckv, kr = _fwd_in(x, meta_pad, g1, w["w_in"], n_rows)
    mix_a, u1 = _fwd_conv(ag, w["conv_w"], w["conv_b"], w["conv_ln_g"], w["conv_ln_b"], w["conv_out_g"], n_rows)
    q_t, k, v, v_t, cqn, ckvn = _fwd_qkv(cq, ckv, kr, gq, gkv, w["wq_t"], w["w_ukv"], w["wv_t"], cos, sin, cos_t, sin_t, n_rows)
    o_t, lse = _attn_fwd(q_t, k, v_t, n_rows)
    w_out, w_up, w_down = ffn_weights()
    mix_bt, h1 = _fwd_out(x, meta_pad, mix_a, o_t, gb_col, w_out, n_rows)
    n2, up0, act, da, db, dh2, loss, dgf = _fwd_ffn(
        h1, target, w["ffn_norm_g"], w_up, w["fw"], w["fb"], w_down, gf, n_rows)

    dup, dfb = _bwd_ffn_act(dh2, da, db, w_down, n_rows)
    dup0, dh1, dfw, dg2 = _bwd_ffn_up(dup, up0, h1, dh2, w["ffn_norm_g"], w_up, w["fw"], n_rows)
    grad_w_out = jnp.concatenate([_weight_grad(mix_a, dh1, "grad_w_out_conv")[0],
                                  _weight_grad(mix_bt, dh1, "grad_w_out_attn", a_transposed=True)[0]], axis=0)
    send_early_grads(_weight_grad(dup0, n2, "grad_w_ffn_up"),
                     _weight_grad(act, dh2, "grad_w_ffn_down").reshape(N_DEV, D_FF // N_DEV, D_MODEL),
                     grad_w_out.reshape(N_DEV, D_MODEL // N_DEV, D_MODEL))
    do_t, delta, du1, dgb, dga, dlg, dlb, dcb = _bwd_out(
        dh1, o_t, u1, w_out, gb_col, w["conv_ln_g"], w["conv_ln_b"], w["conv_out_g"], n_rows)
    dq_t, dk, dv = _attn_bwd(q_t, k, v, do_t, lse, delta, n_rows)
    dqraw_t, dkv, dcq, dckv, dkr, dgq, dgkv = _bwd_qkv(
        dq_t, dk, dv, cq, ckv, gq, gkv, w["wq_t"], w["w_ukv"], cos, sin, cos_t, sin_t, n_rows)
    dag, dcw = _bwd_conv(du1, ag, w["conv_w"], n_rows)
    dz, gx, gmeta, dg1 = _bwd_in(dag, dcq, dckv, dkr, x, meta_pad, dh1, g1, w["w_in"], n_rows)

    sharded = {
        "w_in": _weight_grad(dz, nb, "grad_w_in")[0].reshape(N_DEV, D_IN // N_DEV, D_MODEL),
        "w_uq": _weight_grad(dqraw_t.reshape(N_HEADS * QK_DIM, n_rows), cqn, "grad_w_uq", a_transposed=True)[0].reshape(
            N_HEADS, QK_DIM, Q_LORA),
        "w_ukv": _slabs(_weight_grad(ckvn, dkv, "grad_w_ukv")[0]),
        "conv_w": _slabs(dcw),
        "ffn_conv_w": dfw[:, :, :UP_SLAB],
        "meta_tokens": _slabs(gmeta[DEAD:]),
    }
    replicated = {
        "mix_norm_g": dg1, "q_norm_g": dgq, "kv_norm_g": dgkv, "conv_b": dcb, "conv_ln_g": dlg, "conv_ln_b": dlb,
        "conv_out_g": dga, "attn_out_g": dgb, "ffn_norm_g": dg2, "ffn_conv_b": dfb, "final_norm_g": dgf,
    }
    return loss[0, 0], gx, sharded, replicated


_SHARDED = (
    ("w_in", None, BF16), ("w_uq", None, BF16), ("w_ukv", None, BF16), ("w_out", None, BF16), ("w_ffn_up", None, BF16),
    ("w_ffn_down", None, BF16), ("conv_w", 32, F32), ("ffn_conv_w", 8, F32), ("meta_tokens", None, F32),
)
GATHER_LATE_ID = 3
EXCHANGE_EARLY_ID = 4
_LATE_WEIGHTS = ("w_out", "w_ffn_up", "w_ffn_down")
_COLUMN_SHARDS = ("w_in", "w_uq", "w_ffn_up")
_EARLY_GRADS = ("w_ffn_up", "w_ffn_down", "w_out")


def kernel(x, meta_tokens, mix_norm_g, w_in, q_norm_g, w_uq, kv_norm_g, w_ukv, conv_w, conv_b, conv_ln_g, conv_ln_b, conv_out_g, attn_out_g, w_out, ffn_norm_g, w_ffn_up, ffn_conv_w, ffn_conv_b, w_ffn_down, final_norm_g, loss_target, m_meta_tokens, m_mix_norm_g, m_w_in, m_q_norm_g, m_w_uq, m_kv_norm_g, m_w_ukv, m_conv_w, m_conv_b, m_conv_ln_g, m_conv_ln_b, m_conv_out_g, m_attn_out_g, m_w_out, m_ffn_norm_g, m_w_ffn_up, m_ffn_conv_w, m_ffn_conv_b, m_w_ffn_down, m_final_norm_g, v_meta_tokens, v_mix_norm_g, v_w_in, v_q_norm_g, v_w_uq, v_kv_norm_g, v_w_ukv, v_conv_w, v_conv_b, v_conv_ln_g, v_conv_ln_b, v_conv_out_g, v_attn_out_g, v_w_out, v_ffn_norm_g, v_w_ffn_up, v_ffn_conv_w, v_ffn_conv_b, v_w_ffn_down, v_final_norm_g):
    given = dict(locals())
    weights = {name: given[name] for name in _WEIGHT_ORDER}
    moments_m = {name: given["m_" + name] for name in _WEIGHT_ORDER}
    moments_v = {name: given["v_" + name] for name in _WEIGHT_ORDER}
    seq = x.shape[1]
    n_rows = ROW_TILE + seq

    def shard2d(name, a):
        a = a.reshape(a.shape[-2], a.shape[-1])
        return a.T if name in _COLUMN_SHARDS else a

    early = [entry for entry in _SHARDED if entry[0] not in _LATE_WEIGHTS]
    shards = []
    for name, pad_to, _ in early:
        s = shard2d(name, weights[name])
        shards.append(s if pad_to is None else _pad_rows(s, pad_to))
    gathered = dict(zip([name for name, _, _ in early], _all_gather(shards, [dt for _, _, dt in early])))
    behind = gathered["meta_tokens"][0, 0, 0] * 0.0
    late_parts = [(shard2d(name, weights[name]) + behind).astype(BF16) for name in _LATE_WEIGHTS]
    late = _sequencer_exchange(late_parts, [True] * len(late_parts), "gather_late", GATHER_LATE_ID)
    meta_full = _unslab(gathered["meta_tokens"])
    full = {
        "meta_pad": jnp.concatenate([jnp.zeros((DEAD, D_MODEL), F32), meta_full], axis=0),
        "w_in": gathered["w_in"].reshape(D_IN, D_MODEL),
        "wq_t": gathered["w_uq"],
        "w_ukv": gathered["w_ukv"],
        "wv_t": gathered["w_ukv"][:, :, QK_NOPE:].transpose(0, 2, 1),
        "conv_w": _unslab(gathered["conv_w"][:, :CONV_WIDTH]),
        "fw": jnp.pad(gathered["ffn_conv_w"][:, :FFN_CONV_WIDTH], ((0, 0), (0, 0), (0, UP_PAD - UP_SLAB))),
        "fb": jnp.pad(ffn_conv_b.reshape(N_DEV, 1, UP_SLAB), ((0, 0), (0, 0), (0, UP_PAD - UP_SLAB))),
        "final_norm_g": final_norm_g.reshape(1, D_MODEL),
    }
    for name in ("mix_norm_g", "q_norm_g", "kv_norm_g", "conv_b", "conv_ln_g", "conv_ln_b", "conv_out_g", "attn_out_g",
                 "ffn_norm_g"):
        full[name] = weights[name]

    def ffn_weights():
        w_out_all, w_up_all, w_down_all = late
        return (w_out_all.reshape(D_MODEL, D_MODEL), w_up_all, w_down_all.reshape(N_ACT_SLAB, UP_SLAB, D_MODEL))

    early_landed = []

    def send_early_grads(*grads):
        early_landed.extend(_sequencer_exchange(list(grads), [False] * len(grads), "exchange_early", EXCHANGE_EARLY_ID))

    loss, gx, sharded, replicated = _local_step(x[0], loss_target[0], full, n_rows, ffn_weights, send_early_grads)

    rest = [entry for entry in _SHARDED if entry[0] not in _EARLY_GRADS]
    parts, whole = [], []
    for name, pad_to, dt in rest:
        p = sharded[name].astype(dt)
        parts.append(p if pad_to is None else jnp.pad(p, ((0, 0), (0, pad_to - p.shape[1]), (0, 0))))
        whole.append(False)
    parts.append(_pack_replicated(replicated, loss))
    whole.append(True)
    landed = _exchange(parts, whole)
    landing = dict(zip([name for name, _, _ in rest], landed[:-1]))
    landing.update(zip(_EARLY_GRADS, early_landed))

    grad, delta, new_m, new_v = {}, {}, {}, {}
    for name, pad_to, _ in _SHARDED:
        land = landing[name]
        ws, ms, vs = (shard2d(name, a[name]) for a in (weights, moments_m, moments_v))
        rows = ws.shape[0]
        if pad_to is not None:
            ws, ms, vs = _pad_rows(ws, pad_to), _pad_rows(ms, pad_to), _pad_rows(vs, pad_to)
        outs = _adamw(land, ws, ms, vs, "adamw_" + name)
        shape = weights[name].shape
        grad[name], delta[name], new_m[name], new_v[name] = (
            (o.T if name in _COLUMN_SHARDS else o[:rows]).reshape(shape) for o in outs)
    loss, updates = _adamw_replicated(landed[-1], weights, moments_m, moments_v)
    for name, outs in updates.items():
        grad[name], delta[name], new_m[name], new_v[name] = (o.reshape(weights[name].shape) for o in outs)

    return (loss, gx[None], *[grad[n] for n in _WEIGHT_ORDER], *[delta[n] for n in _WEIGHT_ORDER],
            *[new_m[n] for n in _WEIGHT_ORDER], *[new_v[n] for n in _WEIGHT_ORDER])
```

```python
import functools

import jax
import jax.numpy as jnp
from jax import lax
from jax.experimental import pallas as pl
from jax.experimental.pallas import tpu as pltpu
from jax.experimental.pallas import tpu_sc as plsc

F32 = jnp.float32
BF16 = jnp.bfloat16

N_DEV = 8
D_MODEL = 1024
CHUNK = 64
CHUNK_SHIFT = 6
N_META = 16
D_CONV = 512
CONV_WIDTH = 31
N_HEADS = 8
QK_NOPE = 64
QK_ROPE = 32
QK_DIM = QK_NOPE + QK_ROPE
V_HEAD = 64
KV_HEAD = QK_NOPE + V_HEAD
D_ATTN = N_HEADS * V_HEAD
Q_LORA = 384
KV_LORA = 256
ROPE_THETA = 10000.0
D_IN = 2 * D_CONV + Q_LORA + KV_LORA + QK_ROPE
D_FF = 2816
D_UP = 2 * D_FF
FFN_CONV_WIDTH = 3
UP_SLAB = D_UP // N_DEV
N_ACT_SLAB = D_FF // UP_SLAB
EPS = 1e-6
NEG = -1e30
_LN2 = 0.6931471805599453
QK_LOGIT_SCALE = QK_DIM ** -0.5 / _LN2
ADAM_LR = 0.001
ADAM_B1 = 0.9
ADAM_B2 = 0.999
ADAM_EPS = 1e-08
ADAM_WD = 0.01
ADAM_STEP = 10

ROW_TILE = 256
DEAD = ROW_TILE - N_META
CONV_HALO = 32
FFN_HALO = 16
VMEM_LIMIT = 56 * 1024 * 1024
_LANES = 128

MESH = pl.DeviceIdType.MESH


def _dot(a, b):
    return jnp.dot(a, b, preferred_element_type=F32)


def _dot_nt(a, b):
    return lax.dot_general(a, b, (((1,), (1,)), ((), ())), preferred_element_type=F32)


def _dot_tn(a, b):
    return lax.dot_general(a, b, (((0,), (0,)), ((), ())), preferred_element_type=F32)


def _sigmoid(x):
    return 1.0 / (1.0 + jnp.exp2(x * (-1.0 / _LN2)))


def _rms_fwd(x, g):
    r = lax.rsqrt(jnp.mean(x * x, axis=-1, keepdims=True) + EPS)
    return x * r * g


def _rms_bwd(dy, x, g):
    r = lax.rsqrt(jnp.mean(x * x, axis=-1, keepdims=True) + EPS)
    w = dy * g
    dx = r * w - x * (r * r * r) * jnp.mean(w * x, axis=-1, keepdims=True)
    return dx, jnp.sum(dy * x * r, axis=0, keepdims=True)


def _rope(x, cos, sin):
    half = QK_ROPE // 2
    x1, x2 = x[:, :half], x[:, half:]
    return jnp.concatenate([x1 * cos - x2 * sin, x2 * cos + x1 * sin], axis=-1)


def _rope_t(dy, cos, sin):
    half = QK_ROPE // 2
    d1, d2 = dy[:, :half], dy[:, half:]
    return jnp.concatenate([d1 * cos + d2 * sin, d2 * cos - d1 * sin], axis=-1)


def _row_ids(i, rows):
    return i * rows + lax.broadcasted_iota(jnp.int32, (rows, 1), 0)


def _accumulate(ref, first, value):
    @pl.when(first)
    def _():
        ref[...] = value

    @pl.when(jnp.logical_not(first))
    def _():
        ref[...] += value


def _tile_spec(shape):
    nd = len(shape)
    if nd == 2:
        return pl.BlockSpec((ROW_TILE, shape[1]), lambda i: (i, 0))
    return pl.BlockSpec((shape[0], ROW_TILE, shape[2]), lambda i: (0, i, 0))


def _whole_spec(shape):
    nd = len(shape)
    return pl.BlockSpec(tuple(shape), lambda i: (0,) * nd, pipeline_mode=pl.Buffered(1))


def _acc_spec(shape):
    nd = len(shape)
    return pl.BlockSpec(tuple(shape), lambda i: (0,) * nd)


def _real_spec(width):
    return pl.BlockSpec((ROW_TILE, width), lambda i: (jnp.maximum(i - 1, 0), 0))


def _params(*semantics):
    return pltpu.CompilerParams(dimension_semantics=semantics, vmem_limit_bytes=VMEM_LIMIT)


def _fwd_in(x, meta_pad, g1, w_in, n_rows):
    nt = n_rows // ROW_TILE

    def body(x_ref, meta_ref, g_ref, w_ref, nb_ref, ag_ref, cq_ref, ckv_ref, kr_ref):
        i = pl.program_id(0)
        h0 = jnp.where(i == 0, meta_ref[...], x_ref[...])
        nb = _rms_fwd(h0, g_ref[...]).astype(BF16)
        nb_ref[...] = nb
        z = _dot_nt(nb, w_ref[...])
        ag_ref[...] = z[:, :2 * D_CONV]
        cq_ref[...] = z[:, 2 * D_CONV:2 * D_CONV + Q_LORA]
        ckv_ref[...] = z[:, 2 * D_CONV + Q_LORA:2 * D_CONV + Q_LORA + KV_LORA]
        kr_ref[...] = z[:, 2 * D_CONV + Q_LORA + KV_LORA:]

    out_shapes = [
        jax.ShapeDtypeStruct((n_rows, D_MODEL), BF16),
        jax.ShapeDtypeStruct((n_rows, 2 * D_CONV), F32),
        jax.ShapeDtypeStruct((n_rows, Q_LORA), F32),
        jax.ShapeDtypeStruct((n_rows, KV_LORA), F32),
        jax.ShapeDtypeStruct((n_rows, QK_ROPE), F32),
    ]
    return pl.pallas_call(
        body, name="fwd_in", grid=(nt,),
        in_specs=[_real_spec(D_MODEL), _whole_spec(meta_pad.shape), _whole_spec(g1.shape), _whole_spec(w_in.shape)],
        out_specs=[_tile_spec(s.shape) for s in out_shapes],
        out_shape=out_shapes,
        compiler_params=_params("parallel"),
    )(x, meta_pad, g1, w_in)


def _conv_chain(u1, ln_g, ln_b):
    mu = jnp.mean(u1, axis=-1, keepdims=True)
    xc = u1 - mu
    rstd = lax.rsqrt(jnp.mean(xc * xc, axis=-1, keepdims=True) + EPS)
    xh = xc * rstd
    u2 = xh * ln_g + ln_b
    return xh, u2, u2 * _sigmoid(u2), rstd


def _fwd_conv(ag, conv_w, conv_b, ln_g, ln_b, out_g, n_rows):
    nt = n_rows // ROW_TILE

    def body(ag_ref, w_ref, b_ref, lg_ref, lb_ref, og_ref, mix_ref, u1_ref, ext_ref, conv_ref):
        i = pl.program_id(0)

        @pl.when(i == 0)
        def _():
            ext_ref[:, 0:CONV_HALO, :] = jnp.zeros((CONV_PLANES, CONV_HALO, _LANES), F32)

        ag_t = ag_ref[...]
        live = _row_ids(i, ROW_TILE) >= DEAD
        u0 = jnp.where(live, ag_t[:, :D_CONV] * _sigmoid(ag_t[:, D_CONV:]), 0.0)
        _to_planes(ext_ref, (), slice(CONV_HALO, None), u0)
        first = CONV_HALO - (CONV_WIDTH - 1)
        for c in range(CONV_PLANES):
            taps = w_ref[:, c * _LANES:(c + 1) * _LANES]
            for p in range(PHASES):
                acc = jnp.zeros((PHASE_ROWS, _LANES), F32)
                for k in range(CONV_WIDTH):
                    acc = acc + taps[k:k + 1, :] * ext_ref[c, _phase(first + k + p), :]
                conv_ref[c, _phase(p), :] = acc
        ext_ref[:, 0:CONV_HALO, :] = ext_ref[:, ROW_TILE:ROW_TILE + CONV_HALO, :]
        u1 = _from_planes(conv_ref, (), D_CONV) + b_ref[...]
        u1_ref[...] = u1
        _, _, u3, _ = _conv_chain(u1, lg_ref[...], lb_ref[...])
        mix_ref[...] = _rms_fwd(u3, og_ref[...]).astype(BF16)

    out_shapes = [jax.ShapeDtypeStruct((n_rows, D_CONV), BF16), jax.ShapeDtypeStruct((n_rows, D_CONV), F32)]
    small = [conv_w, conv_b, ln_g, ln_b, out_g]
    return pl.pallas_call(
        body, name="fwd_conv", grid=(nt,),
        in_specs=[_tile_spec(ag.shape)] + [_whole_spec(a.shape) for a in small],
        out_specs=[_tile_spec(s.shape) for s in out_shapes],
        out_shape=out_shapes,
        scratch_shapes=[pltpu.VMEM((CONV_PLANES, ROW_TILE + CONV_HALO, _LANES), F32),
                        pltpu.VMEM((CONV_PLANES, ROW_TILE, _LANES), F32)],
        compiler_params=_params("arbitrary"),
    )(ag, *small)


def _lane_tile(shape):
    if len(shape) == 2:
        return pl.BlockSpec((shape[0], ROW_TILE), lambda i: (0, i))
    return pl.BlockSpec((shape[0], shape[1], ROW_TILE), lambda i: (0, 0, i))


def _rope_rows(x, cos, sin):
    half = QK_ROPE // 2
    x1, x2 = x[:half], x[half:]
    return jnp.concatenate([x1 * cos - x2 * sin, x2 * cos + x1 * sin], axis=0)


def _rope_rows_t(dy, cos, sin):
    half = QK_ROPE // 2
    d1, d2 = dy[:half], dy[half:]
    return jnp.concatenate([d1 * cos + d2 * sin, d2 * cos - d1 * sin], axis=0)


def _fwd_qkv(cq, ckv, kr, gq, gkv, wq_t, w_ukv, wv_t, cos, sin, cos_t, sin_t, n_rows):
    nt = n_rows // ROW_TILE

    def body(cq_ref, ckv_ref, kr_ref, gq_ref, gkv_ref, wqt_ref, wkv_ref, wvt_ref, cos_ref, sin_ref, cost_ref, sint_ref,
             qt_ref, k_ref, v_ref, vt_ref, cqn_ref, ckvn_ref):
        cqn = _rms_fwd(cq_ref[...], gq_ref[...]).astype(BF16)
        ckvn = _rms_fwd(ckv_ref[...], gkv_ref[...]).astype(BF16)
        cqn_ref[...] = cqn
        ckvn_ref[...] = ckvn
        k_rot = _rope(kr_ref[...], cos_ref[...], sin_ref[...])
        cos_rows, sin_rows = cost_ref[...], sint_ref[...]
        for h in range(N_HEADS):
            q_raw = _dot_nt(wqt_ref[h], cqn)
            q_h = jnp.concatenate([q_raw[:QK_NOPE], _rope_rows(q_raw[QK_NOPE:], cos_rows, sin_rows)], axis=0)
            qt_ref[h] = (q_h * QK_LOGIT_SCALE).astype(BF16)
            kv = _dot(ckvn, wkv_ref[h])
            k_ref[h] = jnp.concatenate([kv[:, :QK_NOPE], k_rot], axis=-1).astype(BF16)
            v_ref[h] = kv[:, QK_NOPE:].astype(BF16)
            vt_ref[h] = _dot_nt(wvt_ref[h], ckvn).astype(BF16)

    out_shapes = [
        jax.ShapeDtypeStruct((N_HEADS, QK_DIM, n_rows), BF16),
        jax.ShapeDtypeStruct((N_HEADS, n_rows, QK_DIM), BF16),
        jax.ShapeDtypeStruct((N_HEADS, n_rows, V_HEAD), BF16),
        jax.ShapeDtypeStruct((N_HEADS, V_HEAD, n_rows), BF16),
        jax.ShapeDtypeStruct((n_rows, Q_LORA), BF16),
        jax.ShapeDtypeStruct((n_rows, KV_LORA), BF16),
    ]
    tiles = [cq, ckv, kr]
    whole = [gq, gkv, wq_t, w_ukv, wv_t]
    out_specs = [_lane_tile(out_shapes[0].shape), _tile_spec(out_shapes[1].shape), _tile_spec(out_shapes[2].shape),
                 _lane_tile(out_shapes[3].shape), _tile_spec(out_shapes[4].shape), _tile_spec(out_shapes[5].shape)]
    return pl.pallas_call(
        body, name="fwd_qkv", grid=(nt,),
        in_specs=[_tile_spec(a.shape) for a in tiles] + [_whole_spec(a.shape) for a in whole]
        + [_tile_spec(cos.shape), _tile_spec(sin.shape), _lane_tile(cos_t.shape), _lane_tile(sin_t.shape)],
        out_specs=out_specs,
        out_shape=out_shapes,
        compiler_params=_params("parallel"),
    )(*tiles, *whole, cos, sin, cos_t, sin_t)


def _chunk_of(rows):
    return jnp.where(rows >= ROW_TILE, lax.shift_right_arithmetic(rows - ROW_TILE, CHUNK_SHIFT) + 1, 0)


def _visible(i, j):
    k_rows = j * ROW_TILE + lax.broadcasted_iota(jnp.int32, (ROW_TILE, 1), 0)
    q_rows = i * ROW_TILE + lax.broadcasted_iota(jnp.int32, (1, ROW_TILE), 1)
    return jnp.logical_and(_chunk_of(q_rows) >= _chunk_of(k_rows), k_rows >= DEAD)


def _attn_fwd(q_t, k, v_t, n_rows):
    nt = n_rows // ROW_TILE

    def body(qt_ref, k_ref, vt_ref, ot_ref, lse_ref):
        i = pl.program_id(0)
        q_ts = [qt_ref[h] for h in range(N_HEADS)]

        def make_step(masked):
            def step(j, carry):
                rows = pl.ds(pl.multiple_of(j * ROW_TILE, ROW_TILE), ROW_TILE)
                scores = [_dot(k_ref[h, rows, :], q_ts[h]) for h in range(N_HEADS)]
                visible = _visible(i, j) if masked else None
                probs, state = [], []
                for h in range(N_HEADS):
                    m, l, _ = carry[h]
                    s = jnp.where(visible, scores[h], NEG) if masked else scores[h]
                    m_new = jnp.maximum(m, jnp.max(s, axis=0, keepdims=True))
                    alpha = jnp.exp2(m - m_new)
                    p = jnp.exp2(s - m_new)
                    probs.append(p.astype(BF16))
                    state.append((m_new, alpha * l + jnp.sum(p, axis=0, keepdims=True), alpha))
                outs = [_dot(vt_ref[h, :, rows], probs[h]) for h in range(N_HEADS)]
                return tuple((state[h][0], state[h][1], state[h][2] * carry[h][2] + outs[h]) for h in range(N_HEADS))
            return step

        init = tuple((jnp.full((1, ROW_TILE), NEG, F32), jnp.zeros((1, ROW_TILE), F32),
                      jnp.zeros((V_HEAD, ROW_TILE), F32)) for _ in range(N_HEADS))
        carry = make_step(True)(0, init)
        carry = lax.fori_loop(1, i, make_step(False), carry)
        carry = lax.fori_loop(jnp.maximum(i, 1), i + 1, make_step(True), carry)
        for h in range(N_HEADS):
            m, l, acc = carry[h]
            ot_ref[h] = acc / l
            lse_ref[h] = m + jnp.log2(l)

    out_shapes = [jax.ShapeDtypeStruct((N_HEADS, V_HEAD, n_rows), F32), jax.ShapeDtypeStruct((N_HEADS, 1, n_rows), F32)]
    return pl.pallas_call(
        body, name="attn_fwd", grid=(nt,),
        in_specs=[_lane_tile(q_t.shape), _whole_spec(k.shape), _whole_spec(v_t.shape)],
        out_specs=[_lane_tile(s.shape) for s in out_shapes],
        out_shape=out_shapes,
        compiler_params=_params("parallel"),
    )(q_t, k, v_t)


def _heads_to_rows(ref):
    return jnp.concatenate([ref[h] for h in range(N_HEADS)], axis=0)


def _rms_cols(x, g_col):
    r = lax.rsqrt(jnp.mean(x * x, axis=0, keepdims=True) + EPS)
    return x * r * g_col


def _fwd_out(x, meta_pad, mix_a, o_t, gb_col, w_out, n_rows):
    nt = n_rows // ROW_TILE

    def body(x_ref, meta_ref, mixa_ref, ot_ref, gb_ref, w_ref, mixbt_ref, h1_ref):
        i = pl.program_id(0)
        h0 = jnp.where(i == 0, meta_ref[...], x_ref[...])
        mix_bt = _rms_cols(_heads_to_rows(ot_ref), gb_ref[...]).astype(BF16)
        mixbt_ref[...] = mix_bt
        h1_ref[...] = h0 + _dot(mixa_ref[...], w_ref[:D_CONV, :]) + _dot_tn(mix_bt, w_ref[D_CONV:, :])

    out_shapes = [jax.ShapeDtypeStruct((D_ATTN, n_rows), BF16), jax.ShapeDtypeStruct((n_rows, D_MODEL), F32)]
    return pl.pallas_call(
        body, name="fwd_out", grid=(nt,),
        in_specs=[_real_spec(D_MODEL), _whole_spec(meta_pad.shape), _tile_spec(mix_a.shape), _lane_tile(o_t.shape),
                  _whole_spec(gb_col.shape), _whole_spec(w_out.shape)],
        out_specs=[_lane_tile(out_shapes[0].shape), _tile_spec(out_shapes[1].shape)],
        out_shape=out_shapes,
        compiler_params=_params("parallel"),
    )(x, meta_pad, mix_a, o_t, gb_col, w_out)


PHASES = 8
PHASE_ROWS = ROW_TILE // PHASES
UP_PLANES = -(-UP_SLAB // _LANES)
UP_PAD = UP_PLANES * _LANES
CONV_PLANES = D_CONV // _LANES


def _phase(start):
    return pl.ds(start, PHASE_ROWS, stride=PHASES)


def _to_planes(ref, lead, rows, value):
    width = value.shape[-1]
    for c in range(-(-width // _LANES)):
        part = value[:, c * _LANES:min((c + 1) * _LANES, width)]
        if part.shape[-1] < _LANES:
            part = jnp.concatenate([part, jnp.zeros((part.shape[0], _LANES - part.shape[-1]), part.dtype)], axis=-1)
        ref[(*lead, c, rows, slice(None))] = part


def _from_planes(ref, lead, width):
    planes = [ref[(*lead, c)] for c in range(-(-width // _LANES))]
    last = width - (len(planes) - 1) * _LANES
    return jnp.concatenate(planes[:-1] + [planes[-1][:, :last]], axis=-1)


def _fwd_ffn(h1, target, g2, w_up, fw, fb, w_down, gf, n_rows):
    nt = n_rows // ROW_TILE

    def body(h1_ref, t_ref, g2_ref, wup_ref, fw_ref, fb_ref, wdn_ref, gf_ref,
             n2_ref, up0_ref, act_ref, da_ref, db_ref, dh2_ref, loss_ref, dgf_ref, ext_ref):
        i = pl.program_id(0)

        @pl.when(i == 0)
        def _():
            ext_ref[:, 0:FFN_HALO, :] = jnp.zeros((N_DEV, FFN_HALO, UP_SLAB), F32)

        h1_t = h1_ref[...]
        live = _row_ids(i, ROW_TILE) >= DEAD
        n2 = jnp.where(live, _rms_fwd(h1_t, g2_ref[...]), 0.0).astype(BF16)
        n2_ref[...] = n2
        for s in range(N_DEV):
            up0 = _dot_nt(n2, wup_ref[s])
            up0_ref[s] = up0.astype(BF16)
            ext_ref[s, FFN_HALO:, :] = up0
        first = FFN_HALO - (FFN_CONV_WIDTH - 1)

        def conv(s):
            acc = fb_ref[s, :, :UP_SLAB]
            for k in range(FFN_CONV_WIDTH):
                acc = acc + fw_ref[s, k:k + 1, :UP_SLAB] * ext_ref[s, first + k:first + k + ROW_TILE, :]
            return acc

        h2 = h1_t
        for s in range(N_ACT_SLAB):
            gate = conv(s)
            val = conv(s + N_ACT_SLAB)
            sg = _sigmoid(gate)
            silu = gate * sg
            act = (silu * val).astype(BF16)
            act_ref[s] = act
            da_ref[s] = (val * sg * (1.0 + gate * (1.0 - sg))).astype(BF16)
            db_ref[s] = silu.astype(BF16)
            h2 = h2 + _dot(act, wdn_ref[s])
        ext_ref[:, 0:FFN_HALO, :] = ext_ref[:, ROW_TILE:ROW_TILE + FFN_HALO, :]

        gf_t = gf_ref[...]
        y = _rms_fwd(h2, gf_t)
        diff = jnp.where(i >= 1, y - t_ref[...], 0.0)
        tile_loss = 0.5 * jnp.sum(jnp.sum(diff * diff, axis=-1, keepdims=True), axis=0, keepdims=True) / D_MODEL
        dh2, dgf = _rms_bwd(diff / D_MODEL, h2, gf_t)
        dh2_ref[...] = dh2
        _accumulate(loss_ref, i == 0, jnp.broadcast_to(tile_loss, loss_ref.shape))
        _accumulate(dgf_ref, i == 0, dgf)

    act_like = jax.ShapeDtypeStruct((N_ACT_SLAB, n_rows, UP_SLAB), BF16)
    out_shapes = [
        jax.ShapeDtypeStruct((n_rows, D_MODEL), BF16),
        jax.ShapeDtypeStruct((N_DEV, n_rows, UP_SLAB), BF16),
        act_like, act_like, act_like,
        jax.ShapeDtypeStruct((n_rows, D_MODEL), F32),
        jax.ShapeDtypeStruct((8, 128), F32),
        jax.ShapeDtypeStruct((1, D_MODEL), F32),
    ]
    whole = [g2, w_up, fw, fb, w_down, gf]
    return pl.pallas_call(
        body, name="fwd_ffn", grid=(nt,),
        in_specs=[_tile_spec(h1.shape), _real_spec(D_MODEL)] + [_whole_spec(a.shape) for a in whole],
        out_specs=[_tile_spec(s.shape) for s in out_shapes[:6]] + [_acc_spec(s.shape) for s in out_shapes[6:]],
        out_shape=out_shapes,
        scratch_shapes=[pltpu.VMEM((N_DEV, ROW_TILE + FFN_HALO, UP_SLAB), F32)],
        compiler_params=_params("arbitrary"),
    )(h1, target, *whole)


def _rope_tables(n_rows):
    pos = jnp.maximum(jnp.arange(n_rows, dtype=jnp.int32) - DEAD, 0)
    inv_freq = 1.0 / (ROPE_THETA ** (jnp.arange(0, QK_ROPE, 2, dtype=F32) / QK_ROPE))
    ang_t = inv_freq[:, None] * pos.astype(F32)[None, :]
    return jnp.cos(ang_t), jnp.sin(ang_t)


def _halo_after(shape, halo, n_rows):
    last = n_rows // halo - 1
    step = ROW_TILE // halo
    if len(shape) == 2:
        return pl.BlockSpec((halo, shape[1]), lambda i: (jnp.minimum((i + 1) * step, last), 0))
    return pl.BlockSpec((shape[0], halo, shape[2]), lambda i: (0, jnp.minimum((i + 1) * step, last), 0))


def _halo_before(shape, halo):
    step = ROW_TILE // halo
    if len(shape) == 2:
        return pl.BlockSpec((halo, shape[1]), lambda i: (jnp.maximum(i * step - 1, 0), 0))
    return pl.BlockSpec((shape[0], halo, shape[2]), lambda i: (0, jnp.maximum(i * step - 1, 0), 0))


def _bwd_ffn_act(dh2, da, db, w_down, n_rows):
    nt = n_rows // ROW_TILE

    def body(dh2_ref, da_ref, db_ref, wdn_ref, dup_ref, dfb_ref):
        i = pl.program_id(0)

        @pl.when(i == 0)
        def _():
            dfb_ref[...] = jnp.zeros_like(dfb_ref)

        dh2_b = dh2_ref[...].astype(BF16)
        for s in range(N_ACT_SLAB):
            d_act = _dot_nt(dh2_b, wdn_ref[s])
            d_gate = d_act * da_ref[s].astype(F32)
            d_val = d_act * db_ref[s].astype(F32)
            dup_ref[s] = d_gate.astype(BF16)
            dup_ref[s + N_ACT_SLAB] = d_val.astype(BF16)
            dfb_ref[s] += jnp.sum(d_gate, axis=0, keepdims=True)
            dfb_ref[s + N_ACT_SLAB] += jnp.sum(d_val, axis=0, keepdims=True)

    out_shapes = [jax.ShapeDtypeStruct((N_DEV, n_rows, UP_SLAB), BF16), jax.ShapeDtypeStruct((N_DEV, 1, UP_SLAB), F32)]
    return pl.pallas_call(
        body, name="bwd_ffn_act", grid=(nt,),
        in_specs=[_tile_spec(dh2.shape), _tile_spec(da.shape), _tile_spec(db.shape), _whole_spec(w_down.shape)],
        out_specs=[_tile_spec(out_shapes[0].shape), _acc_spec(out_shapes[1].shape)],
        out_shape=out_shapes,
        compiler_params=_params("arbitrary"),
    )(dh2, da, db, w_down)


def _bwd_ffn_up(dup, up0, h1, dh2, g2, w_up, fw, n_rows):
    nt = n_rows // ROW_TILE
    last_tap = FFN_CONV_WIDTH - 1

    def body(dup_ref, dnext_ref, up0_ref, h1_ref, dh2_ref, g2_ref, wup_ref, fw_ref,
             dup0_ref, dh1_ref, dfw_ref, dg2_ref, dext_ref, uext_ref, conv_ref):
        i = pl.program_id(0)

        @pl.when(i == 0)
        def _():
            dfw_ref[...] = jnp.zeros_like(dfw_ref)

        live = _row_ids(i, ROW_TILE) >= DEAD
        dn2 = jnp.zeros((ROW_TILE, D_MODEL), F32)
        for s in range(N_DEV):
            _to_planes(dext_ref, (), slice(0, ROW_TILE), dup_ref[s].astype(F32))
            _to_planes(dext_ref, (), slice(ROW_TILE, None), jnp.where(i == nt - 1, 0.0, dnext_ref[s].astype(F32)))
            _to_planes(uext_ref, (), slice(None), up0_ref[s].astype(F32))
            for c in range(UP_PLANES):
                lanes = slice(c * _LANES, (c + 1) * _LANES)
                taps = [fw_ref[s, k:k + 1, lanes] for k in range(FFN_CONV_WIDTH)]
                sums = [jnp.zeros((PHASE_ROWS, _LANES), F32) for _ in range(FFN_CONV_WIDTH)]
                for p in range(PHASES):
                    u = uext_ref[c, _phase(p), :]
                    acc = jnp.zeros((PHASE_ROWS, _LANES), F32)
                    for k in range(FFN_CONV_WIDTH):
                        shifted = dext_ref[c, _phase(p + last_tap - k), :]
                        acc = acc + taps[k] * shifted
                        sums[k] = sums[k] + shifted * u
                    conv_ref[c, _phase(p), :] = acc
                for k in range(FFN_CONV_WIDTH):
                    dfw_ref[s, k:k + 1, lanes] += jnp.sum(sums[k], axis=0, keepdims=True)
            dup0_b = jnp.where(live, _from_planes(conv_ref, (), UP_SLAB), 0.0).astype(BF16)
            dup0_ref[s] = dup0_b
            dn2 = dn2 + _dot(dup0_b, wup_ref[s])
        dx, dg2 = _rms_bwd(dn2, h1_ref[...], g2_ref[...])
        dh1_ref[...] = dh2_ref[...] + dx
        _accumulate(dg2_ref, i == 0, dg2)

    out_shapes = [
        jax.ShapeDtypeStruct((N_DEV, n_rows, UP_SLAB), BF16),
        jax.ShapeDtypeStruct((n_rows, D_MODEL), F32),
        jax.ShapeDtypeStruct((N_DEV, FFN_CONV_WIDTH, UP_PAD), F32),
        jax.ShapeDtypeStruct((1, D_MODEL), F32),
    ]
    return pl.pallas_call(
        body, name="bwd_ffn_up", grid=(nt,),
        in_specs=[_tile_spec(dup.shape), _halo_after(dup.shape, FFN_HALO, n_rows), _tile_spec(up0.shape),
                  _tile_spec(h1.shape), _tile_spec(dh2.shape),
                  _whole_spec(g2.shape), _whole_spec(w_up.shape), _whole_spec(fw.shape)],
        out_specs=[_tile_spec(s.shape) for s in out_shapes[:2]] + [_acc_spec(s.shape) for s in out_shapes[2:]],
        out_shape=out_shapes,
        scratch_shapes=[pltpu.VMEM((UP_PLANES, ROW_TILE + FFN_HALO, _LANES), F32),
                        pltpu.VMEM((UP_PLANES, ROW_TILE, _LANES), F32), pltpu.VMEM((UP_PLANES, ROW_TILE, _LANES), F32)],
        compiler_params=_params("arbitrary"),
    )(dup, dup, up0, h1, dh2, g2, w_up, fw)


def _bwd_out(dh1, o_t, u1, w_out, gb_col, ln_g, ln_b, ga, n_rows):
    nt = n_rows // ROW_TILE

    def body(dh1_ref, ot_ref, u1_ref, w_ref, gb_ref, lg_ref, lb_ref, ga_ref,
             dot_ref, delta_ref, du1_ref, dgb_ref, dga_ref, dlg_ref, dlb_ref, dcb_ref):
        i = pl.program_id(0)
        dh1_b = dh1_ref[...].astype(BF16)
        o_t = _heads_to_rows(ot_ref)
        gb = gb_ref[...]
        r = lax.rsqrt(jnp.mean(o_t * o_t, axis=0, keepdims=True) + EPS)
        dmix_bt = _dot_nt(w_ref[D_CONV:, :], dh1_b)
        wgt = dmix_bt * gb
        do_t = r * wgt - o_t * (r * r * r) * jnp.mean(wgt * o_t, axis=0, keepdims=True)
        dgb = jnp.sum(dmix_bt * o_t * r, axis=1, keepdims=True)
        for h in range(N_HEADS):
            do_h = do_t[h * V_HEAD:(h + 1) * V_HEAD]
            dot_ref[h] = do_h.astype(BF16)
            delta_ref[h] = jnp.sum(do_h * ot_ref[h], axis=0, keepdims=True)
        lg = lg_ref[...]
        xh, u2, u3, rstd = _conv_chain(u1_ref[...], lg, lb_ref[...])
        du3, dga = _rms_bwd(_dot_nt(dh1_b, w_ref[:D_CONV, :]), u3, ga_ref[...])
        sg = _sigmoid(u2)
        du2 = du3 * sg * (1.0 + u2 * (1.0 - sg))
        dxh = du2 * lg
        du1 = rstd * (dxh - jnp.mean(dxh, axis=-1, keepdims=True) - xh * jnp.mean(dxh * xh, axis=-1, keepdims=True))
        du1_ref[...] = du1
        first = i == 0
        _accumulate(dgb_ref, first, dgb)
        _accumulate(dga_ref, first, dga)
        _accumulate(dlg_ref, first, jnp.sum(du2 * xh, axis=0, keepdims=True))
        _accumulate(dlb_ref, first, jnp.sum(du2, axis=0, keepdims=True))
        _accumulate(dcb_ref, first, jnp.sum(du1, axis=0, keepdims=True))

    out_shapes = [
        jax.ShapeDtypeStruct((N_HEADS, V_HEAD, n_rows), BF16),
        jax.ShapeDtypeStruct((N_HEADS, 1, n_rows), F32),
        jax.ShapeDtypeStruct((n_rows, D_CONV), F32),
        jax.ShapeDtypeStruct((D_ATTN, 1), F32),
    ] + [jax.ShapeDtypeStruct((1, D_CONV), F32)] * 4
    whole = [w_out, gb_col, ln_g, ln_b, ga]
    return pl.pallas_call(
        body, name="bwd_out", grid=(nt,),
        in_specs=[_tile_spec(dh1.shape), _lane_tile(o_t.shape), _tile_spec(u1.shape)] + [_whole_spec(a.shape) for a in whole],
        out_specs=[_lane_tile(out_shapes[0].shape), _lane_tile(out_shapes[1].shape), _tile_spec(out_shapes[2].shape)]
        + [_acc_spec(s.shape) for s in out_shapes[3:]],
        out_shape=out_shapes,
        compiler_params=_params("arbitrary"),
    )(dh1, o_t, u1, *whole)


ATTN_BWD_HEADS = 4


def _attn_bwd(q_t, k, v, do_t, lse, delta, n_rows):
    nt = n_rows // ROW_TILE
    hp = ATTN_BWD_HEADS

    def body(k_ref, v_ref, qt_ref, dot_ref, lse_ref, delta_ref, dqt_ref, dk_ref, dv_ref):
        j = pl.program_id(1)

        @pl.when(j == 0)
        def _():
            dqt_ref[...] = jnp.zeros_like(dqt_ref)

        k_ts = [k_ref[h] for h in range(hp)]
        v_ts = [v_ref[h] for h in range(hp)]

        def make_step(masked):
            def step(i, carry):
                cols = pl.ds(pl.multiple_of(i * ROW_TILE, ROW_TILE), ROW_TILE)
                q_is = [qt_ref[h, :, cols] for h in range(hp)]
                do_is = [dot_ref[h, :, cols] for h in range(hp)]
                scores = [_dot(k_ts[h], q_is[h]) for h in range(hp)]
                dps = [_dot(v_ts[h], do_is[h]) for h in range(hp)]
                visible = _visible(i, j) if masked else None
                probs, dss = [], []
                for h in range(hp):
                    s = jnp.where(visible, scores[h], NEG) if masked else scores[h]
                    p = jnp.exp2(s - lse_ref[h, :, cols])
                    probs.append(p.astype(BF16))
                    dss.append((p * (dps[h] - delta_ref[h, :, cols])).astype(BF16))
                out = []
                for h in range(hp):
                    dk, dv = carry[h]
                    dv = dv + _dot_nt(probs[h], do_is[h])
                    dk = dk + _dot_nt(dss[h], q_is[h])
                    dqt_ref[h, :, cols] += _dot_tn(k_ts[h], dss[h])
                    out.append((dk, dv))
                return tuple(out)
            return step

        init = tuple((jnp.zeros((ROW_TILE, QK_DIM), F32), jnp.zeros((ROW_TILE, V_HEAD), F32)) for _ in range(hp))
        carry = make_step(True)(j, init)
        carry = lax.fori_loop(jnp.where(j == 0, j + 1, nt), nt, make_step(True), carry)
        carry = lax.fori_loop(jnp.where(j == 0, nt, j + 1), nt, make_step(False), carry)
        for h in range(hp):
            dk_ref[h] = carry[h][0] * _LN2
            dv_ref[h] = carry[h][1]

    key_tile = lambda w: pl.BlockSpec((hp, ROW_TILE, w), lambda g, j: (g, j, 0))
    all_cols = lambda w: pl.BlockSpec((hp, w, n_rows), lambda g, j: (g, 0, 0))
    out_shapes = [
        jax.ShapeDtypeStruct((N_HEADS, QK_DIM, n_rows), F32),
        jax.ShapeDtypeStruct((N_HEADS, n_rows, QK_DIM), F32),
        jax.ShapeDtypeStruct((N_HEADS, n_rows, V_HEAD), F32),
    ]
    return pl.pallas_call(
        body, name="attn_bwd", grid=(N_HEADS // hp, nt),
        in_specs=[key_tile(QK_DIM), key_tile(V_HEAD), all_cols(QK_DIM), all_cols(V_HEAD), all_cols(1), all_cols(1)],
        out_specs=[all_cols(QK_DIM), key_tile(QK_DIM), key_tile(V_HEAD)],
        out_shape=out_shapes,
        compiler_params=_params("parallel", "arbitrary"),
    )(k, v, q_t, do_t, lse, delta)


def _bwd_qkv(dq_t, dk, dv, cq, ckv, gq, gkv, wq_t, w_ukv, cos, sin, cos_t, sin_t, n_rows):
    nt = n_rows // ROW_TILE

    def body(dqt_ref, dk_ref, dv_ref, cq_ref, ckv_ref, gq_ref, gkv_ref, wqt_ref, wkv_ref, cos_ref, sin_ref,
             cost_ref, sint_ref, dqraw_ref, dkv_ref, dcq_ref, dckv_ref, dkr_ref, dgq_ref, dgkv_ref):
        i = pl.program_id(0)
        cos_rows, sin_rows = cost_ref[...], sint_ref[...]
        dcqn = jnp.zeros((ROW_TILE, Q_LORA), F32)
        dckvn = jnp.zeros((ROW_TILE, KV_LORA), F32)
        dk_rot = jnp.zeros((ROW_TILE, QK_ROPE), F32)
        for h in range(N_HEADS):
            dq_h, dk_h = dqt_ref[h] * QK_DIM ** -0.5, dk_ref[h]
            dq_raw = jnp.concatenate(
                [dq_h[:QK_NOPE], _rope_rows_t(dq_h[QK_NOPE:], cos_rows, sin_rows)], axis=0).astype(BF16)
            dqraw_ref[h] = dq_raw
            dcqn = dcqn + _dot_tn(dq_raw, wqt_ref[h])
            dkv = jnp.concatenate([dk_h[:, :QK_NOPE], dv_ref[h]], axis=-1).astype(BF16)
            dkv_ref[:, h * KV_HEAD:(h + 1) * KV_HEAD] = dkv
            dckvn = dckvn + _dot_nt(dkv, wkv_ref[h])
            dk_rot = dk_rot + dk_h[:, QK_NOPE:]
        dkr_ref[...] = _rope_t(dk_rot, cos_ref[...], sin_ref[...]).astype(BF16)
        dcq, dgq = _rms_bwd(dcqn, cq_ref[...], gq_ref[...])
        dckv, dgkv = _rms_bwd(dckvn, ckv_ref[...], gkv_ref[...])
        dcq_ref[...] = dcq.astype(BF16)
        dckv_ref[...] = dckv.astype(BF16)
        _accumulate(dgq_ref, i == 0, dgq)
        _accumulate(dgkv_ref, i == 0, dgkv)

    out_shapes = [
        jax.ShapeDtypeStruct((N_HEADS, QK_DIM, n_rows), BF16),
        jax.ShapeDtypeStruct((n_rows, N_HEADS * KV_HEAD), BF16),
        jax.ShapeDtypeStruct((n_rows, Q_LORA), BF16),
        jax.ShapeDtypeStruct((n_rows, KV_LORA), BF16),
        jax.ShapeDtypeStruct((n_rows, QK_ROPE), BF16),
        jax.ShapeDtypeStruct((1, Q_LORA), F32),
        jax.ShapeDtypeStruct((1, KV_LORA), F32),
    ]
    tiles = [dk, dv, cq, ckv]
    whole = [gq, gkv, wq_t, w_ukv]
    return pl.pallas_call(
        body, name="bwd_qkv", grid=(nt,),
        in_specs=[_lane_tile(dq_t.shape)] + [_tile_spec(a.shape) for a in tiles] + [_whole_spec(a.shape) for a in whole]
        + [_tile_spec(cos.shape), _tile_spec(sin.shape), _lane_tile(cos_t.shape), _lane_tile(sin_t.shape)],
        out_specs=[_lane_tile(out_shapes[0].shape)] + [_tile_spec(s.shape) for s in out_shapes[1:5]]
        + [_acc_spec(s.shape) for s in out_shapes[5:]],
        out_shape=out_shapes,
        compiler_params=_params("arbitrary"),
    )(dq_t, *tiles, *whole, cos, sin, cos_t, sin_t)


def _bwd_conv(du1, ag, conv_w, dcq, dckv, dkr, n_rows):
    nt = n_rows // ROW_TILE

    last_tap = CONV_WIDTH - 1

    def body(du1_ref, dnext_ref, ag_ref, w_ref, dcq_ref, dckv_ref, dkr_ref, dz_ref, dw_ref,
             dext_ref, uext_ref, conv_ref, sums_ref):
        i = pl.program_id(0)

        @pl.when(i == 0)
        def _():
            sums_ref[...] = jnp.zeros_like(sums_ref)

        _to_planes(dext_ref, (), slice(0, ROW_TILE), du1_ref[...])
        _to_planes(dext_ref, (), slice(ROW_TILE, None), jnp.where(i == nt - 1, 0.0, dnext_ref[...]))
        ag_t = ag_ref[...]
        live = _row_ids(i, ROW_TILE) >= DEAD
        sg = _sigmoid(ag_t[:, D_CONV:])
        _to_planes(uext_ref, (), slice(None), jnp.where(live, ag_t[:, :D_CONV] * sg, 0.0))
        for c in range(CONV_PLANES):
            taps = w_ref[:, c * _LANES:(c + 1) * _LANES]
            for p in range(PHASES):
                u = uext_ref[c, _phase(p), :]
                acc = jnp.zeros((PHASE_ROWS, _LANES), F32)
                for k in range(CONV_WIDTH):
                    shifted = dext_ref[c, _phase(p + last_tap - k), :]
                    acc = acc + taps[k:k + 1, :] * shifted
                    sums_ref[c, k] += shifted * u
                conv_ref[c, _phase(p), :] = acc
        du0 = jnp.where(live, _from_planes(conv_ref, (), D_CONV), 0.0)
        da = du0 * sg
        dgate = du0 * ag_t[:, :D_CONV] * sg * (1.0 - sg)
        dz_ref[...] = jnp.concatenate(
            [da.astype(BF16), dgate.astype(BF16), dcq_ref[...], dckv_ref[...], dkr_ref[...]], axis=-1)

        @pl.when(i == nt - 1)
        def _():
            for c in range(CONV_PLANES):
                for k in range(CONV_WIDTH):
                    dw_ref[k:k + 1, c * _LANES:(c + 1) * _LANES] = jnp.sum(sums_ref[c, k], axis=0, keepdims=True)

    out_shapes = [jax.ShapeDtypeStruct((n_rows, D_IN), BF16), jax.ShapeDtypeStruct((CONV_WIDTH, D_CONV), F32)]
    return pl.pallas_call(
        body, name="bwd_conv", grid=(nt,),
        in_specs=[_tile_spec(du1.shape), _halo_after(du1.shape, CONV_HALO, n_rows), _tile_spec(ag.shape),
                  _whole_spec(conv_w.shape), _tile_spec(dcq.shape), _tile_spec(dckv.shape), _tile_spec(dkr.shape)],
        out_specs=[_tile_spec(out_shapes[0].shape), _acc_spec(out_shapes[1].shape)],
        out_shape=out_shapes,
        scratch_shapes=[pltpu.VMEM((CONV_PLANES, ROW_TILE + CONV_HALO, _LANES), F32),
                        pltpu.VMEM((CONV_PLANES, ROW_TILE, _LANES), F32), pltpu.VMEM((CONV_PLANES, ROW_TILE, _LANES), F32),
                        pltpu.VMEM((CONV_PLANES, CONV_WIDTH, PHASE_ROWS, _LANES), F32)],
        compiler_params=_params("arbitrary"),
    )(du1, du1, ag, conv_w, dcq, dckv, dkr)


def _bwd_in(dz, x, meta_pad, dh1, g1, w_in, n_rows):
    nt = n_rows // ROW_TILE

    def body(dz_ref, x_ref, meta_ref, dh1_ref, g_ref, w_ref, gx_ref, gmeta_ref, dg1_ref):
        i = pl.program_id(0)
        h0 = jnp.where(i == 0, meta_ref[...], x_ref[...])
        dx, dg1 = _rms_bwd(_dot(dz_ref[...], w_ref[...]), h0, g_ref[...])
        dh0 = dh1_ref[...] + dx
        gx_ref[...] = dh0

        @pl.when(i == 0)
        def _():
            gmeta_ref[...] = dh0

        _accumulate(dg1_ref, i == 0, dg1)

    out_shapes = [
        jax.ShapeDtypeStruct((n_rows - ROW_TILE, D_MODEL), F32),
        jax.ShapeDtypeStruct((ROW_TILE, D_MODEL), F32),
        jax.ShapeDtypeStruct((1, D_MODEL), F32),
    ]
    return pl.pallas_call(
        body, name="bwd_in", grid=(nt,),
        in_specs=[_tile_spec(dz.shape), _real_spec(D_MODEL), _whole_spec(meta_pad.shape), _tile_spec(dh1.shape),
                  _whole_spec(g1.shape), _whole_spec(w_in.shape)],
        out_specs=[_real_spec(D_MODEL), _acc_spec(out_shapes[1].shape), _acc_spec(out_shapes[2].shape)],
        out_shape=out_shapes,
        compiler_params=_params("arbitrary"),
    )(dz, x, meta_pad, dh1, g1, w_in)


def _contraction_tile(n_rows):
    return next(t for t in range(n_rows // 2 // _LANES * _LANES, 0, -_LANES) if n_rows % t == 0)


def _weight_grad(a, b, name, a_transposed=False):
    groups = max(a.shape[0] if a.ndim == 3 else 1, b.shape[0] if b.ndim == 3 else 1)
    n_rows, n = b.shape[-2], b.shape[-1]
    m = a.shape[-2] if a_transposed else a.shape[-1]
    kt = _contraction_tile(n_rows)
    steps = n_rows // kt

    def body(a_ref, b_ref, out_ref, acc_ref):
        i = pl.program_id(1)
        a_t, b_t = a_ref[...].astype(BF16), b_ref[...].astype(BF16)
        part = _dot(a_t, b_t) if a_transposed else _dot_tn(a_t, b_t)
        _accumulate(acc_ref, i == 0, part)

        @pl.when(i == steps - 1)
        def _():
            out_ref[...] = acc_ref[...].astype(out_ref.dtype)

    def spec(arr, rows_last):
        block = (arr.shape[-2], kt) if rows_last else (kt, arr.shape[-1])
        at = (lambda i: (0, i)) if rows_last else (lambda i: (i, 0))
        if arr.ndim == 3:
            return pl.BlockSpec((None,) + block, lambda g, i: (g,) + at(i))
        return pl.BlockSpec(block, lambda g, i: at(i))

    return pl.pallas_call(
        body, name=name, grid=(groups, steps),
        in_specs=[spec(a, a_transposed), spec(b, False)],
        out_specs=pl.BlockSpec((None, m, n), lambda g, i: (g, 0, 0)),
        out_shape=jax.ShapeDtypeStruct((groups, m, n), BF16),
        scratch_shapes=[pltpu.VMEM((m, n), F32)],
        compiler_params=_params("parallel", "arbitrary"),
    )(a, b)


def _my_index():
    return 4 * lax.axis_index("x") + 2 * lax.axis_index("y") + lax.axis_index("c")


def _peer(k):
    flip = lambda v, bit: 1 - v if bit else v
    px = flip(lax.axis_index("x"), k & 4)
    py = flip(lax.axis_index("y"), k & 2)
    pc = flip(lax.axis_index("c"), k & 1)
    return (px, py, pc), 4 * px + 2 * py + pc


def _all_gather(shards, dtypes):
    n = len(shards)

    def body(*refs):
        ins, outs, stages = refs[:n], refs[n:2 * n], refs[2 * n:3 * n]
        send_sems, recv_sems, local_sems = refs[3 * n:]
        me = _my_index()
        for a in range(n):
            stages[a][...] = ins[a][...].astype(stages[a].dtype)
        local = [pltpu.make_async_copy(stages[a], outs[a].at[me], local_sems.at[a]) for a in range(n)]
        for cp in local:
            cp.start()

        def copy(a, k, slot):
            peer, _ = _peer(k)
            return pltpu.make_async_remote_copy(
                src_ref=stages[a], dst_ref=outs[a].at[slot], send_sem=send_sems.at[a, k - 1],
                recv_sem=recv_sems.at[a, k - 1], device_id=peer, device_id_type=MESH)

        for k in range(1, N_DEV):
            for a in range(n):
                copy(a, k, me).start()
        for k in range(1, N_DEV):
            for a in range(n):
                copy(a, k, _peer(k)[1]).wait()
        for cp in local:
            cp.wait()

    return pl.pallas_call(
        body, name="gather_weights",
        in_specs=[pl.BlockSpec(memory_space=pltpu.VMEM)] * n,
        out_specs=[pl.BlockSpec(memory_space=pl.ANY)] * n,
        out_shape=[jax.ShapeDtypeStruct((N_DEV,) + s.shape, dt) for s, dt in zip(shards, dtypes)],
        scratch_shapes=[pltpu.VMEM(s.shape, dt) for s, dt in zip(shards, dtypes)]
        + [pltpu.SemaphoreType.DMA((n, N_DEV - 1)), pltpu.SemaphoreType.DMA((n, N_DEV - 1)), pltpu.SemaphoreType.DMA((n,))],
        compiler_params=pltpu.CompilerParams(vmem_limit_bytes=VMEM_LIMIT),
    )(*shards)


def _exchange(parts, whole):
    n = len(parts)

    def body(*refs):
        ins, outs = refs[:n], refs[n:2 * n]
        send_sems, recv_sems, local_sems = refs[2 * n:]
        me = _my_index()

        def src(a, slab):
            return ins[a] if whole[a] else ins[a].at[slab]

        local = [pltpu.make_async_copy(src(a, me), outs[a].at[me], local_sems.at[a]) for a in range(n)]
        for cp in local:
            cp.start()

        def copy(a, k, slab, slot):
            peer, _ = _peer(k)
            return pltpu.make_async_remote_copy(
                src_ref=src(a, slab), dst_ref=outs[a].at[slot], send_sem=send_sems.at[a, k - 1],
                recv_sem=recv_sems.at[a, k - 1], device_id=peer, device_id_type=MESH)

        for k in range(1, N_DEV):
            for a in range(n):
                copy(a, k, _peer(k)[1], me).start()
        for k in range(1, N_DEV):
            for a in range(n):
                copy(a, k, _peer(k)[1], _peer(k)[1]).wait()
        for cp in local:
            cp.wait()

    return pl.pallas_call(
        body, name="exchange_grads",
        in_specs=[pl.BlockSpec(memory_space=pl.ANY)] * n,
        out_specs=[pl.BlockSpec(memory_space=pl.ANY)] * n,
        out_shape=[jax.ShapeDtypeStruct(((N_DEV,) + p.shape) if w else p.shape, p.dtype) for p, w in zip(parts, whole)],
        scratch_shapes=[pltpu.SemaphoreType.DMA((n, N_DEV - 1)), pltpu.SemaphoreType.DMA((n, N_DEV - 1)),
                        pltpu.SemaphoreType.DMA((n,))],
    )(*parts)


def _sequencer_exchange(parts, whole, name, collective_id):
    n = len(parts)
    srcs = [jax.new_ref(p, memory_space=pltpu.MemorySpace.HBM) for p in parts]
    lands = [jax.empty_ref(jax.ShapeDtypeStruct(((N_DEV,) + p.shape) if w else p.shape, p.dtype),
                           memory_space=pltpu.MemorySpace.HBM) for p, w in zip(parts, whole)]

    @pl.kernel(mesh=plsc.ScalarSubcoreMesh(axis_name="sequencer", num_cores=1), name=name,
               scratch_types=(pltpu.SemaphoreType.DMA((n, N_DEV - 1)), pltpu.SemaphoreType.DMA((n, N_DEV - 1)),
                              pltpu.SemaphoreType.DMA((n,))),
               compiler_params=pltpu.CompilerParams(collective_id=collective_id))
    def launch(send_sems, recv_sems, local_sems):
        barrier = pltpu.get_barrier_semaphore()
        for k in range(1, N_DEV):
            pl.semaphore_signal(barrier, inc=1, device_id=_peer(k)[0], device_id_type=MESH)
        pl.semaphore_wait(barrier, N_DEV - 1)
        me = _my_index()

        def src(a, slab):
            return srcs[a] if whole[a] else srcs[a].at[slab]

        local = [pltpu.make_async_copy(src(a, me), lands[a].at[me], local_sems.at[a]) for a in range(n)]
        for cp in local:
            cp.start()

        def copy(a, k, slab, slot):
            return pltpu.make_async_remote_copy(
                src_ref=src(a, slab), dst_ref=lands[a].at[slot], send_sem=send_sems.at[a, k - 1],
                recv_sem=recv_sems.at[a, k - 1], device_id=_peer(k)[0], device_id_type=MESH)

        for k in range(1, N_DEV):
            for a in range(n):
                copy(a, k, _peer(k)[1], me).start()
        for k in range(1, N_DEV):
            for a in range(n):
                copy(a, k, _peer(k)[1], _peer(k)[1]).wait()
        for cp in local:
            cp.wait()

    launch()
    return [land[...] for land in lands]


def _row_block(rows):
    if rows <= ROW_TILE:
        return rows
    return next(rb for rb in range(ROW_TILE, 0, -16) if rows % rb == 0)


def _adamw(landing, w, m, v, name):
    rows, cols = w.shape
    rb = _row_block(rows)

    def body(l_ref, w_ref, m_ref, v_ref, g_ref, d_ref, m2_ref, v2_ref):
        g = l_ref[0].astype(F32)
        for p in range(1, N_DEV):
            g = g + l_ref[p].astype(F32)
        g_ref[...] = g
        d_ref[...], m2_ref[...], v2_ref[...] = _adamw_step(g, w_ref[...], m_ref[...], v_ref[...])

    flat = pl.BlockSpec((rb, cols), lambda i: (i, 0))
    return pl.pallas_call(
        body, name=name, grid=(rows // rb,),
        in_specs=[pl.BlockSpec((N_DEV, rb, cols), lambda i: (0, i, 0)), flat, flat, flat],
        out_specs=[flat] * 4,
        out_shape=[jax.ShapeDtypeStruct((rows, cols), F32)] * 4,
        compiler_params=_params("parallel"),
    )(landing, w, m, v)


def _adamw_step(g, w, m, v):
    m2 = ADAM_B1 * m + (1.0 - ADAM_B1) * g
    v2 = ADAM_B2 * v + (1.0 - ADAM_B2) * (g * g)
    m_hat = m2 / (1.0 - ADAM_B1 ** ADAM_STEP)
    v_hat = v2 / (1.0 - ADAM_B2 ** ADAM_STEP)
    return -ADAM_LR * (m_hat / (jnp.sqrt(v_hat) + ADAM_EPS) + ADAM_WD * w), m2, v2


_REPLICATED = (
    ("mix_norm_g", D_MODEL), ("q_norm_g", Q_LORA), ("kv_norm_g", KV_LORA), ("conv_b", D_CONV), ("conv_ln_g", D_CONV),
    ("conv_ln_b", D_CONV), ("conv_out_g", D_CONV), ("attn_out_g", D_CONV), ("ffn_norm_g", D_MODEL),
    ("ffn_conv_b", D_UP), ("final_norm_g", D_MODEL),
)
_REPLICATED_WIDTH = sum(size for _, size in _REPLICATED) + _LANES

_WEIGHT_ORDER = (
    "meta_tokens", "mix_norm_g", "w_in", "q_norm_g", "w_uq", "kv_norm_g", "w_ukv", "conv_w", "conv_b", "conv_ln_g",
    "conv_ln_b", "conv_out_g", "attn_out_g", "w_out", "ffn_norm_g", "w_ffn_up", "ffn_conv_w", "ffn_conv_b",
    "w_ffn_down", "final_norm_g",
)


def _pack_replicated(grads, loss):
    rows = [grads[name].reshape(1, size) for name, size in _REPLICATED]
    return jnp.concatenate(rows + [jnp.broadcast_to(loss.reshape(1, 1), (1, _LANES))], axis=-1)


def _adamw_replicated(landing, weights, moments_m, moments_v):
    n = len(_REPLICATED)

    def body(*refs):
        l_ref, ins, outs = refs[0], refs[1:1 + 3 * n], refs[1 + 3 * n:]
        total = l_ref[0]
        for p in range(1, N_DEV):
            total = total + l_ref[p]
        at = 0
        for a, (_, size) in enumerate(_REPLICATED):
            g = total[:, at:at + size]
            w_ref, m_ref, v_ref = ins[3 * a:3 * a + 3]
            g_ref, d_ref, m2_ref, v2_ref = outs[4 * a:4 * a + 4]
            g_ref[...] = g
            d_ref[...], m2_ref[...], v2_ref[...] = _adamw_step(g, w_ref[...], m_ref[...], v_ref[...])
            at += size
        outs[-1][...] = total[:, at:at + _LANES]

    operands, out_shapes = [], []
    for name, size in _REPLICATED:
        operands += [weights[name].reshape(1, size), moments_m[name].reshape(1, size), moments_v[name].reshape(1, size)]
        out_shapes += [jax.ShapeDtypeStruct((1, size), F32)] * 4
    out_shapes.append(jax.ShapeDtypeStruct((1, _LANES), F32))
    outs = pl.pallas_call(body, name="adamw_replicated", out_shape=out_shapes)(landing, *operands)
    return outs[-1][0, 0], {name: outs[4 * a:4 * a + 4] for a, (name, _) in enumerate(_REPLICATED)}


def _pad_rows(a, rows):
    return jnp.pad(a, ((0, rows - a.shape[0]), (0, 0)))


def _slabs(a):
    r, c = a.shape
    return a.reshape(r, N_DEV, c // N_DEV).transpose(1, 0, 2)


def _unslab(a):
    g, r, c = a.shape
    return a.transpose(1, 0, 2).reshape(r, g * c)


def _local_step(x, target, w, n_rows, ffn_weights, send_grads):
    cos_t, sin_t = _rope_tables(n_rows)
    cos, sin = cos_t.T, sin_t.T
    meta_pad, g1, gf = w["meta_pad"], w["mix_norm_g"], w["final_norm_g"]
    gq, gkv, gb_col = w["q_norm_g"], w["kv_norm_g"], w["attn_out_g"].reshape(D_ATTN, 1)
    nb, ag, cq, ckv, kr = _fwd_in(x, meta_pad, g1, w["w_in"], n_rows)
    mix_a, u1 = _fwd_conv(ag, w["conv_w"], w["conv_b"], w["conv_ln_g"], w["conv_ln_b"], w["conv_out_g"], n_rows)
    q_t, k, v, v_t, cqn, ckvn = _fwd_qkv(cq, ckv, kr, gq, gkv, w["wq_t"], w["w_ukv"], w["wv_t"], cos, sin, cos_t, sin_t, n_rows)
    o_t, lse = _attn_fwd(q_t, k, v_t, n_rows)
    w_out, w_up, w_down = ffn_weights()
    mix_bt, h1 = _fwd_out(x, meta_pad, mix_a, o_t, gb_col, w_out, n_rows)
    n2, up0, act, da, db, dh2, loss, dgf = _fwd_ffn(
        h1, target, w["ffn_norm_g"], w_up, w["fw"], w["fb"], w_down, gf, n_rows)

    dup, dfb = _bwd_ffn_act(dh2, da, db, w_down, n_rows)
    dup0, dh1, dfw, dg2 = _bwd_ffn_up(dup, up0, h1, dh2, w["ffn_norm_g"], w_up, w["fw"], n_rows)
    grad_w_out = jnp.concatenate([_weight_grad(mix_a, dh1, "grad_w_out_conv")[0],
                                  _weight_grad(mix_bt, dh1, "grad_w_out_attn", a_transposed=True)[0]], axis=0)
    send_grads(0, {
        "w_ffn_up": _weight_grad(dup0, n2, "grad_w_ffn_up"),
        "w_ffn_down": _weight_grad(act, dh2, "grad_w_ffn_down").reshape(N_DEV, D_FF // N_DEV, D_MODEL),
        "w_out": grad_w_out.reshape(N_DEV, D_MODEL // N_DEV, D_MODEL),
    })
    do_t, delta, du1, dgb, dga, dlg, dlb, dcb = _bwd_out(
        dh1, o_t, u1, w_out, gb_col, w["conv_ln_g"], w["conv_ln_b"], w["conv_out_g"], n_rows)
    dq_t, dk, dv = _attn_bwd(q_t, k, v, do_t, lse, delta, n_rows)
    dqraw_t, dkv, dcq, dckv, dkr, dgq, dgkv = _bwd_qkv(
        dq_t, dk, dv, cq, ckv, gq, gkv, w["wq_t"], w["w_ukv"], cos, sin, cos_t, sin_t, n_rows)
    dz, dcw = _bwd_conv(du1, ag, w["conv_w"], dcq, dckv, dkr, n_rows)
    send_grads(1, {
        "w_in": _weight_grad(dz, nb, "grad_w_in")[0].reshape(N_DEV, D_IN // N_DEV, D_MODEL),
        "w_uq": _weight_grad(dqraw_t.reshape(N_HEADS * QK_DIM, n_rows), cqn, "grad_w_uq", a_transposed=True)[0].reshape(
            N_HEADS, QK_DIM, Q_LORA),
        "w_ukv": _slabs(_weight_grad(ckvn, dkv, "grad_w_ukv")[0]),
        "conv_w": _slabs(dcw),
        "ffn_conv_w": dfw[:, :, :UP_SLAB],
    })
    gx, gmeta, dg1 = _bwd_in(dz, x, meta_pad, dh1, g1, w["w_in"], n_rows)

    sharded = {"meta_tokens": _slabs(gmeta[DEAD:])}
    replicated = {
        "mix_norm_g": dg1, "q_norm_g": dgq, "kv_norm_g": dgkv, "conv_b": dcb, "conv_ln_g": dlg, "conv_ln_b": dlb,
        "conv_out_g": dga, "attn_out_g": dgb, "ffn_norm_g": dg2, "ffn_conv_b": dfb, "final_norm_g": dgf,
    }
    return loss[0, 0], gx, sharded, replicated


_SHARDED = (
    ("w_in", None, BF16), ("w_uq", None, BF16), ("w_ukv", None, BF16), ("w_out", None, BF16), ("w_ffn_up", None, BF16),
    ("w_ffn_down", None, BF16), ("conv_w", 32, F32), ("ffn_conv_w", 8, F32), ("meta_tokens", None, F32),
)
GATHER_LATE_ID = 3
EXCHANGE_STAGE_IDS = (4, 5)
_LATE_WEIGHTS = ("w_out", "w_ffn_up", "w_ffn_down")
_COLUMN_SHARDS = ("w_in", "w_uq", "w_ffn_up")


def kernel(x, meta_tokens, mix_norm_g, w_in, q_norm_g, w_uq, kv_norm_g, w_ukv, conv_w, conv_b, conv_ln_g, conv_ln_b, conv_out_g, attn_out_g, w_out, ffn_norm_g, w_ffn_up, ffn_conv_w, ffn_conv_b, w_ffn_down, final_norm_g, loss_target, m_meta_tokens, m_mix_norm_g, m_w_in, m_q_norm_g, m_w_uq, m_kv_norm_g, m_w_ukv, m_conv_w, m_conv_b, m_conv_ln_g, m_conv_ln_b, m_conv_out_g, m_attn_out_g, m_w_out, m_ffn_norm_g, m_w_ffn_up, m_ffn_conv_w, m_ffn_conv_b, m_w_ffn_down, m_final_norm_g, v_meta_tokens, v_mix_norm_g, v_w_in, v_q_norm_g, v_w_uq, v_kv_norm_g, v_w_ukv, v_conv_w, v_conv_b, v_conv_ln_g, v_conv_ln_b, v_conv_out_g, v_attn_out_g, v_w_out, v_ffn_norm_g, v_w_ffn_up, v_ffn_conv_w, v_ffn_conv_b, v_w_ffn_down, v_final_norm_g):
    given = dict(locals())
    weights = {name: given[name] for name in _WEIGHT_ORDER}
    moments_m = {name: given["m_" + name] for name in _WEIGHT_ORDER}
    moments_v = {name: given["v_" + name] for name in _WEIGHT_ORDER}
    seq = x.shape[1]
    n_rows = ROW_TILE + seq

    def shard2d(name, a):
        a = a.reshape(a.shape[-2], a.shape[-1])
        return a.T if name in _COLUMN_SHARDS else a

    early = [entry for entry in _SHARDED if entry[0] not in _LATE_WEIGHTS]
    shards = []
    for name, pad_to, _ in early:
        s = shard2d(name, weights[name])
        shards.append(s if pad_to is None else _pad_rows(s, pad_to))
    gathered = dict(zip([name for name, _, _ in early], _all_gather(shards, [dt for _, _, dt in early])))
    behind = gathered["meta_tokens"][0, 0, 0] * 0.0
    late_parts = [(shard2d(name, weights[name]) + behind).astype(BF16) for name in _LATE_WEIGHTS]
    late = _sequencer_exchange(late_parts, [True] * len(late_parts), "gather_late", GATHER_LATE_ID)
    meta_full = _unslab(gathered["meta_tokens"])
    full = {
        "meta_pad": jnp.concatenate([jnp.zeros((DEAD, D_MODEL), F32), meta_full], axis=0),
        "w_in": gathered["w_in"].reshape(D_IN, D_MODEL),
        "wq_t": gathered["w_uq"],
        "w_ukv": gathered["w_ukv"],
        "wv_t": gathered["w_ukv"][:, :, QK_NOPE:].transpose(0, 2, 1),
        "conv_w": _unslab(gathered["conv_w"][:, :CONV_WIDTH]),
        "fw": jnp.pad(gathered["ffn_conv_w"][:, :FFN_CONV_WIDTH], ((0, 0), (0, 0), (0, UP_PAD - UP_SLAB))),
        "fb": jnp.pad(ffn_conv_b.reshape(N_DEV, 1, UP_SLAB), ((0, 0), (0, 0), (0, UP_PAD - UP_SLAB))),
        "final_norm_g": final_norm_g.reshape(1, D_MODEL),
    }
    for name in ("mix_norm_g", "q_norm_g", "kv_norm_g", "conv_b", "conv_ln_g", "conv_ln_b", "conv_out_g", "attn_out_g",
                 "ffn_norm_g"):
        full[name] = weights[name]

    def ffn_weights():
        w_out_all, w_up_all, w_down_all = late
        return (w_out_all.reshape(D_MODEL, D_MODEL), w_up_all, w_down_all.reshape(N_ACT_SLAB, UP_SLAB, D_MODEL))

    wire = {name: (pad_to, dt) for name, pad_to, dt in _SHARDED}
    landing = {}

    def on_the_wire(name, slabs):
        pad_to, dt = wire[name]
        slabs = slabs.astype(dt)
        return slabs if pad_to is None else jnp.pad(slabs, ((0, 0), (0, pad_to - slabs.shape[1]), (0, 0)))

    def send_grads(stage, grads):
        parts = [on_the_wire(name, slabs) for name, slabs in grads.items()]
        landed = _sequencer_exchange(parts, [False] * len(parts), f"exchange_stage{stage}", EXCHANGE_STAGE_IDS[stage])
        landing.update(zip(grads, landed))

    loss, gx, sharded, replicated = _local_step(x[0], loss_target[0], full, n_rows, ffn_weights, send_grads)

    parts = [on_the_wire(name, slabs) for name, slabs in sharded.items()] + [_pack_replicated(replicated, loss)]
    landed = _exchange(parts, [False] * len(sharded) + [True])
    landing.update(zip(sharded, landed[:-1]))

    grad, delta, new_m, new_v = {}, {}, {}, {}
    for name, pad_to, _ in _SHARDED:
        land = landing[name]
        ws, ms, vs = (shard2d(name, a[name]) for a in (weights, moments_m, moments_v))
        rows = ws.shape[0]
        if pad_to is not None:
            ws, ms, vs = _pad_rows(ws, pad_to), _pad_rows(ms, pad_to), _pad_rows(vs, pad_to)
        outs = _adamw(land, ws, ms, vs, "adamw_" + name)
        shape = weights[name].shape
        grad[name], delta[name], new_m[name], new_v[name] = (
            (o.T if name in _COLUMN_SHARDS else o[:rows]).reshape(shape) for o in outs)
    loss, updates = _adamw_replicated(landed[-1], weights, moments_m, moments_v)
    for name, outs in updates.items():
        grad[name], delta[name], new_m[name], new_v[name] = (o.reshape(weights[name].shape) for o in outs)

    return (loss, gx[None], *[grad[n] for n in _WEIGHT_ORDER], *[delta[n] for n in _WEIGHT_ORDER],
            *[new_m[n] for n in _WEIGHT_ORDER], *[new_v[n] for n in _WEIGHT_ORDER])
```

```python
import functools

import jax
import jax.numpy as jnp
from jax import lax
from jax.experimental import pallas as pl
from jax.experimental.pallas import tpu as pltpu
from jax.experimental.pallas import tpu_sc as plsc

F32 = jnp.float32
BF16 = jnp.bfloat16

N_DEV = 8
D_MODEL = 1024
CHUNK = 64
CHUNK_SHIFT = 6
N_META = 16
D_CONV = 512
CONV_WIDTH = 31
N_HEADS = 8
QK_NOPE = 64
QK_ROPE = 32
QK_DIM = QK_NOPE + QK_ROPE
V_HEAD = 64
KV_HEAD = QK_NOPE + V_HEAD
D_ATTN = N_HEADS * V_HEAD
Q_LORA = 384
KV_LORA = 256
ROPE_THETA = 10000.0
D_IN = 2 * D_CONV + Q_LORA + KV_LORA + QK_ROPE
D_FF = 2816
D_UP = 2 * D_FF
FFN_CONV_WIDTH = 3
UP_SLAB = D_UP // N_DEV
N_ACT_SLAB = D_FF // UP_SLAB
EPS = 1e-6
NEG = -1e30
_LN2 = 0.6931471805599453
QK_LOGIT_SCALE = QK_DIM ** -0.5 / _LN2
ADAM_LR = 0.001
ADAM_B1 = 0.9
ADAM_B2 = 0.999
ADAM_EPS = 1e-08
ADAM_WD = 0.01
ADAM_STEP = 10

ROW_TILE = 256
DEAD = ROW_TILE - N_META
CONV_HALO = 32
FFN_HALO = 16
VMEM_LIMIT = 56 * 1024 * 1024
_LANES = 128

MESH = pl.DeviceIdType.MESH


def _dot(a, b):
    return jnp.dot(a, b, preferred_element_type=F32)


def _dot_nt(a, b):
    return lax.dot_general(a, b, (((1,), (1,)), ((), ())), preferred_element_type=F32)


def _dot_tn(a, b):
    return lax.dot_general(a, b, (((0,), (0,)), ((), ())), preferred_element_type=F32)


def _sigmoid(x):
    return 1.0 / (1.0 + jnp.exp2(x * (-1.0 / _LN2)))


def _rms_fwd(x, g):
    r = lax.rsqrt(jnp.mean(x * x, axis=-1, keepdims=True) + EPS)
    return x * r * g


def _rms_bwd(dy, x, g):
    r = lax.rsqrt(jnp.mean(x * x, axis=-1, keepdims=True) + EPS)
    w = dy * g
    dx = r * w - x * (r * r * r) * jnp.mean(w * x, axis=-1, keepdims=True)
    return dx, jnp.sum(dy * x * r, axis=0, keepdims=True)


def _rope(x, cos, sin):
    half = QK_ROPE // 2
    x1, x2 = x[:, :half], x[:, half:]
    return jnp.concatenate([x1 * cos - x2 * sin, x2 * cos + x1 * sin], axis=-1)


def _rope_t(dy, cos, sin):
    half = QK_ROPE // 2
    d1, d2 = dy[:, :half], dy[:, half:]
    return jnp.concatenate([d1 * cos + d2 * sin, d2 * cos - d1 * sin], axis=-1)


def _row_ids(i, rows):
    return i * rows + lax.broadcasted_iota(jnp.int32, (rows, 1), 0)


def _accumulate(ref, first, value):
    @pl.when(first)
    def _():
        ref[...] = value

    @pl.when(jnp.logical_not(first))
    def _():
        ref[...] += value


def _tile_spec(shape):
    nd = len(shape)
    if nd == 2:
        return pl.BlockSpec((ROW_TILE, shape[1]), lambda i: (i, 0))
    return pl.BlockSpec((shape[0], ROW_TILE, shape[2]), lambda i: (0, i, 0))


def _whole_spec(shape):
    nd = len(shape)
    return pl.BlockSpec(tuple(shape), lambda i: (0,) * nd, pipeline_mode=pl.Buffered(1))


def _acc_spec(shape):
    nd = len(shape)
    return pl.BlockSpec(tuple(shape), lambda i: (0,) * nd)


def _real_spec(width):
    return pl.BlockSpec((ROW_TILE, width), lambda i: (jnp.maximum(i - 1, 0), 0))


def _params(*semantics):
    return pltpu.CompilerParams(dimension_semantics=semantics, vmem_limit_bytes=VMEM_LIMIT)


def _fwd_in(x, meta_pad, g1, w_in, n_rows):
    nt = n_rows // ROW_TILE

    def body(x_ref, meta_ref, g_ref, w_ref, nb_ref, ag_ref, cq_ref, ckv_ref, kr_ref):
        i = pl.program_id(0)
        h0 = jnp.where(i == 0, meta_ref[...], x_ref[...])
        nb = _rms_fwd(h0, g_ref[...]).astype(BF16)
        nb_ref[...] = nb
        z = _dot_nt(nb, w_ref[...])
        ag_ref[...] = z[:, :2 * D_CONV]
        cq_ref[...] = z[:, 2 * D_CONV:2 * D_CONV + Q_LORA]
        ckv_ref[...] = z[:, 2 * D_CONV + Q_LORA:2 * D_CONV + Q_LORA + KV_LORA]
        kr_ref[...] = z[:, 2 * D_CONV + Q_LORA + KV_LORA:]

    out_shapes = [
        jax.ShapeDtypeStruct((n_rows, D_MODEL), BF16),
        jax.ShapeDtypeStruct((n_rows, 2 * D_CONV), F32),
        jax.ShapeDtypeStruct((n_rows, Q_LORA), F32),
        jax.ShapeDtypeStruct((n_rows, KV_LORA), F32),
        jax.ShapeDtypeStruct((n_rows, QK_ROPE), F32),
    ]
    return pl.pallas_call(
        body, name="fwd_in", grid=(nt,),
        in_specs=[_real_spec(D_MODEL), _whole_spec(meta_pad.shape), _whole_spec(g1.shape), _whole_spec(w_in.shape)],
        out_specs=[_tile_spec(s.shape) for s in out_shapes],
        out_shape=out_shapes,
        compiler_params=_params("parallel"),
    )(x, meta_pad, g1, w_in)


def _conv_chain(u1, ln_g, ln_b):
    mu = jnp.mean(u1, axis=-1, keepdims=True)
    xc = u1 - mu
    rstd = lax.rsqrt(jnp.mean(xc * xc, axis=-1, keepdims=True) + EPS)
    xh = xc * rstd
    u2 = xh * ln_g + ln_b
    return xh, u2, u2 * _sigmoid(u2), rstd


def _fwd_conv(ag, conv_w, conv_b, ln_g, ln_b, out_g, n_rows):
    nt = n_rows // ROW_TILE

    def body(ag_ref, w_ref, b_ref, lg_ref, lb_ref, og_ref, mix_ref, u1_ref, ext_ref, conv_ref):
        i = pl.program_id(0)

        @pl.when(i == 0)
        def _():
            ext_ref[:, 0:CONV_HALO, :] = jnp.zeros((CONV_PLANES, CONV_HALO, _LANES), F32)

        ag_t = ag_ref[...]
        live = _row_ids(i, ROW_TILE) >= DEAD
        u0 = jnp.where(live, ag_t[:, :D_CONV] * _sigmoid(ag_t[:, D_CONV:]), 0.0)
        _to_planes(ext_ref, (), slice(CONV_HALO, None), u0)
        first = CONV_HALO - (CONV_WIDTH - 1)
        for c in range(CONV_PLANES):
            taps = w_ref[:, c * _LANES:(c + 1) * _LANES]
            for p in range(PHASES):
                acc = jnp.zeros((PHASE_ROWS, _LANES), F32)
                for k in range(CONV_WIDTH):
                    acc = acc + taps[k:k + 1, :] * ext_ref[c, _phase(first + k + p), :]
                conv_ref[c, _phase(p), :] = acc
        ext_ref[:, 0:CONV_HALO, :] = ext_ref[:, ROW_TILE:ROW_TILE + CONV_HALO, :]
        u1 = _from_planes(conv_ref, (), D_CONV) + b_ref[...]
        u1_ref[...] = u1
        _, _, u3, _ = _conv_chain(u1, lg_ref[...], lb_ref[...])
        mix_ref[...] = _rms_fwd(u3, og_ref[...]).astype(BF16)

    out_shapes = [jax.ShapeDtypeStruct((n_rows, D_CONV), BF16), jax.ShapeDtypeStruct((n_rows, D_CONV), F32)]
    small = [conv_w, conv_b, ln_g, ln_b, out_g]
    return pl.pallas_call(
        body, name="fwd_conv", grid=(nt,),
        in_specs=[_tile_spec(ag.shape)] + [_whole_spec(a.shape) for a in small],
        out_specs=[_tile_spec(s.shape) for s in out_shapes],
        out_shape=out_shapes,
        scratch_shapes=[pltpu.VMEM((CONV_PLANES, ROW_TILE + CONV_HALO, _LANES), F32),
                        pltpu.VMEM((CONV_PLANES, ROW_TILE, _LANES), F32)],
        compiler_params=_params("arbitrary"),
    )(ag, *small)


def _lane_tile(shape):
    if len(shape) == 2:
        return pl.BlockSpec((shape[0], ROW_TILE), lambda i: (0, i))
    return pl.BlockSpec((shape[0], shape[1], ROW_TILE), lambda i: (0, 0, i))


def _rope_rows(x, cos, sin):
    half = QK_ROPE // 2
    x1, x2 = x[:half], x[half:]
    return jnp.concatenate([x1 * cos - x2 * sin, x2 * cos + x1 * sin], axis=0)


def _rope_rows_t(dy, cos, sin):
    half = QK_ROPE // 2
    d1, d2 = dy[:half], dy[half:]
    return jnp.concatenate([d1 * cos + d2 * sin, d2 * cos - d1 * sin], axis=0)


def _fwd_qkv(cq, ckv, kr, gq, gkv, wq_t, w_ukv, wv_t, cos, sin, cos_t, sin_t, n_rows):
    nt = n_rows // ROW_TILE

    def body(cq_ref, ckv_ref, kr_ref, gq_ref, gkv_ref, wqt_ref, wkv_ref, wvt_ref, cos_ref, sin_ref, cost_ref, sint_ref,
             qt_ref, k_ref, v_ref, vt_ref, cqn_ref, ckvn_ref):
        cqn = _rms_fwd(cq_ref[...], gq_ref[...]).astype(BF16)
        ckvn = _rms_fwd(ckv_ref[...], gkv_ref[...]).astype(BF16)
        cqn_ref[...] = cqn
        ckvn_ref[...] = ckvn
        k_rot = _rope(kr_ref[...], cos_ref[...], sin_ref[...])
        cos_rows, sin_rows = cost_ref[...], sint_ref[...]
        for h in range(N_HEADS):
            q_raw = _dot_nt(wqt_ref[h], cqn)
            q_h = jnp.concatenate([q_raw[:QK_NOPE], _rope_rows(q_raw[QK_NOPE:], cos_rows, sin_rows)], axis=0)
            qt_ref[h] = (q_h * QK_LOGIT_SCALE).astype(BF16)
            kv = _dot(ckvn, wkv_ref[h])
            k_ref[h] = jnp.concatenate([kv[:, :QK_NOPE], k_rot], axis=-1).astype(BF16)
            v_ref[h] = kv[:, QK_NOPE:].astype(BF16)
            vt_ref[h] = _dot_nt(wvt_ref[h], ckvn).astype(BF16)

    out_shapes = [
        jax.ShapeDtypeStruct((N_HEADS, QK_DIM, n_rows), BF16),
        jax.ShapeDtypeStruct((N_HEADS, n_rows, QK_DIM), BF16),
        jax.ShapeDtypeStruct((N_HEADS, n_rows, V_HEAD), BF16),
        jax.ShapeDtypeStruct((N_HEADS, V_HEAD, n_rows), BF16),
        jax.ShapeDtypeStruct((n_rows, Q_LORA), BF16),
        jax.ShapeDtypeStruct((n_rows, KV_LORA), BF16),
    ]
    tiles = [cq, ckv, kr]
    whole = [gq, gkv, wq_t, w_ukv, wv_t]
    out_specs = [_lane_tile(out_shapes[0].shape), _tile_spec(out_shapes[1].shape), _tile_spec(out_shapes[2].shape),
                 _lane_tile(out_shapes[3].shape), _tile_spec(out_shapes[4].shape), _tile_spec(out_shapes[5].shape)]
    return pl.pallas_call(
        body, name="fwd_qkv", grid=(nt,),
        in_specs=[_tile_spec(a.shape) for a in tiles] + [_whole_spec(a.shape) for a in whole]
        + [_tile_spec(cos.shape), _tile_spec(sin.shape), _lane_tile(cos_t.shape), _lane_tile(sin_t.shape)],
        out_specs=out_specs,
        out_shape=out_shapes,
        compiler_params=_params("parallel"),
    )(*tiles, *whole, cos, sin, cos_t, sin_t)


def _chunk_of(rows):
    return jnp.where(rows >= ROW_TILE, lax.shift_right_arithmetic(rows - ROW_TILE, CHUNK_SHIFT) + 1, 0)


def _visible(i, j):
    k_rows = j * ROW_TILE + lax.broadcasted_iota(jnp.int32, (ROW_TILE, 1), 0)
    q_rows = i * ROW_TILE + lax.broadcasted_iota(jnp.int32, (1, ROW_TILE), 1)
    return jnp.logical_and(_chunk_of(q_rows) >= _chunk_of(k_rows), k_rows >= DEAD)


def _attn_fwd(q_t, k, v_t, n_rows):
    nt = n_rows // ROW_TILE

    def body(qt_ref, k_ref, vt_ref, ot_ref, lse_ref):
        i = pl.program_id(0)
        q_ts = [qt_ref[h] for h in range(N_HEADS)]

        def make_step(masked):
            def step(j, carry):
                rows = pl.ds(pl.multiple_of(j * ROW_TILE, ROW_TILE), ROW_TILE)
                scores = [_dot(k_ref[h, rows, :], q_ts[h]) for h in range(N_HEADS)]
                visible = _visible(i, j) if masked else None
                probs, state = [], []
                for h in range(N_HEADS):
                    m, l, _ = carry[h]
                    s = jnp.where(visible, scores[h], NEG) if masked else scores[h]
                    m_new = jnp.maximum(m, jnp.max(s, axis=0, keepdims=True))
                    alpha = jnp.exp2(m - m_new)
                    p = jnp.exp2(s - m_new)
                    probs.append(p.astype(BF16))
                    state.append((m_new, alpha * l + jnp.sum(p, axis=0, keepdims=True), alpha))
                outs = [_dot(vt_ref[h, :, rows], probs[h]) for h in range(N_HEADS)]
                return tuple((state[h][0], state[h][1], state[h][2] * carry[h][2] + outs[h]) for h in range(N_HEADS))
            return step

        init = tuple((jnp.full((1, ROW_TILE), NEG, F32), jnp.zeros((1, ROW_TILE), F32),
                      jnp.zeros((V_HEAD, ROW_TILE), F32)) for _ in range(N_HEADS))
        carry = make_step(True)(0, init)
        carry = lax.fori_loop(1, i, make_step(False), carry)
        carry = lax.fori_loop(jnp.maximum(i, 1), i + 1, make_step(True), carry)
        for h in range(N_HEADS):
            m, l, acc = carry[h]
            ot_ref[h] = acc / l
            lse_ref[h] = m + jnp.log2(l)

    out_shapes = [jax.ShapeDtypeStruct((N_HEADS, V_HEAD, n_rows), F32), jax.ShapeDtypeStruct((N_HEADS, 1, n_rows), F32)]
    return pl.pallas_call(
        body, name="attn_fwd", grid=(nt,),
        in_specs=[_lane_tile(q_t.shape), _whole_spec(k.shape), _whole_spec(v_t.shape)],
        out_specs=[_lane_tile(s.shape) for s in out_shapes],
        out_shape=out_shapes,
        compiler_params=_params("parallel"),
    )(q_t, k, v_t)


def _heads_to_rows(ref):
    return jnp.concatenate([ref[h] for h in range(N_HEADS)], axis=0)


def _rms_cols(x, g_col):
    r = lax.rsqrt(jnp.mean(x * x, axis=0, keepdims=True) + EPS)
    return x * r * g_col


def _fwd_out(x, meta_pad, mix_a, o_t, gb_col, w_out, n_rows):
    nt = n_rows // ROW_TILE

    def body(x_ref, meta_ref, mixa_ref, ot_ref, gb_ref, w_ref, mixbt_ref, h1_ref):
        i = pl.program_id(0)
        h0 = jnp.where(i == 0, meta_ref[...], x_ref[...])
        mix_bt = _rms_cols(_heads_to_rows(ot_ref), gb_ref[...]).astype(BF16)
        mixbt_ref[...] = mix_bt
        h1_ref[...] = h0 + _dot(mixa_ref[...], w_ref[:D_CONV, :]) + _dot_tn(mix_bt, w_ref[D_CONV:, :])

    out_shapes = [jax.ShapeDtypeStruct((D_ATTN, n_rows), BF16), jax.ShapeDtypeStruct((n_rows, D_MODEL), F32)]
    return pl.pallas_call(
        body, name="fwd_out", grid=(nt,),
        in_specs=[_real_spec(D_MODEL), _whole_spec(meta_pad.shape), _tile_spec(mix_a.shape), _lane_tile(o_t.shape),
                  _whole_spec(gb_col.shape), _whole_spec(w_out.shape)],
        out_specs=[_lane_tile(out_shapes[0].shape), _tile_spec(out_shapes[1].shape)],
        out_shape=out_shapes,
        compiler_params=_params("parallel"),
    )(x, meta_pad, mix_a, o_t, gb_col, w_out)


PHASES = 8
PHASE_ROWS = ROW_TILE // PHASES
UP_PLANES = -(-UP_SLAB // _LANES)
UP_PAD = UP_PLANES * _LANES
CONV_PLANES = D_CONV // _LANES


def _phase(start):
    return pl.ds(start, PHASE_ROWS, stride=PHASES)


def _to_planes(ref, lead, rows, value):
    width = value.shape[-1]
    for c in range(-(-width // _LANES)):
        part = value[:, c * _LANES:min((c + 1) * _LANES, width)]
        if part.shape[-1] < _LANES:
            part = jnp.concatenate([part, jnp.zeros((part.shape[0], _LANES - part.shape[-1]), part.dtype)], axis=-1)
        ref[(*lead, c, rows, slice(None))] = part


def _from_planes(ref, lead, width):
    planes = [ref[(*lead, c)] for c in range(-(-width // _LANES))]
    last = width - (len(planes) - 1) * _LANES
    return jnp.concatenate(planes[:-1] + [planes[-1][:, :last]], axis=-1)


def _fwd_ffn(h1, target, g2, w_up, fw, fb, w_down, gf, n_rows):
    nt = n_rows // ROW_TILE

    def body(h1_ref, t_ref, g2_ref, wup_ref, fw_ref, fb_ref, wdn_ref, gf_ref,
             n2_ref, up0_ref, act_ref, da_ref, db_ref, dh2_ref, loss_ref, dgf_ref, ext_ref):
        i = pl.program_id(0)

        @pl.when(i == 0)
        def _():
            ext_ref[:, 0:FFN_HALO, :] = jnp.zeros((N_DEV, FFN_HALO, UP_SLAB), F32)

        h1_t = h1_ref[...]
        live = _row_ids(i, ROW_TILE) >= DEAD
        n2 = jnp.where(live, _rms_fwd(h1_t, g2_ref[...]), 0.0).astype(BF16)
        n2_ref[...] = n2
        for s in range(N_DEV):
            up0 = _dot_nt(n2, wup_ref[s])
            up0_ref[s] = up0.astype(BF16)
            ext_ref[s, FFN_HALO:, :] = up0
        first = FFN_HALO - (FFN_CONV_WIDTH - 1)

        def conv(s):
            acc = fb_ref[s, :, :UP_SLAB]
            for k in range(FFN_CONV_WIDTH):
                acc = acc + fw_ref[s, k:k + 1, :UP_SLAB] * ext_ref[s, first + k:first + k + ROW_TILE, :]
            return acc

        h2 = h1_t
        for s in range(N_ACT_SLAB):
            gate = conv(s)
            val = conv(s + N_ACT_SLAB)
            sg = _sigmoid(gate)
            silu = gate * sg
            act = (silu * val).astype(BF16)
            act_ref[s] = act
            da_ref[s] = (val * sg * (1.0 + gate * (1.0 - sg))).astype(BF16)
            db_ref[s] = silu.astype(BF16)
            h2 = h2 + _dot(act, wdn_ref[s])
        ext_ref[:, 0:FFN_HALO, :] = ext_ref[:, ROW_TILE:ROW_TILE + FFN_HALO, :]

        gf_t = gf_ref[...]
        y = _rms_fwd(h2, gf_t)
        diff = jnp.where(i >= 1, y - t_ref[...], 0.0)
        tile_loss = 0.5 * jnp.sum(jnp.sum(diff * diff, axis=-1, keepdims=True), axis=0, keepdims=True) / D_MODEL
        dh2, dgf = _rms_bwd(diff / D_MODEL, h2, gf_t)
        dh2_ref[...] = dh2
        _accumulate(loss_ref, i == 0, jnp.broadcast_to(tile_loss, loss_ref.shape))
        _accumulate(dgf_ref, i == 0, dgf)

    act_like = jax.ShapeDtypeStruct((N_ACT_SLAB, n_rows, UP_SLAB), BF16)
    out_shapes = [
        jax.ShapeDtypeStruct((n_rows, D_MODEL), BF16),
        jax.ShapeDtypeStruct((N_DEV, n_rows, UP_SLAB), BF16),
        act_like, act_like, act_like,
        jax.ShapeDtypeStruct((n_rows, D_MODEL), F32),
        jax.ShapeDtypeStruct((8, 128), F32),
        jax.ShapeDtypeStruct((1, D_MODEL), F32),
    ]
    whole = [g2, w_up, fw, fb, w_down, gf]
    return pl.pallas_call(
        body, name="fwd_ffn", grid=(nt,),
        in_specs=[_tile_spec(h1.shape), _real_spec(D_MODEL)] + [_whole_spec(a.shape) for a in whole],
        out_specs=[_tile_spec(s.shape) for s in out_shapes[:6]] + [_acc_spec(s.shape) for s in out_shapes[6:]],
        out_shape=out_shapes,
        scratch_shapes=[pltpu.VMEM((N_DEV, ROW_TILE + FFN_HALO, UP_SLAB), F32)],
        compiler_params=_params("arbitrary"),
    )(h1, target, *whole)


def _rope_tables(n_rows):
    pos = jnp.maximum(jnp.arange(n_rows, dtype=jnp.int32) - DEAD, 0)
    inv_freq = 1.0 / (ROPE_THETA ** (jnp.arange(0, QK_ROPE, 2, dtype=F32) / QK_ROPE))
    ang_t = inv_freq[:, None] * pos.astype(F32)[None, :]
    return jnp.cos(ang_t), jnp.sin(ang_t)


def _halo_after(shape, halo, n_rows):
    last = n_rows // halo - 1
    step = ROW_TILE // halo
    if len(shape) == 2:
        return pl.BlockSpec((halo, shape[1]), lambda i: (jnp.minimum((i + 1) * step, last), 0))
    return pl.BlockSpec((shape[0], halo, shape[2]), lambda i: (0, jnp.minimum((i + 1) * step, last), 0))


def _halo_before(shape, halo):
    step = ROW_TILE // halo
    if len(shape) == 2:
        return pl.BlockSpec((halo, shape[1]), lambda i: (jnp.maximum(i * step - 1, 0), 0))
    return pl.BlockSpec((shape[0], halo, shape[2]), lambda i: (0, jnp.maximum(i * step - 1, 0), 0))


def _bwd_ffn_act(dh2, da, db, w_down, n_rows):
    nt = n_rows // ROW_TILE

    def body(dh2_ref, da_ref, db_ref, wdn_ref, dup_ref, dfb_ref):
        i = pl.program_id(0)

        @pl.when(i == 0)
        def _():
            dfb_ref[...] = jnp.zeros_like(dfb_ref)

        dh2_b = dh2_ref[...].astype(BF16)
        for s in range(N_ACT_SLAB):
            d_act = _dot_nt(dh2_b, wdn_ref[s])
            d_gate = d_act * da_ref[s].astype(F32)
            d_val = d_act * db_ref[s].astype(F32)
            dup_ref[s] = d_gate.astype(BF16)
            dup_ref[s + N_ACT_SLAB] = d_val.astype(BF16)
            dfb_ref[s] += jnp.sum(d_gate, axis=0, keepdims=True)
            dfb_ref[s + N_ACT_SLAB] += jnp.sum(d_val, axis=0, keepdims=True)

    out_shapes = [jax.ShapeDtypeStruct((N_DEV, n_rows, UP_SLAB), BF16), jax.ShapeDtypeStruct((N_DEV, 1, UP_SLAB), F32)]
    return pl.pallas_call(
        body, name="bwd_ffn_act", grid=(nt,),
        in_specs=[_tile_spec(dh2.shape), _tile_spec(da.shape), _tile_spec(db.shape), _whole_spec(w_down.shape)],
        out_specs=[_tile_spec(out_shapes[0].shape), _acc_spec(out_shapes[1].shape)],
        out_shape=out_shapes,
        compiler_params=_params("arbitrary"),
    )(dh2, da, db, w_down)


def _bwd_ffn_up(dup, up0, h1, dh2, g2, w_up, fw, n_rows):
    nt = n_rows // ROW_TILE
    last_tap = FFN_CONV_WIDTH - 1

    def body(dup_ref, dnext_ref, up0_ref, h1_ref, dh2_ref, g2_ref, wup_ref, fw_ref,
             dup0_ref, dh1_ref, dfw_ref, dg2_ref, dext_ref, uext_ref, conv_ref):
        i = pl.program_id(0)

        @pl.when(i == 0)
        def _():
            dfw_ref[...] = jnp.zeros_like(dfw_ref)

        live = _row_ids(i, ROW_TILE) >= DEAD
        dn2 = jnp.zeros((ROW_TILE, D_MODEL), F32)
        for s in range(N_DEV):
            _to_planes(dext_ref, (), slice(0, ROW_TILE), dup_ref[s].astype(F32))
            _to_planes(dext_ref, (), slice(ROW_TILE, None), jnp.where(i == nt - 1, 0.0, dnext_ref[s].astype(F32)))
            _to_planes(uext_ref, (), slice(None), up0_ref[s].astype(F32))
            for c in range(UP_PLANES):
                lanes = slice(c * _LANES, (c + 1) * _LANES)
                taps = [fw_ref[s, k:k + 1, lanes] for k in range(FFN_CONV_WIDTH)]
                sums = [jnp.zeros((PHASE_ROWS, _LANES), F32) for _ in range(FFN_CONV_WIDTH)]
                for p in range(PHASES):
                    u = uext_ref[c, _phase(p), :]
                    acc = jnp.zeros((PHASE_ROWS, _LANES), F32)
                    for k in range(FFN_CONV_WIDTH):
                        shifted = dext_ref[c, _phase(p + last_tap - k), :]
                        acc = acc + taps[k] * shifted
                        sums[k] = sums[k] + shifted * u
                    conv_ref[c, _phase(p), :] = acc
                for k in range(FFN_CONV_WIDTH):
                    dfw_ref[s, k:k + 1, lanes] += jnp.sum(sums[k], axis=0, keepdims=True)
            dup0_b = jnp.where(live, _from_planes(conv_ref, (), UP_SLAB), 0.0).astype(BF16)
            dup0_ref[s] = dup0_b
            dn2 = dn2 + _dot(dup0_b, wup_ref[s])
        dx, dg2 = _rms_bwd(dn2, h1_ref[...], g2_ref[...])
        dh1_ref[...] = dh2_ref[...] + dx
        _accumulate(dg2_ref, i == 0, dg2)

    out_shapes = [
        jax.ShapeDtypeStruct((N_DEV, n_rows, UP_SLAB), BF16),
        jax.ShapeDtypeStruct((n_rows, D_MODEL), F32),
        jax.ShapeDtypeStruct((N_DEV, FFN_CONV_WIDTH, UP_PAD), F32),
        jax.ShapeDtypeStruct((1, D_MODEL), F32),
    ]
    return pl.pallas_call(
        body, name="bwd_ffn_up", grid=(nt,),
        in_specs=[_tile_spec(dup.shape), _halo_after(dup.shape, FFN_HALO, n_rows), _tile_spec(up0.shape),
                  _tile_spec(h1.shape), _tile_spec(dh2.shape),
                  _whole_spec(g2.shape), _whole_spec(w_up.shape), _whole_spec(fw.shape)],
        out_specs=[_tile_spec(s.shape) for s in out_shapes[:2]] + [_acc_spec(s.shape) for s in out_shapes[2:]],
        out_shape=out_shapes,
        scratch_shapes=[pltpu.VMEM((UP_PLANES, ROW_TILE + FFN_HALO, _LANES), F32),
                        pltpu.VMEM((UP_PLANES, ROW_TILE, _LANES), F32), pltpu.VMEM((UP_PLANES, ROW_TILE, _LANES), F32)],
        compiler_params=_params("arbitrary"),
    )(dup, dup, up0, h1, dh2, g2, w_up, fw)


def _bwd_out(dh1, o_t, u1, w_out, gb_col, ln_g, ln_b, ga, n_rows):
    nt = n_rows // ROW_TILE

    def body(dh1_ref, ot_ref, u1_ref, w_ref, gb_ref, lg_ref, lb_ref, ga_ref,
             dot_ref, delta_ref, du1_ref, dgb_ref, dga_ref, dlg_ref, dlb_ref, dcb_ref):
        i = pl.program_id(0)
        dh1_b = dh1_ref[...].astype(BF16)
        o_t = _heads_to_rows(ot_ref)
        gb = gb_ref[...]
        r = lax.rsqrt(jnp.mean(o_t * o_t, axis=0, keepdims=True) + EPS)
        dmix_bt = _dot_nt(w_ref[D_CONV:, :], dh1_b)
        wgt = dmix_bt * gb
        do_t = r * wgt - o_t * (r * r * r) * jnp.mean(wgt * o_t, axis=0, keepdims=True)
        dgb = jnp.sum(dmix_bt * o_t * r, axis=1, keepdims=True)
        for h in range(N_HEADS):
            do_h = do_t[h * V_HEAD:(h + 1) * V_HEAD]
            dot_ref[h] = do_h.astype(BF16)
            delta_ref[h] = jnp.sum(do_h * ot_ref[h], axis=0, keepdims=True)
        lg = lg_ref[...]
        xh, u2, u3, rstd = _conv_chain(u1_ref[...], lg, lb_ref[...])
        du3, dga = _rms_bwd(_dot_nt(dh1_b, w_ref[:D_CONV, :]), u3, ga_ref[...])
        sg = _sigmoid(u2)
        du2 = du3 * sg * (1.0 + u2 * (1.0 - sg))
        dxh = du2 * lg
        du1 = rstd * (dxh - jnp.mean(dxh, axis=-1, keepdims=True) - xh * jnp.mean(dxh * xh, axis=-1, keepdims=True))
        du1_ref[...] = du1
        first = i == 0
        _accumulate(dgb_ref, first, dgb)
        _accumulate(dga_ref, first, dga)
        _accumulate(dlg_ref, first, jnp.sum(du2 * xh, axis=0, keepdims=True))
        _accumulate(dlb_ref, first, jnp.sum(du2, axis=0, keepdims=True))
        _accumulate(dcb_ref, first, jnp.sum(du1, axis=0, keepdims=True))

    out_shapes = [
        jax.ShapeDtypeStruct((N_HEADS, V_HEAD, n_rows), BF16),
        jax.ShapeDtypeStruct((N_HEADS, 1, n_rows), F32),
        jax.ShapeDtypeStruct((n_rows, D_CONV), F32),
        jax.ShapeDtypeStruct((D_ATTN, 1), F32),
    ] + [jax.ShapeDtypeStruct((1, D_CONV), F32)] * 4
    whole = [w_out, gb_col, ln_g, ln_b, ga]
    return pl.pallas_call(
        body, name="bwd_out", grid=(nt,),
        in_specs=[_tile_spec(dh1.shape), _lane_tile(o_t.shape), _tile_spec(u1.shape)] + [_whole_spec(a.shape) for a in whole],
        out_specs=[_lane_tile(out_shapes[0].shape), _lane_tile(out_shapes[1].shape), _tile_spec(out_shapes[2].shape)]
        + [_acc_spec(s.shape) for s in out_shapes[3:]],
        out_shape=out_shapes,
        compiler_params=_params("arbitrary"),
    )(dh1, o_t, u1, *whole)


ATTN_BWD_HEADS = 4


def _attn_bwd(q_t, k, v, do_t, lse, delta, n_rows):
    nt = n_rows // ROW_TILE
    hp = ATTN_BWD_HEADS

    def body(k_ref, v_ref, qt_ref, dot_ref, lse_ref, delta_ref, dqt_ref, dk_ref, dv_ref):
        j = pl.program_id(1)

        @pl.when(j == 0)
        def _():
            dqt_ref[...] = jnp.zeros_like(dqt_ref)

        k_ts = [k_ref[h] for h in range(hp)]
        v_ts = [v_ref[h] for h in range(hp)]

        def make_step(masked):
            def step(i, carry):
                cols = pl.ds(pl.multiple_of(i * ROW_TILE, ROW_TILE), ROW_TILE)
                q_is = [qt_ref[h, :, cols] for h in range(hp)]
                do_is = [dot_ref[h, :, cols] for h in range(hp)]
                scores = [_dot(k_ts[h], q_is[h]) for h in range(hp)]
                dps = [_dot(v_ts[h], do_is[h]) for h in range(hp)]
                visible = _visible(i, j) if masked else None
                probs, dss = [], []
                for h in range(hp):
                    s = jnp.where(visible, scores[h], NEG) if masked else scores[h]
                    p = jnp.exp2(s - lse_ref[h, :, cols])
                    probs.append(p.astype(BF16))
                    dss.append((p * (dps[h] - delta_ref[h, :, cols])).astype(BF16))
                out = []
                for h in range(hp):
                    dk, dv = carry[h]
                    dv = dv + _dot_nt(probs[h], do_is[h])
                    dk = dk + _dot_nt(dss[h], q_is[h])
                    dqt_ref[h, :, cols] += _dot_tn(k_ts[h], dss[h])
                    out.append((dk, dv))
                return tuple(out)
            return step

        init = tuple((jnp.zeros((ROW_TILE, QK_DIM), F32), jnp.zeros((ROW_TILE, V_HEAD), F32)) for _ in range(hp))
        carry = make_step(True)(j, init)
        carry = lax.fori_loop(jnp.where(j == 0, j + 1, nt), nt, make_step(True), carry)
        carry = lax.fori_loop(jnp.where(j == 0, nt, j + 1), nt, make_step(False), carry)
        for h in range(hp):
            dk_ref[h] = carry[h][0] * _LN2
            dv_ref[h] = carry[h][1]

    key_tile = lambda w: pl.BlockSpec((hp, ROW_TILE, w), lambda g, j: (g, j, 0))
    all_cols = lambda w: pl.BlockSpec((hp, w, n_rows), lambda g, j: (g, 0, 0))
    out_shapes = [
        jax.ShapeDtypeStruct((N_HEADS, QK_DIM, n_rows), F32),
        jax.ShapeDtypeStruct((N_HEADS, n_rows, QK_DIM), F32),
        jax.ShapeDtypeStruct((N_HEADS, n_rows, V_HEAD), F32),
    ]
    return pl.pallas_call(
        body, name="attn_bwd", grid=(N_HEADS // hp, nt),
        in_specs=[key_tile(QK_DIM), key_tile(V_HEAD), all_cols(QK_DIM), all_cols(V_HEAD), all_cols(1), all_cols(1)],
        out_specs=[all_cols(QK_DIM), key_tile(QK_DIM), key_tile(V_HEAD)],
        out_shape=out_shapes,
        compiler_params=_params("parallel", "arbitrary"),
    )(k, v, q_t, do_t, lse, delta)


def _bwd_qkv(dq_t, dk, dv, cq, ckv, gq, gkv, wq_t, w_ukv, cos, sin, cos_t, sin_t, n_rows):
    nt = n_rows // ROW_TILE

    def body(dqt_ref, dk_ref, dv_ref, cq_ref, ckv_ref, gq_ref, gkv_ref, wqt_ref, wkv_ref, cos_ref, sin_ref,
             cost_ref, sint_ref, dqraw_ref, dkv_ref, dcq_ref, dckv_ref, dkr_ref, dgq_ref, dgkv_ref):
        i = pl.program_id(0)
        cos_rows, sin_rows = cost_ref[...], sint_ref[...]
        dcqn = jnp.zeros((ROW_TILE, Q_LORA), F32)
        dckvn = jnp.zeros((ROW_TILE, KV_LORA), F32)
        dk_rot = jnp.zeros((ROW_TILE, QK_ROPE), F32)
        for h in range(N_HEADS):
            dq_h, dk_h = dqt_ref[h] * QK_DIM ** -0.5, dk_ref[h]
            dq_raw = jnp.concatenate(
                [dq_h[:QK_NOPE], _rope_rows_t(dq_h[QK_NOPE:], cos_rows, sin_rows)], axis=0).astype(BF16)
            dqraw_ref[h] = dq_raw
            dcqn = dcqn + _dot_tn(dq_raw, wqt_ref[h])
            dkv = jnp.concatenate([dk_h[:, :QK_NOPE], dv_ref[h]], axis=-1).astype(BF16)
            dkv_ref[:, h * KV_HEAD:(h + 1) * KV_HEAD] = dkv
            dckvn = dckvn + _dot_nt(dkv, wkv_ref[h])
            dk_rot = dk_rot + dk_h[:, QK_NOPE:]
        dkr_ref[...] = _rope_t(dk_rot, cos_ref[...], sin_ref[...]).astype(BF16)
        dcq, dgq = _rms_bwd(dcqn, cq_ref[...], gq_ref[...])
        dckv, dgkv = _rms_bwd(dckvn, ckv_ref[...], gkv_ref[...])
        dcq_ref[...] = dcq.astype(BF16)
        dckv_ref[...] = dckv.astype(BF16)
        _accumulate(dgq_ref, i == 0, dgq)
        _accumulate(dgkv_ref, i == 0, dgkv)

    out_shapes = [
        jax.ShapeDtypeStruct((N_HEADS, QK_DIM, n_rows), BF16),
        jax.ShapeDtypeStruct((n_rows, N_HEADS * KV_HEAD), BF16),
        jax.ShapeDtypeStruct((n_rows, Q_LORA), BF16),
        jax.ShapeDtypeStruct((n_rows, KV_LORA), BF16),
        jax.ShapeDtypeStruct((n_rows, QK_ROPE), BF16),
        jax.ShapeDtypeStruct((1, Q_LORA), F32),
        jax.ShapeDtypeStruct((1, KV_LORA), F32),
    ]
    tiles = [dk, dv, cq, ckv]
    whole = [gq, gkv, wq_t, w_ukv]
    return pl.pallas_call(
        body, name="bwd_qkv", grid=(nt,),
        in_specs=[_lane_tile(dq_t.shape)] + [_tile_spec(a.shape) for a in tiles] + [_whole_spec(a.shape) for a in whole]
        + [_tile_spec(cos.shape), _tile_spec(sin.shape), _lane_tile(cos_t.shape), _lane_tile(sin_t.shape)],
        out_specs=[_lane_tile(out_shapes[0].shape)] + [_tile_spec(s.shape) for s in out_shapes[1:5]]
        + [_acc_spec(s.shape) for s in out_shapes[5:]],
        out_shape=out_shapes,
        compiler_params=_params("arbitrary"),
    )(dq_t, *tiles, *whole, cos, sin, cos_t, sin_t)


def _bwd_conv(du1, ag, conv_w, dcq, dckv, dkr, n_rows):
    nt = n_rows // ROW_TILE

    last_tap = CONV_WIDTH - 1

    def body(du1_ref, dnext_ref, ag_ref, w_ref, dcq_ref, dckv_ref, dkr_ref, dz_ref, dw_ref,
             dext_ref, uext_ref, conv_ref, sums_ref):
        i = pl.program_id(0)

        @pl.when(i == 0)
        def _():
            sums_ref[...] = jnp.zeros_like(sums_ref)

        _to_planes(dext_ref, (), slice(0, ROW_TILE), du1_ref[...])
        _to_planes(dext_ref, (), slice(ROW_TILE, None), jnp.where(i == nt - 1, 0.0, dnext_ref[...]))
        ag_t = ag_ref[...]
        live = _row_ids(i, ROW_TILE) >= DEAD
        sg = _sigmoid(ag_t[:, D_CONV:])
        _to_planes(uext_ref, (), slice(None), jnp.where(live, ag_t[:, :D_CONV] * sg, 0.0))
        for c in range(CONV_PLANES):
            taps = w_ref[:, c * _LANES:(c + 1) * _LANES]
            for p in range(PHASES):
                u = uext_ref[c, _phase(p), :]
                acc = jnp.zeros((PHASE_ROWS, _LANES), F32)
                for k in range(CONV_WIDTH):
                    shifted = dext_ref[c, _phase(p + last_tap - k), :]
                    acc = acc + taps[k:k + 1, :] * shifted
                    sums_ref[c, k] += shifted * u
                conv_ref[c, _phase(p), :] = acc
        du0 = jnp.where(live, _from_planes(conv_ref, (), D_CONV), 0.0)
        da = du0 * sg
        dgate = du0 * ag_t[:, :D_CONV] * sg * (1.0 - sg)
        dz_ref[...] = jnp.concatenate(
            [da.astype(BF16), dgate.astype(BF16), dcq_ref[...], dckv_ref[...], dkr_ref[...]], axis=-1)

        @pl.when(i == nt - 1)
        def _():
            for c in range(CONV_PLANES):
                for k in range(CONV_WIDTH):
                    dw_ref[k:k + 1, c * _LANES:(c + 1) * _LANES] = jnp.sum(sums_ref[c, k], axis=0, keepdims=True)

    out_shapes = [jax.ShapeDtypeStruct((n_rows, D_IN), BF16), jax.ShapeDtypeStruct((CONV_WIDTH, D_CONV), F32)]
    return pl.pallas_call(
        body, name="bwd_conv", grid=(nt,),
        in_specs=[_tile_spec(du1.shape), _halo_after(du1.shape, CONV_HALO, n_rows), _tile_spec(ag.shape),
                  _whole_spec(conv_w.shape), _tile_spec(dcq.shape), _tile_spec(dckv.shape), _tile_spec(dkr.shape)],
        out_specs=[_tile_spec(out_shapes[0].shape), _acc_spec(out_shapes[1].shape)],
        out_shape=out_shapes,
        scratch_shapes=[pltpu.VMEM((CONV_PLANES, ROW_TILE + CONV_HALO, _LANES), F32),
                        pltpu.VMEM((CONV_PLANES, ROW_TILE, _LANES), F32), pltpu.VMEM((CONV_PLANES, ROW_TILE, _LANES), F32),
                        pltpu.VMEM((CONV_PLANES, CONV_WIDTH, PHASE_ROWS, _LANES), F32)],
        compiler_params=_params("arbitrary"),
    )(du1, du1, ag, conv_w, dcq, dckv, dkr)


def _bwd_in(dz, x, meta_pad, dh1, g1, w_in, n_rows):
    nt = n_rows // ROW_TILE

    def body(dz_ref, x_ref, meta_ref, dh1_ref, g_ref, w_ref, gx_ref, gmeta_ref, dg1_ref):
        i = pl.program_id(0)
        h0 = jnp.where(i == 0, meta_ref[...], x_ref[...])
        dx, dg1 = _rms_bwd(_dot(dz_ref[...], w_ref[...]), h0, g_ref[...])
        dh0 = dh1_ref[...] + dx
        gx_ref[...] = dh0

        @pl.when(i == 0)
        def _():
            gmeta_ref[...] = dh0

        _accumulate(dg1_ref, i == 0, dg1)

    out_shapes = [
        jax.ShapeDtypeStruct((n_rows - ROW_TILE, D_MODEL), F32),
        jax.ShapeDtypeStruct((ROW_TILE, D_MODEL), F32),
        jax.ShapeDtypeStruct((1, D_MODEL), F32),
    ]
    return pl.pallas_call(
        body, name="bwd_in", grid=(nt,),
        in_specs=[_tile_spec(dz.shape), _real_spec(D_MODEL), _whole_spec(meta_pad.shape), _tile_spec(dh1.shape),
                  _whole_spec(g1.shape), _whole_spec(w_in.shape)],
        out_specs=[_real_spec(D_MODEL), _acc_spec(out_shapes[1].shape), _acc_spec(out_shapes[2].shape)],
        out_shape=out_shapes,
        compiler_params=_params("arbitrary"),
    )(dz, x, meta_pad, dh1, g1, w_in)


def _contraction_tile(n_rows):
    return next(t for t in range(n_rows // 2 // _LANES * _LANES, 0, -_LANES) if n_rows % t == 0)


def _weight_grad(a, b, name, a_transposed=False):
    groups = max(a.shape[0] if a.ndim == 3 else 1, b.shape[0] if b.ndim == 3 else 1)
    n_rows, n = b.shape[-2], b.shape[-1]
    m = a.shape[-2] if a_transposed else a.shape[-1]
    kt = _contraction_tile(n_rows)
    steps = n_rows // kt

    def body(a_ref, b_ref, out_ref, acc_ref):
        i = pl.program_id(1)
        a_t, b_t = a_ref[...].astype(BF16), b_ref[...].astype(BF16)
        part = _dot(a_t, b_t) if a_transposed else _dot_tn(a_t, b_t)
        _accumulate(acc_ref, i == 0, part)

        @pl.when(i == steps - 1)
        def _():
            out_ref[...] = acc_ref[...].astype(out_ref.dtype)

    def spec(arr, rows_last):
        block = (arr.shape[-2], kt) if rows_last else (kt, arr.shape[-1])
        at = (lambda i: (0, i)) if rows_last else (lambda i: (i, 0))
        if arr.ndim == 3:
            return pl.BlockSpec((None,) + block, lambda g, i: (g,) + at(i))
        return pl.BlockSpec(block, lambda g, i: at(i))

    return pl.pallas_call(
        body, name=name, grid=(groups, steps),
        in_specs=[spec(a, a_transposed), spec(b, False)],
        out_specs=pl.BlockSpec((None, m, n), lambda g, i: (g, 0, 0)),
        out_shape=jax.ShapeDtypeStruct((groups, m, n), BF16),
        scratch_shapes=[pltpu.VMEM((m, n), F32)],
        compiler_params=_params("parallel", "arbitrary"),
    )(a, b)


def _my_index():
    return 4 * lax.axis_index("x") + 2 * lax.axis_index("y") + lax.axis_index("c")


def _peer(k):
    flip = lambda v, bit: 1 - v if bit else v
    px = flip(lax.axis_index("x"), k & 4)
    py = flip(lax.axis_index("y"), k & 2)
    pc = flip(lax.axis_index("c"), k & 1)
    return (px, py, pc), 4 * px + 2 * py + pc


def _all_gather(shards, dtypes):
    n = len(shards)

    def body(*refs):
        ins, outs, stages = refs[:n], refs[n:2 * n], refs[2 * n:3 * n]
        send_sems, recv_sems, local_sems = refs[3 * n:]
        me = _my_index()
        for a in range(n):
            stages[a][...] = ins[a][...].astype(stages[a].dtype)
        local = [pltpu.make_async_copy(stages[a], outs[a].at[me], local_sems.at[a]) for a in range(n)]
        for cp in local:
            cp.start()

        def copy(a, k, slot):
            peer, _ = _peer(k)
            return pltpu.make_async_remote_copy(
                src_ref=stages[a], dst_ref=outs[a].at[slot], send_sem=send_sems.at[a, k - 1],
                recv_sem=recv_sems.at[a, k - 1], device_id=peer, device_id_type=MESH)

        for k in range(1, N_DEV):
            for a in range(n):
                copy(a, k, me).start()
        for k in range(1, N_DEV):
            for a in range(n):
                copy(a, k, _peer(k)[1]).wait()
        for cp in local:
            cp.wait()

    return pl.pallas_call(
        body, name="gather_weights",
        in_specs=[pl.BlockSpec(memory_space=pltpu.VMEM)] * n,
        out_specs=[pl.BlockSpec(memory_space=pl.ANY)] * n,
        out_shape=[jax.ShapeDtypeStruct((N_DEV,) + s.shape, dt) for s, dt in zip(shards, dtypes)],
        scratch_shapes=[pltpu.VMEM(s.shape, dt) for s, dt in zip(shards, dtypes)]
        + [pltpu.SemaphoreType.DMA((n, N_DEV - 1)), pltpu.SemaphoreType.DMA((n, N_DEV - 1)), pltpu.SemaphoreType.DMA((n,))],
        compiler_params=pltpu.CompilerParams(vmem_limit_bytes=VMEM_LIMIT),
    )(*shards)


def _exchange(parts, whole):
    n = len(parts)

    def body(*refs):
        ins, outs = refs[:n], refs[n:2 * n]
        send_sems, recv_sems, local_sems = refs[2 * n:]
        me = _my_index()

        def src(a, slab):
            return ins[a] if whole[a] else ins[a].at[slab]

        local = [pltpu.make_async_copy(src(a, me), outs[a].at[me], local_sems.at[a]) for a in range(n)]
        for cp in local:
            cp.start()

        def copy(a, k, slab, slot):
            peer, _ = _peer(k)
            return pltpu.make_async_remote_copy(
                src_ref=src(a, slab), dst_ref=outs[a].at[slot], send_sem=send_sems.at[a, k - 1],
                recv_sem=recv_sems.at[a, k - 1], device_id=peer, device_id_type=MESH)

        for k in range(1, N_DEV):
            for a in range(n):
                copy(a, k, _peer(k)[1], me).start()
        for k in range(1, N_DEV):
            for a in range(n):
                copy(a, k, _peer(k)[1], _peer(k)[1]).wait()
        for cp in local:
            cp.wait()

    return pl.pallas_call(
        body, name="exchange_grads",
        in_specs=[pl.BlockSpec(memory_space=pl.ANY)] * n,
        out_specs=[pl.BlockSpec(memory_space=pl.ANY)] * n,
        out_shape=[jax.ShapeDtypeStruct(((N_DEV,) + p.shape) if w else p.shape, p.dtype) for p, w in zip(parts, whole)],
        scratch_shapes=[pltpu.SemaphoreType.DMA((n, N_DEV - 1)), pltpu.SemaphoreType.DMA((n, N_DEV - 1)),
                        pltpu.SemaphoreType.DMA((n,))],
    )(*parts)


def _sequencer_exchange(parts, whole, name, collective_id):
    n = len(parts)
    srcs = [jax.new_ref(p, memory_space=pltpu.MemorySpace.HBM) for p in parts]
    lands = [jax.empty_ref(jax.ShapeDtypeStruct(((N_DEV,) + p.shape) if w else p.shape, p.dtype),
                           memory_space=pltpu.MemorySpace.HBM) for p, w in zip(parts, whole)]

    @pl.kernel(mesh=plsc.ScalarSubcoreMesh(axis_name="sequencer", num_cores=1), name=name,
               scratch_types=(pltpu.SemaphoreType.DMA((n, N_DEV - 1)), pltpu.SemaphoreType.DMA((n, N_DEV - 1)),
                              pltpu.SemaphoreType.DMA((n,))),
               compiler_params=pltpu.CompilerParams(collective_id=collective_id))
    def launch(send_sems, recv_sems, local_sems):
        barrier = pltpu.get_barrier_semaphore()
        for k in range(1, N_DEV):
            pl.semaphore_signal(barrier, inc=1, device_id=_peer(k)[0], device_id_type=MESH)
        pl.semaphore_wait(barrier, N_DEV - 1)
        me = _my_index()

        def src(a, slab):
            return srcs[a] if whole[a] else srcs[a].at[slab]

        local = [pltpu.make_async_copy(src(a, me), lands[a].at[me], local_sems.at[a]) for a in range(n)]
        for cp in local:
            cp.start()

        def copy(a, k, slab, slot):
            return pltpu.make_async_remote_copy(
                src_ref=src(a, slab), dst_ref=lands[a].at[slot], send_sem=send_sems.at[a, k - 1],
                recv_sem=recv_sems.at[a, k - 1], device_id=_peer(k)[0], device_id_type=MESH)

        for k in range(1, N_DEV):
            for a in range(n):
                copy(a, k, _peer(k)[1], me).start()
        for k in range(1, N_DEV):
            for a in range(n):
                copy(a, k, _peer(k)[1], _peer(k)[1]).wait()
        for cp in local:
            cp.wait()

    launch()
    return [land[...] for land in lands]


def _row_block(rows):
    if rows <= ROW_TILE:
        return rows
    return next(rb for rb in range(ROW_TILE, 0, -16) if rows % rb == 0)


def _adamw(landing, w, m, v, name):
    rows, cols = w.shape
    rb = _row_block(rows)

    def body(l_ref, w_ref, m_ref, v_ref, g_ref, d_ref, m2_ref, v2_ref):
        g = l_ref[0].astype(F32)
        for p in range(1, N_DEV):
            g = g + l_ref[p].astype(F32)
        g_ref[...] = g
        d_ref[...], m2_ref[...], v2_ref[...] = _adamw_step(g, w_ref[...], m_ref[...], v_ref[...])

    flat = pl.BlockSpec((rb, cols), lambda i: (i, 0))
    return pl.pallas_call(
        body, name=name, grid=(rows // rb,),
        in_specs=[pl.BlockSpec((N_DEV, rb, cols), lambda i: (0, i, 0)), flat, flat, flat],
        out_specs=[flat] * 4,
        out_shape=[jax.ShapeDtypeStruct((rows, cols), F32)] * 4,
        compiler_params=_params("parallel"),
    )(landing, w, m, v)


def _adamw_step(g, w, m, v):
    m2 = ADAM_B1 * m + (1.0 - ADAM_B1) * g
    v2 = ADAM_B2 * v + (1.0 - ADAM_B2) * (g * g)
    m_hat = m2 / (1.0 - ADAM_B1 ** ADAM_STEP)
    v_hat = v2 / (1.0 - ADAM_B2 ** ADAM_STEP)
    return -ADAM_LR * (m_hat / (jnp.sqrt(v_hat) + ADAM_EPS) + ADAM_WD * w), m2, v2


_REPLICATED = (
    ("mix_norm_g", D_MODEL), ("q_norm_g", Q_LORA), ("kv_norm_g", KV_LORA), ("conv_b", D_CONV), ("conv_ln_g", D_CONV),
    ("conv_ln_b", D_CONV), ("conv_out_g", D_CONV), ("attn_out_g", D_CONV), ("ffn_norm_g", D_MODEL),
    ("ffn_conv_b", D_UP), ("final_norm_g", D_MODEL),
)
_REPLICATED_WIDTH = sum(size for _, size in _REPLICATED) + _LANES

_WEIGHT_ORDER = (
    "meta_tokens", "mix_norm_g", "w_in", "q_norm_g", "w_uq", "kv_norm_g", "w_ukv", "conv_w", "conv_b", "conv_ln_g",
    "conv_ln_b", "conv_out_g", "attn_out_g", "w_out", "ffn_norm_g", "w_ffn_up", "ffn_conv_w", "ffn_conv_b",
    "w_ffn_down", "final_norm_g",
)


def _pack_replicated(grads, loss):
    rows = [grads[name].reshape(1, size) for name, size in _REPLICATED]
    return jnp.concatenate(rows + [jnp.broadcast_to(loss.reshape(1, 1), (1, _LANES))], axis=-1)


def _adamw_replicated(landing, weights, moments_m, moments_v):
    n = len(_REPLICATED)

    def body(*refs):
        l_ref, ins, outs = refs[0], refs[1:1 + 3 * n], refs[1 + 3 * n:]
        total = l_ref[0]
        for p in range(1, N_DEV):
            total = total + l_ref[p]
        at = 0
        for a, (_, size) in enumerate(_REPLICATED):
            g = total[:, at:at + size]
            w_ref, m_ref, v_ref = ins[3 * a:3 * a + 3]
            g_ref, d_ref, m2_ref, v2_ref = outs[4 * a:4 * a + 4]
            g_ref[...] = g
            d_ref[...], m2_ref[...], v2_ref[...] = _adamw_step(g, w_ref[...], m_ref[...], v_ref[...])
            at += size
        outs[-1][...] = total[:, at:at + _LANES]

    operands, out_shapes = [], []
    for name, size in _REPLICATED:
        operands += [weights[name].reshape(1, size), moments_m[name].reshape(1, size), moments_v[name].reshape(1, size)]
        out_shapes += [jax.ShapeDtypeStruct((1, size), F32)] * 4
    out_shapes.append(jax.ShapeDtypeStruct((1, _LANES), F32))
    outs = pl.pallas_call(body, name="adamw_replicated", out_shape=out_shapes)(landing, *operands)
    return outs[-1][0, 0], {name: outs[4 * a:4 * a + 4] for a, (name, _) in enumerate(_REPLICATED)}


def _pad_rows(a, rows):
    return jnp.pad(a, ((0, rows - a.shape[0]), (0, 0)))


def _slabs(a):
    r, c = a.shape
    return a.reshape(r, N_DEV, c // N_DEV).transpose(1, 0, 2)


def _unslab(a):
    g, r, c = a.shape
    return a.transpose(1, 0, 2).reshape(r, g * c)


def _local_step(x, target, w, n_rows, ffn_weights, send_grads):
    cos_t, sin_t = _rope_tables(n_rows)
    cos, sin = cos_t.T, sin_t.T
    meta_pad, g1, gf = w["meta_pad"], w["mix_norm_g"], w["final_norm_g"]
    gq, gkv, gb_col = w["q_norm_g"], w["kv_norm_g"], w["attn_out_g"].reshape(D_ATTN, 1)
    nb, ag, cq, ckv, kr = _fwd_in(x, meta_pad, g1, w["w_in"], n_rows)
    mix_a, u1 = _fwd_conv(ag, w["conv_w"], w["conv_b"], w["conv_ln_g"], w["conv_ln_b"], w["conv_out_g"], n_rows)
    q_t, k, v, v_t, cqn, ckvn = _fwd_qkv(cq, ckv, kr, gq, gkv, w["wq_t"], w["w_ukv"], w["wv_t"], cos, sin, cos_t, sin_t, n_rows)
    o_t, lse = _attn_fwd(q_t, k, v_t, n_rows)
    w_out, w_up, w_down = ffn_weights()
    mix_bt, h1 = _fwd_out(x, meta_pad, mix_a, o_t, gb_col, w_out, n_rows)
    n2, up0, act, da, db, dh2, loss, dgf = _fwd_ffn(
        h1, target, w["ffn_norm_g"], w_up, w["fw"], w["fb"], w_down, gf, n_rows)

    dup, dfb = _bwd_ffn_act(dh2, da, db, w_down, n_rows)
    dup0, dh1, dfw, dg2 = _bwd_ffn_up(dup, up0, h1, dh2, w["ffn_norm_g"], w_up, w["fw"], n_rows)
    grad_w_out = jnp.concatenate([_weight_grad(mix_a, dh1, "grad_w_out_conv")[0],
                                  _weight_grad(mix_bt, dh1, "grad_w_out_attn", a_transposed=True)[0]], axis=0)
    stage0 = {
        "w_ffn_up": _weight_grad(dup0, n2, "grad_w_ffn_up"),
        "w_ffn_down": _weight_grad(act, dh2, "grad_w_ffn_down").reshape(N_DEV, D_FF // N_DEV, D_MODEL),
        "w_out": grad_w_out.reshape(N_DEV, D_MODEL // N_DEV, D_MODEL),
    }
    stage0, dh1 = lax.optimization_barrier((stage0, dh1))
    send_grads(0, stage0)
    do_t, delta, du1, dgb, dga, dlg, dlb, dcb = _bwd_out(
        dh1, o_t, u1, w_out, gb_col, w["conv_ln_g"], w["conv_ln_b"], w["conv_out_g"], n_rows)
    dq_t, dk, dv = _attn_bwd(q_t, k, v, do_t, lse, delta, n_rows)
    dqraw_t, dkv, dcq, dckv, dkr, dgq, dgkv = _bwd_qkv(
        dq_t, dk, dv, cq, ckv, gq, gkv, w["wq_t"], w["w_ukv"], cos, sin, cos_t, sin_t, n_rows)
    dz, dcw = _bwd_conv(du1, ag, w["conv_w"], dcq, dckv, dkr, n_rows)
    stage1 = {
        "w_in": _weight_grad(dz, nb, "grad_w_in")[0].reshape(N_DEV, D_IN // N_DEV, D_MODEL),
        "w_uq": _weight_grad(dqraw_t.reshape(N_HEADS * QK_DIM, n_rows), cqn, "grad_w_uq", a_transposed=True)[0].reshape(
            N_HEADS, QK_DIM, Q_LORA),
        "w_ukv": _slabs(_weight_grad(ckvn, dkv, "grad_w_ukv")[0]),
        "conv_w": _slabs(dcw),
        "ffn_conv_w": dfw[:, :, :UP_SLAB],
    }
    stage1, dz = lax.optimization_barrier((stage1, dz))
    send_grads(1, stage1)
    gx, gmeta, dg1 = _bwd_in(dz, x, meta_pad, dh1, g1, w["w_in"], n_rows)

    sharded = {"meta_tokens": _slabs(gmeta[DEAD:])}
    replicated = {
        "mix_norm_g": dg1, "q_norm_g": dgq, "kv_norm_g": dgkv, "conv_b": dcb, "conv_ln_g": dlg, "conv_ln_b": dlb,
        "conv_out_g": dga, "attn_out_g": dgb, "ffn_norm_g": dg2, "ffn_conv_b": dfb, "final_norm_g": dgf,
    }
    return loss[0, 0], gx, sharded, replicated


_SHARDED = (
    ("w_in", None, BF16), ("w_uq", None, BF16), ("w_ukv", None, BF16), ("w_out", None, BF16), ("w_ffn_up", None, BF16),
    ("w_ffn_down", None, BF16), ("conv_w", 32, F32), ("ffn_conv_w", 8, F32), ("meta_tokens", None, F32),
)
GATHER_LATE_ID = 3
EXCHANGE_STAGE_IDS = (4, 5)
_LATE_WEIGHTS = ("w_out", "w_ffn_up", "w_ffn_down")
_COLUMN_SHARDS = ("w_in", "w_uq", "w_ffn_up")


def kernel(x, meta_tokens, mix_norm_g, w_in, q_norm_g, w_uq, kv_norm_g, w_ukv, conv_w, conv_b, conv_ln_g, conv_ln_b, conv_out_g, attn_out_g, w_out, ffn_norm_g, w_ffn_up, ffn_conv_w, ffn_conv_b, w_ffn_down, final_norm_g, loss_target, m_meta_tokens, m_mix_norm_g, m_w_in, m_q_norm_g, m_w_uq, m_kv_norm_g, m_w_ukv, m_conv_w, m_conv_b, m_conv_ln_g, m_conv_ln_b, m_conv_out_g, m_attn_out_g, m_w_out, m_ffn_norm_g, m_w_ffn_up, m_ffn_conv_w, m_ffn_conv_b, m_w_ffn_down, m_final_norm_g, v_meta_tokens, v_mix_norm_g, v_w_in, v_q_norm_g, v_w_uq, v_kv_norm_g, v_w_ukv, v_conv_w, v_conv_b, v_conv_ln_g, v_conv_ln_b, v_conv_out_g, v_attn_out_g, v_w_out, v_ffn_norm_g, v_w_ffn_up, v_ffn_conv_w, v_ffn_conv_b, v_w_ffn_down, v_final_norm_g):
    given = dict(locals())
    weights = {name: given[name] for name in _WEIGHT_ORDER}
    moments_m = {name: given["m_" + name] for name in _WEIGHT_ORDER}
    moments_v = {name: given["v_" + name] for name in _WEIGHT_ORDER}
    seq = x.shape[1]
    n_rows = ROW_TILE + seq

    def shard2d(name, a):
        a = a.reshape(a.shape[-2], a.shape[-1])
        return a.T if name in _COLUMN_SHARDS else a

    early = [entry for entry in _SHARDED if entry[0] not in _LATE_WEIGHTS]
    shards = []
    for name, pad_to, _ in early:
        s = shard2d(name, weights[name])
        shards.append(s if pad_to is None else _pad_rows(s, pad_to))
    gathered = dict(zip([name for name, _, _ in early], _all_gather(shards, [dt for _, _, dt in early])))
    behind = gathered["meta_tokens"][0, 0, 0] * 0.0
    late_parts = [(shard2d(name, weights[name]) + behind).astype(BF16) for name in _LATE_WEIGHTS]
    late = _sequencer_exchange(late_parts, [True] * len(late_parts), "gather_late", GATHER_LATE_ID)
    meta_full = _unslab(gathered["meta_tokens"])
    full = {
        "meta_pad": jnp.concatenate([jnp.zeros((DEAD, D_MODEL), F32), meta_full], axis=0),
        "w_in": gathered["w_in"].reshape(D_IN, D_MODEL),
        "wq_t": gathered["w_uq"],
        "w_ukv": gathered["w_ukv"],
        "wv_t": gathered["w_ukv"][:, :, QK_NOPE:].transpose(0, 2, 1),
        "conv_w": _unslab(gathered["conv_w"][:, :CONV_WIDTH]),
        "fw": jnp.pad(gathered["ffn_conv_w"][:, :FFN_CONV_WIDTH], ((0, 0), (0, 0), (0, UP_PAD - UP_SLAB))),
        "fb": jnp.pad(ffn_conv_b.reshape(N_DEV, 1, UP_SLAB), ((0, 0), (0, 0), (0, UP_PAD - UP_SLAB))),
        "final_norm_g": final_norm_g.reshape(1, D_MODEL),
    }
    for name in ("mix_norm_g", "q_norm_g", "kv_norm_g", "conv_b", "conv_ln_g", "conv_ln_b", "conv_out_g", "attn_out_g",
                 "ffn_norm_g"):
        full[name] = weights[name]

    def ffn_weights():
        w_out_all, w_up_all, w_down_all = late
        return (w_out_all.reshape(D_MODEL, D_MODEL), w_up_all, w_down_all.reshape(N_ACT_SLAB, UP_SLAB, D_MODEL))

    wire = {name: (pad_to, dt) for name, pad_to, dt in _SHARDED}
    landing = {}

    def on_the_wire(name, slabs):
        pad_to, dt = wire[name]
        slabs = slabs.astype(dt)
        return slabs if pad_to is None else jnp.pad(slabs, ((0, 0), (0, pad_to - slabs.shape[1]), (0, 0)))

    def send_grads(stage, grads):
        parts = [on_the_wire(name, slabs) for name, slabs in grads.items()]
        if landing:
            arrived = list(landing)
            parts, held = lax.optimization_barrier((parts, [landing[name] for name in arrived]))
            landing.update(zip(arrived, held))
        landed = _sequencer_exchange(parts, [False] * len(parts), f"exchange_stage{stage}", EXCHANGE_STAGE_IDS[stage])
        landing.update(zip(grads, landed))

    loss, gx, sharded, replicated = _local_step(x[0], loss_target[0], full, n_rows, ffn_weights, send_grads)

    parts = [on_the_wire(name, slabs) for name, slabs in sharded.items()] + [_pack_replicated(replicated, loss)]
    landed = _exchange(parts, [False] * len(sharded) + [True])
    landing.update(zip(sharded, landed[:-1]))

    grad, delta, new_m, new_v = {}, {}, {}, {}
    for name, pad_to, _ in _SHARDED:
        land = landing[name]
        ws, ms, vs = (shard2d(name, a[name]) for a in (weights, moments_m, moments_v))
        rows = ws.shape[0]
        if pad_to is not None:
            ws, ms, vs = _pad_rows(ws, pad_to), _pad_rows(ms, pad_to), _pad_rows(vs, pad_to)
        outs = _adamw(land, ws, ms, vs, "adamw_" + name)
        shape = weights[name].shape
        grad[name], delta[name], new_m[name], new_v[name] = (
            (o.T if name in _COLUMN_SHARDS else o[:rows]).reshape(shape) for o in outs)
    loss, updates = _adamw_replicated(landed[-1], weights, moments_m, moments_v)
    for name, outs in updates.items():
        grad[name], delta[name], new_m[name], new_v[name] = (o.reshape(weights[name].shape) for o in outs)

    return (loss, gx[None], *[grad[n] for n in _WEIGHT_ORDER], *[delta[n] for n in _WEIGHT_ORDER],
            *[new_m[n] for n in _WEIGHT_ORDER], *[new_v[n] for n in _WEIGHT_ORDER])
```

```python
import functools

import jax
import jax.numpy as jnp
from jax import lax
from jax.experimental import pallas as pl
from jax.experimental.pallas import tpu as pltpu
from jax.experimental.pallas import tpu_sc as plsc

F32 = jnp.float32
BF16 = jnp.bfloat16

N_DEV = 8
D_MODEL = 1024
CHUNK = 64
CHUNK_SHIFT = 6
N_META = 16
D_CONV = 512
CONV_WIDTH = 31
N_HEADS = 8
QK_NOPE = 64
QK_ROPE = 32
QK_DIM = QK_NOPE + QK_ROPE
V_HEAD = 64
KV_HEAD = QK_NOPE + V_HEAD
D_ATTN = N_HEADS * V_HEAD
Q_LORA = 384
KV_LORA = 256
ROPE_THETA = 10000.0
D_IN = 2 * D_CONV + Q_LORA + KV_LORA + QK_ROPE
D_FF = 2816
D_UP = 2 * D_FF
FFN_CONV_WIDTH = 3
UP_SLAB = D_UP // N_DEV
N_ACT_SLAB = D_FF // UP_SLAB
EPS = 1e-6
NEG = -1e30
_LN2 = 0.6931471805599453
QK_LOGIT_SCALE = QK_DIM ** -0.5 / _LN2
ADAM_LR = 0.001
ADAM_B1 = 0.9
ADAM_B2 = 0.999
ADAM_EPS = 1e-08
ADAM_WD = 0.01
ADAM_STEP = 10

ROW_TILE = 256
DEAD = ROW_TILE - N_META
CONV_HALO = 32
FFN_HALO = 16
VMEM_LIMIT = 56 * 1024 * 1024
_LANES = 128

MESH = pl.DeviceIdType.MESH


def _dot(a, b):
    return jnp.dot(a, b, preferred_element_type=F32)


def _dot_nt(a, b):
    return lax.dot_general(a, b, (((1,), (1,)), ((), ())), preferred_element_type=F32)


def _dot_tn(a, b):
    return lax.dot_general(a, b, (((0,), (0,)), ((), ())), preferred_element_type=F32)


def _sigmoid(x):
    return 1.0 / (1.0 + jnp.exp2(x * (-1.0 / _LN2)))


def _rms_fwd(x, g):
    r = lax.rsqrt(jnp.mean(x * x, axis=-1, keepdims=True) + EPS)
    return x * r * g


def _rms_bwd(dy, x, g):
    r = lax.rsqrt(jnp.mean(x * x, axis=-1, keepdims=True) + EPS)
    w = dy * g
    dx = r * w - x * (r * r * r) * jnp.mean(w * x, axis=-1, keepdims=True)
    return dx, jnp.sum(dy * x * r, axis=0, keepdims=True)


def _rope(x, cos, sin):
    half = QK_ROPE // 2
    x1, x2 = x[:, :half], x[:, half:]
    return jnp.concatenate([x1 * cos - x2 * sin, x2 * cos + x1 * sin], axis=-1)


def _rope_t(dy, cos, sin):
    half = QK_ROPE // 2
    d1, d2 = dy[:, :half], dy[:, half:]
    return jnp.concatenate([d1 * cos + d2 * sin, d2 * cos - d1 * sin], axis=-1)


def _row_ids(i, rows):
    return i * rows + lax.broadcasted_iota(jnp.int32, (rows, 1), 0)


def _accumulate(ref, first, value):
    @pl.when(first)
    def _():
        ref[...] = value

    @pl.when(jnp.logical_not(first))
    def _():
        ref[...] += value


def _tile_spec(shape):
    nd = len(shape)
    if nd == 2:
        return pl.BlockSpec((ROW_TILE, shape[1]), lambda i: (i, 0))
    return pl.BlockSpec((shape[0], ROW_TILE, shape[2]), lambda i: (0, i, 0))


def _whole_spec(shape):
    nd = len(shape)
    return pl.BlockSpec(tuple(shape), lambda i: (0,) * nd, pipeline_mode=pl.Buffered(1))


def _acc_spec(shape):
    nd = len(shape)
    return pl.BlockSpec(tuple(shape), lambda i: (0,) * nd)


def _real_spec(width):
    return pl.BlockSpec((ROW_TILE, width), lambda i: (jnp.maximum(i - 1, 0), 0))


def _params(*semantics):
    return pltpu.CompilerParams(dimension_semantics=semantics, vmem_limit_bytes=VMEM_LIMIT)


def _fwd_in(x, meta_pad, g1, w_in, n_rows):
    nt = n_rows // ROW_TILE

    def body(x_ref, meta_ref, g_ref, w_ref, nb_ref, ag_ref, cq_ref, ckv_ref, kr_ref):
        i = pl.program_id(0)
        h0 = jnp.where(i == 0, meta_ref[...], x_ref[...])
        nb = _rms_fwd(h0, g_ref[...]).astype(BF16)
        nb_ref[...] = nb
        z = _dot_nt(nb, w_ref[...])
        ag_ref[...] = z[:, :2 * D_CONV]
        cq_ref[...] = z[:, 2 * D_CONV:2 * D_CONV + Q_LORA]
        ckv_ref[...] = z[:, 2 * D_CONV + Q_LORA:2 * D_CONV + Q_LORA + KV_LORA]
        kr_ref[...] = z[:, 2 * D_CONV + Q_LORA + KV_LORA:]

    out_shapes = [
        jax.ShapeDtypeStruct((n_rows, D_MODEL), BF16),
        jax.ShapeDtypeStruct((n_rows, 2 * D_CONV), F32),
        jax.ShapeDtypeStruct((n_rows, Q_LORA), F32),
        jax.ShapeDtypeStruct((n_rows, KV_LORA), F32),
        jax.ShapeDtypeStruct((n_rows, QK_ROPE), F32),
    ]
    return pl.pallas_call(
        body, name="fwd_in", grid=(nt,),
        in_specs=[_real_spec(D_MODEL), _whole_spec(meta_pad.shape), _whole_spec(g1.shape), _whole_spec(w_in.shape)],
        out_specs=[_tile_spec(s.shape) for s in out_shapes],
        out_shape=out_shapes,
        compiler_params=_params("parallel"),
    )(x, meta_pad, g1, w_in)


def _conv_chain(u1, ln_g, ln_b):
    mu = jnp.mean(u1, axis=-1, keepdims=True)
    xc = u1 - mu
    rstd = lax.rsqrt(jnp.mean(xc * xc, axis=-1, keepdims=True) + EPS)
    xh = xc * rstd
    u2 = xh * ln_g + ln_b
    return xh, u2, u2 * _sigmoid(u2), rstd


def _fwd_conv(ag, conv_w, conv_b, ln_g, ln_b, out_g, n_rows):
    nt = n_rows // ROW_TILE

    def body(ag_ref, w_ref, b_ref, lg_ref, lb_ref, og_ref, mix_ref, u1_ref, ext_ref, conv_ref):
        i = pl.program_id(0)

        @pl.when(i == 0)
        def _():
            ext_ref[:, 0:CONV_HALO, :] = jnp.zeros((CONV_PLANES, CONV_HALO, _LANES), F32)

        ag_t = ag_ref[...]
        live = _row_ids(i, ROW_TILE) >= DEAD
        u0 = jnp.where(live, ag_t[:, :D_CONV] * _sigmoid(ag_t[:, D_CONV:]), 0.0)
        _to_planes(ext_ref, (), slice(CONV_HALO, None), u0)
        first = CONV_HALO - (CONV_WIDTH - 1)
        for c in range(CONV_PLANES):
            taps = w_ref[:, c * _LANES:(c + 1) * _LANES]
            for p in range(PHASES):
                acc = jnp.zeros((PHASE_ROWS, _LANES), F32)
                for k in range(CONV_WIDTH):
                    acc = acc + taps[k:k + 1, :] * ext_ref[c, _phase(first + k + p), :]
                conv_ref[c, _phase(p), :] = acc
        ext_ref[:, 0:CONV_HALO, :] = ext_ref[:, ROW_TILE:ROW_TILE + CONV_HALO, :]
        u1 = _from_planes(conv_ref, (), D_CONV) + b_ref[...]
        u1_ref[...] = u1
        _, _, u3, _ = _conv_chain(u1, lg_ref[...], lb_ref[...])
        mix_ref[...] = _rms_fwd(u3, og_ref[...]).astype(BF16)

    out_shapes = [jax.ShapeDtypeStruct((n_rows, D_CONV), BF16), jax.ShapeDtypeStruct((n_rows, D_CONV), F32)]
    small = [conv_w, conv_b, ln_g, ln_b, out_g]
    return pl.pallas_call(
        body, name="fwd_conv", grid=(nt,),
        in_specs=[_tile_spec(ag.shape)] + [_whole_spec(a.shape) for a in small],
        out_specs=[_tile_spec(s.shape) for s in out_shapes],
        out_shape=out_shapes,
        scratch_shapes=[pltpu.VMEM((CONV_PLANES, ROW_TILE + CONV_HALO, _LANES), F32),
                        pltpu.VMEM((CONV_PLANES, ROW_TILE, _LANES), F32)],
        compiler_params=_params("arbitrary"),
    )(ag, *small)


def _lane_tile(shape):
    if len(shape) == 2:
        return pl.BlockSpec((shape[0], ROW_TILE), lambda i: (0, i))
    return pl.BlockSpec((shape[0], shape[1], ROW_TILE), lambda i: (0, 0, i))


def _rope_rows(x, cos, sin):
    half = QK_ROPE // 2
    x1, x2 = x[:half], x[half:]
    return jnp.concatenate([x1 * cos - x2 * sin, x2 * cos + x1 * sin], axis=0)


def _rope_rows_t(dy, cos, sin):
    half = QK_ROPE // 2
    d1, d2 = dy[:half], dy[half:]
    return jnp.concatenate([d1 * cos + d2 * sin, d2 * cos - d1 * sin], axis=0)


def _fwd_qkv(cq, ckv, kr, gq, gkv, wq_t, w_ukv, wv_t, cos, sin, cos_t, sin_t, n_rows):
    nt = n_rows // ROW_TILE

    def body(cq_ref, ckv_ref, kr_ref, gq_ref, gkv_ref, wqt_ref, wkv_ref, wvt_ref, cos_ref, sin_ref, cost_ref, sint_ref,
             qt_ref, k_ref, v_ref, vt_ref, cqn_ref, ckvn_ref):
        cqn = _rms_fwd(cq_ref[...], gq_ref[...]).astype(BF16)
        ckvn = _rms_fwd(ckv_ref[...], gkv_ref[...]).astype(BF16)
        cqn_ref[...] = cqn
        ckvn_ref[...] = ckvn
        k_rot = _rope(kr_ref[...], cos_ref[...], sin_ref[...])
        cos_rows, sin_rows = cost_ref[...], sint_ref[...]
        for h in range(N_HEADS):
            q_raw = _dot_nt(wqt_ref[h], cqn)
            q_h = jnp.concatenate([q_raw[:QK_NOPE], _rope_rows(q_raw[QK_NOPE:], cos_rows, sin_rows)], axis=0)
            qt_ref[h] = (q_h * QK_LOGIT_SCALE).astype(BF16)
            kv = _dot(ckvn, wkv_ref[h])
            k_ref[h] = jnp.concatenate([kv[:, :QK_NOPE], k_rot], axis=-1).astype(BF16)
            v_ref[h] = kv[:, QK_NOPE:].astype(BF16)
            vt_ref[h] = _dot_nt(wvt_ref[h], ckvn).astype(BF16)

    out_shapes = [
        jax.ShapeDtypeStruct((N_HEADS, QK_DIM, n_rows), BF16),
        jax.ShapeDtypeStruct((N_HEADS, n_rows, QK_DIM), BF16),
        jax.ShapeDtypeStruct((N_HEADS, n_rows, V_HEAD), BF16),
        jax.ShapeDtypeStruct((N_HEADS, V_HEAD, n_rows), BF16),
        jax.ShapeDtypeStruct((n_rows, Q_LORA), BF16),
        jax.ShapeDtypeStruct((n_rows, KV_LORA), BF16),
    ]
    tiles = [cq, ckv, kr]
    whole = [gq, gkv, wq_t, w_ukv, wv_t]
    out_specs = [_lane_tile(out_shapes[0].shape), _tile_spec(out_shapes[1].shape), _tile_spec(out_shapes[2].shape),
                 _lane_tile(out_shapes[3].shape), _tile_spec(out_shapes[4].shape), _tile_spec(out_shapes[5].shape)]
    return pl.pallas_call(
        body, name="fwd_qkv", grid=(nt,),
        in_specs=[_tile_spec(a.shape) for a in tiles] + [_whole_spec(a.shape) for a in whole]
        + [_tile_spec(cos.shape), _tile_spec(sin.shape), _lane_tile(cos_t.shape), _lane_tile(sin_t.shape)],
        out_specs=out_specs,
        out_shape=out_shapes,
        compiler_params=_params("parallel"),
    )(*tiles, *whole, cos, sin, cos_t, sin_t)


def _chunk_of(rows):
    return jnp.where(rows >= ROW_TILE, lax.shift_right_arithmetic(rows - ROW_TILE, CHUNK_SHIFT) + 1, 0)


def _visible(i, j):
    k_rows = j * ROW_TILE + lax.broadcasted_iota(jnp.int32, (ROW_TILE, 1), 0)
    q_rows = i * ROW_TILE + lax.broadcasted_iota(jnp.int32, (1, ROW_TILE), 1)
    return jnp.logical_and(_chunk_of(q_rows) >= _chunk_of(k_rows), k_rows >= DEAD)


def _attn_fwd(q_t, k, v_t, n_rows):
    nt = n_rows // ROW_TILE

    def body(qt_ref, k_ref, vt_ref, ot_ref, lse_ref):
        i = pl.program_id(0)
        q_ts = [qt_ref[h] for h in range(N_HEADS)]

        def make_step(masked):
            def step(j, carry):
                rows = pl.ds(pl.multiple_of(j * ROW_TILE, ROW_TILE), ROW_TILE)
                scores = [_dot(k_ref[h, rows, :], q_ts[h]) for h in range(N_HEADS)]
                visible = _visible(i, j) if masked else None
                probs, state = [], []
                for h in range(N_HEADS):
                    m, l, _ = carry[h]
                    s = jnp.where(visible, scores[h], NEG) if masked else scores[h]
                    m_new = jnp.maximum(m, jnp.max(s, axis=0, keepdims=True))
                    alpha = jnp.exp2(m - m_new)
                    p = jnp.exp2(s - m_new)
                    probs.append(p.astype(BF16))
                    state.append((m_new, alpha * l + jnp.sum(p, axis=0, keepdims=True), alpha))
                outs = [_dot(vt_ref[h, :, rows], probs[h]) for h in range(N_HEADS)]
                return tuple((state[h][0], state[h][1], state[h][2] * carry[h][2] + outs[h]) for h in range(N_HEADS))
            return step

        init = tuple((jnp.full((1, ROW_TILE), NEG, F32), jnp.zeros((1, ROW_TILE), F32),
                      jnp.zeros((V_HEAD, ROW_TILE), F32)) for _ in range(N_HEADS))
        carry = make_step(True)(0, init)
        carry = lax.fori_loop(1, i, make_step(False), carry)
        carry = lax.fori_loop(jnp.maximum(i, 1), i + 1, make_step(True), carry)
        for h in range(N_HEADS):
            m, l, acc = carry[h]
            ot_ref[h] = acc / l
            lse_ref[h] = m + jnp.log2(l)

    out_shapes = [jax.ShapeDtypeStruct((N_HEADS, V_HEAD, n_rows), F32), jax.ShapeDtypeStruct((N_HEADS, 1, n_rows), F32)]
    return pl.pallas_call(
        body, name="attn_fwd", grid=(nt,),
        in_specs=[_lane_tile(q_t.shape), _whole_spec(k.shape), _whole_spec(v_t.shape)],
        out_specs=[_lane_tile(s.shape) for s in out_shapes],
        out_shape=out_shapes,
        compiler_params=_params("parallel"),
    )(q_t, k, v_t)


def _heads_to_rows(ref):
    return jnp.concatenate([ref[h] for h in range(N_HEADS)], axis=0)


def _rms_cols(x, g_col):
    r = lax.rsqrt(jnp.mean(x * x, axis=0, keepdims=True) + EPS)
    return x * r * g_col


def _fwd_out(x, meta_pad, mix_a, o_t, gb_col, w_out, n_rows):
    nt = n_rows // ROW_TILE

    def body(x_ref, meta_ref, mixa_ref, ot_ref, gb_ref, w_ref, mixbt_ref, h1_ref):
        i = pl.program_id(0)
        h0 = jnp.where(i == 0, meta_ref[...], x_ref[...])
        mix_bt = _rms_cols(_heads_to_rows(ot_ref), gb_ref[...]).astype(BF16)
        mixbt_ref[...] = mix_bt
        h1_ref[...] = h0 + _dot(mixa_ref[...], w_ref[:D_CONV, :]) + _dot_tn(mix_bt, w_ref[D_CONV:, :])

    out_shapes = [jax.ShapeDtypeStruct((D_ATTN, n_rows), BF16), jax.ShapeDtypeStruct((n_rows, D_MODEL), F32)]
    return pl.pallas_call(
        body, name="fwd_out", grid=(nt,),
        in_specs=[_real_spec(D_MODEL), _whole_spec(meta_pad.shape), _tile_spec(mix_a.shape), _lane_tile(o_t.shape),
                  _whole_spec(gb_col.shape), _whole_spec(w_out.shape)],
        out_specs=[_lane_tile(out_shapes[0].shape), _tile_spec(out_shapes[1].shape)],
        out_shape=out_shapes,
        compiler_params=_params("parallel"),
    )(x, meta_pad, mix_a, o_t, gb_col, w_out)


PHASES = 8
PHASE_ROWS = ROW_TILE // PHASES
UP_PLANES = -(-UP_SLAB // _LANES)
UP_PAD = UP_PLANES * _LANES
CONV_PLANES = D_CONV // _LANES


def _phase(start):
    return pl.ds(start, PHASE_ROWS, stride=PHASES)


def _to_planes(ref, lead, rows, value):
    width = value.shape[-1]
    for c in range(-(-width // _LANES)):
        part = value[:, c * _LANES:min((c + 1) * _LANES, width)]
        if part.shape[-1] < _LANES:
            part = jnp.concatenate([part, jnp.zeros((part.shape[0], _LANES - part.shape[-1]), part.dtype)], axis=-1)
        ref[(*lead, c, rows, slice(None))] = part


def _from_planes(ref, lead, width):
    planes = [ref[(*lead, c)] for c in range(-(-width // _LANES))]
    last = width - (len(planes) - 1) * _LANES
    return jnp.concatenate(planes[:-1] + [planes[-1][:, :last]], axis=-1)


def _fwd_ffn(h1, target, g2, w_up, fw, fb, w_down, gf, n_rows):
    nt = n_rows // ROW_TILE

    def body(h1_ref, t_ref, g2_ref, wup_ref, fw_ref, fb_ref, wdn_ref, gf_ref,
             n2_ref, up0_ref, act_ref, da_ref, db_ref, dh2_ref, loss_ref, dgf_ref, ext_ref):
        i = pl.program_id(0)

        @pl.when(i == 0)
        def _():
            ext_ref[:, 0:FFN_HALO, :] = jnp.zeros((N_DEV, FFN_HALO, UP_SLAB), F32)

        h1_t = h1_ref[...]
        live = _row_ids(i, ROW_TILE) >= DEAD
        n2 = jnp.where(live, _rms_fwd(h1_t, g2_ref[...]), 0.0).astype(BF16)
        n2_ref[...] = n2
        for s in range(N_DEV):
            up0 = _dot_nt(n2, wup_ref[s])
            up0_ref[s] = up0.astype(BF16)
            ext_ref[s, FFN_HALO:, :] = up0
        first = FFN_HALO - (FFN_CONV_WIDTH - 1)

        def conv(s):
            block = ext_ref[s]
            acc = fb_ref[s, :, :UP_SLAB] + fw_ref[s, FFN_CONV_WIDTH - 1:FFN_CONV_WIDTH, :UP_SLAB] * block[FFN_HALO:]
            for back in range(1, FFN_CONV_WIDTH):
                k = FFN_CONV_WIDTH - 1 - back
                acc = acc + fw_ref[s, k:k + 1, :UP_SLAB] * pltpu.roll(block, back, 0)[FFN_HALO:]
            return acc

        h2 = h1_t
        for s in range(N_ACT_SLAB):
            gate = conv(s)
            val = conv(s + N_ACT_SLAB)
            sg = _sigmoid(gate)
            silu = gate * sg
            act = (silu * val).astype(BF16)
            act_ref[s] = act
            da_ref[s] = (val * sg * (1.0 + gate * (1.0 - sg))).astype(BF16)
            db_ref[s] = silu.astype(BF16)
            h2 = h2 + _dot(act, wdn_ref[s])
        ext_ref[:, 0:FFN_HALO, :] = ext_ref[:, ROW_TILE:ROW_TILE + FFN_HALO, :]

        gf_t = gf_ref[...]
        y = _rms_fwd(h2, gf_t)
        diff = jnp.where(i >= 1, y - t_ref[...], 0.0)
        tile_loss = 0.5 * jnp.sum(jnp.sum(diff * diff, axis=-1, keepdims=True), axis=0, keepdims=True) / D_MODEL
        dh2, dgf = _rms_bwd(diff / D_MODEL, h2, gf_t)
        dh2_ref[...] = dh2
        _accumulate(loss_ref, i == 0, jnp.broadcast_to(tile_loss, loss_ref.shape))
        _accumulate(dgf_ref, i == 0, dgf)

    act_like = jax.ShapeDtypeStruct((N_ACT_SLAB, n_rows, UP_SLAB), BF16)
    out_shapes = [
        jax.ShapeDtypeStruct((n_rows, D_MODEL), BF16),
        jax.ShapeDtypeStruct((N_DEV, n_rows, UP_SLAB), BF16),
        act_like, act_like, act_like,
        jax.ShapeDtypeStruct((n_rows, D_MODEL), F32),
        jax.ShapeDtypeStruct((8, 128), F32),
        jax.ShapeDtypeStruct((1, D_MODEL), F32),
    ]
    whole = [g2, w_up, fw, fb, w_down, gf]
    return pl.pallas_call(
        body, name="fwd_ffn", grid=(nt,),
        in_specs=[_tile_spec(h1.shape), _real_spec(D_MODEL)] + [_whole_spec(a.shape) for a in whole],
        out_specs=[_tile_spec(s.shape) for s in out_shapes[:6]] + [_acc_spec(s.shape) for s in out_shapes[6:]],
        out_shape=out_shapes,
        scratch_shapes=[pltpu.VMEM((N_DEV, ROW_TILE + FFN_HALO, UP_SLAB), F32)],
        compiler_params=_params("arbitrary"),
    )(h1, target, *whole)


def _rope_tables(n_rows):
    pos = jnp.maximum(jnp.arange(n_rows, dtype=jnp.int32) - DEAD, 0)
    inv_freq = 1.0 / (ROPE_THETA ** (jnp.arange(0, QK_ROPE, 2, dtype=F32) / QK_ROPE))
    ang_t = inv_freq[:, None] * pos.astype(F32)[None, :]
    return jnp.cos(ang_t), jnp.sin(ang_t)


def _halo_after(shape, halo, n_rows):
    last = n_rows // halo - 1
    step = ROW_TILE // halo
    if len(shape) == 2:
        return pl.BlockSpec((halo, shape[1]), lambda i: (jnp.minimum((i + 1) * step, last), 0))
    return pl.BlockSpec((shape[0], halo, shape[2]), lambda i: (0, jnp.minimum((i + 1) * step, last), 0))


def _halo_before(shape, halo):
    step = ROW_TILE // halo
    if len(shape) == 2:
        return pl.BlockSpec((halo, shape[1]), lambda i: (jnp.maximum(i * step - 1, 0), 0))
    return pl.BlockSpec((shape[0], halo, shape[2]), lambda i: (0, jnp.maximum(i * step - 1, 0), 0))


def _bwd_ffn_act(dh2, da, db, w_down, n_rows):
    nt = n_rows // ROW_TILE

    def body(dh2_ref, da_ref, db_ref, wdn_ref, dup_ref, dfb_ref):
        i = pl.program_id(0)

        @pl.when(i == 0)
        def _():
            dfb_ref[...] = jnp.zeros_like(dfb_ref)

        dh2_b = dh2_ref[...].astype(BF16)
        for s in range(N_ACT_SLAB):
            d_act = _dot_nt(dh2_b, wdn_ref[s])
            d_gate = d_act * da_ref[s].astype(F32)
            d_val = d_act * db_ref[s].astype(F32)
            dup_ref[s] = d_gate.astype(BF16)
            dup_ref[s + N_ACT_SLAB] = d_val.astype(BF16)
            dfb_ref[s] += jnp.sum(d_gate, axis=0, keepdims=True)
            dfb_ref[s + N_ACT_SLAB] += jnp.sum(d_val, axis=0, keepdims=True)

    out_shapes = [jax.ShapeDtypeStruct((N_DEV, n_rows, UP_SLAB), BF16), jax.ShapeDtypeStruct((N_DEV, 1, UP_SLAB), F32)]
    return pl.pallas_call(
        body, name="bwd_ffn_act", grid=(nt,),
        in_specs=[_tile_spec(dh2.shape), _tile_spec(da.shape), _tile_spec(db.shape), _whole_spec(w_down.shape)],
        out_specs=[_tile_spec(out_shapes[0].shape), _acc_spec(out_shapes[1].shape)],
        out_shape=out_shapes,
        compiler_params=_params("arbitrary"),
    )(dh2, da, db, w_down)


def _bwd_ffn_up(dup, up0, h1, dh2, g2, w_up, fw, n_rows):
    nt = n_rows // ROW_TILE
    last_tap = FFN_CONV_WIDTH - 1
    ext_rows = ROW_TILE + FFN_HALO

    def body(dup_ref, dnext_ref, up0_ref, h1_ref, dh2_ref, g2_ref, wup_ref, fw_ref,
             dup0_ref, dh1_ref, dfw_ref, dg2_ref):
        i = pl.program_id(0)

        @pl.when(i == 0)
        def _():
            dfw_ref[...] = jnp.zeros_like(dfw_ref)

        live = _row_ids(i, ROW_TILE) >= DEAD
        dn2 = jnp.zeros((ROW_TILE, D_MODEL), F32)
        for s in range(N_DEV):
            d = dup_ref[s].astype(F32)
            block = jnp.concatenate([d, jnp.where(i == nt - 1, 0.0, dnext_ref[s].astype(F32))], axis=0)
            u = up0_ref[s].astype(F32)
            dup0 = fw_ref[s, last_tap:last_tap + 1, :UP_SLAB] * d
            dfw_ref[s, last_tap:last_tap + 1, :UP_SLAB] += jnp.sum(d * u, axis=0, keepdims=True)
            for ahead in range(1, FFN_CONV_WIDTH):
                k = last_tap - ahead
                shifted = pltpu.roll(block, ext_rows - ahead, 0)[:ROW_TILE]
                dup0 = dup0 + fw_ref[s, k:k + 1, :UP_SLAB] * shifted
                dfw_ref[s, k:k + 1, :UP_SLAB] += jnp.sum(shifted * u, axis=0, keepdims=True)
            dup0_b = jnp.where(live, dup0, 0.0).astype(BF16)
            dup0_ref[s] = dup0_b
            dn2 = dn2 + _dot(dup0_b, wup_ref[s])
        dx, dg2 = _rms_bwd(dn2, h1_ref[...], g2_ref[...])
        dh1_ref[...] = dh2_ref[...] + dx
        _accumulate(dg2_ref, i == 0, dg2)

    out_shapes = [
        jax.ShapeDtypeStruct((N_DEV, n_rows, UP_SLAB), BF16),
        jax.ShapeDtypeStruct((n_rows, D_MODEL), F32),
        jax.ShapeDtypeStruct((N_DEV, FFN_CONV_WIDTH, UP_PAD), F32),
        jax.ShapeDtypeStruct((1, D_MODEL), F32),
    ]
    return pl.pallas_call(
        body, name="bwd_ffn_up", grid=(nt,),
        in_specs=[_tile_spec(dup.shape), _halo_after(dup.shape, FFN_HALO, n_rows), _tile_spec(up0.shape),
                  _tile_spec(h1.shape), _tile_spec(dh2.shape),
                  _whole_spec(g2.shape), _whole_spec(w_up.shape), _whole_spec(fw.shape)],
        out_specs=[_tile_spec(s.shape) for s in out_shapes[:2]] + [_acc_spec(s.shape) for s in out_shapes[2:]],
        out_shape=out_shapes,
        compiler_params=_params("arbitrary"),
    )(dup, dup, up0, h1, dh2, g2, w_up, fw)


def _bwd_out(dh1, o_t, u1, w_out, gb_col, ln_g, ln_b, ga, n_rows):
    nt = n_rows // ROW_TILE

    def body(dh1_ref, ot_ref, u1_ref, w_ref, gb_ref, lg_ref, lb_ref, ga_ref,
             dot_ref, delta_ref, du1_ref, dgb_ref, dga_ref, dlg_ref, dlb_ref, dcb_ref):
        i = pl.program_id(0)
        dh1_b = dh1_ref[...].astype(BF16)
        o_t = _heads_to_rows(ot_ref)
        gb = gb_ref[...]
        r = lax.rsqrt(jnp.mean(o_t * o_t, axis=0, keepdims=True) + EPS)
        dmix_bt = _dot_nt(w_ref[D_CONV:, :], dh1_b)
        wgt = dmix_bt * gb
        do_t = r * wgt - o_t * (r * r * r) * jnp.mean(wgt * o_t, axis=0, keepdims=True)
        dgb = jnp.sum(dmix_bt * o_t * r, axis=1, keepdims=True)
        for h in range(N_HEADS):
            do_h = do_t[h * V_HEAD:(h + 1) * V_HEAD]
            dot_ref[h] = do_h.astype(BF16)
            delta_ref[h] = jnp.sum(do_h * ot_ref[h], axis=0, keepdims=True)
        lg = lg_ref[...]
        xh, u2, u3, rstd = _conv_chain(u1_ref[...], lg, lb_ref[...])
        du3, dga = _rms_bwd(_dot_nt(dh1_b, w_ref[:D_CONV, :]), u3, ga_ref[...])
        sg = _sigmoid(u2)
        du2 = du3 * sg * (1.0 + u2 * (1.0 - sg))
        dxh = du2 * lg
        du1 = rstd * (dxh - jnp.mean(dxh, axis=-1, keepdims=True) - xh * jnp.mean(dxh * xh, axis=-1, keepdims=True))
        du1_ref[...] = du1
        first = i == 0
        _accumulate(dgb_ref, first, dgb)
        _accumulate(dga_ref, first, dga)
        _accumulate(dlg_ref, first, jnp.sum(du2 * xh, axis=0, keepdims=True))
        _accumulate(dlb_ref, first, jnp.sum(du2, axis=0, keepdims=True))
        _accumulate(dcb_ref, first, jnp.sum(du1, axis=0, keepdims=True))

    out_shapes = [
        jax.ShapeDtypeStruct((N_HEADS, V_HEAD, n_rows), BF16),
        jax.ShapeDtypeStruct((N_HEADS, 1, n_rows), F32),
        jax.ShapeDtypeStruct((n_rows, D_CONV), F32),
        jax.ShapeDtypeStruct((D_ATTN, 1), F32),
    ] + [jax.ShapeDtypeStruct((1, D_CONV), F32)] * 4
    whole = [w_out, gb_col, ln_g, ln_b, ga]
    return pl.pallas_call(
        body, name="bwd_out", grid=(nt,),
        in_specs=[_tile_spec(dh1.shape), _lane_tile(o_t.shape), _tile_spec(u1.shape)] + [_whole_spec(a.shape) for a in whole],
        out_specs=[_lane_tile(out_shapes[0].shape), _lane_tile(out_shapes[1].shape), _tile_spec(out_shapes[2].shape)]
        + [_acc_spec(s.shape) for s in out_shapes[3:]],
        out_shape=out_shapes,
        compiler_params=_params("arbitrary"),
    )(dh1, o_t, u1, *whole)


ATTN_BWD_HEADS = 8


def _attn_bwd(q_t, k, v, do_t, lse, delta, n_rows):
    nt = n_rows // ROW_TILE
    hp = ATTN_BWD_HEADS

    def body(k_ref, v_ref, qt_ref, dot_ref, lse_ref, delta_ref, dqt_ref, dk_ref, dv_ref):
        j = pl.program_id(1)

        @pl.when(j == 0)
        def _():
            dqt_ref[...] = jnp.zeros_like(dqt_ref)

        k_ts = [k_ref[h] for h in range(hp)]
        v_ts = [v_ref[h] for h in range(hp)]

        def make_step(masked):
            def step(i, carry):
                cols = pl.ds(pl.multiple_of(i * ROW_TILE, ROW_TILE), ROW_TILE)
                q_is = [qt_ref[h, :, cols] for h in range(hp)]
                do_is = [dot_ref[h, :, cols] for h in range(hp)]
                scores = [_dot(k_ts[h], q_is[h]) for h in range(hp)]
                dps = [_dot(v_ts[h], do_is[h]) for h in range(hp)]
                visible = _visible(i, j) if masked else None
                probs, dss = [], []
                for h in range(hp):
                    s = jnp.where(visible, scores[h], NEG) if masked else scores[h]
                    p = jnp.exp2(s - lse_ref[h, :, cols])
                    probs.append(p.astype(BF16))
                    dss.append((p * (dps[h] - delta_ref[h, :, cols])).astype(BF16))
                out = []
                for h in range(hp):
                    dk, dv = carry[h]
                    dv = dv + _dot_nt(probs[h], do_is[h])
                    dk = dk + _dot_nt(dss[h], q_is[h])
                    dqt_ref[h, :, cols] += _dot_tn(k_ts[h], dss[h])
                    out.append((dk, dv))
                return tuple(out)
            return step

        init = tuple((jnp.zeros((ROW_TILE, QK_DIM), F32), jnp.zeros((ROW_TILE, V_HEAD), F32)) for _ in range(hp))
        carry = make_step(True)(j, init)
        carry = lax.fori_loop(jnp.where(j == 0, j + 1, nt), nt, make_step(True), carry)
        carry = lax.fori_loop(jnp.where(j == 0, nt, j + 1), nt, make_step(False), carry)
        for h in range(hp):
            dk_ref[h] = carry[h][0] * _LN2
            dv_ref[h] = carry[h][1]

    key_tile = lambda w: pl.BlockSpec((hp, ROW_TILE, w), lambda g, j: (g, j, 0))
    all_cols = lambda w: pl.BlockSpec((hp, w, n_rows), lambda g, j: (g, 0, 0))
    resident = lambda w: pl.BlockSpec((hp, w, n_rows), lambda g, j: (g, 0, 0), pipeline_mode=pl.Buffered(1))
    out_shapes = [
        jax.ShapeDtypeStruct((N_HEADS, QK_DIM, n_rows), F32),
        jax.ShapeDtypeStruct((N_HEADS, n_rows, QK_DIM), F32),
        jax.ShapeDtypeStruct((N_HEADS, n_rows, V_HEAD), F32),
    ]
    return pl.pallas_call(
        body, name="attn_bwd", grid=(N_HEADS // hp, nt),
        in_specs=[key_tile(QK_DIM), key_tile(V_HEAD), resident(QK_DIM), resident(V_HEAD), resident(1), resident(1)],
        out_specs=[all_cols(QK_DIM), key_tile(QK_DIM), key_tile(V_HEAD)],
        out_shape=out_shapes,
        compiler_params=_params("parallel", "arbitrary"),
    )(k, v, q_t, do_t, lse, delta)


def _bwd_qkv(dq_t, dk, dv, cq, ckv, gq, gkv, wq_t, w_ukv, cos, sin, cos_t, sin_t, n_rows):
    nt = n_rows // ROW_TILE

    def body(dqt_ref, dk_ref, dv_ref, cq_ref, ckv_ref, gq_ref, gkv_ref, wqt_ref, wkv_ref, cos_ref, sin_ref,
             cost_ref, sint_ref, dqraw_ref, dkv_ref, dcq_ref, dckv_ref, dkr_ref, dgq_ref, dgkv_ref):
        i = pl.program_id(0)
        cos_rows, sin_rows = cost_ref[...], sint_ref[...]
        dcqn = jnp.zeros((ROW_TILE, Q_LORA), F32)
        dckvn = jnp.zeros((ROW_TILE, KV_LORA), F32)
        dk_rot = jnp.zeros((ROW_TILE, QK_ROPE), F32)
        for h in range(N_HEADS):
            dq_h, dk_h = dqt_ref[h] * QK_DIM ** -0.5, dk_ref[h]
            dq_raw = jnp.concatenate(
                [dq_h[:QK_NOPE], _rope_rows_t(dq_h[QK_NOPE:], cos_rows, sin_rows)], axis=0).astype(BF16)
            dqraw_ref[h] = dq_raw
            dcqn = dcqn + _dot_tn(dq_raw, wqt_ref[h])
            dkv = jnp.concatenate([dk_h[:, :QK_NOPE], dv_ref[h]], axis=-1).astype(BF16)
            dkv_ref[:, h * KV_HEAD:(h + 1) * KV_HEAD] = dkv
            dckvn = dckvn + _dot_nt(dkv, wkv_ref[h])
            dk_rot = dk_rot + dk_h[:, QK_NOPE:]
        dkr_ref[...] = _rope_t(dk_rot, cos_ref[...], sin_ref[...]).astype(BF16)
        dcq, dgq = _rms_bwd(dcqn, cq_ref[...], gq_ref[...])
        dckv, dgkv = _rms_bwd(dckvn, ckv_ref[...], gkv_ref[...])
        dcq_ref[...] = dcq.astype(BF16)
        dckv_ref[...] = dckv.astype(BF16)
        _accumulate(dgq_ref, i == 0, dgq)
        _accumulate(dgkv_ref, i == 0, dgkv)

    out_shapes = [
        jax.ShapeDtypeStruct((N_HEADS, QK_DIM, n_rows), BF16),
        jax.ShapeDtypeStruct((n_rows, N_HEADS * KV_HEAD), BF16),
        jax.ShapeDtypeStruct((n_rows, Q_LORA), BF16),
        jax.ShapeDtypeStruct((n_rows, KV_LORA), BF16),
        jax.ShapeDtypeStruct((n_rows, QK_ROPE), BF16),
        jax.ShapeDtypeStruct((1, Q_LORA), F32),
        jax.ShapeDtypeStruct((1, KV_LORA), F32),
    ]
    tiles = [dk, dv, cq, ckv]
    whole = [gq, gkv, wq_t, w_ukv]
    return pl.pallas_call(
        body, name="bwd_qkv", grid=(nt,),
        in_specs=[_lane_tile(dq_t.shape)] + [_tile_spec(a.shape) for a in tiles] + [_whole_spec(a.shape) for a in whole]
        + [_tile_spec(cos.shape), _tile_spec(sin.shape), _lane_tile(cos_t.shape), _lane_tile(sin_t.shape)],
        out_specs=[_lane_tile(out_shapes[0].shape)] + [_tile_spec(s.shape) for s in out_shapes[1:5]]
        + [_acc_spec(s.shape) for s in out_shapes[5:]],
        out_shape=out_shapes,
        compiler_params=_params("arbitrary"),
    )(dq_t, *tiles, *whole, cos, sin, cos_t, sin_t)


def _bwd_conv(du1, ag, conv_w, dcq, dckv, dkr, n_rows):
    nt = n_rows // ROW_TILE

    last_tap = CONV_WIDTH - 1

    def body(du1_ref, dnext_ref, ag_ref, w_ref, dcq_ref, dckv_ref, dkr_ref, dz_ref, dw_ref,
             dext_ref, uext_ref, conv_ref, sums_ref):
        i = pl.program_id(0)

        @pl.when(i == 0)
        def _():
            sums_ref[...] = jnp.zeros_like(sums_ref)

        _to_planes(dext_ref, (), slice(0, ROW_TILE), du1_ref[...])
        _to_planes(dext_ref, (), slice(ROW_TILE, None), jnp.where(i == nt - 1, 0.0, dnext_ref[...]))
        ag_t = ag_ref[...]
        live = _row_ids(i, ROW_TILE) >= DEAD
        sg = _sigmoid(ag_t[:, D_CONV:])
        _to_planes(uext_ref, (), slice(None), jnp.where(live, ag_t[:, :D_CONV] * sg, 0.0))
        for c in range(CONV_PLANES):
            taps = w_ref[:, c * _LANES:(c + 1) * _LANES]
            for p in range(PHASES):
                u = uext_ref[c, _phase(p), :]
                acc = jnp.zeros((PHASE_ROWS, _LANES), F32)
                for k in range(CONV_WIDTH):
                    shifted = dext_ref[c, _phase(p + last_tap - k), :]
                    acc = acc + taps[k:k + 1, :] * shifted
                    sums_ref[c, k] += shifted * u
                conv_ref[c, _phase(p), :] = acc
        du0 = jnp.where(live, _from_planes(conv_ref, (), D_CONV), 0.0)
        da = du0 * sg
        dgate = du0 * ag_t[:, :D_CONV] * sg * (1.0 - sg)
        dz_ref[...] = jnp.concatenate(
            [da.astype(BF16), dgate.astype(BF16), dcq_ref[...], dckv_ref[...], dkr_ref[...]], axis=-1)

        @pl.when(i == nt - 1)
        def _():
            for c in range(CONV_PLANES):
                for k in range(CONV_WIDTH):
                    dw_ref[k:k + 1, c * _LANES:(c + 1) * _LANES] = jnp.sum(sums_ref[c, k], axis=0, keepdims=True)

    out_shapes = [jax.ShapeDtypeStruct((n_rows, D_IN), BF16), jax.ShapeDtypeStruct((CONV_WIDTH, D_CONV), F32)]
    return pl.pallas_call(
        body, name="bwd_conv", grid=(nt,),
        in_specs=[_tile_spec(du1.shape), _halo_after(du1.shape, CONV_HALO, n_rows), _tile_spec(ag.shape),
                  _whole_spec(conv_w.shape), _tile_spec(dcq.shape), _tile_spec(dckv.shape), _tile_spec(dkr.shape)],
        out_specs=[_tile_spec(out_shapes[0].shape), _acc_spec(out_shapes[1].shape)],
        out_shape=out_shapes,
        scratch_shapes=[pltpu.VMEM((CONV_PLANES, ROW_TILE + CONV_HALO, _LANES), F32),
                        pltpu.VMEM((CONV_PLANES, ROW_TILE, _LANES), F32), pltpu.VMEM((CONV_PLANES, ROW_TILE, _LANES), F32),
                        pltpu.VMEM((CONV_PLANES, CONV_WIDTH, PHASE_ROWS, _LANES), F32)],
        compiler_params=_params("arbitrary"),
    )(du1, du1, ag, conv_w, dcq, dckv, dkr)


def _bwd_in(dz, x, meta_pad, dh1, g1, w_in, n_rows):
    nt = n_rows // ROW_TILE

    def body(dz_ref, x_ref, meta_ref, dh1_ref, g_ref, w_ref, gx_ref, gmeta_ref, dg1_ref):
        i = pl.program_id(0)
        h0 = jnp.where(i == 0, meta_ref[...], x_ref[...])
        dx, dg1 = _rms_bwd(_dot(dz_ref[...], w_ref[...]), h0, g_ref[...])
        dh0 = dh1_ref[...] + dx
        gx_ref[...] = dh0

        @pl.when(i == 0)
        def _():
            gmeta_ref[...] = dh0

        _accumulate(dg1_ref, i == 0, dg1)

    out_shapes = [
        jax.ShapeDtypeStruct((n_rows - ROW_TILE, D_MODEL), F32),
        jax.ShapeDtypeStruct((ROW_TILE, D_MODEL), F32),
        jax.ShapeDtypeStruct((1, D_MODEL), F32),
    ]
    return pl.pallas_call(
        body, name="bwd_in", grid=(nt,),
        in_specs=[_tile_spec(dz.shape), _real_spec(D_MODEL), _whole_spec(meta_pad.shape), _tile_spec(dh1.shape),
                  _whole_spec(g1.shape), _whole_spec(w_in.shape)],
        out_specs=[_real_spec(D_MODEL), _acc_spec(out_shapes[1].shape), _acc_spec(out_shapes[2].shape)],
        out_shape=out_shapes,
        compiler_params=_params("arbitrary"),
    )(dz, x, meta_pad, dh1, g1, w_in)


def _contraction_tile(n_rows):
    return next(t for t in range(n_rows // 2 // _LANES * _LANES, 0, -_LANES) if n_rows % t == 0)


def _weight_grad(a, b, name, a_transposed=False):
    groups = max(a.shape[0] if a.ndim == 3 else 1, b.shape[0] if b.ndim == 3 else 1)
    n_rows, n = b.shape[-2], b.shape[-1]
    m = a.shape[-2] if a_transposed else a.shape[-1]
    kt = _contraction_tile(n_rows)
    steps = n_rows // kt

    def body(a_ref, b_ref, out_ref, acc_ref):
        i = pl.program_id(1)
        a_t, b_t = a_ref[...].astype(BF16), b_ref[...].astype(BF16)
        part = _dot(a_t, b_t) if a_transposed else _dot_tn(a_t, b_t)
        _accumulate(acc_ref, i == 0, part)

        @pl.when(i == steps - 1)
        def _():
            out_ref[...] = acc_ref[...].astype(out_ref.dtype)

    def spec(arr, rows_last):
        block = (arr.shape[-2], kt) if rows_last else (kt, arr.shape[-1])
        at = (lambda i: (0, i)) if rows_last else (lambda i: (i, 0))
        if arr.ndim == 3:
            return pl.BlockSpec((None,) + block, lambda g, i: (g,) + at(i))
        return pl.BlockSpec(block, lambda g, i: at(i))

    return pl.pallas_call(
        body, name=name, grid=(groups, steps),
        in_specs=[spec(a, a_transposed), spec(b, False)],
        out_specs=pl.BlockSpec((None, m, n), lambda g, i: (g, 0, 0)),
        out_shape=jax.ShapeDtypeStruct((groups, m, n), BF16),
        scratch_shapes=[pltpu.VMEM((m, n), F32)],
        compiler_params=_params("parallel", "arbitrary"),
    )(a, b)


def _my_index():
    return 4 * lax.axis_index("x") + 2 * lax.axis_index("y") + lax.axis_index("c")


def _peer(k):
    flip = lambda v, bit: 1 - v if bit else v
    px = flip(lax.axis_index("x"), k & 4)
    py = flip(lax.axis_index("y"), k & 2)
    pc = flip(lax.axis_index("c"), k & 1)
    return (px, py, pc), 4 * px + 2 * py + pc


def _all_gather(shards, dtypes):
    n = len(shards)

    def body(*refs):
        ins, outs, stages = refs[:n], refs[n:2 * n], refs[2 * n:3 * n]
        send_sems, recv_sems, local_sems = refs[3 * n:]
        me = _my_index()
        for a in range(n):
            stages[a][...] = ins[a][...].astype(stages[a].dtype)
        local = [pltpu.make_async_copy(stages[a], outs[a].at[me], local_sems.at[a]) for a in range(n)]
        for cp in local:
            cp.start()

        def copy(a, k, slot):
            peer, _ = _peer(k)
            return pltpu.make_async_remote_copy(
                src_ref=stages[a], dst_ref=outs[a].at[slot], send_sem=send_sems.at[a, k - 1],
                recv_sem=recv_sems.at[a, k - 1], device_id=peer, device_id_type=MESH)

        for k in range(1, N_DEV):
            for a in range(n):
                copy(a, k, me).start()
        for k in range(1, N_DEV):
            for a in range(n):
                copy(a, k, _peer(k)[1]).wait()
        for cp in local:
            cp.wait()

    return pl.pallas_call(
        body, name="gather_weights",
        in_specs=[pl.BlockSpec(memory_space=pltpu.VMEM)] * n,
        out_specs=[pl.BlockSpec(memory_space=pl.ANY)] * n,
        out_shape=[jax.ShapeDtypeStruct((N_DEV,) + s.shape, dt) for s, dt in zip(shards, dtypes)],
        scratch_shapes=[pltpu.VMEM(s.shape, dt) for s, dt in zip(shards, dtypes)]
        + [pltpu.SemaphoreType.DMA((n, N_DEV - 1)), pltpu.SemaphoreType.DMA((n, N_DEV - 1)), pltpu.SemaphoreType.DMA((n,))],
        compiler_params=pltpu.CompilerParams(vmem_limit_bytes=VMEM_LIMIT),
    )(*shards)


def _exchange(parts, whole):
    n = len(parts)

    def body(*refs):
        ins, outs = refs[:n], refs[n:2 * n]
        send_sems, recv_sems, local_sems = refs[2 * n:]
        me = _my_index()

        def src(a, slab):
            return ins[a] if whole[a] else ins[a].at[slab]

        local = [pltpu.make_async_copy(src(a, me), outs[a].at[me], local_sems.at[a]) for a in range(n)]
        for cp in local:
            cp.start()

        def copy(a, k, slab, slot):
            peer, _ = _peer(k)
            return pltpu.make_async_remote_copy(
                src_ref=src(a, slab), dst_ref=outs[a].at[slot], send_sem=send_sems.at[a, k - 1],
                recv_sem=recv_sems.at[a, k - 1], device_id=peer, device_id_type=MESH)

        for k in range(1, N_DEV):
            for a in range(n):
                copy(a, k, _peer(k)[1], me).start()
        for k in range(1, N_DEV):
            for a in range(n):
                copy(a, k, _peer(k)[1], _peer(k)[1]).wait()
        for cp in local:
            cp.wait()

    return pl.pallas_call(
        body, name="exchange_grads",
        in_specs=[pl.BlockSpec(memory_space=pl.ANY)] * n,
        out_specs=[pl.BlockSpec(memory_space=pl.ANY)] * n,
        out_shape=[jax.ShapeDtypeStruct(((N_DEV,) + p.shape) if w else p.shape, p.dtype) for p, w in zip(parts, whole)],
        scratch_shapes=[pltpu.SemaphoreType.DMA((n, N_DEV - 1)), pltpu.SemaphoreType.DMA((n, N_DEV - 1)),
                        pltpu.SemaphoreType.DMA((n,))],
    )(*parts)


def _sequencer_exchange(parts, whole, name, collective_id):
    n = len(parts)
    srcs = [jax.new_ref(p, memory_space=pltpu.MemorySpace.HBM) for p in parts]
    lands = [jax.empty_ref(jax.ShapeDtypeStruct(((N_DEV,) + p.shape) if w else p.shape, p.dtype),
                           memory_space=pltpu.MemorySpace.HBM) for p, w in zip(parts, whole)]

    @pl.kernel(mesh=plsc.ScalarSubcoreMesh(axis_name="sequencer", num_cores=1), name=name,
               scratch_types=(pltpu.SemaphoreType.DMA((n, N_DEV - 1)), pltpu.SemaphoreType.DMA((n, N_DEV - 1)),
                              pltpu.SemaphoreType.DMA((n,))),
               compiler_params=pltpu.CompilerParams(collective_id=collective_id))
    def launch(send_sems, recv_sems, local_sems):
        barrier = pltpu.get_barrier_semaphore()
        for k in range(1, N_DEV):
            pl.semaphore_signal(barrier, inc=1, device_id=_peer(k)[0], device_id_type=MESH)
        pl.semaphore_wait(barrier, N_DEV - 1)
        me = _my_index()

        def src(a, slab):
            return srcs[a] if whole[a] else srcs[a].at[slab]

        local = [pltpu.make_async_copy(src(a, me), lands[a].at[me], local_sems.at[a]) for a in range(n)]
        for cp in local:
            cp.start()

        def copy(a, k, slab, slot):
            return pltpu.make_async_remote_copy(
                src_ref=src(a, slab), dst_ref=lands[a].at[slot], send_sem=send_sems.at[a, k - 1],
                recv_sem=recv_sems.at[a, k - 1], device_id=_peer(k)[0], device_id_type=MESH)

        for k in range(1, N_DEV):
            for a in range(n):
                copy(a, k, _peer(k)[1], me).start()
        for k in range(1, N_DEV):
            for a in range(n):
                copy(a, k, _peer(k)[1], _peer(k)[1]).wait()
        for cp in local:
            cp.wait()

    launch()
    return [land[...] for land in lands]


def _row_block(rows):
    if rows <= ROW_TILE:
        return rows
    return next(rb for rb in range(ROW_TILE, 0, -16) if rows % rb == 0)


def _adamw(landing, w, m, v, name):
    rows, cols = w.shape
    rb = _row_block(rows)

    def body(l_ref, w_ref, m_ref, v_ref, g_ref, d_ref, m2_ref, v2_ref):
        g = l_ref[0].astype(F32)
        for p in range(1, N_DEV):
            g = g + l_ref[p].astype(F32)
        g_ref[...] = g
        d_ref[...], m2_ref[...], v2_ref[...] = _adamw_step(g, w_ref[...], m_ref[...], v_ref[...])

    flat = pl.BlockSpec((rb, cols), lambda i: (i, 0))
    return pl.pallas_call(
        body, name=name, grid=(rows // rb,),
        in_specs=[pl.BlockSpec((N_DEV, rb, cols), lambda i: (0, i, 0)), flat, flat, flat],
        out_specs=[flat] * 4,
        out_shape=[jax.ShapeDtypeStruct((rows, cols), F32)] * 4,
        compiler_params=_params("parallel"),
    )(landing, w, m, v)


def _adamw_step(g, w, m, v):
    m2 = ADAM_B1 * m + (1.0 - ADAM_B1) * g
    v2 = ADAM_B2 * v + (1.0 - ADAM_B2) * (g * g)
    m_hat = m2 / (1.0 - ADAM_B1 ** ADAM_STEP)
    v_hat = v2 / (1.0 - ADAM_B2 ** ADAM_STEP)
    return -ADAM_LR * (m_hat / (jnp.sqrt(v_hat) + ADAM_EPS) + ADAM_WD * w), m2, v2


_REPLICATED = (
    ("mix_norm_g", D_MODEL), ("q_norm_g", Q_LORA), ("kv_norm_g", KV_LORA), ("conv_b", D_CONV), ("conv_ln_g", D_CONV),
    ("conv_ln_b", D_CONV), ("conv_out_g", D_CONV), ("attn_out_g", D_CONV), ("ffn_norm_g", D_MODEL),
    ("ffn_conv_b", D_UP), ("final_norm_g", D_MODEL),
)
_REPLICATED_WIDTH = sum(size for _, size in _REPLICATED) + _LANES

_WEIGHT_ORDER = (
    "meta_tokens", "mix_norm_g", "w_in", "q_norm_g", "w_uq", "kv_norm_g", "w_ukv", "conv_w", "conv_b", "conv_ln_g",
    "conv_ln_b", "conv_out_g", "attn_out_g", "w_out", "ffn_norm_g", "w_ffn_up", "ffn_conv_w", "ffn_conv_b",
    "w_ffn_down", "final_norm_g",
)


def _pack_replicated(grads, loss):
    rows = [grads[name].reshape(1, size) for name, size in _REPLICATED]
    return jnp.concatenate(rows + [jnp.broadcast_to(loss.reshape(1, 1), (1, _LANES))], axis=-1)


def _adamw_replicated(landing, weights, moments_m, moments_v):
    n = len(_REPLICATED)

    def body(*refs):
        l_ref, ins, outs = refs[0], refs[1:1 + 3 * n], refs[1 + 3 * n:]
        total = l_ref[0]
        for p in range(1, N_DEV):
            total = total + l_ref[p]
        at = 0
        for a, (_, size) in enumerate(_REPLICATED):
            g = total[:, at:at + size]
            w_ref, m_ref, v_ref = ins[3 * a:3 * a + 3]
            g_ref, d_ref, m2_ref, v2_ref = outs[4 * a:4 * a + 4]
            g_ref[...] = g
            d_ref[...], m2_ref[...], v2_ref[...] = _adamw_step(g, w_ref[...], m_ref[...], v_ref[...])
            at += size
        outs[-1][...] = total[:, at:at + _LANES]

    operands, out_shapes = [], []
    for name, size in _REPLICATED:
        operands += [weights[name].reshape(1, size), moments_m[name].reshape(1, size), moments_v[name].reshape(1, size)]
        out_shapes += [jax.ShapeDtypeStruct((1, size), F32)] * 4
    out_shapes.append(jax.ShapeDtypeStruct((1, _LANES), F32))
    outs = pl.pallas_call(body, name="adamw_replicated", out_shape=out_shapes)(landing, *operands)
    return outs[-1][0, 0], {name: outs[4 * a:4 * a + 4] for a, (name, _) in enumerate(_REPLICATED)}


def _pad_rows(a, rows):
    return jnp.pad(a, ((0, rows - a.shape[0]), (0, 0)))


def _slabs(a):
    r, c = a.shape
    return a.reshape(r, N_DEV, c // N_DEV).transpose(1, 0, 2)


def _unslab(a):
    g, r, c = a.shape
    return a.transpose(1, 0, 2).reshape(r, g * c)


def _local_step(x, target, w, n_rows, ffn_weights, send_grads):
    cos_t, sin_t = _rope_tables(n_rows)
    cos, sin = cos_t.T, sin_t.T
    meta_pad, g1, gf = w["meta_pad"], w["mix_norm_g"], w["final_norm_g"]
    gq, gkv, gb_col = w["q_norm_g"], w["kv_norm_g"], w["attn_out_g"].reshape(D_ATTN, 1)
    nb, ag, cq, ckv, kr = _fwd_in(x, meta_pad, g1, w["w_in"], n_rows)
    mix_a, u1 = _fwd_conv(ag, w["conv_w"], w["conv_b"], w["conv_ln_g"], w["conv_ln_b"], w["conv_out_g"], n_rows)
    q_t, k, v, v_t, cqn, ckvn = _fwd_qkv(cq, ckv, kr, gq, gkv, w["wq_t"], w["w_ukv"], w["wv_t"], cos, sin, cos_t, sin_t, n_rows)
    o_t, lse = _attn_fwd(q_t, k, v_t, n_rows)
    w_out, w_up, w_down = ffn_weights()
    mix_bt, h1 = _fwd_out(x, meta_pad, mix_a, o_t, gb_col, w_out, n_rows)
    n2, up0, act, da, db, dh2, loss, dgf = _fwd_ffn(
        h1, target, w["ffn_norm_g"], w_up, w["fw"], w["fb"], w_down, gf, n_rows)

    dup, dfb = _bwd_ffn_act(dh2, da, db, w_down, n_rows)
    dup0, dh1, dfw, dg2 = _bwd_ffn_up(dup, up0, h1, dh2, w["ffn_norm_g"], w_up, w["fw"], n_rows)
    grad_w_out = jnp.concatenate([_weight_grad(mix_a, dh1, "grad_w_out_conv")[0],
                                  _weight_grad(mix_bt, dh1, "grad_w_out_attn", a_transposed=True)[0]], axis=0)
    stage0 = {
        "w_ffn_up": _weight_grad(dup0, n2, "grad_w_ffn_up"),
        "w_ffn_down": _weight_grad(act, dh2, "grad_w_ffn_down").reshape(N_DEV, D_FF // N_DEV, D_MODEL),
        "w_out": grad_w_out.reshape(N_DEV, D_MODEL // N_DEV, D_MODEL),
    }
    stage0, dh1 = lax.optimization_barrier((stage0, dh1))
    send_grads(0, stage0)
    do_t, delta, du1, dgb, dga, dlg, dlb, dcb = _bwd_out(
        dh1, o_t, u1, w_out, gb_col, w["conv_ln_g"], w["conv_ln_b"], w["conv_out_g"], n_rows)
    dq_t, dk, dv = _attn_bwd(q_t, k, v, do_t, lse, delta, n_rows)
    dqraw_t, dkv, dcq, dckv, dkr, dgq, dgkv = _bwd_qkv(
        dq_t, dk, dv, cq, ckv, gq, gkv, w["wq_t"], w["w_ukv"], cos, sin, cos_t, sin_t, n_rows)
    dz, dcw = _bwd_conv(du1, ag, w["conv_w"], dcq, dckv, dkr, n_rows)
    stage1 = {
        "w_in": _weight_grad(dz, nb, "grad_w_in")[0].reshape(N_DEV, D_IN // N_DEV, D_MODEL),
        "w_uq": _weight_grad(dqraw_t.reshape(N_HEADS * QK_DIM, n_rows), cqn, "grad_w_uq", a_transposed=True)[0].reshape(
            N_HEADS, QK_DIM, Q_LORA),
        "w_ukv": _slabs(_weight_grad(ckvn, dkv, "grad_w_ukv")[0]),
        "conv_w": _slabs(dcw),
        "ffn_conv_w": dfw[:, :, :UP_SLAB],
    }
    stage1, dz = lax.optimization_barrier((stage1, dz))
    send_grads(1, stage1)
    gx, gmeta, dg1 = _bwd_in(dz, x, meta_pad, dh1, g1, w["w_in"], n_rows)

    sharded = {"meta_tokens": _slabs(gmeta[DEAD:])}
    replicated = {
        "mix_norm_g": dg1, "q_norm_g": dgq, "kv_norm_g": dgkv, "conv_b": dcb, "conv_ln_g": dlg, "conv_ln_b": dlb,
        "conv_out_g": dga, "attn_out_g": dgb, "ffn_norm_g": dg2, "ffn_conv_b": dfb, "final_norm_g": dgf,
    }
    return loss[0, 0], gx, sharded, replicated


_SHARDED = (
    ("w_in", None, BF16), ("w_uq", None, BF16), ("w_ukv", None, BF16), ("w_out", None, BF16), ("w_ffn_up", None, BF16),
    ("w_ffn_down", None, BF16), ("conv_w", 32, F32), ("ffn_conv_w", 8, F32), ("meta_tokens", None, F32),
)
GATHER_LATE_ID = 3
EXCHANGE_STAGE_IDS = (4, 5)
_LATE_WEIGHTS = ("w_out", "w_ffn_up", "w_ffn_down")
_COLUMN_SHARDS = ("w_in", "w_uq", "w_ffn_up")


def kernel(x, meta_tokens, mix_norm_g, w_in, q_norm_g, w_uq, kv_norm_g, w_ukv, conv_w, conv_b, conv_ln_g, conv_ln_b, conv_out_g, attn_out_g, w_out, ffn_norm_g, w_ffn_up, ffn_conv_w, ffn_conv_b, w_ffn_down, final_norm_g, loss_target, m_meta_tokens, m_mix_norm_g, m_w_in, m_q_norm_g, m_w_uq, m_kv_norm_g, m_w_ukv, m_conv_w, m_conv_b, m_conv_ln_g, m_conv_ln_b, m_conv_out_g, m_attn_out_g, m_w_out, m_ffn_norm_g, m_w_ffn_up, m_ffn_conv_w, m_ffn_conv_b, m_w_ffn_down, m_final_norm_g, v_meta_tokens, v_mix_norm_g, v_w_in, v_q_norm_g, v_w_uq, v_kv_norm_g, v_w_ukv, v_conv_w, v_conv_b, v_conv_ln_g, v_conv_ln_b, v_conv_out_g, v_attn_out_g, v_w_out, v_ffn_norm_g, v_w_ffn_up, v_ffn_conv_w, v_ffn_conv_b, v_w_ffn_down, v_final_norm_g):
    given = dict(locals())
    weights = {name: given[name] for name in _WEIGHT_ORDER}
    moments_m = {name: given["m_" + name] for name in _WEIGHT_ORDER}
    moments_v = {name: given["v_" + name] for name in _WEIGHT_ORDER}
    seq = x.shape[1]
    n_rows = ROW_TILE + seq

    def shard2d(name, a):
        a = a.reshape(a.shape[-2], a.shape[-1])
        return a.T if name in _COLUMN_SHARDS else a

    early = [entry for entry in _SHARDED if entry[0] not in _LATE_WEIGHTS]
    shards = []
    for name, pad_to, _ in early:
        s = shard2d(name, weights[name])
        shards.append(s if pad_to is None else _pad_rows(s, pad_to))
    gathered = dict(zip([name for name, _, _ in early], _all_gather(shards, [dt for _, _, dt in early])))
    behind = gathered["meta_tokens"][0, 0, 0] * 0.0
    late_parts = [(shard2d(name, weights[name]) + behind).astype(BF16) for name in _LATE_WEIGHTS]
    late = _sequencer_exchange(late_parts, [True] * len(late_parts), "gather_late", GATHER_LATE_ID)
    meta_full = _unslab(gathered["meta_tokens"])
    full = {
        "meta_pad": jnp.concatenate([jnp.zeros((DEAD, D_MODEL), F32), meta_full], axis=0),
        "w_in": gathered["w_in"].reshape(D_IN, D_MODEL),
        "wq_t": gathered["w_uq"],
        "w_ukv": gathered["w_ukv"],
        "wv_t": gathered["w_ukv"][:, :, QK_NOPE:].transpose(0, 2, 1),
        "conv_w": _unslab(gathered["conv_w"][:, :CONV_WIDTH]),
        "fw": jnp.pad(gathered["ffn_conv_w"][:, :FFN_CONV_WIDTH], ((0, 0), (0, 0), (0, UP_PAD - UP_SLAB))),
        "fb": jnp.pad(ffn_conv_b.reshape(N_DEV, 1, UP_SLAB), ((0, 0), (0, 0), (0, UP_PAD - UP_SLAB))),
        "final_norm_g": final_norm_g.reshape(1, D_MODEL),
    }
    for name in ("mix_norm_g", "q_norm_g", "kv_norm_g", "conv_b", "conv_ln_g", "conv_ln_b", "conv_out_g", "attn_out_g",
                 "ffn_norm_g"):
        full[name] = weights[name]

    def ffn_weights():
        w_out_all, w_up_all, w_down_all = late
        return (w_out_all.reshape(D_MODEL, D_MODEL), w_up_all, w_down_all.reshape(N_ACT_SLAB, UP_SLAB, D_MODEL))

    wire = {name: (pad_to, dt) for name, pad_to, dt in _SHARDED}
    landing = {}

    def on_the_wire(name, slabs):
        pad_to, dt = wire[name]
        slabs = slabs.astype(dt)
        return slabs if pad_to is None else jnp.pad(slabs, ((0, 0), (0, pad_to - slabs.shape[1]), (0, 0)))

    def send_grads(stage, grads):
        parts = [on_the_wire(name, slabs) for name, slabs in grads.items()]
        if landing:
            arrived = list(landing)
            parts, held = lax.optimization_barrier((parts, [landing[name] for name in arrived]))
            landing.update(zip(arrived, held))
        landed = _sequencer_exchange(parts, [False] * len(parts), f"exchange_stage{stage}", EXCHANGE_STAGE_IDS[stage])
        landing.update(zip(grads, landed))

    loss, gx, sharded, replicated = _local_step(x[0], loss_target[0], full, n_rows, ffn_weights, send_grads)

    parts = [on_the_wire(name, slabs) for name, slabs in sharded.items()] + [_pack_replicated(replicated, loss)]
    landed = _exchange(parts, [False] * len(sharded) + [True])
    landing.update(zip(sharded, landed[:-1]))

    grad, delta, new_m, new_v = {}, {}, {}, {}
    for name, pad_to, _ in _SHARDED:
        land = landing[name]
        ws, ms, vs = (shard2d(name, a[name]) for a in (weights, moments_m, moments_v))
        rows = ws.shape[0]
        if pad_to is not None:
            ws, ms, vs = _pad_rows(ws, pad_to), _pad_rows(ms, pad_to), _pad_rows(vs, pad_to)
        outs = _adamw(land, ws, ms, vs, "adamw_" + name)
        shape = weights[name].shape
        grad[name], delta[name], new_m[name], new_v[name] = (
            (o.T if name in _COLUMN_SHARDS else o[:rows]).reshape(shape) for o in outs)
    loss, updates = _adamw_replicated(landed[-1], weights, moments_m, moments_v)
    for name, outs in updates.items():
        grad[name], delta[name], new_m[name], new_v[name] = (o.reshape(weights[name].shape) for o in outs)

    return (loss, gx[None], *[grad[n] for n in _WEIGHT_ORDER], *[delta[n] for n in _WEIGHT_ORDER],
            *[new_m[n] for n in _WEIGHT_ORDER], *[new_v[n] for n in _WEIGHT_ORDER])
```

```python
import functools

import jax
import jax.numpy as jnp
from jax import lax
from jax.experimental import pallas as pl
from jax.experimental.pallas import tpu as pltpu
from jax.experimental.pallas import tpu_sc as plsc

F32 = jnp.float32
BF16 = jnp.bfloat16

N_DEV = 8
D_MODEL = 1024
CHUNK = 64
CHUNK_SHIFT = 6
N_META = 16
D_CONV = 512
CONV_WIDTH = 31
N_HEADS = 8
QK_NOPE = 64
QK_ROPE = 32
QK_DIM = QK_NOPE + QK_ROPE
V_HEAD = 64
KV_HEAD = QK_NOPE + V_HEAD
D_ATTN = N_HEADS * V_HEAD
Q_LORA = 384
KV_LORA = 256
ROPE_THETA = 10000.0
D_IN = 2 * D_CONV + Q_LORA + KV_LORA + QK_ROPE
D_FF = 2816
D_UP = 2 * D_FF
FFN_CONV_WIDTH = 3
UP_SLAB = D_UP // N_DEV
N_ACT_SLAB = D_FF // UP_SLAB
EPS = 1e-6
NEG = -1e30
_LN2 = 0.6931471805599453
QK_LOGIT_SCALE = QK_DIM ** -0.5 / _LN2
ADAM_LR = 0.001
ADAM_B1 = 0.9
ADAM_B2 = 0.999
ADAM_EPS = 1e-08
ADAM_WD = 0.01
ADAM_STEP = 10

ROW_TILE = 256
DEAD = ROW_TILE - N_META
CONV_HALO = 32
FFN_HALO = 16
VMEM_LIMIT = 56 * 1024 * 1024
_LANES = 128

MESH = pl.DeviceIdType.MESH


def _dot(a, b):
    return jnp.dot(a, b, preferred_element_type=F32)


def _dot_nt(a, b):
    return lax.dot_general(a, b, (((1,), (1,)), ((), ())), preferred_element_type=F32)


def _dot_tn(a, b):
    return lax.dot_general(a, b, (((0,), (0,)), ((), ())), preferred_element_type=F32)


def _sigmoid(x):
    return 1.0 / (1.0 + jnp.exp2(x * (-1.0 / _LN2)))


def _rms_fwd(x, g):
    r = lax.rsqrt(jnp.mean(x * x, axis=-1, keepdims=True) + EPS)
    return x * r * g


def _rms_bwd(dy, x, g):
    r = lax.rsqrt(jnp.mean(x * x, axis=-1, keepdims=True) + EPS)
    w = dy * g
    dx = r * w - x * (r * r * r) * jnp.mean(w * x, axis=-1, keepdims=True)
    return dx, jnp.sum(dy * x * r, axis=0, keepdims=True)


def _rope(x, cos, sin):
    half = QK_ROPE // 2
    x1, x2 = x[:, :half], x[:, half:]
    return jnp.concatenate([x1 * cos - x2 * sin, x2 * cos + x1 * sin], axis=-1)


def _rope_t(dy, cos, sin):
    half = QK_ROPE // 2
    d1, d2 = dy[:, :half], dy[:, half:]
    return jnp.concatenate([d1 * cos + d2 * sin, d2 * cos - d1 * sin], axis=-1)


def _row_ids(i, rows):
    return i * rows + lax.broadcasted_iota(jnp.int32, (rows, 1), 0)


def _accumulate(ref, first, value):
    @pl.when(first)
    def _():
        ref[...] = value

    @pl.when(jnp.logical_not(first))
    def _():
        ref[...] += value


def _tile_spec(shape):
    nd = len(shape)
    if nd == 2:
        return pl.BlockSpec((ROW_TILE, shape[1]), lambda i: (i, 0))
    return pl.BlockSpec((shape[0], ROW_TILE, shape[2]), lambda i: (0, i, 0))


def _whole_spec(shape):
    nd = len(shape)
    return pl.BlockSpec(tuple(shape), lambda i: (0,) * nd, pipeline_mode=pl.Buffered(1))


def _acc_spec(shape):
    nd = len(shape)
    return pl.BlockSpec(tuple(shape), lambda i: (0,) * nd)


def _real_spec(width):
    return pl.BlockSpec((ROW_TILE, width), lambda i: (jnp.maximum(i - 1, 0), 0))


def _params(*semantics):
    return pltpu.CompilerParams(dimension_semantics=semantics, vmem_limit_bytes=VMEM_LIMIT)


def _fwd_in(x, meta_pad, g1, w_in, n_rows):
    nt = n_rows // ROW_TILE

    def body(x_ref, meta_ref, g_ref, w_ref, nb_ref, ag_ref, cq_ref, ckv_ref, kr_ref):
        i = pl.program_id(0)
        h0 = jnp.where(i == 0, meta_ref[...], x_ref[...])
        nb = _rms_fwd(h0, g_ref[...]).astype(BF16)
        nb_ref[...] = nb
        z = _dot_nt(nb, w_ref[...])
        ag_ref[...] = z[:, :2 * D_CONV]
        cq_ref[...] = z[:, 2 * D_CONV:2 * D_CONV + Q_LORA]
        ckv_ref[...] = z[:, 2 * D_CONV + Q_LORA:2 * D_CONV + Q_LORA + KV_LORA]
        kr_ref[...] = z[:, 2 * D_CONV + Q_LORA + KV_LORA:]

    out_shapes = [
        jax.ShapeDtypeStruct((n_rows, D_MODEL), BF16),
        jax.ShapeDtypeStruct((n_rows, 2 * D_CONV), F32),
        jax.ShapeDtypeStruct((n_rows, Q_LORA), F32),
        jax.ShapeDtypeStruct((n_rows, KV_LORA), F32),
        jax.ShapeDtypeStruct((n_rows, QK_ROPE), F32),
    ]
    return pl.pallas_call(
        body, name="fwd_in", grid=(nt,),
        in_specs=[_real_spec(D_MODEL), _whole_spec(meta_pad.shape), _whole_spec(g1.shape), _whole_spec(w_in.shape)],
        out_specs=[_tile_spec(s.shape) for s in out_shapes],
        out_shape=out_shapes,
        compiler_params=_params("parallel"),
    )(x, meta_pad, g1, w_in)


def _conv_chain(u1, ln_g, ln_b):
    mu = jnp.mean(u1, axis=-1, keepdims=True)
    xc = u1 - mu
    rstd = lax.rsqrt(jnp.mean(xc * xc, axis=-1, keepdims=True) + EPS)
    xh = xc * rstd
    u2 = xh * ln_g + ln_b
    return xh, u2, u2 * _sigmoid(u2), rstd


def _fwd_conv(ag, conv_w, conv_b, ln_g, ln_b, out_g, n_rows):
    nt = n_rows // ROW_TILE

    def body(ag_ref, w_ref, b_ref, lg_ref, lb_ref, og_ref, mix_ref, u1_ref, ext_ref, conv_ref):
        i = pl.program_id(0)

        @pl.when(i == 0)
        def _():
            ext_ref[:, 0:CONV_HALO, :] = jnp.zeros((CONV_PLANES, CONV_HALO, _LANES), F32)

        ag_t = ag_ref[...]
        live = _row_ids(i, ROW_TILE) >= DEAD
        u0 = jnp.where(live, ag_t[:, :D_CONV] * _sigmoid(ag_t[:, D_CONV:]), 0.0)
        _to_planes(ext_ref, (), slice(CONV_HALO, None), u0)
        first = CONV_HALO - (CONV_WIDTH - 1)
        for c in range(CONV_PLANES):
            taps = w_ref[:, c * _LANES:(c + 1) * _LANES]
            for p in range(PHASES):
                acc = jnp.zeros((PHASE_ROWS, _LANES), F32)
                for k in range(CONV_WIDTH):
                    acc = acc + taps[k:k + 1, :] * ext_ref[c, _phase(first + k + p), :]
                conv_ref[c, _phase(p), :] = acc
        ext_ref[:, 0:CONV_HALO, :] = ext_ref[:, ROW_TILE:ROW_TILE + CONV_HALO, :]
        u1 = _from_planes(conv_ref, (), D_CONV) + b_ref[...]
        u1_ref[...] = u1
        _, _, u3, _ = _conv_chain(u1, lg_ref[...], lb_ref[...])
        mix_ref[...] = _rms_fwd(u3, og_ref[...]).astype(BF16)

    out_shapes = [jax.ShapeDtypeStruct((n_rows, D_CONV), BF16), jax.ShapeDtypeStruct((n_rows, D_CONV), F32)]
    small = [conv_w, conv_b, ln_g, ln_b, out_g]
    return pl.pallas_call(
        body, name="fwd_conv", grid=(nt,),
        in_specs=[_tile_spec(ag.shape)] + [_whole_spec(a.shape) for a in small],
        out_specs=[_tile_spec(s.shape) for s in out_shapes],
        out_shape=out_shapes,
        scratch_shapes=[pltpu.VMEM((CONV_PLANES, ROW_TILE + CONV_HALO, _LANES), F32),
                        pltpu.VMEM((CONV_PLANES, ROW_TILE, _LANES), F32)],
        compiler_params=_params("arbitrary"),
    )(ag, *small)


def _lane_tile(shape):
    if len(shape) == 2:
        return pl.BlockSpec((shape[0], ROW_TILE), lambda i: (0, i))
    return pl.BlockSpec((shape[0], shape[1], ROW_TILE), lambda i: (0, 0, i))


def _rope_rows(x, cos, sin):
    half = QK_ROPE // 2
    x1, x2 = x[:half], x[half:]
    return jnp.concatenate([x1 * cos - x2 * sin, x2 * cos + x1 * sin], axis=0)


def _rope_rows_t(dy, cos, sin):
    half = QK_ROPE // 2
    d1, d2 = dy[:half], dy[half:]
    return jnp.concatenate([d1 * cos + d2 * sin, d2 * cos - d1 * sin], axis=0)


def _fwd_qkv(cq, ckv, kr, gq, gkv, wq_t, w_ukv, wv_t, cos, sin, cos_t, sin_t, n_rows):
    nt = n_rows // ROW_TILE

    def body(cq_ref, ckv_ref, kr_ref, gq_ref, gkv_ref, wqt_ref, wkv_ref, wvt_ref, cos_ref, sin_ref, cost_ref, sint_ref,
             qt_ref, k_ref, v_ref, vt_ref, cqn_ref, ckvn_ref):
        cqn = _rms_fwd(cq_ref[...], gq_ref[...]).astype(BF16)
        ckvn = _rms_fwd(ckv_ref[...], gkv_ref[...]).astype(BF16)
        cqn_ref[...] = cqn
        ckvn_ref[...] = ckvn
        k_rot = _rope(kr_ref[...], cos_ref[...], sin_ref[...])
        cos_rows, sin_rows = cost_ref[...], sint_ref[...]
        for h in range(N_HEADS):
            q_raw = _dot_nt(wqt_ref[h], cqn)
            q_h = jnp.concatenate([q_raw[:QK_NOPE], _rope_rows(q_raw[QK_NOPE:], cos_rows, sin_rows)], axis=0)
            qt_ref[h] = (q_h * QK_LOGIT_SCALE).astype(BF16)
            kv = _dot(ckvn, wkv_ref[h])
            k_ref[h] = jnp.concatenate([kv[:, :QK_NOPE], k_rot], axis=-1).astype(BF16)
            v_ref[h] = kv[:, QK_NOPE:].astype(BF16)
            vt_ref[h] = _dot_nt(wvt_ref[h], ckvn).astype(BF16)

    out_shapes = [
        jax.ShapeDtypeStruct((N_HEADS, QK_DIM, n_rows), BF16),
        jax.ShapeDtypeStruct((N_HEADS, n_rows, QK_DIM), BF16),
        jax.ShapeDtypeStruct((N_HEADS, n_rows, V_HEAD), BF16),
        jax.ShapeDtypeStruct((N_HEADS, V_HEAD, n_rows), BF16),
        jax.ShapeDtypeStruct((n_rows, Q_LORA), BF16),
        jax.ShapeDtypeStruct((n_rows, KV_LORA), BF16),
    ]
    tiles = [cq, ckv, kr]
    whole = [gq, gkv, wq_t, w_ukv, wv_t]
    out_specs = [_lane_tile(out_shapes[0].shape), _tile_spec(out_shapes[1].shape), _tile_spec(out_shapes[2].shape),
                 _lane_tile(out_shapes[3].shape), _tile_spec(out_shapes[4].shape), _tile_spec(out_shapes[5].shape)]
    return pl.pallas_call(
        body, name="fwd_qkv", grid=(nt,),
        in_specs=[_tile_spec(a.shape) for a in tiles] + [_whole_spec(a.shape) for a in whole]
        + [_tile_spec(cos.shape), _tile_spec(sin.shape), _lane_tile(cos_t.shape), _lane_tile(sin_t.shape)],
        out_specs=out_specs,
        out_shape=out_shapes,
        compiler_params=_params("parallel"),
    )(*tiles, *whole, cos, sin, cos_t, sin_t)


def _chunk_of(rows):
    return jnp.where(rows >= ROW_TILE, lax.shift_right_arithmetic(rows - ROW_TILE, CHUNK_SHIFT) + 1, 0)


def _visible(i, j):
    k_rows = j * ROW_TILE + lax.broadcasted_iota(jnp.int32, (ROW_TILE, 1), 0)
    q_rows = i * ROW_TILE + lax.broadcasted_iota(jnp.int32, (1, ROW_TILE), 1)
    return jnp.logical_and(_chunk_of(q_rows) >= _chunk_of(k_rows), k_rows >= DEAD)


def _attn_fwd(q_t, k, v_t, n_rows):
    nt = n_rows // ROW_TILE

    def body(qt_ref, k_ref, vt_ref, ot_ref, lse_ref):
        i = pl.program_id(0)
        q_ts = [qt_ref[h] for h in range(N_HEADS)]

        def make_step(masked):
            def step(j, carry):
                rows = pl.ds(pl.multiple_of(j * ROW_TILE, ROW_TILE), ROW_TILE)
                scores = [_dot(k_ref[h, rows, :], q_ts[h]) for h in range(N_HEADS)]
                visible = _visible(i, j) if masked else None
                probs, state = [], []
                for h in range(N_HEADS):
                    m, l, _ = carry[h]
                    s = jnp.where(visible, scores[h], NEG) if masked else scores[h]
                    m_new = jnp.maximum(m, jnp.max(s, axis=0, keepdims=True))
                    alpha = jnp.exp2(m - m_new)
                    p = jnp.exp2(s - m_new)
                    probs.append(p.astype(BF16))
                    state.append((m_new, alpha * l + jnp.sum(p, axis=0, keepdims=True), alpha))
                outs = [_dot(vt_ref[h, :, rows], probs[h]) for h in range(N_HEADS)]
                return tuple((state[h][0], state[h][1], state[h][2] * carry[h][2] + outs[h]) for h in range(N_HEADS))
            return step

        init = tuple((jnp.full((1, ROW_TILE), NEG, F32), jnp.zeros((1, ROW_TILE), F32),
                      jnp.zeros((V_HEAD, ROW_TILE), F32)) for _ in range(N_HEADS))
        carry = make_step(True)(0, init)
        carry = lax.fori_loop(1, i, make_step(False), carry)
        carry = lax.fori_loop(jnp.maximum(i, 1), i + 1, make_step(True), carry)
        for h in range(N_HEADS):
            m, l, acc = carry[h]
            ot_ref[h] = acc / l
            lse_ref[h] = m + jnp.log2(l)

    out_shapes = [jax.ShapeDtypeStruct((N_HEADS, V_HEAD, n_rows), F32), jax.ShapeDtypeStruct((N_HEADS, 1, n_rows), F32)]
    return pl.pallas_call(
        body, name="attn_fwd", grid=(nt,),
        in_specs=[_lane_tile(q_t.shape), _whole_spec(k.shape), _whole_spec(v_t.shape)],
        out_specs=[_lane_tile(s.shape) for s in out_shapes],
        out_shape=out_shapes,
        compiler_params=_params("parallel"),
    )(q_t, k, v_t)


def _heads_to_rows(ref):
    return jnp.concatenate([ref[h] for h in range(N_HEADS)], axis=0)


def _rms_cols(x, g_col):
    r = lax.rsqrt(jnp.mean(x * x, axis=0, keepdims=True) + EPS)
    return x * r * g_col


def _fwd_out(x, meta_pad, mix_a, o_t, gb_col, w_out, n_rows):
    nt = n_rows // ROW_TILE

    def body(x_ref, meta_ref, mixa_ref, ot_ref, gb_ref, w_ref, mixbt_ref, h1_ref):
        i = pl.program_id(0)
        h0 = jnp.where(i == 0, meta_ref[...], x_ref[...])
        mix_bt = _rms_cols(_heads_to_rows(ot_ref), gb_ref[...]).astype(BF16)
        mixbt_ref[...] = mix_bt
        h1_ref[...] = h0 + _dot(mixa_ref[...], w_ref[:D_CONV, :]) + _dot_tn(mix_bt, w_ref[D_CONV:, :])

    out_shapes = [jax.ShapeDtypeStruct((D_ATTN, n_rows), BF16), jax.ShapeDtypeStruct((n_rows, D_MODEL), F32)]
    return pl.pallas_call(
        body, name="fwd_out", grid=(nt,),
        in_specs=[_real_spec(D_MODEL), _whole_spec(meta_pad.shape), _tile_spec(mix_a.shape), _lane_tile(o_t.shape),
                  _whole_spec(gb_col.shape), _whole_spec(w_out.shape)],
        out_specs=[_lane_tile(out_shapes[0].shape), _tile_spec(out_shapes[1].shape)],
        out_shape=out_shapes,
        compiler_params=_params("parallel"),
    )(x, meta_pad, mix_a, o_t, gb_col, w_out)


PHASES = 8
PHASE_ROWS = ROW_TILE // PHASES
UP_PLANES = -(-UP_SLAB // _LANES)
UP_PAD = UP_PLANES * _LANES
CONV_PLANES = D_CONV // _LANES


def _phase(start):
    return pl.ds(start, PHASE_ROWS, stride=PHASES)


def _to_planes(ref, lead, rows, value):
    width = value.shape[-1]
    for c in range(-(-width // _LANES)):
        part = value[:, c * _LANES:min((c + 1) * _LANES, width)]
        if part.shape[-1] < _LANES:
            part = jnp.concatenate([part, jnp.zeros((part.shape[0], _LANES - part.shape[-1]), part.dtype)], axis=-1)
        ref[(*lead, c, rows, slice(None))] = part


def _from_planes(ref, lead, width):
    planes = [ref[(*lead, c)] for c in range(-(-width // _LANES))]
    last = width - (len(planes) - 1) * _LANES
    return jnp.concatenate(planes[:-1] + [planes[-1][:, :last]], axis=-1)


def _fwd_ffn(h1, target, g2, w_up, fw, fb, w_down, gf, n_rows):
    nt = n_rows // ROW_TILE

    def body(h1_ref, t_ref, g2_ref, wup_ref, fw_ref, fb_ref, wdn_ref, gf_ref,
             n2_ref, up0_ref, act_ref, da_ref, db_ref, dh2_ref, loss_ref, dgf_ref, ext_ref):
        i = pl.program_id(0)

        @pl.when(i == 0)
        def _():
            ext_ref[:, 0:FFN_HALO, :] = jnp.zeros((N_DEV, FFN_HALO, UP_SLAB), F32)

        h1_t = h1_ref[...]
        live = _row_ids(i, ROW_TILE) >= DEAD
        n2 = jnp.where(live, _rms_fwd(h1_t, g2_ref[...]), 0.0).astype(BF16)
        n2_ref[...] = n2
        for s in range(N_DEV):
            up0 = _dot_nt(n2, wup_ref[s])
            up0_ref[s] = up0.astype(BF16)
            ext_ref[s, FFN_HALO:, :] = up0
        first = FFN_HALO - (FFN_CONV_WIDTH - 1)

        def conv(s):
            block = ext_ref[s]
            acc = fb_ref[s, :, :UP_SLAB] + fw_ref[s, FFN_CONV_WIDTH - 1:FFN_CONV_WIDTH, :UP_SLAB] * block[FFN_HALO:]
            for back in range(1, FFN_CONV_WIDTH):
                k = FFN_CONV_WIDTH - 1 - back
                acc = acc + fw_ref[s, k:k + 1, :UP_SLAB] * pltpu.roll(block, back, 0)[FFN_HALO:]
            return acc

        h2 = h1_t
        for s in range(N_ACT_SLAB):
            gate = conv(s)
            val = conv(s + N_ACT_SLAB)
            sg = _sigmoid(gate)
            silu = gate * sg
            act = (silu * val).astype(BF16)
            act_ref[s] = act
            da_ref[s] = (val * sg * (1.0 + gate * (1.0 - sg))).astype(BF16)
            db_ref[s] = silu.astype(BF16)
            h2 = h2 + _dot(act, wdn_ref[s])
        ext_ref[:, 0:FFN_HALO, :] = ext_ref[:, ROW_TILE:ROW_TILE + FFN_HALO, :]

        gf_t = gf_ref[...]
        y = _rms_fwd(h2, gf_t)
        diff = jnp.where(i >= 1, y - t_ref[...], 0.0)
        tile_loss = 0.5 * jnp.sum(jnp.sum(diff * diff, axis=-1, keepdims=True), axis=0, keepdims=True) / D_MODEL
        dh2, dgf = _rms_bwd(diff / D_MODEL, h2, gf_t)
        dh2_ref[...] = dh2
        _accumulate(loss_ref, i == 0, jnp.broadcast_to(tile_loss, loss_ref.shape))
        _accumulate(dgf_ref, i == 0, dgf)

    act_like = jax.ShapeDtypeStruct((N_ACT_SLAB, n_rows, UP_SLAB), BF16)
    out_shapes = [
        jax.ShapeDtypeStruct((n_rows, D_MODEL), BF16),
        jax.ShapeDtypeStruct((N_DEV, n_rows, UP_SLAB), BF16),
        act_like, act_like, act_like,
        jax.ShapeDtypeStruct((n_rows, D_MODEL), F32),
        jax.ShapeDtypeStruct((8, 128), F32),
        jax.ShapeDtypeStruct((1, D_MODEL), F32),
    ]
    whole = [g2, w_up, fw, fb, w_down, gf]
    return pl.pallas_call(
        body, name="fwd_ffn", grid=(nt,),
        in_specs=[_tile_spec(h1.shape), _real_spec(D_MODEL)] + [_whole_spec(a.shape) for a in whole],
        out_specs=[_tile_spec(s.shape) for s in out_shapes[:6]] + [_acc_spec(s.shape) for s in out_shapes[6:]],
        out_shape=out_shapes,
        scratch_shapes=[pltpu.VMEM((N_DEV, ROW_TILE + FFN_HALO, UP_SLAB), F32)],
        compiler_params=_params("arbitrary"),
    )(h1, target, *whole)


def _rope_tables(n_rows):
    pos = jnp.maximum(jnp.arange(n_rows, dtype=jnp.int32) - DEAD, 0)
    inv_freq = 1.0 / (ROPE_THETA ** (jnp.arange(0, QK_ROPE, 2, dtype=F32) / QK_ROPE))
    ang_t = inv_freq[:, None] * pos.astype(F32)[None, :]
    return jnp.cos(ang_t), jnp.sin(ang_t)


def _halo_after(shape, halo, n_rows):
    last = n_rows // halo - 1
    step = ROW_TILE // halo
    if len(shape) == 2:
        return pl.BlockSpec((halo, shape[1]), lambda i: (jnp.minimum((i + 1) * step, last), 0))
    return pl.BlockSpec((shape[0], halo, shape[2]), lambda i: (0, jnp.minimum((i + 1) * step, last), 0))


def _halo_before(shape, halo):
    step = ROW_TILE // halo
    if len(shape) == 2:
        return pl.BlockSpec((halo, shape[1]), lambda i: (jnp.maximum(i * step - 1, 0), 0))
    return pl.BlockSpec((shape[0], halo, shape[2]), lambda i: (0, jnp.maximum(i * step - 1, 0), 0))


def _bwd_ffn_act(dh2, da, db, w_down, n_rows):
    nt = n_rows // ROW_TILE

    def body(dh2_ref, da_ref, db_ref, wdn_ref, dup_ref, dfb_ref):
        i = pl.program_id(0)

        @pl.when(i == 0)
        def _():
            dfb_ref[...] = jnp.zeros_like(dfb_ref)

        dh2_b = dh2_ref[...].astype(BF16)
        for s in range(N_ACT_SLAB):
            d_act = _dot_nt(dh2_b, wdn_ref[s])
            d_gate = d_act * da_ref[s].astype(F32)
            d_val = d_act * db_ref[s].astype(F32)
            dup_ref[s] = d_gate.astype(BF16)
            dup_ref[s + N_ACT_SLAB] = d_val.astype(BF16)
            dfb_ref[s] += jnp.sum(d_gate, axis=0, keepdims=True)
            dfb_ref[s + N_ACT_SLAB] += jnp.sum(d_val, axis=0, keepdims=True)

    out_shapes = [jax.ShapeDtypeStruct((N_DEV, n_rows, UP_SLAB), BF16), jax.ShapeDtypeStruct((N_DEV, 1, UP_SLAB), F32)]
    return pl.pallas_call(
        body, name="bwd_ffn_act", grid=(nt,),
        in_specs=[_tile_spec(dh2.shape), _tile_spec(da.shape), _tile_spec(db.shape), _whole_spec(w_down.shape)],
        out_specs=[_tile_spec(out_shapes[0].shape), _acc_spec(out_shapes[1].shape)],
        out_shape=out_shapes,
        compiler_params=_params("arbitrary"),
    )(dh2, da, db, w_down)


def _bwd_ffn_up(dup, up0, h1, dh2, g2, w_up, fw, n_rows):
    nt = n_rows // ROW_TILE
    last_tap = FFN_CONV_WIDTH - 1
    ext_rows = ROW_TILE + FFN_HALO

    def body(dup_ref, dnext_ref, up0_ref, h1_ref, dh2_ref, g2_ref, wup_ref, fw_ref,
             dup0_ref, dh1_ref, dfw_ref, dg2_ref):
        i = pl.program_id(0)

        @pl.when(i == 0)
        def _():
            dfw_ref[...] = jnp.zeros_like(dfw_ref)

        live = _row_ids(i, ROW_TILE) >= DEAD
        dn2 = jnp.zeros((ROW_TILE, D_MODEL), F32)
        for s in range(N_DEV):
            d = dup_ref[s].astype(F32)
            block = jnp.concatenate([d, jnp.where(i == nt - 1, 0.0, dnext_ref[s].astype(F32))], axis=0)
            u = up0_ref[s].astype(F32)
            dup0 = fw_ref[s, last_tap:last_tap + 1, :UP_SLAB] * d
            dfw_ref[s, last_tap:last_tap + 1, :UP_SLAB] += jnp.sum(d * u, axis=0, keepdims=True)
            for ahead in range(1, FFN_CONV_WIDTH):
                k = last_tap - ahead
                shifted = pltpu.roll(block, ext_rows - ahead, 0)[:ROW_TILE]
                dup0 = dup0 + fw_ref[s, k:k + 1, :UP_SLAB] * shifted
                dfw_ref[s, k:k + 1, :UP_SLAB] += jnp.sum(shifted * u, axis=0, keepdims=True)
            dup0_b = jnp.where(live, dup0, 0.0).astype(BF16)
            dup0_ref[s] = dup0_b
            dn2 = dn2 + _dot(dup0_b, wup_ref[s])
        dx, dg2 = _rms_bwd(dn2, h1_ref[...], g2_ref[...])
        dh1_ref[...] = dh2_ref[...] + dx
        _accumulate(dg2_ref, i == 0, dg2)

    out_shapes = [
        jax.ShapeDtypeStruct((N_DEV, n_rows, UP_SLAB), BF16),
        jax.ShapeDtypeStruct((n_rows, D_MODEL), F32),
        jax.ShapeDtypeStruct((N_DEV, FFN_CONV_WIDTH, UP_PAD), F32),
        jax.ShapeDtypeStruct((1, D_MODEL), F32),
    ]
    return pl.pallas_call(
        body, name="bwd_ffn_up", grid=(nt,),
        in_specs=[_tile_spec(dup.shape), _halo_after(dup.shape, FFN_HALO, n_rows), _tile_spec(up0.shape),
                  _tile_spec(h1.shape), _tile_spec(dh2.shape),
                  _whole_spec(g2.shape), _whole_spec(w_up.shape), _whole_spec(fw.shape)],
        out_specs=[_tile_spec(s.shape) for s in out_shapes[:2]] + [_acc_spec(s.shape) for s in out_shapes[2:]],
        out_shape=out_shapes,
        compiler_params=_params("arbitrary"),
    )(dup, dup, up0, h1, dh2, g2, w_up, fw)


def _bwd_out(dh1, o_t, u1, w_out, gb_col, ln_g, ln_b, ga, n_rows):
    nt = n_rows // ROW_TILE

    def body(dh1_ref, ot_ref, u1_ref, w_ref, gb_ref, lg_ref, lb_ref, ga_ref,
             dot_ref, delta_ref, du1_ref, dgb_ref, dga_ref, dlg_ref, dlb_ref, dcb_ref):
        i = pl.program_id(0)
        dh1_b = dh1_ref[...].astype(BF16)
        o_t = _heads_to_rows(ot_ref)
        gb = gb_ref[...]
        r = lax.rsqrt(jnp.mean(o_t * o_t, axis=0, keepdims=True) + EPS)
        dmix_bt = _dot_nt(w_ref[D_CONV:, :], dh1_b)
        wgt = dmix_bt * gb
        do_t = r * wgt - o_t * (r * r * r) * jnp.mean(wgt * o_t, axis=0, keepdims=True)
        dgb = jnp.sum(dmix_bt * o_t * r, axis=1, keepdims=True)
        for h in range(N_HEADS):
            do_h = do_t[h * V_HEAD:(h + 1) * V_HEAD]
            dot_ref[h] = do_h.astype(BF16)
            delta_ref[h] = jnp.sum(do_h * ot_ref[h], axis=0, keepdims=True)
        lg = lg_ref[...]
        xh, u2, u3, rstd = _conv_chain(u1_ref[...], lg, lb_ref[...])
        du3, dga = _rms_bwd(_dot_nt(dh1_b, w_ref[:D_CONV, :]), u3, ga_ref[...])
        sg = _sigmoid(u2)
        du2 = du3 * sg * (1.0 + u2 * (1.0 - sg))
        dxh = du2 * lg
        du1 = rstd * (dxh - jnp.mean(dxh, axis=-1, keepdims=True) - xh * jnp.mean(dxh * xh, axis=-1, keepdims=True))
        du1_ref[...] = du1
        first = i == 0
        _accumulate(dgb_ref, first, dgb)
        _accumulate(dga_ref, first, dga)
        _accumulate(dlg_ref, first, jnp.sum(du2 * xh, axis=0, keepdims=True))
        _accumulate(dlb_ref, first, jnp.sum(du2, axis=0, keepdims=True))
        _accumulate(dcb_ref, first, jnp.sum(du1, axis=0, keepdims=True))

    out_shapes = [
        jax.ShapeDtypeStruct((N_HEADS, V_HEAD, n_rows), BF16),
        jax.ShapeDtypeStruct((N_HEADS, 1, n_rows), F32),
        jax.ShapeDtypeStruct((n_rows, D_CONV), F32),
        jax.ShapeDtypeStruct((D_ATTN, 1), F32),
    ] + [jax.ShapeDtypeStruct((1, D_CONV), F32)] * 4
    whole = [w_out, gb_col, ln_g, ln_b, ga]
    return pl.pallas_call(
        body, name="bwd_out", grid=(nt,),
        in_specs=[_tile_spec(dh1.shape), _lane_tile(o_t.shape), _tile_spec(u1.shape)] + [_whole_spec(a.shape) for a in whole],
        out_specs=[_lane_tile(out_shapes[0].shape), _lane_tile(out_shapes[1].shape), _tile_spec(out_shapes[2].shape)]
        + [_acc_spec(s.shape) for s in out_shapes[3:]],
        out_shape=out_shapes,
        compiler_params=_params("arbitrary"),
    )(dh1, o_t, u1, *whole)


ATTN_BWD_HEADS = 8


def _attn_bwd(q_t, k, v, do_t, lse, delta, n_rows):
    nt = n_rows // ROW_TILE
    hp = ATTN_BWD_HEADS

    def body(k_ref, v_ref, qt_ref, dot_ref, lse_ref, delta_ref, dqt_ref, dk_ref, dv_ref):
        j = pl.program_id(1)

        @pl.when(j == 0)
        def _():
            dqt_ref[...] = jnp.zeros_like(dqt_ref)

        k_ts = [k_ref[h] for h in range(hp)]
        v_ts = [v_ref[h] for h in range(hp)]

        def make_step(masked):
            def step(i, carry):
                cols = pl.ds(pl.multiple_of(i * ROW_TILE, ROW_TILE), ROW_TILE)
                q_is = [qt_ref[h, :, cols] for h in range(hp)]
                do_is = [dot_ref[h, :, cols] for h in range(hp)]
                scores = [_dot(k_ts[h], q_is[h]) for h in range(hp)]
                dps = [_dot(v_ts[h], do_is[h]) for h in range(hp)]
                visible = _visible(i, j) if masked else None
                probs, dss = [], []
                for h in range(hp):
                    s = jnp.where(visible, scores[h], NEG) if masked else scores[h]
                    p = jnp.exp2(s - lse_ref[h, :, cols])
                    probs.append(p.astype(BF16))
                    dss.append((p * (dps[h] - delta_ref[h, :, cols])).astype(BF16))
                out = []
                for h in range(hp):
                    dk, dv = carry[h]
                    dv = dv + _dot_nt(probs[h], do_is[h])
                    dk = dk + _dot_nt(dss[h], q_is[h])
                    dqt_ref[h, :, cols] += _dot_tn(k_ts[h], dss[h])
                    out.append((dk, dv))
                return tuple(out)
            return step

        init = tuple((jnp.zeros((ROW_TILE, QK_DIM), F32), jnp.zeros((ROW_TILE, V_HEAD), F32)) for _ in range(hp))
        carry = make_step(True)(j, init)
        carry = lax.fori_loop(jnp.where(j == 0, j + 1, nt), nt, make_step(True), carry)
        carry = lax.fori_loop(jnp.where(j == 0, nt, j + 1), nt, make_step(False), carry)
        for h in range(hp):
            dk_ref[h] = carry[h][0] * _LN2
            dv_ref[h] = carry[h][1]

    key_tile = lambda w: pl.BlockSpec((hp, ROW_TILE, w), lambda g, j: (g, j, 0))
    all_cols = lambda w: pl.BlockSpec((hp, w, n_rows), lambda g, j: (g, 0, 0))
    resident = lambda w: pl.BlockSpec((hp, w, n_rows), lambda g, j: (g, 0, 0), pipeline_mode=pl.Buffered(1))
    out_shapes = [
        jax.ShapeDtypeStruct((N_HEADS, QK_DIM, n_rows), F32),
        jax.ShapeDtypeStruct((N_HEADS, n_rows, QK_DIM), F32),
        jax.ShapeDtypeStruct((N_HEADS, n_rows, V_HEAD), F32),
    ]
    return pl.pallas_call(
        body, name="attn_bwd", grid=(N_HEADS // hp, nt),
        in_specs=[key_tile(QK_DIM), key_tile(V_HEAD), resident(QK_DIM), resident(V_HEAD), resident(1), resident(1)],
        out_specs=[all_cols(QK_DIM), key_tile(QK_DIM), key_tile(V_HEAD)],
        out_shape=out_shapes,
        compiler_params=_params("parallel", "arbitrary"),
    )(k, v, q_t, do_t, lse, delta)


def _bwd_qkv(dq_t, dk, dv, cq, ckv, gq, gkv, wq_t, w_ukv, cos, sin, cos_t, sin_t, n_rows):
    nt = n_rows // ROW_TILE

    def body(dqt_ref, dk_ref, dv_ref, cq_ref, ckv_ref, gq_ref, gkv_ref, wqt_ref, wkv_ref, cos_ref, sin_ref,
             cost_ref, sint_ref, dqraw_ref, dkv_ref, dcq_ref, dckv_ref, dkr_ref, dgq_ref, dgkv_ref):
        i = pl.program_id(0)
        cos_rows, sin_rows = cost_ref[...], sint_ref[...]
        dcqn = jnp.zeros((ROW_TILE, Q_LORA), F32)
        dckvn = jnp.zeros((ROW_TILE, KV_LORA), F32)
        dk_rot = jnp.zeros((ROW_TILE, QK_ROPE), F32)
        for h in range(N_HEADS):
            dq_h, dk_h = dqt_ref[h] * QK_DIM ** -0.5, dk_ref[h]
            dq_raw = jnp.concatenate(
                [dq_h[:QK_NOPE], _rope_rows_t(dq_h[QK_NOPE:], cos_rows, sin_rows)], axis=0).astype(BF16)
            dqraw_ref[h] = dq_raw
            dcqn = dcqn + _dot_tn(dq_raw, wqt_ref[h])
            dkv = jnp.concatenate([dk_h[:, :QK_NOPE], dv_ref[h]], axis=-1).astype(BF16)
            dkv_ref[:, h * KV_HEAD:(h + 1) * KV_HEAD] = dkv
            dckvn = dckvn + _dot_nt(dkv, wkv_ref[h])
            dk_rot = dk_rot + dk_h[:, QK_NOPE:]
        dkr_ref[...] = _rope_t(dk_rot, cos_ref[...], sin_ref[...]).astype(BF16)
        dcq, dgq = _rms_bwd(dcqn, cq_ref[...], gq_ref[...])
        dckv, dgkv = _rms_bwd(dckvn, ckv_ref[...], gkv_ref[...])
        dcq_ref[...] = dcq.astype(BF16)
        dckv_ref[...] = dckv.astype(BF16)
        _accumulate(dgq_ref, i == 0, dgq)
        _accumulate(dgkv_ref, i == 0, dgkv)

    out_shapes = [
        jax.ShapeDtypeStruct((N_HEADS, QK_DIM, n_rows), BF16),
        jax.ShapeDtypeStruct((n_rows, N_HEADS * KV_HEAD), BF16),
        jax.ShapeDtypeStruct((n_rows, Q_LORA), BF16),
        jax.ShapeDtypeStruct((n_rows, KV_LORA), BF16),
        jax.ShapeDtypeStruct((n_rows, QK_ROPE), BF16),
        jax.ShapeDtypeStruct((1, Q_LORA), F32),
        jax.ShapeDtypeStruct((1, KV_LORA), F32),
    ]
    tiles = [dk, dv, cq, ckv]
    whole = [gq, gkv, wq_t, w_ukv]
    return pl.pallas_call(
        body, name="bwd_qkv", grid=(nt,),
        in_specs=[_lane_tile(dq_t.shape)] + [_tile_spec(a.shape) for a in tiles] + [_whole_spec(a.shape) for a in whole]
        + [_tile_spec(cos.shape), _tile_spec(sin.shape), _lane_tile(cos_t.shape), _lane_tile(sin_t.shape)],
        out_specs=[_lane_tile(out_shapes[0].shape)] + [_tile_spec(s.shape) for s in out_shapes[1:5]]
        + [_acc_spec(s.shape) for s in out_shapes[5:]],
        out_shape=out_shapes,
        compiler_params=_params("arbitrary"),
    )(dq_t, *tiles, *whole, cos, sin, cos_t, sin_t)


def _bwd_conv(du1, ag, conv_w, dcq, dckv, dkr, n_rows):
    nt = n_rows // ROW_TILE

    last_tap = CONV_WIDTH - 1

    def body(du1_ref, dnext_ref, ag_ref, w_ref, dcq_ref, dckv_ref, dkr_ref, dz_ref, dw_ref,
             dext_ref, uext_ref, conv_ref, sums_ref):
        i = pl.program_id(0)

        @pl.when(i == 0)
        def _():
            sums_ref[...] = jnp.zeros_like(sums_ref)

        _to_planes(dext_ref, (), slice(0, ROW_TILE), du1_ref[...])
        _to_planes(dext_ref, (), slice(ROW_TILE, None), jnp.where(i == nt - 1, 0.0, dnext_ref[...]))
        ag_t = ag_ref[...]
        live = _row_ids(i, ROW_TILE) >= DEAD
        sg = _sigmoid(ag_t[:, D_CONV:])
        _to_planes(uext_ref, (), slice(None), jnp.where(live, ag_t[:, :D_CONV] * sg, 0.0))
        for c in range(CONV_PLANES):
            taps = w_ref[:, c * _LANES:(c + 1) * _LANES]
            for p in range(PHASES):
                u = uext_ref[c, _phase(p), :]
                acc = jnp.zeros((PHASE_ROWS, _LANES), F32)
                for k in range(CONV_WIDTH):
                    shifted = dext_ref[c, _phase(p + last_tap - k), :]
                    acc = acc + taps[k:k + 1, :] * shifted
                    sums_ref[c, k] += shifted * u
                conv_ref[c, _phase(p), :] = acc
        du0 = jnp.where(live, _from_planes(conv_ref, (), D_CONV), 0.0)
        da = du0 * sg
        dgate = du0 * ag_t[:, :D_CONV] * sg * (1.0 - sg)
        dz_ref[...] = jnp.concatenate(
            [da.astype(BF16), dgate.astype(BF16), dcq_ref[...], dckv_ref[...], dkr_ref[...]], axis=-1)

        @pl.when(i == nt - 1)
        def _():
            for c in range(CONV_PLANES):
                for k in range(CONV_WIDTH):
                    dw_ref[k:k + 1, c * _LANES:(c + 1) * _LANES] = jnp.sum(sums_ref[c, k], axis=0, keepdims=True)

    out_shapes = [jax.ShapeDtypeStruct((n_rows, D_IN), BF16), jax.ShapeDtypeStruct((CONV_WIDTH, D_CONV), F32)]
    return pl.pallas_call(
        body, name="bwd_conv", grid=(nt,),
        in_specs=[_tile_spec(du1.shape), _halo_after(du1.shape, CONV_HALO, n_rows), _tile_spec(ag.shape),
                  _whole_spec(conv_w.shape), _tile_spec(dcq.shape), _tile_spec(dckv.shape), _tile_spec(dkr.shape)],
        out_specs=[_tile_spec(out_shapes[0].shape), _acc_spec(out_shapes[1].shape)],
        out_shape=out_shapes,
        scratch_shapes=[pltpu.VMEM((CONV_PLANES, ROW_TILE + CONV_HALO, _LANES), F32),
                        pltpu.VMEM((CONV_PLANES, ROW_TILE, _LANES), F32), pltpu.VMEM((CONV_PLANES, ROW_TILE, _LANES), F32),
                        pltpu.VMEM((CONV_PLANES, CONV_WIDTH, PHASE_ROWS, _LANES), F32)],
        compiler_params=_params("arbitrary"),
    )(du1, du1, ag, conv_w, dcq, dckv, dkr)


def _bwd_in(dz, x, meta_pad, dh1, g1, w_in, n_rows):
    nt = n_rows // ROW_TILE

    def body(dz_ref, x_ref, meta_ref, dh1_ref, g_ref, w_ref, gx_ref, gmeta_ref, dg1_ref):
        i = pl.program_id(0)
        h0 = jnp.where(i == 0, meta_ref[...], x_ref[...])
        dx, dg1 = _rms_bwd(_dot(dz_ref[...], w_ref[...]), h0, g_ref[...])
        dh0 = dh1_ref[...] + dx
        gx_ref[...] = dh0

        @pl.when(i == 0)
        def _():
            gmeta_ref[...] = dh0

        _accumulate(dg1_ref, i == 0, dg1)

    out_shapes = [
        jax.ShapeDtypeStruct((n_rows - ROW_TILE, D_MODEL), F32),
        jax.ShapeDtypeStruct((ROW_TILE, D_MODEL), F32),
        jax.ShapeDtypeStruct((1, D_MODEL), F32),
    ]
    return pl.pallas_call(
        body, name="bwd_in", grid=(nt,),
        in_specs=[_tile_spec(dz.shape), _real_spec(D_MODEL), _whole_spec(meta_pad.shape), _tile_spec(dh1.shape),
                  _whole_spec(g1.shape), _whole_spec(w_in.shape)],
        out_specs=[_real_spec(D_MODEL), _acc_spec(out_shapes[1].shape), _acc_spec(out_shapes[2].shape)],
        out_shape=out_shapes,
        compiler_params=_params("arbitrary"),
    )(dz, x, meta_pad, dh1, g1, w_in)


def _contraction_tile(n_rows):
    return next(t for t in range(n_rows // 2 // _LANES * _LANES, 0, -_LANES) if n_rows % t == 0)


def _weight_grad(a, b, name, a_transposed=False):
    groups = max(a.shape[0] if a.ndim == 3 else 1, b.shape[0] if b.ndim == 3 else 1)
    n_rows, n = b.shape[-2], b.shape[-1]
    m = a.shape[-2] if a_transposed else a.shape[-1]
    kt = _contraction_tile(n_rows)
    steps = n_rows // kt

    def body(a_ref, b_ref, out_ref, acc_ref):
        i = pl.program_id(1)
        a_t, b_t = a_ref[...].astype(BF16), b_ref[...].astype(BF16)
        part = _dot(a_t, b_t) if a_transposed else _dot_tn(a_t, b_t)
        _accumulate(acc_ref, i == 0, part)

        @pl.when(i == steps - 1)
        def _():
            out_ref[...] = acc_ref[...].astype(out_ref.dtype)

    def spec(arr, rows_last):
        block = (arr.shape[-2], kt) if rows_last else (kt, arr.shape[-1])
        at = (lambda i: (0, i)) if rows_last else (lambda i: (i, 0))
        if arr.ndim == 3:
            return pl.BlockSpec((None,) + block, lambda g, i: (g,) + at(i))
        return pl.BlockSpec(block, lambda g, i: at(i))

    return pl.pallas_call(
        body, name=name, grid=(groups, steps),
        in_specs=[spec(a, a_transposed), spec(b, False)],
        out_specs=pl.BlockSpec((None, m, n), lambda g, i: (g, 0, 0)),
        out_shape=jax.ShapeDtypeStruct((groups, m, n), BF16),
        scratch_shapes=[pltpu.VMEM((m, n), F32)],
        compiler_params=_params("parallel", "arbitrary"),
    )(a, b)


def _my_index():
    return 4 * lax.axis_index("x") + 2 * lax.axis_index("y") + lax.axis_index("c")


def _peer(k):
    flip = lambda v, bit: 1 - v if bit else v
    px = flip(lax.axis_index("x"), k & 4)
    py = flip(lax.axis_index("y"), k & 2)
    pc = flip(lax.axis_index("c"), k & 1)
    return (px, py, pc), 4 * px + 2 * py + pc


def _all_gather(shards, dtypes):
    n = len(shards)
    sibling, chips = 1, (2, 4, 6)

    def body(*refs):
        ins, outs, stages = refs[:n], refs[n:2 * n], refs[2 * n:3 * n]
        send_sems, recv_sems, local_sems = refs[3 * n:]
        me = _my_index()
        for a in range(n):
            stages[a][...] = ins[a][...].astype(stages[a].dtype)
        local = [pltpu.make_async_copy(stages[a], outs[a].at[me], local_sems.at[a]) for a in range(n)]
        for cp in local:
            cp.start()

        def copy(a, k, src, slot, to):
            return pltpu.make_async_remote_copy(
                src_ref=src, dst_ref=outs[a].at[slot], send_sem=send_sems.at[a, k - 1],
                recv_sem=recv_sems.at[a, k - 1], device_id=_peer(to)[0], device_id_type=MESH)

        def own(a, k):
            return copy(a, k, stages[a], me, k)

        def passed(a, k):
            slot = _peer(k)[1]
            return copy(a, k ^ sibling, outs[a].at[slot], slot, sibling)

        def arrival(a, k):
            return copy(a, k, stages[a], _peer(k)[1], k)

        for k in (sibling,) + chips:
            for a in range(n):
                own(a, k).start()
        for k in chips:
            for a in range(n):
                arrival(a, k).wait_recv()
                passed(a, k).start()
        for a in range(n):
            arrival(a, sibling).wait_recv()
            for k in chips:
                arrival(a, k ^ sibling).wait_recv()
        for a in range(n):
            for k in (sibling,) + chips:
                own(a, k).wait_send()
            for k in chips:
                passed(a, k).wait_send()
        for cp in local:
            cp.wait()

    return pl.pallas_call(
        body, name="gather_weights",
        in_specs=[pl.BlockSpec(memory_space=pltpu.VMEM)] * n,
        out_specs=[pl.BlockSpec(memory_space=pl.ANY)] * n,
        out_shape=[jax.ShapeDtypeStruct((N_DEV,) + s.shape, dt) for s, dt in zip(shards, dtypes)],
        scratch_shapes=[pltpu.VMEM(s.shape, dt) for s, dt in zip(shards, dtypes)]
        + [pltpu.SemaphoreType.DMA((n, N_DEV - 1)), pltpu.SemaphoreType.DMA((n, N_DEV - 1)), pltpu.SemaphoreType.DMA((n,))],
        compiler_params=pltpu.CompilerParams(vmem_limit_bytes=VMEM_LIMIT),
    )(*shards)


def _exchange(parts, whole):
    n = len(parts)

    def body(*refs):
        ins, outs = refs[:n], refs[n:2 * n]
        send_sems, recv_sems, local_sems = refs[2 * n:]
        me = _my_index()

        def src(a, slab):
            return ins[a] if whole[a] else ins[a].at[slab]

        local = [pltpu.make_async_copy(src(a, me), outs[a].at[me], local_sems.at[a]) for a in range(n)]
        for cp in local:
            cp.start()

        def copy(a, k, slab, slot):
            peer, _ = _peer(k)
            return pltpu.make_async_remote_copy(
                src_ref=src(a, slab), dst_ref=outs[a].at[slot], send_sem=send_sems.at[a, k - 1],
                recv_sem=recv_sems.at[a, k - 1], device_id=peer, device_id_type=MESH)

        for k in range(1, N_DEV):
            for a in range(n):
                copy(a, k, _peer(k)[1], me).start()
        for k in range(1, N_DEV):
            for a in range(n):
                copy(a, k, _peer(k)[1], _peer(k)[1]).wait()
        for cp in local:
            cp.wait()

    return pl.pallas_call(
        body, name="exchange_grads",
        in_specs=[pl.BlockSpec(memory_space=pl.ANY)] * n,
        out_specs=[pl.BlockSpec(memory_space=pl.ANY)] * n,
        out_shape=[jax.ShapeDtypeStruct(((N_DEV,) + p.shape) if w else p.shape, p.dtype) for p, w in zip(parts, whole)],
        scratch_shapes=[pltpu.SemaphoreType.DMA((n, N_DEV - 1)), pltpu.SemaphoreType.DMA((n, N_DEV - 1)),
                        pltpu.SemaphoreType.DMA((n,))],
    )(*parts)


def _sequencer_exchange(parts, whole, name, collective_id):
    n = len(parts)
    srcs = [jax.new_ref(p, memory_space=pltpu.MemorySpace.HBM) for p in parts]
    lands = [jax.empty_ref(jax.ShapeDtypeStruct(((N_DEV,) + p.shape) if w else p.shape, p.dtype),
                           memory_space=pltpu.MemorySpace.HBM) for p, w in zip(parts, whole)]

    @pl.kernel(mesh=plsc.ScalarSubcoreMesh(axis_name="sequencer", num_cores=1), name=name,
               scratch_types=(pltpu.SemaphoreType.DMA((n, N_DEV - 1)), pltpu.SemaphoreType.DMA((n, N_DEV - 1)),
                              pltpu.SemaphoreType.DMA((n,))),
               compiler_params=pltpu.CompilerParams(collective_id=collective_id))
    def launch(send_sems, recv_sems, local_sems):
        barrier = pltpu.get_barrier_semaphore()
        for k in range(1, N_DEV):
            pl.semaphore_signal(barrier, inc=1, device_id=_peer(k)[0], device_id_type=MESH)
        pl.semaphore_wait(barrier, N_DEV - 1)
        me = _my_index()

        def src(a, slab):
            return srcs[a] if whole[a] else srcs[a].at[slab]

        local = [pltpu.make_async_copy(src(a, me), lands[a].at[me], local_sems.at[a]) for a in range(n)]
        for cp in local:
            cp.start()

        def copy(a, k, slab, slot):
            return pltpu.make_async_remote_copy(
                src_ref=src(a, slab), dst_ref=lands[a].at[slot], send_sem=send_sems.at[a, k - 1],
                recv_sem=recv_sems.at[a, k - 1], device_id=_peer(k)[0], device_id_type=MESH)

        for k in range(1, N_DEV):
            for a in range(n):
                copy(a, k, _peer(k)[1], me).start()
        for k in range(1, N_DEV):
            for a in range(n):
                copy(a, k, _peer(k)[1], _peer(k)[1]).wait()
        for cp in local:
            cp.wait()

    launch()
    return [land[...] for land in lands]


def _row_block(rows):
    if rows <= ROW_TILE:
        return rows
    return next(rb for rb in range(ROW_TILE, 0, -16) if rows % rb == 0)


def _adamw(landing, w, m, v, name):
    rows, cols = w.shape
    rb = _row_block(rows)

    def body(l_ref, w_ref, m_ref, v_ref, g_ref, d_ref, m2_ref, v2_ref):
        g = l_ref[0].astype(F32)
        for p in range(1, N_DEV):
            g = g + l_ref[p].astype(F32)
        g_ref[...] = g
        d_ref[...], m2_ref[...], v2_ref[...] = _adamw_step(g, w_ref[...], m_ref[...], v_ref[...])

    flat = pl.BlockSpec((rb, cols), lambda i: (i, 0))
    return pl.pallas_call(
        body, name=name, grid=(rows // rb,),
        in_specs=[pl.BlockSpec((N_DEV, rb, cols), lambda i: (0, i, 0)), flat, flat, flat],
        out_specs=[flat] * 4,
        out_shape=[jax.ShapeDtypeStruct((rows, cols), F32)] * 4,
        compiler_params=_params("parallel"),
    )(landing, w, m, v)


def _adamw_step(g, w, m, v):
    m2 = ADAM_B1 * m + (1.0 - ADAM_B1) * g
    v2 = ADAM_B2 * v + (1.0 - ADAM_B2) * (g * g)
    m_hat = m2 / (1.0 - ADAM_B1 ** ADAM_STEP)
    v_hat = v2 / (1.0 - ADAM_B2 ** ADAM_STEP)
    return -ADAM_LR * (m_hat / (jnp.sqrt(v_hat) + ADAM_EPS) + ADAM_WD * w), m2, v2


_REPLICATED = (
    ("mix_norm_g", D_MODEL), ("q_norm_g", Q_LORA), ("kv_norm_g", KV_LORA), ("conv_b", D_CONV), ("conv_ln_g", D_CONV),
    ("conv_ln_b", D_CONV), ("conv_out_g", D_CONV), ("attn_out_g", D_CONV), ("ffn_norm_g", D_MODEL),
    ("ffn_conv_b", D_UP), ("final_norm_g", D_MODEL),
)
_REPLICATED_WIDTH = sum(size for _, size in _REPLICATED) + _LANES

_WEIGHT_ORDER = (
    "meta_tokens", "mix_norm_g", "w_in", "q_norm_g", "w_uq", "kv_norm_g", "w_ukv", "conv_w", "conv_b", "conv_ln_g",
    "conv_ln_b", "conv_out_g", "attn_out_g", "w_out", "ffn_norm_g", "w_ffn_up", "ffn_conv_w", "ffn_conv_b",
    "w_ffn_down", "final_norm_g",
)


def _pack_replicated(grads, loss):
    rows = [grads[name].reshape(1, size) for name, size in _REPLICATED]
    return jnp.concatenate(rows + [jnp.broadcast_to(loss.reshape(1, 1), (1, _LANES))], axis=-1)


def _adamw_replicated(landing, weights, moments_m, moments_v):
    n = len(_REPLICATED)

    def body(*refs):
        l_ref, ins, outs = refs[0], refs[1:1 + 3 * n], refs[1 + 3 * n:]
        total = l_ref[0]
        for p in range(1, N_DEV):
            total = total + l_ref[p]
        at = 0
        for a, (_, size) in enumerate(_REPLICATED):
            g = total[:, at:at + size]
            w_ref, m_ref, v_ref = ins[3 * a:3 * a + 3]
            g_ref, d_ref, m2_ref, v2_ref = outs[4 * a:4 * a + 4]
            g_ref[...] = g
            d_ref[...], m2_ref[...], v2_ref[...] = _adamw_step(g, w_ref[...], m_ref[...], v_ref[...])
            at += size
        outs[-1][...] = total[:, at:at + _LANES]

    operands, out_shapes = [], []
    for name, size in _REPLICATED:
        operands += [weights[name].reshape(1, size), moments_m[name].reshape(1, size), moments_v[name].reshape(1, size)]
        out_shapes += [jax.ShapeDtypeStruct((1, size), F32)] * 4
    out_shapes.append(jax.ShapeDtypeStruct((1, _LANES), F32))
    outs = pl.pallas_call(body, name="adamw_replicated", out_shape=out_shapes)(landing, *operands)
    return outs[-1][0, 0], {name: outs[4 * a:4 * a + 4] for a, (name, _) in enumerate(_REPLICATED)}


def _pad_rows(a, rows):
    return jnp.pad(a, ((0, rows - a.shape[0]), (0, 0)))


def _slabs(a):
    r, c = a.shape
    return a.reshape(r, N_DEV, c // N_DEV).transpose(1, 0, 2)


def _unslab(a):
    g, r, c = a.shape
    return a.transpose(1, 0, 2).reshape(r, g * c)


def _local_step(x, target, w, n_rows, ffn_weights, send_grads):
    cos_t, sin_t = _rope_tables(n_rows)
    cos, sin = cos_t.T, sin_t.T
    meta_pad, g1, gf = w["meta_pad"], w["mix_norm_g"], w["final_norm_g"]
    gq, gkv, gb_col = w["q_norm_g"], w["kv_norm_g"], w["attn_out_g"].reshape(D_ATTN, 1)
    nb, ag, cq, ckv, kr = _fwd_in(x, meta_pad, g1, w["w_in"], n_rows)
    mix_a, u1 = _fwd_conv(ag, w["conv_w"], w["conv_b"], w["conv_ln_g"], w["conv_ln_b"], w["conv_out_g"], n_rows)
    q_t, k, v, v_t, cqn, ckvn = _fwd_qkv(cq, ckv, kr, gq, gkv, w["wq_t"], w["w_ukv"], w["wv_t"], cos, sin, cos_t, sin_t, n_rows)
    o_t, lse = _attn_fwd(q_t, k, v_t, n_rows)
    w_out, w_up, w_down = ffn_weights()
    mix_bt, h1 = _fwd_out(x, meta_pad, mix_a, o_t, gb_col, w_out, n_rows)
    n2, up0, act, da, db, dh2, loss, dgf = _fwd_ffn(
        h1, target, w["ffn_norm_g"], w_up, w["fw"], w["fb"], w_down, gf, n_rows)

    dup, dfb = _bwd_ffn_act(dh2, da, db, w_down, n_rows)
    dup0, dh1, dfw, dg2 = _bwd_ffn_up(dup, up0, h1, dh2, w["ffn_norm_g"], w_up, w["fw"], n_rows)
    grad_w_out = jnp.concatenate([_weight_grad(mix_a, dh1, "grad_w_out_conv")[0],
                                  _weight_grad(mix_bt, dh1, "grad_w_out_attn", a_transposed=True)[0]], axis=0)
    stage0 = {
        "w_ffn_up": _weight_grad(dup0, n2, "grad_w_ffn_up"),
        "w_ffn_down": _weight_grad(act, dh2, "grad_w_ffn_down").reshape(N_DEV, D_FF // N_DEV, D_MODEL),
        "w_out": grad_w_out.reshape(N_DEV, D_MODEL // N_DEV, D_MODEL),
    }
    stage0, dh1 = lax.optimization_barrier((stage0, dh1))
    send_grads(0, stage0)
    do_t, delta, du1, dgb, dga, dlg, dlb, dcb = _bwd_out(
        dh1, o_t, u1, w_out, gb_col, w["conv_ln_g"], w["conv_ln_b"], w["conv_out_g"], n_rows)
    dq_t, dk, dv = _attn_bwd(q_t, k, v, do_t, lse, delta, n_rows)
    dqraw_t, dkv, dcq, dckv, dkr, dgq, dgkv = _bwd_qkv(
        dq_t, dk, dv, cq, ckv, gq, gkv, w["wq_t"], w["w_ukv"], cos, sin, cos_t, sin_t, n_rows)
    dz, dcw = _bwd_conv(du1, ag, w["conv_w"], dcq, dckv, dkr, n_rows)
    stage1 = {
        "w_in": _weight_grad(dz, nb, "grad_w_in")[0].reshape(N_DEV, D_IN // N_DEV, D_MODEL),
        "w_uq": _weight_grad(dqraw_t.reshape(N_HEADS * QK_DIM, n_rows), cqn, "grad_w_uq", a_transposed=True)[0].reshape(
            N_HEADS, QK_DIM, Q_LORA),
        "w_ukv": _slabs(_weight_grad(ckvn, dkv, "grad_w_ukv")[0]),
        "conv_w": _slabs(dcw),
        "ffn_conv_w": dfw[:, :, :UP_SLAB],
    }
    stage1, dz = lax.optimization_barrier((stage1, dz))
    send_grads(1, stage1)
    gx, gmeta, dg1 = _bwd_in(dz, x, meta_pad, dh1, g1, w["w_in"], n_rows)

    sharded = {"meta_tokens": _slabs(gmeta[DEAD:])}
    replicated = {
        "mix_norm_g": dg1, "q_norm_g": dgq, "kv_norm_g": dgkv, "conv_b": dcb, "conv_ln_g": dlg, "conv_ln_b": dlb,
        "conv_out_g": dga, "attn_out_g": dgb, "ffn_norm_g": dg2, "ffn_conv_b": dfb, "final_norm_g": dgf,
    }
    return loss[0, 0], gx, sharded, replicated


_SHARDED = (
    ("w_in", None, BF16), ("w_uq", None, BF16), ("w_ukv", None, BF16), ("w_out", None, BF16), ("w_ffn_up", None, BF16),
    ("w_ffn_down", None, BF16), ("conv_w", 32, F32), ("ffn_conv_w", 8, F32), ("meta_tokens", None, F32),
)
GATHER_LATE_ID = 3
EXCHANGE_STAGE_IDS = (4, 5)
_LATE_WEIGHTS = ("w_out", "w_ffn_up", "w_ffn_down")
_COLUMN_SHARDS = ("w_in", "w_uq", "w_ffn_up")


def kernel(x, meta_tokens, mix_norm_g, w_in, q_norm_g, w_uq, kv_norm_g, w_ukv, conv_w, conv_b, conv_ln_g, conv_ln_b, conv_out_g, attn_out_g, w_out, ffn_norm_g, w_ffn_up, ffn_conv_w, ffn_conv_b, w_ffn_down, final_norm_g, loss_target, m_meta_tokens, m_mix_norm_g, m_w_in, m_q_norm_g, m_w_uq, m_kv_norm_g, m_w_ukv, m_conv_w, m_conv_b, m_conv_ln_g, m_conv_ln_b, m_conv_out_g, m_attn_out_g, m_w_out, m_ffn_norm_g, m_w_ffn_up, m_ffn_conv_w, m_ffn_conv_b, m_w_ffn_down, m_final_norm_g, v_meta_tokens, v_mix_norm_g, v_w_in, v_q_norm_g, v_w_uq, v_kv_norm_g, v_w_ukv, v_conv_w, v_conv_b, v_conv_ln_g, v_conv_ln_b, v_conv_out_g, v_attn_out_g, v_w_out, v_ffn_norm_g, v_w_ffn_up, v_ffn_conv_w, v_ffn_conv_b, v_w_ffn_down, v_final_norm_g):
    given = dict(locals())
    weights = {name: given[name] for name in _WEIGHT_ORDER}
    moments_m = {name: given["m_" + name] for name in _WEIGHT_ORDER}
    moments_v = {name: given["v_" + name] for name in _WEIGHT_ORDER}
    seq = x.shape[1]
    n_rows = ROW_TILE + seq

    def shard2d(name, a):
        a = a.reshape(a.shape[-2], a.shape[-1])
        return a.T if name in _COLUMN_SHARDS else a

    early = [entry for entry in _SHARDED if entry[0] not in _LATE_WEIGHTS]
    shards = []
    for name, pad_to, _ in early:
        s = shard2d(name, weights[name])
        shards.append(s if pad_to is None else _pad_rows(s, pad_to))
    gathered = dict(zip([name for name, _, _ in early], _all_gather(shards, [dt for _, _, dt in early])))
    behind = gathered["meta_tokens"][0, 0, 0] * 0.0
    late_parts = [(shard2d(name, weights[name]) + behind).astype(BF16) for name in _LATE_WEIGHTS]
    late = _sequencer_exchange(late_parts, [True] * len(late_parts), "gather_late", GATHER_LATE_ID)
    meta_full = _unslab(gathered["meta_tokens"])
    full = {
        "meta_pad": jnp.concatenate([jnp.zeros((DEAD, D_MODEL), F32), meta_full], axis=0),
        "w_in": gathered["w_in"].reshape(D_IN, D_MODEL),
        "wq_t": gathered["w_uq"],
        "w_ukv": gathered["w_ukv"],
        "wv_t": gathered["w_ukv"][:, :, QK_NOPE:].transpose(0, 2, 1),
        "conv_w": _unslab(gathered["conv_w"][:, :CONV_WIDTH]),
        "fw": jnp.pad(gathered["ffn_conv_w"][:, :FFN_CONV_WIDTH], ((0, 0), (0, 0), (0, UP_PAD - UP_SLAB))),
        "fb": jnp.pad(ffn_conv_b.reshape(N_DEV, 1, UP_SLAB), ((0, 0), (0, 0), (0, UP_PAD - UP_SLAB))),
        "final_norm_g": final_norm_g.reshape(1, D_MODEL),
    }
    for name in ("mix_norm_g", "q_norm_g", "kv_norm_g", "conv_b", "conv_ln_g", "conv_ln_b", "conv_out_g", "attn_out_g",
                 "ffn_norm_g"):
        full[name] = weights[name]

    def ffn_weights():
        w_out_all, w_up_all, w_down_all = late
        return (w_out_all.reshape(D_MODEL, D_MODEL), w_up_all, w_down_all.reshape(N_ACT_SLAB, UP_SLAB, D_MODEL))

    wire = {name: (pad_to, dt) for name, pad_to, dt in _SHARDED}
    landing = {}

    def on_the_wire(name, slabs):
        pad_to, dt = wire[name]
        slabs = slabs.astype(dt)
        return slabs if pad_to is None else jnp.pad(slabs, ((0, 0), (0, pad_to - slabs.shape[1]), (0, 0)))

    def send_grads(stage, grads):
        parts = [on_the_wire(name, slabs) for name, slabs in grads.items()]
        if landing:
            arrived = list(landing)
            parts, held = lax.optimization_barrier((parts, [landing[name] for name in arrived]))
            landing.update(zip(arrived, held))
        landed = _sequencer_exchange(parts, [False] * len(parts), f"exchange_stage{stage}", EXCHANGE_STAGE_IDS[stage])
        landing.update(zip(grads, landed))

    loss, gx, sharded, replicated = _local_step(x[0], loss_target[0], full, n_rows, ffn_weights, send_grads)

    parts = [on_the_wire(name, slabs) for name, slabs in sharded.items()] + [_pack_replicated(replicated, loss)]
    landed = _exchange(parts, [False] * len(sharded) + [True])
    landing.update(zip(sharded, landed[:-1]))

    grad, delta, new_m, new_v = {}, {}, {}, {}
    for name, pad_to, _ in _SHARDED:
        land = landing[name]
        ws, ms, vs = (shard2d(name, a[name]) for a in (weights, moments_m, moments_v))
        rows = ws.shape[0]
        if pad_to is not None:
            ws, ms, vs = _pad_rows(ws, pad_to), _pad_rows(ms, pad_to), _pad_rows(vs, pad_to)
        outs = _adamw(land, ws, ms, vs, "adamw_" + name)
        shape = weights[name].shape
        grad[name], delta[name], new_m[name], new_v[name] = (
            (o.T if name in _COLUMN_SHARDS else o[:rows]).reshape(shape) for o in outs)
    loss, updates = _adamw_replicated(landed[-1], weights, moments_m, moments_v)
    for name, outs in updates.items():
        grad[name], delta[name], new_m[name], new_v[name] = (o.reshape(weights[name].shape) for o in outs)

    return (loss, gx[None], *[grad[n] for n in _WEIGHT_ORDER], *[delta[n] for n in _WEIGHT_ORDER],
            *[new_m[n] for n in _WEIGHT_ORDER], *[new_v[n] for n in _WEIGHT_ORDER])
```

```python
import functools

import jax
import jax.numpy as jnp
from jax import lax
from jax.experimental import pallas as pl
from jax.experimental.pallas import tpu as pltpu
from jax.experimental.pallas import tpu_sc as plsc

F32 = jnp.float32
BF16 = jnp.bfloat16

N_DEV = 8
D_MODEL = 1024
CHUNK = 64
CHUNK_SHIFT = 6
N_META = 16
D_CONV = 512
CONV_WIDTH = 31
N_HEADS = 8
QK_NOPE = 64
QK_ROPE = 32
QK_DIM = QK_NOPE + QK_ROPE
V_HEAD = 64
KV_HEAD = QK_NOPE + V_HEAD
D_ATTN = N_HEADS * V_HEAD
Q_LORA = 384
KV_LORA = 256
ROPE_THETA = 10000.0
D_IN = 2 * D_CONV + Q_LORA + KV_LORA + QK_ROPE
D_FF = 2816
D_UP = 2 * D_FF
FFN_CONV_WIDTH = 3
UP_SLAB = D_UP // N_DEV
N_ACT_SLAB = D_FF // UP_SLAB
EPS = 1e-6
NEG = -1e30
_LN2 = 0.6931471805599453
QK_LOGIT_SCALE = QK_DIM ** -0.5 / _LN2
ADAM_LR = 0.001
ADAM_B1 = 0.9
ADAM_B2 = 0.999
ADAM_EPS = 1e-08
ADAM_WD = 0.01
ADAM_STEP = 10

ROW_TILE = 256
DEAD = ROW_TILE - N_META
CONV_HALO = 32
FFN_HALO = 16
VMEM_LIMIT = 56 * 1024 * 1024
_LANES = 128

MESH = pl.DeviceIdType.MESH


def _dot(a, b):
    return jnp.dot(a, b, preferred_element_type=F32)


def _dot_nt(a, b):
    return lax.dot_general(a, b, (((1,), (1,)), ((), ())), preferred_element_type=F32)


def _dot_tn(a, b):
    return lax.dot_general(a, b, (((0,), (0,)), ((), ())), preferred_element_type=F32)


def _sigmoid(x):
    return 1.0 / (1.0 + jnp.exp2(x * (-1.0 / _LN2)))


def _rms_fwd(x, g):
    r = lax.rsqrt(jnp.mean(x * x, axis=-1, keepdims=True) + EPS)
    return x * r * g


def _rms_bwd(dy, x, g):
    r = lax.rsqrt(jnp.mean(x * x, axis=-1, keepdims=True) + EPS)
    w = dy * g
    dx = r * w - x * (r * r * r) * jnp.mean(w * x, axis=-1, keepdims=True)
    return dx, jnp.sum(dy * x * r, axis=0, keepdims=True)


def _rope(x, cos, sin):
    half = QK_ROPE // 2
    x1, x2 = x[:, :half], x[:, half:]
    return jnp.concatenate([x1 * cos - x2 * sin, x2 * cos + x1 * sin], axis=-1)


def _rope_t(dy, cos, sin):
    half = QK_ROPE // 2
    d1, d2 = dy[:, :half], dy[:, half:]
    return jnp.concatenate([d1 * cos + d2 * sin, d2 * cos - d1 * sin], axis=-1)


def _row_ids(i, rows):
    return i * rows + lax.broadcasted_iota(jnp.int32, (rows, 1), 0)


def _accumulate(ref, first, value):
    @pl.when(first)
    def _():
        ref[...] = value

    @pl.when(jnp.logical_not(first))
    def _():
        ref[...] += value


def _tile_spec(shape):
    nd = len(shape)
    if nd == 2:
        return pl.BlockSpec((ROW_TILE, shape[1]), lambda i: (i, 0))
    return pl.BlockSpec((shape[0], ROW_TILE, shape[2]), lambda i: (0, i, 0))


def _whole_spec(shape):
    nd = len(shape)
    return pl.BlockSpec(tuple(shape), lambda i: (0,) * nd, pipeline_mode=pl.Buffered(1))


def _acc_spec(shape):
    nd = len(shape)
    return pl.BlockSpec(tuple(shape), lambda i: (0,) * nd)


def _real_spec(width):
    return pl.BlockSpec((ROW_TILE, width), lambda i: (jnp.maximum(i - 1, 0), 0))


def _params(*semantics):
    return pltpu.CompilerParams(dimension_semantics=semantics, vmem_limit_bytes=VMEM_LIMIT)


def _fwd_in(x, meta_pad, g1, w_in, n_rows):
    nt = n_rows // ROW_TILE

    def body(x_ref, meta_ref, g_ref, w_ref, nb_ref, ag_ref, cq_ref, ckv_ref, kr_ref):
        i = pl.program_id(0)
        h0 = jnp.where(i == 0, meta_ref[...], x_ref[...])
        nb = _rms_fwd(h0, g_ref[...]).astype(BF16)
        nb_ref[...] = nb
        z = _dot_nt(nb, w_ref[...])
        ag_ref[...] = z[:, :2 * D_CONV]
        cq_ref[...] = z[:, 2 * D_CONV:2 * D_CONV + Q_LORA]
        ckv_ref[...] = z[:, 2 * D_CONV + Q_LORA:2 * D_CONV + Q_LORA + KV_LORA]
        kr_ref[...] = z[:, 2 * D_CONV + Q_LORA + KV_LORA:]

    out_shapes = [
        jax.ShapeDtypeStruct((n_rows, D_MODEL), BF16),
        jax.ShapeDtypeStruct((n_rows, 2 * D_CONV), F32),
        jax.ShapeDtypeStruct((n_rows, Q_LORA), F32),
        jax.ShapeDtypeStruct((n_rows, KV_LORA), F32),
        jax.ShapeDtypeStruct((n_rows, QK_ROPE), F32),
    ]
    return pl.pallas_call(
        body, name="fwd_in", grid=(nt,),
        in_specs=[_real_spec(D_MODEL), _whole_spec(meta_pad.shape), _whole_spec(g1.shape), _whole_spec(w_in.shape)],
        out_specs=[_tile_spec(s.shape) for s in out_shapes],
        out_shape=out_shapes,
        compiler_params=_params("parallel"),
    )(x, meta_pad, g1, w_in)


def _conv_chain(u1, ln_g, ln_b):
    mu = jnp.mean(u1, axis=-1, keepdims=True)
    xc = u1 - mu
    rstd = lax.rsqrt(jnp.mean(xc * xc, axis=-1, keepdims=True) + EPS)
    xh = xc * rstd
    u2 = xh * ln_g + ln_b
    return xh, u2, u2 * _sigmoid(u2), rstd


def _fwd_conv(ag, conv_w, conv_b, ln_g, ln_b, out_g, n_rows):
    nt = n_rows // ROW_TILE

    def body(ag_ref, w_ref, b_ref, lg_ref, lb_ref, og_ref, mix_ref, u1_ref, ext_ref, conv_ref):
        i = pl.program_id(0)

        @pl.when(i == 0)
        def _():
            ext_ref[:, 0:CONV_HALO, :] = jnp.zeros((CONV_PLANES, CONV_HALO, _LANES), F32)

        ag_t = ag_ref[...]
        live = _row_ids(i, ROW_TILE) >= DEAD
        u0 = jnp.where(live, ag_t[:, :D_CONV] * _sigmoid(ag_t[:, D_CONV:]), 0.0)
        _to_planes(ext_ref, (), slice(CONV_HALO, None), u0)
        first = CONV_HALO - (CONV_WIDTH - 1)
        for c in range(CONV_PLANES):
            taps = w_ref[:, c * _LANES:(c + 1) * _LANES]
            for p in range(PHASES):
                acc = jnp.zeros((PHASE_ROWS, _LANES), F32)
                for k in range(CONV_WIDTH):
                    acc = acc + taps[k:k + 1, :] * ext_ref[c, _phase(first + k + p), :]
                conv_ref[c, _phase(p), :] = acc
        ext_ref[:, 0:CONV_HALO, :] = ext_ref[:, ROW_TILE:ROW_TILE + CONV_HALO, :]
        u1 = _from_planes(conv_ref, (), D_CONV) + b_ref[...]
        u1_ref[...] = u1
        _, _, u3, _ = _conv_chain(u1, lg_ref[...], lb_ref[...])
        mix_ref[...] = _rms_fwd(u3, og_ref[...]).astype(BF16)

    out_shapes = [jax.ShapeDtypeStruct((n_rows, D_CONV), BF16), jax.ShapeDtypeStruct((n_rows, D_CONV), F32)]
    small = [conv_w, conv_b, ln_g, ln_b, out_g]
    return pl.pallas_call(
        body, name="fwd_conv", grid=(nt,),
        in_specs=[_tile_spec(ag.shape)] + [_whole_spec(a.shape) for a in small],
        out_specs=[_tile_spec(s.shape) for s in out_shapes],
        out_shape=out_shapes,
        scratch_shapes=[pltpu.VMEM((CONV_PLANES, ROW_TILE + CONV_HALO, _LANES), F32),
                        pltpu.VMEM((CONV_PLANES, ROW_TILE, _LANES), F32)],
        compiler_params=_params("arbitrary"),
    )(ag, *small)


def _lane_tile(shape):
    if len(shape) == 2:
        return pl.BlockSpec((shape[0], ROW_TILE), lambda i: (0, i))
    return pl.BlockSpec((shape[0], shape[1], ROW_TILE), lambda i: (0, 0, i))


def _rope_rows(x, cos, sin):
    half = QK_ROPE // 2
    x1, x2 = x[:half], x[half:]
    return jnp.concatenate([x1 * cos - x2 * sin, x2 * cos + x1 * sin], axis=0)


def _rope_rows_t(dy, cos, sin):
    half = QK_ROPE // 2
    d1, d2 = dy[:half], dy[half:]
    return jnp.concatenate([d1 * cos + d2 * sin, d2 * cos - d1 * sin], axis=0)


def _fwd_qkv(cq, ckv, kr, gq, gkv, wq_t, w_ukv, wv_t, cos, sin, cos_t, sin_t, n_rows):
    nt = n_rows // ROW_TILE

    def body(cq_ref, ckv_ref, kr_ref, gq_ref, gkv_ref, wqt_ref, wkv_ref, wvt_ref, cos_ref, sin_ref, cost_ref, sint_ref,
             qt_ref, k_ref, v_ref, vt_ref, cqn_ref, ckvn_ref):
        cqn = _rms_fwd(cq_ref[...], gq_ref[...]).astype(BF16)
        ckvn = _rms_fwd(ckv_ref[...], gkv_ref[...]).astype(BF16)
        cqn_ref[...] = cqn
        ckvn_ref[...] = ckvn
        k_rot = _rope(kr_ref[...], cos_ref[...], sin_ref[...])
        cos_rows, sin_rows = cost_ref[...], sint_ref[...]
        for h in range(N_HEADS):
            q_raw = _dot_nt(wqt_ref[h], cqn)
            q_h = jnp.concatenate([q_raw[:QK_NOPE], _rope_rows(q_raw[QK_NOPE:], cos_rows, sin_rows)], axis=0)
            qt_ref[h] = (q_h * QK_LOGIT_SCALE).astype(BF16)
            kv = _dot(ckvn, wkv_ref[h])
            k_ref[h] = jnp.concatenate([kv[:, :QK_NOPE], k_rot], axis=-1).astype(BF16)
            v_ref[h] = kv[:, QK_NOPE:].astype(BF16)
            vt_ref[h] = _dot_nt(wvt_ref[h], ckvn).astype(BF16)

    out_shapes = [
        jax.ShapeDtypeStruct((N_HEADS, QK_DIM, n_rows), BF16),
        jax.ShapeDtypeStruct((N_HEADS, n_rows, QK_DIM), BF16),
        jax.ShapeDtypeStruct((N_HEADS, n_rows, V_HEAD), BF16),
        jax.ShapeDtypeStruct((N_HEADS, V_HEAD, n_rows), BF16),
        jax.ShapeDtypeStruct((n_rows, Q_LORA), BF16),
        jax.ShapeDtypeStruct((n_rows, KV_LORA), BF16),
    ]
    tiles = [cq, ckv, kr]
    whole = [gq, gkv, wq_t, w_ukv, wv_t]
    out_specs = [_lane_tile(out_shapes[0].shape), _tile_spec(out_shapes[1].shape), _tile_spec(out_shapes[2].shape),
                 _lane_tile(out_shapes[3].shape), _tile_spec(out_shapes[4].shape), _tile_spec(out_shapes[5].shape)]
    return pl.pallas_call(
        body, name="fwd_qkv", grid=(nt,),
        in_specs=[_tile_spec(a.shape) for a in tiles] + [_whole_spec(a.shape) for a in whole]
        + [_tile_spec(cos.shape), _tile_spec(sin.shape), _lane_tile(cos_t.shape), _lane_tile(sin_t.shape)],
        out_specs=out_specs,
        out_shape=out_shapes,
        compiler_params=_params("parallel"),
    )(*tiles, *whole, cos, sin, cos_t, sin_t)


def _chunk_of(rows):
    return jnp.where(rows >= ROW_TILE, lax.shift_right_arithmetic(rows - ROW_TILE, CHUNK_SHIFT) + 1, 0)


def _visible(i, j):
    k_rows = j * ROW_TILE + lax.broadcasted_iota(jnp.int32, (ROW_TILE, 1), 0)
    q_rows = i * ROW_TILE + lax.broadcasted_iota(jnp.int32, (1, ROW_TILE), 1)
    return jnp.logical_and(_chunk_of(q_rows) >= _chunk_of(k_rows), k_rows >= DEAD)


def _attn_fwd(q_t, k, v_t, n_rows):
    nt = n_rows // ROW_TILE

    def body(qt_ref, k_ref, vt_ref, ot_ref, lse_ref):
        i = pl.program_id(0)
        q_ts = [qt_ref[h] for h in range(N_HEADS)]

        def key_rows(j):
            return pl.ds(pl.multiple_of(j * ROW_TILE, ROW_TILE), ROW_TILE)

        def make_step(masked, tiles, first=0):
            def step(t, carry):
                js = [first + tiles * t + u for u in range(tiles)]
                scores = [[_dot(k_ref[h, key_rows(j), :], q_ts[h]) for h in range(N_HEADS)] for j in js]
                for j, tile_scores in zip(js, scores):
                    visible = _visible(i, j) if masked else None
                    probs, state = [], []
                    for h in range(N_HEADS):
                        m, l, _ = carry[h]
                        s = jnp.where(visible, tile_scores[h], NEG) if masked else tile_scores[h]
                        m_new = jnp.maximum(m, jnp.max(s, axis=0, keepdims=True))
                        alpha = jnp.exp2(m - m_new)
                        p = jnp.exp2(s - m_new)
                        probs.append(p.astype(BF16))
                        state.append((m_new, alpha * l + jnp.sum(p, axis=0, keepdims=True), alpha))
                    outs = [_dot(vt_ref[h, :, key_rows(j)], probs[h]) for h in range(N_HEADS)]
                    carry = tuple((state[h][0], state[h][1], state[h][2] * carry[h][2] + outs[h]) for h in range(N_HEADS))
                return carry
            return step

        init = tuple((jnp.full((1, ROW_TILE), NEG, F32), jnp.zeros((1, ROW_TILE), F32),
                      jnp.zeros((V_HEAD, ROW_TILE), F32)) for _ in range(N_HEADS))
        pairs = lax.shift_right_logical(jnp.maximum(i - 1, 0), 1)
        carry = make_step(True, 1)(0, init)
        carry = lax.fori_loop(0, pairs, make_step(False, 2, first=1), carry)
        carry = lax.fori_loop(1 + 2 * pairs, i, make_step(False, 1), carry)
        carry = lax.fori_loop(jnp.maximum(i, 1), i + 1, make_step(True, 1), carry)
        for h in range(N_HEADS):
            m, l, acc = carry[h]
            ot_ref[h] = acc / l
            lse_ref[h] = m + jnp.log2(l)

    out_shapes = [jax.ShapeDtypeStruct((N_HEADS, V_HEAD, n_rows), F32), jax.ShapeDtypeStruct((N_HEADS, 1, n_rows), F32)]
    return pl.pallas_call(
        body, name="attn_fwd", grid=(nt,),
        in_specs=[_lane_tile(q_t.shape), _whole_spec(k.shape), _whole_spec(v_t.shape)],
        out_specs=[_lane_tile(s.shape) for s in out_shapes],
        out_shape=out_shapes,
        compiler_params=_params("parallel"),
    )(q_t, k, v_t)


def _heads_to_rows(ref):
    return jnp.concatenate([ref[h] for h in range(N_HEADS)], axis=0)


def _rms_cols(x, g_col):
    r = lax.rsqrt(jnp.mean(x * x, axis=0, keepdims=True) + EPS)
    return x * r * g_col


def _fwd_out(x, meta_pad, mix_a, o_t, gb_col, w_out, n_rows):
    nt = n_rows // ROW_TILE

    def body(x_ref, meta_ref, mixa_ref, ot_ref, gb_ref, w_ref, mixbt_ref, h1_ref):
        i = pl.program_id(0)
        h0 = jnp.where(i == 0, meta_ref[...], x_ref[...])
        mix_bt = _rms_cols(_heads_to_rows(ot_ref), gb_ref[...]).astype(BF16)
        mixbt_ref[...] = mix_bt
        h1_ref[...] = h0 + _dot(mixa_ref[...], w_ref[:D_CONV, :]) + _dot_tn(mix_bt, w_ref[D_CONV:, :])

    out_shapes = [jax.ShapeDtypeStruct((D_ATTN, n_rows), BF16), jax.ShapeDtypeStruct((n_rows, D_MODEL), F32)]
    return pl.pallas_call(
        body, name="fwd_out", grid=(nt,),
        in_specs=[_real_spec(D_MODEL), _whole_spec(meta_pad.shape), _tile_spec(mix_a.shape), _lane_tile(o_t.shape),
                  _whole_spec(gb_col.shape), _whole_spec(w_out.shape)],
        out_specs=[_lane_tile(out_shapes[0].shape), _tile_spec(out_shapes[1].shape)],
        out_shape=out_shapes,
        compiler_params=_params("parallel"),
    )(x, meta_pad, mix_a, o_t, gb_col, w_out)


PHASES = 8
PHASE_ROWS = ROW_TILE // PHASES
UP_PLANES = -(-UP_SLAB // _LANES)
UP_PAD = UP_PLANES * _LANES
CONV_PLANES = D_CONV // _LANES


def _phase(start):
    return pl.ds(start, PHASE_ROWS, stride=PHASES)


def _to_planes(ref, lead, rows, value):
    width = value.shape[-1]
    for c in range(-(-width // _LANES)):
        part = value[:, c * _LANES:min((c + 1) * _LANES, width)]
        if part.shape[-1] < _LANES:
            part = jnp.concatenate([part, jnp.zeros((part.shape[0], _LANES - part.shape[-1]), part.dtype)], axis=-1)
        ref[(*lead, c, rows, slice(None))] = part


def _from_planes(ref, lead, width):
    planes = [ref[(*lead, c)] for c in range(-(-width // _LANES))]
    last = width - (len(planes) - 1) * _LANES
    return jnp.concatenate(planes[:-1] + [planes[-1][:, :last]], axis=-1)


def _fwd_ffn(h1, target, g2, w_up, fw, fb, w_down, gf, n_rows):
    nt = n_rows // ROW_TILE

    def body(h1_ref, t_ref, g2_ref, wup_ref, fw_ref, fb_ref, wdn_ref, gf_ref,
             n2_ref, up0_ref, act_ref, da_ref, db_ref, dh2_ref, loss_ref, dgf_ref, ext_ref):
        i = pl.program_id(0)

        @pl.when(i == 0)
        def _():
            ext_ref[:, 0:FFN_HALO, :] = jnp.zeros((N_DEV, FFN_HALO, UP_SLAB), F32)

        h1_t = h1_ref[...]
        live = _row_ids(i, ROW_TILE) >= DEAD
        n2 = jnp.where(live, _rms_fwd(h1_t, g2_ref[...]), 0.0).astype(BF16)
        n2_ref[...] = n2
        for s in range(N_DEV):
            up0 = _dot_nt(n2, wup_ref[s])
            up0_ref[s] = up0.astype(BF16)
            ext_ref[s, FFN_HALO:, :] = up0
        first = FFN_HALO - (FFN_CONV_WIDTH - 1)

        def conv(s):
            block = ext_ref[s]
            acc = fb_ref[s, :, :UP_SLAB] + fw_ref[s, FFN_CONV_WIDTH - 1:FFN_CONV_WIDTH, :UP_SLAB] * block[FFN_HALO:]
            for back in range(1, FFN_CONV_WIDTH):
                k = FFN_CONV_WIDTH - 1 - back
                acc = acc + fw_ref[s, k:k + 1, :UP_SLAB] * pltpu.roll(block, back, 0)[FFN_HALO:]
            return acc

        h2 = h1_t
        for s in range(N_ACT_SLAB):
            gate = conv(s)
            val = conv(s + N_ACT_SLAB)
            sg = _sigmoid(gate)
            silu = gate * sg
            act = (silu * val).astype(BF16)
            act_ref[s] = act
            da_ref[s] = (val * sg * (1.0 + gate * (1.0 - sg))).astype(BF16)
            db_ref[s] = silu.astype(BF16)
            h2 = h2 + _dot(act, wdn_ref[s])
        ext_ref[:, 0:FFN_HALO, :] = ext_ref[:, ROW_TILE:ROW_TILE + FFN_HALO, :]

        gf_t = gf_ref[...]
        y = _rms_fwd(h2, gf_t)
        diff = jnp.where(i >= 1, y - t_ref[...], 0.0)
        tile_loss = 0.5 * jnp.sum(jnp.sum(diff * diff, axis=-1, keepdims=True), axis=0, keepdims=True) / D_MODEL
        dh2, dgf = _rms_bwd(diff / D_MODEL, h2, gf_t)
        dh2_ref[...] = dh2
        _accumulate(loss_ref, i == 0, jnp.broadcast_to(tile_loss, loss_ref.shape))
        _accumulate(dgf_ref, i == 0, dgf)

    act_like = jax.ShapeDtypeStruct((N_ACT_SLAB, n_rows, UP_SLAB), BF16)
    out_shapes = [
        jax.ShapeDtypeStruct((n_rows, D_MODEL), BF16),
        jax.ShapeDtypeStruct((N_DEV, n_rows, UP_SLAB), BF16),
        act_like, act_like, act_like,
        jax.ShapeDtypeStruct((n_rows, D_MODEL), F32),
        jax.ShapeDtypeStruct((8, 128), F32),
        jax.ShapeDtypeStruct((1, D_MODEL), F32),
    ]
    whole = [g2, w_up, fw, fb, w_down, gf]
    return pl.pallas_call(
        body, name="fwd_ffn", grid=(nt,),
        in_specs=[_tile_spec(h1.shape), _real_spec(D_MODEL)] + [_whole_spec(a.shape) for a in whole],
        out_specs=[_tile_spec(s.shape) for s in out_shapes[:6]] + [_acc_spec(s.shape) for s in out_shapes[6:]],
        out_shape=out_shapes,
        scratch_shapes=[pltpu.VMEM((N_DEV, ROW_TILE + FFN_HALO, UP_SLAB), F32)],
        compiler_params=_params("arbitrary"),
    )(h1, target, *whole)


def _rope_tables(n_rows):
    pos = jnp.maximum(jnp.arange(n_rows, dtype=jnp.int32) - DEAD, 0)
    inv_freq = 1.0 / (ROPE_THETA ** (jnp.arange(0, QK_ROPE, 2, dtype=F32) / QK_ROPE))
    ang_t = inv_freq[:, None] * pos.astype(F32)[None, :]
    return jnp.cos(ang_t), jnp.sin(ang_t)


def _halo_after(shape, halo, n_rows):
    last = n_rows // halo - 1
    step = ROW_TILE // halo
    if len(shape) == 2:
        return pl.BlockSpec((halo, shape[1]), lambda i: (jnp.minimum((i + 1) * step, last), 0))
    return pl.BlockSpec((shape[0], halo, shape[2]), lambda i: (0, jnp.minimum((i + 1) * step, last), 0))


def _halo_before(shape, halo):
    step = ROW_TILE // halo
    if len(shape) == 2:
        return pl.BlockSpec((halo, shape[1]), lambda i: (jnp.maximum(i * step - 1, 0), 0))
    return pl.BlockSpec((shape[0], halo, shape[2]), lambda i: (0, jnp.maximum(i * step - 1, 0), 0))


def _bwd_ffn_act(dh2, da, db, w_down, n_rows):
    nt = n_rows // ROW_TILE

    def body(dh2_ref, da_ref, db_ref, wdn_ref, dup_ref, dfb_ref):
        i = pl.program_id(0)

        @pl.when(i == 0)
        def _():
            dfb_ref[...] = jnp.zeros_like(dfb_ref)

        dh2_b = dh2_ref[...].astype(BF16)
        for s in range(N_ACT_SLAB):
            d_act = _dot_nt(dh2_b, wdn_ref[s])
            d_gate = d_act * da_ref[s].astype(F32)
            d_val = d_act * db_ref[s].astype(F32)
            dup_ref[s] = d_gate.astype(BF16)
            dup_ref[s + N_ACT_SLAB] = d_val.astype(BF16)
            dfb_ref[s] += jnp.sum(d_gate, axis=0, keepdims=True)
            dfb_ref[s + N_ACT_SLAB] += jnp.sum(d_val, axis=0, keepdims=True)

    out_shapes = [jax.ShapeDtypeStruct((N_DEV, n_rows, UP_SLAB), BF16), jax.ShapeDtypeStruct((N_DEV, 1, UP_SLAB), F32)]
    return pl.pallas_call(
        body, name="bwd_ffn_act", grid=(nt,),
        in_specs=[_tile_spec(dh2.shape), _tile_spec(da.shape), _tile_spec(db.shape), _whole_spec(w_down.shape)],
        out_specs=[_tile_spec(out_shapes[0].shape), _acc_spec(out_shapes[1].shape)],
        out_shape=out_shapes,
        compiler_params=_params("arbitrary"),
    )(dh2, da, db, w_down)


def _bwd_ffn_up(dup, up0, h1, dh2, g2, w_up, fw, n_rows):
    nt = n_rows // ROW_TILE
    last_tap = FFN_CONV_WIDTH - 1
    ext_rows = ROW_TILE + FFN_HALO

    def body(dup_ref, dnext_ref, up0_ref, h1_ref, dh2_ref, g2_ref, wup_ref, fw_ref,
             dup0_ref, dh1_ref, dfw_ref, dg2_ref):
        i = pl.program_id(0)

        @pl.when(i == 0)
        def _():
            dfw_ref[...] = jnp.zeros_like(dfw_ref)

        live = _row_ids(i, ROW_TILE) >= DEAD
        dn2 = jnp.zeros((ROW_TILE, D_MODEL), F32)
        for s in range(N_DEV):
            d = dup_ref[s].astype(F32)
            block = jnp.concatenate([d, jnp.where(i == nt - 1, 0.0, dnext_ref[s].astype(F32))], axis=0)
            u = up0_ref[s].astype(F32)
            dup0 = fw_ref[s, last_tap:last_tap + 1, :UP_SLAB] * d
            dfw_ref[s, last_tap:last_tap + 1, :UP_SLAB] += jnp.sum(d * u, axis=0, keepdims=True)
            for ahead in range(1, FFN_CONV_WIDTH):
                k = last_tap - ahead
                shifted = pltpu.roll(block, ext_rows - ahead, 0)[:ROW_TILE]
                dup0 = dup0 + fw_ref[s, k:k + 1, :UP_SLAB] * shifted
                dfw_ref[s, k:k + 1, :UP_SLAB] += jnp.sum(shifted * u, axis=0, keepdims=True)
            dup0_b = jnp.where(live, dup0, 0.0).astype(BF16)
            dup0_ref[s] = dup0_b
            dn2 = dn2 + _dot(dup0_b, wup_ref[s])
        dx, dg2 = _rms_bwd(dn2, h1_ref[...], g2_ref[...])
        dh1_ref[...] = dh2_ref[...] + dx
        _accumulate(dg2_ref, i == 0, dg2)

    out_shapes = [
        jax.ShapeDtypeStruct((N_DEV, n_rows, UP_SLAB), BF16),
        jax.ShapeDtypeStruct((n_rows, D_MODEL), F32),
        jax.ShapeDtypeStruct((N_DEV, FFN_CONV_WIDTH, UP_PAD), F32),
        jax.ShapeDtypeStruct((1, D_MODEL), F32),
    ]
    return pl.pallas_call(
        body, name="bwd_ffn_up", grid=(nt,),
        in_specs=[_tile_spec(dup.shape), _halo_after(dup.shape, FFN_HALO, n_rows), _tile_spec(up0.shape),
                  _tile_spec(h1.shape), _tile_spec(dh2.shape),
                  _whole_spec(g2.shape), _whole_spec(w_up.shape), _whole_spec(fw.shape)],
        out_specs=[_tile_spec(s.shape) for s in out_shapes[:2]] + [_acc_spec(s.shape) for s in out_shapes[2:]],
        out_shape=out_shapes,
        compiler_params=_params("arbitrary"),
    )(dup, dup, up0, h1, dh2, g2, w_up, fw)


def _bwd_out(dh1, o_t, u1, w_out, gb_col, ln_g, ln_b, ga, n_rows):
    nt = n_rows // ROW_TILE

    def body(dh1_ref, ot_ref, u1_ref, w_ref, gb_ref, lg_ref, lb_ref, ga_ref,
             dot_ref, delta_ref, du1_ref, dgb_ref, dga_ref, dlg_ref, dlb_ref, dcb_ref):
        i = pl.program_id(0)
        dh1_b = dh1_ref[...].astype(BF16)
        o_t = _heads_to_rows(ot_ref)
        gb = gb_ref[...]
        r = lax.rsqrt(jnp.mean(o_t * o_t, axis=0, keepdims=True) + EPS)
        dmix_bt = _dot_nt(w_ref[D_CONV:, :], dh1_b)
        wgt = dmix_bt * gb
        do_t = r * wgt - o_t * (r * r * r) * jnp.mean(wgt * o_t, axis=0, keepdims=True)
        dgb = jnp.sum(dmix_bt * o_t * r, axis=1, keepdims=True)
        for h in range(N_HEADS):
            do_h = do_t[h * V_HEAD:(h + 1) * V_HEAD]
            dot_ref[h] = do_h.astype(BF16)
            delta_ref[h] = jnp.sum(do_h * ot_ref[h], axis=0, keepdims=True)
        lg = lg_ref[...]
        xh, u2, u3, rstd = _conv_chain(u1_ref[...], lg, lb_ref[...])
        du3, dga = _rms_bwd(_dot_nt(dh1_b, w_ref[:D_CONV, :]), u3, ga_ref[...])
        sg = _sigmoid(u2)
        du2 = du3 * sg * (1.0 + u2 * (1.0 - sg))
        dxh = du2 * lg
        du1 = rstd * (dxh - jnp.mean(dxh, axis=-1, keepdims=True) - xh * jnp.mean(dxh * xh, axis=-1, keepdims=True))
        du1_ref[...] = du1
        first = i == 0
        _accumulate(dgb_ref, first, dgb)
        _accumulate(dga_ref, first, dga)
        _accumulate(dlg_ref, first, jnp.sum(du2 * xh, axis=0, keepdims=True))
        _accumulate(dlb_ref, first, jnp.sum(du2, axis=0, keepdims=True))
        _accumulate(dcb_ref, first, jnp.sum(du1, axis=0, keepdims=True))

    out_shapes = [
        jax.ShapeDtypeStruct((N_HEADS, V_HEAD, n_rows), BF16),
        jax.ShapeDtypeStruct((N_HEADS, 1, n_rows), F32),
        jax.ShapeDtypeStruct((n_rows, D_CONV), F32),
        jax.ShapeDtypeStruct((D_ATTN, 1), F32),
    ] + [jax.ShapeDtypeStruct((1, D_CONV), F32)] * 4
    whole = [w_out, gb_col, ln_g, ln_b, ga]
    return pl.pallas_call(
        body, name="bwd_out", grid=(nt,),
        in_specs=[_tile_spec(dh1.shape), _lane_tile(o_t.shape), _tile_spec(u1.shape)] + [_whole_spec(a.shape) for a in whole],
        out_specs=[_lane_tile(out_shapes[0].shape), _lane_tile(out_shapes[1].shape), _tile_spec(out_shapes[2].shape)]
        + [_acc_spec(s.shape) for s in out_shapes[3:]],
        out_shape=out_shapes,
        compiler_params=_params("arbitrary"),
    )(dh1, o_t, u1, *whole)


ATTN_BWD_HEADS = 8


def _attn_bwd(q_t, k, v, do_t, lse, delta, n_rows):
    nt = n_rows // ROW_TILE
    hp = ATTN_BWD_HEADS

    def body(k_ref, v_ref, qt_ref, dot_ref, lse_ref, delta_ref, dqt_ref, dk_ref, dv_ref):
        j = pl.program_id(1)

        @pl.when(j == 0)
        def _():
            dqt_ref[...] = jnp.zeros_like(dqt_ref)

        k_ts = [k_ref[h] for h in range(hp)]
        v_ts = [v_ref[h] for h in range(hp)]

        def make_step(masked, tiles, first=0):
            def step(t, carry):
                tiles_of_step = []
                for u in range(tiles):
                    i = first + tiles * t + u
                    cols = pl.ds(pl.multiple_of(i * ROW_TILE, ROW_TILE), ROW_TILE)
                    q_is = [qt_ref[h, :, cols] for h in range(hp)]
                    do_is = [dot_ref[h, :, cols] for h in range(hp)]
                    scores = [_dot(k_ts[h], q_is[h]) for h in range(hp)]
                    dps = [_dot(v_ts[h], do_is[h]) for h in range(hp)]
                    tiles_of_step.append((i, cols, q_is, do_is, scores, dps))
                for i, cols, q_is, do_is, scores, dps in tiles_of_step:
                    visible = _visible(i, j) if masked else None
                    probs, dss = [], []
                    for h in range(hp):
                        s = jnp.where(visible, scores[h], NEG) if masked else scores[h]
                        p = jnp.exp2(s - lse_ref[h, :, cols])
                        probs.append(p.astype(BF16))
                        dss.append((p * (dps[h] - delta_ref[h, :, cols])).astype(BF16))
                    out = []
                    for h in range(hp):
                        dk, dv = carry[h]
                        dv = dv + _dot_nt(probs[h], do_is[h])
                        dk = dk + _dot_nt(dss[h], q_is[h])
                        dqt_ref[h, :, cols] += _dot_tn(k_ts[h], dss[h])
                        out.append((dk, dv))
                    carry = tuple(out)
                return carry
            return step

        init = tuple((jnp.zeros((ROW_TILE, QK_DIM), F32), jnp.zeros((ROW_TILE, V_HEAD), F32)) for _ in range(hp))
        carry = make_step(True, 1)(j, init)
        carry = lax.fori_loop(jnp.where(j == 0, j + 1, nt), nt, make_step(True, 1), carry)
        pairs = jnp.where(j == 0, 0, lax.shift_right_logical(nt - 1 - j, 1))
        carry = lax.fori_loop(0, pairs, make_step(False, 2, first=j + 1), carry)
        carry = lax.fori_loop(jnp.where(j == 0, nt, j + 1 + 2 * pairs), nt, make_step(False, 1), carry)
        for h in range(hp):
            dk_ref[h] = carry[h][0] * _LN2
            dv_ref[h] = carry[h][1]

    key_tile = lambda w: pl.BlockSpec((hp, ROW_TILE, w), lambda g, j: (g, j, 0))
    all_cols = lambda w: pl.BlockSpec((hp, w, n_rows), lambda g, j: (g, 0, 0))
    resident = lambda w: pl.BlockSpec((hp, w, n_rows), lambda g, j: (g, 0, 0), pipeline_mode=pl.Buffered(1))
    out_shapes = [
        jax.ShapeDtypeStruct((N_HEADS, QK_DIM, n_rows), F32),
        jax.ShapeDtypeStruct((N_HEADS, n_rows, QK_DIM), F32),
        jax.ShapeDtypeStruct((N_HEADS, n_rows, V_HEAD), F32),
    ]
    return pl.pallas_call(
        body, name="attn_bwd", grid=(N_HEADS // hp, nt),
        in_specs=[key_tile(QK_DIM), key_tile(V_HEAD), resident(QK_DIM), resident(V_HEAD), resident(1), resident(1)],
        out_specs=[all_cols(QK_DIM), key_tile(QK_DIM), key_tile(V_HEAD)],
        out_shape=out_shapes,
        compiler_params=_params("parallel", "arbitrary"),
    )(k, v, q_t, do_t, lse, delta)


def _bwd_qkv(dq_t, dk, dv, cq, ckv, gq, gkv, wq_t, w_ukv, cos, sin, cos_t, sin_t, n_rows):
    nt = n_rows // ROW_TILE

    def body(dqt_ref, dk_ref, dv_ref, cq_ref, ckv_ref, gq_ref, gkv_ref, wqt_ref, wkv_ref, cos_ref, sin_ref,
             cost_ref, sint_ref, dqraw_ref, dkv_ref, dcq_ref, dckv_ref, dkr_ref, dgq_ref, dgkv_ref):
        i = pl.program_id(0)
        cos_rows, sin_rows = cost_ref[...], sint_ref[...]
        dcqn = jnp.zeros((ROW_TILE, Q_LORA), F32)
        dckvn = jnp.zeros((ROW_TILE, KV_LORA), F32)
        dk_rot = jnp.zeros((ROW_TILE, QK_ROPE), F32)
        for h in range(N_HEADS):
            dq_h, dk_h = dqt_ref[h] * QK_DIM ** -0.5, dk_ref[h]
            dq_raw = jnp.concatenate(
                [dq_h[:QK_NOPE], _rope_rows_t(dq_h[QK_NOPE:], cos_rows, sin_rows)], axis=0).astype(BF16)
            dqraw_ref[h] = dq_raw
            dcqn = dcqn + _dot_tn(dq_raw, wqt_ref[h])
            dkv = jnp.concatenate([dk_h[:, :QK_NOPE], dv_ref[h]], axis=-1).astype(BF16)
            dkv_ref[:, h * KV_HEAD:(h + 1) * KV_HEAD] = dkv
            dckvn = dckvn + _dot_nt(dkv, wkv_ref[h])
            dk_rot = dk_rot + dk_h[:, QK_NOPE:]
        dkr_ref[...] = _rope_t(dk_rot, cos_ref[...], sin_ref[...]).astype(BF16)
        dcq, dgq = _rms_bwd(dcqn, cq_ref[...], gq_ref[...])
        dckv, dgkv = _rms_bwd(dckvn, ckv_ref[...], gkv_ref[...])
        dcq_ref[...] = dcq.astype(BF16)
        dckv_ref[...] = dckv.astype(BF16)
        _accumulate(dgq_ref, i == 0, dgq)
        _accumulate(dgkv_ref, i == 0, dgkv)

    out_shapes = [
        jax.ShapeDtypeStruct((N_HEADS, QK_DIM, n_rows), BF16),
        jax.ShapeDtypeStruct((n_rows, N_HEADS * KV_HEAD), BF16),
        jax.ShapeDtypeStruct((n_rows, Q_LORA), BF16),
        jax.ShapeDtypeStruct((n_rows, KV_LORA), BF16),
        jax.ShapeDtypeStruct((n_rows, QK_ROPE), BF16),
        jax.ShapeDtypeStruct((1, Q_LORA), F32),
        jax.ShapeDtypeStruct((1, KV_LORA), F32),
    ]
    tiles = [dk, dv, cq, ckv]
    whole = [gq, gkv, wq_t, w_ukv]
    return pl.pallas_call(
        body, name="bwd_qkv", grid=(nt,),
        in_specs=[_lane_tile(dq_t.shape)] + [_tile_spec(a.shape) for a in tiles] + [_whole_spec(a.shape) for a in whole]
        + [_tile_spec(cos.shape), _tile_spec(sin.shape), _lane_tile(cos_t.shape), _lane_tile(sin_t.shape)],
        out_specs=[_lane_tile(out_shapes[0].shape)] + [_tile_spec(s.shape) for s in out_shapes[1:5]]
        + [_acc_spec(s.shape) for s in out_shapes[5:]],
        out_shape=out_shapes,
        compiler_params=_params("arbitrary"),
    )(dq_t, *tiles, *whole, cos, sin, cos_t, sin_t)


def _bwd_conv(du1, ag, conv_w, dcq, dckv, dkr, n_rows):
    nt = n_rows // ROW_TILE

    last_tap = CONV_WIDTH - 1

    def body(du1_ref, dnext_ref, ag_ref, w_ref, dcq_ref, dckv_ref, dkr_ref, dz_ref, dw_ref,
             dext_ref, uext_ref, conv_ref, sums_ref):
        i = pl.program_id(0)

        @pl.when(i == 0)
        def _():
            sums_ref[...] = jnp.zeros_like(sums_ref)

        _to_planes(dext_ref, (), slice(0, ROW_TILE), du1_ref[...])
        _to_planes(dext_ref, (), slice(ROW_TILE, None), jnp.where(i == nt - 1, 0.0, dnext_ref[...]))
        ag_t = ag_ref[...]
        live = _row_ids(i, ROW_TILE) >= DEAD
        sg = _sigmoid(ag_t[:, D_CONV:])
        _to_planes(uext_ref, (), slice(None), jnp.where(live, ag_t[:, :D_CONV] * sg, 0.0))
        for c in range(CONV_PLANES):
            taps = w_ref[:, c * _LANES:(c + 1) * _LANES]
            for p in range(PHASES):
                u = uext_ref[c, _phase(p), :]
                acc = jnp.zeros((PHASE_ROWS, _LANES), F32)
                for k in range(CONV_WIDTH):
                    shifted = dext_ref[c, _phase(p + last_tap - k), :]
                    acc = acc + taps[k:k + 1, :] * shifted
                    sums_ref[c, k] += shifted * u
                conv_ref[c, _phase(p), :] = acc
        du0 = jnp.where(live, _from_planes(conv_ref, (), D_CONV), 0.0)
        da = du0 * sg
        dgate = du0 * ag_t[:, :D_CONV] * sg * (1.0 - sg)
        dz_ref[...] = jnp.concatenate(
            [da.astype(BF16), dgate.astype(BF16), dcq_ref[...], dckv_ref[...], dkr_ref[...]], axis=-1)

        @pl.when(i == nt - 1)
        def _():
            for c in range(CONV_PLANES):
                for k in range(CONV_WIDTH):
                    dw_ref[k:k + 1, c * _LANES:(c + 1) * _LANES] = jnp.sum(sums_ref[c, k], axis=0, keepdims=True)

    out_shapes = [jax.ShapeDtypeStruct((n_rows, D_IN), BF16), jax.ShapeDtypeStruct((CONV_WIDTH, D_CONV), F32)]
    return pl.pallas_call(
        body, name="bwd_conv", grid=(nt,),
        in_specs=[_tile_spec(du1.shape), _halo_after(du1.shape, CONV_HALO, n_rows), _tile_spec(ag.shape),
                  _whole_spec(conv_w.shape), _tile_spec(dcq.shape), _tile_spec(dckv.shape), _tile_spec(dkr.shape)],
        out_specs=[_tile_spec(out_shapes[0].shape), _acc_spec(out_shapes[1].shape)],
        out_shape=out_shapes,
        scratch_shapes=[pltpu.VMEM((CONV_PLANES, ROW_TILE + CONV_HALO, _LANES), F32),
                        pltpu.VMEM((CONV_PLANES, ROW_TILE, _LANES), F32), pltpu.VMEM((CONV_PLANES, ROW_TILE, _LANES), F32),
                        pltpu.VMEM((CONV_PLANES, CONV_WIDTH, PHASE_ROWS, _LANES), F32)],
        compiler_params=_params("arbitrary"),
    )(du1, du1, ag, conv_w, dcq, dckv, dkr)


def _bwd_in(dz, x, meta_pad, dh1, g1, w_in, n_rows):
    nt = n_rows // ROW_TILE

    def body(dz_ref, x_ref, meta_ref, dh1_ref, g_ref, w_ref, gx_ref, gmeta_ref, dg1_ref):
        i = pl.program_id(0)
        h0 = jnp.where(i == 0, meta_ref[...], x_ref[...])
        dx, dg1 = _rms_bwd(_dot(dz_ref[...], w_ref[...]), h0, g_ref[...])
        dh0 = dh1_ref[...] + dx
        gx_ref[...] = dh0

        @pl.when(i == 0)
        def _():
            gmeta_ref[...] = dh0

        _accumulate(dg1_ref, i == 0, dg1)

    out_shapes = [
        jax.ShapeDtypeStruct((n_rows - ROW_TILE, D_MODEL), F32),
        jax.ShapeDtypeStruct((ROW_TILE, D_MODEL), F32),
        jax.ShapeDtypeStruct((1, D_MODEL), F32),
    ]
    return pl.pallas_call(
        body, name="bwd_in", grid=(nt,),
        in_specs=[_tile_spec(dz.shape), _real_spec(D_MODEL), _whole_spec(meta_pad.shape), _tile_spec(dh1.shape),
                  _whole_spec(g1.shape), _whole_spec(w_in.shape)],
        out_specs=[_real_spec(D_MODEL), _acc_spec(out_shapes[1].shape), _acc_spec(out_shapes[2].shape)],
        out_shape=out_shapes,
        compiler_params=_params("arbitrary"),
    )(dz, x, meta_pad, dh1, g1, w_in)


def _contraction_tile(n_rows):
    return next(t for t in range(n_rows // 2 // _LANES * _LANES, 0, -_LANES) if n_rows % t == 0)


def _weight_grad(a, b, name, a_transposed=False):
    groups = max(a.shape[0] if a.ndim == 3 else 1, b.shape[0] if b.ndim == 3 else 1)
    n_rows, n = b.shape[-2], b.shape[-1]
    m = a.shape[-2] if a_transposed else a.shape[-1]
    kt = _contraction_tile(n_rows)
    steps = n_rows // kt

    def body(a_ref, b_ref, out_ref, acc_ref):
        i = pl.program_id(1)
        a_t, b_t = a_ref[...].astype(BF16), b_ref[...].astype(BF16)
        part = _dot(a_t, b_t) if a_transposed else _dot_tn(a_t, b_t)
        _accumulate(acc_ref, i == 0, part)

        @pl.when(i == steps - 1)
        def _():
            out_ref[...] = acc_ref[...].astype(out_ref.dtype)

    def spec(arr, rows_last):
        block = (arr.shape[-2], kt) if rows_last else (kt, arr.shape[-1])
        at = (lambda i: (0, i)) if rows_last else (lambda i: (i, 0))
        if arr.ndim == 3:
            return pl.BlockSpec((None,) + block, lambda g, i: (g,) + at(i))
        return pl.BlockSpec(block, lambda g, i: at(i))

    return pl.pallas_call(
        body, name=name, grid=(groups, steps),
        in_specs=[spec(a, a_transposed), spec(b, False)],
        out_specs=pl.BlockSpec((None, m, n), lambda g, i: (g, 0, 0)),
        out_shape=jax.ShapeDtypeStruct((groups, m, n), BF16),
        scratch_shapes=[pltpu.VMEM((m, n), F32)],
        compiler_params=_params("parallel", "arbitrary"),
    )(a, b)


def _my_index():
    return 4 * lax.axis_index("x") + 2 * lax.axis_index("y") + lax.axis_index("c")


def _peer(k):
    flip = lambda v, bit: 1 - v if bit else v
    px = flip(lax.axis_index("x"), k & 4)
    py = flip(lax.axis_index("y"), k & 2)
    pc = flip(lax.axis_index("c"), k & 1)
    return (px, py, pc), 4 * px + 2 * py + pc


def _all_gather(shards, dtypes):
    n = len(shards)
    sibling, chips = 1, (2, 4, 6)

    def body(*refs):
        ins, outs, stages = refs[:n], refs[n:2 * n], refs[2 * n:3 * n]
        send_sems, recv_sems, local_sems = refs[3 * n:]
        me = _my_index()
        for a in range(n):
            stages[a][...] = ins[a][...].astype(stages[a].dtype)
        local = [pltpu.make_async_copy(stages[a], outs[a].at[me], local_sems.at[a]) for a in range(n)]
        for cp in local:
            cp.start()

        def copy(a, k, src, slot, to):
            return pltpu.make_async_remote_copy(
                src_ref=src, dst_ref=outs[a].at[slot], send_sem=send_sems.at[a, k - 1],
                recv_sem=recv_sems.at[a, k - 1], device_id=_peer(to)[0], device_id_type=MESH)

        def own(a, k):
            return copy(a, k, stages[a], me, k)

        def passed(a, k):
            slot = _peer(k)[1]
            return copy(a, k ^ sibling, outs[a].at[slot], slot, sibling)

        def arrival(a, k):
            return copy(a, k, stages[a], _peer(k)[1], k)

        for k in (sibling,) + chips:
            for a in range(n):
                own(a, k).start()
        for k in chips:
            for a in range(n):
                arrival(a, k).wait_recv()
                passed(a, k).start()
        for a in range(n):
            arrival(a, sibling).wait_recv()
            for k in chips:
                arrival(a, k ^ sibling).wait_recv()
        for a in range(n):
            for k in (sibling,) + chips:
                own(a, k).wait_send()
            for k in chips:
                passed(a, k).wait_send()
        for cp in local:
            cp.wait()

    return pl.pallas_call(
        body, name="gather_weights",
        in_specs=[pl.BlockSpec(memory_space=pltpu.VMEM)] * n,
        out_specs=[pl.BlockSpec(memory_space=pl.ANY)] * n,
        out_shape=[jax.ShapeDtypeStruct((N_DEV,) + s.shape, dt) for s, dt in zip(shards, dtypes)],
        scratch_shapes=[pltpu.VMEM(s.shape, dt) for s, dt in zip(shards, dtypes)]
        + [pltpu.SemaphoreType.DMA((n, N_DEV - 1)), pltpu.SemaphoreType.DMA((n, N_DEV - 1)), pltpu.SemaphoreType.DMA((n,))],
        compiler_params=pltpu.CompilerParams(vmem_limit_bytes=VMEM_LIMIT),
    )(*shards)


def _exchange(parts, whole):
    n = len(parts)

    def body(*refs):
        ins, outs = refs[:n], refs[n:2 * n]
        send_sems, recv_sems, local_sems = refs[2 * n:]
        me = _my_index()

        def src(a, slab):
            return ins[a] if whole[a] else ins[a].at[slab]

        local = [pltpu.make_async_copy(src(a, me), outs[a].at[me], local_sems.at[a]) for a in range(n)]
        for cp in local:
            cp.start()

        def copy(a, k, slab, slot):
            peer, _ = _peer(k)
            return pltpu.make_async_remote_copy(
                src_ref=src(a, slab), dst_ref=outs[a].at[slot], send_sem=send_sems.at[a, k - 1],
                recv_sem=recv_sems.at[a, k - 1], device_id=peer, device_id_type=MESH)

        for k in range(1, N_DEV):
            for a in range(n):
                copy(a, k, _peer(k)[1], me).start()
        for k in range(1, N_DEV):
            for a in range(n):
                copy(a, k, _peer(k)[1], _peer(k)[1]).wait()
        for cp in local:
            cp.wait()

    return pl.pallas_call(
        body, name="exchange_grads",
        in_specs=[pl.BlockSpec(memory_space=pl.ANY)] * n,
        out_specs=[pl.BlockSpec(memory_space=pl.ANY)] * n,
        out_shape=[jax.ShapeDtypeStruct(((N_DEV,) + p.shape) if w else p.shape, p.dtype) for p, w in zip(parts, whole)],
        scratch_shapes=[pltpu.SemaphoreType.DMA((n, N_DEV - 1)), pltpu.SemaphoreType.DMA((n, N_DEV - 1)),
                        pltpu.SemaphoreType.DMA((n,))],
    )(*parts)


def _sequencer_exchange(parts, whole, name, collective_id):
    n = len(parts)
    srcs = [jax.new_ref(p, memory_space=pltpu.MemorySpace.HBM) for p in parts]
    lands = [jax.empty_ref(jax.ShapeDtypeStruct(((N_DEV,) + p.shape) if w else p.shape, p.dtype),
                           memory_space=pltpu.MemorySpace.HBM) for p, w in zip(parts, whole)]

    @pl.kernel(mesh=plsc.ScalarSubcoreMesh(axis_name="sequencer", num_cores=1), name=name,
               scratch_types=(pltpu.SemaphoreType.DMA((n, N_DEV - 1)), pltpu.SemaphoreType.DMA((n, N_DEV - 1)),
                              pltpu.SemaphoreType.DMA((n,))),
               compiler_params=pltpu.CompilerParams(collective_id=collective_id))
    def launch(send_sems, recv_sems, local_sems):
        barrier = pltpu.get_barrier_semaphore()
        for k in range(1, N_DEV):
            pl.semaphore_signal(barrier, inc=1, device_id=_peer(k)[0], device_id_type=MESH)
        pl.semaphore_wait(barrier, N_DEV - 1)
        me = _my_index()

        def src(a, slab):
            return srcs[a] if whole[a] else srcs[a].at[slab]

        local = [pltpu.make_async_copy(src(a, me), lands[a].at[me], local_sems.at[a]) for a in range(n)]
        for cp in local:
            cp.start()

        def copy(a, k, slab, slot):
            return pltpu.make_async_remote_copy(
                src_ref=src(a, slab), dst_ref=lands[a].at[slot], send_sem=send_sems.at[a, k - 1],
                recv_sem=recv_sems.at[a, k - 1], device_id=_peer(k)[0], device_id_type=MESH)

        for k in range(1, N_DEV):
            for a in range(n):
                copy(a, k, _peer(k)[1], me).start()
        for k in range(1, N_DEV):
            for a in range(n):
                copy(a, k, _peer(k)[1], _peer(k)[1]).wait()
        for cp in local:
            cp.wait()

    launch()
    return [land[...] for land in lands]


def _row_block(rows):
    if rows <= ROW_TILE:
        return rows
    return next(rb for rb in range(ROW_TILE, 0, -16) if rows % rb == 0)


def _adamw(landing, w, m, v, name):
    rows, cols = w.shape
    rb = _row_block(rows)

    def body(l_ref, w_ref, m_ref, v_ref, g_ref, d_ref, m2_ref, v2_ref):
        g = l_ref[0].astype(F32)
        for p in range(1, N_DEV):
            g = g + l_ref[p].astype(F32)
        g_ref[...] = g
        d_ref[...], m2_ref[...], v2_ref[...] = _adamw_step(g, w_ref[...], m_ref[...], v_ref[...])

    flat = pl.BlockSpec((rb, cols), lambda i: (i, 0))
    return pl.pallas_call(
        body, name=name, grid=(rows // rb,),
        in_specs=[pl.BlockSpec((N_DEV, rb, cols), lambda i: (0, i, 0)), flat, flat, flat],
        out_specs=[flat] * 4,
        out_shape=[jax.ShapeDtypeStruct((rows, cols), F32)] * 4,
        compiler_params=_params("parallel"),
    )(landing, w, m, v)


def _adamw_step(g, w, m, v):
    m2 = ADAM_B1 * m + (1.0 - ADAM_B1) * g
    v2 = ADAM_B2 * v + (1.0 - ADAM_B2) * (g * g)
    m_hat = m2 / (1.0 - ADAM_B1 ** ADAM_STEP)
    v_hat = v2 / (1.0 - ADAM_B2 ** ADAM_STEP)
    return -ADAM_LR * (m_hat / (jnp.sqrt(v_hat) + ADAM_EPS) + ADAM_WD * w), m2, v2


_REPLICATED = (
    ("mix_norm_g", D_MODEL), ("q_norm_g", Q_LORA), ("kv_norm_g", KV_LORA), ("conv_b", D_CONV), ("conv_ln_g", D_CONV),
    ("conv_ln_b", D_CONV), ("conv_out_g", D_CONV), ("attn_out_g", D_CONV), ("ffn_norm_g", D_MODEL),
    ("ffn_conv_b", D_UP), ("final_norm_g", D_MODEL),
)
_REPLICATED_WIDTH = sum(size for _, size in _REPLICATED) + _LANES

_WEIGHT_ORDER = (
    "meta_tokens", "mix_norm_g", "w_in", "q_norm_g", "w_uq", "kv_norm_g", "w_ukv", "conv_w", "conv_b", "conv_ln_g",
    "conv_ln_b", "conv_out_g", "attn_out_g", "w_out", "ffn_norm_g", "w_ffn_up", "ffn_conv_w", "ffn_conv_b",
    "w_ffn_down", "final_norm_g",
)


def _pack_replicated(grads, loss):
    rows = [grads[name].reshape(1, size) for name, size in _REPLICATED]
    return jnp.concatenate(rows + [jnp.broadcast_to(loss.reshape(1, 1), (1, _LANES))], axis=-1)


def _adamw_replicated(landing, weights, moments_m, moments_v):
    n = len(_REPLICATED)

    def body(*refs):
        l_ref, ins, outs = refs[0], refs[1:1 + 3 * n], refs[1 + 3 * n:]
        total = l_ref[0]
        for p in range(1, N_DEV):
            total = total + l_ref[p]
        at = 0
        for a, (_, size) in enumerate(_REPLICATED):
            g = total[:, at:at + size]
            w_ref, m_ref, v_ref = ins[3 * a:3 * a + 3]
            g_ref, d_ref, m2_ref, v2_ref = outs[4 * a:4 * a + 4]
            g_ref[...] = g
            d_ref[...], m2_ref[...], v2_ref[...] = _adamw_step(g, w_ref[...], m_ref[...], v_ref[...])
            at += size
        outs[-1][...] = total[:, at:at + _LANES]

    operands, out_shapes = [], []
    for name, size in _REPLICATED:
        operands += [weights[name].reshape(1, size), moments_m[name].reshape(1, size), moments_v[name].reshape(1, size)]
        out_shapes += [jax.ShapeDtypeStruct((1, size), F32)] * 4
    out_shapes.append(jax.ShapeDtypeStruct((1, _LANES), F32))
    outs = pl.pallas_call(body, name="adamw_replicated", out_shape=out_shapes)(landing, *operands)
    return outs[-1][0, 0], {name: outs[4 * a:4 * a + 4] for a, (name, _) in enumerate(_REPLICATED)}


def _pad_rows(a, rows):
    return jnp.pad(a, ((0, rows - a.shape[0]), (0, 0)))


def _slabs(a):
    r, c = a.shape
    return a.reshape(r, N_DEV, c // N_DEV).transpose(1, 0, 2)


def _unslab(a):
    g, r, c = a.shape
    return a.transpose(1, 0, 2).reshape(r, g * c)


def _local_step(x, target, w, n_rows, ffn_weights, send_grads):
    cos_t, sin_t = _rope_tables(n_rows)
    cos, sin = cos_t.T, sin_t.T
    meta_pad, g1, gf = w["meta_pad"], w["mix_norm_g"], w["final_norm_g"]
    gq, gkv, gb_col = w["q_norm_g"], w["kv_norm_g"], w["attn_out_g"].reshape(D_ATTN, 1)
    nb, ag, cq, ckv, kr = _fwd_in(x, meta_pad, g1, w["w_in"], n_rows)
    mix_a, u1 = _fwd_conv(ag, w["conv_w"], w["conv_b"], w["conv_ln_g"], w["conv_ln_b"], w["conv_out_g"], n_rows)
    q_t, k, v, v_t, cqn, ckvn = _fwd_qkv(cq, ckv, kr, gq, gkv, w["wq_t"], w["w_ukv"], w["wv_t"], cos, sin, cos_t, sin_t, n_rows)
    o_t, lse = _attn_fwd(q_t, k, v_t, n_rows)
    w_out, w_up, w_down = ffn_weights()
    mix_bt, h1 = _fwd_out(x, meta_pad, mix_a, o_t, gb_col, w_out, n_rows)
    n2, up0, act, da, db, dh2, loss, dgf = _fwd_ffn(
        h1, target, w["ffn_norm_g"], w_up, w["fw"], w["fb"], w_down, gf, n_rows)

    dup, dfb = _bwd_ffn_act(dh2, da, db, w_down, n_rows)
    dup0, dh1, dfw, dg2 = _bwd_ffn_up(dup, up0, h1, dh2, w["ffn_norm_g"], w_up, w["fw"], n_rows)
    grad_w_out = jnp.concatenate([_weight_grad(mix_a, dh1, "grad_w_out_conv")[0],
                                  _weight_grad(mix_bt, dh1, "grad_w_out_attn", a_transposed=True)[0]], axis=0)
    stage0 = {
        "w_ffn_up": _weight_grad(dup0, n2, "grad_w_ffn_up"),
        "w_ffn_down": _weight_grad(act, dh2, "grad_w_ffn_down").reshape(N_DEV, D_FF // N_DEV, D_MODEL),
        "w_out": grad_w_out.reshape(N_DEV, D_MODEL // N_DEV, D_MODEL),
    }
    stage0, dh1 = lax.optimization_barrier((stage0, dh1))
    send_grads(0, stage0)
    do_t, delta, du1, dgb, dga, dlg, dlb, dcb = _bwd_out(
        dh1, o_t, u1, w_out, gb_col, w["conv_ln_g"], w["conv_ln_b"], w["conv_out_g"], n_rows)
    dq_t, dk, dv = _attn_bwd(q_t, k, v, do_t, lse, delta, n_rows)
    dqraw_t, dkv, dcq, dckv, dkr, dgq, dgkv = _bwd_qkv(
        dq_t, dk, dv, cq, ckv, gq, gkv, w["wq_t"], w["w_ukv"], cos, sin, cos_t, sin_t, n_rows)
    dz, dcw = _bwd_conv(du1, ag, w["conv_w"], dcq, dckv, dkr, n_rows)
    stage1 = {
        "w_in": _weight_grad(dz, nb, "grad_w_in")[0].reshape(N_DEV, D_IN // N_DEV, D_MODEL),
        "w_uq": _weight_grad(dqraw_t.reshape(N_HEADS * QK_DIM, n_rows), cqn, "grad_w_uq", a_transposed=True)[0].reshape(
            N_HEADS, QK_DIM, Q_LORA),
        "w_ukv": _slabs(_weight_grad(ckvn, dkv, "grad_w_ukv")[0]),
        "conv_w": _slabs(dcw),
        "ffn_conv_w": dfw[:, :, :UP_SLAB],
    }
    stage1, dz = lax.optimization_barrier((stage1, dz))
    send_grads(1, stage1)
    gx, gmeta, dg1 = _bwd_in(dz, x, meta_pad, dh1, g1, w["w_in"], n_rows)

    sharded = {"meta_tokens": _slabs(gmeta[DEAD:])}
    replicated = {
        "mix_norm_g": dg1, "q_norm_g": dgq, "kv_norm_g": dgkv, "conv_b": dcb, "conv_ln_g": dlg, "conv_ln_b": dlb,
        "conv_out_g": dga, "attn_out_g": dgb, "ffn_norm_g": dg2, "ffn_conv_b": dfb, "final_norm_g": dgf,
    }
    return loss[0, 0], gx, sharded, replicated


_SHARDED = (
    ("w_in", None, BF16), ("w_uq", None, BF16), ("w_ukv", None, BF16), ("w_out", None, BF16), ("w_ffn_up", None, BF16),
    ("w_ffn_down", None, BF16), ("conv_w", 32, F32), ("ffn_conv_w", 8, F32), ("meta_tokens", None, F32),
)
GATHER_LATE_ID = 3
EXCHANGE_STAGE_IDS = (4, 5)
_LATE_WEIGHTS = ("w_out", "w_ffn_up", "w_ffn_down")
_COLUMN_SHARDS = ("w_in", "w_uq", "w_ffn_up")


def kernel(x, meta_tokens, mix_norm_g, w_in, q_norm_g, w_uq, kv_norm_g, w_ukv, conv_w, conv_b, conv_ln_g, conv_ln_b, conv_out_g, attn_out_g, w_out, ffn_norm_g, w_ffn_up, ffn_conv_w, ffn_conv_b, w_ffn_down, final_norm_g, loss_target, m_meta_tokens, m_mix_norm_g, m_w_in, m_q_norm_g, m_w_uq, m_kv_norm_g, m_w_ukv, m_conv_w, m_conv_b, m_conv_ln_g, m_conv_ln_b, m_conv_out_g, m_attn_out_g, m_w_out, m_ffn_norm_g, m_w_ffn_up, m_ffn_conv_w, m_ffn_conv_b, m_w_ffn_down, m_final_norm_g, v_meta_tokens, v_mix_norm_g, v_w_in, v_q_norm_g, v_w_uq, v_kv_norm_g, v_w_ukv, v_conv_w, v_conv_b, v_conv_ln_g, v_conv_ln_b, v_conv_out_g, v_attn_out_g, v_w_out, v_ffn_norm_g, v_w_ffn_up, v_ffn_conv_w, v_ffn_conv_b, v_w_ffn_down, v_final_norm_g):
    given = dict(locals())
    weights = {name: given[name] for name in _WEIGHT_ORDER}
    moments_m = {name: given["m_" + name] for name in _WEIGHT_ORDER}
    moments_v = {name: given["v_" + name] for name in _WEIGHT_ORDER}
    seq = x.shape[1]
    n_rows = ROW_TILE + seq

    def shard2d(name, a):
        a = a.reshape(a.shape[-2], a.shape[-1])
        return a.T if name in _COLUMN_SHARDS else a

    early = [entry for entry in _SHARDED if entry[0] not in _LATE_WEIGHTS]
    shards = []
    for name, pad_to, _ in early:
        s = shard2d(name, weights[name])
        shards.append(s if pad_to is None else _pad_rows(s, pad_to))
    gathered = dict(zip([name for name, _, _ in early], _all_gather(shards, [dt for _, _, dt in early])))
    behind = gathered["meta_tokens"][0, 0, 0] * 0.0
    late_parts = [(shard2d(name, weights[name]) + behind).astype(BF16) for name in _LATE_WEIGHTS]
    late = _sequencer_exchange(late_parts, [True] * len(late_parts), "gather_late", GATHER_LATE_ID)
    meta_full = _unslab(gathered["meta_tokens"])
    full = {
        "meta_pad": jnp.concatenate([jnp.zeros((DEAD, D_MODEL), F32), meta_full], axis=0),
        "w_in": gathered["w_in"].reshape(D_IN, D_MODEL),
        "wq_t": gathered["w_uq"],
        "w_ukv": gathered["w_ukv"],
        "wv_t": gathered["w_ukv"][:, :, QK_NOPE:].transpose(0, 2, 1),
        "conv_w": _unslab(gathered["conv_w"][:, :CONV_WIDTH]),
        "fw": jnp.pad(gathered["ffn_conv_w"][:, :FFN_CONV_WIDTH], ((0, 0), (0, 0), (0, UP_PAD - UP_SLAB))),
        "fb": jnp.pad(ffn_conv_b.reshape(N_DEV, 1, UP_SLAB), ((0, 0), (0, 0), (0, UP_PAD - UP_SLAB))),
        "final_norm_g": final_norm_g.reshape(1, D_MODEL),
    }
    for name in ("mix_norm_g", "q_norm_g", "kv_norm_g", "conv_b", "conv_ln_g", "conv_ln_b", "conv_out_g", "attn_out_g",
                 "ffn_norm_g"):
        full[name] = weights[name]

    def ffn_weights():
        w_out_all, w_up_all, w_down_all = late
        return (w_out_all.reshape(D_MODEL, D_MODEL), w_up_all, w_down_all.reshape(N_ACT_SLAB, UP_SLAB, D_MODEL))

    wire = {name: (pad_to, dt) for name, pad_to, dt in _SHARDED}
    landing = {}

    def on_the_wire(name, slabs):
        pad_to, dt = wire[name]
        slabs = slabs.astype(dt)
        return slabs if pad_to is None else jnp.pad(slabs, ((0, 0), (0, pad_to - slabs.shape[1]), (0, 0)))

    def send_grads(stage, grads):
        parts = [on_the_wire(name, slabs) for name, slabs in grads.items()]
        if landing:
            arrived = list(landing)
            parts, held = lax.optimization_barrier((parts, [landing[name] for name in arrived]))
            landing.update(zip(arrived, held))
        landed = _sequencer_exchange(parts, [False] * len(parts), f"exchange_stage{stage}", EXCHANGE_STAGE_IDS[stage])
        landing.update(zip(grads, landed))

    loss, gx, sharded, replicated = _local_step(x[0], loss_target[0], full, n_rows, ffn_weights, send_grads)

    parts = [on_the_wire(name, slabs) for name, slabs in sharded.items()] + [_pack_replicated(replicated, loss)]
    landed = _exchange(parts, [False] * len(sharded) + [True])
    landing.update(zip(sharded, landed[:-1]))

    grad, delta, new_m, new_v = {}, {}, {}, {}
    for name, pad_to, _ in _SHARDED:
        land = landing[name]
        ws, ms, vs = (shard2d(name, a[name]) for a in (weights, moments_m, moments_v))
        rows = ws.shape[0]
        if pad_to is not None:
            ws, ms, vs = _pad_rows(ws, pad_to), _pad_rows(ms, pad_to), _pad_rows(vs, pad_to)
        outs = _adamw(land, ws, ms, vs, "adamw_" + name)
        shape = weights[name].shape
        grad[name], delta[name], new_m[name], new_v[name] = (
            (o.T if name in _COLUMN_SHARDS else o[:rows]).reshape(shape) for o in outs)
    loss, updates = _adamw_replicated(landed[-1], weights, moments_m, moments_v)
    for name, outs in updates.items():
        grad[name], delta[name], new_m[name], new_v[name] = (o.reshape(weights[name].shape) for o in outs)

    return (loss, gx[None], *[grad[n] for n in _WEIGHT_ORDER], *[delta[n] for n in _WEIGHT_ORDER],
            *[new_m[n] for n in _WEIGHT_ORDER], *[new_v[n] for n in _WEIGHT_ORDER])
```

```python
import functools

import jax
import jax.numpy as jnp
from jax import lax
from jax.experimental import pallas as pl
from jax.experimental.pallas import tpu as pltpu
from jax.experimental.pallas import tpu_sc as plsc

F32 = jnp.float32
BF16 = jnp.bfloat16

N_DEV = 8
D_MODEL = 1024
CHUNK = 64
CHUNK_SHIFT = 6
N_META = 16
D_CONV = 512
CONV_WIDTH = 31
N_HEADS = 8
QK_NOPE = 64
QK_ROPE = 32
QK_DIM = QK_NOPE + QK_ROPE
V_HEAD = 64
KV_HEAD = QK_NOPE + V_HEAD
D_ATTN = N_HEADS * V_HEAD
Q_LORA = 384
KV_LORA = 256
ROPE_THETA = 10000.0
D_IN = 2 * D_CONV + Q_LORA + KV_LORA + QK_ROPE
D_FF = 2816
D_UP = 2 * D_FF
FFN_CONV_WIDTH = 3
UP_SLAB = D_UP // N_DEV
N_ACT_SLAB = D_FF // UP_SLAB
EPS = 1e-6
NEG = -1e30
_LN2 = 0.6931471805599453
QK_LOGIT_SCALE = QK_DIM ** -0.5 / _LN2
ADAM_LR = 0.001
ADAM_B1 = 0.9
ADAM_B2 = 0.999
ADAM_EPS = 1e-08
ADAM_WD = 0.01
ADAM_STEP = 10

ROW_TILE = 256
DEAD = ROW_TILE - N_META
CONV_HALO = 32
FFN_HALO = 16
VMEM_LIMIT = 56 * 1024 * 1024
_LANES = 128

MESH = pl.DeviceIdType.MESH


def _dot(a, b):
    return jnp.dot(a, b, preferred_element_type=F32)


def _dot_nt(a, b):
    return lax.dot_general(a, b, (((1,), (1,)), ((), ())), preferred_element_type=F32)


def _dot_tn(a, b):
    return lax.dot_general(a, b, (((0,), (0,)), ((), ())), preferred_element_type=F32)


def _sigmoid(x):
    return 1.0 / (1.0 + jnp.exp2(x * (-1.0 / _LN2)))


def _rms_fwd(x, g):
    r = lax.rsqrt(jnp.mean(x * x, axis=-1, keepdims=True) + EPS)
    return x * r * g


def _rms_bwd(dy, x, g):
    r = lax.rsqrt(jnp.mean(x * x, axis=-1, keepdims=True) + EPS)
    w = dy * g
    dx = r * w - x * (r * r * r) * jnp.mean(w * x, axis=-1, keepdims=True)
    return dx, jnp.sum(dy * x * r, axis=0, keepdims=True)


def _rope(x, cos, sin):
    half = QK_ROPE // 2
    x1, x2 = x[:, :half], x[:, half:]
    return jnp.concatenate([x1 * cos - x2 * sin, x2 * cos + x1 * sin], axis=-1)


def _rope_t(dy, cos, sin):
    half = QK_ROPE // 2
    d1, d2 = dy[:, :half], dy[:, half:]
    return jnp.concatenate([d1 * cos + d2 * sin, d2 * cos - d1 * sin], axis=-1)


def _row_ids(i, rows):
    return i * rows + lax.broadcasted_iota(jnp.int32, (rows, 1), 0)


def _accumulate(ref, first, value):
    @pl.when(first)
    def _():
        ref[...] = value

    @pl.when(jnp.logical_not(first))
    def _():
        ref[...] += value


def _tile_spec(shape):
    nd = len(shape)
    if nd == 2:
        return pl.BlockSpec((ROW_TILE, shape[1]), lambda i: (i, 0))
    return pl.BlockSpec((shape[0], ROW_TILE, shape[2]), lambda i: (0, i, 0))


def _whole_spec(shape):
    nd = len(shape)
    return pl.BlockSpec(tuple(shape), lambda i: (0,) * nd, pipeline_mode=pl.Buffered(1))


def _acc_spec(shape):
    nd = len(shape)
    return pl.BlockSpec(tuple(shape), lambda i: (0,) * nd)


def _real_spec(width):
    return pl.BlockSpec((ROW_TILE, width), lambda i: (jnp.maximum(i - 1, 0), 0))


def _params(*semantics):
    return pltpu.CompilerParams(dimension_semantics=semantics, vmem_limit_bytes=VMEM_LIMIT)


def _fwd_in(x, meta_pad, g1, w_in, n_rows):
    nt = n_rows // ROW_TILE

    def body(x_ref, meta_ref, g_ref, w_ref, nb_ref, ag_ref, cq_ref, ckv_ref, kr_ref):
        i = pl.program_id(0)
        h0 = jnp.where(i == 0, meta_ref[...], x_ref[...])
        nb = _rms_fwd(h0, g_ref[...]).astype(BF16)
        nb_ref[...] = nb
        z = _dot_nt(nb, w_ref[...])
        ag_ref[...] = z[:, :2 * D_CONV]
        cq_ref[...] = z[:, 2 * D_CONV:2 * D_CONV + Q_LORA]
        ckv_ref[...] = z[:, 2 * D_CONV + Q_LORA:2 * D_CONV + Q_LORA + KV_LORA]
        kr_ref[...] = z[:, 2 * D_CONV + Q_LORA + KV_LORA:]

    out_shapes = [
        jax.ShapeDtypeStruct((n_rows, D_MODEL), BF16),
        jax.ShapeDtypeStruct((n_rows, 2 * D_CONV), F32),
        jax.ShapeDtypeStruct((n_rows, Q_LORA), F32),
        jax.ShapeDtypeStruct((n_rows, KV_LORA), F32),
        jax.ShapeDtypeStruct((n_rows, QK_ROPE), F32),
    ]
    return pl.pallas_call(
        body, name="fwd_in", grid=(nt,),
        in_specs=[_real_spec(D_MODEL), _whole_spec(meta_pad.shape), _whole_spec(g1.shape), _whole_spec(w_in.shape)],
        out_specs=[_tile_spec(s.shape) for s in out_shapes],
        out_shape=out_shapes,
        compiler_params=_params("parallel"),
    )(x, meta_pad, g1, w_in)


def _conv_chain(u1, ln_g, ln_b):
    mu = jnp.mean(u1, axis=-1, keepdims=True)
    xc = u1 - mu
    rstd = lax.rsqrt(jnp.mean(xc * xc, axis=-1, keepdims=True) + EPS)
    xh = xc * rstd
    u2 = xh * ln_g + ln_b
    return xh, u2, u2 * _sigmoid(u2), rstd


def _fwd_conv(ag, conv_w, conv_b, ln_g, ln_b, out_g, n_rows):
    nt = n_rows // ROW_TILE

    def body(ag_ref, w_ref, b_ref, lg_ref, lb_ref, og_ref, mix_ref, u1_ref, ext_ref, conv_ref):
        i = pl.program_id(0)

        @pl.when(i == 0)
        def _():
            ext_ref[:, 0:CONV_HALO, :] = jnp.zeros((CONV_PLANES, CONV_HALO, _LANES), F32)

        ag_t = ag_ref[...]
        live = _row_ids(i, ROW_TILE) >= DEAD
        u0 = jnp.where(live, ag_t[:, :D_CONV] * _sigmoid(ag_t[:, D_CONV:]), 0.0)
        _to_planes(ext_ref, (), slice(CONV_HALO, None), u0)
        first = CONV_HALO - (CONV_WIDTH - 1)
        for c in range(CONV_PLANES):
            taps = w_ref[:, c * _LANES:(c + 1) * _LANES]
            for p in range(PHASES):
                acc = jnp.zeros((PHASE_ROWS, _LANES), F32)
                for k in range(CONV_WIDTH):
                    acc = acc + taps[k:k + 1, :] * ext_ref[c, _phase(first + k + p), :]
                conv_ref[c, _phase(p), :] = acc
        ext_ref[:, 0:CONV_HALO, :] = ext_ref[:, ROW_TILE:ROW_TILE + CONV_HALO, :]
        u1 = _from_planes(conv_ref, (), D_CONV) + b_ref[...]
        u1_ref[...] = u1
        _, _, u3, _ = _conv_chain(u1, lg_ref[...], lb_ref[...])
        mix_ref[...] = _rms_fwd(u3, og_ref[...]).astype(BF16)

    out_shapes = [jax.ShapeDtypeStruct((n_rows, D_CONV), BF16), jax.ShapeDtypeStruct((n_rows, D_CONV), F32)]
    small = [conv_w, conv_b, ln_g, ln_b, out_g]
    return pl.pallas_call(
        body, name="fwd_conv", grid=(nt,),
        in_specs=[_tile_spec(ag.shape)] + [_whole_spec(a.shape) for a in small],
        out_specs=[_tile_spec(s.shape) for s in out_shapes],
        out_shape=out_shapes,
        scratch_shapes=[pltpu.VMEM((CONV_PLANES, ROW_TILE + CONV_HALO, _LANES), F32),
                        pltpu.VMEM((CONV_PLANES, ROW_TILE, _LANES), F32)],
        compiler_params=_params("arbitrary"),
    )(ag, *small)


def _lane_tile(shape):
    if len(shape) == 2:
        return pl.BlockSpec((shape[0], ROW_TILE), lambda i: (0, i))
    return pl.BlockSpec((shape[0], shape[1], ROW_TILE), lambda i: (0, 0, i))


def _rope_rows(x, cos, sin):
    half = QK_ROPE // 2
    x1, x2 = x[:half], x[half:]
    return jnp.concatenate([x1 * cos - x2 * sin, x2 * cos + x1 * sin], axis=0)


def _rope_rows_t(dy, cos, sin):
    half = QK_ROPE // 2
    d1, d2 = dy[:half], dy[half:]
    return jnp.concatenate([d1 * cos + d2 * sin, d2 * cos - d1 * sin], axis=0)


def _fwd_qkv(cq, ckv, kr, gq, gkv, wq_t, w_ukv, wv_t, cos, sin, cos_t, sin_t, n_rows):
    nt = n_rows // ROW_TILE

    def body(cq_ref, ckv_ref, kr_ref, gq_ref, gkv_ref, wqt_ref, wkv_ref, wvt_ref, cos_ref, sin_ref, cost_ref, sint_ref,
             qt_ref, k_ref, v_ref, vt_ref, cqn_ref, ckvn_ref):
        cqn = _rms_fwd(cq_ref[...], gq_ref[...]).astype(BF16)
        ckvn = _rms_fwd(ckv_ref[...], gkv_ref[...]).astype(BF16)
        cqn_ref[...] = cqn
        ckvn_ref[...] = ckvn
        k_rot = _rope(kr_ref[...], cos_ref[...], sin_ref[...])
        cos_rows, sin_rows = cost_ref[...], sint_ref[...]
        for h in range(N_HEADS):
            q_raw = _dot_nt(wqt_ref[h], cqn)
            q_h = jnp.concatenate([q_raw[:QK_NOPE], _rope_rows(q_raw[QK_NOPE:], cos_rows, sin_rows)], axis=0)
            qt_ref[h] = (q_h * QK_LOGIT_SCALE).astype(BF16)
            kv = _dot(ckvn, wkv_ref[h])
            k_ref[h] = jnp.concatenate([kv[:, :QK_NOPE], k_rot], axis=-1).astype(BF16)
            v_ref[h] = kv[:, QK_NOPE:].astype(BF16)
            vt_ref[h] = _dot_nt(wvt_ref[h], ckvn).astype(BF16)

    out_shapes = [
        jax.ShapeDtypeStruct((N_HEADS, QK_DIM, n_rows), BF16),
        jax.ShapeDtypeStruct((N_HEADS, n_rows, QK_DIM), BF16),
        jax.ShapeDtypeStruct((N_HEADS, n_rows, V_HEAD), BF16),
        jax.ShapeDtypeStruct((N_HEADS, V_HEAD, n_rows), BF16),
        jax.ShapeDtypeStruct((n_rows, Q_LORA), BF16),
        jax.ShapeDtypeStruct((n_rows, KV_LORA), BF16),
    ]
    tiles = [cq, ckv, kr]
    whole = [gq, gkv, wq_t, w_ukv, wv_t]
    out_specs = [_lane_tile(out_shapes[0].shape), _tile_spec(out_shapes[1].shape), _tile_spec(out_shapes[2].shape),
                 _lane_tile(out_shapes[3].shape), _tile_spec(out_shapes[4].shape), _tile_spec(out_shapes[5].shape)]
    return pl.pallas_call(
        body, name="fwd_qkv", grid=(nt,),
        in_specs=[_tile_spec(a.shape) for a in tiles] + [_whole_spec(a.shape) for a in whole]
        + [_tile_spec(cos.shape), _tile_spec(sin.shape), _lane_tile(cos_t.shape), _lane_tile(sin_t.shape)],
        out_specs=out_specs,
        out_shape=out_shapes,
        compiler_params=_params("parallel"),
    )(*tiles, *whole, cos, sin, cos_t, sin_t)


def _chunk_of(rows):
    return jnp.where(rows >= ROW_TILE, lax.shift_right_arithmetic(rows - ROW_TILE, CHUNK_SHIFT) + 1, 0)


def _visible(i, j):
    k_rows = j * ROW_TILE + lax.broadcasted_iota(jnp.int32, (ROW_TILE, 1), 0)
    q_rows = i * ROW_TILE + lax.broadcasted_iota(jnp.int32, (1, ROW_TILE), 1)
    return jnp.logical_and(_chunk_of(q_rows) >= _chunk_of(k_rows), k_rows >= DEAD)


def _attn_fwd(q_t, k, v_t, n_rows):
    nt = n_rows // ROW_TILE

    def body(qt_ref, k_ref, vt_ref, ot_ref, lse_ref):
        i = pl.program_id(0)
        q_ts = [qt_ref[h] for h in range(N_HEADS)]

        def key_rows(j):
            return pl.ds(pl.multiple_of(j * ROW_TILE, ROW_TILE), ROW_TILE)

        def make_step(masked, tiles, first=0):
            def step(t, carry):
                js = [first + tiles * t + u for u in range(tiles)]
                scores = [[_dot(k_ref[h, key_rows(j), :], q_ts[h]) for h in range(N_HEADS)] for j in js]
                for j, tile_scores in zip(js, scores):
                    visible = _visible(i, j) if masked else None
                    probs, state = [], []
                    for h in range(N_HEADS):
                        m, l, _ = carry[h]
                        s = jnp.where(visible, tile_scores[h], NEG) if masked else tile_scores[h]
                        m_new = jnp.maximum(m, jnp.max(s, axis=0, keepdims=True))
                        alpha = jnp.exp2(m - m_new)
                        p = jnp.exp2(s - m_new)
                        probs.append(p.astype(BF16))
                        state.append((m_new, alpha * l + jnp.sum(p, axis=0, keepdims=True), alpha))
                    outs = [_dot(vt_ref[h, :, key_rows(j)], probs[h]) for h in range(N_HEADS)]
                    carry = tuple((state[h][0], state[h][1], state[h][2] * carry[h][2] + outs[h]) for h in range(N_HEADS))
                return carry
            return step

        init = tuple((jnp.full((1, ROW_TILE), NEG, F32), jnp.zeros((1, ROW_TILE), F32),
                      jnp.zeros((V_HEAD, ROW_TILE), F32)) for _ in range(N_HEADS))
        between = jnp.maximum(i - 1, 0)
        quads = lax.shift_right_logical(between, 2)
        pairs = jnp.bitwise_and(lax.shift_right_logical(between, 1), 1)
        carry = make_step(True, 1)(0, init)
        carry = lax.fori_loop(0, quads, make_step(False, 4, first=1), carry)
        carry = lax.fori_loop(0, pairs, make_step(False, 2, first=1 + 4 * quads), carry)
        carry = lax.fori_loop(1 + 4 * quads + 2 * pairs, i, make_step(False, 1), carry)
        carry = lax.fori_loop(jnp.maximum(i, 1), i + 1, make_step(True, 1), carry)
        for h in range(N_HEADS):
            m, l, acc = carry[h]
            ot_ref[h] = acc / l
            lse_ref[h] = m + jnp.log2(l)

    out_shapes = [jax.ShapeDtypeStruct((N_HEADS, V_HEAD, n_rows), F32), jax.ShapeDtypeStruct((N_HEADS, 1, n_rows), F32)]
    return pl.pallas_call(
        body, name="attn_fwd", grid=(nt,),
        in_specs=[_lane_tile(q_t.shape), _whole_spec(k.shape), _whole_spec(v_t.shape)],
        out_specs=[_lane_tile(s.shape) for s in out_shapes],
        out_shape=out_shapes,
        compiler_params=_params("parallel"),
    )(q_t, k, v_t)


def _heads_to_rows(ref):
    return jnp.concatenate([ref[h] for h in range(N_HEADS)], axis=0)


def _rms_cols(x, g_col):
    r = lax.rsqrt(jnp.mean(x * x, axis=0, keepdims=True) + EPS)
    return x * r * g_col


def _fwd_out(x, meta_pad, mix_a, o_t, gb_col, w_out, n_rows):
    nt = n_rows // ROW_TILE

    def body(x_ref, meta_ref, mixa_ref, ot_ref, gb_ref, w_ref, mixbt_ref, h1_ref):
        i = pl.program_id(0)
        h0 = jnp.where(i == 0, meta_ref[...], x_ref[...])
        mix_bt = _rms_cols(_heads_to_rows(ot_ref), gb_ref[...]).astype(BF16)
        mixbt_ref[...] = mix_bt
        h1_ref[...] = h0 + _dot(mixa_ref[...], w_ref[:D_CONV, :]) + _dot_tn(mix_bt, w_ref[D_CONV:, :])

    out_shapes = [jax.ShapeDtypeStruct((D_ATTN, n_rows), BF16), jax.ShapeDtypeStruct((n_rows, D_MODEL), F32)]
    return pl.pallas_call(
        body, name="fwd_out", grid=(nt,),
        in_specs=[_real_spec(D_MODEL), _whole_spec(meta_pad.shape), _tile_spec(mix_a.shape), _lane_tile(o_t.shape),
                  _whole_spec(gb_col.shape), _whole_spec(w_out.shape)],
        out_specs=[_lane_tile(out_shapes[0].shape), _tile_spec(out_shapes[1].shape)],
        out_shape=out_shapes,
        compiler_params=_params("parallel"),
    )(x, meta_pad, mix_a, o_t, gb_col, w_out)


PHASES = 8
PHASE_ROWS = ROW_TILE // PHASES
UP_PLANES = -(-UP_SLAB // _LANES)
UP_PAD = UP_PLANES * _LANES
CONV_PLANES = D_CONV // _LANES


def _phase(start):
    return pl.ds(start, PHASE_ROWS, stride=PHASES)


def _to_planes(ref, lead, rows, value):
    width = value.shape[-1]
    for c in range(-(-width // _LANES)):
        part = value[:, c * _LANES:min((c + 1) * _LANES, width)]
        if part.shape[-1] < _LANES:
            part = jnp.concatenate([part, jnp.zeros((part.shape[0], _LANES - part.shape[-1]), part.dtype)], axis=-1)
        ref[(*lead, c, rows, slice(None))] = part


def _from_planes(ref, lead, width):
    planes = [ref[(*lead, c)] for c in range(-(-width // _LANES))]
    last = width - (len(planes) - 1) * _LANES
    return jnp.concatenate(planes[:-1] + [planes[-1][:, :last]], axis=-1)


def _fwd_ffn(h1, target, g2, w_up, fw, fb, w_down, gf, n_rows):
    nt = n_rows // ROW_TILE

    def body(h1_ref, t_ref, g2_ref, wup_ref, fw_ref, fb_ref, wdn_ref, gf_ref,
             n2_ref, up0_ref, act_ref, da_ref, db_ref, dh2_ref, loss_ref, dgf_ref, ext_ref):
        i = pl.program_id(0)

        @pl.when(i == 0)
        def _():
            ext_ref[:, 0:FFN_HALO, :] = jnp.zeros((N_DEV, FFN_HALO, UP_SLAB), F32)

        h1_t = h1_ref[...]
        live = _row_ids(i, ROW_TILE) >= DEAD
        n2 = jnp.where(live, _rms_fwd(h1_t, g2_ref[...]), 0.0).astype(BF16)
        n2_ref[...] = n2
        for s in range(N_DEV):
            up0 = _dot_nt(n2, wup_ref[s])
            up0_ref[s] = up0.astype(BF16)
            ext_ref[s, FFN_HALO:, :] = up0
        first = FFN_HALO - (FFN_CONV_WIDTH - 1)

        def conv(s):
            block = ext_ref[s]
            acc = fb_ref[s, :, :UP_SLAB] + fw_ref[s, FFN_CONV_WIDTH - 1:FFN_CONV_WIDTH, :UP_SLAB] * block[FFN_HALO:]
            for back in range(1, FFN_CONV_WIDTH):
                k = FFN_CONV_WIDTH - 1 - back
                acc = acc + fw_ref[s, k:k + 1, :UP_SLAB] * pltpu.roll(block, back, 0)[FFN_HALO:]
            return acc

        h2 = h1_t
        for s in range(N_ACT_SLAB):
            gate = conv(s)
            val = conv(s + N_ACT_SLAB)
            sg = _sigmoid(gate)
            silu = gate * sg
            act = (silu * val).astype(BF16)
            act_ref[s] = act
            da_ref[s] = (val * sg * (1.0 + gate * (1.0 - sg))).astype(BF16)
            db_ref[s] = silu.astype(BF16)
            h2 = h2 + _dot(act, wdn_ref[s])
        ext_ref[:, 0:FFN_HALO, :] = ext_ref[:, ROW_TILE:ROW_TILE + FFN_HALO, :]

        gf_t = gf_ref[...]
        y = _rms_fwd(h2, gf_t)
        diff = jnp.where(i >= 1, y - t_ref[...], 0.0)
        tile_loss = 0.5 * jnp.sum(jnp.sum(diff * diff, axis=-1, keepdims=True), axis=0, keepdims=True) / D_MODEL
        dh2, dgf = _rms_bwd(diff / D_MODEL, h2, gf_t)
        dh2_ref[...] = dh2
        _accumulate(loss_ref, i == 0, jnp.broadcast_to(tile_loss, loss_ref.shape))
        _accumulate(dgf_ref, i == 0, dgf)

    act_like = jax.ShapeDtypeStruct((N_ACT_SLAB, n_rows, UP_SLAB), BF16)
    out_shapes = [
        jax.ShapeDtypeStruct((n_rows, D_MODEL), BF16),
        jax.ShapeDtypeStruct((N_DEV, n_rows, UP_SLAB), BF16),
        act_like, act_like, act_like,
        jax.ShapeDtypeStruct((n_rows, D_MODEL), F32),
        jax.ShapeDtypeStruct((8, 128), F32),
        jax.ShapeDtypeStruct((1, D_MODEL), F32),
    ]
    whole = [g2, w_up, fw, fb, w_down, gf]
    return pl.pallas_call(
        body, name="fwd_ffn", grid=(nt,),
        in_specs=[_tile_spec(h1.shape), _real_spec(D_MODEL)] + [_whole_spec(a.shape) for a in whole],
        out_specs=[_tile_spec(s.shape) for s in out_shapes[:6]] + [_acc_spec(s.shape) for s in out_shapes[6:]],
        out_shape=out_shapes,
        scratch_shapes=[pltpu.VMEM((N_DEV, ROW_TILE + FFN_HALO, UP_SLAB), F32)],
        compiler_params=_params("arbitrary"),
    )(h1, target, *whole)


def _rope_tables(n_rows):
    pos = jnp.maximum(jnp.arange(n_rows, dtype=jnp.int32) - DEAD, 0)
    inv_freq = 1.0 / (ROPE_THETA ** (jnp.arange(0, QK_ROPE, 2, dtype=F32) / QK_ROPE))
    ang_t = inv_freq[:, None] * pos.astype(F32)[None, :]
    return jnp.cos(ang_t), jnp.sin(ang_t)


def _halo_after(shape, halo, n_rows):
    last = n_rows // halo - 1
    step = ROW_TILE // halo
    if len(shape) == 2:
        return pl.BlockSpec((halo, shape[1]), lambda i: (jnp.minimum((i + 1) * step, last), 0))
    return pl.BlockSpec((shape[0], halo, shape[2]), lambda i: (0, jnp.minimum((i + 1) * step, last), 0))


def _halo_before(shape, halo):
    step = ROW_TILE // halo
    if len(shape) == 2:
        return pl.BlockSpec((halo, shape[1]), lambda i: (jnp.maximum(i * step - 1, 0), 0))
    return pl.BlockSpec((shape[0], halo, shape[2]), lambda i: (0, jnp.maximum(i * step - 1, 0), 0))


def _bwd_ffn_act(dh2, da, db, w_down, n_rows):
    nt = n_rows // ROW_TILE

    def body(dh2_ref, da_ref, db_ref, wdn_ref, dup_ref, dfb_ref):
        i = pl.program_id(0)

        @pl.when(i == 0)
        def _():
            dfb_ref[...] = jnp.zeros_like(dfb_ref)

        dh2_b = dh2_ref[...].astype(BF16)
        for s in range(N_ACT_SLAB):
            d_act = _dot_nt(dh2_b, wdn_ref[s])
            d_gate = d_act * da_ref[s].astype(F32)
            d_val = d_act * db_ref[s].astype(F32)
            dup_ref[s] = d_gate.astype(BF16)
            dup_ref[s + N_ACT_SLAB] = d_val.astype(BF16)
            dfb_ref[s] += jnp.sum(d_gate, axis=0, keepdims=True)
            dfb_ref[s + N_ACT_SLAB] += jnp.sum(d_val, axis=0, keepdims=True)

    out_shapes = [jax.ShapeDtypeStruct((N_DEV, n_rows, UP_SLAB), BF16), jax.ShapeDtypeStruct((N_DEV, 1, UP_SLAB), F32)]
    return pl.pallas_call(
        body, name="bwd_ffn_act", grid=(nt,),
        in_specs=[_tile_spec(dh2.shape), _tile_spec(da.shape), _tile_spec(db.shape), _whole_spec(w_down.shape)],
        out_specs=[_tile_spec(out_shapes[0].shape), _acc_spec(out_shapes[1].shape)],
        out_shape=out_shapes,
        compiler_params=_params("arbitrary"),
    )(dh2, da, db, w_down)


def _bwd_ffn_up(dup, up0, h1, dh2, g2, w_up, fw, n_rows):
    nt = n_rows // ROW_TILE
    last_tap = FFN_CONV_WIDTH - 1
    ext_rows = ROW_TILE + FFN_HALO

    def body(dup_ref, dnext_ref, up0_ref, h1_ref, dh2_ref, g2_ref, wup_ref, fw_ref,
             dup0_ref, dh1_ref, dfw_ref, dg2_ref):
        i = pl.program_id(0)

        @pl.when(i == 0)
        def _():
            dfw_ref[...] = jnp.zeros_like(dfw_ref)

        live = _row_ids(i, ROW_TILE) >= DEAD
        dn2 = jnp.zeros((ROW_TILE, D_MODEL), F32)
        for s in range(N_DEV):
            d = dup_ref[s].astype(F32)
            block = jnp.concatenate([d, jnp.where(i == nt - 1, 0.0, dnext_ref[s].astype(F32))], axis=0)
            u = up0_ref[s].astype(F32)
            dup0 = fw_ref[s, last_tap:last_tap + 1, :UP_SLAB] * d
            dfw_ref[s, last_tap:last_tap + 1, :UP_SLAB] += jnp.sum(d * u, axis=0, keepdims=True)
            for ahead in range(1, FFN_CONV_WIDTH):
                k = last_tap - ahead
                shifted = pltpu.roll(block, ext_rows - ahead, 0)[:ROW_TILE]
                dup0 = dup0 + fw_ref[s, k:k + 1, :UP_SLAB] * shifted
                dfw_ref[s, k:k + 1, :UP_SLAB] += jnp.sum(shifted * u, axis=0, keepdims=True)
            dup0_b = jnp.where(live, dup0, 0.0).astype(BF16)
            dup0_ref[s] = dup0_b
            dn2 = dn2 + _dot(dup0_b, wup_ref[s])
        dx, dg2 = _rms_bwd(dn2, h1_ref[...], g2_ref[...])
        dh1_ref[...] = dh2_ref[...] + dx
        _accumulate(dg2_ref, i == 0, dg2)

    out_shapes = [
        jax.ShapeDtypeStruct((N_DEV, n_rows, UP_SLAB), BF16),
        jax.ShapeDtypeStruct((n_rows, D_MODEL), F32),
        jax.ShapeDtypeStruct((N_DEV, FFN_CONV_WIDTH, UP_PAD), F32),
        jax.ShapeDtypeStruct((1, D_MODEL), F32),
    ]
    return pl.pallas_call(
        body, name="bwd_ffn_up", grid=(nt,),
        in_specs=[_tile_spec(dup.shape), _halo_after(dup.shape, FFN_HALO, n_rows), _tile_spec(up0.shape),
                  _tile_spec(h1.shape), _tile_spec(dh2.shape),
                  _whole_spec(g2.shape), _whole_spec(w_up.shape), _whole_spec(fw.shape)],
        out_specs=[_tile_spec(s.shape) for s in out_shapes[:2]] + [_acc_spec(s.shape) for s in out_shapes[2:]],
        out_shape=out_shapes,
        compiler_params=_params("arbitrary"),
    )(dup, dup, up0, h1, dh2, g2, w_up, fw)


def _bwd_out(dh1, o_t, u1, w_out, gb_col, ln_g, ln_b, ga, n_rows):
    nt = n_rows // ROW_TILE

    def body(dh1_ref, ot_ref, u1_ref, w_ref, gb_ref, lg_ref, lb_ref, ga_ref,
             dot_ref, delta_ref, du1_ref, dgb_ref, dga_ref, dlg_ref, dlb_ref, dcb_ref):
        i = pl.program_id(0)
        dh1_b = dh1_ref[...].astype(BF16)
        o_t = _heads_to_rows(ot_ref)
        gb = gb_ref[...]
        r = lax.rsqrt(jnp.mean(o_t * o_t, axis=0, keepdims=True) + EPS)
        dmix_bt = _dot_nt(w_ref[D_CONV:, :], dh1_b)
        wgt = dmix_bt * gb
        do_t = r * wgt - o_t * (r * r * r) * jnp.mean(wgt * o_t, axis=0, keepdims=True)
        dgb = jnp.sum(dmix_bt * o_t * r, axis=1, keepdims=True)
        for h in range(N_HEADS):
            do_h = do_t[h * V_HEAD:(h + 1) * V_HEAD]
            dot_ref[h] = do_h.astype(BF16)
            delta_ref[h] = jnp.sum(do_h * ot_ref[h], axis=0, keepdims=True)
        lg = lg_ref[...]
        xh, u2, u3, rstd = _conv_chain(u1_ref[...], lg, lb_ref[...])
        du3, dga = _rms_bwd(_dot_nt(dh1_b, w_ref[:D_CONV, :]), u3, ga_ref[...])
        sg = _sigmoid(u2)
        du2 = du3 * sg * (1.0 + u2 * (1.0 - sg))
        dxh = du2 * lg
        du1 = rstd * (dxh - jnp.mean(dxh, axis=-1, keepdims=True) - xh * jnp.mean(dxh * xh, axis=-1, keepdims=True))
        du1_ref[...] = du1
        first = i == 0
        _accumulate(dgb_ref, first, dgb)
        _accumulate(dga_ref, first, dga)
        _accumulate(dlg_ref, first, jnp.sum(du2 * xh, axis=0, keepdims=True))
        _accumulate(dlb_ref, first, jnp.sum(du2, axis=0, keepdims=True))
        _accumulate(dcb_ref, first, jnp.sum(du1, axis=0, keepdims=True))

    out_shapes = [
        jax.ShapeDtypeStruct((N_HEADS, V_HEAD, n_rows), BF16),
        jax.ShapeDtypeStruct((N_HEADS, 1, n_rows), F32),
        jax.ShapeDtypeStruct((n_rows, D_CONV), F32),
        jax.ShapeDtypeStruct((D_ATTN, 1), F32),
    ] + [jax.ShapeDtypeStruct((1, D_CONV), F32)] * 4
    whole = [w_out, gb_col, ln_g, ln_b, ga]
    return pl.pallas_call(
        body, name="bwd_out", grid=(nt,),
        in_specs=[_tile_spec(dh1.shape), _lane_tile(o_t.shape), _tile_spec(u1.shape)] + [_whole_spec(a.shape) for a in whole],
        out_specs=[_lane_tile(out_shapes[0].shape), _lane_tile(out_shapes[1].shape), _tile_spec(out_shapes[2].shape)]
        + [_acc_spec(s.shape) for s in out_shapes[3:]],
        out_shape=out_shapes,
        compiler_params=_params("arbitrary"),
    )(dh1, o_t, u1, *whole)


ATTN_BWD_HEADS = 8


def _attn_bwd(q_t, k, v, do_t, lse, delta, n_rows):
    nt = n_rows // ROW_TILE
    hp = ATTN_BWD_HEADS

    def body(k_ref, v_ref, qt_ref, dot_ref, lse_ref, delta_ref, dqt_ref, dk_ref, dv_ref):
        j = pl.program_id(1)

        @pl.when(j == 0)
        def _():
            dqt_ref[...] = jnp.zeros_like(dqt_ref)

        k_ts = [k_ref[h] for h in range(hp)]
        v_ts = [v_ref[h] for h in range(hp)]

        def make_step(masked, tiles, first=0):
            def step(t, carry):
                tiles_of_step = []
                for u in range(tiles):
                    i = first + tiles * t + u
                    cols = pl.ds(pl.multiple_of(i * ROW_TILE, ROW_TILE), ROW_TILE)
                    q_is = [qt_ref[h, :, cols] for h in range(hp)]
                    do_is = [dot_ref[h, :, cols] for h in range(hp)]
                    scores = [_dot(k_ts[h], q_is[h]) for h in range(hp)]
                    dps = [_dot(v_ts[h], do_is[h]) for h in range(hp)]
                    tiles_of_step.append((i, cols, q_is, do_is, scores, dps))
                for i, cols, q_is, do_is, scores, dps in tiles_of_step:
                    visible = _visible(i, j) if masked else None
                    probs, dss = [], []
                    for h in range(hp):
                        s = jnp.where(visible, scores[h], NEG) if masked else scores[h]
                        p = jnp.exp2(s - lse_ref[h, :, cols])
                        probs.append(p.astype(BF16))
                        dss.append((p * (dps[h] - delta_ref[h, :, cols])).astype(BF16))
                    out = []
                    for h in range(hp):
                        dk, dv = carry[h]
                        dv = dv + _dot_nt(probs[h], do_is[h])
                        dk = dk + _dot_nt(dss[h], q_is[h])
                        dqt_ref[h, :, cols] += _dot_tn(k_ts[h], dss[h])
                        out.append((dk, dv))
                    carry = tuple(out)
                return carry
            return step

        init = tuple((jnp.zeros((ROW_TILE, QK_DIM), F32), jnp.zeros((ROW_TILE, V_HEAD), F32)) for _ in range(hp))
        carry = make_step(True, 1)(j, init)
        carry = lax.fori_loop(jnp.where(j == 0, j + 1, nt), nt, make_step(True, 1), carry)
        unmasked = jnp.where(j == 0, 0, nt - 1 - j)
        quads = lax.shift_right_logical(unmasked, 2)
        pairs = jnp.bitwise_and(lax.shift_right_logical(unmasked, 1), 1)
        carry = lax.fori_loop(0, quads, make_step(False, 4, first=j + 1), carry)
        carry = lax.fori_loop(0, pairs, make_step(False, 2, first=j + 1 + 4 * quads), carry)
        carry = lax.fori_loop(jnp.where(j == 0, nt, j + 1 + 4 * quads + 2 * pairs), nt, make_step(False, 1), carry)
        for h in range(hp):
            dk_ref[h] = carry[h][0] * _LN2
            dv_ref[h] = carry[h][1]

    key_tile = lambda w: pl.BlockSpec((hp, ROW_TILE, w), lambda g, j: (g, j, 0))
    all_cols = lambda w: pl.BlockSpec((hp, w, n_rows), lambda g, j: (g, 0, 0))
    resident = lambda w: pl.BlockSpec((hp, w, n_rows), lambda g, j: (g, 0, 0), pipeline_mode=pl.Buffered(1))
    out_shapes = [
        jax.ShapeDtypeStruct((N_HEADS, QK_DIM, n_rows), F32),
        jax.ShapeDtypeStruct((N_HEADS, n_rows, QK_DIM), F32),
        jax.ShapeDtypeStruct((N_HEADS, n_rows, V_HEAD), F32),
    ]
    return pl.pallas_call(
        body, name="attn_bwd", grid=(N_HEADS // hp, nt),
        in_specs=[key_tile(QK_DIM), key_tile(V_HEAD), resident(QK_DIM), resident(V_HEAD), resident(1), resident(1)],
        out_specs=[all_cols(QK_DIM), key_tile(QK_DIM), key_tile(V_HEAD)],
        out_shape=out_shapes,
        compiler_params=_params("parallel", "arbitrary"),
    )(k, v, q_t, do_t, lse, delta)


def _bwd_qkv(dq_t, dk, dv, cq, ckv, gq, gkv, wq_t, w_ukv, cos, sin, cos_t, sin_t, n_rows):
    nt = n_rows // ROW_TILE

    def body(dqt_ref, dk_ref, dv_ref, cq_ref, ckv_ref, gq_ref, gkv_ref, wqt_ref, wkv_ref, cos_ref, sin_ref,
             cost_ref, sint_ref, dqraw_ref, dkv_ref, dcq_ref, dckv_ref, dkr_ref, dgq_ref, dgkv_ref):
        i = pl.program_id(0)
        cos_rows, sin_rows = cost_ref[...], sint_ref[...]
        dcqn = jnp.zeros((ROW_TILE, Q_LORA), F32)
        dckvn = jnp.zeros((ROW_TILE, KV_LORA), F32)
        dk_rot = jnp.zeros((ROW_TILE, QK_ROPE), F32)
        for h in range(N_HEADS):
            dq_h, dk_h = dqt_ref[h] * QK_DIM ** -0.5, dk_ref[h]
            dq_raw = jnp.concatenate(
                [dq_h[:QK_NOPE], _rope_rows_t(dq_h[QK_NOPE:], cos_rows, sin_rows)], axis=0).astype(BF16)
            dqraw_ref[h] = dq_raw
            dcqn = dcqn + _dot_tn(dq_raw, wqt_ref[h])
            dkv = jnp.concatenate([dk_h[:, :QK_NOPE], dv_ref[h]], axis=-1).astype(BF16)
            dkv_ref[:, h * KV_HEAD:(h + 1) * KV_HEAD] = dkv
            dckvn = dckvn + _dot_nt(dkv, wkv_ref[h])
            dk_rot = dk_rot + dk_h[:, QK_NOPE:]
        dkr_ref[...] = _rope_t(dk_rot, cos_ref[...], sin_ref[...]).astype(BF16)
        dcq, dgq = _rms_bwd(dcqn, cq_ref[...], gq_ref[...])
        dckv, dgkv = _rms_bwd(dckvn, ckv_ref[...], gkv_ref[...])
        dcq_ref[...] = dcq.astype(BF16)
        dckv_ref[...] = dckv.astype(BF16)
        _accumulate(dgq_ref, i == 0, dgq)
        _accumulate(dgkv_ref, i == 0, dgkv)

    out_shapes = [
        jax.ShapeDtypeStruct((N_HEADS, QK_DIM, n_rows), BF16),
        jax.ShapeDtypeStruct((n_rows, N_HEADS * KV_HEAD), BF16),
        jax.ShapeDtypeStruct((n_rows, Q_LORA), BF16),
        jax.ShapeDtypeStruct((n_rows, KV_LORA), BF16),
        jax.ShapeDtypeStruct((n_rows, QK_ROPE), BF16),
        jax.ShapeDtypeStruct((1, Q_LORA), F32),
        jax.ShapeDtypeStruct((1, KV_LORA), F32),
    ]
    tiles = [dk, dv, cq, ckv]
    whole = [gq, gkv, wq_t, w_ukv]
    return pl.pallas_call(
        body, name="bwd_qkv", grid=(nt,),
        in_specs=[_lane_tile(dq_t.shape)] + [_tile_spec(a.shape) for a in tiles] + [_whole_spec(a.shape) for a in whole]
        + [_tile_spec(cos.shape), _tile_spec(sin.shape), _lane_tile(cos_t.shape), _lane_tile(sin_t.shape)],
        out_specs=[_lane_tile(out_shapes[0].shape)] + [_tile_spec(s.shape) for s in out_shapes[1:5]]
        + [_acc_spec(s.shape) for s in out_shapes[5:]],
        out_shape=out_shapes,
        compiler_params=_params("arbitrary"),
    )(dq_t, *tiles, *whole, cos, sin, cos_t, sin_t)


def _bwd_conv(du1, ag, conv_w, dcq, dckv, dkr, n_rows):
    nt = n_rows // ROW_TILE

    last_tap = CONV_WIDTH - 1

    def body(du1_ref, dnext_ref, ag_ref, w_ref, dcq_ref, dckv_ref, dkr_ref, dz_ref, dw_ref,
             dext_ref, uext_ref, conv_ref, sums_ref):
        i = pl.program_id(0)

        @pl.when(i == 0)
        def _():
            sums_ref[...] = jnp.zeros_like(sums_ref)

        _to_planes(dext_ref, (), slice(0, ROW_TILE), du1_ref[...])
        _to_planes(dext_ref, (), slice(ROW_TILE, None), jnp.where(i == nt - 1, 0.0, dnext_ref[...]))
        ag_t = ag_ref[...]
        live = _row_ids(i, ROW_TILE) >= DEAD
        sg = _sigmoid(ag_t[:, D_CONV:])
        _to_planes(uext_ref, (), slice(None), jnp.where(live, ag_t[:, :D_CONV] * sg, 0.0))
        for c in range(CONV_PLANES):
            taps = w_ref[:, c * _LANES:(c + 1) * _LANES]
            for p in range(PHASES):
                u = uext_ref[c, _phase(p), :]
                acc = jnp.zeros((PHASE_ROWS, _LANES), F32)
                for k in range(CONV_WIDTH):
                    shifted = dext_ref[c, _phase(p + last_tap - k), :]
                    acc = acc + taps[k:k + 1, :] * shifted
                    sums_ref[c, k] += shifted * u
                conv_ref[c, _phase(p), :] = acc
        du0 = jnp.where(live, _from_planes(conv_ref, (), D_CONV), 0.0)
        da = du0 * sg
        dgate = du0 * ag_t[:, :D_CONV] * sg * (1.0 - sg)
        dz_ref[...] = jnp.concatenate(
            [da.astype(BF16), dgate.astype(BF16), dcq_ref[...], dckv_ref[...], dkr_ref[...]], axis=-1)

        @pl.when(i == nt - 1)
        def _():
            for c in range(CONV_PLANES):
                for k in range(CONV_WIDTH):
                    dw_ref[k:k + 1, c * _LANES:(c + 1) * _LANES] = jnp.sum(sums_ref[c, k], axis=0, keepdims=True)

    out_shapes = [jax.ShapeDtypeStruct((n_rows, D_IN), BF16), jax.ShapeDtypeStruct((CONV_WIDTH, D_CONV), F32)]
    return pl.pallas_call(
        body, name="bwd_conv", grid=(nt,),
        in_specs=[_tile_spec(du1.shape), _halo_after(du1.shape, CONV_HALO, n_rows), _tile_spec(ag.shape),
                  _whole_spec(conv_w.shape), _tile_spec(dcq.shape), _tile_spec(dckv.shape), _tile_spec(dkr.shape)],
        out_specs=[_tile_spec(out_shapes[0].shape), _acc_spec(out_shapes[1].shape)],
        out_shape=out_shapes,
        scratch_shapes=[pltpu.VMEM((CONV_PLANES, ROW_TILE + CONV_HALO, _LANES), F32),
                        pltpu.VMEM((CONV_PLANES, ROW_TILE, _LANES), F32), pltpu.VMEM((CONV_PLANES, ROW_TILE, _LANES), F32),
                        pltpu.VMEM((CONV_PLANES, CONV_WIDTH, PHASE_ROWS, _LANES), F32)],
        compiler_params=_params("arbitrary"),
    )(du1, du1, ag, conv_w, dcq, dckv, dkr)


def _bwd_in(dz, x, meta_pad, dh1, g1, w_in, n_rows):
    nt = n_rows // ROW_TILE

    def body(dz_ref, x_ref, meta_ref, dh1_ref, g_ref, w_ref, gx_ref, gmeta_ref, dg1_ref):
        i = pl.program_id(0)
        h0 = jnp.where(i == 0, meta_ref[...], x_ref[...])
        dx, dg1 = _rms_bwd(_dot(dz_ref[...], w_ref[...]), h0, g_ref[...])
        dh0 = dh1_ref[...] + dx
        gx_ref[...] = dh0

        @pl.when(i == 0)
        def _():
            gmeta_ref[...] = dh0

        _accumulate(dg1_ref, i == 0, dg1)

    out_shapes = [
        jax.ShapeDtypeStruct((n_rows - ROW_TILE, D_MODEL), F32),
        jax.ShapeDtypeStruct((ROW_TILE, D_MODEL), F32),
        jax.ShapeDtypeStruct((1, D_MODEL), F32),
    ]
    return pl.pallas_call(
        body, name="bwd_in", grid=(nt,),
        in_specs=[_tile_spec(dz.shape), _real_spec(D_MODEL), _whole_spec(meta_pad.shape), _tile_spec(dh1.shape),
                  _whole_spec(g1.shape), _whole_spec(w_in.shape)],
        out_specs=[_real_spec(D_MODEL), _acc_spec(out_shapes[1].shape), _acc_spec(out_shapes[2].shape)],
        out_shape=out_shapes,
        compiler_params=_params("arbitrary"),
    )(dz, x, meta_pad, dh1, g1, w_in)


def _contraction_tile(n_rows):
    return next(t for t in range(n_rows // 2 // _LANES * _LANES, 0, -_LANES) if n_rows % t == 0)


def _weight_grad(a, b, name, a_transposed=False):
    groups = max(a.shape[0] if a.ndim == 3 else 1, b.shape[0] if b.ndim == 3 else 1)
    n_rows, n = b.shape[-2], b.shape[-1]
    m = a.shape[-2] if a_transposed else a.shape[-1]
    kt = _contraction_tile(n_rows)
    steps = n_rows // kt

    def body(a_ref, b_ref, out_ref, acc_ref):
        i = pl.program_id(1)
        a_t, b_t = a_ref[...].astype(BF16), b_ref[...].astype(BF16)
        part = _dot(a_t, b_t) if a_transposed else _dot_tn(a_t, b_t)
        _accumulate(acc_ref, i == 0, part)

        @pl.when(i == steps - 1)
        def _():
            out_ref[...] = acc_ref[...].astype(out_ref.dtype)

    def spec(arr, rows_last):
        block = (arr.shape[-2], kt) if rows_last else (kt, arr.shape[-1])
        at = (lambda i: (0, i)) if rows_last else (lambda i: (i, 0))
        if arr.ndim == 3:
            return pl.BlockSpec((None,) + block, lambda g, i: (g,) + at(i))
        return pl.BlockSpec(block, lambda g, i: at(i))

    return pl.pallas_call(
        body, name=name, grid=(groups, steps),
        in_specs=[spec(a, a_transposed), spec(b, False)],
        out_specs=pl.BlockSpec((None, m, n), lambda g, i: (g, 0, 0)),
        out_shape=jax.ShapeDtypeStruct((groups, m, n), BF16),
        scratch_shapes=[pltpu.VMEM((m, n), F32)],
        compiler_params=_params("parallel", "arbitrary"),
    )(a, b)


def _my_index():
    return 4 * lax.axis_index("x") + 2 * lax.axis_index("y") + lax.axis_index("c")


def _peer(k):
    flip = lambda v, bit: 1 - v if bit else v
    px = flip(lax.axis_index("x"), k & 4)
    py = flip(lax.axis_index("y"), k & 2)
    pc = flip(lax.axis_index("c"), k & 1)
    return (px, py, pc), 4 * px + 2 * py + pc


def _all_gather(shards, dtypes):
    n = len(shards)
    sibling, chips = 1, (2, 4, 6)

    def body(*refs):
        ins, outs, stages = refs[:n], refs[n:2 * n], refs[2 * n:3 * n]
        send_sems, recv_sems, local_sems = refs[3 * n:]
        me = _my_index()
        for a in range(n):
            stages[a][...] = ins[a][...].astype(stages[a].dtype)
        local = [pltpu.make_async_copy(stages[a], outs[a].at[me], local_sems.at[a]) for a in range(n)]
        for cp in local:
            cp.start()

        def copy(a, k, src, slot, to):
            return pltpu.make_async_remote_copy(
                src_ref=src, dst_ref=outs[a].at[slot], send_sem=send_sems.at[a, k - 1],
                recv_sem=recv_sems.at[a, k - 1], device_id=_peer(to)[0], device_id_type=MESH)

        def own(a, k):
            return copy(a, k, stages[a], me, k)

        def passed(a, k):
            slot = _peer(k)[1]
            return copy(a, k ^ sibling, outs[a].at[slot], slot, sibling)

        def arrival(a, k):
            return copy(a, k, stages[a], _peer(k)[1], k)

        for k in (sibling,) + chips:
            for a in range(n):
                own(a, k).start()
        for k in chips:
            for a in range(n):
                arrival(a, k).wait_recv()
                passed(a, k).start()
        for a in range(n):
            arrival(a, sibling).wait_recv()
            for k in chips:
                arrival(a, k ^ sibling).wait_recv()
        for a in range(n):
            for k in (sibling,) + chips:
                own(a, k).wait_send()
            for k in chips:
                passed(a, k).wait_send()
        for cp in local:
            cp.wait()

    return pl.pallas_call(
        body, name="gather_weights",
        in_specs=[pl.BlockSpec(memory_space=pltpu.VMEM)] * n,
        out_specs=[pl.BlockSpec(memory_space=pl.ANY)] * n,
        out_shape=[jax.ShapeDtypeStruct((N_DEV,) + s.shape, dt) for s, dt in zip(shards, dtypes)],
        scratch_shapes=[pltpu.VMEM(s.shape, dt) for s, dt in zip(shards, dtypes)]
        + [pltpu.SemaphoreType.DMA((n, N_DEV - 1)), pltpu.SemaphoreType.DMA((n, N_DEV - 1)), pltpu.SemaphoreType.DMA((n,))],
        compiler_params=pltpu.CompilerParams(vmem_limit_bytes=VMEM_LIMIT),
    )(*shards)


def _exchange(parts, whole):
    n = len(parts)

    def body(*refs):
        ins, outs = refs[:n], refs[n:2 * n]
        send_sems, recv_sems, local_sems = refs[2 * n:]
        me = _my_index()

        def src(a, slab):
            return ins[a] if whole[a] else ins[a].at[slab]

        local = [pltpu.make_async_copy(src(a, me), outs[a].at[me], local_sems.at[a]) for a in range(n)]
        for cp in local:
            cp.start()

        def copy(a, k, slab, slot):
            peer, _ = _peer(k)
            return pltpu.make_async_remote_copy(
                src_ref=src(a, slab), dst_ref=outs[a].at[slot], send_sem=send_sems.at[a, k - 1],
                recv_sem=recv_sems.at[a, k - 1], device_id=peer, device_id_type=MESH)

        for k in range(1, N_DEV):
            for a in range(n):
                copy(a, k, _peer(k)[1], me).start()
        for k in range(1, N_DEV):
            for a in range(n):
                copy(a, k, _peer(k)[1], _peer(k)[1]).wait()
        for cp in local:
            cp.wait()

    return pl.pallas_call(
        body, name="exchange_grads",
        in_specs=[pl.BlockSpec(memory_space=pl.ANY)] * n,
        out_specs=[pl.BlockSpec(memory_space=pl.ANY)] * n,
        out_shape=[jax.ShapeDtypeStruct(((N_DEV,) + p.shape) if w else p.shape, p.dtype) for p, w in zip(parts, whole)],
        scratch_shapes=[pltpu.SemaphoreType.DMA((n, N_DEV - 1)), pltpu.SemaphoreType.DMA((n, N_DEV - 1)),
                        pltpu.SemaphoreType.DMA((n,))],
    )(*parts)


def _sequencer_exchange(parts, whole, name, collective_id):
    n = len(parts)
    srcs = [jax.new_ref(p, memory_space=pltpu.MemorySpace.HBM) for p in parts]
    lands = [jax.empty_ref(jax.ShapeDtypeStruct(((N_DEV,) + p.shape) if w else p.shape, p.dtype),
                           memory_space=pltpu.MemorySpace.HBM) for p, w in zip(parts, whole)]

    @pl.kernel(mesh=plsc.ScalarSubcoreMesh(axis_name="sequencer", num_cores=1), name=name,
               scratch_types=(pltpu.SemaphoreType.DMA((n, N_DEV - 1)), pltpu.SemaphoreType.DMA((n, N_DEV - 1)),
                              pltpu.SemaphoreType.DMA((n,))),
               compiler_params=pltpu.CompilerParams(collective_id=collective_id))
    def launch(send_sems, recv_sems, local_sems):
        barrier = pltpu.get_barrier_semaphore()
        for k in range(1, N_DEV):
            pl.semaphore_signal(barrier, inc=1, device_id=_peer(k)[0], device_id_type=MESH)
        pl.semaphore_wait(barrier, N_DEV - 1)
        me = _my_index()

        def src(a, slab):
            return srcs[a] if whole[a] else srcs[a].at[slab]

        local = [pltpu.make_async_copy(src(a, me), lands[a].at[me], local_sems.at[a]) for a in range(n)]
        for cp in local:
            cp.start()

        def copy(a, k, slab, slot):
            return pltpu.make_async_remote_copy(
                src_ref=src(a, slab), dst_ref=lands[a].at[slot], send_sem=send_sems.at[a, k - 1],
                recv_sem=recv_sems.at[a, k - 1], device_id=_peer(k)[0], device_id_type=MESH)

        for k in range(1, N_DEV):
            for a in range(n):
                copy(a, k, _peer(k)[1], me).start()
        for k in range(1, N_DEV):
            for a in range(n):
                copy(a, k, _peer(k)[1], _peer(k)[1]).wait()
        for cp in local:
            cp.wait()

    launch()
    return [land[...] for land in lands]


def _row_block(rows):
    if rows <= ROW_TILE:
        return rows
    return next(rb for rb in range(ROW_TILE, 0, -16) if rows % rb == 0)


def _adamw(landing, w, m, v, name):
    rows, cols = w.shape
    rb = _row_block(rows)

    def body(l_ref, w_ref, m_ref, v_ref, g_ref, d_ref, m2_ref, v2_ref):
        g = l_ref[0].astype(F32)
        for p in range(1, N_DEV):
            g = g + l_ref[p].astype(F32)
        g_ref[...] = g
        d_ref[...], m2_ref[...], v2_ref[...] = _adamw_step(g, w_ref[...], m_ref[...], v_ref[...])

    flat = pl.BlockSpec((rb, cols), lambda i: (i, 0))
    return pl.pallas_call(
        body, name=name, grid=(rows // rb,),
        in_specs=[pl.BlockSpec((N_DEV, rb, cols), lambda i: (0, i, 0)), flat, flat, flat],
        out_specs=[flat] * 4,
        out_shape=[jax.ShapeDtypeStruct((rows, cols), F32)] * 4,
        compiler_params=_params("parallel"),
    )(landing, w, m, v)


def _adamw_step(g, w, m, v):
    m2 = ADAM_B1 * m + (1.0 - ADAM_B1) * g
    v2 = ADAM_B2 * v + (1.0 - ADAM_B2) * (g * g)
    m_hat = m2 / (1.0 - ADAM_B1 ** ADAM_STEP)
    v_hat = v2 / (1.0 - ADAM_B2 ** ADAM_STEP)
    return -ADAM_LR * (m_hat / (jnp.sqrt(v_hat) + ADAM_EPS) + ADAM_WD * w), m2, v2


_REPLICATED = (
    ("mix_norm_g", D_MODEL), ("q_norm_g", Q_LORA), ("kv_norm_g", KV_LORA), ("conv_b", D_CONV), ("conv_ln_g", D_CONV),
    ("conv_ln_b", D_CONV), ("conv_out_g", D_CONV), ("attn_out_g", D_CONV), ("ffn_norm_g", D_MODEL),
    ("ffn_conv_b", D_UP), ("final_norm_g", D_MODEL),
)
_REPLICATED_WIDTH = sum(size for _, size in _REPLICATED) + _LANES

_WEIGHT_ORDER = (
    "meta_tokens", "mix_norm_g", "w_in", "q_norm_g", "w_uq", "kv_norm_g", "w_ukv", "conv_w", "conv_b", "conv_ln_g",
    "conv_ln_b", "conv_out_g", "attn_out_g", "w_out", "ffn_norm_g", "w_ffn_up", "ffn_conv_w", "ffn_conv_b",
    "w_ffn_down", "final_norm_g",
)


def _pack_replicated(grads, loss):
    rows = [grads[name].reshape(1, size) for name, size in _REPLICATED]
    return jnp.concatenate(rows + [jnp.broadcast_to(loss.reshape(1, 1), (1, _LANES))], axis=-1)


def _adamw_replicated(landing, weights, moments_m, moments_v):
    n = len(_REPLICATED)

    def body(*refs):
        l_ref, ins, outs = refs[0], refs[1:1 + 3 * n], refs[1 + 3 * n:]
        total = l_ref[0]
        for p in range(1, N_DEV):
            total = total + l_ref[p]
        at = 0
        for a, (_, size) in enumerate(_REPLICATED):
            g = total[:, at:at + size]
            w_ref, m_ref, v_ref = ins[3 * a:3 * a + 3]
            g_ref, d_ref, m2_ref, v2_ref = outs[4 * a:4 * a + 4]
            g_ref[...] = g
            d_ref[...], m2_ref[...], v2_ref[...] = _adamw_step(g, w_ref[...], m_ref[...], v_ref[...])
            at += size
        outs[-1][...] = total[:, at:at + _LANES]

    operands, out_shapes = [], []
    for name, size in _REPLICATED:
        operands += [weights[name].reshape(1, size), moments_m[name].reshape(1, size), moments_v[name].reshape(1, size)]
        out_shapes += [jax.ShapeDtypeStruct((1, size), F32)] * 4
    out_shapes.append(jax.ShapeDtypeStruct((1, _LANES), F32))
    outs = pl.pallas_call(body, name="adamw_replicated", out_shape=out_shapes)(landing, *operands)
    return outs[-1][0, 0], {name: outs[4 * a:4 * a + 4] for a, (name, _) in enumerate(_REPLICATED)}


def _pad_rows(a, rows):
    return jnp.pad(a, ((0, rows - a.shape[0]), (0, 0)))


def _slabs(a):
    r, c = a.shape
    return a.reshape(r, N_DEV, c // N_DEV).transpose(1, 0, 2)


def _unslab(a):
    g, r, c = a.shape
    return a.transpose(1, 0, 2).reshape(r, g * c)


def _local_step(x, target, w, n_rows, ffn_weights, send_grads):
    cos_t, sin_t = _rope_tables(n_rows)
    cos, sin = cos_t.T, sin_t.T
    meta_pad, g1, gf = w["meta_pad"], w["mix_norm_g"], w["final_norm_g"]
    gq, gkv, gb_col = w["q_norm_g"], w["kv_norm_g"], w["attn_out_g"].reshape(D_ATTN, 1)
    nb, ag, cq, ckv, kr = _fwd_in(x, meta_pad, g1, w["w_in"], n_rows)
    mix_a, u1 = _fwd_conv(ag, w["conv_w"], w["conv_b"], w["conv_ln_g"], w["conv_ln_b"], w["conv_out_g"], n_rows)
    q_t, k, v, v_t, cqn, ckvn = _fwd_qkv(cq, ckv, kr, gq, gkv, w["wq_t"], w["w_ukv"], w["wv_t"], cos, sin, cos_t, sin_t, n_rows)
    o_t, lse = _attn_fwd(q_t, k, v_t, n_rows)
    w_out, w_up, w_down = ffn_weights()
    mix_bt, h1 = _fwd_out(x, meta_pad, mix_a, o_t, gb_col, w_out, n_rows)
    n2, up0, act, da, db, dh2, loss, dgf = _fwd_ffn(
        h1, target, w["ffn_norm_g"], w_up, w["fw"], w["fb"], w_down, gf, n_rows)

    dup, dfb = _bwd_ffn_act(dh2, da, db, w_down, n_rows)
    dup0, dh1, dfw, dg2 = _bwd_ffn_up(dup, up0, h1, dh2, w["ffn_norm_g"], w_up, w["fw"], n_rows)
    grad_w_out = jnp.concatenate([_weight_grad(mix_a, dh1, "grad_w_out_conv")[0],
                                  _weight_grad(mix_bt, dh1, "grad_w_out_attn", a_transposed=True)[0]], axis=0)
    stage0 = {
        "w_ffn_up": _weight_grad(dup0, n2, "grad_w_ffn_up"),
        "w_ffn_down": _weight_grad(act, dh2, "grad_w_ffn_down").reshape(N_DEV, D_FF // N_DEV, D_MODEL),
        "w_out": grad_w_out.reshape(N_DEV, D_MODEL // N_DEV, D_MODEL),
    }
    stage0, dh1 = lax.optimization_barrier((stage0, dh1))
    send_grads(0, stage0)
    do_t, delta, du1, dgb, dga, dlg, dlb, dcb = _bwd_out(
        dh1, o_t, u1, w_out, gb_col, w["conv_ln_g"], w["conv_ln_b"], w["conv_out_g"], n_rows)
    dq_t, dk, dv = _attn_bwd(q_t, k, v, do_t, lse, delta, n_rows)
    dqraw_t, dkv, dcq, dckv, dkr, dgq, dgkv = _bwd_qkv(
        dq_t, dk, dv, cq, ckv, gq, gkv, w["wq_t"], w["w_ukv"], cos, sin, cos_t, sin_t, n_rows)
    dz, dcw = _bwd_conv(du1, ag, w["conv_w"], dcq, dckv, dkr, n_rows)
    stage1 = {
        "w_in": _weight_grad(dz, nb, "grad_w_in")[0].reshape(N_DEV, D_IN // N_DEV, D_MODEL),
        "w_uq": _weight_grad(dqraw_t.reshape(N_HEADS * QK_DIM, n_rows), cqn, "grad_w_uq", a_transposed=True)[0].reshape(
            N_HEADS, QK_DIM, Q_LORA),
        "w_ukv": _slabs(_weight_grad(ckvn, dkv, "grad_w_ukv")[0]),
        "conv_w": _slabs(dcw),
        "ffn_conv_w": dfw[:, :, :UP_SLAB],
    }
    stage1, dz = lax.optimization_barrier((stage1, dz))
    send_grads(1, stage1)
    gx, gmeta, dg1 = _bwd_in(dz, x, meta_pad, dh1, g1, w["w_in"], n_rows)

    sharded = {"meta_tokens": _slabs(gmeta[DEAD:])}
    replicated = {
        "mix_norm_g": dg1, "q_norm_g": dgq, "kv_norm_g": dgkv, "conv_b": dcb, "conv_ln_g": dlg, "conv_ln_b": dlb,
        "conv_out_g": dga, "attn_out_g": dgb, "ffn_norm_g": dg2, "ffn_conv_b": dfb, "final_norm_g": dgf,
    }
    return loss[0, 0], gx, sharded, replicated


_SHARDED = (
    ("w_in", None, BF16), ("w_uq", None, BF16), ("w_ukv", None, BF16), ("w_out", None, BF16), ("w_ffn_up", None, BF16),
    ("w_ffn_down", None, BF16), ("conv_w", 32, F32), ("ffn_conv_w", 8, F32), ("meta_tokens", None, F32),
)
GATHER_LATE_ID = 3
EXCHANGE_STAGE_IDS = (4, 5)
_LATE_WEIGHTS = ("w_out", "w_ffn_up", "w_ffn_down")
_COLUMN_SHARDS = ("w_in", "w_uq", "w_ffn_up")


def kernel(x, meta_tokens, mix_norm_g, w_in, q_norm_g, w_uq, kv_norm_g, w_ukv, conv_w, conv_b, conv_ln_g, conv_ln_b, conv_out_g, attn_out_g, w_out, ffn_norm_g, w_ffn_up, ffn_conv_w, ffn_conv_b, w_ffn_down, final_norm_g, loss_target, m_meta_tokens, m_mix_norm_g, m_w_in, m_q_norm_g, m_w_uq, m_kv_norm_g, m_w_ukv, m_conv_w, m_conv_b, m_conv_ln_g, m_conv_ln_b, m_conv_out_g, m_attn_out_g, m_w_out, m_ffn_norm_g, m_w_ffn_up, m_ffn_conv_w, m_ffn_conv_b, m_w_ffn_down, m_final_norm_g, v_meta_tokens, v_mix_norm_g, v_w_in, v_q_norm_g, v_w_uq, v_kv_norm_g, v_w_ukv, v_conv_w, v_conv_b, v_conv_ln_g, v_conv_ln_b, v_conv_out_g, v_attn_out_g, v_w_out, v_ffn_norm_g, v_w_ffn_up, v_ffn_conv_w, v_ffn_conv_b, v_w_ffn_down, v_final_norm_g):
    given = dict(locals())
    weights = {name: given[name] for name in _WEIGHT_ORDER}
    moments_m = {name: given["m_" + name] for name in _WEIGHT_ORDER}
    moments_v = {name: given["v_" + name] for name in _WEIGHT_ORDER}
    seq = x.shape[1]
    n_rows = ROW_TILE + seq

    def shard2d(name, a):
        a = a.reshape(a.shape[-2], a.shape[-1])
        return a.T if name in _COLUMN_SHARDS else a

    early = [entry for entry in _SHARDED if entry[0] not in _LATE_WEIGHTS]
    shards = []
    for name, pad_to, _ in early:
        s = shard2d(name, weights[name])
        shards.append(s if pad_to is None else _pad_rows(s, pad_to))
    gathered = dict(zip([name for name, _, _ in early], _all_gather(shards, [dt for _, _, dt in early])))
    behind = gathered["meta_tokens"][0, 0, 0] * 0.0
    late_parts = [(shard2d(name, weights[name]) + behind).astype(BF16) for name in _LATE_WEIGHTS]
    late = _sequencer_exchange(late_parts, [True] * len(late_parts), "gather_late", GATHER_LATE_ID)
    meta_full = _unslab(gathered["meta_tokens"])
    full = {
        "meta_pad": jnp.concatenate([jnp.zeros((DEAD, D_MODEL), F32), meta_full], axis=0),
        "w_in": gathered["w_in"].reshape(D_IN, D_MODEL),
        "wq_t": gathered["w_uq"],
        "w_ukv": gathered["w_ukv"],
        "wv_t": gathered["w_ukv"][:, :, QK_NOPE:].transpose(0, 2, 1),
        "conv_w": _unslab(gathered["conv_w"][:, :CONV_WIDTH]),
        "fw": jnp.pad(gathered["ffn_conv_w"][:, :FFN_CONV_WIDTH], ((0, 0), (0, 0), (0, UP_PAD - UP_SLAB))),
        "fb": jnp.pad(ffn_conv_b.reshape(N_DEV, 1, UP_SLAB), ((0, 0), (0, 0), (0, UP_PAD - UP_SLAB))),
        "final_norm_g": final_norm_g.reshape(1, D_MODEL),
    }
    for name in ("mix_norm_g", "q_norm_g", "kv_norm_g", "conv_b", "conv_ln_g", "conv_ln_b", "conv_out_g", "attn_out_g",
                 "ffn_norm_g"):
        full[name] = weights[name]

    def ffn_weights():
        w_out_all, w_up_all, w_down_all = late
        return (w_out_all.reshape(D_MODEL, D_MODEL), w_up_all, w_down_all.reshape(N_ACT_SLAB, UP_SLAB, D_MODEL))

    wire = {name: (pad_to, dt) for name, pad_to, dt in _SHARDED}
    landing = {}

    def on_the_wire(name, slabs):
        pad_to, dt = wire[name]
        slabs = slabs.astype(dt)
        return slabs if pad_to is None else jnp.pad(slabs, ((0, 0), (0, pad_to - slabs.shape[1]), (0, 0)))

    def send_grads(stage, grads):
        parts = [on_the_wire(name, slabs) for name, slabs in grads.items()]
        if landing:
            arrived = list(landing)
            parts, held = lax.optimization_barrier((parts, [landing[name] for name in arrived]))
            landing.update(zip(arrived, held))
        landed = _sequencer_exchange(parts, [False] * len(parts), f"exchange_stage{stage}", EXCHANGE_STAGE_IDS[stage])
        landing.update(zip(grads, landed))

    loss, gx, sharded, replicated = _local_step(x[0], loss_target[0], full, n_rows, ffn_weights, send_grads)

    parts = [on_the_wire(name, slabs) for name, slabs in sharded.items()] + [_pack_replicated(replicated, loss)]
    landed = _exchange(parts, [False] * len(sharded) + [True])
    landing.update(zip(sharded, landed[:-1]))

    grad, delta, new_m, new_v = {}, {}, {}, {}
    for name, pad_to, _ in _SHARDED:
        land = landing[name]
        ws, ms, vs = (shard2d(name, a[name]) for a in (weights, moments_m, moments_v))
        rows = ws.shape[0]
        if pad_to is not None:
            ws, ms, vs = _pad_rows(ws, pad_to), _pad_rows(ms, pad_to), _pad_rows(vs, pad_to)
        outs = _adamw(land, ws, ms, vs, "adamw_" + name)
        shape = weights[name].shape
        grad[name], delta[name], new_m[name], new_v[name] = (
            (o.T if name in _COLUMN_SHARDS else o[:rows]).reshape(shape) for o in outs)
    loss, updates = _adamw_replicated(landed[-1], weights, moments_m, moments_v)
    for name, outs in updates.items():
        grad[name], delta[name], new_m[name], new_v[name] = (o.reshape(weights[name].shape) for o in outs)

    return (loss, gx[None], *[grad[n] for n in _WEIGHT_ORDER], *[delta[n] for n in _WEIGHT_ORDER],
            *[new_m[n] for n in _WEIGHT_ORDER], *[new_v[n] for n in _WEIGHT_ORDER])
```

```python
import functools

import jax
import jax.numpy as jnp
from jax import lax
from jax.experimental import pallas as pl
from jax.experimental.pallas import tpu as pltpu
from jax.experimental.pallas import tpu_sc as plsc

F32 = jnp.float32
BF16 = jnp.bfloat16

N_DEV = 8
D_MODEL = 1024
CHUNK = 64
CHUNK_SHIFT = 6
N_META = 16
D_CONV = 512
CONV_WIDTH = 31
N_HEADS = 8
QK_NOPE = 64
QK_ROPE = 32
QK_DIM = QK_NOPE + QK_ROPE
V_HEAD = 64
KV_HEAD = QK_NOPE + V_HEAD
D_ATTN = N_HEADS * V_HEAD
Q_LORA = 384
KV_LORA = 256
ROPE_THETA = 10000.0
D_IN = 2 * D_CONV + Q_LORA + KV_LORA + QK_ROPE
D_FF = 2816
D_UP = 2 * D_FF
FFN_CONV_WIDTH = 3
UP_SLAB = D_UP // N_DEV
N_ACT_SLAB = D_FF // UP_SLAB
EPS = 1e-6
NEG = -1e30
_LN2 = 0.6931471805599453
QK_LOGIT_SCALE = QK_DIM ** -0.5 / _LN2
ADAM_LR = 0.001
ADAM_B1 = 0.9
ADAM_B2 = 0.999
ADAM_EPS = 1e-08
ADAM_WD = 0.01
ADAM_STEP = 10

ROW_TILE = 256
DEAD = ROW_TILE - N_META
CONV_HALO = 32
FFN_HALO = 16
VMEM_LIMIT = 56 * 1024 * 1024
_LANES = 128

MESH = pl.DeviceIdType.MESH


def _dot(a, b):
    return jnp.dot(a, b, preferred_element_type=F32)


def _dot_nt(a, b):
    return lax.dot_general(a, b, (((1,), (1,)), ((), ())), preferred_element_type=F32)


def _dot_tn(a, b):
    return lax.dot_general(a, b, (((0,), (0,)), ((), ())), preferred_element_type=F32)


def _sigmoid(x):
    return 1.0 / (1.0 + jnp.exp2(x * (-1.0 / _LN2)))


def _rms_fwd(x, g):
    r = lax.rsqrt(jnp.mean(x * x, axis=-1, keepdims=True) + EPS)
    return x * r * g


def _rms_bwd(dy, x, g):
    r = lax.rsqrt(jnp.mean(x * x, axis=-1, keepdims=True) + EPS)
    w = dy * g
    dx = r * w - x * (r * r * r) * jnp.mean(w * x, axis=-1, keepdims=True)
    return dx, jnp.sum(dy * x * r, axis=0, keepdims=True)


def _rope(x, cos, sin):
    half = QK_ROPE // 2
    x1, x2 = x[:, :half], x[:, half:]
    return jnp.concatenate([x1 * cos - x2 * sin, x2 * cos + x1 * sin], axis=-1)


def _rope_t(dy, cos, sin):
    half = QK_ROPE // 2
    d1, d2 = dy[:, :half], dy[:, half:]
    return jnp.concatenate([d1 * cos + d2 * sin, d2 * cos - d1 * sin], axis=-1)


def _row_ids(i, rows):
    return i * rows + lax.broadcasted_iota(jnp.int32, (rows, 1), 0)


def _accumulate(ref, first, value):
    @pl.when(first)
    def _():
        ref[...] = value

    @pl.when(jnp.logical_not(first))
    def _():
        ref[...] += value


def _tile_spec(shape):
    nd = len(shape)
    if nd == 2:
        return pl.BlockSpec((ROW_TILE, shape[1]), lambda i: (i, 0))
    return pl.BlockSpec((shape[0], ROW_TILE, shape[2]), lambda i: (0, i, 0))


def _whole_spec(shape):
    nd = len(shape)
    return pl.BlockSpec(tuple(shape), lambda i: (0,) * nd, pipeline_mode=pl.Buffered(1))


def _acc_spec(shape):
    nd = len(shape)
    return pl.BlockSpec(tuple(shape), lambda i: (0,) * nd)


def _real_spec(width):
    return pl.BlockSpec((ROW_TILE, width), lambda i: (jnp.maximum(i - 1, 0), 0))


def _params(*semantics):
    return pltpu.CompilerParams(dimension_semantics=semantics, vmem_limit_bytes=VMEM_LIMIT)


def _fwd_in(x, meta_pad, g1, w_in, n_rows):
    nt = n_rows // ROW_TILE

    def body(x_ref, meta_ref, g_ref, w_ref, nb_ref, ag_ref, cq_ref, ckv_ref, kr_ref):
        i = pl.program_id(0)
        h0 = jnp.where(i == 0, meta_ref[...], x_ref[...])
        nb = _rms_fwd(h0, g_ref[...]).astype(BF16)
        nb_ref[...] = nb
        z = _dot_nt(nb, w_ref[...])
        ag_ref[...] = z[:, :2 * D_CONV]
        cq_ref[...] = z[:, 2 * D_CONV:2 * D_CONV + Q_LORA]
        ckv_ref[...] = z[:, 2 * D_CONV + Q_LORA:2 * D_CONV + Q_LORA + KV_LORA]
        kr_ref[...] = z[:, 2 * D_CONV + Q_LORA + KV_LORA:]

    out_shapes = [
        jax.ShapeDtypeStruct((n_rows, D_MODEL), BF16),
        jax.ShapeDtypeStruct((n_rows, 2 * D_CONV), F32),
        jax.ShapeDtypeStruct((n_rows, Q_LORA), F32),
        jax.ShapeDtypeStruct((n_rows, KV_LORA), F32),
        jax.ShapeDtypeStruct((n_rows, QK_ROPE), F32),
    ]
    return pl.pallas_call(
        body, name="fwd_in", grid=(nt,),
        in_specs=[_real_spec(D_MODEL), _whole_spec(meta_pad.shape), _whole_spec(g1.shape), _whole_spec(w_in.shape)],
        out_specs=[_tile_spec(s.shape) for s in out_shapes],
        out_shape=out_shapes,
        compiler_params=_params("parallel"),
    )(x, meta_pad, g1, w_in)


def _conv_chain(u1, ln_g, ln_b):
    mu = jnp.mean(u1, axis=-1, keepdims=True)
    xc = u1 - mu
    rstd = lax.rsqrt(jnp.mean(xc * xc, axis=-1, keepdims=True) + EPS)
    xh = xc * rstd
    u2 = xh * ln_g + ln_b
    return xh, u2, u2 * _sigmoid(u2), rstd


def _fwd_conv(ag, conv_w, conv_b, ln_g, ln_b, out_g, n_rows):
    nt = n_rows // ROW_TILE

    def body(ag_ref, w_ref, b_ref, lg_ref, lb_ref, og_ref, mix_ref, u1_ref, ext_ref, conv_ref):
        i = pl.program_id(0)

        @pl.when(i == 0)
        def _():
            ext_ref[:, 0:CONV_HALO, :] = jnp.zeros((CONV_PLANES, CONV_HALO, _LANES), F32)

        ag_t = ag_ref[...]
        live = _row_ids(i, ROW_TILE) >= DEAD
        u0 = jnp.where(live, ag_t[:, :D_CONV] * _sigmoid(ag_t[:, D_CONV:]), 0.0)
        _to_planes(ext_ref, (), slice(CONV_HALO, None), u0)
        first = CONV_HALO - (CONV_WIDTH - 1)
        for c in range(CONV_PLANES):
            taps = w_ref[:, c * _LANES:(c + 1) * _LANES]
            for p in range(PHASES):
                acc = jnp.zeros((PHASE_ROWS, _LANES), F32)
                for k in range(CONV_WIDTH):
                    acc = acc + taps[k:k + 1, :] * ext_ref[c, _phase(first + k + p), :]
                conv_ref[c, _phase(p), :] = acc
        ext_ref[:, 0:CONV_HALO, :] = ext_ref[:, ROW_TILE:ROW_TILE + CONV_HALO, :]
        u1 = _from_planes(conv_ref, (), D_CONV) + b_ref[...]
        u1_ref[...] = u1
        _, _, u3, _ = _conv_chain(u1, lg_ref[...], lb_ref[...])
        mix_ref[...] = _rms_fwd(u3, og_ref[...]).astype(BF16)

    out_shapes = [jax.ShapeDtypeStruct((n_rows, D_CONV), BF16), jax.ShapeDtypeStruct((n_rows, D_CONV), F32)]
    small = [conv_w, conv_b, ln_g, ln_b, out_g]
    return pl.pallas_call(
        body, name="fwd_conv", grid=(nt,),
        in_specs=[_tile_spec(ag.shape)] + [_whole_spec(a.shape) for a in small],
        out_specs=[_tile_spec(s.shape) for s in out_shapes],
        out_shape=out_shapes,
        scratch_shapes=[pltpu.VMEM((CONV_PLANES, ROW_TILE + CONV_HALO, _LANES), F32),
                        pltpu.VMEM((CONV_PLANES, ROW_TILE, _LANES), F32)],
        compiler_params=_params("arbitrary"),
    )(ag, *small)


def _lane_tile(shape):
    if len(shape) == 2:
        return pl.BlockSpec((shape[0], ROW_TILE), lambda i: (0, i))
    return pl.BlockSpec((shape[0], shape[1], ROW_TILE), lambda i: (0, 0, i))


def _rope_rows(x, cos, sin):
    half = QK_ROPE // 2
    x1, x2 = x[:half], x[half:]
    return jnp.concatenate([x1 * cos - x2 * sin, x2 * cos + x1 * sin], axis=0)


def _rope_rows_t(dy, cos, sin):
    half = QK_ROPE // 2
    d1, d2 = dy[:half], dy[half:]
    return jnp.concatenate([d1 * cos + d2 * sin, d2 * cos - d1 * sin], axis=0)


def _fwd_qkv(cq, ckv, kr, gq, gkv, wq_t, w_ukv, wv_t, cos, sin, cos_t, sin_t, n_rows):
    nt = n_rows // ROW_TILE

    def body(cq_ref, ckv_ref, kr_ref, gq_ref, gkv_ref, wqt_ref, wkv_ref, wvt_ref, cos_ref, sin_ref, cost_ref, sint_ref,
             qt_ref, k_ref, v_ref, vt_ref, cqn_ref, ckvn_ref):
        cqn = _rms_fwd(cq_ref[...], gq_ref[...]).astype(BF16)
        ckvn = _rms_fwd(ckv_ref[...], gkv_ref[...]).astype(BF16)
        cqn_ref[...] = cqn
        ckvn_ref[...] = ckvn
        k_rot = _rope(kr_ref[...], cos_ref[...], sin_ref[...])
        cos_rows, sin_rows = cost_ref[...], sint_ref[...]
        for h in range(N_HEADS):
            q_raw = _dot_nt(wqt_ref[h], cqn)
            q_h = jnp.concatenate([q_raw[:QK_NOPE], _rope_rows(q_raw[QK_NOPE:], cos_rows, sin_rows)], axis=0)
            qt_ref[h] = (q_h * QK_LOGIT_SCALE).astype(BF16)
            kv = _dot(ckvn, wkv_ref[h])
            k_ref[h] = jnp.concatenate([kv[:, :QK_NOPE], k_rot], axis=-1).astype(BF16)
            v_ref[h] = kv[:, QK_NOPE:].astype(BF16)
            vt_ref[h] = _dot_nt(wvt_ref[h], ckvn).astype(BF16)

    out_shapes = [
        jax.ShapeDtypeStruct((N_HEADS, QK_DIM, n_rows), BF16),
        jax.ShapeDtypeStruct((N_HEADS, n_rows, QK_DIM), BF16),
        jax.ShapeDtypeStruct((N_HEADS, n_rows, V_HEAD), BF16),
        jax.ShapeDtypeStruct((N_HEADS, V_HEAD, n_rows), BF16),
        jax.ShapeDtypeStruct((n_rows, Q_LORA), BF16),
        jax.ShapeDtypeStruct((n_rows, KV_LORA), BF16),
    ]
    tiles = [cq, ckv, kr]
    whole = [gq, gkv, wq_t, w_ukv, wv_t]
    out_specs = [_lane_tile(out_shapes[0].shape), _tile_spec(out_shapes[1].shape), _tile_spec(out_shapes[2].shape),
                 _lane_tile(out_shapes[3].shape), _tile_spec(out_shapes[4].shape), _tile_spec(out_shapes[5].shape)]
    return pl.pallas_call(
        body, name="fwd_qkv", grid=(nt,),
        in_specs=[_tile_spec(a.shape) for a in tiles] + [_whole_spec(a.shape) for a in whole]
        + [_tile_spec(cos.shape), _tile_spec(sin.shape), _lane_tile(cos_t.shape), _lane_tile(sin_t.shape)],
        out_specs=out_specs,
        out_shape=out_shapes,
        compiler_params=_params("parallel"),
    )(*tiles, *whole, cos, sin, cos_t, sin_t)


def _chunk_of(rows):
    return jnp.where(rows >= ROW_TILE, lax.shift_right_arithmetic(rows - ROW_TILE, CHUNK_SHIFT) + 1, 0)


def _visible(i, j):
    k_rows = j * ROW_TILE + lax.broadcasted_iota(jnp.int32, (ROW_TILE, 1), 0)
    q_rows = i * ROW_TILE + lax.broadcasted_iota(jnp.int32, (1, ROW_TILE), 1)
    return jnp.logical_and(_chunk_of(q_rows) >= _chunk_of(k_rows), k_rows >= DEAD)


def _attn_fwd(q_t, k, v_t, n_rows):
    nt = n_rows // ROW_TILE

    def body(qt_ref, k_ref, vt_ref, ot_ref, lse_ref, max_ref, sum_ref):
        i = pl.program_id(0)
        q_ts = [qt_ref[h] for h in range(N_HEADS)]

        def key_rows(j):
            return pl.ds(pl.multiple_of(j * ROW_TILE, ROW_TILE), ROW_TILE)

        def make_step(masked, tiles, first=0):
            def step(t, carry):
                js = [first + tiles * t + u for u in range(tiles)]
                scores = [[_dot(k_ref[h, key_rows(j), :], q_ts[h]) for h in range(N_HEADS)] for j in js]
                for j, tile_scores in zip(js, scores):
                    visible = _visible(i, j) if masked else None
                    probs, alphas = [], []
                    for h in range(N_HEADS):
                        m = max_ref[h]
                        s = jnp.where(visible, tile_scores[h], NEG) if masked else tile_scores[h]
                        m_new = jnp.maximum(m, jnp.max(s, axis=0, keepdims=True))
                        alpha = jnp.exp2(m - m_new)
                        p = jnp.exp2(s - m_new)
                        probs.append(p.astype(BF16))
                        alphas.append(alpha)
                        max_ref[h] = m_new
                        sum_ref[h] = alpha * sum_ref[h] + jnp.sum(p, axis=0, keepdims=True)
                    for h in range(N_HEADS):
                        ot_ref[h] = alphas[h] * ot_ref[h] + _dot(vt_ref[h, :, key_rows(j)], probs[h])
                return carry
            return step

        max_ref[...] = jnp.full(max_ref.shape, NEG, F32)
        sum_ref[...] = jnp.zeros_like(sum_ref)
        ot_ref[...] = jnp.zeros_like(ot_ref)
        between = jnp.maximum(i - 1, 0)
        quads = lax.shift_right_logical(between, 2)
        pairs = jnp.bitwise_and(lax.shift_right_logical(between, 1), 1)
        make_step(True, 1)(0, 0)
        lax.fori_loop(0, quads, make_step(False, 4, first=1), 0)
        lax.fori_loop(0, pairs, make_step(False, 2, first=1 + 4 * quads), 0)
        lax.fori_loop(1 + 4 * quads + 2 * pairs, i, make_step(False, 1), 0)
        lax.fori_loop(jnp.maximum(i, 1), i + 1, make_step(True, 1), 0)
        for h in range(N_HEADS):
            l = sum_ref[h]
            ot_ref[h] = ot_ref[h] / l
            lse_ref[h] = max_ref[h] + jnp.log2(l)

    out_shapes = [jax.ShapeDtypeStruct((N_HEADS, V_HEAD, n_rows), F32), jax.ShapeDtypeStruct((N_HEADS, 1, n_rows), F32)]
    return pl.pallas_call(
        body, name="attn_fwd", grid=(nt,),
        in_specs=[_lane_tile(q_t.shape), _whole_spec(k.shape), _whole_spec(v_t.shape)],
        out_specs=[_lane_tile(s.shape) for s in out_shapes],
        out_shape=out_shapes,
        scratch_shapes=[pltpu.VMEM((N_HEADS, 1, ROW_TILE), F32), pltpu.VMEM((N_HEADS, 1, ROW_TILE), F32)],
        compiler_params=_params("parallel"),
    )(q_t, k, v_t)


def _heads_to_rows(ref):
    return jnp.concatenate([ref[h] for h in range(N_HEADS)], axis=0)


def _rms_cols(x, g_col):
    r = lax.rsqrt(jnp.mean(x * x, axis=0, keepdims=True) + EPS)
    return x * r * g_col


def _fwd_out(x, meta_pad, mix_a, o_t, gb_col, w_out, n_rows):
    nt = n_rows // ROW_TILE

    def body(x_ref, meta_ref, mixa_ref, ot_ref, gb_ref, w_ref, mixbt_ref, h1_ref):
        i = pl.program_id(0)
        h0 = jnp.where(i == 0, meta_ref[...], x_ref[...])
        mix_bt = _rms_cols(_heads_to_rows(ot_ref), gb_ref[...]).astype(BF16)
        mixbt_ref[...] = mix_bt
        h1_ref[...] = h0 + _dot(mixa_ref[...], w_ref[:D_CONV, :]) + _dot_tn(mix_bt, w_ref[D_CONV:, :])

    out_shapes = [jax.ShapeDtypeStruct((D_ATTN, n_rows), BF16), jax.ShapeDtypeStruct((n_rows, D_MODEL), F32)]
    return pl.pallas_call(
        body, name="fwd_out", grid=(nt,),
        in_specs=[_real_spec(D_MODEL), _whole_spec(meta_pad.shape), _tile_spec(mix_a.shape), _lane_tile(o_t.shape),
                  _whole_spec(gb_col.shape), _whole_spec(w_out.shape)],
        out_specs=[_lane_tile(out_shapes[0].shape), _tile_spec(out_shapes[1].shape)],
        out_shape=out_shapes,
        compiler_params=_params("parallel"),
    )(x, meta_pad, mix_a, o_t, gb_col, w_out)


PHASES = 8
PHASE_ROWS = ROW_TILE // PHASES
UP_PLANES = -(-UP_SLAB // _LANES)
UP_PAD = UP_PLANES * _LANES
CONV_PLANES = D_CONV // _LANES


def _phase(start):
    return pl.ds(start, PHASE_ROWS, stride=PHASES)


def _to_planes(ref, lead, rows, value):
    width = value.shape[-1]
    for c in range(-(-width // _LANES)):
        part = value[:, c * _LANES:min((c + 1) * _LANES, width)]
        if part.shape[-1] < _LANES:
            part = jnp.concatenate([part, jnp.zeros((part.shape[0], _LANES - part.shape[-1]), part.dtype)], axis=-1)
        ref[(*lead, c, rows, slice(None))] = part


def _from_planes(ref, lead, width):
    planes = [ref[(*lead, c)] for c in range(-(-width // _LANES))]
    last = width - (len(planes) - 1) * _LANES
    return jnp.concatenate(planes[:-1] + [planes[-1][:, :last]], axis=-1)


def _fwd_ffn(h1, target, g2, w_up, fw, fb, w_down, gf, n_rows):
    nt = n_rows // ROW_TILE

    def body(h1_ref, t_ref, g2_ref, wup_ref, fw_ref, fb_ref, wdn_ref, gf_ref,
             n2_ref, up0_ref, act_ref, da_ref, db_ref, dh2_ref, loss_ref, dgf_ref, ext_ref):
        i = pl.program_id(0)

        @pl.when(i == 0)
        def _():
            ext_ref[:, 0:FFN_HALO, :] = jnp.zeros((N_DEV, FFN_HALO, UP_SLAB), F32)

        h1_t = h1_ref[...]
        live = _row_ids(i, ROW_TILE) >= DEAD
        n2 = jnp.where(live, _rms_fwd(h1_t, g2_ref[...]), 0.0).astype(BF16)
        n2_ref[...] = n2
        for s in range(N_DEV):
            up0 = _dot_nt(n2, wup_ref[s])
            up0_ref[s] = up0.astype(BF16)
            ext_ref[s, FFN_HALO:, :] = up0
        first = FFN_HALO - (FFN_CONV_WIDTH - 1)

        def conv(s):
            block = ext_ref[s]
            acc = fb_ref[s, :, :UP_SLAB] + fw_ref[s, FFN_CONV_WIDTH - 1:FFN_CONV_WIDTH, :UP_SLAB] * block[FFN_HALO:]
            for back in range(1, FFN_CONV_WIDTH):
                k = FFN_CONV_WIDTH - 1 - back
                acc = acc + fw_ref[s, k:k + 1, :UP_SLAB] * pltpu.roll(block, back, 0)[FFN_HALO:]
            return acc

        h2 = h1_t
        for s in range(N_ACT_SLAB):
            gate = conv(s)
            val = conv(s + N_ACT_SLAB)
            sg = _sigmoid(gate)
            silu = gate * sg
            act = (silu * val).astype(BF16)
            act_ref[s] = act
            da_ref[s] = (val * sg * (1.0 + gate * (1.0 - sg))).astype(BF16)
            db_ref[s] = silu.astype(BF16)
            h2 = h2 + _dot(act, wdn_ref[s])
        ext_ref[:, 0:FFN_HALO, :] = ext_ref[:, ROW_TILE:ROW_TILE + FFN_HALO, :]

        gf_t = gf_ref[...]
        y = _rms_fwd(h2, gf_t)
        diff = jnp.where(i >= 1, y - t_ref[...], 0.0)
        tile_loss = 0.5 * jnp.sum(jnp.sum(diff * diff, axis=-1, keepdims=True), axis=0, keepdims=True) / D_MODEL
        dh2, dgf = _rms_bwd(diff / D_MODEL, h2, gf_t)
        dh2_ref[...] = dh2
        _accumulate(loss_ref, i == 0, jnp.broadcast_to(tile_loss, loss_ref.shape))
        _accumulate(dgf_ref, i == 0, dgf)

    act_like = jax.ShapeDtypeStruct((N_ACT_SLAB, n_rows, UP_SLAB), BF16)
    out_shapes = [
        jax.ShapeDtypeStruct((n_rows, D_MODEL), BF16),
        jax.ShapeDtypeStruct((N_DEV, n_rows, UP_SLAB), BF16),
        act_like, act_like, act_like,
        jax.ShapeDtypeStruct((n_rows, D_MODEL), F32),
        jax.ShapeDtypeStruct((8, 128), F32),
        jax.ShapeDtypeStruct((1, D_MODEL), F32),
    ]
    whole = [g2, w_up, fw, fb, w_down, gf]
    return pl.pallas_call(
        body, name="fwd_ffn", grid=(nt,),
        in_specs=[_tile_spec(h1.shape), _real_spec(D_MODEL)] + [_whole_spec(a.shape) for a in whole],
        out_specs=[_tile_spec(s.shape) for s in out_shapes[:6]] + [_acc_spec(s.shape) for s in out_shapes[6:]],
        out_shape=out_shapes,
        scratch_shapes=[pltpu.VMEM((N_DEV, ROW_TILE + FFN_HALO, UP_SLAB), F32)],
        compiler_params=_params("arbitrary"),
    )(h1, target, *whole)


def _rope_tables(n_rows):
    pos = jnp.maximum(jnp.arange(n_rows, dtype=jnp.int32) - DEAD, 0)
    inv_freq = 1.0 / (ROPE_THETA ** (jnp.arange(0, QK_ROPE, 2, dtype=F32) / QK_ROPE))
    ang_t = inv_freq[:, None] * pos.astype(F32)[None, :]
    return jnp.cos(ang_t), jnp.sin(ang_t)


def _halo_after(shape, halo, n_rows):
    last = n_rows // halo - 1
    step = ROW_TILE // halo
    if len(shape) == 2:
        return pl.BlockSpec((halo, shape[1]), lambda i: (jnp.minimum((i + 1) * step, last), 0))
    return pl.BlockSpec((shape[0], halo, shape[2]), lambda i: (0, jnp.minimum((i + 1) * step, last), 0))


def _halo_before(shape, halo):
    step = ROW_TILE // halo
    if len(shape) == 2:
        return pl.BlockSpec((halo, shape[1]), lambda i: (jnp.maximum(i * step - 1, 0), 0))
    return pl.BlockSpec((shape[0], halo, shape[2]), lambda i: (0, jnp.maximum(i * step - 1, 0), 0))


def _bwd_ffn_act(dh2, da, db, w_down, n_rows):
    nt = n_rows // ROW_TILE

    def body(dh2_ref, da_ref, db_ref, wdn_ref, dup_ref, dfb_ref):
        i = pl.program_id(0)

        @pl.when(i == 0)
        def _():
            dfb_ref[...] = jnp.zeros_like(dfb_ref)

        dh2_b = dh2_ref[...].astype(BF16)
        for s in range(N_ACT_SLAB):
            d_act = _dot_nt(dh2_b, wdn_ref[s])
            d_gate = d_act * da_ref[s].astype(F32)
            d_val = d_act * db_ref[s].astype(F32)
            dup_ref[s] = d_gate.astype(BF16)
            dup_ref[s + N_ACT_SLAB] = d_val.astype(BF16)
            dfb_ref[s] += jnp.sum(d_gate, axis=0, keepdims=True)
            dfb_ref[s + N_ACT_SLAB] += jnp.sum(d_val, axis=0, keepdims=True)

    out_shapes = [jax.ShapeDtypeStruct((N_DEV, n_rows, UP_SLAB), BF16), jax.ShapeDtypeStruct((N_DEV, 1, UP_SLAB), F32)]
    return pl.pallas_call(
        body, name="bwd_ffn_act", grid=(nt,),
        in_specs=[_tile_spec(dh2.shape), _tile_spec(da.shape), _tile_spec(db.shape), _whole_spec(w_down.shape)],
        out_specs=[_tile_spec(out_shapes[0].shape), _acc_spec(out_shapes[1].shape)],
        out_shape=out_shapes,
        compiler_params=_params("arbitrary"),
    )(dh2, da, db, w_down)


def _bwd_ffn_up(dup, up0, h1, dh2, g2, w_up, fw, n_rows):
    nt = n_rows // ROW_TILE
    last_tap = FFN_CONV_WIDTH - 1
    ext_rows = ROW_TILE + FFN_HALO

    def body(dup_ref, dnext_ref, up0_ref, h1_ref, dh2_ref, g2_ref, wup_ref, fw_ref,
             dup0_ref, dh1_ref, dfw_ref, dg2_ref):
        i = pl.program_id(0)

        @pl.when(i == 0)
        def _():
            dfw_ref[...] = jnp.zeros_like(dfw_ref)

        live = _row_ids(i, ROW_TILE) >= DEAD
        dn2 = jnp.zeros((ROW_TILE, D_MODEL), F32)
        for s in range(N_DEV):
            d = dup_ref[s].astype(F32)
            block = jnp.concatenate([d, jnp.where(i == nt - 1, 0.0, dnext_ref[s].astype(F32))], axis=0)
            u = up0_ref[s].astype(F32)
            dup0 = fw_ref[s, last_tap:last_tap + 1, :UP_SLAB] * d
            dfw_ref[s, last_tap:last_tap + 1, :UP_SLAB] += jnp.sum(d * u, axis=0, keepdims=True)
            for ahead in range(1, FFN_CONV_WIDTH):
                k = last_tap - ahead
                shifted = pltpu.roll(block, ext_rows - ahead, 0)[:ROW_TILE]
                dup0 = dup0 + fw_ref[s, k:k + 1, :UP_SLAB] * shifted
                dfw_ref[s, k:k + 1, :UP_SLAB] += jnp.sum(shifted * u, axis=0, keepdims=True)
            dup0_b = jnp.where(live, dup0, 0.0).astype(BF16)
            dup0_ref[s] = dup0_b
            dn2 = dn2 + _dot(dup0_b, wup_ref[s])
        dx, dg2 = _rms_bwd(dn2, h1_ref[...], g2_ref[...])
        dh1_ref[...] = dh2_ref[...] + dx
        _accumulate(dg2_ref, i == 0, dg2)

    out_shapes = [
        jax.ShapeDtypeStruct((N_DEV, n_rows, UP_SLAB), BF16),
        jax.ShapeDtypeStruct((n_rows, D_MODEL), F32),
        jax.ShapeDtypeStruct((N_DEV, FFN_CONV_WIDTH, UP_PAD), F32),
        jax.ShapeDtypeStruct((1, D_MODEL), F32),
    ]
    return pl.pallas_call(
        body, name="bwd_ffn_up", grid=(nt,),
        in_specs=[_tile_spec(dup.shape), _halo_after(dup.shape, FFN_HALO, n_rows), _tile_spec(up0.shape),
                  _tile_spec(h1.shape), _tile_spec(dh2.shape),
                  _whole_spec(g2.shape), _whole_spec(w_up.shape), _whole_spec(fw.shape)],
        out_specs=[_tile_spec(s.shape) for s in out_shapes[:2]] + [_acc_spec(s.shape) for s in out_shapes[2:]],
        out_shape=out_shapes,
        compiler_params=_params("arbitrary"),
    )(dup, dup, up0, h1, dh2, g2, w_up, fw)


def _bwd_out(dh1, o_t, u1, w_out, gb_col, ln_g, ln_b, ga, n_rows):
    nt = n_rows // ROW_TILE

    def body(dh1_ref, ot_ref, u1_ref, w_ref, gb_ref, lg_ref, lb_ref, ga_ref,
             dot_ref, delta_ref, du1_ref, dgb_ref, dga_ref, dlg_ref, dlb_ref, dcb_ref):
        i = pl.program_id(0)
        dh1_b = dh1_ref[...].astype(BF16)
        o_t = _heads_to_rows(ot_ref)
        gb = gb_ref[...]
        r = lax.rsqrt(jnp.mean(o_t * o_t, axis=0, keepdims=True) + EPS)
        dmix_bt = _dot_nt(w_ref[D_CONV:, :], dh1_b)
        wgt = dmix_bt * gb
        do_t = r * wgt - o_t * (r * r * r) * jnp.mean(wgt * o_t, axis=0, keepdims=True)
        dgb = jnp.sum(dmix_bt * o_t * r, axis=1, keepdims=True)
        for h in range(N_HEADS):
            do_h = do_t[h * V_HEAD:(h + 1) * V_HEAD]
            dot_ref[h] = do_h.astype(BF16)
            delta_ref[h] = jnp.sum(do_h * ot_ref[h], axis=0, keepdims=True)
        lg = lg_ref[...]
        xh, u2, u3, rstd = _conv_chain(u1_ref[...], lg, lb_ref[...])
        du3, dga = _rms_bwd(_dot_nt(dh1_b, w_ref[:D_CONV, :]), u3, ga_ref[...])
        sg = _sigmoid(u2)
        du2 = du3 * sg * (1.0 + u2 * (1.0 - sg))
        dxh = du2 * lg
        du1 = rstd * (dxh - jnp.mean(dxh, axis=-1, keepdims=True) - xh * jnp.mean(dxh * xh, axis=-1, keepdims=True))
        du1_ref[...] = du1
        first = i == 0
        _accumulate(dgb_ref, first, dgb)
        _accumulate(dga_ref, first, dga)
        _accumulate(dlg_ref, first, jnp.sum(du2 * xh, axis=0, keepdims=True))
        _accumulate(dlb_ref, first, jnp.sum(du2, axis=0, keepdims=True))
        _accumulate(dcb_ref, first, jnp.sum(du1, axis=0, keepdims=True))

    out_shapes = [
        jax.ShapeDtypeStruct((N_HEADS, V_HEAD, n_rows), BF16),
        jax.ShapeDtypeStruct((N_HEADS, 1, n_rows), F32),
        jax.ShapeDtypeStruct((n_rows, D_CONV), F32),
        jax.ShapeDtypeStruct((D_ATTN, 1), F32),
    ] + [jax.ShapeDtypeStruct((1, D_CONV), F32)] * 4
    whole = [w_out, gb_col, ln_g, ln_b, ga]
    return pl.pallas_call(
        body, name="bwd_out", grid=(nt,),
        in_specs=[_tile_spec(dh1.shape), _lane_tile(o_t.shape), _tile_spec(u1.shape)] + [_whole_spec(a.shape) for a in whole],
        out_specs=[_lane_tile(out_shapes[0].shape), _lane_tile(out_shapes[1].shape), _tile_spec(out_shapes[2].shape)]
        + [_acc_spec(s.shape) for s in out_shapes[3:]],
        out_shape=out_shapes,
        compiler_params=_params("arbitrary"),
    )(dh1, o_t, u1, *whole)


ATTN_BWD_HEADS = 8


def _attn_bwd(q_t, k, v, do_t, lse, delta, n_rows):
    nt = n_rows // ROW_TILE
    hp = ATTN_BWD_HEADS

    def body(k_ref, v_ref, qt_ref, dot_ref, lse_ref, delta_ref, dqt_ref, dk_ref, dv_ref):
        j = pl.program_id(1)

        @pl.when(j == 0)
        def _():
            dqt_ref[...] = jnp.zeros_like(dqt_ref)

        k_ts = [k_ref[h] for h in range(hp)]
        v_ts = [v_ref[h] for h in range(hp)]

        def make_step(masked, tiles, first=0):
            def step(t, carry):
                tiles_of_step = []
                for u in range(tiles):
                    i = first + tiles * t + u
                    cols = pl.ds(pl.multiple_of(i * ROW_TILE, ROW_TILE), ROW_TILE)
                    q_is = [qt_ref[h, :, cols] for h in range(hp)]
                    do_is = [dot_ref[h, :, cols] for h in range(hp)]
                    scores = [_dot(k_ts[h], q_is[h]) for h in range(hp)]
                    dps = [_dot(v_ts[h], do_is[h]) for h in range(hp)]
                    tiles_of_step.append((i, cols, q_is, do_is, scores, dps))
                for i, cols, q_is, do_is, scores, dps in tiles_of_step:
                    visible = _visible(i, j) if masked else None
                    probs, dss = [], []
                    for h in range(hp):
                        s = jnp.where(visible, scores[h], NEG) if masked else scores[h]
                        p = jnp.exp2(s - lse_ref[h, :, cols])
                        probs.append(p.astype(BF16))
                        dss.append((p * (dps[h] - delta_ref[h, :, cols])).astype(BF16))
                    for h in range(hp):
                        dv_ref[h] += _dot_nt(probs[h], do_is[h])
                        dk_ref[h] += _dot_nt(dss[h], q_is[h])
                        dqt_ref[h, :, cols] += _dot_tn(k_ts[h], dss[h])
                return carry
            return step

        dk_ref[...] = jnp.zeros_like(dk_ref)
        dv_ref[...] = jnp.zeros_like(dv_ref)
        make_step(True, 1)(j, 0)
        lax.fori_loop(jnp.where(j == 0, j + 1, nt), nt, make_step(True, 1), 0)
        unmasked = jnp.where(j == 0, 0, nt - 1 - j)
        quads = lax.shift_right_logical(unmasked, 2)
        pairs = jnp.bitwise_and(lax.shift_right_logical(unmasked, 1), 1)
        lax.fori_loop(0, quads, make_step(False, 4, first=j + 1), 0)
        lax.fori_loop(0, pairs, make_step(False, 2, first=j + 1 + 4 * quads), 0)
        lax.fori_loop(jnp.where(j == 0, nt, j + 1 + 4 * quads + 2 * pairs), nt, make_step(False, 1), 0)
        dk_ref[...] = dk_ref[...] * _LN2

    key_tile = lambda w: pl.BlockSpec((hp, ROW_TILE, w), lambda g, j: (g, j, 0))
    all_cols = lambda w: pl.BlockSpec((hp, w, n_rows), lambda g, j: (g, 0, 0))
    resident = lambda w: pl.BlockSpec((hp, w, n_rows), lambda g, j: (g, 0, 0), pipeline_mode=pl.Buffered(1))
    out_shapes = [
        jax.ShapeDtypeStruct((N_HEADS, QK_DIM, n_rows), F32),
        jax.ShapeDtypeStruct((N_HEADS, n_rows, QK_DIM), F32),
        jax.ShapeDtypeStruct((N_HEADS, n_rows, V_HEAD), F32),
    ]
    return pl.pallas_call(
        body, name="attn_bwd", grid=(N_HEADS // hp, nt),
        in_specs=[key_tile(QK_DIM), key_tile(V_HEAD), resident(QK_DIM), resident(V_HEAD), resident(1), resident(1)],
        out_specs=[all_cols(QK_DIM), key_tile(QK_DIM), key_tile(V_HEAD)],
        out_shape=out_shapes,
        compiler_params=_params("parallel", "arbitrary"),
    )(k, v, q_t, do_t, lse, delta)


def _bwd_qkv(dq_t, dk, dv, cq, ckv, gq, gkv, wq_t, w_ukv, cos, sin, cos_t, sin_t, n_rows):
    nt = n_rows // ROW_TILE

    def body(dqt_ref, dk_ref, dv_ref, cq_ref, ckv_ref, gq_ref, gkv_ref, wqt_ref, wkv_ref, cos_ref, sin_ref,
             cost_ref, sint_ref, dqraw_ref, dkv_ref, dcq_ref, dckv_ref, dkr_ref, dgq_ref, dgkv_ref):
        i = pl.program_id(0)
        cos_rows, sin_rows = cost_ref[...], sint_ref[...]
        dcqn = jnp.zeros((ROW_TILE, Q_LORA), F32)
        dckvn = jnp.zeros((ROW_TILE, KV_LORA), F32)
        dk_rot = jnp.zeros((ROW_TILE, QK_ROPE), F32)
        for h in range(N_HEADS):
            dq_h, dk_h = dqt_ref[h] * QK_DIM ** -0.5, dk_ref[h]
            dq_raw = jnp.concatenate(
                [dq_h[:QK_NOPE], _rope_rows_t(dq_h[QK_NOPE:], cos_rows, sin_rows)], axis=0).astype(BF16)
            dqraw_ref[h] = dq_raw
            dcqn = dcqn + _dot_tn(dq_raw, wqt_ref[h])
            dkv = jnp.concatenate([dk_h[:, :QK_NOPE], dv_ref[h]], axis=-1).astype(BF16)
            dkv_ref[:, h * KV_HEAD:(h + 1) * KV_HEAD] = dkv
            dckvn = dckvn + _dot_nt(dkv, wkv_ref[h])
            dk_rot = dk_rot + dk_h[:, QK_NOPE:]
        dkr_ref[...] = _rope_t(dk_rot, cos_ref[...], sin_ref[...]).astype(BF16)
        dcq, dgq = _rms_bwd(dcqn, cq_ref[...], gq_ref[...])
        dckv, dgkv = _rms_bwd(dckvn, ckv_ref[...], gkv_ref[...])
        dcq_ref[...] = dcq.astype(BF16)
        dckv_ref[...] = dckv.astype(BF16)
        _accumulate(dgq_ref, i == 0, dgq)
        _accumulate(dgkv_ref, i == 0, dgkv)

    out_shapes = [
        jax.ShapeDtypeStruct((N_HEADS, QK_DIM, n_rows), BF16),
        jax.ShapeDtypeStruct((n_rows, N_HEADS * KV_HEAD), BF16),
        jax.ShapeDtypeStruct((n_rows, Q_LORA), BF16),
        jax.ShapeDtypeStruct((n_rows, KV_LORA), BF16),
        jax.ShapeDtypeStruct((n_rows, QK_ROPE), BF16),
        jax.ShapeDtypeStruct((1, Q_LORA), F32),
        jax.ShapeDtypeStruct((1, KV_LORA), F32),
    ]
    tiles = [dk, dv, cq, ckv]
    whole = [gq, gkv, wq_t, w_ukv]
    return pl.pallas_call(
        body, name="bwd_qkv", grid=(nt,),
        in_specs=[_lane_tile(dq_t.shape)] + [_tile_spec(a.shape) for a in tiles] + [_whole_spec(a.shape) for a in whole]
        + [_tile_spec(cos.shape), _tile_spec(sin.shape), _lane_tile(cos_t.shape), _lane_tile(sin_t.shape)],
        out_specs=[_lane_tile(out_shapes[0].shape)] + [_tile_spec(s.shape) for s in out_shapes[1:5]]
        + [_acc_spec(s.shape) for s in out_shapes[5:]],
        out_shape=out_shapes,
        compiler_params=_params("arbitrary"),
    )(dq_t, *tiles, *whole, cos, sin, cos_t, sin_t)


def _bwd_conv(du1, ag, conv_w, dcq, dckv, dkr, n_rows):
    nt = n_rows // ROW_TILE

    last_tap = CONV_WIDTH - 1

    def body(du1_ref, dnext_ref, ag_ref, w_ref, dcq_ref, dckv_ref, dkr_ref, dz_ref, dw_ref,
             dext_ref, uext_ref, conv_ref, sums_ref):
        i = pl.program_id(0)

        @pl.when(i == 0)
        def _():
            sums_ref[...] = jnp.zeros_like(sums_ref)

        _to_planes(dext_ref, (), slice(0, ROW_TILE), du1_ref[...])
        _to_planes(dext_ref, (), slice(ROW_TILE, None), jnp.where(i == nt - 1, 0.0, dnext_ref[...]))
        ag_t = ag_ref[...]
        live = _row_ids(i, ROW_TILE) >= DEAD
        sg = _sigmoid(ag_t[:, D_CONV:])
        _to_planes(uext_ref, (), slice(None), jnp.where(live, ag_t[:, :D_CONV] * sg, 0.0))
        for c in range(CONV_PLANES):
            taps = w_ref[:, c * _LANES:(c + 1) * _LANES]
            for p in range(PHASES):
                u = uext_ref[c, _phase(p), :]
                acc = jnp.zeros((PHASE_ROWS, _LANES), F32)
                for k in range(CONV_WIDTH):
                    shifted = dext_ref[c, _phase(p + last_tap - k), :]
                    acc = acc + taps[k:k + 1, :] * shifted
                    sums_ref[c, k] += shifted * u
                conv_ref[c, _phase(p), :] = acc
        du0 = jnp.where(live, _from_planes(conv_ref, (), D_CONV), 0.0)
        da = du0 * sg
        dgate = du0 * ag_t[:, :D_CONV] * sg * (1.0 - sg)
        dz_ref[...] = jnp.concatenate(
            [da.astype(BF16), dgate.astype(BF16), dcq_ref[...], dckv_ref[...], dkr_ref[...]], axis=-1)

        @pl.when(i == nt - 1)
        def _():
            for c in range(CONV_PLANES):
                for k in range(CONV_WIDTH):
                    dw_ref[k:k + 1, c * _LANES:(c + 1) * _LANES] = jnp.sum(sums_ref[c, k], axis=0, keepdims=True)

    out_shapes = [jax.ShapeDtypeStruct((n_rows, D_IN), BF16), jax.ShapeDtypeStruct((CONV_WIDTH, D_CONV), F32)]
    return pl.pallas_call(
        body, name="bwd_conv", grid=(nt,),
        in_specs=[_tile_spec(du1.shape), _halo_after(du1.shape, CONV_HALO, n_rows), _tile_spec(ag.shape),
                  _whole_spec(conv_w.shape), _tile_spec(dcq.shape), _tile_spec(dckv.shape), _tile_spec(dkr.shape)],
        out_specs=[_tile_spec(out_shapes[0].shape), _acc_spec(out_shapes[1].shape)],
        out_shape=out_shapes,
        scratch_shapes=[pltpu.VMEM((CONV_PLANES, ROW_TILE + CONV_HALO, _LANES), F32),
                        pltpu.VMEM((CONV_PLANES, ROW_TILE, _LANES), F32), pltpu.VMEM((CONV_PLANES, ROW_TILE, _LANES), F32),
                        pltpu.VMEM((CONV_PLANES, CONV_WIDTH, PHASE_ROWS, _LANES), F32)],
        compiler_params=_params("arbitrary"),
    )(du1, du1, ag, conv_w, dcq, dckv, dkr)


def _bwd_in(dz, x, meta_pad, dh1, g1, w_in, n_rows):
    nt = n_rows // ROW_TILE

    def body(dz_ref, x_ref, meta_ref, dh1_ref, g_ref, w_ref, gx_ref, gmeta_ref, dg1_ref):
        i = pl.program_id(0)
        h0 = jnp.where(i == 0, meta_ref[...], x_ref[...])
        dx, dg1 = _rms_bwd(_dot(dz_ref[...], w_ref[...]), h0, g_ref[...])
        dh0 = dh1_ref[...] + dx
        gx_ref[...] = dh0

        @pl.when(i == 0)
        def _():
            gmeta_ref[...] = dh0

        _accumulate(dg1_ref, i == 0, dg1)

    out_shapes = [
        jax.ShapeDtypeStruct((n_rows - ROW_TILE, D_MODEL), F32),
        jax.ShapeDtypeStruct((ROW_TILE, D_MODEL), F32),
        jax.ShapeDtypeStruct((1, D_MODEL), F32),
    ]
    return pl.pallas_call(
        body, name="bwd_in", grid=(nt,),
        in_specs=[_tile_spec(dz.shape), _real_spec(D_MODEL), _whole_spec(meta_pad.shape), _tile_spec(dh1.shape),
                  _whole_spec(g1.shape), _whole_spec(w_in.shape)],
        out_specs=[_real_spec(D_MODEL), _acc_spec(out_shapes[1].shape), _acc_spec(out_shapes[2].shape)],
        out_shape=out_shapes,
        compiler_params=_params("arbitrary"),
    )(dz, x, meta_pad, dh1, g1, w_in)


def _contraction_tile(n_rows):
    return next(t for t in range(n_rows // 2 // _LANES * _LANES, 0, -_LANES) if n_rows % t == 0)


def _weight_grad(a, b, name, a_transposed=False):
    groups = max(a.shape[0] if a.ndim == 3 else 1, b.shape[0] if b.ndim == 3 else 1)
    n_rows, n = b.shape[-2], b.shape[-1]
    m = a.shape[-2] if a_transposed else a.shape[-1]
    kt = _contraction_tile(n_rows)
    steps = n_rows // kt

    def body(a_ref, b_ref, out_ref, acc_ref):
        i = pl.program_id(1)
        a_t, b_t = a_ref[...].astype(BF16), b_ref[...].astype(BF16)
        part = _dot(a_t, b_t) if a_transposed else _dot_tn(a_t, b_t)
        _accumulate(acc_ref, i == 0, part)

        @pl.when(i == steps - 1)
        def _():
            out_ref[...] = acc_ref[...].astype(out_ref.dtype)

    def spec(arr, rows_last):
        block = (arr.shape[-2], kt) if rows_last else (kt, arr.shape[-1])
        at = (lambda i: (0, i)) if rows_last else (lambda i: (i, 0))
        if arr.ndim == 3:
            return pl.BlockSpec((None,) + block, lambda g, i: (g,) + at(i))
        return pl.BlockSpec(block, lambda g, i: at(i))

    return pl.pallas_call(
        body, name=name, grid=(groups, steps),
        in_specs=[spec(a, a_transposed), spec(b, False)],
        out_specs=pl.BlockSpec((None, m, n), lambda g, i: (g, 0, 0)),
        out_shape=jax.ShapeDtypeStruct((groups, m, n), BF16),
        scratch_shapes=[pltpu.VMEM((m, n), F32)],
        compiler_params=_params("parallel", "arbitrary"),
    )(a, b)


def _my_index():
    return 4 * lax.axis_index("x") + 2 * lax.axis_index("y") + lax.axis_index("c")


def _peer(k):
    flip = lambda v, bit: 1 - v if bit else v
    px = flip(lax.axis_index("x"), k & 4)
    py = flip(lax.axis_index("y"), k & 2)
    pc = flip(lax.axis_index("c"), k & 1)
    return (px, py, pc), 4 * px + 2 * py + pc


def _all_gather(shards, dtypes):
    n = len(shards)
    sibling, chips = 1, (2, 4, 6)

    def body(*refs):
        ins, outs, stages = refs[:n], refs[n:2 * n], refs[2 * n:3 * n]
        send_sems, recv_sems, local_sems = refs[3 * n:]
        me = _my_index()
        for a in range(n):
            stages[a][...] = ins[a][...].astype(stages[a].dtype)
        local = [pltpu.make_async_copy(stages[a], outs[a].at[me], local_sems.at[a]) for a in range(n)]
        for cp in local:
            cp.start()

        def copy(a, k, src, slot, to):
            return pltpu.make_async_remote_copy(
                src_ref=src, dst_ref=outs[a].at[slot], send_sem=send_sems.at[a, k - 1],
                recv_sem=recv_sems.at[a, k - 1], device_id=_peer(to)[0], device_id_type=MESH)

        def own(a, k):
            return copy(a, k, stages[a], me, k)

        def passed(a, k):
            slot = _peer(k)[1]
            return copy(a, k ^ sibling, outs[a].at[slot], slot, sibling)

        def arrival(a, k):
            return copy(a, k, stages[a], _peer(k)[1], k)

        for k in (sibling,) + chips:
            for a in range(n):
                own(a, k).start()
        for k in chips:
            for a in range(n):
                arrival(a, k).wait_recv()
                passed(a, k).start()
        for a in range(n):
            arrival(a, sibling).wait_recv()
            for k in chips:
                arrival(a, k ^ sibling).wait_recv()
        for a in range(n):
            for k in (sibling,) + chips:
                own(a, k).wait_send()
            for k in chips:
                passed(a, k).wait_send()
        for cp in local:
            cp.wait()

    return pl.pallas_call(
        body, name="gather_weights",
        in_specs=[pl.BlockSpec(memory_space=pltpu.VMEM)] * n,
        out_specs=[pl.BlockSpec(memory_space=pl.ANY)] * n,
        out_shape=[jax.ShapeDtypeStruct((N_DEV,) + s.shape, dt) for s, dt in zip(shards, dtypes)],
        scratch_shapes=[pltpu.VMEM(s.shape, dt) for s, dt in zip(shards, dtypes)]
        + [pltpu.SemaphoreType.DMA((n, N_DEV - 1)), pltpu.SemaphoreType.DMA((n, N_DEV - 1)), pltpu.SemaphoreType.DMA((n,))],
        compiler_params=pltpu.CompilerParams(vmem_limit_bytes=VMEM_LIMIT),
    )(*shards)


def _exchange(parts, whole):
    n = len(parts)

    def body(*refs):
        ins, outs = refs[:n], refs[n:2 * n]
        send_sems, recv_sems, local_sems = refs[2 * n:]
        me = _my_index()

        def src(a, slab):
            return ins[a] if whole[a] else ins[a].at[slab]

        local = [pltpu.make_async_copy(src(a, me), outs[a].at[me], local_sems.at[a]) for a in range(n)]
        for cp in local:
            cp.start()

        def copy(a, k, slab, slot):
            peer, _ = _peer(k)
            return pltpu.make_async_remote_copy(
                src_ref=src(a, slab), dst_ref=outs[a].at[slot], send_sem=send_sems.at[a, k - 1],
                recv_sem=recv_sems.at[a, k - 1], device_id=peer, device_id_type=MESH)

        for k in range(1, N_DEV):
            for a in range(n):
                copy(a, k, _peer(k)[1], me).start()
        for k in range(1, N_DEV):
            for a in range(n):
                copy(a, k, _peer(k)[1], _peer(k)[1]).wait()
        for cp in local:
            cp.wait()

    return pl.pallas_call(
        body, name="exchange_grads",
        in_specs=[pl.BlockSpec(memory_space=pl.ANY)] * n,
        out_specs=[pl.BlockSpec(memory_space=pl.ANY)] * n,
        out_shape=[jax.ShapeDtypeStruct(((N_DEV,) + p.shape) if w else p.shape, p.dtype) for p, w in zip(parts, whole)],
        scratch_shapes=[pltpu.SemaphoreType.DMA((n, N_DEV - 1)), pltpu.SemaphoreType.DMA((n, N_DEV - 1)),
                        pltpu.SemaphoreType.DMA((n,))],
    )(*parts)


def _sequencer_exchange(parts, whole, name, collective_id):
    n = len(parts)
    srcs = [jax.new_ref(p, memory_space=pltpu.MemorySpace.HBM) for p in parts]
    lands = [jax.empty_ref(jax.ShapeDtypeStruct(((N_DEV,) + p.shape) if w else p.shape, p.dtype),
                           memory_space=pltpu.MemorySpace.HBM) for p, w in zip(parts, whole)]

    @pl.kernel(mesh=plsc.ScalarSubcoreMesh(axis_name="sequencer", num_cores=1), name=name,
               scratch_types=(pltpu.SemaphoreType.DMA((n, N_DEV - 1)), pltpu.SemaphoreType.DMA((n, N_DEV - 1)),
                              pltpu.SemaphoreType.DMA((n,))),
               compiler_params=pltpu.CompilerParams(collective_id=collective_id))
    def launch(send_sems, recv_sems, local_sems):
        barrier = pltpu.get_barrier_semaphore()
        for k in range(1, N_DEV):
            pl.semaphore_signal(barrier, inc=1, device_id=_peer(k)[0], device_id_type=MESH)
        pl.semaphore_wait(barrier, N_DEV - 1)
        me = _my_index()

        def src(a, slab):
            return srcs[a] if whole[a] else srcs[a].at[slab]

        local = [pltpu.make_async_copy(src(a, me), lands[a].at[me], local_sems.at[a]) for a in range(n)]
        for cp in local:
            cp.start()

        def copy(a, k, slab, slot):
            return pltpu.make_async_remote_copy(
                src_ref=src(a, slab), dst_ref=lands[a].at[slot], send_sem=send_sems.at[a, k - 1],
                recv_sem=recv_sems.at[a, k - 1], device_id=_peer(k)[0], device_id_type=MESH)

        for k in range(1, N_DEV):
            for a in range(n):
                copy(a, k, _peer(k)[1], me).start()
        for k in range(1, N_DEV):
            for a in range(n):
                copy(a, k, _peer(k)[1], _peer(k)[1]).wait()
        for cp in local:
            cp.wait()

    launch()
    return [land[...] for land in lands]


def _row_block(rows):
    if rows <= ROW_TILE:
        return rows
    return next(rb for rb in range(ROW_TILE, 0, -16) if rows % rb == 0)


def _adamw(landing, w, m, v, name):
    rows, cols = w.shape
    rb = _row_block(rows)

    def body(l_ref, w_ref, m_ref, v_ref, g_ref, d_ref, m2_ref, v2_ref):
        g = l_ref[0].astype(F32)
        for p in range(1, N_DEV):
            g = g + l_ref[p].astype(F32)
        g_ref[...] = g
        d_ref[...], m2_ref[...], v2_ref[...] = _adamw_step(g, w_ref[...], m_ref[...], v_ref[...])

    flat = pl.BlockSpec((rb, cols), lambda i: (i, 0))
    return pl.pallas_call(
        body, name=name, grid=(rows // rb,),
        in_specs=[pl.BlockSpec((N_DEV, rb, cols), lambda i: (0, i, 0)), flat, flat, flat],
        out_specs=[flat] * 4,
        out_shape=[jax.ShapeDtypeStruct((rows, cols), F32)] * 4,
        compiler_params=_params("parallel"),
    )(landing, w, m, v)


def _adamw_step(g, w, m, v):
    m2 = ADAM_B1 * m + (1.0 - ADAM_B1) * g
    v2 = ADAM_B2 * v + (1.0 - ADAM_B2) * (g * g)
    m_hat = m2 / (1.0 - ADAM_B1 ** ADAM_STEP)
    v_hat = v2 / (1.0 - ADAM_B2 ** ADAM_STEP)
    return -ADAM_LR * (m_hat / (jnp.sqrt(v_hat) + ADAM_EPS) + ADAM_WD * w), m2, v2


_REPLICATED = (
    ("mix_norm_g", D_MODEL), ("q_norm_g", Q_LORA), ("kv_norm_g", KV_LORA), ("conv_b", D_CONV), ("conv_ln_g", D_CONV),
    ("conv_ln_b", D_CONV), ("conv_out_g", D_CONV), ("attn_out_g", D_CONV), ("ffn_norm_g", D_MODEL),
    ("ffn_conv_b", D_UP), ("final_norm_g", D_MODEL),
)
_REPLICATED_WIDTH = sum(size for _, size in _REPLICATED) + _LANES

_WEIGHT_ORDER = (
    "meta_tokens", "mix_norm_g", "w_in", "q_norm_g", "w_uq", "kv_norm_g", "w_ukv", "conv_w", "conv_b", "conv_ln_g",
    "conv_ln_b", "conv_out_g", "attn_out_g", "w_out", "ffn_norm_g", "w_ffn_up", "ffn_conv_w", "ffn_conv_b",
    "w_ffn_down", "final_norm_g",
)


def _pack_replicated(grads, loss):
    rows = [grads[name].reshape(1, size) for name, size in _REPLICATED]
    return jnp.concatenate(rows + [jnp.broadcast_to(loss.reshape(1, 1), (1, _LANES))], axis=-1)


def _adamw_replicated(landing, weights, moments_m, moments_v):
    n = len(_REPLICATED)

    def body(*refs):
        l_ref, ins, outs = refs[0], refs[1:1 + 3 * n], refs[1 + 3 * n:]
        total = l_ref[0]
        for p in range(1, N_DEV):
            total = total + l_ref[p]
        at = 0
        for a, (_, size) in enumerate(_REPLICATED):
            g = total[:, at:at + size]
            w_ref, m_ref, v_ref = ins[3 * a:3 * a + 3]
            g_ref, d_ref, m2_ref, v2_ref = outs[4 * a:4 * a + 4]
            g_ref[...] = g
            d_ref[...], m2_ref[...], v2_ref[...] = _adamw_step(g, w_ref[...], m_ref[...], v_ref[...])
            at += size
        outs[-1][...] = total[:, at:at + _LANES]

    operands, out_shapes = [], []
    for name, size in _REPLICATED:
        operands += [weights[name].reshape(1, size), moments_m[name].reshape(1, size), moments_v[name].reshape(1, size)]
        out_shapes += [jax.ShapeDtypeStruct((1, size), F32)] * 4
    out_shapes.append(jax.ShapeDtypeStruct((1, _LANES), F32))
    outs = pl.pallas_call(body, name="adamw_replicated", out_shape=out_shapes)(landing, *operands)
    return outs[-1][0, 0], {name: outs[4 * a:4 * a + 4] for a, (name, _) in enumerate(_REPLICATED)}


def _pad_rows(a, rows):
    return jnp.pad(a, ((0, rows - a.shape[0]), (0, 0)))


def _slabs(a):
    r, c = a.shape
    return a.reshape(r, N_DEV, c // N_DEV).transpose(1, 0, 2)


def _unslab(a):
    g, r, c = a.shape
    return a.transpose(1, 0, 2).reshape(r, g * c)


def _local_step(x, target, w, n_rows, ffn_weights, send_grads):
    cos_t, sin_t = _rope_tables(n_rows)
    cos, sin = cos_t.T, sin_t.T
    meta_pad, g1, gf = w["meta_pad"], w["mix_norm_g"], w["final_norm_g"]
    gq, gkv, gb_col = w["q_norm_g"], w["kv_norm_g"], w["attn_out_g"].reshape(D_ATTN, 1)
    nb, ag, cq, ckv, kr = _fwd_in(x, meta_pad, g1, w["w_in"], n_rows)
    mix_a, u1 = _fwd_conv(ag, w["conv_w"], w["conv_b"], w["conv_ln_g"], w["conv_ln_b"], w["conv_out_g"], n_rows)
    q_t, k, v, v_t, cqn, ckvn = _fwd_qkv(cq, ckv, kr, gq, gkv, w["wq_t"], w["w_ukv"], w["wv_t"], cos, sin, cos_t, sin_t, n_rows)
    o_t, lse = _attn_fwd(q_t, k, v_t, n_rows)
    w_out, w_up, w_down = ffn_weights()
    mix_bt, h1 = _fwd_out(x, meta_pad, mix_a, o_t, gb_col, w_out, n_rows)
    n2, up0, act, da, db, dh2, loss, dgf = _fwd_ffn(
        h1, target, w["ffn_norm_g"], w_up, w["fw"], w["fb"], w_down, gf, n_rows)

    dup, dfb = _bwd_ffn_act(dh2, da, db, w_down, n_rows)
    dup0, dh1, dfw, dg2 = _bwd_ffn_up(dup, up0, h1, dh2, w["ffn_norm_g"], w_up, w["fw"], n_rows)
    grad_w_out = jnp.concatenate([_weight_grad(mix_a, dh1, "grad_w_out_conv")[0],
                                  _weight_grad(mix_bt, dh1, "grad_w_out_attn", a_transposed=True)[0]], axis=0)
    stage0 = {
        "w_ffn_up": _weight_grad(dup0, n2, "grad_w_ffn_up"),
        "w_ffn_down": _weight_grad(act, dh2, "grad_w_ffn_down").reshape(N_DEV, D_FF // N_DEV, D_MODEL),
        "w_out": grad_w_out.reshape(N_DEV, D_MODEL // N_DEV, D_MODEL),
    }
    stage0, dh1 = lax.optimization_barrier((stage0, dh1))
    send_grads(0, stage0)
    do_t, delta, du1, dgb, dga, dlg, dlb, dcb = _bwd_out(
        dh1, o_t, u1, w_out, gb_col, w["conv_ln_g"], w["conv_ln_b"], w["conv_out_g"], n_rows)
    dq_t, dk, dv = _attn_bwd(q_t, k, v, do_t, lse, delta, n_rows)
    dqraw_t, dkv, dcq, dckv, dkr, dgq, dgkv = _bwd_qkv(
        dq_t, dk, dv, cq, ckv, gq, gkv, w["wq_t"], w["w_ukv"], cos, sin, cos_t, sin_t, n_rows)
    dz, dcw = _bwd_conv(du1, ag, w["conv_w"], dcq, dckv, dkr, n_rows)
    stage1 = {
        "w_in": _weight_grad(dz, nb, "grad_w_in")[0].reshape(N_DEV, D_IN // N_DEV, D_MODEL),
        "w_uq": _weight_grad(dqraw_t.reshape(N_HEADS * QK_DIM, n_rows), cqn, "grad_w_uq", a_transposed=True)[0].reshape(
            N_HEADS, QK_DIM, Q_LORA),
        "w_ukv": _slabs(_weight_grad(ckvn, dkv, "grad_w_ukv")[0]),
        "conv_w": _slabs(dcw),
        "ffn_conv_w": dfw[:, :, :UP_SLAB],
    }
    stage1, dz = lax.optimization_barrier((stage1, dz))
    send_grads(1, stage1)
    gx, gmeta, dg1 = _bwd_in(dz, x, meta_pad, dh1, g1, w["w_in"], n_rows)

    sharded = {"meta_tokens": _slabs(gmeta[DEAD:])}
    replicated = {
        "mix_norm_g": dg1, "q_norm_g": dgq, "kv_norm_g": dgkv, "conv_b": dcb, "conv_ln_g": dlg, "conv_ln_b": dlb,
        "conv_out_g": dga, "attn_out_g": dgb, "ffn_norm_g": dg2, "ffn_conv_b": dfb, "final_norm_g": dgf,
    }
    return loss[0, 0], gx, sharded, replicated


_SHARDED = (
    ("w_in", None, BF16), ("w_uq", None, BF16), ("w_ukv", None, BF16), ("w_out", None, BF16), ("w_ffn_up", None, BF16),
    ("w_ffn_down", None, BF16), ("conv_w", 32, F32), ("ffn_conv_w", 8, F32), ("meta_tokens", None, F32),
)
GATHER_LATE_ID = 3
EXCHANGE_STAGE_IDS = (4, 5)
_LATE_WEIGHTS = ("w_out", "w_ffn_up", "w_ffn_down")
_COLUMN_SHARDS = ("w_in", "w_uq", "w_ffn_up")


def kernel(x, meta_tokens, mix_norm_g, w_in, q_norm_g, w_uq, kv_norm_g, w_ukv, conv_w, conv_b, conv_ln_g, conv_ln_b, conv_out_g, attn_out_g, w_out, ffn_norm_g, w_ffn_up, ffn_conv_w, ffn_conv_b, w_ffn_down, final_norm_g, loss_target, m_meta_tokens, m_mix_norm_g, m_w_in, m_q_norm_g, m_w_uq, m_kv_norm_g, m_w_ukv, m_conv_w, m_conv_b, m_conv_ln_g, m_conv_ln_b, m_conv_out_g, m_attn_out_g, m_w_out, m_ffn_norm_g, m_w_ffn_up, m_ffn_conv_w, m_ffn_conv_b, m_w_ffn_down, m_final_norm_g, v_meta_tokens, v_mix_norm_g, v_w_in, v_q_norm_g, v_w_uq, v_kv_norm_g, v_w_ukv, v_conv_w, v_conv_b, v_conv_ln_g, v_conv_ln_b, v_conv_out_g, v_attn_out_g, v_w_out, v_ffn_norm_g, v_w_ffn_up, v_ffn_conv_w, v_ffn_conv_b, v_w_ffn_down, v_final_norm_g):
    given = dict(locals())
    weights = {name: given[name] for name in _WEIGHT_ORDER}
    moments_m = {name: given["m_" + name] for name in _WEIGHT_ORDER}
    moments_v = {name: given["v_" + name] for name in _WEIGHT_ORDER}
    seq = x.shape[1]
    n_rows = ROW_TILE + seq

    def shard2d(name, a):
        a = a.reshape(a.shape[-2], a.shape[-1])
        return a.T if name in _COLUMN_SHARDS else a

    early = [entry for entry in _SHARDED if entry[0] not in _LATE_WEIGHTS]
    shards = []
    for name, pad_to, _ in early:
        s = shard2d(name, weights[name])
        shards.append(s if pad_to is None else _pad_rows(s, pad_to))
    gathered = dict(zip([name for name, _, _ in early], _all_gather(shards, [dt for _, _, dt in early])))
    behind = gathered["meta_tokens"][0, 0, 0] * 0.0
    late_parts = [(shard2d(name, weights[name]) + behind).astype(BF16) for name in _LATE_WEIGHTS]
    late = _sequencer_exchange(late_parts, [True] * len(late_parts), "gather_late", GATHER_LATE_ID)
    meta_full = _unslab(gathered["meta_tokens"])
    full = {
        "meta_pad": jnp.concatenate([jnp.zeros((DEAD, D_MODEL), F32), meta_full], axis=0),
        "w_in": gathered["w_in"].reshape(D_IN, D_MODEL),
        "wq_t": gathered["w_uq"],
        "w_ukv": gathered["w_ukv"],
        "wv_t": gathered["w_ukv"][:, :, QK_NOPE:].transpose(0, 2, 1),
        "conv_w": _unslab(gathered["conv_w"][:, :CONV_WIDTH]),
        "fw": jnp.pad(gathered["ffn_conv_w"][:, :FFN_CONV_WIDTH], ((0, 0), (0, 0), (0, UP_PAD - UP_SLAB))),
        "fb": jnp.pad(ffn_conv_b.reshape(N_DEV, 1, UP_SLAB), ((0, 0), (0, 0), (0, UP_PAD - UP_SLAB))),
        "final_norm_g": final_norm_g.reshape(1, D_MODEL),
    }
    for name in ("mix_norm_g", "q_norm_g", "kv_norm_g", "conv_b", "conv_ln_g", "conv_ln_b", "conv_out_g", "attn_out_g",
                 "ffn_norm_g"):
        full[name] = weights[name]

    def ffn_weights():
        w_out_all, w_up_all, w_down_all = late
        return (w_out_all.reshape(D_MODEL, D_MODEL), w_up_all, w_down_all.reshape(N_ACT_SLAB, UP_SLAB, D_MODEL))

    wire = {name: (pad_to, dt) for name, pad_to, dt in _SHARDED}
    landing = {}

    def on_the_wire(name, slabs):
        pad_to, dt = wire[name]
        slabs = slabs.astype(dt)
        return slabs if pad_to is None else jnp.pad(slabs, ((0, 0), (0, pad_to - slabs.shape[1]), (0, 0)))

    def send_grads(stage, grads):
        parts = [on_the_wire(name, slabs) for name, slabs in grads.items()]
        if landing:
            arrived = list(landing)
            parts, held = lax.optimization_barrier((parts, [landing[name] for name in arrived]))
            landing.update(zip(arrived, held))
        landed = _sequencer_exchange(parts, [False] * len(parts), f"exchange_stage{stage}", EXCHANGE_STAGE_IDS[stage])
        landing.update(zip(grads, landed))

    loss, gx, sharded, replicated = _local_step(x[0], loss_target[0], full, n_rows, ffn_weights, send_grads)

    parts = [on_the_wire(name, slabs) for name, slabs in sharded.items()] + [_pack_replicated(replicated, loss)]
    landed = _exchange(parts, [False] * len(sharded) + [True])
    landing.update(zip(sharded, landed[:-1]))

    grad, delta, new_m, new_v = {}, {}, {}, {}
    for name, pad_to, _ in _SHARDED:
        land = landing[name]
        ws, ms, vs = (shard2d(name, a[name]) for a in (weights, moments_m, moments_v))
        rows = ws.shape[0]
        if pad_to is not None:
            ws, ms, vs = _pad_rows(ws, pad_to), _pad_rows(ms, pad_to), _pad_rows(vs, pad_to)
        outs = _adamw(land, ws, ms, vs, "adamw_" + name)
        shape = weights[name].shape
        grad[name], delta[name], new_m[name], new_v[name] = (
            (o.T if name in _COLUMN_SHARDS else o[:rows]).reshape(shape) for o in outs)
    loss, updates = _adamw_replicated(landed[-1], weights, moments_m, moments_v)
    for name, outs in updates.items():
        grad[name], delta[name], new_m[name], new_v[name] = (o.reshape(weights[name].shape) for o in outs)

    return (loss, gx[None], *[grad[n] for n in _WEIGHT_ORDER], *[delta[n] for n in _WEIGHT_ORDER],
            *[new_m[n] for n in _WEIGHT_ORDER], *[new_v[n] for n in _WEIGHT_ORDER])
```

```python
import functools

import jax
import jax.numpy as jnp
from jax import lax
from jax.experimental import pallas as pl
from jax.experimental.pallas import tpu as pltpu
from jax.experimental.pallas import tpu_sc as plsc

F32 = jnp.float32
BF16 = jnp.bfloat16

N_DEV = 8
D_MODEL = 1024
CHUNK = 64
CHUNK_SHIFT = 6
N_META = 16
D_CONV = 512
CONV_WIDTH = 31
N_HEADS = 8
QK_NOPE = 64
QK_ROPE = 32
QK_DIM = QK_NOPE + QK_ROPE
V_HEAD = 64
KV_HEAD = QK_NOPE + V_HEAD
D_ATTN = N_HEADS * V_HEAD
Q_LORA = 384
KV_LORA = 256
ROPE_THETA = 10000.0
D_IN = 2 * D_CONV + Q_LORA + KV_LORA + QK_ROPE
D_FF = 2816
D_UP = 2 * D_FF
FFN_CONV_WIDTH = 3
UP_SLAB = D_UP // N_DEV
N_ACT_SLAB = D_FF // UP_SLAB
EPS = 1e-6
NEG = -1e30
_LN2 = 0.6931471805599453
QK_LOGIT_SCALE = QK_DIM ** -0.5 / _LN2
ADAM_LR = 0.001
ADAM_B1 = 0.9
ADAM_B2 = 0.999
ADAM_EPS = 1e-08
ADAM_WD = 0.01
ADAM_STEP = 10

ROW_TILE = 256
DEAD = ROW_TILE - N_META
CONV_HALO = 32
FFN_HALO = 16
VMEM_LIMIT = 56 * 1024 * 1024
_LANES = 128

MESH = pl.DeviceIdType.MESH


def _dot(a, b):
    return jnp.dot(a, b, preferred_element_type=F32)


def _dot_nt(a, b):
    return lax.dot_general(a, b, (((1,), (1,)), ((), ())), preferred_element_type=F32)


def _dot_tn(a, b):
    return lax.dot_general(a, b, (((0,), (0,)), ((), ())), preferred_element_type=F32)


def _sigmoid(x):
    return 1.0 / (1.0 + jnp.exp2(x * (-1.0 / _LN2)))


def _rms_fwd(x, g):
    r = lax.rsqrt(jnp.mean(x * x, axis=-1, keepdims=True) + EPS)
    return x * r * g


def _rms_bwd(dy, x, g):
    r = lax.rsqrt(jnp.mean(x * x, axis=-1, keepdims=True) + EPS)
    w = dy * g
    dx = r * w - x * (r * r * r) * jnp.mean(w * x, axis=-1, keepdims=True)
    return dx, jnp.sum(dy * x * r, axis=0, keepdims=True)


def _rope(x, cos, sin):
    half = QK_ROPE // 2
    x1, x2 = x[:, :half], x[:, half:]
    return jnp.concatenate([x1 * cos - x2 * sin, x2 * cos + x1 * sin], axis=-1)


def _rope_t(dy, cos, sin):
    half = QK_ROPE // 2
    d1, d2 = dy[:, :half], dy[:, half:]
    return jnp.concatenate([d1 * cos + d2 * sin, d2 * cos - d1 * sin], axis=-1)


def _row_ids(i, rows):
    return i * rows + lax.broadcasted_iota(jnp.int32, (rows, 1), 0)


def _accumulate(ref, first, value):
    @pl.when(first)
    def _():
        ref[...] = value

    @pl.when(jnp.logical_not(first))
    def _():
        ref[...] += value


def _tile_spec(shape):
    nd = len(shape)
    if nd == 2:
        return pl.BlockSpec((ROW_TILE, shape[1]), lambda i: (i, 0))
    return pl.BlockSpec((shape[0], ROW_TILE, shape[2]), lambda i: (0, i, 0))


def _whole_spec(shape):
    nd = len(shape)
    return pl.BlockSpec(tuple(shape), lambda i: (0,) * nd, pipeline_mode=pl.Buffered(1))


def _acc_spec(shape):
    nd = len(shape)
    return pl.BlockSpec(tuple(shape), lambda i: (0,) * nd)


def _real_spec(width):
    return pl.BlockSpec((ROW_TILE, width), lambda i: (jnp.maximum(i - 1, 0), 0))


def _params(*semantics):
    return pltpu.CompilerParams(dimension_semantics=semantics, vmem_limit_bytes=VMEM_LIMIT)


def _fwd_in(x, meta_pad, g1, w_in, n_rows):
    nt = n_rows // ROW_TILE

    def body(x_ref, meta_ref, g_ref, w_ref, nb_ref, ag_ref, cq_ref, ckv_ref, kr_ref):
        i = pl.program_id(0)
        h0 = jnp.where(i == 0, meta_ref[...], x_ref[...])
        nb = _rms_fwd(h0, g_ref[...]).astype(BF16)
        nb_ref[...] = nb
        z = _dot_nt(nb, w_ref[...])
        ag_ref[...] = z[:, :2 * D_CONV]
        cq_ref[...] = z[:, 2 * D_CONV:2 * D_CONV + Q_LORA]
        ckv_ref[...] = z[:, 2 * D_CONV + Q_LORA:2 * D_CONV + Q_LORA + KV_LORA]
        kr_ref[...] = z[:, 2 * D_CONV + Q_LORA + KV_LORA:]

    out_shapes = [
        jax.ShapeDtypeStruct((n_rows, D_MODEL), BF16),
        jax.ShapeDtypeStruct((n_rows, 2 * D_CONV), F32),
        jax.ShapeDtypeStruct((n_rows, Q_LORA), F32),
        jax.ShapeDtypeStruct((n_rows, KV_LORA), F32),
        jax.ShapeDtypeStruct((n_rows, QK_ROPE), F32),
    ]
    return pl.pallas_call(
        body, name="fwd_in", grid=(nt,),
        in_specs=[_real_spec(D_MODEL), _whole_spec(meta_pad.shape), _whole_spec(g1.shape), _whole_spec(w_in.shape)],
        out_specs=[_tile_spec(s.shape) for s in out_shapes],
        out_shape=out_shapes,
        compiler_params=_params("parallel"),
    )(x, meta_pad, g1, w_in)


def _conv_chain(u1, ln_g, ln_b):
    mu = jnp.mean(u1, axis=-1, keepdims=True)
    xc = u1 - mu
    rstd = lax.rsqrt(jnp.mean(xc * xc, axis=-1, keepdims=True) + EPS)
    xh = xc * rstd
    u2 = xh * ln_g + ln_b
    return xh, u2, u2 * _sigmoid(u2), rstd


def _fwd_conv(ag, conv_w, conv_b, ln_g, ln_b, out_g, n_rows):
    nt = n_rows // ROW_TILE

    def body(ag_ref, w_ref, b_ref, lg_ref, lb_ref, og_ref, mix_ref, u1_ref, ext_ref, conv_ref):
        i = pl.program_id(0)

        @pl.when(i == 0)
        def _():
            ext_ref[:, 0:CONV_HALO, :] = jnp.zeros((CONV_PLANES, CONV_HALO, _LANES), F32)

        ag_t = ag_ref[...]
        live = _row_ids(i, ROW_TILE) >= DEAD
        u0 = jnp.where(live, ag_t[:, :D_CONV] * _sigmoid(ag_t[:, D_CONV:]), 0.0)
        _to_planes(ext_ref, (), slice(CONV_HALO, None), u0)
        first = CONV_HALO - (CONV_WIDTH - 1)
        for c in range(CONV_PLANES):
            taps = w_ref[:, c * _LANES:(c + 1) * _LANES]
            for p in range(PHASES):
                acc = jnp.zeros((PHASE_ROWS, _LANES), F32)
                for k in range(CONV_WIDTH):
                    acc = acc + taps[k:k + 1, :] * ext_ref[c, _phase(first + k + p), :]
                conv_ref[c, _phase(p), :] = acc
        ext_ref[:, 0:CONV_HALO, :] = ext_ref[:, ROW_TILE:ROW_TILE + CONV_HALO, :]
        u1 = _from_planes(conv_ref, (), D_CONV) + b_ref[...]
        u1_ref[...] = u1
        _, _, u3, _ = _conv_chain(u1, lg_ref[...], lb_ref[...])
        mix_ref[...] = _rms_fwd(u3, og_ref[...]).astype(BF16)

    out_shapes = [jax.ShapeDtypeStruct((n_rows, D_CONV), BF16), jax.ShapeDtypeStruct((n_rows, D_CONV), F32)]
    small = [conv_w, conv_b, ln_g, ln_b, out_g]
    return pl.pallas_call(
        body, name="fwd_conv", grid=(nt,),
        in_specs=[_tile_spec(ag.shape)] + [_whole_spec(a.shape) for a in small],
        out_specs=[_tile_spec(s.shape) for s in out_shapes],
        out_shape=out_shapes,
        scratch_shapes=[pltpu.VMEM((CONV_PLANES, ROW_TILE + CONV_HALO, _LANES), F32),
                        pltpu.VMEM((CONV_PLANES, ROW_TILE, _LANES), F32)],
        compiler_params=_params("arbitrary"),
    )(ag, *small)


def _lane_tile(shape):
    if len(shape) == 2:
        return pl.BlockSpec((shape[0], ROW_TILE), lambda i: (0, i))
    return pl.BlockSpec((shape[0], shape[1], ROW_TILE), lambda i: (0, 0, i))


def _rope_rows(x, cos, sin):
    half = QK_ROPE // 2
    x1, x2 = x[:half], x[half:]
    return jnp.concatenate([x1 * cos - x2 * sin, x2 * cos + x1 * sin], axis=0)


def _rope_rows_t(dy, cos, sin):
    half = QK_ROPE // 2
    d1, d2 = dy[:half], dy[half:]
    return jnp.concatenate([d1 * cos + d2 * sin, d2 * cos - d1 * sin], axis=0)


def _fwd_qkv(cq, ckv, kr, gq, gkv, wq_t, w_ukv, wv_t, cos, sin, cos_t, sin_t, n_rows):
    nt = n_rows // ROW_TILE

    def body(cq_ref, ckv_ref, kr_ref, gq_ref, gkv_ref, wqt_ref, wkv_ref, wvt_ref, cos_ref, sin_ref, cost_ref, sint_ref,
             qt_ref, k_ref, v_ref, vt_ref, cqn_ref, ckvn_ref):
        cqn = _rms_fwd(cq_ref[...], gq_ref[...]).astype(BF16)
        ckvn = _rms_fwd(ckv_ref[...], gkv_ref[...]).astype(BF16)
        cqn_ref[...] = cqn
        ckvn_ref[...] = ckvn
        k_rot = _rope(kr_ref[...], cos_ref[...], sin_ref[...])
        cos_rows, sin_rows = cost_ref[...], sint_ref[...]
        for h in range(N_HEADS):
            q_raw = _dot_nt(wqt_ref[h], cqn)
            q_h = jnp.concatenate([q_raw[:QK_NOPE], _rope_rows(q_raw[QK_NOPE:], cos_rows, sin_rows)], axis=0)
            qt_ref[h] = (q_h * QK_LOGIT_SCALE).astype(BF16)
            kv = _dot(ckvn, wkv_ref[h])
            k_ref[h] = jnp.concatenate([kv[:, :QK_NOPE], k_rot], axis=-1).astype(BF16)
            v_ref[h] = kv[:, QK_NOPE:].astype(BF16)
            vt_ref[h] = _dot_nt(wvt_ref[h], ckvn).astype(BF16)

    out_shapes = [
        jax.ShapeDtypeStruct((N_HEADS, QK_DIM, n_rows), BF16),
        jax.ShapeDtypeStruct((N_HEADS, n_rows, QK_DIM), BF16),
        jax.ShapeDtypeStruct((N_HEADS, n_rows, V_HEAD), BF16),
        jax.ShapeDtypeStruct((N_HEADS, V_HEAD, n_rows), BF16),
        jax.ShapeDtypeStruct((n_rows, Q_LORA), BF16),
        jax.ShapeDtypeStruct((n_rows, KV_LORA), BF16),
    ]
    tiles = [cq, ckv, kr]
    whole = [gq, gkv, wq_t, w_ukv, wv_t]
    out_specs = [_lane_tile(out_shapes[0].shape), _tile_spec(out_shapes[1].shape), _tile_spec(out_shapes[2].shape),
                 _lane_tile(out_shapes[3].shape), _tile_spec(out_shapes[4].shape), _tile_spec(out_shapes[5].shape)]
    return pl.pallas_call(
        body, name="fwd_qkv", grid=(nt,),
        in_specs=[_tile_spec(a.shape) for a in tiles] + [_whole_spec(a.shape) for a in whole]
        + [_tile_spec(cos.shape), _tile_spec(sin.shape), _lane_tile(cos_t.shape), _lane_tile(sin_t.shape)],
        out_specs=out_specs,
        out_shape=out_shapes,
        compiler_params=_params("parallel"),
    )(*tiles, *whole, cos, sin, cos_t, sin_t)


def _chunk_of(rows):
    return jnp.where(rows >= ROW_TILE, lax.shift_right_arithmetic(rows - ROW_TILE, CHUNK_SHIFT) + 1, 0)


def _visible(i, j):
    k_rows = j * ROW_TILE + lax.broadcasted_iota(jnp.int32, (ROW_TILE, 1), 0)
    q_rows = i * ROW_TILE + lax.broadcasted_iota(jnp.int32, (1, ROW_TILE), 1)
    return jnp.logical_and(_chunk_of(q_rows) >= _chunk_of(k_rows), k_rows >= DEAD)


def _attn_fwd(q_t, k, v_t, n_rows):
    nt = n_rows // ROW_TILE

    def body(qt_ref, k_ref, vt_ref, ot_ref, lse_ref, max_ref, sum_ref):
        i = pl.program_id(0)
        q_ts = [qt_ref[h] for h in range(N_HEADS)]

        def key_rows(j):
            return pl.ds(pl.multiple_of(j * ROW_TILE, ROW_TILE), ROW_TILE)

        def make_step(masked, tiles, first=0):
            def step(t, carry):
                js = [first + tiles * t + u for u in range(tiles)]
                scores = [[_dot(k_ref[h, key_rows(j), :], q_ts[h]) for h in range(N_HEADS)] for j in js]
                for j, tile_scores in zip(js, scores):
                    visible = _visible(i, j) if masked else None
                    probs, alphas = [], []
                    for h in range(N_HEADS):
                        m = max_ref[h]
                        s = jnp.where(visible, tile_scores[h], NEG) if masked else tile_scores[h]
                        m_new = jnp.maximum(m, jnp.max(s, axis=0, keepdims=True))
                        alpha = jnp.exp2(m - m_new)
                        p = jnp.exp2(s - m_new)
                        probs.append(p.astype(BF16))
                        alphas.append(alpha)
                        max_ref[h] = m_new
                        sum_ref[h] = alpha * sum_ref[h] + jnp.sum(p, axis=0, keepdims=True)
                    for h in range(N_HEADS):
                        ot_ref[h] = alphas[h] * ot_ref[h] + _dot(vt_ref[h, :, key_rows(j)], probs[h])
                return carry
            return step

        max_ref[...] = jnp.full(max_ref.shape, NEG, F32)
        sum_ref[...] = jnp.zeros_like(sum_ref)
        ot_ref[...] = jnp.zeros_like(ot_ref)
        between = jnp.maximum(i - 1, 0)
        quads = lax.shift_right_logical(between, 2)
        pairs = jnp.bitwise_and(lax.shift_right_logical(between, 1), 1)
        make_step(True, 1)(0, 0)
        lax.fori_loop(0, quads, make_step(False, 4, first=1), 0)
        lax.fori_loop(0, pairs, make_step(False, 2, first=1 + 4 * quads), 0)
        lax.fori_loop(1 + 4 * quads + 2 * pairs, i, make_step(False, 1), 0)
        lax.fori_loop(jnp.maximum(i, 1), i + 1, make_step(True, 1), 0)
        for h in range(N_HEADS):
            l = sum_ref[h]
            ot_ref[h] = ot_ref[h] / l
            lse_ref[h] = max_ref[h] + jnp.log2(l)

    out_shapes = [jax.ShapeDtypeStruct((N_HEADS, V_HEAD, n_rows), F32), jax.ShapeDtypeStruct((N_HEADS, 1, n_rows), F32)]
    return pl.pallas_call(
        body, name="attn_fwd", grid=(nt,),
        in_specs=[_lane_tile(q_t.shape), _whole_spec(k.shape), _whole_spec(v_t.shape)],
        out_specs=[_lane_tile(s.shape) for s in out_shapes],
        out_shape=out_shapes,
        scratch_shapes=[pltpu.VMEM((N_HEADS, 1, ROW_TILE), F32), pltpu.VMEM((N_HEADS, 1, ROW_TILE), F32)],
        compiler_params=_params("parallel"),
    )(q_t, k, v_t)


def _heads_to_rows(ref):
    return jnp.concatenate([ref[h] for h in range(N_HEADS)], axis=0)


def _rms_cols(x, g_col):
    r = lax.rsqrt(jnp.mean(x * x, axis=0, keepdims=True) + EPS)
    return x * r * g_col


def _fwd_out(x, meta_pad, mix_a, o_t, gb_col, w_out, n_rows):
    nt = n_rows // ROW_TILE

    def body(x_ref, meta_ref, mixa_ref, ot_ref, gb_ref, w_ref, mixbt_ref, h1_ref):
        i = pl.program_id(0)
        h0 = jnp.where(i == 0, meta_ref[...], x_ref[...])
        mix_bt = _rms_cols(_heads_to_rows(ot_ref), gb_ref[...]).astype(BF16)
        mixbt_ref[...] = mix_bt
        h1_ref[...] = h0 + _dot(mixa_ref[...], w_ref[:D_CONV, :]) + _dot_tn(mix_bt, w_ref[D_CONV:, :])

    out_shapes = [jax.ShapeDtypeStruct((D_ATTN, n_rows), BF16), jax.ShapeDtypeStruct((n_rows, D_MODEL), F32)]
    return pl.pallas_call(
        body, name="fwd_out", grid=(nt,),
        in_specs=[_real_spec(D_MODEL), _whole_spec(meta_pad.shape), _tile_spec(mix_a.shape), _lane_tile(o_t.shape),
                  _whole_spec(gb_col.shape), _whole_spec(w_out.shape)],
        out_specs=[_lane_tile(out_shapes[0].shape), _tile_spec(out_shapes[1].shape)],
        out_shape=out_shapes,
        compiler_params=_params("parallel"),
    )(x, meta_pad, mix_a, o_t, gb_col, w_out)


PHASES = 8
PHASE_ROWS = ROW_TILE // PHASES
UP_PLANES = -(-UP_SLAB // _LANES)
UP_PAD = UP_PLANES * _LANES
CONV_PLANES = D_CONV // _LANES


def _phase(start):
    return pl.ds(start, PHASE_ROWS, stride=PHASES)


def _to_planes(ref, lead, rows, value):
    width = value.shape[-1]
    for c in range(-(-width // _LANES)):
        part = value[:, c * _LANES:min((c + 1) * _LANES, width)]
        if part.shape[-1] < _LANES:
            part = jnp.concatenate([part, jnp.zeros((part.shape[0], _LANES - part.shape[-1]), part.dtype)], axis=-1)
        ref[(*lead, c, rows, slice(None))] = part


def _from_planes(ref, lead, width):
    planes = [ref[(*lead, c)] for c in range(-(-width // _LANES))]
    last = width - (len(planes) - 1) * _LANES
    return jnp.concatenate(planes[:-1] + [planes[-1][:, :last]], axis=-1)


def _fwd_ffn(h1, target, g2, w_up, fw, fb, w_down, gf, n_rows):
    nt = n_rows // ROW_TILE

    def body(h1_ref, t_ref, g2_ref, wup_ref, fw_ref, fb_ref, wdn_ref, gf_ref,
             n2_ref, up0_ref, act_ref, da_ref, db_ref, dh2_ref, loss_ref, dgf_ref, ext_ref):
        i = pl.program_id(0)

        @pl.when(i == 0)
        def _():
            ext_ref[:, 0:FFN_HALO, :] = jnp.zeros((N_DEV, FFN_HALO, UP_SLAB), F32)

        h1_t = h1_ref[...]
        live = _row_ids(i, ROW_TILE) >= DEAD
        n2 = jnp.where(live, _rms_fwd(h1_t, g2_ref[...]), 0.0).astype(BF16)
        n2_ref[...] = n2
        for s in range(N_DEV):
            up0 = _dot_nt(n2, wup_ref[s])
            up0_ref[s] = up0.astype(BF16)
            ext_ref[s, FFN_HALO:, :] = up0
        first = FFN_HALO - (FFN_CONV_WIDTH - 1)

        def conv(s):
            block = ext_ref[s]
            acc = fb_ref[s, :, :UP_SLAB] + fw_ref[s, FFN_CONV_WIDTH - 1:FFN_CONV_WIDTH, :UP_SLAB] * block[FFN_HALO:]
            for back in range(1, FFN_CONV_WIDTH):
                k = FFN_CONV_WIDTH - 1 - back
                acc = acc + fw_ref[s, k:k + 1, :UP_SLAB] * pltpu.roll(block, back, 0)[FFN_HALO:]
            return acc

        h2 = h1_t
        for s in range(N_ACT_SLAB):
            gate = conv(s)
            val = conv(s + N_ACT_SLAB)
            sg = _sigmoid(gate)
            silu = gate * sg
            act = (silu * val).astype(BF16)
            act_ref[s] = act
            da_ref[s] = (val * sg * (1.0 + gate * (1.0 - sg))).astype(BF16)
            db_ref[s] = silu.astype(BF16)
            h2 = h2 + _dot(act, wdn_ref[s])
        ext_ref[:, 0:FFN_HALO, :] = ext_ref[:, ROW_TILE:ROW_TILE + FFN_HALO, :]

        gf_t = gf_ref[...]
        y = _rms_fwd(h2, gf_t)
        diff = jnp.where(i >= 1, y - t_ref[...], 0.0)
        tile_loss = 0.5 * jnp.sum(jnp.sum(diff * diff, axis=-1, keepdims=True), axis=0, keepdims=True) / D_MODEL
        dh2, dgf = _rms_bwd(diff / D_MODEL, h2, gf_t)
        dh2_ref[...] = dh2
        _accumulate(loss_ref, i == 0, jnp.broadcast_to(tile_loss, loss_ref.shape))
        _accumulate(dgf_ref, i == 0, dgf)

    act_like = jax.ShapeDtypeStruct((N_ACT_SLAB, n_rows, UP_SLAB), BF16)
    out_shapes = [
        jax.ShapeDtypeStruct((n_rows, D_MODEL), BF16),
        jax.ShapeDtypeStruct((N_DEV, n_rows, UP_SLAB), BF16),
        act_like, act_like, act_like,
        jax.ShapeDtypeStruct((n_rows, D_MODEL), F32),
        jax.ShapeDtypeStruct((8, 128), F32),
        jax.ShapeDtypeStruct((1, D_MODEL), F32),
    ]
    whole = [g2, w_up, fw, fb, w_down, gf]
    return pl.pallas_call(
        body, name="fwd_ffn", grid=(nt,),
        in_specs=[_tile_spec(h1.shape), _real_spec(D_MODEL)] + [_whole_spec(a.shape) for a in whole],
        out_specs=[_tile_spec(s.shape) for s in out_shapes[:6]] + [_acc_spec(s.shape) for s in out_shapes[6:]],
        out_shape=out_shapes,
        scratch_shapes=[pltpu.VMEM((N_DEV, ROW_TILE + FFN_HALO, UP_SLAB), F32)],
        compiler_params=_params("arbitrary"),
    )(h1, target, *whole)


def _rope_tables(n_rows):
    pos = jnp.maximum(jnp.arange(n_rows, dtype=jnp.int32) - DEAD, 0)
    inv_freq = 1.0 / (ROPE_THETA ** (jnp.arange(0, QK_ROPE, 2, dtype=F32) / QK_ROPE))
    ang_t = inv_freq[:, None] * pos.astype(F32)[None, :]
    return jnp.cos(ang_t), jnp.sin(ang_t)


def _halo_after(shape, halo, n_rows):
    last = n_rows // halo - 1
    step = ROW_TILE // halo
    if len(shape) == 2:
        return pl.BlockSpec((halo, shape[1]), lambda i: (jnp.minimum((i + 1) * step, last), 0))
    return pl.BlockSpec((shape[0], halo, shape[2]), lambda i: (0, jnp.minimum((i + 1) * step, last), 0))


def _halo_before(shape, halo):
    step = ROW_TILE // halo
    if len(shape) == 2:
        return pl.BlockSpec((halo, shape[1]), lambda i: (jnp.maximum(i * step - 1, 0), 0))
    return pl.BlockSpec((shape[0], halo, shape[2]), lambda i: (0, jnp.maximum(i * step - 1, 0), 0))


def _bwd_ffn_act(dh2, da, db, w_down, n_rows):
    nt = n_rows // ROW_TILE

    def body(dh2_ref, da_ref, db_ref, wdn_ref, dup_ref, dfb_ref):
        i = pl.program_id(0)

        @pl.when(i == 0)
        def _():
            dfb_ref[...] = jnp.zeros_like(dfb_ref)

        dh2_b = dh2_ref[...].astype(BF16)
        for s in range(N_ACT_SLAB):
            d_act = _dot_nt(dh2_b, wdn_ref[s])
            d_gate = d_act * da_ref[s].astype(F32)
            d_val = d_act * db_ref[s].astype(F32)
            dup_ref[s] = d_gate.astype(BF16)
            dup_ref[s + N_ACT_SLAB] = d_val.astype(BF16)
            dfb_ref[s] += jnp.sum(d_gate, axis=0, keepdims=True)
            dfb_ref[s + N_ACT_SLAB] += jnp.sum(d_val, axis=0, keepdims=True)

    out_shapes = [jax.ShapeDtypeStruct((N_DEV, n_rows, UP_SLAB), BF16), jax.ShapeDtypeStruct((N_DEV, 1, UP_SLAB), F32)]
    return pl.pallas_call(
        body, name="bwd_ffn_act", grid=(nt,),
        in_specs=[_tile_spec(dh2.shape), _tile_spec(da.shape), _tile_spec(db.shape), _whole_spec(w_down.shape)],
        out_specs=[_tile_spec(out_shapes[0].shape), _acc_spec(out_shapes[1].shape)],
        out_shape=out_shapes,
        compiler_params=_params("arbitrary"),
    )(dh2, da, db, w_down)


def _bwd_ffn_up(dup, up0, h1, dh2, g2, w_up, fw, n_rows):
    nt = n_rows // ROW_TILE
    last_tap = FFN_CONV_WIDTH - 1
    ext_rows = ROW_TILE + FFN_HALO

    def body(dup_ref, dnext_ref, up0_ref, h1_ref, dh2_ref, g2_ref, wup_ref, fw_ref,
             dup0_ref, dh1_ref, dfw_ref, dg2_ref):
        i = pl.program_id(0)

        @pl.when(i == 0)
        def _():
            dfw_ref[...] = jnp.zeros_like(dfw_ref)

        live = _row_ids(i, ROW_TILE) >= DEAD
        dn2 = jnp.zeros((ROW_TILE, D_MODEL), F32)
        for s in range(N_DEV):
            d = dup_ref[s].astype(F32)
            block = jnp.concatenate([d, jnp.where(i == nt - 1, 0.0, dnext_ref[s].astype(F32))], axis=0)
            u = up0_ref[s].astype(F32)
            dup0 = fw_ref[s, last_tap:last_tap + 1, :UP_SLAB] * d
            dfw_ref[s, last_tap:last_tap + 1, :UP_SLAB] += jnp.sum(d * u, axis=0, keepdims=True)
            for ahead in range(1, FFN_CONV_WIDTH):
                k = last_tap - ahead
                shifted = pltpu.roll(block, ext_rows - ahead, 0)[:ROW_TILE]
                dup0 = dup0 + fw_ref[s, k:k + 1, :UP_SLAB] * shifted
                dfw_ref[s, k:k + 1, :UP_SLAB] += jnp.sum(shifted * u, axis=0, keepdims=True)
            dup0_b = jnp.where(live, dup0, 0.0).astype(BF16)
            dup0_ref[s] = dup0_b
            dn2 = dn2 + _dot(dup0_b, wup_ref[s])
        dx, dg2 = _rms_bwd(dn2, h1_ref[...], g2_ref[...])
        dh1_ref[...] = dh2_ref[...] + dx
        _accumulate(dg2_ref, i == 0, dg2)

    out_shapes = [
        jax.ShapeDtypeStruct((N_DEV, n_rows, UP_SLAB), BF16),
        jax.ShapeDtypeStruct((n_rows, D_MODEL), F32),
        jax.ShapeDtypeStruct((N_DEV, FFN_CONV_WIDTH, UP_PAD), F32),
        jax.ShapeDtypeStruct((1, D_MODEL), F32),
    ]
    return pl.pallas_call(
        body, name="bwd_ffn_up", grid=(nt,),
        in_specs=[_tile_spec(dup.shape), _halo_after(dup.shape, FFN_HALO, n_rows), _tile_spec(up0.shape),
                  _tile_spec(h1.shape), _tile_spec(dh2.shape),
                  _whole_spec(g2.shape), _whole_spec(w_up.shape), _whole_spec(fw.shape)],
        out_specs=[_tile_spec(s.shape) for s in out_shapes[:2]] + [_acc_spec(s.shape) for s in out_shapes[2:]],
        out_shape=out_shapes,
        compiler_params=_params("arbitrary"),
    )(dup, dup, up0, h1, dh2, g2, w_up, fw)


def _bwd_out(dh1, o_t, u1, w_out, gb_col, ln_g, ln_b, ga, n_rows):
    nt = n_rows // ROW_TILE

    def body(dh1_ref, ot_ref, u1_ref, w_ref, gb_ref, lg_ref, lb_ref, ga_ref,
             dot_ref, delta_ref, du1_ref, dgb_ref, dga_ref, dlg_ref, dlb_ref, dcb_ref):
        i = pl.program_id(0)
        dh1_b = dh1_ref[...].astype(BF16)
        o_t = _heads_to_rows(ot_ref)
        gb = gb_ref[...]
        r = lax.rsqrt(jnp.mean(o_t * o_t, axis=0, keepdims=True) + EPS)
        dmix_bt = _dot_nt(w_ref[D_CONV:, :], dh1_b)
        wgt = dmix_bt * gb
        do_t = r * wgt - o_t * (r * r * r) * jnp.mean(wgt * o_t, axis=0, keepdims=True)
        dgb = jnp.sum(dmix_bt * o_t * r, axis=1, keepdims=True)
        for h in range(N_HEADS):
            do_h = do_t[h * V_HEAD:(h + 1) * V_HEAD]
            dot_ref[h] = do_h.astype(BF16)
            delta_ref[h] = jnp.sum(do_h * ot_ref[h], axis=0, keepdims=True)
        lg = lg_ref[...]
        xh, u2, u3, rstd = _conv_chain(u1_ref[...], lg, lb_ref[...])
        du3, dga = _rms_bwd(_dot_nt(dh1_b, w_ref[:D_CONV, :]), u3, ga_ref[...])
        sg = _sigmoid(u2)
        du2 = du3 * sg * (1.0 + u2 * (1.0 - sg))
        dxh = du2 * lg
        du1 = rstd * (dxh - jnp.mean(dxh, axis=-1, keepdims=True) - xh * jnp.mean(dxh * xh, axis=-1, keepdims=True))
        du1_ref[...] = du1
        first = i == 0
        _accumulate(dgb_ref, first, dgb)
        _accumulate(dga_ref, first, dga)
        _accumulate(dlg_ref, first, jnp.sum(du2 * xh, axis=0, keepdims=True))
        _accumulate(dlb_ref, first, jnp.sum(du2, axis=0, keepdims=True))
        _accumulate(dcb_ref, first, jnp.sum(du1, axis=0, keepdims=True))

    out_shapes = [
        jax.ShapeDtypeStruct((N_HEADS, V_HEAD, n_rows), BF16),
        jax.ShapeDtypeStruct((N_HEADS, 1, n_rows), F32),
        jax.ShapeDtypeStruct((n_rows, D_CONV), F32),
        jax.ShapeDtypeStruct((D_ATTN, 1), F32),
    ] + [jax.ShapeDtypeStruct((1, D_CONV), F32)] * 4
    whole = [w_out, gb_col, ln_g, ln_b, ga]
    return pl.pallas_call(
        body, name="bwd_out", grid=(nt,),
        in_specs=[_tile_spec(dh1.shape), _lane_tile(o_t.shape), _tile_spec(u1.shape)] + [_whole_spec(a.shape) for a in whole],
        out_specs=[_lane_tile(out_shapes[0].shape), _lane_tile(out_shapes[1].shape), _tile_spec(out_shapes[2].shape)]
        + [_acc_spec(s.shape) for s in out_shapes[3:]],
        out_shape=out_shapes,
        compiler_params=_params("arbitrary"),
    )(dh1, o_t, u1, *whole)


ATTN_BWD_HEADS = 8


def _attn_bwd(q_t, k, v, do_t, lse, delta, n_rows):
    nt = n_rows // ROW_TILE
    hp = ATTN_BWD_HEADS

    def body(k_ref, v_ref, qt_ref, dot_ref, lse_ref, delta_ref, dqt_ref, dk_ref, dv_ref):
        j = pl.program_id(1)

        @pl.when(j == 0)
        def _():
            dqt_ref[...] = jnp.zeros_like(dqt_ref)

        k_ts = [k_ref[h] for h in range(hp)]
        v_ts = [v_ref[h] for h in range(hp)]

        def make_step(masked, tiles, first=0):
            def step(t, carry):
                tiles_of_step = []
                for u in range(tiles):
                    i = first + tiles * t + u
                    cols = pl.ds(pl.multiple_of(i * ROW_TILE, ROW_TILE), ROW_TILE)
                    q_is = [qt_ref[h, :, cols] for h in range(hp)]
                    do_is = [dot_ref[h, :, cols] for h in range(hp)]
                    scores = [_dot(k_ts[h], q_is[h]) for h in range(hp)]
                    dps = [_dot(v_ts[h], do_is[h]) for h in range(hp)]
                    tiles_of_step.append((i, cols, q_is, do_is, scores, dps))
                for i, cols, q_is, do_is, scores, dps in tiles_of_step:
                    visible = _visible(i, j) if masked else None
                    probs, dss = [], []
                    for h in range(hp):
                        s = jnp.where(visible, scores[h], NEG) if masked else scores[h]
                        p = jnp.exp2(s - lse_ref[h, :, cols])
                        probs.append(p.astype(BF16))
                        dss.append((p * (dps[h] - delta_ref[h, :, cols])).astype(BF16))
                    for h in range(hp):
                        dv_ref[h] += _dot_nt(probs[h], do_is[h])
                        dk_ref[h] += _dot_nt(dss[h], q_is[h])
                        dqt_ref[h, :, cols] += _dot_tn(k_ts[h], dss[h])
                return carry
            return step

        dk_ref[...] = jnp.zeros_like(dk_ref)
        dv_ref[...] = jnp.zeros_like(dv_ref)
        make_step(True, 1)(j, 0)
        lax.fori_loop(jnp.where(j == 0, j + 1, nt), nt, make_step(True, 1), 0)
        unmasked = jnp.where(j == 0, 0, nt - 1 - j)
        quads = lax.shift_right_logical(unmasked, 2)
        pairs = jnp.bitwise_and(lax.shift_right_logical(unmasked, 1), 1)
        lax.fori_loop(0, quads, make_step(False, 4, first=j + 1), 0)
        lax.fori_loop(0, pairs, make_step(False, 2, first=j + 1 + 4 * quads), 0)
        lax.fori_loop(jnp.where(j == 0, nt, j + 1 + 4 * quads + 2 * pairs), nt, make_step(False, 1), 0)
        dk_ref[...] = dk_ref[...] * _LN2

    key_tile = lambda w: pl.BlockSpec((hp, ROW_TILE, w), lambda g, j: (g, j, 0))
    all_cols = lambda w: pl.BlockSpec((hp, w, n_rows), lambda g, j: (g, 0, 0))
    resident = lambda w: pl.BlockSpec((hp, w, n_rows), lambda g, j: (g, 0, 0), pipeline_mode=pl.Buffered(1))
    out_shapes = [
        jax.ShapeDtypeStruct((N_HEADS, QK_DIM, n_rows), F32),
        jax.ShapeDtypeStruct((N_HEADS, n_rows, QK_DIM), F32),
        jax.ShapeDtypeStruct((N_HEADS, n_rows, V_HEAD), F32),
    ]
    return pl.pallas_call(
        body, name="attn_bwd", grid=(N_HEADS // hp, nt),
        in_specs=[key_tile(QK_DIM), key_tile(V_HEAD), resident(QK_DIM), resident(V_HEAD), resident(1), resident(1)],
        out_specs=[all_cols(QK_DIM), key_tile(QK_DIM), key_tile(V_HEAD)],
        out_shape=out_shapes,
        compiler_params=_params("parallel", "arbitrary"),
    )(k, v, q_t, do_t, lse, delta)


def _bwd_qkv(dq_t, dk, dv, cq, ckv, gq, gkv, wq_t, w_ukv, cos, sin, cos_t, sin_t, n_rows):
    nt = n_rows // ROW_TILE

    def body(dqt_ref, dk_ref, dv_ref, cq_ref, ckv_ref, gq_ref, gkv_ref, wqt_ref, wkv_ref, cos_ref, sin_ref,
             cost_ref, sint_ref, dqraw_ref, dkv_ref, dcq_ref, dckv_ref, dkr_ref, dgq_ref, dgkv_ref):
        i = pl.program_id(0)
        cos_rows, sin_rows = cost_ref[...], sint_ref[...]
        dcqn = jnp.zeros((ROW_TILE, Q_LORA), F32)
        dckvn = jnp.zeros((ROW_TILE, KV_LORA), F32)
        dk_rot = jnp.zeros((ROW_TILE, QK_ROPE), F32)
        for h in range(N_HEADS):
            dq_h, dk_h = dqt_ref[h] * QK_DIM ** -0.5, dk_ref[h]
            dq_raw = jnp.concatenate(
                [dq_h[:QK_NOPE], _rope_rows_t(dq_h[QK_NOPE:], cos_rows, sin_rows)], axis=0).astype(BF16)
            dqraw_ref[h] = dq_raw
            dcqn = dcqn + _dot_tn(dq_raw, wqt_ref[h])
            dkv = jnp.concatenate([dk_h[:, :QK_NOPE], dv_ref[h]], axis=-1).astype(BF16)
            dkv_ref[:, h * KV_HEAD:(h + 1) * KV_HEAD] = dkv
            dckvn = dckvn + _dot_nt(dkv, wkv_ref[h])
            dk_rot = dk_rot + dk_h[:, QK_NOPE:]
        dkr_ref[...] = _rope_t(dk_rot, cos_ref[...], sin_ref[...]).astype(BF16)
        dcq, dgq = _rms_bwd(dcqn, cq_ref[...], gq_ref[...])
        dckv, dgkv = _rms_bwd(dckvn, ckv_ref[...], gkv_ref[...])
        dcq_ref[...] = dcq.astype(BF16)
        dckv_ref[...] = dckv.astype(BF16)
        _accumulate(dgq_ref, i == 0, dgq)
        _accumulate(dgkv_ref, i == 0, dgkv)

    out_shapes = [
        jax.ShapeDtypeStruct((N_HEADS, QK_DIM, n_rows), BF16),
        jax.ShapeDtypeStruct((n_rows, N_HEADS * KV_HEAD), BF16),
        jax.ShapeDtypeStruct((n_rows, Q_LORA), BF16),
        jax.ShapeDtypeStruct((n_rows, KV_LORA), BF16),
        jax.ShapeDtypeStruct((n_rows, QK_ROPE), BF16),
        jax.ShapeDtypeStruct((1, Q_LORA), F32),
        jax.ShapeDtypeStruct((1, KV_LORA), F32),
    ]
    tiles = [dk, dv, cq, ckv]
    whole = [gq, gkv, wq_t, w_ukv]
    return pl.pallas_call(
        body, name="bwd_qkv", grid=(nt,),
        in_specs=[_lane_tile(dq_t.shape)] + [_tile_spec(a.shape) for a in tiles] + [_whole_spec(a.shape) for a in whole]
        + [_tile_spec(cos.shape), _tile_spec(sin.shape), _lane_tile(cos_t.shape), _lane_tile(sin_t.shape)],
        out_specs=[_lane_tile(out_shapes[0].shape)] + [_tile_spec(s.shape) for s in out_shapes[1:5]]
        + [_acc_spec(s.shape) for s in out_shapes[5:]],
        out_shape=out_shapes,
        compiler_params=_params("arbitrary"),
    )(dq_t, *tiles, *whole, cos, sin, cos_t, sin_t)


def _bwd_conv(du1, ag, conv_w, dcq, dckv, dkr, n_rows):
    nt = n_rows // ROW_TILE

    last_tap = CONV_WIDTH - 1

    def body(du1_ref, dnext_ref, ag_ref, w_ref, dcq_ref, dckv_ref, dkr_ref, dz_ref, dw_ref,
             dext_ref, uext_ref, conv_ref, sums_ref):
        i = pl.program_id(0)

        @pl.when(i == 0)
        def _():
            sums_ref[...] = jnp.zeros_like(sums_ref)

        _to_planes(dext_ref, (), slice(0, ROW_TILE), du1_ref[...])
        _to_planes(dext_ref, (), slice(ROW_TILE, None), jnp.where(i == nt - 1, 0.0, dnext_ref[...]))
        ag_t = ag_ref[...]
        live = _row_ids(i, ROW_TILE) >= DEAD
        sg = _sigmoid(ag_t[:, D_CONV:])
        _to_planes(uext_ref, (), slice(None), jnp.where(live, ag_t[:, :D_CONV] * sg, 0.0))
        for c in range(CONV_PLANES):
            taps = w_ref[:, c * _LANES:(c + 1) * _LANES]
            for half in range(0, PHASES, PHASES // 2):
                phases = range(half, half + PHASES // 2)
                us = {p: uext_ref[c, _phase(p), :] for p in phases}
                accs = {p: jnp.zeros((PHASE_ROWS, _LANES), F32) for p in phases}
                for k in range(CONV_WIDTH):
                    tap_sum = jnp.zeros((PHASE_ROWS, _LANES), F32)
                    for p in phases:
                        shifted = dext_ref[c, _phase(p + last_tap - k), :]
                        accs[p] = accs[p] + taps[k:k + 1, :] * shifted
                        tap_sum = tap_sum + shifted * us[p]
                    sums_ref[c, k] += tap_sum
                for p in phases:
                    conv_ref[c, _phase(p), :] = accs[p]
        du0 = jnp.where(live, _from_planes(conv_ref, (), D_CONV), 0.0)
        da = du0 * sg
        dgate = du0 * ag_t[:, :D_CONV] * sg * (1.0 - sg)
        dz_ref[...] = jnp.concatenate(
            [da.astype(BF16), dgate.astype(BF16), dcq_ref[...], dckv_ref[...], dkr_ref[...]], axis=-1)

        @pl.when(i == nt - 1)
        def _():
            for c in range(CONV_PLANES):
                for k in range(CONV_WIDTH):
                    dw_ref[k:k + 1, c * _LANES:(c + 1) * _LANES] = jnp.sum(sums_ref[c, k], axis=0, keepdims=True)

    out_shapes = [jax.ShapeDtypeStruct((n_rows, D_IN), BF16), jax.ShapeDtypeStruct((CONV_WIDTH, D_CONV), F32)]
    return pl.pallas_call(
        body, name="bwd_conv", grid=(nt,),
        in_specs=[_tile_spec(du1.shape), _halo_after(du1.shape, CONV_HALO, n_rows), _tile_spec(ag.shape),
                  _whole_spec(conv_w.shape), _tile_spec(dcq.shape), _tile_spec(dckv.shape), _tile_spec(dkr.shape)],
        out_specs=[_tile_spec(out_shapes[0].shape), _acc_spec(out_shapes[1].shape)],
        out_shape=out_shapes,
        scratch_shapes=[pltpu.VMEM((CONV_PLANES, ROW_TILE + CONV_HALO, _LANES), F32),
                        pltpu.VMEM((CONV_PLANES, ROW_TILE, _LANES), F32), pltpu.VMEM((CONV_PLANES, ROW_TILE, _LANES), F32),
                        pltpu.VMEM((CONV_PLANES, CONV_WIDTH, PHASE_ROWS, _LANES), F32)],
        compiler_params=_params("arbitrary"),
    )(du1, du1, ag, conv_w, dcq, dckv, dkr)


def _bwd_in(dz, x, meta_pad, dh1, g1, w_in, n_rows):
    nt = n_rows // ROW_TILE

    def body(dz_ref, x_ref, meta_ref, dh1_ref, g_ref, w_ref, gx_ref, gmeta_ref, dg1_ref):
        i = pl.program_id(0)
        h0 = jnp.where(i == 0, meta_ref[...], x_ref[...])
        dx, dg1 = _rms_bwd(_dot(dz_ref[...], w_ref[...]), h0, g_ref[...])
        dh0 = dh1_ref[...] + dx
        gx_ref[...] = dh0

        @pl.when(i == 0)
        def _():
            gmeta_ref[...] = dh0

        _accumulate(dg1_ref, i == 0, dg1)

    out_shapes = [
        jax.ShapeDtypeStruct((n_rows - ROW_TILE, D_MODEL), F32),
        jax.ShapeDtypeStruct((ROW_TILE, D_MODEL), F32),
        jax.ShapeDtypeStruct((1, D_MODEL), F32),
    ]
    return pl.pallas_call(
        body, name="bwd_in", grid=(nt,),
        in_specs=[_tile_spec(dz.shape), _real_spec(D_MODEL), _whole_spec(meta_pad.shape), _tile_spec(dh1.shape),
                  _whole_spec(g1.shape), _whole_spec(w_in.shape)],
        out_specs=[_real_spec(D_MODEL), _acc_spec(out_shapes[1].shape), _acc_spec(out_shapes[2].shape)],
        out_shape=out_shapes,
        compiler_params=_params("arbitrary"),
    )(dz, x, meta_pad, dh1, g1, w_in)


def _contraction_tile(n_rows):
    return next(t for t in range(n_rows // 2 // _LANES * _LANES, 0, -_LANES) if n_rows % t == 0)


def _weight_grad(a, b, name, a_transposed=False):
    groups = max(a.shape[0] if a.ndim == 3 else 1, b.shape[0] if b.ndim == 3 else 1)
    n_rows, n = b.shape[-2], b.shape[-1]
    m = a.shape[-2] if a_transposed else a.shape[-1]
    kt = _contraction_tile(n_rows)
    steps = n_rows // kt

    def body(a_ref, b_ref, out_ref, acc_ref):
        i = pl.program_id(1)
        a_t, b_t = a_ref[...].astype(BF16), b_ref[...].astype(BF16)
        part = _dot(a_t, b_t) if a_transposed else _dot_tn(a_t, b_t)
        _accumulate(acc_ref, i == 0, part)

        @pl.when(i == steps - 1)
        def _():
            out_ref[...] = acc_ref[...].astype(out_ref.dtype)

    def spec(arr, rows_last):
        block = (arr.shape[-2], kt) if rows_last else (kt, arr.shape[-1])
        at = (lambda i: (0, i)) if rows_last else (lambda i: (i, 0))
        if arr.ndim == 3:
            return pl.BlockSpec((None,) + block, lambda g, i: (g,) + at(i))
        return pl.BlockSpec(block, lambda g, i: at(i))

    return pl.pallas_call(
        body, name=name, grid=(groups, steps),
        in_specs=[spec(a, a_transposed), spec(b, False)],
        out_specs=pl.BlockSpec((None, m, n), lambda g, i: (g, 0, 0)),
        out_shape=jax.ShapeDtypeStruct((groups, m, n), BF16),
        scratch_shapes=[pltpu.VMEM((m, n), F32)],
        compiler_params=_params("parallel", "arbitrary"),
    )(a, b)


def _my_index():
    return 4 * lax.axis_index("x") + 2 * lax.axis_index("y") + lax.axis_index("c")


def _peer(k):
    flip = lambda v, bit: 1 - v if bit else v
    px = flip(lax.axis_index("x"), k & 4)
    py = flip(lax.axis_index("y"), k & 2)
    pc = flip(lax.axis_index("c"), k & 1)
    return (px, py, pc), 4 * px + 2 * py + pc


def _all_gather(shards, dtypes):
    n = len(shards)
    sibling, chips = 1, (2, 4, 6)

    def body(*refs):
        ins, outs, stages = refs[:n], refs[n:2 * n], refs[2 * n:3 * n]
        send_sems, recv_sems, local_sems = refs[3 * n:]
        me = _my_index()
        for a in range(n):
            stages[a][...] = ins[a][...].astype(stages[a].dtype)
        local = [pltpu.make_async_copy(stages[a], outs[a].at[me], local_sems.at[a]) for a in range(n)]
        for cp in local:
            cp.start()

        def copy(a, k, src, slot, to):
            return pltpu.make_async_remote_copy(
                src_ref=src, dst_ref=outs[a].at[slot], send_sem=send_sems.at[a, k - 1],
                recv_sem=recv_sems.at[a, k - 1], device_id=_peer(to)[0], device_id_type=MESH)

        def own(a, k):
            return copy(a, k, stages[a], me, k)

        def passed(a, k):
            slot = _peer(k)[1]
            return copy(a, k ^ sibling, outs[a].at[slot], slot, sibling)

        def arrival(a, k):
            return copy(a, k, stages[a], _peer(k)[1], k)

        for k in (sibling,) + chips:
            for a in range(n):
                own(a, k).start()
        for k in chips:
            for a in range(n):
                arrival(a, k).wait_recv()
                passed(a, k).start()
        for a in range(n):
            arrival(a, sibling).wait_recv()
            for k in chips:
                arrival(a, k ^ sibling).wait_recv()
        for a in range(n):
            for k in (sibling,) + chips:
                own(a, k).wait_send()
            for k in chips:
                passed(a, k).wait_send()
        for cp in local:
            cp.wait()

    return pl.pallas_call(
        body, name="gather_weights",
        in_specs=[pl.BlockSpec(memory_space=pltpu.VMEM)] * n,
        out_specs=[pl.BlockSpec(memory_space=pl.ANY)] * n,
        out_shape=[jax.ShapeDtypeStruct((N_DEV,) + s.shape, dt) for s, dt in zip(shards, dtypes)],
        scratch_shapes=[pltpu.VMEM(s.shape, dt) for s, dt in zip(shards, dtypes)]
        + [pltpu.SemaphoreType.DMA((n, N_DEV - 1)), pltpu.SemaphoreType.DMA((n, N_DEV - 1)), pltpu.SemaphoreType.DMA((n,))],
        compiler_params=pltpu.CompilerParams(vmem_limit_bytes=VMEM_LIMIT),
    )(*shards)


def _exchange(parts, whole):
    n = len(parts)

    def body(*refs):
        ins, outs = refs[:n], refs[n:2 * n]
        send_sems, recv_sems, local_sems = refs[2 * n:]
        me = _my_index()

        def src(a, slab):
            return ins[a] if whole[a] else ins[a].at[slab]

        local = [pltpu.make_async_copy(src(a, me), outs[a].at[me], local_sems.at[a]) for a in range(n)]
        for cp in local:
            cp.start()

        def copy(a, k, slab, slot):
            peer, _ = _peer(k)
            return pltpu.make_async_remote_copy(
                src_ref=src(a, slab), dst_ref=outs[a].at[slot], send_sem=send_sems.at[a, k - 1],
                recv_sem=recv_sems.at[a, k - 1], device_id=peer, device_id_type=MESH)

        for k in range(1, N_DEV):
            for a in range(n):
                copy(a, k, _peer(k)[1], me).start()
        for k in range(1, N_DEV):
            for a in range(n):
                copy(a, k, _peer(k)[1], _peer(k)[1]).wait()
        for cp in local:
            cp.wait()

    return pl.pallas_call(
        body, name="exchange_grads",
        in_specs=[pl.BlockSpec(memory_space=pl.ANY)] * n,
        out_specs=[pl.BlockSpec(memory_space=pl.ANY)] * n,
        out_shape=[jax.ShapeDtypeStruct(((N_DEV,) + p.shape) if w else p.shape, p.dtype) for p, w in zip(parts, whole)],
        scratch_shapes=[pltpu.SemaphoreType.DMA((n, N_DEV - 1)), pltpu.SemaphoreType.DMA((n, N_DEV - 1)),
                        pltpu.SemaphoreType.DMA((n,))],
    )(*parts)


def _sequencer_exchange(parts, whole, name, collective_id):
    n = len(parts)
    srcs = [jax.new_ref(p, memory_space=pltpu.MemorySpace.HBM) for p in parts]
    lands = [jax.empty_ref(jax.ShapeDtypeStruct(((N_DEV,) + p.shape) if w else p.shape, p.dtype),
                           memory_space=pltpu.MemorySpace.HBM) for p, w in zip(parts, whole)]

    @pl.kernel(mesh=plsc.ScalarSubcoreMesh(axis_name="sequencer", num_cores=1), name=name,
               scratch_types=(pltpu.SemaphoreType.DMA((n, N_DEV - 1)), pltpu.SemaphoreType.DMA((n, N_DEV - 1)),
                              pltpu.SemaphoreType.DMA((n,))),
               compiler_params=pltpu.CompilerParams(collective_id=collective_id))
    def launch(send_sems, recv_sems, local_sems):
        barrier = pltpu.get_barrier_semaphore()
        for k in range(1, N_DEV):
            pl.semaphore_signal(barrier, inc=1, device_id=_peer(k)[0], device_id_type=MESH)
        pl.semaphore_wait(barrier, N_DEV - 1)
        me = _my_index()

        def src(a, slab):
            return srcs[a] if whole[a] else srcs[a].at[slab]

        local = [pltpu.make_async_copy(src(a, me), lands[a].at[me], local_sems.at[a]) for a in range(n)]
        for cp in local:
            cp.start()

        def copy(a, k, slab, slot):
            return pltpu.make_async_remote_copy(
                src_ref=src(a, slab), dst_ref=lands[a].at[slot], send_sem=send_sems.at[a, k - 1],
                recv_sem=recv_sems.at[a, k - 1], device_id=_peer(k)[0], device_id_type=MESH)

        for k in range(1, N_DEV):
            for a in range(n):
                copy(a, k, _peer(k)[1], me).start()
        for k in range(1, N_DEV):
            for a in range(n):
                copy(a, k, _peer(k)[1], _peer(k)[1]).wait()
        for cp in local:
            cp.wait()

    launch()
    return [land[...] for land in lands]


def _row_block(rows):
    if rows <= ROW_TILE:
        return rows
    return next(rb for rb in range(ROW_TILE, 0, -16) if rows % rb == 0)


def _adamw(landing, w, m, v, name):
    rows, cols = w.shape
    rb = _row_block(rows)

    def body(l_ref, w_ref, m_ref, v_ref, g_ref, d_ref, m2_ref, v2_ref):
        g = l_ref[0].astype(F32)
        for p in range(1, N_DEV):
            g = g + l_ref[p].astype(F32)
        g_ref[...] = g
        d_ref[...], m2_ref[...], v2_ref[...] = _adamw_step(g, w_ref[...], m_ref[...], v_ref[...])

    flat = pl.BlockSpec((rb, cols), lambda i: (i, 0))
    return pl.pallas_call(
        body, name=name, grid=(rows // rb,),
        in_specs=[pl.BlockSpec((N_DEV, rb, cols), lambda i: (0, i, 0)), flat, flat, flat],
        out_specs=[flat] * 4,
        out_shape=[jax.ShapeDtypeStruct((rows, cols), F32)] * 4,
        compiler_params=_params("parallel"),
    )(landing, w, m, v)


def _adamw_step(g, w, m, v):
    m2 = ADAM_B1 * m + (1.0 - ADAM_B1) * g
    v2 = ADAM_B2 * v + (1.0 - ADAM_B2) * (g * g)
    m_hat = m2 / (1.0 - ADAM_B1 ** ADAM_STEP)
    v_hat = v2 / (1.0 - ADAM_B2 ** ADAM_STEP)
    return -ADAM_LR * (m_hat / (jnp.sqrt(v_hat) + ADAM_EPS) + ADAM_WD * w), m2, v2


_REPLICATED = (
    ("mix_norm_g", D_MODEL), ("q_norm_g", Q_LORA), ("kv_norm_g", KV_LORA), ("conv_b", D_CONV), ("conv_ln_g", D_CONV),
    ("conv_ln_b", D_CONV), ("conv_out_g", D_CONV), ("attn_out_g", D_CONV), ("ffn_norm_g", D_MODEL),
    ("ffn_conv_b", D_UP), ("final_norm_g", D_MODEL),
)
_REPLICATED_WIDTH = sum(size for _, size in _REPLICATED) + _LANES

_WEIGHT_ORDER = (
    "meta_tokens", "mix_norm_g", "w_in", "q_norm_g", "w_uq", "kv_norm_g", "w_ukv", "conv_w", "conv_b", "conv_ln_g",
    "conv_ln_b", "conv_out_g", "attn_out_g", "w_out", "ffn_norm_g", "w_ffn_up", "ffn_conv_w", "ffn_conv_b",
    "w_ffn_down", "final_norm_g",
)


def _pack_replicated(grads, loss):
    rows = [grads[name].reshape(1, size) for name, size in _REPLICATED]
    return jnp.concatenate(rows + [jnp.broadcast_to(loss.reshape(1, 1), (1, _LANES))], axis=-1)


def _adamw_replicated(landing, weights, moments_m, moments_v):
    n = len(_REPLICATED)

    def body(*refs):
        l_ref, ins, outs = refs[0], refs[1:1 + 3 * n], refs[1 + 3 * n:]
        total = l_ref[0]
        for p in range(1, N_DEV):
            total = total + l_ref[p]
        at = 0
        for a, (_, size) in enumerate(_REPLICATED):
            g = total[:, at:at + size]
            w_ref, m_ref, v_ref = ins[3 * a:3 * a + 3]
            g_ref, d_ref, m2_ref, v2_ref = outs[4 * a:4 * a + 4]
            g_ref[...] = g
            d_ref[...], m2_ref[...], v2_ref[...] = _adamw_step(g, w_ref[...], m_ref[...], v_ref[...])
            at += size
        outs[-1][...] = total[:, at:at + _LANES]

    operands, out_shapes = [], []
    for name, size in _REPLICATED:
        operands += [weights[name].reshape(1, size), moments_m[name].reshape(1, size), moments_v[name].reshape(1, size)]
        out_shapes += [jax.ShapeDtypeStruct((1, size), F32)] * 4
    out_shapes.append(jax.ShapeDtypeStruct((1, _LANES), F32))
    outs = pl.pallas_call(body, name="adamw_replicated", out_shape=out_shapes)(landing, *operands)
    return outs[-1][0, 0], {name: outs[4 * a:4 * a + 4] for a, (name, _) in enumerate(_REPLICATED)}


def _pad_rows(a, rows):
    return jnp.pad(a, ((0, rows - a.shape[0]), (0, 0)))


def _slabs(a):
    r, c = a.shape
    return a.reshape(r, N_DEV, c // N_DEV).transpose(1, 0, 2)


def _unslab(a):
    g, r, c = a.shape
    return a.transpose(1, 0, 2).reshape(r, g * c)


def _local_step(x, target, w, n_rows, ffn_weights, send_grads):
    cos_t, sin_t = lax.optimization_barrier(_rope_tables(n_rows))
    cos, sin = cos_t.T, sin_t.T
    meta_pad, g1, gf = w["meta_pad"], w["mix_norm_g"], w["final_norm_g"]
    gq, gkv, gb_col = w["q_norm_g"], w["kv_norm_g"], w["attn_out_g"].reshape(D_ATTN, 1)
    nb, ag, cq, ckv, kr = _fwd_in(x, meta_pad, g1, w["w_in"], n_rows)
    mix_a, u1 = _fwd_conv(ag, w["conv_w"], w["conv_b"], w["conv_ln_g"], w["conv_ln_b"], w["conv_out_g"], n_rows)
    q_t, k, v, v_t, cqn, ckvn = _fwd_qkv(cq, ckv, kr, gq, gkv, w["wq_t"], w["w_ukv"], w["wv_t"], cos, sin, cos_t, sin_t, n_rows)
    o_t, lse = _attn_fwd(q_t, k, v_t, n_rows)
    w_out, w_up, w_down = ffn_weights()
    mix_bt, h1 = _fwd_out(x, meta_pad, mix_a, o_t, gb_col, w_out, n_rows)
    n2, up0, act, da, db, dh2, loss, dgf = _fwd_ffn(
        h1, target, w["ffn_norm_g"], w_up, w["fw"], w["fb"], w_down, gf, n_rows)

    dup, dfb = _bwd_ffn_act(dh2, da, db, w_down, n_rows)
    dup0, dh1, dfw, dg2 = _bwd_ffn_up(dup, up0, h1, dh2, w["ffn_norm_g"], w_up, w["fw"], n_rows)
    grad_w_out = jnp.concatenate([_weight_grad(mix_a, dh1, "grad_w_out_conv")[0],
                                  _weight_grad(mix_bt, dh1, "grad_w_out_attn", a_transposed=True)[0]], axis=0)
    stage0 = {
        "w_ffn_up": _weight_grad(dup0, n2, "grad_w_ffn_up"),
        "w_ffn_down": _weight_grad(act, dh2, "grad_w_ffn_down").reshape(N_DEV, D_FF // N_DEV, D_MODEL),
        "w_out": grad_w_out.reshape(N_DEV, D_MODEL // N_DEV, D_MODEL),
    }
    stage0, dh1 = lax.optimization_barrier((stage0, dh1))
    send_grads(0, stage0)
    do_t, delta, du1, dgb, dga, dlg, dlb, dcb = _bwd_out(
        dh1, o_t, u1, w_out, gb_col, w["conv_ln_g"], w["conv_ln_b"], w["conv_out_g"], n_rows)
    dq_t, dk, dv = _attn_bwd(q_t, k, v, do_t, lse, delta, n_rows)
    dqraw_t, dkv, dcq, dckv, dkr, dgq, dgkv = _bwd_qkv(
        dq_t, dk, dv, cq, ckv, gq, gkv, w["wq_t"], w["w_ukv"], cos, sin, cos_t, sin_t, n_rows)
    dz, dcw = _bwd_conv(du1, ag, w["conv_w"], dcq, dckv, dkr, n_rows)
    stage1 = {
        "w_in": _weight_grad(dz, nb, "grad_w_in")[0].reshape(N_DEV, D_IN // N_DEV, D_MODEL),
        "w_uq": _weight_grad(dqraw_t.reshape(N_HEADS * QK_DIM, n_rows), cqn, "grad_w_uq", a_transposed=True)[0].reshape(
            N_HEADS, QK_DIM, Q_LORA),
        "w_ukv": _slabs(_weight_grad(ckvn, dkv, "grad_w_ukv")[0]),
        "conv_w": _slabs(dcw),
        "ffn_conv_w": dfw[:, :, :UP_SLAB],
    }
    stage1, dz = lax.optimization_barrier((stage1, dz))
    send_grads(1, stage1)
    gx, gmeta, dg1 = _bwd_in(dz, x, meta_pad, dh1, g1, w["w_in"], n_rows)

    sharded = {"meta_tokens": _slabs(gmeta[DEAD:])}
    replicated = {
        "mix_norm_g": dg1, "q_norm_g": dgq, "kv_norm_g": dgkv, "conv_b": dcb, "conv_ln_g": dlg, "conv_ln_b": dlb,
        "conv_out_g": dga, "attn_out_g": dgb, "ffn_norm_g": dg2, "ffn_conv_b": dfb, "final_norm_g": dgf,
    }
    return loss[0, 0], gx, sharded, replicated


_SHARDED = (
    ("w_in", None, BF16), ("w_uq", None, BF16), ("w_ukv", None, BF16), ("w_out", None, BF16), ("w_ffn_up", None, BF16),
    ("w_ffn_down", None, BF16), ("conv_w", 32, F32), ("ffn_conv_w", 8, F32), ("meta_tokens", None, F32),
)
GATHER_LATE_ID = 3
EXCHANGE_STAGE_IDS = (4, 5)
_LATE_WEIGHTS = ("w_out", "w_ffn_up", "w_ffn_down")
_COLUMN_SHARDS = ("w_in", "w_uq", "w_ffn_up")


def kernel(x, meta_tokens, mix_norm_g, w_in, q_norm_g, w_uq, kv_norm_g, w_ukv, conv_w, conv_b, conv_ln_g, conv_ln_b, conv_out_g, attn_out_g, w_out, ffn_norm_g, w_ffn_up, ffn_conv_w, ffn_conv_b, w_ffn_down, final_norm_g, loss_target, m_meta_tokens, m_mix_norm_g, m_w_in, m_q_norm_g, m_w_uq, m_kv_norm_g, m_w_ukv, m_conv_w, m_conv_b, m_conv_ln_g, m_conv_ln_b, m_conv_out_g, m_attn_out_g, m_w_out, m_ffn_norm_g, m_w_ffn_up, m_ffn_conv_w, m_ffn_conv_b, m_w_ffn_down, m_final_norm_g, v_meta_tokens, v_mix_norm_g, v_w_in, v_q_norm_g, v_w_uq, v_kv_norm_g, v_w_ukv, v_conv_w, v_conv_b, v_conv_ln_g, v_conv_ln_b, v_conv_out_g, v_attn_out_g, v_w_out, v_ffn_norm_g, v_w_ffn_up, v_ffn_conv_w, v_ffn_conv_b, v_w_ffn_down, v_final_norm_g):
    given = dict(locals())
    weights = {name: given[name] for name in _WEIGHT_ORDER}
    moments_m = {name: given["m_" + name] for name in _WEIGHT_ORDER}
    moments_v = {name: given["v_" + name] for name in _WEIGHT_ORDER}
    seq = x.shape[1]
    n_rows = ROW_TILE + seq

    def shard2d(name, a):
        a = a.reshape(a.shape[-2], a.shape[-1])
        return a.T if name in _COLUMN_SHARDS else a

    early = [entry for entry in _SHARDED if entry[0] not in _LATE_WEIGHTS]
    shards = []
    for name, pad_to, _ in early:
        s = shard2d(name, weights[name])
        shards.append(s if pad_to is None else _pad_rows(s, pad_to))
    gathered = dict(zip([name for name, _, _ in early], _all_gather(shards, [dt for _, _, dt in early])))
    behind = gathered["meta_tokens"][0, 0, 0] * 0.0
    late_parts = [(shard2d(name, weights[name]) + behind).astype(BF16) for name in _LATE_WEIGHTS]
    late = _sequencer_exchange(late_parts, [True] * len(late_parts), "gather_late", GATHER_LATE_ID)
    meta_full = _unslab(gathered["meta_tokens"])
    full = {
        "meta_pad": jnp.concatenate([jnp.zeros((DEAD, D_MODEL), F32), meta_full], axis=0),
        "w_in": gathered["w_in"].reshape(D_IN, D_MODEL),
        "wq_t": gathered["w_uq"],
        "w_ukv": gathered["w_ukv"],
        "wv_t": gathered["w_ukv"][:, :, QK_NOPE:].transpose(0, 2, 1),
        "conv_w": _unslab(gathered["conv_w"][:, :CONV_WIDTH]),
        "fw": jnp.pad(gathered["ffn_conv_w"][:, :FFN_CONV_WIDTH], ((0, 0), (0, 0), (0, UP_PAD - UP_SLAB))),
        "fb": jnp.pad(ffn_conv_b.reshape(N_DEV, 1, UP_SLAB), ((0, 0), (0, 0), (0, UP_PAD - UP_SLAB))),
        "final_norm_g": final_norm_g.reshape(1, D_MODEL),
    }
    for name in ("mix_norm_g", "q_norm_g", "kv_norm_g", "conv_b", "conv_ln_g", "conv_ln_b", "conv_out_g", "attn_out_g",
                 "ffn_norm_g"):
        full[name] = weights[name]

    def ffn_weights():
        w_out_all, w_up_all, w_down_all = late
        return (w_out_all.reshape(D_MODEL, D_MODEL), w_up_all, w_down_all.reshape(N_ACT_SLAB, UP_SLAB, D_MODEL))

    wire = {name: (pad_to, dt) for name, pad_to, dt in _SHARDED}
    landing = {}

    def on_the_wire(name, slabs):
        pad_to, dt = wire[name]
        slabs = slabs.astype(dt)
        return slabs if pad_to is None else jnp.pad(slabs, ((0, 0), (0, pad_to - slabs.shape[1]), (0, 0)))

    def send_grads(stage, grads):
        parts = [on_the_wire(name, slabs) for name, slabs in grads.items()]
        if landing:
            arrived = list(landing)
            parts, held = lax.optimization_barrier((parts, [landing[name] for name in arrived]))
            landing.update(zip(arrived, held))
        landed = _sequencer_exchange(parts, [False] * len(parts), f"exchange_stage{stage}", EXCHANGE_STAGE_IDS[stage])
        landing.update(zip(grads, landed))

    loss, gx, sharded, replicated = _local_step(x[0], loss_target[0], full, n_rows, ffn_weights, send_grads)

    parts = [on_the_wire(name, slabs) for name, slabs in sharded.items()] + [_pack_replicated(replicated, loss)]
    landed = _exchange(parts, [False] * len(sharded) + [True])
    landing.update(zip(sharded, landed[:-1]))

    grad, delta, new_m, new_v = {}, {}, {}, {}
    for name, pad_to, _ in _SHARDED:
        land = landing[name]
        ws, ms, vs = (shard2d(name, a[name]) for a in (weights, moments_m, moments_v))
        rows = ws.shape[0]
        if pad_to is not None:
            ws, ms, vs = _pad_rows(ws, pad_to), _pad_rows(ms, pad_to), _pad_rows(vs, pad_to)
        outs = _adamw(land, ws, ms, vs, "adamw_" + name)
        shape = weights[name].shape
        grad[name], delta[name], new_m[name], new_v[name] = (
            (o.T if name in _COLUMN_SHARDS else o[:rows]).reshape(shape) for o in outs)
    loss, updates = _adamw_replicated(landed[-1], weights, moments_m, moments_v)
    for name, outs in updates.items():
        grad[name], delta[name], new_m[name], new_v[name] = (o.reshape(weights[name].shape) for o in outs)

    return (loss, gx[None], *[grad[n] for n in _WEIGHT_ORDER], *[delta[n] for n in _WEIGHT_ORDER],
            *[new_m[n] for n in _WEIGHT_ORDER], *[new_v[n] for n in _WEIGHT_ORDER])
```

```python
import jax
import jax.numpy as jnp
from jax import lax
from jax.experimental import pallas as pl
from jax.experimental.pallas import tpu as pltpu
from jax.experimental.pallas import tpu_sc as plsc

F32 = jnp.float32
BF16 = jnp.bfloat16

N_DEV = 8
D_MODEL = 1024
CHUNK = 64
CHUNK_SHIFT = 6
N_META = 16
D_CONV = 512
CONV_WIDTH = 31
N_HEADS = 8
QK_NOPE = 64
QK_ROPE = 32
QK_DIM = QK_NOPE + QK_ROPE
V_HEAD = 64
KV_HEAD = QK_NOPE + V_HEAD
D_ATTN = N_HEADS * V_HEAD
Q_LORA = 384
KV_LORA = 256
ROPE_THETA = 10000.0
D_IN = 2 * D_CONV + Q_LORA + KV_LORA + QK_ROPE
D_FF = 2816
D_UP = 2 * D_FF
FFN_CONV_WIDTH = 3
UP_SLAB = D_UP // N_DEV
N_ACT_SLAB = D_FF // UP_SLAB
EPS = 1e-6
NEG = -1e30
_LN2 = 0.6931471805599453
QK_LOGIT_SCALE = QK_DIM ** -0.5 / _LN2
ADAM_LR = 0.001
ADAM_B1 = 0.9
ADAM_B2 = 0.999
ADAM_EPS = 1e-08
ADAM_WD = 0.01
ADAM_STEP = 10

ROW_TILE = 256
DEAD = ROW_TILE - N_META
CONV_HALO = 32
FFN_HALO = 16
VMEM_LIMIT = 56 * 1024 * 1024
_LANES = 128

MESH = pl.DeviceIdType.MESH


def _dot(a, b):
    return jnp.dot(a, b, preferred_element_type=F32)


def _dot_nt(a, b):
    return lax.dot_general(a, b, (((1,), (1,)), ((), ())), preferred_element_type=F32)


def _dot_tn(a, b):
    return lax.dot_general(a, b, (((0,), (0,)), ((), ())), preferred_element_type=F32)


def _sigmoid(x):
    return 1.0 / (1.0 + jnp.exp2(x * (-1.0 / _LN2)))


def _rms_fwd(x, g):
    r = lax.rsqrt(jnp.mean(x * x, axis=-1, keepdims=True) + EPS)
    return x * r * g


def _rms_bwd(dy, x, g):
    r = lax.rsqrt(jnp.mean(x * x, axis=-1, keepdims=True) + EPS)
    w = dy * g
    dx = r * w - x * (r * r * r) * jnp.mean(w * x, axis=-1, keepdims=True)
    return dx, jnp.sum(dy * x * r, axis=0, keepdims=True)


def _rope(x, cos, sin):
    half = QK_ROPE // 2
    x1, x2 = x[:, :half], x[:, half:]
    return jnp.concatenate([x1 * cos - x2 * sin, x2 * cos + x1 * sin], axis=-1)


def _rope_t(dy, cos, sin):
    half = QK_ROPE // 2
    d1, d2 = dy[:, :half], dy[:, half:]
    return jnp.concatenate([d1 * cos + d2 * sin, d2 * cos - d1 * sin], axis=-1)


def _row_ids(i, rows):
    return i * rows + lax.broadcasted_iota(jnp.int32, (rows, 1), 0)


def _accumulate(ref, first, value):
    @pl.when(first)
    def _():
        ref[...] = value

    @pl.when(jnp.logical_not(first))
    def _():
        ref[...] += value


def _tile_spec(shape):
    nd = len(shape)
    if nd == 2:
        return pl.BlockSpec((ROW_TILE, shape[1]), lambda i: (i, 0))
    return pl.BlockSpec((shape[0], ROW_TILE, shape[2]), lambda i: (0, i, 0))


def _whole_spec(shape):
    nd = len(shape)
    return pl.BlockSpec(tuple(shape), lambda i: (0,) * nd, pipeline_mode=pl.Buffered(1))


def _acc_spec(shape):
    nd = len(shape)
    return pl.BlockSpec(tuple(shape), lambda i: (0,) * nd)


def _real_spec(width):
    return pl.BlockSpec((ROW_TILE, width), lambda i: (jnp.maximum(i - 1, 0), 0))


def _params(*semantics):
    return pltpu.CompilerParams(dimension_semantics=semantics, vmem_limit_bytes=VMEM_LIMIT)


def _fwd_in(x, meta_pad, g1, w_in, n_rows):
    nt = n_rows // ROW_TILE

    def body(x_ref, meta_ref, g_ref, w_ref, nb_ref, ag_ref, cq_ref, ckv_ref, kr_ref):
        i = pl.program_id(0)
        h0 = jnp.where(i == 0, meta_ref[...], x_ref[...])
        nb = _rms_fwd(h0, g_ref[...]).astype(BF16)
        nb_ref[...] = nb
        z = _dot_nt(nb, w_ref[...])
        ag_ref[...] = z[:, :2 * D_CONV]
        cq_ref[...] = z[:, 2 * D_CONV:2 * D_CONV + Q_LORA]
        ckv_ref[...] = z[:, 2 * D_CONV + Q_LORA:2 * D_CONV + Q_LORA + KV_LORA]
        kr_ref[...] = z[:, 2 * D_CONV + Q_LORA + KV_LORA:]

    out_shapes = [
        jax.ShapeDtypeStruct((n_rows, D_MODEL), BF16),
        jax.ShapeDtypeStruct((n_rows, 2 * D_CONV), F32),
        jax.ShapeDtypeStruct((n_rows, Q_LORA), F32),
        jax.ShapeDtypeStruct((n_rows, KV_LORA), F32),
        jax.ShapeDtypeStruct((n_rows, QK_ROPE), F32),
    ]
    return pl.pallas_call(
        body, name="fwd_in", grid=(nt,),
        in_specs=[_real_spec(D_MODEL), _whole_spec(meta_pad.shape), _whole_spec(g1.shape), _whole_spec(w_in.shape)],
        out_specs=[_tile_spec(s.shape) for s in out_shapes],
        out_shape=out_shapes,
        compiler_params=_params("parallel"),
    )(x, meta_pad, g1, w_in)


def _conv_chain(u1, ln_g, ln_b):
    mu = jnp.mean(u1, axis=-1, keepdims=True)
    xc = u1 - mu
    rstd = lax.rsqrt(jnp.mean(xc * xc, axis=-1, keepdims=True) + EPS)
    xh = xc * rstd
    u2 = xh * ln_g + ln_b
    return xh, u2, u2 * _sigmoid(u2), rstd


def _fwd_conv(ag, conv_w, conv_b, ln_g, ln_b, out_g, n_rows):
    nt = n_rows // ROW_TILE

    def body(ag_ref, w_ref, b_ref, lg_ref, lb_ref, og_ref, mix_ref, u1_ref, ext_ref, conv_ref):
        i = pl.program_id(0)

        @pl.when(i == 0)
        def _():
            ext_ref[:, 0:CONV_HALO, :] = jnp.zeros((CONV_PLANES, CONV_HALO, _LANES), F32)

        ag_t = ag_ref[...]
        live = _row_ids(i, ROW_TILE) >= DEAD
        u0 = jnp.where(live, ag_t[:, :D_CONV] * _sigmoid(ag_t[:, D_CONV:]), 0.0)
        _to_planes(ext_ref, (), slice(CONV_HALO, None), u0)
        first = CONV_HALO - (CONV_WIDTH - 1)
        for c in range(CONV_PLANES):
            taps = w_ref[:, c * _LANES:(c + 1) * _LANES]
            for p in range(PHASES):
                acc = jnp.zeros((PHASE_ROWS, _LANES), F32)
                for k in range(CONV_WIDTH):
                    acc = acc + taps[k:k + 1, :] * ext_ref[c, _phase(first + k + p), :]
                conv_ref[c, _phase(p), :] = acc
        ext_ref[:, 0:CONV_HALO, :] = ext_ref[:, ROW_TILE:ROW_TILE + CONV_HALO, :]
        u1 = _from_planes(conv_ref, (), D_CONV) + b_ref[...]
        u1_ref[...] = u1
        _, _, u3, _ = _conv_chain(u1, lg_ref[...], lb_ref[...])
        mix_ref[...] = _rms_fwd(u3, og_ref[...]).astype(BF16)

    out_shapes = [jax.ShapeDtypeStruct((n_rows, D_CONV), BF16), jax.ShapeDtypeStruct((n_rows, D_CONV), F32)]
    small = [conv_w, conv_b, ln_g, ln_b, out_g]
    return pl.pallas_call(
        body, name="fwd_conv", grid=(nt,),
        in_specs=[_tile_spec(ag.shape)] + [_whole_spec(a.shape) for a in small],
        out_specs=[_tile_spec(s.shape) for s in out_shapes],
        out_shape=out_shapes,
        scratch_shapes=[pltpu.VMEM((CONV_PLANES, ROW_TILE + CONV_HALO, _LANES), F32),
                        pltpu.VMEM((CONV_PLANES, ROW_TILE, _LANES), F32)],
        compiler_params=_params("arbitrary"),
    )(ag, *small)


def _lane_tile(shape):
    if len(shape) == 2:
        return pl.BlockSpec((shape[0], ROW_TILE), lambda i: (0, i))
    return pl.BlockSpec((shape[0], shape[1], ROW_TILE), lambda i: (0, 0, i))


def _rope_rows(x, cos, sin):
    half = QK_ROPE // 2
    x1, x2 = x[:half], x[half:]
    return jnp.concatenate([x1 * cos - x2 * sin, x2 * cos + x1 * sin], axis=0)


def _rope_rows_t(dy, cos, sin):
    half = QK_ROPE // 2
    d1, d2 = dy[:half], dy[half:]
    return jnp.concatenate([d1 * cos + d2 * sin, d2 * cos - d1 * sin], axis=0)


def _fwd_qkv(cq, ckv, kr, gq, gkv, wq_t, w_ukv, wv_t, cos, sin, cos_t, sin_t, n_rows):
    nt = n_rows // ROW_TILE

    def body(cq_ref, ckv_ref, kr_ref, gq_ref, gkv_ref, wqt_ref, wkv_ref, wvt_ref, cos_ref, sin_ref, cost_ref, sint_ref,
             qt_ref, k_ref, v_ref, vt_ref, cqn_ref, ckvn_ref):
        cqn = _rms_fwd(cq_ref[...], gq_ref[...]).astype(BF16)
        ckvn = _rms_fwd(ckv_ref[...], gkv_ref[...]).astype(BF16)
        cqn_ref[...] = cqn
        ckvn_ref[...] = ckvn
        k_rot = _rope(kr_ref[...], cos_ref[...], sin_ref[...])
        cos_rows, sin_rows = cost_ref[...], sint_ref[...]
        q_all = _dot_nt(wqt_ref[...].reshape(N_HEADS * QK_DIM, Q_LORA), cqn)
        vt_all = _dot_nt(wvt_ref[...].reshape(N_HEADS * V_HEAD, KV_LORA), ckvn).astype(BF16)
        for h in range(N_HEADS):
            q_raw = q_all[h * QK_DIM:(h + 1) * QK_DIM]
            q_h = jnp.concatenate([q_raw[:QK_NOPE], _rope_rows(q_raw[QK_NOPE:], cos_rows, sin_rows)], axis=0)
            qt_ref[h] = (q_h * QK_LOGIT_SCALE).astype(BF16)
            kv = _dot(ckvn, wkv_ref[h])
            k_ref[h] = jnp.concatenate([kv[:, :QK_NOPE], k_rot], axis=-1).astype(BF16)
            v_ref[h] = kv[:, QK_NOPE:].astype(BF16)
            vt_ref[h] = vt_all[h * V_HEAD:(h + 1) * V_HEAD]

    out_shapes = [
        jax.ShapeDtypeStruct((N_HEADS, QK_DIM, n_rows), BF16),
        jax.ShapeDtypeStruct((N_HEADS, n_rows, QK_DIM), BF16),
        jax.ShapeDtypeStruct((N_HEADS, n_rows, V_HEAD), BF16),
        jax.ShapeDtypeStruct((N_HEADS, V_HEAD, n_rows), BF16),
        jax.ShapeDtypeStruct((n_rows, Q_LORA), BF16),
        jax.ShapeDtypeStruct((n_rows, KV_LORA), BF16),
    ]
    tiles = [cq, ckv, kr]
    whole = [gq, gkv, wq_t, w_ukv, wv_t]
    out_specs = [_lane_tile(out_shapes[0].shape), _tile_spec(out_shapes[1].shape), _tile_spec(out_shapes[2].shape),
                 _lane_tile(out_shapes[3].shape), _tile_spec(out_shapes[4].shape), _tile_spec(out_shapes[5].shape)]
    return pl.pallas_call(
        body, name="fwd_qkv", grid=(nt,),
        in_specs=[_tile_spec(a.shape) for a in tiles] + [_whole_spec(a.shape) for a in whole]
        + [_tile_spec(cos.shape), _tile_spec(sin.shape), _lane_tile(cos_t.shape), _lane_tile(sin_t.shape)],
        out_specs=out_specs,
        out_shape=out_shapes,
        compiler_params=_params("parallel"),
    )(*tiles, *whole, cos, sin, cos_t, sin_t)


def _chunk_of(rows):
    return jnp.where(rows >= ROW_TILE, lax.shift_right_arithmetic(rows - ROW_TILE, CHUNK_SHIFT) + 1, 0)


def _visible(i, j):
    k_rows = j * ROW_TILE + lax.broadcasted_iota(jnp.int32, (ROW_TILE, 1), 0)
    q_rows = i * ROW_TILE + lax.broadcasted_iota(jnp.int32, (1, ROW_TILE), 1)
    return jnp.logical_and(_chunk_of(q_rows) >= _chunk_of(k_rows), k_rows >= DEAD)


def _attn_fwd(q_t, k, v_t, n_rows):
    nt = n_rows // ROW_TILE

    def body(qt_ref, k_ref, vt_ref, ot_ref, lse_ref, max_ref, sum_ref):
        i = pl.program_id(0)
        q_ts = [qt_ref[h] for h in range(N_HEADS)]

        def key_rows(j):
            return pl.ds(pl.multiple_of(j * ROW_TILE, ROW_TILE), ROW_TILE)

        def make_step(masked, tiles, first=0):
            def step(t, carry):
                js = [first + tiles * t + u for u in range(tiles)]
                scores = [[_dot(k_ref[h, key_rows(j), :], q_ts[h]) for h in range(N_HEADS)] for j in js]
                for j, tile_scores in zip(js, scores):
                    visible = _visible(i, j) if masked else None
                    probs, alphas = [], []
                    for h in range(N_HEADS):
                        m = max_ref[h]
                        s = jnp.where(visible, tile_scores[h], NEG) if masked else tile_scores[h]
                        m_new = jnp.maximum(m, jnp.max(s, axis=0, keepdims=True))
                        alpha = jnp.exp2(m - m_new)
                        p = jnp.exp2(s - m_new)
                        probs.append(p.astype(BF16))
                        alphas.append(alpha)
                        max_ref[h] = m_new
                        sum_ref[h] = alpha * sum_ref[h] + jnp.sum(p, axis=0, keepdims=True)
                    for h in range(N_HEADS):
                        ot_ref[h] = alphas[h] * ot_ref[h] + _dot(vt_ref[h, :, key_rows(j)], probs[h])
                return carry
            return step

        max_ref[...] = jnp.full(max_ref.shape, NEG, F32)
        sum_ref[...] = jnp.zeros_like(sum_ref)
        ot_ref[...] = jnp.zeros_like(ot_ref)
        between = jnp.maximum(i - 1, 0)
        quads = lax.shift_right_logical(between, 2)
        pairs = jnp.bitwise_and(lax.shift_right_logical(between, 1), 1)
        make_step(True, 1)(0, 0)
        lax.fori_loop(0, quads, make_step(False, 4, first=1), 0)
        lax.fori_loop(0, pairs, make_step(False, 2, first=1 + 4 * quads), 0)
        lax.fori_loop(1 + 4 * quads + 2 * pairs, i, make_step(False, 1), 0)
        lax.fori_loop(jnp.maximum(i, 1), i + 1, make_step(True, 1), 0)
        for h in range(N_HEADS):
            l = sum_ref[h]
            ot_ref[h] = ot_ref[h] / l
            lse_ref[h] = max_ref[h] + jnp.log2(l)

    out_shapes = [jax.ShapeDtypeStruct((N_HEADS, V_HEAD, n_rows), F32), jax.ShapeDtypeStruct((N_HEADS, 1, n_rows), F32)]
    return pl.pallas_call(
        body, name="attn_fwd", grid=(nt,),
        in_specs=[_lane_tile(q_t.shape), _whole_spec(k.shape), _whole_spec(v_t.shape)],
        out_specs=[_lane_tile(s.shape) for s in out_shapes],
        out_shape=out_shapes,
        scratch_shapes=[pltpu.VMEM((N_HEADS, 1, ROW_TILE), F32), pltpu.VMEM((N_HEADS, 1, ROW_TILE), F32)],
        compiler_params=_params("parallel"),
    )(q_t, k, v_t)


def _heads_to_rows(ref):
    return jnp.concatenate([ref[h] for h in range(N_HEADS)], axis=0)


def _rms_cols(x, g_col):
    r = lax.rsqrt(jnp.mean(x * x, axis=0, keepdims=True) + EPS)
    return x * r * g_col


def _fwd_out(x, meta_pad, mix_a, o_t, gb_col, w_out, n_rows):
    nt = n_rows // ROW_TILE

    def body(x_ref, meta_ref, mixa_ref, ot_ref, gb_ref, w_ref, mixbt_ref, h1_ref):
        i = pl.program_id(0)
        h0 = jnp.where(i == 0, meta_ref[...], x_ref[...])
        mix_bt = _rms_cols(_heads_to_rows(ot_ref), gb_ref[...]).astype(BF16)
        mixbt_ref[...] = mix_bt
        h1_ref[...] = h0 + _dot(mixa_ref[...], w_ref[:D_CONV, :]) + _dot_tn(mix_bt, w_ref[D_CONV:, :])

    out_shapes = [jax.ShapeDtypeStruct((D_ATTN, n_rows), BF16), jax.ShapeDtypeStruct((n_rows, D_MODEL), F32)]
    return pl.pallas_call(
        body, name="fwd_out", grid=(nt,),
        in_specs=[_real_spec(D_MODEL), _whole_spec(meta_pad.shape), _tile_spec(mix_a.shape), _lane_tile(o_t.shape),
                  _whole_spec(gb_col.shape), _whole_spec(w_out.shape)],
        out_specs=[_lane_tile(out_shapes[0].shape), _tile_spec(out_shapes[1].shape)],
        out_shape=out_shapes,
        compiler_params=_params("parallel"),
    )(x, meta_pad, mix_a, o_t, gb_col, w_out)


PHASES = 8
PHASE_ROWS = ROW_TILE // PHASES
UP_PLANES = -(-UP_SLAB // _LANES)
UP_PAD = UP_PLANES * _LANES
CONV_PLANES = D_CONV // _LANES


def _phase(start):
    return pl.ds(start, PHASE_ROWS, stride=PHASES)


def _to_planes(ref, lead, rows, value):
    width = value.shape[-1]
    for c in range(-(-width // _LANES)):
        part = value[:, c * _LANES:min((c + 1) * _LANES, width)]
        if part.shape[-1] < _LANES:
            part = jnp.concatenate([part, jnp.zeros((part.shape[0], _LANES - part.shape[-1]), part.dtype)], axis=-1)
        ref[(*lead, c, rows, slice(None))] = part


def _from_planes(ref, lead, width):
    planes = [ref[(*lead, c)] for c in range(-(-width // _LANES))]
    last = width - (len(planes) - 1) * _LANES
    return jnp.concatenate(planes[:-1] + [planes[-1][:, :last]], axis=-1)


def _fwd_ffn(h1, target, g2, w_up, fw, fb, w_down, gf, n_rows):
    nt = n_rows // ROW_TILE

    def body(h1_ref, t_ref, g2_ref, wup_ref, fw_ref, fb_ref, wdn_ref, gf_ref,
             n2_ref, up0_ref, act_ref, da_ref, db_ref, dh2_ref, loss_ref, dgf_ref, ext_ref):
        i = pl.program_id(0)

        @pl.when(i == 0)
        def _():
            ext_ref[:, 0:FFN_HALO, :] = jnp.zeros((N_DEV, FFN_HALO, UP_SLAB), F32)

        h1_t = h1_ref[...]
        live = _row_ids(i, ROW_TILE) >= DEAD
        n2 = jnp.where(live, _rms_fwd(h1_t, g2_ref[...]), 0.0).astype(BF16)
        n2_ref[...] = n2
        for s in range(N_DEV):
            up0 = _dot_nt(n2, wup_ref[s])
            up0_ref[s] = up0.astype(BF16)
            ext_ref[s, FFN_HALO:, :] = up0
        first = FFN_HALO - (FFN_CONV_WIDTH - 1)

        def conv(s):
            block = ext_ref[s]
            acc = fb_ref[s, :, :UP_SLAB] + fw_ref[s, FFN_CONV_WIDTH - 1:FFN_CONV_WIDTH, :UP_SLAB] * block[FFN_HALO:]
            for back in range(1, FFN_CONV_WIDTH):
                k = FFN_CONV_WIDTH - 1 - back
                acc = acc + fw_ref[s, k:k + 1, :UP_SLAB] * pltpu.roll(block, back, 0)[FFN_HALO:]
            return acc

        h2 = h1_t
        for s in range(N_ACT_SLAB):
            gate = conv(s)
            val = conv(s + N_ACT_SLAB)
            sg = _sigmoid(gate)
            silu = gate * sg
            act = (silu * val).astype(BF16)
            act_ref[s] = act
            da_ref[s] = (val * sg * (1.0 + gate * (1.0 - sg))).astype(BF16)
            db_ref[s] = silu.astype(BF16)
            h2 = h2 + _dot(act, wdn_ref[s])
        ext_ref[:, 0:FFN_HALO, :] = ext_ref[:, ROW_TILE:ROW_TILE + FFN_HALO, :]

        gf_t = gf_ref[...]
        y = _rms_fwd(h2, gf_t)
        diff = jnp.where(i >= 1, y - t_ref[...], 0.0)
        tile_loss = 0.5 * jnp.sum(jnp.sum(diff * diff, axis=-1, keepdims=True), axis=0, keepdims=True) / D_MODEL
        dh2, dgf = _rms_bwd(diff / D_MODEL, h2, gf_t)
        dh2_ref[...] = dh2
        _accumulate(loss_ref, i == 0, jnp.broadcast_to(tile_loss, loss_ref.shape))
        _accumulate(dgf_ref, i == 0, dgf)

    act_like = jax.ShapeDtypeStruct((N_ACT_SLAB, n_rows, UP_SLAB), BF16)
    out_shapes = [
        jax.ShapeDtypeStruct((n_rows, D_MODEL), BF16),
        jax.ShapeDtypeStruct((N_DEV, n_rows, UP_SLAB), BF16),
        act_like, act_like, act_like,
        jax.ShapeDtypeStruct((n_rows, D_MODEL), F32),
        jax.ShapeDtypeStruct((8, 128), F32),
        jax.ShapeDtypeStruct((1, D_MODEL), F32),
    ]
    whole = [g2, w_up, fw, fb, w_down, gf]
    return pl.pallas_call(
        body, name="fwd_ffn", grid=(nt,),
        in_specs=[_tile_spec(h1.shape), _real_spec(D_MODEL)] + [_whole_spec(a.shape) for a in whole],
        out_specs=[_tile_spec(s.shape) for s in out_shapes[:6]] + [_acc_spec(s.shape) for s in out_shapes[6:]],
        out_shape=out_shapes,
        scratch_shapes=[pltpu.VMEM((N_DEV, ROW_TILE + FFN_HALO, UP_SLAB), F32)],
        compiler_params=_params("arbitrary"),
    )(h1, target, *whole)


def _rope_tables(n_rows):
    pos = jnp.maximum(jnp.arange(n_rows, dtype=jnp.int32) - DEAD, 0)
    inv_freq = 1.0 / (ROPE_THETA ** (jnp.arange(0, QK_ROPE, 2, dtype=F32) / QK_ROPE))
    ang_t = inv_freq[:, None] * pos.astype(F32)[None, :]
    return jnp.cos(ang_t), jnp.sin(ang_t)


def _halo_after(shape, halo, n_rows):
    last = n_rows // halo - 1
    step = ROW_TILE // halo
    if len(shape) == 2:
        return pl.BlockSpec((halo, shape[1]), lambda i: (jnp.minimum((i + 1) * step, last), 0))
    return pl.BlockSpec((shape[0], halo, shape[2]), lambda i: (0, jnp.minimum((i + 1) * step, last), 0))


def _bwd_ffn_act(dh2, da, db, w_down, n_rows):
    nt = n_rows // ROW_TILE

    def body(dh2_ref, da_ref, db_ref, wdn_ref, dup_ref, dfb_ref):
        i = pl.program_id(0)

        @pl.when(i == 0)
        def _():
            dfb_ref[...] = jnp.zeros_like(dfb_ref)

        dh2_b = dh2_ref[...].astype(BF16)
        for s in range(N_ACT_SLAB):
            d_act = _dot_nt(dh2_b, wdn_ref[s])
            d_gate = d_act * da_ref[s].astype(F32)
            d_val = d_act * db_ref[s].astype(F32)
            dup_ref[s] = d_gate.astype(BF16)
            dup_ref[s + N_ACT_SLAB] = d_val.astype(BF16)
            dfb_ref[s] += jnp.sum(d_gate, axis=0, keepdims=True)
            dfb_ref[s + N_ACT_SLAB] += jnp.sum(d_val, axis=0, keepdims=True)

    out_shapes = [jax.ShapeDtypeStruct((N_DEV, n_rows, UP_SLAB), BF16), jax.ShapeDtypeStruct((N_DEV, 1, UP_SLAB), F32)]
    return pl.pallas_call(
        body, name="bwd_ffn_act", grid=(nt,),
        in_specs=[_tile_spec(dh2.shape), _tile_spec(da.shape), _tile_spec(db.shape), _whole_spec(w_down.shape)],
        out_specs=[_tile_spec(out_shapes[0].shape), _acc_spec(out_shapes[1].shape)],
        out_shape=out_shapes,
        compiler_params=_params("arbitrary"),
    )(dh2, da, db, w_down)


def _bwd_ffn_up(dup, up0, h1, dh2, g2, w_up, fw, n_rows):
    nt = n_rows // ROW_TILE
    last_tap = FFN_CONV_WIDTH - 1
    ext_rows = ROW_TILE + FFN_HALO

    def body(dup_ref, dnext_ref, up0_ref, h1_ref, dh2_ref, g2_ref, wup_ref, fw_ref,
             dup0_ref, dh1_ref, dfw_ref, dg2_ref):
        i = pl.program_id(0)

        @pl.when(i == 0)
        def _():
            dfw_ref[...] = jnp.zeros_like(dfw_ref)

        live = _row_ids(i, ROW_TILE) >= DEAD
        dn2 = jnp.zeros((ROW_TILE, D_MODEL), F32)
        for s in range(N_DEV):
            d = dup_ref[s].astype(F32)
            block = jnp.concatenate([d, jnp.where(i == nt - 1, 0.0, dnext_ref[s].astype(F32))], axis=0)
            u = up0_ref[s].astype(F32)
            dup0 = fw_ref[s, last_tap:last_tap + 1, :UP_SLAB] * d
            dfw_ref[s, last_tap:last_tap + 1, :UP_SLAB] += jnp.sum(d * u, axis=0, keepdims=True)
            for ahead in range(1, FFN_CONV_WIDTH):
                k = last_tap - ahead
                shifted = pltpu.roll(block, ext_rows - ahead, 0)[:ROW_TILE]
                dup0 = dup0 + fw_ref[s, k:k + 1, :UP_SLAB] * shifted
                dfw_ref[s, k:k + 1, :UP_SLAB] += jnp.sum(shifted * u, axis=0, keepdims=True)
            dup0_b = jnp.where(live, dup0, 0.0).astype(BF16)
            dup0_ref[s] = dup0_b
            dn2 = dn2 + _dot(dup0_b, wup_ref[s])
        dx, dg2 = _rms_bwd(dn2, h1_ref[...], g2_ref[...])
        dh1_ref[...] = dh2_ref[...] + dx
        _accumulate(dg2_ref, i == 0, dg2)

    out_shapes = [
        jax.ShapeDtypeStruct((N_DEV, n_rows, UP_SLAB), BF16),
        jax.ShapeDtypeStruct((n_rows, D_MODEL), F32),
        jax.ShapeDtypeStruct((N_DEV, FFN_CONV_WIDTH, UP_PAD), F32),
        jax.ShapeDtypeStruct((1, D_MODEL), F32),
    ]
    return pl.pallas_call(
        body, name="bwd_ffn_up", grid=(nt,),
        in_specs=[_tile_spec(dup.shape), _halo_after(dup.shape, FFN_HALO, n_rows), _tile_spec(up0.shape),
                  _tile_spec(h1.shape), _tile_spec(dh2.shape),
                  _whole_spec(g2.shape), _whole_spec(w_up.shape), _whole_spec(fw.shape)],
        out_specs=[_tile_spec(s.shape) for s in out_shapes[:2]] + [_acc_spec(s.shape) for s in out_shapes[2:]],
        out_shape=out_shapes,
        compiler_params=_params("arbitrary"),
    )(dup, dup, up0, h1, dh2, g2, w_up, fw)


def _bwd_out(dh1, o_t, u1, w_out, gb_col, ln_g, ln_b, ga, n_rows):
    nt = n_rows // ROW_TILE

    def body(dh1_ref, ot_ref, u1_ref, w_ref, gb_ref, lg_ref, lb_ref, ga_ref,
             dot_ref, delta_ref, du1_ref, dgb_ref, dga_ref, dlg_ref, dlb_ref, dcb_ref):
        i = pl.program_id(0)
        dh1_b = dh1_ref[...].astype(BF16)
        o_t = _heads_to_rows(ot_ref)
        gb = gb_ref[...]
        r = lax.rsqrt(jnp.mean(o_t * o_t, axis=0, keepdims=True) + EPS)
        dmix_bt = _dot_nt(w_ref[D_CONV:, :], dh1_b)
        wgt = dmix_bt * gb
        do_t = r * wgt - o_t * (r * r * r) * jnp.mean(wgt * o_t, axis=0, keepdims=True)
        dgb = jnp.sum(dmix_bt * o_t * r, axis=1, keepdims=True)
        for h in range(N_HEADS):
            do_h = do_t[h * V_HEAD:(h + 1) * V_HEAD]
            dot_ref[h] = do_h.astype(BF16)
            delta_ref[h] = jnp.sum(do_h * ot_ref[h], axis=0, keepdims=True)
        lg = lg_ref[...]
        xh, u2, u3, rstd = _conv_chain(u1_ref[...], lg, lb_ref[...])
        du3, dga = _rms_bwd(_dot_nt(dh1_b, w_ref[:D_CONV, :]), u3, ga_ref[...])
        sg = _sigmoid(u2)
        du2 = du3 * sg * (1.0 + u2 * (1.0 - sg))
        dxh = du2 * lg
        du1 = rstd * (dxh - jnp.mean(dxh, axis=-1, keepdims=True) - xh * jnp.mean(dxh * xh, axis=-1, keepdims=True))
        du1_ref[...] = du1
        first = i == 0
        _accumulate(dgb_ref, first, dgb)
        _accumulate(dga_ref, first, dga)
        _accumulate(dlg_ref, first, jnp.sum(du2 * xh, axis=0, keepdims=True))
        _accumulate(dlb_ref, first, jnp.sum(du2, axis=0, keepdims=True))
        _accumulate(dcb_ref, first, jnp.sum(du1, axis=0, keepdims=True))

    out_shapes = [
        jax.ShapeDtypeStruct((N_HEADS, V_HEAD, n_rows), BF16),
        jax.ShapeDtypeStruct((N_HEADS, 1, n_rows), F32),
        jax.ShapeDtypeStruct((n_rows, D_CONV), F32),
        jax.ShapeDtypeStruct((D_ATTN, 1), F32),
    ] + [jax.ShapeDtypeStruct((1, D_CONV), F32)] * 4
    whole = [w_out, gb_col, ln_g, ln_b, ga]
    return pl.pallas_call(
        body, name="bwd_out", grid=(nt,),
        in_specs=[_tile_spec(dh1.shape), _lane_tile(o_t.shape), _tile_spec(u1.shape)] + [_whole_spec(a.shape) for a in whole],
        out_specs=[_lane_tile(out_shapes[0].shape), _lane_tile(out_shapes[1].shape), _tile_spec(out_shapes[2].shape)]
        + [_acc_spec(s.shape) for s in out_shapes[3:]],
        out_shape=out_shapes,
        compiler_params=_params("arbitrary"),
    )(dh1, o_t, u1, *whole)


ATTN_BWD_HEADS = 8


def _attn_bwd(q_t, k, v, do_t, lse, delta, n_rows):
    nt = n_rows // ROW_TILE
    hp = ATTN_BWD_HEADS

    def body(k_ref, v_ref, qt_ref, dot_ref, lse_ref, delta_ref, dqt_ref, dk_ref, dv_ref):
        j = pl.program_id(1)

        @pl.when(j == 0)
        def _():
            dqt_ref[...] = jnp.zeros_like(dqt_ref)

        k_ts = [k_ref[h] for h in range(hp)]
        v_ts = [v_ref[h] for h in range(hp)]

        def make_step(masked, tiles, first=0):
            def step(t, carry):
                tiles_of_step = []
                for u in range(tiles):
                    i = first + tiles * t + u
                    cols = pl.ds(pl.multiple_of(i * ROW_TILE, ROW_TILE), ROW_TILE)
                    q_is = [qt_ref[h, :, cols] for h in range(hp)]
                    do_is = [dot_ref[h, :, cols] for h in range(hp)]
                    scores = [_dot(k_ts[h], q_is[h]) for h in range(hp)]
                    dps = [_dot(v_ts[h], do_is[h]) for h in range(hp)]
                    tiles_of_step.append((i, cols, q_is, do_is, scores, dps))
                for i, cols, q_is, do_is, scores, dps in tiles_of_step:
                    visible = _visible(i, j) if masked else None
                    probs, dss = [], []
                    for h in range(hp):
                        s = jnp.where(visible, scores[h], NEG) if masked else scores[h]
                        p = jnp.exp2(s - lse_ref[h, :, cols])
                        probs.append(p.astype(BF16))
                        dss.append((p * (dps[h] - delta_ref[h, :, cols])).astype(BF16))
                    for h in range(hp):
                        dv_ref[h] += _dot_nt(probs[h], do_is[h])
                        dk_ref[h] += _dot_nt(dss[h], q_is[h])
                        dqt_ref[h, :, cols] += _dot_tn(k_ts[h], dss[h])
                return carry
            return step

        dk_ref[...] = jnp.zeros_like(dk_ref)
        dv_ref[...] = jnp.zeros_like(dv_ref)
        make_step(True, 1)(j, 0)
        lax.fori_loop(jnp.where(j == 0, j + 1, nt), nt, make_step(True, 1), 0)
        unmasked = jnp.where(j == 0, 0, nt - 1 - j)
        quads = lax.shift_right_logical(unmasked, 2)
        pairs = jnp.bitwise_and(lax.shift_right_logical(unmasked, 1), 1)
        lax.fori_loop(0, quads, make_step(False, 4, first=j + 1), 0)
        lax.fori_loop(0, pairs, make_step(False, 2, first=j + 1 + 4 * quads), 0)
        lax.fori_loop(jnp.where(j == 0, nt, j + 1 + 4 * quads + 2 * pairs), nt, make_step(False, 1), 0)
        dk_ref[...] = dk_ref[...] * _LN2

    key_tile = lambda w: pl.BlockSpec((hp, ROW_TILE, w), lambda g, j: (g, j, 0))
    all_cols = lambda w: pl.BlockSpec((hp, w, n_rows), lambda g, j: (g, 0, 0))
    resident = lambda w: pl.BlockSpec((hp, w, n_rows), lambda g, j: (g, 0, 0), pipeline_mode=pl.Buffered(1))
    out_shapes = [
        jax.ShapeDtypeStruct((N_HEADS, QK_DIM, n_rows), F32),
        jax.ShapeDtypeStruct((N_HEADS, n_rows, QK_DIM), F32),
        jax.ShapeDtypeStruct((N_HEADS, n_rows, V_HEAD), F32),
    ]
    return pl.pallas_call(
        body, name="attn_bwd", grid=(N_HEADS // hp, nt),
        in_specs=[key_tile(QK_DIM), key_tile(V_HEAD), resident(QK_DIM), resident(V_HEAD), resident(1), resident(1)],
        out_specs=[all_cols(QK_DIM), key_tile(QK_DIM), key_tile(V_HEAD)],
        out_shape=out_shapes,
        compiler_params=_params("parallel", "arbitrary"),
    )(k, v, q_t, do_t, lse, delta)


def _bwd_qkv(dq_t, dk, dv, cq, ckv, gq, gkv, wq_t, w_ukv, cos, sin, cos_t, sin_t, n_rows):
    nt = n_rows // ROW_TILE

    def body(dqt_ref, dk_ref, dv_ref, cq_ref, ckv_ref, gq_ref, gkv_ref, wqt_ref, wkv_ref, cos_ref, sin_ref,
             cost_ref, sint_ref, dqraw_ref, dkv_ref, dcq_ref, dckv_ref, dkr_ref, dgq_ref, dgkv_ref):
        i = pl.program_id(0)
        cos_rows, sin_rows = cost_ref[...], sint_ref[...]
        dcqn = jnp.zeros((ROW_TILE, Q_LORA), F32)
        dckvn = jnp.zeros((ROW_TILE, KV_LORA), F32)
        dk_rot = jnp.zeros((ROW_TILE, QK_ROPE), F32)
        for h in range(N_HEADS):
            dq_h, dk_h = dqt_ref[h] * QK_DIM ** -0.5, dk_ref[h]
            dq_raw = jnp.concatenate(
                [dq_h[:QK_NOPE], _rope_rows_t(dq_h[QK_NOPE:], cos_rows, sin_rows)], axis=0).astype(BF16)
            dqraw_ref[h] = dq_raw
            dcqn = dcqn + _dot_tn(dq_raw, wqt_ref[h])
            dkv = jnp.concatenate([dk_h[:, :QK_NOPE], dv_ref[h]], axis=-1).astype(BF16)
            dkv_ref[:, h * KV_HEAD:(h + 1) * KV_HEAD] = dkv
            dckvn = dckvn + _dot_nt(dkv, wkv_ref[h])
            dk_rot = dk_rot + dk_h[:, QK_NOPE:]
        dkr_ref[...] = _rope_t(dk_rot, cos_ref[...], sin_ref[...]).astype(BF16)
        dcq, dgq = _rms_bwd(dcqn, cq_ref[...], gq_ref[...])
        dckv, dgkv = _rms_bwd(dckvn, ckv_ref[...], gkv_ref[...])
        dcq_ref[...] = dcq.astype(BF16)
        dckv_ref[...] = dckv.astype(BF16)
        _accumulate(dgq_ref, i == 0, dgq)
        _accumulate(dgkv_ref, i == 0, dgkv)

    out_shapes = [
        jax.ShapeDtypeStruct((N_HEADS, QK_DIM, n_rows), BF16),
        jax.ShapeDtypeStruct((n_rows, N_HEADS * KV_HEAD), BF16),
        jax.ShapeDtypeStruct((n_rows, Q_LORA), BF16),
        jax.ShapeDtypeStruct((n_rows, KV_LORA), BF16),
        jax.ShapeDtypeStruct((n_rows, QK_ROPE), BF16),
        jax.ShapeDtypeStruct((1, Q_LORA), F32),
        jax.ShapeDtypeStruct((1, KV_LORA), F32),
    ]
    tiles = [dk, dv, cq, ckv]
    whole = [gq, gkv, wq_t, w_ukv]
    return pl.pallas_call(
        body, name="bwd_qkv", grid=(nt,),
        in_specs=[_lane_tile(dq_t.shape)] + [_tile_spec(a.shape) for a in tiles] + [_whole_spec(a.shape) for a in whole]
        + [_tile_spec(cos.shape), _tile_spec(sin.shape), _lane_tile(cos_t.shape), _lane_tile(sin_t.shape)],
        out_specs=[_lane_tile(out_shapes[0].shape)] + [_tile_spec(s.shape) for s in out_shapes[1:5]]
        + [_acc_spec(s.shape) for s in out_shapes[5:]],
        out_shape=out_shapes,
        compiler_params=_params("arbitrary"),
    )(dq_t, *tiles, *whole, cos, sin, cos_t, sin_t)


def _bwd_conv(du1, ag, conv_w, dcq, dckv, dkr, n_rows):
    nt = n_rows // ROW_TILE

    last_tap = CONV_WIDTH - 1

    def body(du1_ref, dnext_ref, ag_ref, w_ref, dcq_ref, dckv_ref, dkr_ref, dz_ref, dw_ref,
             dext_ref, uext_ref, conv_ref, sums_ref):
        i = pl.program_id(0)

        @pl.when(i == 0)
        def _():
            sums_ref[...] = jnp.zeros_like(sums_ref)

        _to_planes(dext_ref, (), slice(0, ROW_TILE), du1_ref[...])
        _to_planes(dext_ref, (), slice(ROW_TILE, None), jnp.where(i == nt - 1, 0.0, dnext_ref[...]))
        ag_t = ag_ref[...]
        live = _row_ids(i, ROW_TILE) >= DEAD
        sg = _sigmoid(ag_t[:, D_CONV:])
        _to_planes(uext_ref, (), slice(None), jnp.where(live, ag_t[:, :D_CONV] * sg, 0.0))
        for c in range(CONV_PLANES):
            taps = w_ref[:, c * _LANES:(c + 1) * _LANES]
            for half in range(0, PHASES, PHASES // 2):
                phases = range(half, half + PHASES // 2)
                us = {p: uext_ref[c, _phase(p), :] for p in phases}
                accs = {p: jnp.zeros((PHASE_ROWS, _LANES), F32) for p in phases}
                for k in range(CONV_WIDTH):
                    tap_sum = jnp.zeros((PHASE_ROWS, _LANES), F32)
                    for p in phases:
                        shifted = dext_ref[c, _phase(p + last_tap - k), :]
                        accs[p] = accs[p] + taps[k:k + 1, :] * shifted
                        tap_sum = tap_sum + shifted * us[p]
                    sums_ref[c, k] += tap_sum
                for p in phases:
                    conv_ref[c, _phase(p), :] = accs[p]
        du0 = jnp.where(live, _from_planes(conv_ref, (), D_CONV), 0.0)
        da = du0 * sg
        dgate = du0 * ag_t[:, :D_CONV] * sg * (1.0 - sg)
        dz_ref[...] = jnp.concatenate(
            [da.astype(BF16), dgate.astype(BF16), dcq_ref[...], dckv_ref[...], dkr_ref[...]], axis=-1)

        @pl.when(i == nt - 1)
        def _():
            for c in range(CONV_PLANES):
                for k in range(CONV_WIDTH):
                    dw_ref[k:k + 1, c * _LANES:(c + 1) * _LANES] = jnp.sum(sums_ref[c, k], axis=0, keepdims=True)

    out_shapes = [jax.ShapeDtypeStruct((n_rows, D_IN), BF16), jax.ShapeDtypeStruct((CONV_WIDTH, D_CONV), F32)]
    return pl.pallas_call(
        body, name="bwd_conv", grid=(nt,),
        in_specs=[_tile_spec(du1.shape), _halo_after(du1.shape, CONV_HALO, n_rows), _tile_spec(ag.shape),
                  _whole_spec(conv_w.shape), _tile_spec(dcq.shape), _tile_spec(dckv.shape), _tile_spec(dkr.shape)],
        out_specs=[_tile_spec(out_shapes[0].shape), _acc_spec(out_shapes[1].shape)],
        out_shape=out_shapes,
        scratch_shapes=[pltpu.VMEM((CONV_PLANES, ROW_TILE + CONV_HALO, _LANES), F32),
                        pltpu.VMEM((CONV_PLANES, ROW_TILE, _LANES), F32), pltpu.VMEM((CONV_PLANES, ROW_TILE, _LANES), F32),
                        pltpu.VMEM((CONV_PLANES, CONV_WIDTH, PHASE_ROWS, _LANES), F32)],
        compiler_params=_params("arbitrary"),
    )(du1, du1, ag, conv_w, dcq, dckv, dkr)


def _bwd_in(dz, x, meta_pad, dh1, g1, w_in, n_rows):
    nt = n_rows // ROW_TILE

    def body(dz_ref, x_ref, meta_ref, dh1_ref, g_ref, w_ref, gx_ref, gmeta_ref, dg1_ref):
        i = pl.program_id(0)
        h0 = jnp.where(i == 0, meta_ref[...], x_ref[...])
        dx, dg1 = _rms_bwd(_dot(dz_ref[...], w_ref[...]), h0, g_ref[...])
        dh0 = dh1_ref[...] + dx
        gx_ref[...] = dh0

        @pl.when(i == 0)
        def _():
            gmeta_ref[...] = dh0

        _accumulate(dg1_ref, i == 0, dg1)

    out_shapes = [
        jax.ShapeDtypeStruct((n_rows - ROW_TILE, D_MODEL), F32),
        jax.ShapeDtypeStruct((ROW_TILE, D_MODEL), F32),
        jax.ShapeDtypeStruct((1, D_MODEL), F32),
    ]
    return pl.pallas_call(
        body, name="bwd_in", grid=(nt,),
        in_specs=[_tile_spec(dz.shape), _real_spec(D_MODEL), _whole_spec(meta_pad.shape), _tile_spec(dh1.shape),
                  _whole_spec(g1.shape), _whole_spec(w_in.shape)],
        out_specs=[_real_spec(D_MODEL), _acc_spec(out_shapes[1].shape), _acc_spec(out_shapes[2].shape)],
        out_shape=out_shapes,
        compiler_params=_params("arbitrary"),
    )(dz, x, meta_pad, dh1, g1, w_in)


def _contraction_tile(n_rows):
    return next(t for t in range(n_rows // 2 // _LANES * _LANES, 0, -_LANES) if n_rows % t == 0)


def _weight_grad(a, b, name, a_transposed=False):
    groups = max(a.shape[0] if a.ndim == 3 else 1, b.shape[0] if b.ndim == 3 else 1)
    n_rows, n = b.shape[-2], b.shape[-1]
    m = a.shape[-2] if a_transposed else a.shape[-1]
    kt = _contraction_tile(n_rows)
    steps = n_rows // kt

    def body(a_ref, b_ref, out_ref, acc_ref):
        i = pl.program_id(1)
        a_t, b_t = a_ref[...].astype(BF16), b_ref[...].astype(BF16)
        part = _dot(a_t, b_t) if a_transposed else _dot_tn(a_t, b_t)
        _accumulate(acc_ref, i == 0, part)

        @pl.when(i == steps - 1)
        def _():
            out_ref[...] = acc_ref[...].astype(out_ref.dtype)

    def spec(arr, rows_last):
        block = (arr.shape[-2], kt) if rows_last else (kt, arr.shape[-1])
        at = (lambda i: (0, i)) if rows_last else (lambda i: (i, 0))
        if arr.ndim == 3:
            return pl.BlockSpec((None,) + block, lambda g, i: (g,) + at(i))
        return pl.BlockSpec(block, lambda g, i: at(i))

    return pl.pallas_call(
        body, name=name, grid=(groups, steps),
        in_specs=[spec(a, a_transposed), spec(b, False)],
        out_specs=pl.BlockSpec((None, m, n), lambda g, i: (g, 0, 0)),
        out_shape=jax.ShapeDtypeStruct((groups, m, n), BF16),
        scratch_shapes=[pltpu.VMEM((m, n), F32)],
        compiler_params=_params("parallel", "arbitrary"),
    )(a, b)


def _my_index():
    return 4 * lax.axis_index("x") + 2 * lax.axis_index("y") + lax.axis_index("c")


def _peer(k):
    flip = lambda v, bit: 1 - v if bit else v
    px = flip(lax.axis_index("x"), k & 4)
    py = flip(lax.axis_index("y"), k & 2)
    pc = flip(lax.axis_index("c"), k & 1)
    return (px, py, pc), 4 * px + 2 * py + pc


def _all_gather(shards, dtypes):
    n = len(shards)
    sibling, chips = 1, (2, 4, 6)

    def body(*refs):
        ins, outs, stages = refs[:n], refs[n:2 * n], refs[2 * n:3 * n]
        send_sems, recv_sems, local_sems = refs[3 * n:]
        me = _my_index()
        for a in range(n):
            stages[a][...] = ins[a][...].astype(stages[a].dtype)
        local = [pltpu.make_async_copy(stages[a], outs[a].at[me], local_sems.at[a]) for a in range(n)]
        for cp in local:
            cp.start()

        def copy(a, k, src, slot, to):
            return pltpu.make_async_remote_copy(
                src_ref=src, dst_ref=outs[a].at[slot], send_sem=send_sems.at[a, k - 1],
                recv_sem=recv_sems.at[a, k - 1], device_id=_peer(to)[0], device_id_type=MESH)

        def own(a, k):
            return copy(a, k, stages[a], me, k)

        def passed(a, k):
            slot = _peer(k)[1]
            return copy(a, k ^ sibling, outs[a].at[slot], slot, sibling)

        def arrival(a, k):
            return copy(a, k, stages[a], _peer(k)[1], k)

        for k in (sibling,) + chips:
            for a in range(n):
                own(a, k).start()
        for k in chips:
            for a in range(n):
                arrival(a, k).wait_recv()
                passed(a, k).start()
        for a in range(n):
            arrival(a, sibling).wait_recv()
            for k in chips:
                arrival(a, k ^ sibling).wait_recv()
        for a in range(n):
            for k in (sibling,) + chips:
                own(a, k).wait_send()
            for k in chips:
                passed(a, k).wait_send()
        for cp in local:
            cp.wait()

    return pl.pallas_call(
        body, name="gather_weights",
        in_specs=[pl.BlockSpec(memory_space=pltpu.VMEM)] * n,
        out_specs=[pl.BlockSpec(memory_space=pl.ANY)] * n,
        out_shape=[jax.ShapeDtypeStruct((N_DEV,) + s.shape, dt) for s, dt in zip(shards, dtypes)],
        scratch_shapes=[pltpu.VMEM(s.shape, dt) for s, dt in zip(shards, dtypes)]
        + [pltpu.SemaphoreType.DMA((n, N_DEV - 1)), pltpu.SemaphoreType.DMA((n, N_DEV - 1)), pltpu.SemaphoreType.DMA((n,))],
        compiler_params=pltpu.CompilerParams(vmem_limit_bytes=VMEM_LIMIT),
    )(*shards)


def _exchange(parts, whole):
    n = len(parts)

    def body(*refs):
        ins, outs = refs[:n], refs[n:2 * n]
        send_sems, recv_sems, local_sems = refs[2 * n:]
        me = _my_index()

        def src(a, slab):
            return ins[a] if whole[a] else ins[a].at[slab]

        local = [pltpu.make_async_copy(src(a, me), outs[a].at[me], local_sems.at[a]) for a in range(n)]
        for cp in local:
            cp.start()

        def copy(a, k, slab, slot):
            peer, _ = _peer(k)
            return pltpu.make_async_remote_copy(
                src_ref=src(a, slab), dst_ref=outs[a].at[slot], send_sem=send_sems.at[a, k - 1],
                recv_sem=recv_sems.at[a, k - 1], device_id=peer, device_id_type=MESH)

        for k in range(1, N_DEV):
            for a in range(n):
                copy(a, k, _peer(k)[1], me).start()
        for k in range(1, N_DEV):
            for a in range(n):
                copy(a, k, _peer(k)[1], _peer(k)[1]).wait()
        for cp in local:
            cp.wait()

    return pl.pallas_call(
        body, name="exchange_grads",
        in_specs=[pl.BlockSpec(memory_space=pl.ANY)] * n,
        out_specs=[pl.BlockSpec(memory_space=pl.ANY)] * n,
        out_shape=[jax.ShapeDtypeStruct(((N_DEV,) + p.shape) if w else p.shape, p.dtype) for p, w in zip(parts, whole)],
        scratch_shapes=[pltpu.SemaphoreType.DMA((n, N_DEV - 1)), pltpu.SemaphoreType.DMA((n, N_DEV - 1)),
                        pltpu.SemaphoreType.DMA((n,))],
    )(*parts)


def _sequencer_exchange(parts, whole, name, collective_id):
    n = len(parts)
    srcs = [jax.new_ref(p, memory_space=pltpu.MemorySpace.HBM) for p in parts]
    lands = [jax.empty_ref(jax.ShapeDtypeStruct(((N_DEV,) + p.shape) if w else p.shape, p.dtype),
                           memory_space=pltpu.MemorySpace.HBM) for p, w in zip(parts, whole)]

    @pl.kernel(mesh=plsc.ScalarSubcoreMesh(axis_name="sequencer", num_cores=1), name=name,
               scratch_types=(pltpu.SemaphoreType.DMA((n, N_DEV - 1)), pltpu.SemaphoreType.DMA((n, N_DEV - 1)),
                              pltpu.SemaphoreType.DMA((n,))),
               compiler_params=pltpu.CompilerParams(collective_id=collective_id))
    def launch(send_sems, recv_sems, local_sems):
        barrier = pltpu.get_barrier_semaphore()
        for k in range(1, N_DEV):
            pl.semaphore_signal(barrier, inc=1, device_id=_peer(k)[0], device_id_type=MESH)
        pl.semaphore_wait(barrier, N_DEV - 1)
        me = _my_index()

        def src(a, slab):
            return srcs[a] if whole[a] else srcs[a].at[slab]

        local = [pltpu.make_async_copy(src(a, me), lands[a].at[me], local_sems.at[a]) for a in range(n)]
        for cp in local:
            cp.start()

        def copy(a, k, slab, slot):
            return pltpu.make_async_remote_copy(
                src_ref=src(a, slab), dst_ref=lands[a].at[slot], send_sem=send_sems.at[a, k - 1],
                recv_sem=recv_sems.at[a, k - 1], device_id=_peer(k)[0], device_id_type=MESH)

        for k in range(1, N_DEV):
            for a in range(n):
                copy(a, k, _peer(k)[1], me).start()
        for k in range(1, N_DEV):
            for a in range(n):
                copy(a, k, _peer(k)[1], _peer(k)[1]).wait()
        for cp in local:
            cp.wait()

    launch()
    return [land[...] for land in lands]


def _row_block(rows):
    if rows <= ROW_TILE:
        return rows
    return next(rb for rb in range(ROW_TILE, 0, -16) if rows % rb == 0)


def _adamw(landing, w, m, v, name):
    rows, cols = w.shape
    rb = _row_block(rows)

    def body(l_ref, w_ref, m_ref, v_ref, g_ref, d_ref, m2_ref, v2_ref):
        g = l_ref[0].astype(F32)
        for p in range(1, N_DEV):
            g = g + l_ref[p].astype(F32)
        g_ref[...] = g
        d_ref[...], m2_ref[...], v2_ref[...] = _adamw_step(g, w_ref[...], m_ref[...], v_ref[...])

    flat = pl.BlockSpec((rb, cols), lambda i: (i, 0))
    return pl.pallas_call(
        body, name=name, grid=(rows // rb,),
        in_specs=[pl.BlockSpec((N_DEV, rb, cols), lambda i: (0, i, 0)), flat, flat, flat],
        out_specs=[flat] * 4,
        out_shape=[jax.ShapeDtypeStruct((rows, cols), F32)] * 4,
        compiler_params=_params("parallel"),
    )(landing, w, m, v)


def _adamw_step(g, w, m, v):
    m2 = ADAM_B1 * m + (1.0 - ADAM_B1) * g
    v2 = ADAM_B2 * v + (1.0 - ADAM_B2) * (g * g)
    m_hat = m2 / (1.0 - ADAM_B1 ** ADAM_STEP)
    v_hat = v2 / (1.0 - ADAM_B2 ** ADAM_STEP)
    return -ADAM_LR * (m_hat / (jnp.sqrt(v_hat) + ADAM_EPS) + ADAM_WD * w), m2, v2


_REPLICATED = (
    ("mix_norm_g", D_MODEL), ("q_norm_g", Q_LORA), ("kv_norm_g", KV_LORA), ("conv_b", D_CONV), ("conv_ln_g", D_CONV),
    ("conv_ln_b", D_CONV), ("conv_out_g", D_CONV), ("attn_out_g", D_CONV), ("ffn_norm_g", D_MODEL),
    ("ffn_conv_b", D_UP), ("final_norm_g", D_MODEL),
)
_REPLICATED_WIDTH = sum(size for _, size in _REPLICATED) + _LANES

_WEIGHT_ORDER = (
    "meta_tokens", "mix_norm_g", "w_in", "q_norm_g", "w_uq", "kv_norm_g", "w_ukv", "conv_w", "conv_b", "conv_ln_g",
    "conv_ln_b", "conv_out_g", "attn_out_g", "w_out", "ffn_norm_g", "w_ffn_up", "ffn_conv_w", "ffn_conv_b",
    "w_ffn_down", "final_norm_g",
)


def _pack_replicated(grads, loss):
    rows = [grads[name].reshape(1, size) for name, size in _REPLICATED]
    return jnp.concatenate(rows + [jnp.broadcast_to(loss.reshape(1, 1), (1, _LANES))], axis=-1)


def _adamw_replicated(landing, weights, moments_m, moments_v):
    n = len(_REPLICATED)

    def body(*refs):
        l_ref, ins, outs = refs[0], refs[1:1 + 3 * n], refs[1 + 3 * n:]
        total = l_ref[0]
        for p in range(1, N_DEV):
            total = total + l_ref[p]
        at = 0
        for a, (_, size) in enumerate(_REPLICATED):
            g = total[:, at:at + size]
            w_ref, m_ref, v_ref = ins[3 * a:3 * a + 3]
            g_ref, d_ref, m2_ref, v2_ref = outs[4 * a:4 * a + 4]
            g_ref[...] = g
            d_ref[...], m2_ref[...], v2_ref[...] = _adamw_step(g, w_ref[...], m_ref[...], v_ref[...])
            at += size
        outs[-1][...] = total[:, at:at + _LANES]

    operands, out_shapes = [], []
    for name, size in _REPLICATED:
        operands += [weights[name].reshape(1, size), moments_m[name].reshape(1, size), moments_v[name].reshape(1, size)]
        out_shapes += [jax.ShapeDtypeStruct((1, size), F32)] * 4
    out_shapes.append(jax.ShapeDtypeStruct((1, _LANES), F32))
    outs = pl.pallas_call(body, name="adamw_replicated", out_shape=out_shapes)(landing, *operands)
    return outs[-1][0, 0], {name: outs[4 * a:4 * a + 4] for a, (name, _) in enumerate(_REPLICATED)}


def _pad_rows(a, rows):
    return jnp.pad(a, ((0, rows - a.shape[0]), (0, 0)))


def _slabs(a):
    r, c = a.shape
    return a.reshape(r, N_DEV, c // N_DEV).transpose(1, 0, 2)


def _unslab(a):
    g, r, c = a.shape
    return a.transpose(1, 0, 2).reshape(r, g * c)


def _local_step(x, target, w, n_rows, ffn_weights, send_grads):
    cos_t, sin_t = lax.optimization_barrier(_rope_tables(n_rows))
    cos, sin = cos_t.T, sin_t.T
    meta_pad, g1, gf = w["meta_pad"], w["mix_norm_g"], w["final_norm_g"]
    gq, gkv, gb_col = w["q_norm_g"], w["kv_norm_g"], w["attn_out_g"].reshape(D_ATTN, 1)
    nb, ag, cq, ckv, kr = _fwd_in(x, meta_pad, g1, w["w_in"], n_rows)
    mix_a, u1 = _fwd_conv(ag, w["conv_w"], w["conv_b"], w["conv_ln_g"], w["conv_ln_b"], w["conv_out_g"], n_rows)
    q_t, k, v, v_t, cqn, ckvn = _fwd_qkv(cq, ckv, kr, gq, gkv, w["wq_t"], w["w_ukv"], w["wv_t"], cos, sin, cos_t, sin_t, n_rows)
    o_t, lse = _attn_fwd(q_t, k, v_t, n_rows)
    w_out, w_up, w_down = ffn_weights()
    mix_bt, h1 = _fwd_out(x, meta_pad, mix_a, o_t, gb_col, w_out, n_rows)
    n2, up0, act, da, db, dh2, loss, dgf = _fwd_ffn(
        h1, target, w["ffn_norm_g"], w_up, w["fw"], w["fb"], w_down, gf, n_rows)

    dup, dfb = _bwd_ffn_act(dh2, da, db, w_down, n_rows)
    dup0, dh1, dfw, dg2 = _bwd_ffn_up(dup, up0, h1, dh2, w["ffn_norm_g"], w_up, w["fw"], n_rows)
    grad_w_out = jnp.concatenate([_weight_grad(mix_a, dh1, "grad_w_out_conv")[0],
                                  _weight_grad(mix_bt, dh1, "grad_w_out_attn", a_transposed=True)[0]], axis=0)
    stage0 = {
        "w_ffn_up": _weight_grad(dup0, n2, "grad_w_ffn_up"),
        "w_ffn_down": _weight_grad(act, dh2, "grad_w_ffn_down").reshape(N_DEV, D_FF // N_DEV, D_MODEL),
        "w_out": grad_w_out.reshape(N_DEV, D_MODEL // N_DEV, D_MODEL),
    }
    stage0, dh1 = lax.optimization_barrier((stage0, dh1))
    send_grads(0, stage0)
    do_t, delta, du1, dgb, dga, dlg, dlb, dcb = _bwd_out(
        dh1, o_t, u1, w_out, gb_col, w["conv_ln_g"], w["conv_ln_b"], w["conv_out_g"], n_rows)
    dq_t, dk, dv = _attn_bwd(q_t, k, v, do_t, lse, delta, n_rows)
    dqraw_t, dkv, dcq, dckv, dkr, dgq, dgkv = _bwd_qkv(
        dq_t, dk, dv, cq, ckv, gq, gkv, w["wq_t"], w["w_ukv"], cos, sin, cos_t, sin_t, n_rows)
    dz, dcw = _bwd_conv(du1, ag, w["conv_w"], dcq, dckv, dkr, n_rows)
    stage1 = {
        "w_in": _weight_grad(dz, nb, "grad_w_in")[0].reshape(N_DEV, D_IN // N_DEV, D_MODEL),
        "w_uq": _weight_grad(dqraw_t.reshape(N_HEADS * QK_DIM, n_rows), cqn, "grad_w_uq", a_transposed=True)[0].reshape(
            N_HEADS, QK_DIM, Q_LORA),
        "w_ukv": _slabs(_weight_grad(ckvn, dkv, "grad_w_ukv")[0]),
        "conv_w": _slabs(dcw),
        "ffn_conv_w": dfw[:, :, :UP_SLAB],
    }
    stage1, dz = lax.optimization_barrier((stage1, dz))
    send_grads(1, stage1)
    gx, gmeta, dg1 = _bwd_in(dz, x, meta_pad, dh1, g1, w["w_in"], n_rows)

    sharded = {"meta_tokens": _slabs(gmeta[DEAD:])}
    replicated = {
        "mix_norm_g": dg1, "q_norm_g": dgq, "kv_norm_g": dgkv, "conv_b": dcb, "conv_ln_g": dlg, "conv_ln_b": dlb,
        "conv_out_g": dga, "attn_out_g": dgb, "ffn_norm_g": dg2, "ffn_conv_b": dfb, "final_norm_g": dgf,
    }
    return loss[0, 0], gx, sharded, replicated


_SHARDED = (
    ("w_in", None, BF16), ("w_uq", None, BF16), ("w_ukv", None, BF16), ("w_out", None, BF16), ("w_ffn_up", None, BF16),
    ("w_ffn_down", None, BF16), ("conv_w", 32, F32), ("ffn_conv_w", 8, F32), ("meta_tokens", None, F32),
)
GATHER_LATE_ID = 3
EXCHANGE_STAGE_IDS = (4, 5)
_LATE_WEIGHTS = ("w_out", "w_ffn_up", "w_ffn_down")
_COLUMN_SHARDS = ("w_in", "w_uq", "w_ffn_up")


def kernel(x, meta_tokens, mix_norm_g, w_in, q_norm_g, w_uq, kv_norm_g, w_ukv, conv_w, conv_b, conv_ln_g, conv_ln_b, conv_out_g, attn_out_g, w_out, ffn_norm_g, w_ffn_up, ffn_conv_w, ffn_conv_b, w_ffn_down, final_norm_g, loss_target, m_meta_tokens, m_mix_norm_g, m_w_in, m_q_norm_g, m_w_uq, m_kv_norm_g, m_w_ukv, m_conv_w, m_conv_b, m_conv_ln_g, m_conv_ln_b, m_conv_out_g, m_attn_out_g, m_w_out, m_ffn_norm_g, m_w_ffn_up, m_ffn_conv_w, m_ffn_conv_b, m_w_ffn_down, m_final_norm_g, v_meta_tokens, v_mix_norm_g, v_w_in, v_q_norm_g, v_w_uq, v_kv_norm_g, v_w_ukv, v_conv_w, v_conv_b, v_conv_ln_g, v_conv_ln_b, v_conv_out_g, v_attn_out_g, v_w_out, v_ffn_norm_g, v_w_ffn_up, v_ffn_conv_w, v_ffn_conv_b, v_w_ffn_down, v_final_norm_g):
    given = dict(locals())
    weights = {name: given[name] for name in _WEIGHT_ORDER}
    moments_m = {name: given["m_" + name] for name in _WEIGHT_ORDER}
    moments_v = {name: given["v_" + name] for name in _WEIGHT_ORDER}
    seq = x.shape[1]
    n_rows = ROW_TILE + seq

    def shard2d(name, a):
        a = a.reshape(a.shape[-2], a.shape[-1])
        return a.T if name in _COLUMN_SHARDS else a

    early = [entry for entry in _SHARDED if entry[0] not in _LATE_WEIGHTS]
    shards = []
    for name, pad_to, _ in early:
        s = shard2d(name, weights[name])
        shards.append(s if pad_to is None else _pad_rows(s, pad_to))
    gathered = dict(zip([name for name, _, _ in early], _all_gather(shards, [dt for _, _, dt in early])))
    behind = gathered["meta_tokens"][0, 0, 0] * 0.0
    late_parts = [(shard2d(name, weights[name]) + behind).astype(BF16) for name in _LATE_WEIGHTS]
    late = _sequencer_exchange(late_parts, [True] * len(late_parts), "gather_late", GATHER_LATE_ID)
    meta_full = _unslab(gathered["meta_tokens"])
    full = {
        "meta_pad": jnp.concatenate([jnp.zeros((DEAD, D_MODEL), F32), meta_full], axis=0),
        "w_in": gathered["w_in"].reshape(D_IN, D_MODEL),
        "wq_t": gathered["w_uq"],
        "w_ukv": gathered["w_ukv"],
        "wv_t": gathered["w_ukv"][:, :, QK_NOPE:].transpose(0, 2, 1),
        "conv_w": _unslab(gathered["conv_w"][:, :CONV_WIDTH]),
        "fw": jnp.pad(gathered["ffn_conv_w"][:, :FFN_CONV_WIDTH], ((0, 0), (0, 0), (0, UP_PAD - UP_SLAB))),
        "fb": jnp.pad(ffn_conv_b.reshape(N_DEV, 1, UP_SLAB), ((0, 0), (0, 0), (0, UP_PAD - UP_SLAB))),
        "final_norm_g": final_norm_g.reshape(1, D_MODEL),
    }
    for name in ("mix_norm_g", "q_norm_g", "kv_norm_g", "conv_b", "conv_ln_g", "conv_ln_b", "conv_out_g", "attn_out_g",
                 "ffn_norm_g"):
        full[name] = weights[name]

    def ffn_weights():
        w_out_all, w_up_all, w_down_all = late
        return (w_out_all.reshape(D_MODEL, D_MODEL), w_up_all, w_down_all.reshape(N_ACT_SLAB, UP_SLAB, D_MODEL))

    wire = {name: (pad_to, dt) for name, pad_to, dt in _SHARDED}
    landing = {}

    def on_the_wire(name, slabs):
        pad_to, dt = wire[name]
        slabs = slabs.astype(dt)
        return slabs if pad_to is None else jnp.pad(slabs, ((0, 0), (0, pad_to - slabs.shape[1]), (0, 0)))

    def send_grads(stage, grads):
        parts = [on_the_wire(name, slabs) for name, slabs in grads.items()]
        if landing:
            arrived = list(landing)
            parts, held = lax.optimization_barrier((parts, [landing[name] for name in arrived]))
            landing.update(zip(arrived, held))
        landed = _sequencer_exchange(parts, [False] * len(parts), f"exchange_stage{stage}", EXCHANGE_STAGE_IDS[stage])
        landing.update(zip(grads, landed))

    loss, gx, sharded, replicated = _local_step(x[0], loss_target[0], full, n_rows, ffn_weights, send_grads)

    parts = [on_the_wire(name, slabs) for name, slabs in sharded.items()] + [_pack_replicated(replicated, loss)]
    landed = _exchange(parts, [False] * len(sharded) + [True])
    landing.update(zip(sharded, landed[:-1]))

    grad, delta, new_m, new_v = {}, {}, {}, {}
    for name, pad_to, _ in _SHARDED:
        land = landing[name]
        ws, ms, vs = (shard2d(name, a[name]) for a in (weights, moments_m, moments_v))
        rows = ws.shape[0]
        if pad_to is not None:
            ws, ms, vs = _pad_rows(ws, pad_to), _pad_rows(ms, pad_to), _pad_rows(vs, pad_to)
        outs = _adamw(land, ws, ms, vs, "adamw_" + name)
        shape = weights[name].shape
        grad[name], delta[name], new_m[name], new_v[name] = (
            (o.T if name in _COLUMN_SHARDS else o[:rows]).reshape(shape) for o in outs)
    loss, updates = _adamw_replicated(landed[-1], weights, moments_m, moments_v)
    for name, outs in updates.items():
        grad[name], delta[name], new_m[name], new_v[name] = (o.reshape(weights[name].shape) for o in outs)

    return (loss, gx[None], *[grad[n] for n in _WEIGHT_ORDER], *[delta[n] for n in _WEIGHT_ORDER],
            *[new_m[n] for n in _WEIGHT_ORDER], *[new_v[n] for n in _WEIGHT_ORDER])
```

```python
import jax
import jax.numpy as jnp
from jax import lax
from jax.experimental import pallas as pl
from jax.experimental.pallas import tpu as pltpu
from jax.experimental.pallas import tpu_sc as plsc

F32 = jnp.float32
BF16 = jnp.bfloat16

N_DEV = 8
D_MODEL = 1024
CHUNK = 64
CHUNK_SHIFT = 6
N_META = 16
D_CONV = 512
CONV_WIDTH = 31
N_HEADS = 8
QK_NOPE = 64
QK_ROPE = 32
QK_DIM = QK_NOPE + QK_ROPE
V_HEAD = 64
KV_HEAD = QK_NOPE + V_HEAD
D_ATTN = N_HEADS * V_HEAD
Q_LORA = 384
KV_LORA = 256
ROPE_THETA = 10000.0
D_IN = 2 * D_CONV + Q_LORA + KV_LORA + QK_ROPE
D_FF = 2816
D_UP = 2 * D_FF
FFN_CONV_WIDTH = 3
UP_SLAB = D_UP // N_DEV
N_ACT_SLAB = D_FF // UP_SLAB
EPS = 1e-6
NEG = -1e30
_LN2 = 0.6931471805599453
QK_LOGIT_SCALE = QK_DIM ** -0.5 / _LN2
ADAM_LR = 0.001
ADAM_B1 = 0.9
ADAM_B2 = 0.999
ADAM_EPS = 1e-08
ADAM_WD = 0.01
ADAM_STEP = 10

ROW_TILE = 256
DEAD = ROW_TILE - N_META
CONV_HALO = 32
FFN_HALO = 16
VMEM_LIMIT = 56 * 1024 * 1024
_LANES = 128

MESH = pl.DeviceIdType.MESH


def _dot(a, b):
    return jnp.dot(a, b, preferred_element_type=F32)


def _dot_nt(a, b):
    return lax.dot_general(a, b, (((1,), (1,)), ((), ())), preferred_element_type=F32)


def _dot_tn(a, b):
    return lax.dot_general(a, b, (((0,), (0,)), ((), ())), preferred_element_type=F32)


def _sigmoid(x):
    return 1.0 / (1.0 + jnp.exp2(x * (-1.0 / _LN2)))


def _rms_fwd(x, g):
    r = lax.rsqrt(jnp.mean(x * x, axis=-1, keepdims=True) + EPS)
    return x * r * g


def _rms_bwd(dy, x, g):
    r = lax.rsqrt(jnp.mean(x * x, axis=-1, keepdims=True) + EPS)
    w = dy * g
    dx = r * w - x * (r * r * r) * jnp.mean(w * x, axis=-1, keepdims=True)
    return dx, jnp.sum(dy * x * r, axis=0, keepdims=True)


def _rope(x, cos, sin):
    half = QK_ROPE // 2
    x1, x2 = x[:, :half], x[:, half:]
    return jnp.concatenate([x1 * cos - x2 * sin, x2 * cos + x1 * sin], axis=-1)


def _rope_t(dy, cos, sin):
    half = QK_ROPE // 2
    d1, d2 = dy[:, :half], dy[:, half:]
    return jnp.concatenate([d1 * cos + d2 * sin, d2 * cos - d1 * sin], axis=-1)


def _row_ids(i, rows):
    return i * rows + lax.broadcasted_iota(jnp.int32, (rows, 1), 0)


def _accumulate(ref, first, value):
    @pl.when(first)
    def _():
        ref[...] = value

    @pl.when(jnp.logical_not(first))
    def _():
        ref[...] += value


def _tile_spec(shape):
    nd = len(shape)
    if nd == 2:
        return pl.BlockSpec((ROW_TILE, shape[1]), lambda i: (i, 0))
    return pl.BlockSpec((shape[0], ROW_TILE, shape[2]), lambda i: (0, i, 0))


def _whole_spec(shape):
    nd = len(shape)
    return pl.BlockSpec(tuple(shape), lambda i: (0,) * nd, pipeline_mode=pl.Buffered(1))


def _acc_spec(shape):
    nd = len(shape)
    return pl.BlockSpec(tuple(shape), lambda i: (0,) * nd)


def _real_spec(width):
    return pl.BlockSpec((ROW_TILE, width), lambda i: (jnp.maximum(i - 1, 0), 0))


def _params(*semantics):
    return pltpu.CompilerParams(dimension_semantics=semantics, vmem_limit_bytes=VMEM_LIMIT)


def _fwd_in(x, meta_pad, g1, w_in, n_rows):
    nt = n_rows // ROW_TILE

    def body(x_ref, meta_ref, g_ref, w_ref, nb_ref, ag_ref, cq_ref, ckv_ref, kr_ref):
        i = pl.program_id(0)
        h0 = jnp.where(i == 0, meta_ref[...], x_ref[...])
        nb = _rms_fwd(h0, g_ref[...]).astype(BF16)
        nb_ref[...] = nb
        z = _dot_nt(nb, w_ref[...])
        ag_ref[...] = z[:, :2 * D_CONV]
        cq_ref[...] = z[:, 2 * D_CONV:2 * D_CONV + Q_LORA]
        ckv_ref[...] = z[:, 2 * D_CONV + Q_LORA:2 * D_CONV + Q_LORA + KV_LORA]
        kr_ref[...] = z[:, 2 * D_CONV + Q_LORA + KV_LORA:]

    out_shapes = [
        jax.ShapeDtypeStruct((n_rows, D_MODEL), BF16),
        jax.ShapeDtypeStruct((n_rows, 2 * D_CONV), F32),
        jax.ShapeDtypeStruct((n_rows, Q_LORA), F32),
        jax.ShapeDtypeStruct((n_rows, KV_LORA), F32),
        jax.ShapeDtypeStruct((n_rows, QK_ROPE), F32),
    ]
    return pl.pallas_call(
        body, name="fwd_in", grid=(nt,),
        in_specs=[_real_spec(D_MODEL), _whole_spec(meta_pad.shape), _whole_spec(g1.shape), _whole_spec(w_in.shape)],
        out_specs=[_tile_spec(s.shape) for s in out_shapes],
        out_shape=out_shapes,
        compiler_params=_params("parallel"),
    )(x, meta_pad, g1, w_in)


def _conv_chain(u1, ln_g, ln_b):
    mu = jnp.mean(u1, axis=-1, keepdims=True)
    xc = u1 - mu
    rstd = lax.rsqrt(jnp.mean(xc * xc, axis=-1, keepdims=True) + EPS)
    xh = xc * rstd
    u2 = xh * ln_g + ln_b
    return xh, u2, u2 * _sigmoid(u2), rstd


def _fwd_conv(ag, conv_w, conv_b, ln_g, ln_b, out_g, n_rows):
    nt = n_rows // ROW_TILE

    def body(ag_ref, w_ref, b_ref, lg_ref, lb_ref, og_ref, mix_ref, u1_ref, ext_ref, conv_ref):
        i = pl.program_id(0)

        @pl.when(i == 0)
        def _():
            ext_ref[:, 0:CONV_HALO, :] = jnp.zeros((CONV_PLANES, CONV_HALO, _LANES), F32)

        ag_t = ag_ref[...]
        live = _row_ids(i, ROW_TILE) >= DEAD
        u0 = jnp.where(live, ag_t[:, :D_CONV] * _sigmoid(ag_t[:, D_CONV:]), 0.0)
        _to_planes(ext_ref, (), slice(CONV_HALO, None), u0)
        first = CONV_HALO - (CONV_WIDTH - 1)
        for c in range(CONV_PLANES):
            taps = w_ref[:, c * _LANES:(c + 1) * _LANES]
            for p in range(PHASES):
                acc = jnp.zeros((PHASE_ROWS, _LANES), F32)
                for k in range(CONV_WIDTH):
                    acc = acc + taps[k:k + 1, :] * ext_ref[c, _phase(first + k + p), :]
                conv_ref[c, _phase(p), :] = acc
        ext_ref[:, 0:CONV_HALO, :] = ext_ref[:, ROW_TILE:ROW_TILE + CONV_HALO, :]
        u1 = _from_planes(conv_ref, (), D_CONV) + b_ref[...]
        u1_ref[...] = u1
        _, _, u3, _ = _conv_chain(u1, lg_ref[...], lb_ref[...])
        mix_ref[...] = _rms_fwd(u3, og_ref[...]).astype(BF16)

    out_shapes = [jax.ShapeDtypeStruct((n_rows, D_CONV), BF16), jax.ShapeDtypeStruct((n_rows, D_CONV), F32)]
    small = [conv_w, conv_b, ln_g, ln_b, out_g]
    return pl.pallas_call(
        body, name="fwd_conv", grid=(nt,),
        in_specs=[_tile_spec(ag.shape)] + [_whole_spec(a.shape) for a in small],
        out_specs=[_tile_spec(s.shape) for s in out_shapes],
        out_shape=out_shapes,
        scratch_shapes=[pltpu.VMEM((CONV_PLANES, ROW_TILE + CONV_HALO, _LANES), F32),
                        pltpu.VMEM((CONV_PLANES, ROW_TILE, _LANES), F32)],
        compiler_params=_params("arbitrary"),
    )(ag, *small)


def _lane_tile(shape):
    if len(shape) == 2:
        return pl.BlockSpec((shape[0], ROW_TILE), lambda i: (0, i))
    return pl.BlockSpec((shape[0], shape[1], ROW_TILE), lambda i: (0, 0, i))


def _rope_rows(x, cos, sin):
    half = QK_ROPE // 2
    x1, x2 = x[:half], x[half:]
    return jnp.concatenate([x1 * cos - x2 * sin, x2 * cos + x1 * sin], axis=0)


def _rope_rows_t(dy, cos, sin):
    half = QK_ROPE // 2
    d1, d2 = dy[:half], dy[half:]
    return jnp.concatenate([d1 * cos + d2 * sin, d2 * cos - d1 * sin], axis=0)


def _fwd_qkv(cq, ckv, kr, gq, gkv, wq_t, w_ukv, wv_t, cos, sin, cos_t, sin_t, n_rows):
    nt = n_rows // ROW_TILE

    def body(cq_ref, ckv_ref, kr_ref, gq_ref, gkv_ref, wqt_ref, wkv_ref, wvt_ref, cos_ref, sin_ref, cost_ref, sint_ref,
             qt_ref, k_ref, v_ref, vt_ref, cqn_ref, ckvn_ref):
        cqn = _rms_fwd(cq_ref[...], gq_ref[...]).astype(BF16)
        ckvn = _rms_fwd(ckv_ref[...], gkv_ref[...]).astype(BF16)
        cqn_ref[...] = cqn
        ckvn_ref[...] = ckvn
        k_rot = _rope(kr_ref[...], cos_ref[...], sin_ref[...])
        cos_rows, sin_rows = cost_ref[...], sint_ref[...]
        q_all = _dot_nt(wqt_ref[...].reshape(N_HEADS * QK_DIM, Q_LORA), cqn)
        vt_all = _dot_nt(wvt_ref[...].reshape(N_HEADS * V_HEAD, KV_LORA), ckvn).astype(BF16)
        for h in range(N_HEADS):
            q_raw = q_all[h * QK_DIM:(h + 1) * QK_DIM]
            q_h = jnp.concatenate([q_raw[:QK_NOPE], _rope_rows(q_raw[QK_NOPE:], cos_rows, sin_rows)], axis=0)
            qt_ref[h] = (q_h * QK_LOGIT_SCALE).astype(BF16)
            kv = _dot(ckvn, wkv_ref[h])
            k_ref[h] = jnp.concatenate([kv[:, :QK_NOPE], k_rot], axis=-1).astype(BF16)
            v_ref[h] = kv[:, QK_NOPE:].astype(BF16)
            vt_ref[h] = vt_all[h * V_HEAD:(h + 1) * V_HEAD]

    out_shapes = [
        jax.ShapeDtypeStruct((N_HEADS, QK_DIM, n_rows), BF16),
        jax.ShapeDtypeStruct((N_HEADS, n_rows, QK_DIM), BF16),
        jax.ShapeDtypeStruct((N_HEADS, n_rows, V_HEAD), BF16),
        jax.ShapeDtypeStruct((N_HEADS, V_HEAD, n_rows), BF16),
        jax.ShapeDtypeStruct((n_rows, Q_LORA), BF16),
        jax.ShapeDtypeStruct((n_rows, KV_LORA), BF16),
    ]
    tiles = [cq, ckv, kr]
    whole = [gq, gkv, wq_t, w_ukv, wv_t]
    out_specs = [_lane_tile(out_shapes[0].shape), _tile_spec(out_shapes[1].shape), _tile_spec(out_shapes[2].shape),
                 _lane_tile(out_shapes[3].shape), _tile_spec(out_shapes[4].shape), _tile_spec(out_shapes[5].shape)]
    return pl.pallas_call(
        body, name="fwd_qkv", grid=(nt,),
        in_specs=[_tile_spec(a.shape) for a in tiles] + [_whole_spec(a.shape) for a in whole]
        + [_tile_spec(cos.shape), _tile_spec(sin.shape), _lane_tile(cos_t.shape), _lane_tile(sin_t.shape)],
        out_specs=out_specs,
        out_shape=out_shapes,
        compiler_params=_params("parallel"),
    )(*tiles, *whole, cos, sin, cos_t, sin_t)


def _chunk_of(rows):
    return jnp.where(rows >= ROW_TILE, lax.shift_right_arithmetic(rows - ROW_TILE, CHUNK_SHIFT) + 1, 0)


def _visible(i, j):
    k_rows = j * ROW_TILE + lax.broadcasted_iota(jnp.int32, (ROW_TILE, 1), 0)
    q_rows = i * ROW_TILE + lax.broadcasted_iota(jnp.int32, (1, ROW_TILE), 1)
    return jnp.logical_and(_chunk_of(q_rows) >= _chunk_of(k_rows), k_rows >= DEAD)


def _attn_fwd(q_t, k, v_t, n_rows):
    nt = n_rows // ROW_TILE

    def body(qt_ref, k_ref, vt_ref, ot_ref, lse_ref, max_ref, sum_ref):
        i = pl.program_id(0)
        q_ts = [qt_ref[h] for h in range(N_HEADS)]

        def key_rows(j):
            return pl.ds(pl.multiple_of(j * ROW_TILE, ROW_TILE), ROW_TILE)

        def make_step(masked, tiles, first=0):
            def step(t, carry):
                js = [first + tiles * t + u for u in range(tiles)]
                scores = [[_dot(k_ref[h, key_rows(j), :], q_ts[h]) for h in range(N_HEADS)] for j in js]
                for j, tile_scores in zip(js, scores):
                    visible = _visible(i, j) if masked else None
                    probs, alphas = [], []
                    for h in range(N_HEADS):
                        m = max_ref[h]
                        s = jnp.where(visible, tile_scores[h], NEG) if masked else tile_scores[h]
                        m_new = jnp.maximum(m, jnp.max(s, axis=0, keepdims=True))
                        alpha = jnp.exp2(m - m_new)
                        p = jnp.exp2(s - m_new)
                        probs.append(p.astype(BF16))
                        alphas.append(alpha)
                        max_ref[h] = m_new
                        sum_ref[h] = alpha * sum_ref[h] + jnp.sum(p, axis=0, keepdims=True)
                    for h in range(N_HEADS):
                        ot_ref[h] = alphas[h] * ot_ref[h] + _dot(vt_ref[h, :, key_rows(j)], probs[h])
                return carry
            return step

        max_ref[...] = jnp.full(max_ref.shape, NEG, F32)
        sum_ref[...] = jnp.zeros_like(sum_ref)
        ot_ref[...] = jnp.zeros_like(ot_ref)
        between = jnp.maximum(i - 1, 0)
        quads = lax.shift_right_logical(between, 2)
        pairs = jnp.bitwise_and(lax.shift_right_logical(between, 1), 1)
        make_step(True, 1)(0, 0)
        lax.fori_loop(0, quads, make_step(False, 4, first=1), 0)
        lax.fori_loop(0, pairs, make_step(False, 2, first=1 + 4 * quads), 0)
        lax.fori_loop(1 + 4 * quads + 2 * pairs, i, make_step(False, 1), 0)
        lax.fori_loop(jnp.maximum(i, 1), i + 1, make_step(True, 1), 0)
        for h in range(N_HEADS):
            l = sum_ref[h]
            ot_ref[h] = ot_ref[h] / l
            lse_ref[h] = max_ref[h] + jnp.log2(l)

    out_shapes = [jax.ShapeDtypeStruct((N_HEADS, V_HEAD, n_rows), F32), jax.ShapeDtypeStruct((N_HEADS, 1, n_rows), F32)]
    return pl.pallas_call(
        body, name="attn_fwd", grid=(nt,),
        in_specs=[_lane_tile(q_t.shape), _whole_spec(k.shape), _whole_spec(v_t.shape)],
        out_specs=[_lane_tile(s.shape) for s in out_shapes],
        out_shape=out_shapes,
        scratch_shapes=[pltpu.VMEM((N_HEADS, 1, ROW_TILE), F32), pltpu.VMEM((N_HEADS, 1, ROW_TILE), F32)],
        compiler_params=_params("parallel"),
    )(q_t, k, v_t)


def _heads_to_rows(ref):
    return jnp.concatenate([ref[h] for h in range(N_HEADS)], axis=0)


def _rms_cols(x, g_col):
    r = lax.rsqrt(jnp.mean(x * x, axis=0, keepdims=True) + EPS)
    return x * r * g_col


def _fwd_out(x, meta_pad, mix_a, o_t, gb_col, w_out, n_rows):
    nt = n_rows // ROW_TILE

    def body(x_ref, meta_ref, mixa_ref, ot_ref, gb_ref, w_ref, mixbt_ref, h1_ref):
        i = pl.program_id(0)
        h0 = jnp.where(i == 0, meta_ref[...], x_ref[...])
        mix_bt = _rms_cols(_heads_to_rows(ot_ref), gb_ref[...]).astype(BF16)
        mixbt_ref[...] = mix_bt
        h1_ref[...] = h0 + _dot(mixa_ref[...], w_ref[:D_CONV, :]) + _dot_tn(mix_bt, w_ref[D_CONV:, :])

    out_shapes = [jax.ShapeDtypeStruct((D_ATTN, n_rows), BF16), jax.ShapeDtypeStruct((n_rows, D_MODEL), F32)]
    return pl.pallas_call(
        body, name="fwd_out", grid=(nt,),
        in_specs=[_real_spec(D_MODEL), _whole_spec(meta_pad.shape), _tile_spec(mix_a.shape), _lane_tile(o_t.shape),
                  _whole_spec(gb_col.shape), _whole_spec(w_out.shape)],
        out_specs=[_lane_tile(out_shapes[0].shape), _tile_spec(out_shapes[1].shape)],
        out_shape=out_shapes,
        compiler_params=_params("parallel"),
    )(x, meta_pad, mix_a, o_t, gb_col, w_out)


PHASES = 8
PHASE_ROWS = ROW_TILE // PHASES
UP_PLANES = -(-UP_SLAB // _LANES)
UP_PAD = UP_PLANES * _LANES
CONV_PLANES = D_CONV // _LANES


def _phase(start):
    return pl.ds(start, PHASE_ROWS, stride=PHASES)


def _to_planes(ref, lead, rows, value):
    width = value.shape[-1]
    for c in range(-(-width // _LANES)):
        part = value[:, c * _LANES:min((c + 1) * _LANES, width)]
        if part.shape[-1] < _LANES:
            part = jnp.concatenate([part, jnp.zeros((part.shape[0], _LANES - part.shape[-1]), part.dtype)], axis=-1)
        ref[(*lead, c, rows, slice(None))] = part


def _from_planes(ref, lead, width):
    planes = [ref[(*lead, c)] for c in range(-(-width // _LANES))]
    last = width - (len(planes) - 1) * _LANES
    return jnp.concatenate(planes[:-1] + [planes[-1][:, :last]], axis=-1)


def _fwd_ffn(h1, target, g2, w_up, fw, fb, w_down, gf, n_rows):
    nt = n_rows // ROW_TILE

    def body(h1_ref, t_ref, g2_ref, wup_ref, fw_ref, fb_ref, wdn_ref, gf_ref,
             n2_ref, up0_ref, act_ref, da_ref, db_ref, dh2_ref, loss_ref, dgf_ref, ext_ref):
        i = pl.program_id(0)

        @pl.when(i == 0)
        def _():
            ext_ref[:, 0:FFN_HALO, :] = jnp.zeros((N_DEV, FFN_HALO, UP_SLAB), F32)

        h1_t = h1_ref[...]
        live = _row_ids(i, ROW_TILE) >= DEAD
        n2 = jnp.where(live, _rms_fwd(h1_t, g2_ref[...]), 0.0).astype(BF16)
        n2_ref[...] = n2
        for s in range(N_DEV):
            up0 = _dot_nt(n2, wup_ref[s])
            up0_ref[s] = up0.astype(BF16)
            ext_ref[s, FFN_HALO:, :] = up0
        first = FFN_HALO - (FFN_CONV_WIDTH - 1)

        def conv(s):
            block = ext_ref[s]
            acc = fb_ref[s, :, :UP_SLAB] + fw_ref[s, FFN_CONV_WIDTH - 1:FFN_CONV_WIDTH, :UP_SLAB] * block[FFN_HALO:]
            for back in range(1, FFN_CONV_WIDTH):
                k = FFN_CONV_WIDTH - 1 - back
                acc = acc + fw_ref[s, k:k + 1, :UP_SLAB] * pltpu.roll(block, back, 0)[FFN_HALO:]
            return acc

        h2 = h1_t
        for s in range(N_ACT_SLAB):
            gate = conv(s)
            val = conv(s + N_ACT_SLAB)
            sg = _sigmoid(gate)
            silu = gate * sg
            act = (silu * val).astype(BF16)
            act_ref[s] = act
            da_ref[s] = (val * sg * (1.0 + gate * (1.0 - sg))).astype(BF16)
            db_ref[s] = silu.astype(BF16)
            h2 = h2 + _dot(act, wdn_ref[s])
        ext_ref[:, 0:FFN_HALO, :] = ext_ref[:, ROW_TILE:ROW_TILE + FFN_HALO, :]

        gf_t = gf_ref[...]
        y = _rms_fwd(h2, gf_t)
        diff = jnp.where(i >= 1, y - t_ref[...], 0.0)
        tile_loss = 0.5 * jnp.sum(jnp.sum(diff * diff, axis=-1, keepdims=True), axis=0, keepdims=True) / D_MODEL
        dh2, dgf = _rms_bwd(diff / D_MODEL, h2, gf_t)
        dh2_ref[...] = dh2
        _accumulate(loss_ref, i == 0, jnp.broadcast_to(tile_loss, loss_ref.shape))
        _accumulate(dgf_ref, i == 0, dgf)

    act_like = jax.ShapeDtypeStruct((N_ACT_SLAB, n_rows, UP_SLAB), BF16)
    out_shapes = [
        jax.ShapeDtypeStruct((n_rows, D_MODEL), BF16),
        jax.ShapeDtypeStruct((N_DEV, n_rows, UP_SLAB), BF16),
        act_like, act_like, act_like,
        jax.ShapeDtypeStruct((n_rows, D_MODEL), F32),
        jax.ShapeDtypeStruct((8, 128), F32),
        jax.ShapeDtypeStruct((1, D_MODEL), F32),
    ]
    whole = [g2, w_up, fw, fb, w_down, gf]
    return pl.pallas_call(
        body, name="fwd_ffn", grid=(nt,),
        in_specs=[_tile_spec(h1.shape), _real_spec(D_MODEL)] + [_whole_spec(a.shape) for a in whole],
        out_specs=[_tile_spec(s.shape) for s in out_shapes[:6]] + [_acc_spec(s.shape) for s in out_shapes[6:]],
        out_shape=out_shapes,
        scratch_shapes=[pltpu.VMEM((N_DEV, ROW_TILE + FFN_HALO, UP_SLAB), F32)],
        compiler_params=_params("arbitrary"),
    )(h1, target, *whole)


def _rope_tables(n_rows):
    pos = jnp.maximum(jnp.arange(n_rows, dtype=jnp.int32) - DEAD, 0)
    inv_freq = 1.0 / (ROPE_THETA ** (jnp.arange(0, QK_ROPE, 2, dtype=F32) / QK_ROPE))
    ang_t = inv_freq[:, None] * pos.astype(F32)[None, :]
    return jnp.cos(ang_t), jnp.sin(ang_t)


def _halo_after(shape, halo, n_rows):
    last = n_rows // halo - 1
    step = ROW_TILE // halo
    if len(shape) == 2:
        return pl.BlockSpec((halo, shape[1]), lambda i: (jnp.minimum((i + 1) * step, last), 0))
    return pl.BlockSpec((shape[0], halo, shape[2]), lambda i: (0, jnp.minimum((i + 1) * step, last), 0))


def _bwd_ffn_act(dh2, da, db, w_down, n_rows):
    nt = n_rows // ROW_TILE

    def body(dh2_ref, da_ref, db_ref, wdn_ref, dup_ref, dfb_ref):
        i = pl.program_id(0)

        @pl.when(i == 0)
        def _():
            dfb_ref[...] = jnp.zeros_like(dfb_ref)

        dh2_b = dh2_ref[...].astype(BF16)
        for s in range(N_ACT_SLAB):
            d_act = _dot_nt(dh2_b, wdn_ref[s])
            d_gate = d_act * da_ref[s].astype(F32)
            d_val = d_act * db_ref[s].astype(F32)
            dup_ref[s] = d_gate.astype(BF16)
            dup_ref[s + N_ACT_SLAB] = d_val.astype(BF16)
            dfb_ref[s] += jnp.sum(d_gate, axis=0, keepdims=True)
            dfb_ref[s + N_ACT_SLAB] += jnp.sum(d_val, axis=0, keepdims=True)

    out_shapes = [jax.ShapeDtypeStruct((N_DEV, n_rows, UP_SLAB), BF16), jax.ShapeDtypeStruct((N_DEV, 1, UP_SLAB), F32)]
    return pl.pallas_call(
        body, name="bwd_ffn_act", grid=(nt,),
        in_specs=[_tile_spec(dh2.shape), _tile_spec(da.shape), _tile_spec(db.shape), _whole_spec(w_down.shape)],
        out_specs=[_tile_spec(out_shapes[0].shape), _acc_spec(out_shapes[1].shape)],
        out_shape=out_shapes,
        compiler_params=_params("arbitrary"),
    )(dh2, da, db, w_down)


def _bwd_ffn_up(dup, up0, h1, dh2, g2, w_up, fw, n_rows):
    nt = n_rows // ROW_TILE
    last_tap = FFN_CONV_WIDTH - 1
    ext_rows = ROW_TILE + FFN_HALO

    def body(dup_ref, dnext_ref, up0_ref, h1_ref, dh2_ref, g2_ref, wup_ref, fw_ref,
             dup0_ref, dh1_ref, dfw_ref, dg2_ref):
        i = pl.program_id(0)

        @pl.when(i == 0)
        def _():
            dfw_ref[...] = jnp.zeros_like(dfw_ref)

        live = _row_ids(i, ROW_TILE) >= DEAD
        dn2 = jnp.zeros((ROW_TILE, D_MODEL), F32)
        for s in range(N_DEV):
            d = dup_ref[s].astype(F32)
            block = jnp.concatenate([d, jnp.where(i == nt - 1, 0.0, dnext_ref[s].astype(F32))], axis=0)
            u = up0_ref[s].astype(F32)
            dup0 = fw_ref[s, last_tap:last_tap + 1, :UP_SLAB] * d
            dfw_ref[s, last_tap:last_tap + 1, :UP_SLAB] += jnp.sum(d * u, axis=0, keepdims=True)
            for ahead in range(1, FFN_CONV_WIDTH):
                k = last_tap - ahead
                shifted = pltpu.roll(block, ext_rows - ahead, 0)[:ROW_TILE]
                dup0 = dup0 + fw_ref[s, k:k + 1, :UP_SLAB] * shifted
                dfw_ref[s, k:k + 1, :UP_SLAB] += jnp.sum(shifted * u, axis=0, keepdims=True)
            dup0_b = dup0.astype(BF16)
            dup0_ref[s] = dup0_b
            dn2 = dn2 + _dot(dup0_b, wup_ref[s])
        dn2 = jnp.where(live, dn2, 0.0)
        dx, dg2 = _rms_bwd(dn2, h1_ref[...], g2_ref[...])
        dh1_ref[...] = dh2_ref[...] + dx
        _accumulate(dg2_ref, i == 0, dg2)

    out_shapes = [
        jax.ShapeDtypeStruct((N_DEV, n_rows, UP_SLAB), BF16),
        jax.ShapeDtypeStruct((n_rows, D_MODEL), F32),
        jax.ShapeDtypeStruct((N_DEV, FFN_CONV_WIDTH, UP_PAD), F32),
        jax.ShapeDtypeStruct((1, D_MODEL), F32),
    ]
    return pl.pallas_call(
        body, name="bwd_ffn_up", grid=(nt,),
        in_specs=[_tile_spec(dup.shape), _halo_after(dup.shape, FFN_HALO, n_rows), _tile_spec(up0.shape),
                  _tile_spec(h1.shape), _tile_spec(dh2.shape),
                  _whole_spec(g2.shape), _whole_spec(w_up.shape), _whole_spec(fw.shape)],
        out_specs=[_tile_spec(s.shape) for s in out_shapes[:2]] + [_acc_spec(s.shape) for s in out_shapes[2:]],
        out_shape=out_shapes,
        compiler_params=_params("arbitrary"),
    )(dup, dup, up0, h1, dh2, g2, w_up, fw)


def _bwd_out(dh1, o_t, u1, w_out, gb_col, ln_g, ln_b, ga, n_rows):
    nt = n_rows // ROW_TILE

    def body(dh1_ref, ot_ref, u1_ref, w_ref, gb_ref, lg_ref, lb_ref, ga_ref,
             dot_ref, delta_ref, du1_ref, dgb_ref, dga_ref, dlg_ref, dlb_ref, dcb_ref):
        i = pl.program_id(0)
        dh1_b = dh1_ref[...].astype(BF16)
        o_t = _heads_to_rows(ot_ref)
        gb = gb_ref[...]
        r = lax.rsqrt(jnp.mean(o_t * o_t, axis=0, keepdims=True) + EPS)
        dmix_bt = _dot_nt(w_ref[D_CONV:, :], dh1_b)
        wgt = dmix_bt * gb
        do_t = r * wgt - o_t * (r * r * r) * jnp.mean(wgt * o_t, axis=0, keepdims=True)
        dgb = jnp.sum(dmix_bt * o_t * r, axis=1, keepdims=True)
        for h in range(N_HEADS):
            do_h = do_t[h * V_HEAD:(h + 1) * V_HEAD]
            dot_ref[h] = do_h.astype(BF16)
            delta_ref[h] = jnp.sum(do_h * ot_ref[h], axis=0, keepdims=True)
        lg = lg_ref[...]
        xh, u2, u3, rstd = _conv_chain(u1_ref[...], lg, lb_ref[...])
        du3, dga = _rms_bwd(_dot_nt(dh1_b, w_ref[:D_CONV, :]), u3, ga_ref[...])
        sg = _sigmoid(u2)
        du2 = du3 * sg * (1.0 + u2 * (1.0 - sg))
        dxh = du2 * lg
        du1 = rstd * (dxh - jnp.mean(dxh, axis=-1, keepdims=True) - xh * jnp.mean(dxh * xh, axis=-1, keepdims=True))
        du1_ref[...] = du1
        first = i == 0
        _accumulate(dgb_ref, first, dgb)
        _accumulate(dga_ref, first, dga)
        _accumulate(dlg_ref, first, jnp.sum(du2 * xh, axis=0, keepdims=True))
        _accumulate(dlb_ref, first, jnp.sum(du2, axis=0, keepdims=True))
        _accumulate(dcb_ref, first, jnp.sum(du1, axis=0, keepdims=True))

    out_shapes = [
        jax.ShapeDtypeStruct((N_HEADS, V_HEAD, n_rows), BF16),
        jax.ShapeDtypeStruct((N_HEADS, 1, n_rows), F32),
        jax.ShapeDtypeStruct((n_rows, D_CONV), F32),
        jax.ShapeDtypeStruct((D_ATTN, 1), F32),
    ] + [jax.ShapeDtypeStruct((1, D_CONV), F32)] * 4
    whole = [w_out, gb_col, ln_g, ln_b, ga]
    return pl.pallas_call(
        body, name="bwd_out", grid=(nt,),
        in_specs=[_tile_spec(dh1.shape), _lane_tile(o_t.shape), _tile_spec(u1.shape)] + [_whole_spec(a.shape) for a in whole],
        out_specs=[_lane_tile(out_shapes[0].shape), _lane_tile(out_shapes[1].shape), _tile_spec(out_shapes[2].shape)]
        + [_acc_spec(s.shape) for s in out_shapes[3:]],
        out_shape=out_shapes,
        compiler_params=_params("arbitrary"),
    )(dh1, o_t, u1, *whole)


ATTN_BWD_HEADS = 8


def _attn_bwd(q_t, k, v, do_t, lse, delta, n_rows):
    nt = n_rows // ROW_TILE
    hp = ATTN_BWD_HEADS

    def body(k_ref, v_ref, qt_ref, dot_ref, lse_ref, delta_ref, dqt_ref, dk_ref, dv_ref):
        j = pl.program_id(1)

        @pl.when(j == 0)
        def _():
            dqt_ref[...] = jnp.zeros_like(dqt_ref)

        k_ts = [k_ref[h] for h in range(hp)]
        v_ts = [v_ref[h] for h in range(hp)]

        def make_step(masked, tiles, first=0):
            def step(t, carry):
                tiles_of_step = []
                for u in range(tiles):
                    i = first + tiles * t + u
                    cols = pl.ds(pl.multiple_of(i * ROW_TILE, ROW_TILE), ROW_TILE)
                    q_is = [qt_ref[h, :, cols] for h in range(hp)]
                    do_is = [dot_ref[h, :, cols] for h in range(hp)]
                    scores = [_dot(k_ts[h], q_is[h]) for h in range(hp)]
                    dps = [_dot(v_ts[h], do_is[h]) for h in range(hp)]
                    tiles_of_step.append((i, cols, q_is, do_is, scores, dps))
                for i, cols, q_is, do_is, scores, dps in tiles_of_step:
                    visible = _visible(i, j) if masked else None
                    probs, dss = [], []
                    for h in range(hp):
                        s = jnp.where(visible, scores[h], NEG) if masked else scores[h]
                        p = jnp.exp2(s - lse_ref[h, :, cols])
                        probs.append(p.astype(BF16))
                        dss.append((p * (dps[h] - delta_ref[h, :, cols])).astype(BF16))
                    for h in range(hp):
                        dv_ref[h] += _dot_nt(probs[h], do_is[h])
                        dk_ref[h] += _dot_nt(dss[h], q_is[h])
                        dqt_ref[h, :, cols] += _dot_tn(k_ts[h], dss[h])
                return carry
            return step

        dk_ref[...] = jnp.zeros_like(dk_ref)
        dv_ref[...] = jnp.zeros_like(dv_ref)
        make_step(True, 1)(j, 0)
        lax.fori_loop(jnp.where(j == 0, j + 1, nt), nt, make_step(True, 1), 0)
        unmasked = jnp.where(j == 0, 0, nt - 1 - j)
        quads = lax.shift_right_logical(unmasked, 2)
        pairs = jnp.bitwise_and(lax.shift_right_logical(unmasked, 1), 1)
        lax.fori_loop(0, quads, make_step(False, 4, first=j + 1), 0)
        lax.fori_loop(0, pairs, make_step(False, 2, first=j + 1 + 4 * quads), 0)
        lax.fori_loop(jnp.where(j == 0, nt, j + 1 + 4 * quads + 2 * pairs), nt, make_step(False, 1), 0)
        dk_ref[...] = dk_ref[...] * _LN2

    key_tile = lambda w: pl.BlockSpec((hp, ROW_TILE, w), lambda g, j: (g, j, 0))
    all_cols = lambda w: pl.BlockSpec((hp, w, n_rows), lambda g, j: (g, 0, 0))
    resident = lambda w: pl.BlockSpec((hp, w, n_rows), lambda g, j: (g, 0, 0), pipeline_mode=pl.Buffered(1))
    out_shapes = [
        jax.ShapeDtypeStruct((N_HEADS, QK_DIM, n_rows), F32),
        jax.ShapeDtypeStruct((N_HEADS, n_rows, QK_DIM), F32),
        jax.ShapeDtypeStruct((N_HEADS, n_rows, V_HEAD), F32),
    ]
    return pl.pallas_call(
        body, name="attn_bwd", grid=(N_HEADS // hp, nt),
        in_specs=[key_tile(QK_DIM), key_tile(V_HEAD), resident(QK_DIM), resident(V_HEAD), resident(1), resident(1)],
        out_specs=[all_cols(QK_DIM), key_tile(QK_DIM), key_tile(V_HEAD)],
        out_shape=out_shapes,
        compiler_params=_params("parallel", "arbitrary"),
    )(k, v, q_t, do_t, lse, delta)


def _bwd_qkv(dq_t, dk, dv, cq, ckv, gq, gkv, wq_t, w_ukv, cos, sin, cos_t, sin_t, n_rows):
    nt = n_rows // ROW_TILE

    def body(dqt_ref, dk_ref, dv_ref, cq_ref, ckv_ref, gq_ref, gkv_ref, wqt_ref, wkv_ref, cos_ref, sin_ref,
             cost_ref, sint_ref, dqraw_ref, dkv_ref, dcq_ref, dckv_ref, dkr_ref, dgq_ref, dgkv_ref):
        i = pl.program_id(0)
        cos_rows, sin_rows = cost_ref[...], sint_ref[...]
        dcqn = jnp.zeros((ROW_TILE, Q_LORA), F32)
        dckvn = jnp.zeros((ROW_TILE, KV_LORA), F32)
        dk_rot = jnp.zeros((ROW_TILE, QK_ROPE), F32)
        for h in range(N_HEADS):
            dq_h, dk_h = dqt_ref[h] * QK_DIM ** -0.5, dk_ref[h]
            dq_raw = jnp.concatenate(
                [dq_h[:QK_NOPE], _rope_rows_t(dq_h[QK_NOPE:], cos_rows, sin_rows)], axis=0).astype(BF16)
            dqraw_ref[h] = dq_raw
            dcqn = dcqn + _dot_tn(dq_raw, wqt_ref[h])
            dkv = jnp.concatenate([dk_h[:, :QK_NOPE], dv_ref[h]], axis=-1).astype(BF16)
            dkv_ref[:, h * KV_HEAD:(h + 1) * KV_HEAD] = dkv
            dckvn = dckvn + _dot_nt(dkv, wkv_ref[h])
            dk_rot = dk_rot + dk_h[:, QK_NOPE:]
        dkr_ref[...] = _rope_t(dk_rot, cos_ref[...], sin_ref[...]).astype(BF16)
        dcq, dgq = _rms_bwd(dcqn, cq_ref[...], gq_ref[...])
        dckv, dgkv = _rms_bwd(dckvn, ckv_ref[...], gkv_ref[...])
        dcq_ref[...] = dcq.astype(BF16)
        dckv_ref[...] = dckv.astype(BF16)
        _accumulate(dgq_ref, i == 0, dgq)
        _accumulate(dgkv_ref, i == 0, dgkv)

    out_shapes = [
        jax.ShapeDtypeStruct((N_HEADS, QK_DIM, n_rows), BF16),
        jax.ShapeDtypeStruct((n_rows, N_HEADS * KV_HEAD), BF16),
        jax.ShapeDtypeStruct((n_rows, Q_LORA), BF16),
        jax.ShapeDtypeStruct((n_rows, KV_LORA), BF16),
        jax.ShapeDtypeStruct((n_rows, QK_ROPE), BF16),
        jax.ShapeDtypeStruct((1, Q_LORA), F32),
        jax.ShapeDtypeStruct((1, KV_LORA), F32),
    ]
    tiles = [dk, dv, cq, ckv]
    whole = [gq, gkv, wq_t, w_ukv]
    return pl.pallas_call(
        body, name="bwd_qkv", grid=(nt,),
        in_specs=[_lane_tile(dq_t.shape)] + [_tile_spec(a.shape) for a in tiles] + [_whole_spec(a.shape) for a in whole]
        + [_tile_spec(cos.shape), _tile_spec(sin.shape), _lane_tile(cos_t.shape), _lane_tile(sin_t.shape)],
        out_specs=[_lane_tile(out_shapes[0].shape)] + [_tile_spec(s.shape) for s in out_shapes[1:5]]
        + [_acc_spec(s.shape) for s in out_shapes[5:]],
        out_shape=out_shapes,
        compiler_params=_params("arbitrary"),
    )(dq_t, *tiles, *whole, cos, sin, cos_t, sin_t)


def _bwd_conv(du1, ag, conv_w, dcq, dckv, dkr, n_rows):
    nt = n_rows // ROW_TILE

    last_tap = CONV_WIDTH - 1

    def body(du1_ref, dnext_ref, ag_ref, w_ref, dcq_ref, dckv_ref, dkr_ref, dz_ref, dw_ref,
             dext_ref, uext_ref, conv_ref, sums_ref):
        i = pl.program_id(0)

        @pl.when(i == 0)
        def _():
            sums_ref[...] = jnp.zeros_like(sums_ref)

        _to_planes(dext_ref, (), slice(0, ROW_TILE), du1_ref[...])
        _to_planes(dext_ref, (), slice(ROW_TILE, None), jnp.where(i == nt - 1, 0.0, dnext_ref[...]))
        ag_t = ag_ref[...]
        live = _row_ids(i, ROW_TILE) >= DEAD
        sg = _sigmoid(ag_t[:, D_CONV:])
        _to_planes(uext_ref, (), slice(None), jnp.where(live, ag_t[:, :D_CONV] * sg, 0.0))
        for c in range(CONV_PLANES):
            taps = w_ref[:, c * _LANES:(c + 1) * _LANES]
            for half in range(0, PHASES, PHASES // 2):
                phases = range(half, half + PHASES // 2)
                us = {p: uext_ref[c, _phase(p), :] for p in phases}
                accs = {p: jnp.zeros((PHASE_ROWS, _LANES), F32) for p in phases}
                for k in range(CONV_WIDTH):
                    tap_sum = jnp.zeros((PHASE_ROWS, _LANES), F32)
                    for p in phases:
                        shifted = dext_ref[c, _phase(p + last_tap - k), :]
                        accs[p] = accs[p] + taps[k:k + 1, :] * shifted
                        tap_sum = tap_sum + shifted * us[p]
                    sums_ref[c, k] += tap_sum
                for p in phases:
                    conv_ref[c, _phase(p), :] = accs[p]
        du0 = jnp.where(live, _from_planes(conv_ref, (), D_CONV), 0.0)
        da = du0 * sg
        dgate = du0 * ag_t[:, :D_CONV] * sg * (1.0 - sg)
        dz_ref[...] = jnp.concatenate(
            [da.astype(BF16), dgate.astype(BF16), dcq_ref[...], dckv_ref[...], dkr_ref[...]], axis=-1)

        @pl.when(i == nt - 1)
        def _():
            for c in range(CONV_PLANES):
                for k in range(CONV_WIDTH):
                    dw_ref[k:k + 1, c * _LANES:(c + 1) * _LANES] = jnp.sum(sums_ref[c, k], axis=0, keepdims=True)

    out_shapes = [jax.ShapeDtypeStruct((n_rows, D_IN), BF16), jax.ShapeDtypeStruct((CONV_WIDTH, D_CONV), F32)]
    return pl.pallas_call(
        body, name="bwd_conv", grid=(nt,),
        in_specs=[_tile_spec(du1.shape), _halo_after(du1.shape, CONV_HALO, n_rows), _tile_spec(ag.shape),
                  _whole_spec(conv_w.shape), _tile_spec(dcq.shape), _tile_spec(dckv.shape), _tile_spec(dkr.shape)],
        out_specs=[_tile_spec(out_shapes[0].shape), _acc_spec(out_shapes[1].shape)],
        out_shape=out_shapes,
        scratch_shapes=[pltpu.VMEM((CONV_PLANES, ROW_TILE + CONV_HALO, _LANES), F32),
                        pltpu.VMEM((CONV_PLANES, ROW_TILE, _LANES), F32), pltpu.VMEM((CONV_PLANES, ROW_TILE, _LANES), F32),
                        pltpu.VMEM((CONV_PLANES, CONV_WIDTH, PHASE_ROWS, _LANES), F32)],
        compiler_params=_params("arbitrary"),
    )(du1, du1, ag, conv_w, dcq, dckv, dkr)


def _bwd_in(dz, x, meta_pad, dh1, g1, w_in, n_rows):
    nt = n_rows // ROW_TILE

    def body(dz_ref, x_ref, meta_ref, dh1_ref, g_ref, w_ref, gx_ref, gmeta_ref, dg1_ref):
        i = pl.program_id(0)
        h0 = jnp.where(i == 0, meta_ref[...], x_ref[...])
        dx, dg1 = _rms_bwd(_dot(dz_ref[...], w_ref[...]), h0, g_ref[...])
        dh0 = dh1_ref[...] + dx
        gx_ref[...] = dh0

        @pl.when(i == 0)
        def _():
            gmeta_ref[...] = dh0

        _accumulate(dg1_ref, i == 0, dg1)

    out_shapes = [
        jax.ShapeDtypeStruct((n_rows - ROW_TILE, D_MODEL), F32),
        jax.ShapeDtypeStruct((ROW_TILE, D_MODEL), F32),
        jax.ShapeDtypeStruct((1, D_MODEL), F32),
    ]
    return pl.pallas_call(
        body, name="bwd_in", grid=(nt,),
        in_specs=[_tile_spec(dz.shape), _real_spec(D_MODEL), _whole_spec(meta_pad.shape), _tile_spec(dh1.shape),
                  _whole_spec(g1.shape), _whole_spec(w_in.shape)],
        out_specs=[_real_spec(D_MODEL), _acc_spec(out_shapes[1].shape), _acc_spec(out_shapes[2].shape)],
        out_shape=out_shapes,
        compiler_params=_params("arbitrary"),
    )(dz, x, meta_pad, dh1, g1, w_in)


def _contraction_tile(n_rows):
    return next(t for t in range(n_rows // 2 // _LANES * _LANES, 0, -_LANES) if n_rows % t == 0)


def _weight_grad(a, b, name, a_transposed=False):
    groups = max(a.shape[0] if a.ndim == 3 else 1, b.shape[0] if b.ndim == 3 else 1)
    n_rows, n = b.shape[-2], b.shape[-1]
    m = a.shape[-2] if a_transposed else a.shape[-1]
    kt = _contraction_tile(n_rows)
    steps = n_rows // kt

    def body(a_ref, b_ref, out_ref, acc_ref):
        i = pl.program_id(1)
        a_t, b_t = a_ref[...].astype(BF16), b_ref[...].astype(BF16)
        part = _dot(a_t, b_t) if a_transposed else _dot_tn(a_t, b_t)
        _accumulate(acc_ref, i == 0, part)

        @pl.when(i == steps - 1)
        def _():
            out_ref[...] = acc_ref[...].astype(out_ref.dtype)

    def spec(arr, rows_last):
        block = (arr.shape[-2], kt) if rows_last else (kt, arr.shape[-1])
        at = (lambda i: (0, i)) if rows_last else (lambda i: (i, 0))
        if arr.ndim == 3:
            return pl.BlockSpec((None,) + block, lambda g, i: (g,) + at(i))
        return pl.BlockSpec(block, lambda g, i: at(i))

    return pl.pallas_call(
        body, name=name, grid=(groups, steps),
        in_specs=[spec(a, a_transposed), spec(b, False)],
        out_specs=pl.BlockSpec((None, m, n), lambda g, i: (g, 0, 0)),
        out_shape=jax.ShapeDtypeStruct((groups, m, n), BF16),
        scratch_shapes=[pltpu.VMEM((m, n), F32)],
        compiler_params=_params("parallel", "arbitrary"),
    )(a, b)


def _my_index():
    return 4 * lax.axis_index("x") + 2 * lax.axis_index("y") + lax.axis_index("c")


def _peer(k):
    flip = lambda v, bit: 1 - v if bit else v
    px = flip(lax.axis_index("x"), k & 4)
    py = flip(lax.axis_index("y"), k & 2)
    pc = flip(lax.axis_index("c"), k & 1)
    return (px, py, pc), 4 * px + 2 * py + pc


def _all_gather(shards, dtypes):
    n = len(shards)
    sibling, chips = 1, (2, 4, 6)

    def body(*refs):
        ins, outs, stages = refs[:n], refs[n:2 * n], refs[2 * n:3 * n]
        send_sems, recv_sems, local_sems = refs[3 * n:]
        me = _my_index()
        for a in range(n):
            stages[a][...] = ins[a][...].astype(stages[a].dtype)
        local = [pltpu.make_async_copy(stages[a], outs[a].at[me], local_sems.at[a]) for a in range(n)]
        for cp in local:
            cp.start()

        def copy(a, k, src, slot, to):
            return pltpu.make_async_remote_copy(
                src_ref=src, dst_ref=outs[a].at[slot], send_sem=send_sems.at[a, k - 1],
                recv_sem=recv_sems.at[a, k - 1], device_id=_peer(to)[0], device_id_type=MESH)

        def own(a, k):
            return copy(a, k, stages[a], me, k)

        def passed(a, k):
            slot = _peer(k)[1]
            return copy(a, k ^ sibling, outs[a].at[slot], slot, sibling)

        def arrival(a, k):
            return copy(a, k, stages[a], _peer(k)[1], k)

        for k in (sibling,) + chips:
            for a in range(n):
                own(a, k).start()
        for k in chips:
            for a in range(n):
                arrival(a, k).wait_recv()
                passed(a, k).start()
        for a in range(n):
            arrival(a, sibling).wait_recv()
            for k in chips:
                arrival(a, k ^ sibling).wait_recv()
        for a in range(n):
            for k in (sibling,) + chips:
                own(a, k).wait_send()
            for k in chips:
                passed(a, k).wait_send()
        for cp in local:
            cp.wait()

    return pl.pallas_call(
        body, name="gather_weights",
        in_specs=[pl.BlockSpec(memory_space=pltpu.VMEM)] * n,
        out_specs=[pl.BlockSpec(memory_space=pl.ANY)] * n,
        out_shape=[jax.ShapeDtypeStruct((N_DEV,) + s.shape, dt) for s, dt in zip(shards, dtypes)],
        scratch_shapes=[pltpu.VMEM(s.shape, dt) for s, dt in zip(shards, dtypes)]
        + [pltpu.SemaphoreType.DMA((n, N_DEV - 1)), pltpu.SemaphoreType.DMA((n, N_DEV - 1)), pltpu.SemaphoreType.DMA((n,))],
        compiler_params=pltpu.CompilerParams(vmem_limit_bytes=VMEM_LIMIT),
    )(*shards)


def _exchange(parts, whole):
    n = len(parts)

    def body(*refs):
        ins, outs = refs[:n], refs[n:2 * n]
        send_sems, recv_sems, local_sems = refs[2 * n:]
        me = _my_index()

        def src(a, slab):
            return ins[a] if whole[a] else ins[a].at[slab]

        local = [pltpu.make_async_copy(src(a, me), outs[a].at[me], local_sems.at[a]) for a in range(n)]
        for cp in local:
            cp.start()

        def copy(a, k, slab, slot):
            peer, _ = _peer(k)
            return pltpu.make_async_remote_copy(
                src_ref=src(a, slab), dst_ref=outs[a].at[slot], send_sem=send_sems.at[a, k - 1],
                recv_sem=recv_sems.at[a, k - 1], device_id=peer, device_id_type=MESH)

        for k in range(1, N_DEV):
            for a in range(n):
                copy(a, k, _peer(k)[1], me).start()
        for k in range(1, N_DEV):
            for a in range(n):
                copy(a, k, _peer(k)[1], _peer(k)[1]).wait()
        for cp in local:
            cp.wait()

    return pl.pallas_call(
        body, name="exchange_grads",
        in_specs=[pl.BlockSpec(memory_space=pl.ANY)] * n,
        out_specs=[pl.BlockSpec(memory_space=pl.ANY)] * n,
        out_shape=[jax.ShapeDtypeStruct(((N_DEV,) + p.shape) if w else p.shape, p.dtype) for p, w in zip(parts, whole)],
        scratch_shapes=[pltpu.SemaphoreType.DMA((n, N_DEV - 1)), pltpu.SemaphoreType.DMA((n, N_DEV - 1)),
                        pltpu.SemaphoreType.DMA((n,))],
    )(*parts)


def _sequencer_exchange(parts, whole, name, collective_id):
    n = len(parts)
    srcs = [jax.new_ref(p, memory_space=pltpu.MemorySpace.HBM) for p in parts]
    lands = [jax.empty_ref(jax.ShapeDtypeStruct(((N_DEV,) + p.shape) if w else p.shape, p.dtype),
                           memory_space=pltpu.MemorySpace.HBM) for p, w in zip(parts, whole)]

    @pl.kernel(mesh=plsc.ScalarSubcoreMesh(axis_name="sequencer", num_cores=1), name=name,
               scratch_types=(pltpu.SemaphoreType.DMA((n, N_DEV - 1)), pltpu.SemaphoreType.DMA((n, N_DEV - 1)),
                              pltpu.SemaphoreType.DMA((n,))),
               compiler_params=pltpu.CompilerParams(collective_id=collective_id))
    def launch(send_sems, recv_sems, local_sems):
        barrier = pltpu.get_barrier_semaphore()
        for k in range(1, N_DEV):
            pl.semaphore_signal(barrier, inc=1, device_id=_peer(k)[0], device_id_type=MESH)
        pl.semaphore_wait(barrier, N_DEV - 1)
        me = _my_index()

        def src(a, slab):
            return srcs[a] if whole[a] else srcs[a].at[slab]

        local = [pltpu.make_async_copy(src(a, me), lands[a].at[me], local_sems.at[a]) for a in range(n)]
        for cp in local:
            cp.start()

        def copy(a, k, slab, slot):
            return pltpu.make_async_remote_copy(
                src_ref=src(a, slab), dst_ref=lands[a].at[slot], send_sem=send_sems.at[a, k - 1],
                recv_sem=recv_sems.at[a, k - 1], device_id=_peer(k)[0], device_id_type=MESH)

        for k in range(1, N_DEV):
            for a in range(n):
                copy(a, k, _peer(k)[1], me).start()
        for k in range(1, N_DEV):
            for a in range(n):
                copy(a, k, _peer(k)[1], _peer(k)[1]).wait()
        for cp in local:
            cp.wait()

    launch()
    return [land[...] for land in lands]


def _row_block(rows):
    if rows <= ROW_TILE:
        return rows
    return next(rb for rb in range(ROW_TILE, 0, -16) if rows % rb == 0)


def _adamw(landing, w, m, v, name):
    rows, cols = w.shape
    rb = _row_block(rows)

    def body(l_ref, w_ref, m_ref, v_ref, g_ref, d_ref, m2_ref, v2_ref):
        g = l_ref[0].astype(F32)
        for p in range(1, N_DEV):
            g = g + l_ref[p].astype(F32)
        g_ref[...] = g
        d_ref[...], m2_ref[...], v2_ref[...] = _adamw_step(g, w_ref[...], m_ref[...], v_ref[...])

    flat = pl.BlockSpec((rb, cols), lambda i: (i, 0))
    return pl.pallas_call(
        body, name=name, grid=(rows // rb,),
        in_specs=[pl.BlockSpec((N_DEV, rb, cols), lambda i: (0, i, 0)), flat, flat, flat],
        out_specs=[flat] * 4,
        out_shape=[jax.ShapeDtypeStruct((rows, cols), F32)] * 4,
        compiler_params=_params("parallel"),
    )(landing, w, m, v)


def _adamw_step(g, w, m, v):
    m2 = ADAM_B1 * m + (1.0 - ADAM_B1) * g
    v2 = ADAM_B2 * v + (1.0 - ADAM_B2) * (g * g)
    m_hat = m2 / (1.0 - ADAM_B1 ** ADAM_STEP)
    v_hat = v2 / (1.0 - ADAM_B2 ** ADAM_STEP)
    return -ADAM_LR * (m_hat / (jnp.sqrt(v_hat) + ADAM_EPS) + ADAM_WD * w), m2, v2


_REPLICATED = (
    ("mix_norm_g", D_MODEL), ("q_norm_g", Q_LORA), ("kv_norm_g", KV_LORA), ("conv_b", D_CONV), ("conv_ln_g", D_CONV),
    ("conv_ln_b", D_CONV), ("conv_out_g", D_CONV), ("attn_out_g", D_CONV), ("ffn_norm_g", D_MODEL),
    ("ffn_conv_b", D_UP), ("final_norm_g", D_MODEL),
)
_REPLICATED_WIDTH = sum(size for _, size in _REPLICATED) + _LANES

_WEIGHT_ORDER = (
    "meta_tokens", "mix_norm_g", "w_in", "q_norm_g", "w_uq", "kv_norm_g", "w_ukv", "conv_w", "conv_b", "conv_ln_g",
    "conv_ln_b", "conv_out_g", "attn_out_g", "w_out", "ffn_norm_g", "w_ffn_up", "ffn_conv_w", "ffn_conv_b",
    "w_ffn_down", "final_norm_g",
)


def _pack_replicated(grads, loss):
    rows = [grads[name].reshape(1, size) for name, size in _REPLICATED]
    return jnp.concatenate(rows + [jnp.broadcast_to(loss.reshape(1, 1), (1, _LANES))], axis=-1)


def _adamw_replicated(landing, weights, moments_m, moments_v):
    n = len(_REPLICATED)

    def body(*refs):
        l_ref, ins, outs = refs[0], refs[1:1 + 3 * n], refs[1 + 3 * n:]
        total = l_ref[0]
        for p in range(1, N_DEV):
            total = total + l_ref[p]
        at = 0
        for a, (_, size) in enumerate(_REPLICATED):
            g = total[:, at:at + size]
            w_ref, m_ref, v_ref = ins[3 * a:3 * a + 3]
            g_ref, d_ref, m2_ref, v2_ref = outs[4 * a:4 * a + 4]
            g_ref[...] = g
            d_ref[...], m2_ref[...], v2_ref[...] = _adamw_step(g, w_ref[...], m_ref[...], v_ref[...])
            at += size
        outs[-1][...] = total[:, at:at + _LANES]

    operands, out_shapes = [], []
    for name, size in _REPLICATED:
        operands += [weights[name].reshape(1, size), moments_m[name].reshape(1, size), moments_v[name].reshape(1, size)]
        out_shapes += [jax.ShapeDtypeStruct((1, size), F32)] * 4
    out_shapes.append(jax.ShapeDtypeStruct((1, _LANES), F32))
    outs = pl.pallas_call(body, name="adamw_replicated", out_shape=out_shapes)(landing, *operands)
    return outs[-1][0, 0], {name: outs[4 * a:4 * a + 4] for a, (name, _) in enumerate(_REPLICATED)}


def _pad_rows(a, rows):
    return jnp.pad(a, ((0, rows - a.shape[0]), (0, 0)))


def _slabs(a):
    r, c = a.shape
    return a.reshape(r, N_DEV, c // N_DEV).transpose(1, 0, 2)


def _unslab(a):
    g, r, c = a.shape
    return a.transpose(1, 0, 2).reshape(r, g * c)


def _local_step(x, target, w, n_rows, ffn_weights, send_grads):
    cos_t, sin_t = lax.optimization_barrier(_rope_tables(n_rows))
    cos, sin = cos_t.T, sin_t.T
    meta_pad, g1, gf = w["meta_pad"], w["mix_norm_g"], w["final_norm_g"]
    gq, gkv, gb_col = w["q_norm_g"], w["kv_norm_g"], w["attn_out_g"].reshape(D_ATTN, 1)
    nb, ag, cq, ckv, kr = _fwd_in(x, meta_pad, g1, w["w_in"], n_rows)
    mix_a, u1 = _fwd_conv(ag, w["conv_w"], w["conv_b"], w["conv_ln_g"], w["conv_ln_b"], w["conv_out_g"], n_rows)
    q_t, k, v, v_t, cqn, ckvn = _fwd_qkv(cq, ckv, kr, gq, gkv, w["wq_t"], w["w_ukv"], w["wv_t"], cos, sin, cos_t, sin_t, n_rows)
    o_t, lse = _attn_fwd(q_t, k, v_t, n_rows)
    w_out, w_up, w_down = ffn_weights()
    mix_bt, h1 = _fwd_out(x, meta_pad, mix_a, o_t, gb_col, w_out, n_rows)
    n2, up0, act, da, db, dh2, loss, dgf = _fwd_ffn(
        h1, target, w["ffn_norm_g"], w_up, w["fw"], w["fb"], w_down, gf, n_rows)

    dup, dfb = _bwd_ffn_act(dh2, da, db, w_down, n_rows)
    dup0, dh1, dfw, dg2 = _bwd_ffn_up(dup, up0, h1, dh2, w["ffn_norm_g"], w_up, w["fw"], n_rows)
    grad_w_out = jnp.concatenate([_weight_grad(mix_a, dh1, "grad_w_out_conv")[0],
                                  _weight_grad(mix_bt, dh1, "grad_w_out_attn", a_transposed=True)[0]], axis=0)
    stage0 = {
        "w_ffn_up": _weight_grad(dup0, n2, "grad_w_ffn_up"),
        "w_ffn_down": _weight_grad(act, dh2, "grad_w_ffn_down").reshape(N_DEV, D_FF // N_DEV, D_MODEL),
        "w_out": grad_w_out.reshape(N_DEV, D_MODEL // N_DEV, D_MODEL),
    }
    stage0, dh1 = lax.optimization_barrier((stage0, dh1))
    send_grads(0, stage0)
    do_t, delta, du1, dgb, dga, dlg, dlb, dcb = _bwd_out(
        dh1, o_t, u1, w_out, gb_col, w["conv_ln_g"], w["conv_ln_b"], w["conv_out_g"], n_rows)
    dq_t, dk, dv = _attn_bwd(q_t, k, v, do_t, lse, delta, n_rows)
    dqraw_t, dkv, dcq, dckv, dkr, dgq, dgkv = _bwd_qkv(
        dq_t, dk, dv, cq, ckv, gq, gkv, w["wq_t"], w["w_ukv"], cos, sin, cos_t, sin_t, n_rows)
    dz, dcw = _bwd_conv(du1, ag, w["conv_w"], dcq, dckv, dkr, n_rows)
    stage1 = {
        "w_in": _weight_grad(dz, nb, "grad_w_in")[0].reshape(N_DEV, D_IN // N_DEV, D_MODEL),
        "w_uq": _weight_grad(dqraw_t.reshape(N_HEADS * QK_DIM, n_rows), cqn, "grad_w_uq", a_transposed=True)[0].reshape(
            N_HEADS, QK_DIM, Q_LORA),
        "w_ukv": _slabs(_weight_grad(ckvn, dkv, "grad_w_ukv")[0]),
        "conv_w": _slabs(dcw),
        "ffn_conv_w": dfw[:, :, :UP_SLAB],
    }
    stage1, dz = lax.optimization_barrier((stage1, dz))
    send_grads(1, stage1)
    gx, gmeta, dg1 = _bwd_in(dz, x, meta_pad, dh1, g1, w["w_in"], n_rows)

    sharded = {"meta_tokens": _slabs(gmeta[DEAD:])}
    replicated = {
        "mix_norm_g": dg1, "q_norm_g": dgq, "kv_norm_g": dgkv, "conv_b": dcb, "conv_ln_g": dlg, "conv_ln_b": dlb,
        "conv_out_g": dga, "attn_out_g": dgb, "ffn_norm_g": dg2, "ffn_conv_b": dfb, "final_norm_g": dgf,
    }
    return loss[0, 0], gx, sharded, replicated


_SHARDED = (
    ("w_in", None, BF16), ("w_uq", None, BF16), ("w_ukv", None, BF16), ("w_out", None, BF16), ("w_ffn_up", None, BF16),
    ("w_ffn_down", None, BF16), ("conv_w", 32, F32), ("ffn_conv_w", 8, F32), ("meta_tokens", None, F32),
)
GATHER_LATE_ID = 3
EXCHANGE_STAGE_IDS = (4, 5)
_LATE_WEIGHTS = ("w_out", "w_ffn_up", "w_ffn_down")
_COLUMN_SHARDS = ("w_in", "w_uq", "w_ffn_up")


def kernel(x, meta_tokens, mix_norm_g, w_in, q_norm_g, w_uq, kv_norm_g, w_ukv, conv_w, conv_b, conv_ln_g, conv_ln_b, conv_out_g, attn_out_g, w_out, ffn_norm_g, w_ffn_up, ffn_conv_w, ffn_conv_b, w_ffn_down, final_norm_g, loss_target, m_meta_tokens, m_mix_norm_g, m_w_in, m_q_norm_g, m_w_uq, m_kv_norm_g, m_w_ukv, m_conv_w, m_conv_b, m_conv_ln_g, m_conv_ln_b, m_conv_out_g, m_attn_out_g, m_w_out, m_ffn_norm_g, m_w_ffn_up, m_ffn_conv_w, m_ffn_conv_b, m_w_ffn_down, m_final_norm_g, v_meta_tokens, v_mix_norm_g, v_w_in, v_q_norm_g, v_w_uq, v_kv_norm_g, v_w_ukv, v_conv_w, v_conv_b, v_conv_ln_g, v_conv_ln_b, v_conv_out_g, v_attn_out_g, v_w_out, v_ffn_norm_g, v_w_ffn_up, v_ffn_conv_w, v_ffn_conv_b, v_w_ffn_down, v_final_norm_g):
    given = dict(locals())
    weights = {name: given[name] for name in _WEIGHT_ORDER}
    moments_m = {name: given["m_" + name] for name in _WEIGHT_ORDER}
    moments_v = {name: given["v_" + name] for name in _WEIGHT_ORDER}
    seq = x.shape[1]
    n_rows = ROW_TILE + seq

    def shard2d(name, a):
        a = a.reshape(a.shape[-2], a.shape[-1])
        return a.T if name in _COLUMN_SHARDS else a

    early = [entry for entry in _SHARDED if entry[0] not in _LATE_WEIGHTS]
    shards = []
    for name, pad_to, _ in early:
        s = shard2d(name, weights[name])
        shards.append(s if pad_to is None else _pad_rows(s, pad_to))
    gathered = dict(zip([name for name, _, _ in early], _all_gather(shards, [dt for _, _, dt in early])))
    behind = gathered["meta_tokens"][0, 0, 0] * 0.0
    late_parts = [(shard2d(name, weights[name]) + behind).astype(BF16) for name in _LATE_WEIGHTS]
    late = _sequencer_exchange(late_parts, [True] * len(late_parts), "gather_late", GATHER_LATE_ID)
    meta_full = _unslab(gathered["meta_tokens"])
    full = {
        "meta_pad": jnp.concatenate([jnp.zeros((DEAD, D_MODEL), F32), meta_full], axis=0),
        "w_in": gathered["w_in"].reshape(D_IN, D_MODEL),
        "wq_t": gathered["w_uq"],
        "w_ukv": gathered["w_ukv"],
        "wv_t": gathered["w_ukv"][:, :, QK_NOPE:].transpose(0, 2, 1),
        "conv_w": _unslab(gathered["conv_w"][:, :CONV_WIDTH]),
        "fw": jnp.pad(gathered["ffn_conv_w"][:, :FFN_CONV_WIDTH], ((0, 0), (0, 0), (0, UP_PAD - UP_SLAB))),
        "fb": jnp.pad(ffn_conv_b.reshape(N_DEV, 1, UP_SLAB), ((0, 0), (0, 0), (0, UP_PAD - UP_SLAB))),
        "final_norm_g": final_norm_g.reshape(1, D_MODEL),
    }
    for name in ("mix_norm_g", "q_norm_g", "kv_norm_g", "conv_b", "conv_ln_g", "conv_ln_b", "conv_out_g", "attn_out_g",
                 "ffn_norm_g"):
        full[name] = weights[name]

    def ffn_weights():
        w_out_all, w_up_all, w_down_all = late
        return (w_out_all.reshape(D_MODEL, D_MODEL), w_up_all, w_down_all.reshape(N_ACT_SLAB, UP_SLAB, D_MODEL))

    wire = {name: (pad_to, dt) for name, pad_to, dt in _SHARDED}
    landing = {}

    def on_the_wire(name, slabs):
        pad_to, dt = wire[name]
        slabs = slabs.astype(dt)
        return slabs if pad_to is None else jnp.pad(slabs, ((0, 0), (0, pad_to - slabs.shape[1]), (0, 0)))

    def send_grads(stage, grads):
        parts = [on_the_wire(name, slabs) for name, slabs in grads.items()]
        if landing:
            arrived = list(landing)
            parts, held = lax.optimization_barrier((parts, [landing[name] for name in arrived]))
            landing.update(zip(arrived, held))
        landed = _sequencer_exchange(parts, [False] * len(parts), f"exchange_stage{stage}", EXCHANGE_STAGE_IDS[stage])
        landing.update(zip(grads, landed))

    loss, gx, sharded, replicated = _local_step(x[0], loss_target[0], full, n_rows, ffn_weights, send_grads)

    parts = [on_the_wire(name, slabs) for name, slabs in sharded.items()] + [_pack_replicated(replicated, loss)]
    landed = _exchange(parts, [False] * len(sharded) + [True])
    landing.update(zip(sharded, landed[:-1]))

    grad, delta, new_m, new_v = {}, {}, {}, {}
    for name, pad_to, _ in _SHARDED:
        land = landing[name]
        ws, ms, vs = (shard2d(name, a[name]) for a in (weights, moments_m, moments_v))
        rows = ws.shape[0]
        if pad_to is not None:
            ws, ms, vs = _pad_rows(ws, pad_to), _pad_rows(ms, pad_to), _pad_rows(vs, pad_to)
        outs = _adamw(land, ws, ms, vs, "adamw_" + name)
        shape = weights[name].shape
        grad[name], delta[name], new_m[name], new_v[name] = (
            (o.T if name in _COLUMN_SHARDS else o[:rows]).reshape(shape) for o in outs)
    loss, updates = _adamw_replicated(landed[-1], weights, moments_m, moments_v)
    for name, outs in updates.items():
        grad[name], delta[name], new_m[name], new_v[name] = (o.reshape(weights[name].shape) for o in outs)

    return (loss, gx[None], *[grad[n] for n in _WEIGHT_ORDER], *[delta[n] for n in _WEIGHT_ORDER],
            *[new_m[n] for n in _WEIGHT_ORDER], *[new_v[n] for n in _WEIGHT_ORDER])
```

```python
import jax
import jax.numpy as jnp
from jax import lax
from jax.experimental import pallas as pl
from jax.experimental.pallas import tpu as pltpu
from jax.experimental.pallas import tpu_sc as plsc

F32 = jnp.float32
BF16 = jnp.bfloat16

N_DEV = 8
D_MODEL = 1024
CHUNK = 64
CHUNK_SHIFT = 6
N_META = 16
D_CONV = 512
CONV_WIDTH = 31
N_HEADS = 8
QK_NOPE = 64
QK_ROPE = 32
QK_DIM = QK_NOPE + QK_ROPE
V_HEAD = 64
KV_HEAD = QK_NOPE + V_HEAD
D_ATTN = N_HEADS * V_HEAD
Q_LORA = 384
KV_LORA = 256
ROPE_THETA = 10000.0
D_IN = 2 * D_CONV + Q_LORA + KV_LORA + QK_ROPE
D_FF = 2816
D_UP = 2 * D_FF
FFN_CONV_WIDTH = 3
UP_SLAB = D_UP // N_DEV
N_ACT_SLAB = D_FF // UP_SLAB
EPS = 1e-6
NEG = -1e30
_LN2 = 0.6931471805599453
QK_LOGIT_SCALE = QK_DIM ** -0.5 / _LN2
ADAM_LR = 0.001
ADAM_B1 = 0.9
ADAM_B2 = 0.999
ADAM_EPS = 1e-08
ADAM_WD = 0.01
ADAM_STEP = 10

ROW_TILE = 256
DEAD = ROW_TILE - N_META
CONV_HALO = 32
FFN_HALO = 16
VMEM_LIMIT = 56 * 1024 * 1024
_LANES = 128

MESH = pl.DeviceIdType.MESH


def _dot(a, b):
    return jnp.dot(a, b, preferred_element_type=F32)


def _dot_nt(a, b):
    return lax.dot_general(a, b, (((1,), (1,)), ((), ())), preferred_element_type=F32)


def _dot_tn(a, b):
    return lax.dot_general(a, b, (((0,), (0,)), ((), ())), preferred_element_type=F32)


def _sigmoid(x):
    return 1.0 / (1.0 + jnp.exp2(x * (-1.0 / _LN2)))


def _rms_fwd(x, g):
    r = lax.rsqrt(jnp.mean(x * x, axis=-1, keepdims=True) + EPS)
    return x * r * g


def _rms_bwd(dy, x, g):
    r = lax.rsqrt(jnp.mean(x * x, axis=-1, keepdims=True) + EPS)
    w = dy * g
    dx = r * w - x * (r * r * r) * jnp.mean(w * x, axis=-1, keepdims=True)
    return dx, jnp.sum(dy * x * r, axis=0, keepdims=True)


def _rope(x, cos, sin):
    half = QK_ROPE // 2
    x1, x2 = x[:, :half], x[:, half:]
    return jnp.concatenate([x1 * cos - x2 * sin, x2 * cos + x1 * sin], axis=-1)


def _rope_t(dy, cos, sin):
    half = QK_ROPE // 2
    d1, d2 = dy[:, :half], dy[:, half:]
    return jnp.concatenate([d1 * cos + d2 * sin, d2 * cos - d1 * sin], axis=-1)


def _row_ids(i, rows):
    return i * rows + lax.broadcasted_iota(jnp.int32, (rows, 1), 0)


def _accumulate(ref, first, value):
    @pl.when(first)
    def _():
        ref[...] = value

    @pl.when(jnp.logical_not(first))
    def _():
        ref[...] += value


def _tile_spec(shape):
    nd = len(shape)
    if nd == 2:
        return pl.BlockSpec((ROW_TILE, shape[1]), lambda i: (i, 0))
    return pl.BlockSpec((shape[0], ROW_TILE, shape[2]), lambda i: (0, i, 0))


def _whole_spec(shape):
    nd = len(shape)
    return pl.BlockSpec(tuple(shape), lambda i: (0,) * nd, pipeline_mode=pl.Buffered(1))


def _acc_spec(shape):
    nd = len(shape)
    return pl.BlockSpec(tuple(shape), lambda i: (0,) * nd)


def _real_spec(width):
    return pl.BlockSpec((ROW_TILE, width), lambda i: (jnp.maximum(i - 1, 0), 0))


def _params(*semantics):
    return pltpu.CompilerParams(dimension_semantics=semantics, vmem_limit_bytes=VMEM_LIMIT)


def _fwd_in(x, meta_pad, g1, w_in, n_rows):
    nt = n_rows // ROW_TILE

    def body(x_ref, meta_ref, g_ref, w_ref, nb_ref, ag_ref, cq_ref, ckv_ref, kr_ref):
        i = pl.program_id(0)
        h0 = jnp.where(i == 0, meta_ref[...], x_ref[...])
        nb = _rms_fwd(h0, g_ref[...]).astype(BF16)
        nb_ref[...] = nb
        z = _dot_nt(nb, w_ref[...])
        ag_ref[...] = z[:, :2 * D_CONV]
        cq_ref[...] = z[:, 2 * D_CONV:2 * D_CONV + Q_LORA]
        ckv_ref[...] = z[:, 2 * D_CONV + Q_LORA:2 * D_CONV + Q_LORA + KV_LORA]
        kr_ref[...] = z[:, 2 * D_CONV + Q_LORA + KV_LORA:]

    out_shapes = [
        jax.ShapeDtypeStruct((n_rows, D_MODEL), BF16),
        jax.ShapeDtypeStruct((n_rows, 2 * D_CONV), F32),
        jax.ShapeDtypeStruct((n_rows, Q_LORA), F32),
        jax.ShapeDtypeStruct((n_rows, KV_LORA), F32),
        jax.ShapeDtypeStruct((n_rows, QK_ROPE), F32),
    ]
    return pl.pallas_call(
        body, name="fwd_in", grid=(nt,),
        in_specs=[_real_spec(D_MODEL), _whole_spec(meta_pad.shape), _whole_spec(g1.shape), _whole_spec(w_in.shape)],
        out_specs=[_tile_spec(s.shape) for s in out_shapes],
        out_shape=out_shapes,
        compiler_params=_params("parallel"),
    )(x, meta_pad, g1, w_in)


def _conv_chain(u1, ln_g, ln_b):
    mu = jnp.mean(u1, axis=-1, keepdims=True)
    xc = u1 - mu
    rstd = lax.rsqrt(jnp.mean(xc * xc, axis=-1, keepdims=True) + EPS)
    xh = xc * rstd
    u2 = xh * ln_g + ln_b
    return xh, u2, u2 * _sigmoid(u2), rstd


def _fwd_conv(ag, conv_w, conv_b, ln_g, ln_b, out_g, n_rows):
    nt = n_rows // ROW_TILE

    def body(ag_ref, w_ref, b_ref, lg_ref, lb_ref, og_ref, mix_ref, u1_ref, ext_ref, conv_ref):
        i = pl.program_id(0)

        @pl.when(i == 0)
        def _():
            ext_ref[:, 0:CONV_HALO, :] = jnp.zeros((CONV_PLANES, CONV_HALO, _LANES), F32)

        ag_t = ag_ref[...]
        live = _row_ids(i, ROW_TILE) >= DEAD
        u0 = jnp.where(live, ag_t[:, :D_CONV] * _sigmoid(ag_t[:, D_CONV:]), 0.0)
        _to_planes(ext_ref, (), slice(CONV_HALO, None), u0)
        first = CONV_HALO - (CONV_WIDTH - 1)
        for c in range(CONV_PLANES):
            taps = w_ref[:, c * _LANES:(c + 1) * _LANES]
            for p in range(PHASES):
                acc = jnp.zeros((PHASE_ROWS, _LANES), F32)
                for k in range(CONV_WIDTH):
                    acc = acc + taps[k:k + 1, :] * ext_ref[c, _phase(first + k + p), :]
                conv_ref[c, _phase(p), :] = acc
        ext_ref[:, 0:CONV_HALO, :] = ext_ref[:, ROW_TILE:ROW_TILE + CONV_HALO, :]
        u1 = _from_planes(conv_ref, (), D_CONV) + b_ref[...]
        u1_ref[...] = u1
        _, _, u3, _ = _conv_chain(u1, lg_ref[...], lb_ref[...])
        mix_ref[...] = _rms_fwd(u3, og_ref[...]).astype(BF16)

    out_shapes = [jax.ShapeDtypeStruct((n_rows, D_CONV), BF16), jax.ShapeDtypeStruct((n_rows, D_CONV), F32)]
    small = [conv_w, conv_b, ln_g, ln_b, out_g]
    return pl.pallas_call(
        body, name="fwd_conv", grid=(nt,),
        in_specs=[_tile_spec(ag.shape)] + [_whole_spec(a.shape) for a in small],
        out_specs=[_tile_spec(s.shape) for s in out_shapes],
        out_shape=out_shapes,
        scratch_shapes=[pltpu.VMEM((CONV_PLANES, ROW_TILE + CONV_HALO, _LANES), F32),
                        pltpu.VMEM((CONV_PLANES, ROW_TILE, _LANES), F32)],
        compiler_params=_params("arbitrary"),
    )(ag, *small)


def _lane_tile(shape):
    if len(shape) == 2:
        return pl.BlockSpec((shape[0], ROW_TILE), lambda i: (0, i))
    return pl.BlockSpec((shape[0], shape[1], ROW_TILE), lambda i: (0, 0, i))


def _rope_rows(x, cos, sin):
    half = QK_ROPE // 2
    x1, x2 = x[:half], x[half:]
    return jnp.concatenate([x1 * cos - x2 * sin, x2 * cos + x1 * sin], axis=0)


def _rope_rows_t(dy, cos, sin):
    half = QK_ROPE // 2
    d1, d2 = dy[:half], dy[half:]
    return jnp.concatenate([d1 * cos + d2 * sin, d2 * cos - d1 * sin], axis=0)


def _fwd_qkv(cq, ckv, kr, gq, gkv, wq_t, w_ukv, wv_t, cos, sin, cos_t, sin_t, n_rows):
    nt = n_rows // ROW_TILE

    def body(cq_ref, ckv_ref, kr_ref, gq_ref, gkv_ref, wqt_ref, wkv_ref, wvt_ref, cos_ref, sin_ref, cost_ref, sint_ref,
             qt_ref, k_ref, v_ref, vt_ref, cqn_ref, ckvn_ref):
        cqn = _rms_fwd(cq_ref[...], gq_ref[...]).astype(BF16)
        ckvn = _rms_fwd(ckv_ref[...], gkv_ref[...]).astype(BF16)
        cqn_ref[...] = cqn
        ckvn_ref[...] = ckvn
        k_rot = _rope(kr_ref[...], cos_ref[...], sin_ref[...])
        cos_rows, sin_rows = cost_ref[...], sint_ref[...]
        q_all = _dot_nt(wqt_ref[...].reshape(N_HEADS * QK_DIM, Q_LORA), cqn)
        vt_all = _dot_nt(wvt_ref[...].reshape(N_HEADS * V_HEAD, KV_LORA), ckvn).astype(BF16)
        for h in range(N_HEADS):
            q_raw = q_all[h * QK_DIM:(h + 1) * QK_DIM]
            q_h = jnp.concatenate([q_raw[:QK_NOPE], _rope_rows(q_raw[QK_NOPE:], cos_rows, sin_rows)], axis=0)
            qt_ref[h] = (q_h * QK_LOGIT_SCALE).astype(BF16)
            kv = _dot(ckvn, wkv_ref[h])
            k_ref[h] = jnp.concatenate([kv[:, :QK_NOPE], k_rot], axis=-1).astype(BF16)
            v_ref[h] = kv[:, QK_NOPE:].astype(BF16)
            vt_ref[h] = vt_all[h * V_HEAD:(h + 1) * V_HEAD]

    out_shapes = [
        jax.ShapeDtypeStruct((N_HEADS, QK_DIM, n_rows), BF16),
        jax.ShapeDtypeStruct((N_HEADS, n_rows, QK_DIM), BF16),
        jax.ShapeDtypeStruct((N_HEADS, n_rows, V_HEAD), BF16),
        jax.ShapeDtypeStruct((N_HEADS, V_HEAD, n_rows), BF16),
        jax.ShapeDtypeStruct((n_rows, Q_LORA), BF16),
        jax.ShapeDtypeStruct((n_rows, KV_LORA), BF16),
    ]
    tiles = [cq, ckv, kr]
    whole = [gq, gkv, wq_t, w_ukv, wv_t]
    out_specs = [_lane_tile(out_shapes[0].shape), _tile_spec(out_shapes[1].shape), _tile_spec(out_shapes[2].shape),
                 _lane_tile(out_shapes[3].shape), _tile_spec(out_shapes[4].shape), _tile_spec(out_shapes[5].shape)]
    return pl.pallas_call(
        body, name="fwd_qkv", grid=(nt,),
        in_specs=[_tile_spec(a.shape) for a in tiles] + [_whole_spec(a.shape) for a in whole]
        + [_tile_spec(cos.shape), _tile_spec(sin.shape), _lane_tile(cos_t.shape), _lane_tile(sin_t.shape)],
        out_specs=out_specs,
        out_shape=out_shapes,
        compiler_params=_params("parallel"),
    )(*tiles, *whole, cos, sin, cos_t, sin_t)


def _chunk_of(rows):
    return jnp.where(rows >= ROW_TILE, lax.shift_right_arithmetic(rows - ROW_TILE, CHUNK_SHIFT) + 1, 0)


def _visible(i, j):
    k_rows = j * ROW_TILE + lax.broadcasted_iota(jnp.int32, (ROW_TILE, 1), 0)
    q_rows = i * ROW_TILE + lax.broadcasted_iota(jnp.int32, (1, ROW_TILE), 1)
    return jnp.logical_and(_chunk_of(q_rows) >= _chunk_of(k_rows), k_rows >= DEAD)


def _attn_fwd(q_t, k, v_t, n_rows):
    nt = n_rows // ROW_TILE

    def body(qt_ref, k_ref, vt_ref, ot_ref, lse_ref, max_ref, sum_ref):
        i = pl.program_id(0)
        q_ts = [qt_ref[h] for h in range(N_HEADS)]

        def key_rows(j):
            return pl.ds(pl.multiple_of(j * ROW_TILE, ROW_TILE), ROW_TILE)

        def make_step(masked, tiles, first=0):
            def step(t, carry):
                js = [first + tiles * t + u for u in range(tiles)]
                scores = [[_dot(k_ref[h, key_rows(j), :], q_ts[h]) for h in range(N_HEADS)] for j in js]
                for j, tile_scores in zip(js, scores):
                    visible = _visible(i, j) if masked else None
                    probs, alphas = [], []
                    for h in range(N_HEADS):
                        m = max_ref[h]
                        s = jnp.where(visible, tile_scores[h], NEG) if masked else tile_scores[h]
                        m_new = jnp.maximum(m, jnp.max(s, axis=0, keepdims=True))
                        alpha = jnp.exp2(m - m_new)
                        p = jnp.exp2(s - m_new)
                        probs.append(p.astype(BF16))
                        alphas.append(alpha)
                        max_ref[h] = m_new
                        sum_ref[h] = alpha * sum_ref[h] + jnp.sum(p, axis=0, keepdims=True)
                    for h in range(N_HEADS):
                        ot_ref[h] = alphas[h] * ot_ref[h] + _dot(vt_ref[h, :, key_rows(j)], probs[h])
                return carry
            return step

        max_ref[...] = jnp.full(max_ref.shape, NEG, F32)
        sum_ref[...] = jnp.zeros_like(sum_ref)
        ot_ref[...] = jnp.zeros_like(ot_ref)
        between = jnp.maximum(i - 1, 0)
        quads = lax.shift_right_logical(between, 2)
        pairs = jnp.bitwise_and(lax.shift_right_logical(between, 1), 1)
        make_step(True, 1)(0, 0)
        lax.fori_loop(0, quads, make_step(False, 4, first=1), 0)
        lax.fori_loop(0, pairs, make_step(False, 2, first=1 + 4 * quads), 0)
        lax.fori_loop(1 + 4 * quads + 2 * pairs, i, make_step(False, 1), 0)
        lax.fori_loop(jnp.maximum(i, 1), i + 1, make_step(True, 1), 0)
        for h in range(N_HEADS):
            l = sum_ref[h]
            ot_ref[h] = ot_ref[h] / l
            lse_ref[h] = max_ref[h] + jnp.log2(l)

    out_shapes = [jax.ShapeDtypeStruct((N_HEADS, V_HEAD, n_rows), F32), jax.ShapeDtypeStruct((N_HEADS, 1, n_rows), F32)]
    return pl.pallas_call(
        body, name="attn_fwd", grid=(nt,),
        in_specs=[_lane_tile(q_t.shape), _whole_spec(k.shape), _whole_spec(v_t.shape)],
        out_specs=[_lane_tile(s.shape) for s in out_shapes],
        out_shape=out_shapes,
        scratch_shapes=[pltpu.VMEM((N_HEADS, 1, ROW_TILE), F32), pltpu.VMEM((N_HEADS, 1, ROW_TILE), F32)],
        compiler_params=_params("parallel"),
    )(q_t, k, v_t)


def _heads_to_rows(ref):
    return jnp.concatenate([ref[h] for h in range(N_HEADS)], axis=0)


def _rms_cols(x, g_col):
    r = lax.rsqrt(jnp.mean(x * x, axis=0, keepdims=True) + EPS)
    return x * r * g_col


def _fwd_out(x, meta_pad, mix_a, o_t, gb_col, w_out, n_rows):
    nt = n_rows // ROW_TILE

    def body(x_ref, meta_ref, mixa_ref, ot_ref, gb_ref, w_ref, mixbt_ref, h1_ref):
        i = pl.program_id(0)
        h0 = jnp.where(i == 0, meta_ref[...], x_ref[...])
        mix_bt = _rms_cols(_heads_to_rows(ot_ref), gb_ref[...]).astype(BF16)
        mixbt_ref[...] = mix_bt
        h1_ref[...] = h0 + _dot(mixa_ref[...], w_ref[:D_CONV, :]) + _dot_tn(mix_bt, w_ref[D_CONV:, :])

    out_shapes = [jax.ShapeDtypeStruct((D_ATTN, n_rows), BF16), jax.ShapeDtypeStruct((n_rows, D_MODEL), F32)]
    return pl.pallas_call(
        body, name="fwd_out", grid=(nt,),
        in_specs=[_real_spec(D_MODEL), _whole_spec(meta_pad.shape), _tile_spec(mix_a.shape), _lane_tile(o_t.shape),
                  _whole_spec(gb_col.shape), _whole_spec(w_out.shape)],
        out_specs=[_lane_tile(out_shapes[0].shape), _tile_spec(out_shapes[1].shape)],
        out_shape=out_shapes,
        compiler_params=_params("parallel"),
    )(x, meta_pad, mix_a, o_t, gb_col, w_out)


PHASES = 8
PHASE_ROWS = ROW_TILE // PHASES
UP_PLANES = -(-UP_SLAB // _LANES)
UP_PAD = UP_PLANES * _LANES
CONV_PLANES = D_CONV // _LANES


def _phase(start):
    return pl.ds(start, PHASE_ROWS, stride=PHASES)


def _to_planes(ref, lead, rows, value):
    width = value.shape[-1]
    for c in range(-(-width // _LANES)):
        part = value[:, c * _LANES:min((c + 1) * _LANES, width)]
        if part.shape[-1] < _LANES:
            part = jnp.concatenate([part, jnp.zeros((part.shape[0], _LANES - part.shape[-1]), part.dtype)], axis=-1)
        ref[(*lead, c, rows, slice(None))] = part


def _from_planes(ref, lead, width):
    planes = [ref[(*lead, c)] for c in range(-(-width // _LANES))]
    last = width - (len(planes) - 1) * _LANES
    return jnp.concatenate(planes[:-1] + [planes[-1][:, :last]], axis=-1)


def _fwd_ffn(h1, target, g2, w_up, fw, fb, w_down, gf, n_rows):
    nt = n_rows // ROW_TILE

    def body(h1_ref, t_ref, g2_ref, wup_ref, fw_ref, fb_ref, wdn_ref, gf_ref,
             n2_ref, up0_ref, act_ref, da_ref, db_ref, dh2_ref, loss_ref, dgf_ref, ext_ref):
        i = pl.program_id(0)

        @pl.when(i == 0)
        def _():
            ext_ref[:, 0:FFN_HALO, :] = jnp.zeros((N_DEV, FFN_HALO, UP_SLAB), F32)

        h1_t = h1_ref[...]
        live = _row_ids(i, ROW_TILE) >= DEAD
        n2 = jnp.where(live, _rms_fwd(h1_t, g2_ref[...]), 0.0).astype(BF16)
        n2_ref[...] = n2
        for s in range(N_DEV):
            up0 = _dot_nt(n2, wup_ref[s])
            up0_ref[s] = up0.astype(BF16)
            ext_ref[s, FFN_HALO:, :] = up0
        first = FFN_HALO - (FFN_CONV_WIDTH - 1)

        def conv(s):
            block = ext_ref[s]
            acc = fb_ref[s, :, :UP_SLAB] + fw_ref[s, FFN_CONV_WIDTH - 1:FFN_CONV_WIDTH, :UP_SLAB] * block[FFN_HALO:]
            for back in range(1, FFN_CONV_WIDTH):
                k = FFN_CONV_WIDTH - 1 - back
                acc = acc + fw_ref[s, k:k + 1, :UP_SLAB] * pltpu.roll(block, back, 0)[FFN_HALO:]
            return acc

        h2 = h1_t
        for s in range(N_ACT_SLAB):
            gate = conv(s)
            val = conv(s + N_ACT_SLAB)
            sg = _sigmoid(gate)
            silu = gate * sg
            act = (silu * val).astype(BF16)
            act_ref[s] = act
            da_ref[s] = (val * sg * (1.0 + gate * (1.0 - sg))).astype(BF16)
            db_ref[s] = silu.astype(BF16)
            h2 = h2 + _dot(act, wdn_ref[s])
        ext_ref[:, 0:FFN_HALO, :] = ext_ref[:, ROW_TILE:ROW_TILE + FFN_HALO, :]

        gf_t = gf_ref[...]
        y = _rms_fwd(h2, gf_t)
        diff = jnp.where(i >= 1, y - t_ref[...], 0.0)
        tile_loss = 0.5 * jnp.sum(jnp.sum(diff * diff, axis=-1, keepdims=True), axis=0, keepdims=True) / D_MODEL
        dh2, dgf = _rms_bwd(diff / D_MODEL, h2, gf_t)
        dh2_ref[...] = dh2
        _accumulate(loss_ref, i == 0, jnp.broadcast_to(tile_loss, loss_ref.shape))
        _accumulate(dgf_ref, i == 0, dgf)

    act_like = jax.ShapeDtypeStruct((N_ACT_SLAB, n_rows, UP_SLAB), BF16)
    out_shapes = [
        jax.ShapeDtypeStruct((n_rows, D_MODEL), BF16),
        jax.ShapeDtypeStruct((N_DEV, n_rows, UP_SLAB), BF16),
        act_like, act_like, act_like,
        jax.ShapeDtypeStruct((n_rows, D_MODEL), F32),
        jax.ShapeDtypeStruct((8, 128), F32),
        jax.ShapeDtypeStruct((1, D_MODEL), F32),
    ]
    whole = [g2, w_up, fw, fb, w_down, gf]
    return pl.pallas_call(
        body, name="fwd_ffn", grid=(nt,),
        in_specs=[_tile_spec(h1.shape), _real_spec(D_MODEL)] + [_whole_spec(a.shape) for a in whole],
        out_specs=[_tile_spec(s.shape) for s in out_shapes[:6]] + [_acc_spec(s.shape) for s in out_shapes[6:]],
        out_shape=out_shapes,
        scratch_shapes=[pltpu.VMEM((N_DEV, ROW_TILE + FFN_HALO, UP_SLAB), F32)],
        compiler_params=_params("arbitrary"),
    )(h1, target, *whole)


def _rope_tables(n_rows):
    pos = jnp.maximum(jnp.arange(n_rows, dtype=jnp.int32) - DEAD, 0)
    inv_freq = 1.0 / (ROPE_THETA ** (jnp.arange(0, QK_ROPE, 2, dtype=F32) / QK_ROPE))
    ang_t = inv_freq[:, None] * pos.astype(F32)[None, :]
    return jnp.cos(ang_t), jnp.sin(ang_t)


def _halo_after(shape, halo, n_rows):
    last = n_rows // halo - 1
    step = ROW_TILE // halo
    if len(shape) == 2:
        return pl.BlockSpec((halo, shape[1]), lambda i: (jnp.minimum((i + 1) * step, last), 0))
    return pl.BlockSpec((shape[0], halo, shape[2]), lambda i: (0, jnp.minimum((i + 1) * step, last), 0))


def _bwd_ffn_act(dh2, da, db, w_down, n_rows):
    nt = n_rows // ROW_TILE

    def body(dh2_ref, da_ref, db_ref, wdn_ref, dup_ref, dfb_ref):
        i = pl.program_id(0)

        @pl.when(i == 0)
        def _():
            dfb_ref[...] = jnp.zeros_like(dfb_ref)

        dh2_b = dh2_ref[...].astype(BF16)
        for s in range(N_ACT_SLAB):
            d_act = _dot_nt(dh2_b, wdn_ref[s])
            d_gate = d_act * da_ref[s].astype(F32)
            d_val = d_act * db_ref[s].astype(F32)
            dup_ref[s] = d_gate.astype(BF16)
            dup_ref[s + N_ACT_SLAB] = d_val.astype(BF16)
            dfb_ref[s] += jnp.sum(d_gate, axis=0, keepdims=True)
            dfb_ref[s + N_ACT_SLAB] += jnp.sum(d_val, axis=0, keepdims=True)

    out_shapes = [jax.ShapeDtypeStruct((N_DEV, n_rows, UP_SLAB), BF16), jax.ShapeDtypeStruct((N_DEV, 1, UP_SLAB), F32)]
    return pl.pallas_call(
        body, name="bwd_ffn_act", grid=(nt,),
        in_specs=[_tile_spec(dh2.shape), _tile_spec(da.shape), _tile_spec(db.shape), _whole_spec(w_down.shape)],
        out_specs=[_tile_spec(out_shapes[0].shape), _acc_spec(out_shapes[1].shape)],
        out_shape=out_shapes,
        compiler_params=_params("arbitrary"),
    )(dh2, da, db, w_down)


def _bwd_ffn_up(dup, up0, h1, dh2, g2, w_up, fw, n_rows):
    nt = n_rows // ROW_TILE
    last_tap = FFN_CONV_WIDTH - 1
    ext_rows = ROW_TILE + FFN_HALO

    def body(dup_ref, dnext_ref, up0_ref, h1_ref, dh2_ref, g2_ref, wup_ref, fw_ref,
             dup0_ref, dh1_ref, dfw_ref, dg2_ref):
        i = pl.program_id(0)

        @pl.when(i == 0)
        def _():
            dfw_ref[...] = jnp.zeros_like(dfw_ref)

        live = _row_ids(i, ROW_TILE) >= DEAD
        dn2 = jnp.zeros((ROW_TILE, D_MODEL), F32)
        for s in range(N_DEV):
            d = dup_ref[s].astype(F32)
            block = jnp.concatenate([d, jnp.where(i == nt - 1, 0.0, dnext_ref[s].astype(F32))], axis=0)
            u = up0_ref[s].astype(F32)
            dup0 = fw_ref[s, last_tap:last_tap + 1, :UP_SLAB] * d
            dfw_ref[s, last_tap:last_tap + 1, :UP_SLAB] += jnp.sum(d * u, axis=0, keepdims=True)
            for ahead in range(1, FFN_CONV_WIDTH):
                k = last_tap - ahead
                shifted = pltpu.roll(block, ext_rows - ahead, 0)[:ROW_TILE]
                dup0 = dup0 + fw_ref[s, k:k + 1, :UP_SLAB] * shifted
                dfw_ref[s, k:k + 1, :UP_SLAB] += jnp.sum(shifted * u, axis=0, keepdims=True)
            dup0_b = dup0.astype(BF16)
            dup0_ref[s] = dup0_b
            dn2 = dn2 + _dot(dup0_b, wup_ref[s])
        dn2 = jnp.where(live, dn2, 0.0)
        dx, dg2 = _rms_bwd(dn2, h1_ref[...], g2_ref[...])
        dh1_ref[...] = dh2_ref[...] + dx
        _accumulate(dg2_ref, i == 0, dg2)

    out_shapes = [
        jax.ShapeDtypeStruct((N_DEV, n_rows, UP_SLAB), BF16),
        jax.ShapeDtypeStruct((n_rows, D_MODEL), F32),
        jax.ShapeDtypeStruct((N_DEV, FFN_CONV_WIDTH, UP_PAD), F32),
        jax.ShapeDtypeStruct((1, D_MODEL), F32),
    ]
    return pl.pallas_call(
        body, name="bwd_ffn_up", grid=(nt,),
        in_specs=[_tile_spec(dup.shape), _halo_after(dup.shape, FFN_HALO, n_rows), _tile_spec(up0.shape),
                  _tile_spec(h1.shape), _tile_spec(dh2.shape),
                  _whole_spec(g2.shape), _whole_spec(w_up.shape), _whole_spec(fw.shape)],
        out_specs=[_tile_spec(s.shape) for s in out_shapes[:2]] + [_acc_spec(s.shape) for s in out_shapes[2:]],
        out_shape=out_shapes,
        compiler_params=_params("arbitrary"),
    )(dup, dup, up0, h1, dh2, g2, w_up, fw)


def _bwd_out(dh1, o_t, u1, w_out, gb_col, ln_g, ln_b, ga, n_rows):
    nt = n_rows // ROW_TILE

    def body(dh1_ref, ot_ref, u1_ref, w_ref, gb_ref, lg_ref, lb_ref, ga_ref,
             dot_ref, delta_ref, du1_ref, dgb_ref, dga_ref, dlg_ref, dlb_ref, dcb_ref):
        i = pl.program_id(0)
        dh1_b = dh1_ref[...].astype(BF16)
        o_t = _heads_to_rows(ot_ref)
        gb = gb_ref[...]
        r = lax.rsqrt(jnp.mean(o_t * o_t, axis=0, keepdims=True) + EPS)
        dmix_bt = _dot_nt(w_ref[D_CONV:, :], dh1_b)
        wgt = dmix_bt * gb
        do_t = r * wgt - o_t * (r * r * r) * jnp.mean(wgt * o_t, axis=0, keepdims=True)
        dgb = jnp.sum(dmix_bt * o_t * r, axis=1, keepdims=True)
        for h in range(N_HEADS):
            do_h = do_t[h * V_HEAD:(h + 1) * V_HEAD]
            dot_ref[h] = do_h.astype(BF16)
            delta_ref[h] = jnp.sum(do_h * ot_ref[h], axis=0, keepdims=True)
        lg = lg_ref[...]
        xh, u2, u3, rstd = _conv_chain(u1_ref[...], lg, lb_ref[...])
        du3, dga = _rms_bwd(_dot_nt(dh1_b, w_ref[:D_CONV, :]), u3, ga_ref[...])
        sg = _sigmoid(u2)
        du2 = du3 * sg * (1.0 + u2 * (1.0 - sg))
        dxh = du2 * lg
        du1 = rstd * (dxh - jnp.mean(dxh, axis=-1, keepdims=True) - xh * jnp.mean(dxh * xh, axis=-1, keepdims=True))
        du1_ref[...] = du1
        first = i == 0
        _accumulate(dgb_ref, first, dgb)
        _accumulate(dga_ref, first, dga)
        _accumulate(dlg_ref, first, jnp.sum(du2 * xh, axis=0, keepdims=True))
        _accumulate(dlb_ref, first, jnp.sum(du2, axis=0, keepdims=True))
        _accumulate(dcb_ref, first, jnp.sum(du1, axis=0, keepdims=True))

    out_shapes = [
        jax.ShapeDtypeStruct((N_HEADS, V_HEAD, n_rows), BF16),
        jax.ShapeDtypeStruct((N_HEADS, 1, n_rows), F32),
        jax.ShapeDtypeStruct((n_rows, D_CONV), F32),
        jax.ShapeDtypeStruct((D_ATTN, 1), F32),
    ] + [jax.ShapeDtypeStruct((1, D_CONV), F32)] * 4
    whole = [w_out, gb_col, ln_g, ln_b, ga]
    return pl.pallas_call(
        body, name="bwd_out", grid=(nt,),
        in_specs=[_tile_spec(dh1.shape), _lane_tile(o_t.shape), _tile_spec(u1.shape)] + [_whole_spec(a.shape) for a in whole],
        out_specs=[_lane_tile(out_shapes[0].shape), _lane_tile(out_shapes[1].shape), _tile_spec(out_shapes[2].shape)]
        + [_acc_spec(s.shape) for s in out_shapes[3:]],
        out_shape=out_shapes,
        compiler_params=_params("arbitrary"),
    )(dh1, o_t, u1, *whole)


ATTN_BWD_HEADS = 8


def _attn_bwd(q_t, k, v, do_t, lse, delta, n_rows):
    nt = n_rows // ROW_TILE
    hp = ATTN_BWD_HEADS

    def body(k_ref, v_ref, qt_ref, dot_ref, lse_ref, delta_ref, dqt_ref, dk_ref, dv_ref):
        j = pl.program_id(1)

        @pl.when(j == 0)
        def _():
            dqt_ref[...] = jnp.zeros_like(dqt_ref)

        k_ts = [k_ref[h] for h in range(hp)]
        v_ts = [v_ref[h] for h in range(hp)]

        def make_step(masked, tiles, first=0):
            def step(t, carry):
                tiles_of_step = []
                for u in range(tiles):
                    i = first + tiles * t + u
                    cols = pl.ds(pl.multiple_of(i * ROW_TILE, ROW_TILE), ROW_TILE)
                    q_is = [qt_ref[h, :, cols] for h in range(hp)]
                    do_is = [dot_ref[h, :, cols] for h in range(hp)]
                    scores = [_dot(k_ts[h], q_is[h]) for h in range(hp)]
                    dps = [_dot(v_ts[h], do_is[h]) for h in range(hp)]
                    tiles_of_step.append((i, cols, q_is, do_is, scores, dps))
                for i, cols, q_is, do_is, scores, dps in tiles_of_step:
                    visible = _visible(i, j) if masked else None
                    probs, dss = [], []
                    for h in range(hp):
                        s = jnp.where(visible, scores[h], NEG) if masked else scores[h]
                        p = jnp.exp2(s - lse_ref[h, :, cols])
                        probs.append(p.astype(BF16))
                        dss.append((p * (dps[h] - delta_ref[h, :, cols])).astype(BF16))
                    for h in range(hp):
                        dv_ref[h] += _dot_nt(probs[h], do_is[h])
                        dk_ref[h] += _dot_nt(dss[h], q_is[h])
                        dqt_ref[h, :, cols] += _dot_tn(k_ts[h], dss[h])
                return carry
            return step

        dk_ref[...] = jnp.zeros_like(dk_ref)
        dv_ref[...] = jnp.zeros_like(dv_ref)
        make_step(True, 1)(j, 0)
        lax.fori_loop(jnp.where(j == 0, j + 1, nt), nt, make_step(True, 1), 0)
        unmasked = jnp.where(j == 0, 0, nt - 1 - j)
        quads = lax.shift_right_logical(unmasked, 2)
        pairs = jnp.bitwise_and(lax.shift_right_logical(unmasked, 1), 1)
        lax.fori_loop(0, quads, make_step(False, 4, first=j + 1), 0)
        lax.fori_loop(0, pairs, make_step(False, 2, first=j + 1 + 4 * quads), 0)
        lax.fori_loop(jnp.where(j == 0, nt, j + 1 + 4 * quads + 2 * pairs), nt, make_step(False, 1), 0)
        dk_ref[...] = dk_ref[...] * _LN2

    key_tile = lambda w: pl.BlockSpec((hp, ROW_TILE, w), lambda g, j: (g, j, 0))
    all_cols = lambda w: pl.BlockSpec((hp, w, n_rows), lambda g, j: (g, 0, 0))
    resident = lambda w: pl.BlockSpec((hp, w, n_rows), lambda g, j: (g, 0, 0), pipeline_mode=pl.Buffered(1))
    out_shapes = [
        jax.ShapeDtypeStruct((N_HEADS, QK_DIM, n_rows), F32),
        jax.ShapeDtypeStruct((N_HEADS, n_rows, QK_DIM), F32),
        jax.ShapeDtypeStruct((N_HEADS, n_rows, V_HEAD), F32),
    ]
    return pl.pallas_call(
        body, name="attn_bwd", grid=(N_HEADS // hp, nt),
        in_specs=[key_tile(QK_DIM), key_tile(V_HEAD), resident(QK_DIM), resident(V_HEAD), resident(1), resident(1)],
        out_specs=[all_cols(QK_DIM), key_tile(QK_DIM), key_tile(V_HEAD)],
        out_shape=out_shapes,
        compiler_params=_params("parallel", "arbitrary"),
    )(k, v, q_t, do_t, lse, delta)


def _bwd_qkv(dq_t, dk, dv, cq, ckv, gq, gkv, wq_t, w_ukv, cos, sin, cos_t, sin_t, n_rows):
    nt = n_rows // ROW_TILE

    def body(dqt_ref, dk_ref, dv_ref, cq_ref, ckv_ref, gq_ref, gkv_ref, wqt_ref, wkv_ref, cos_ref, sin_ref,
             cost_ref, sint_ref, dqraw_ref, dkv_ref, dcq_ref, dckv_ref, dkr_ref, dgq_ref, dgkv_ref):
        i = pl.program_id(0)
        cos_rows, sin_rows = cost_ref[...], sint_ref[...]
        dcqn = jnp.zeros((ROW_TILE, Q_LORA), F32)
        dckvn = jnp.zeros((ROW_TILE, KV_LORA), F32)
        dk_rot = jnp.zeros((ROW_TILE, QK_ROPE), F32)
        for h in range(N_HEADS):
            dq_h, dk_h = dqt_ref[h] * QK_DIM ** -0.5, dk_ref[h]
            dq_raw = jnp.concatenate(
                [dq_h[:QK_NOPE], _rope_rows_t(dq_h[QK_NOPE:], cos_rows, sin_rows)], axis=0).astype(BF16)
            dqraw_ref[h] = dq_raw
            dcqn = dcqn + _dot_tn(dq_raw, wqt_ref[h])
            dkv = jnp.concatenate([dk_h[:, :QK_NOPE], dv_ref[h]], axis=-1).astype(BF16)
            dkv_ref[:, h * KV_HEAD:(h + 1) * KV_HEAD] = dkv
            dckvn = dckvn + _dot_nt(dkv, wkv_ref[h])
            dk_rot = dk_rot + dk_h[:, QK_NOPE:]
        dkr_ref[...] = _rope_t(dk_rot, cos_ref[...], sin_ref[...]).astype(BF16)
        dcq, dgq = _rms_bwd(dcqn, cq_ref[...], gq_ref[...])
        dckv, dgkv = _rms_bwd(dckvn, ckv_ref[...], gkv_ref[...])
        dcq_ref[...] = dcq.astype(BF16)
        dckv_ref[...] = dckv.astype(BF16)
        _accumulate(dgq_ref, i == 0, dgq)
        _accumulate(dgkv_ref, i == 0, dgkv)

    out_shapes = [
        jax.ShapeDtypeStruct((N_HEADS, QK_DIM, n_rows), BF16),
        jax.ShapeDtypeStruct((n_rows, N_HEADS * KV_HEAD), BF16),
        jax.ShapeDtypeStruct((n_rows, Q_LORA), BF16),
        jax.ShapeDtypeStruct((n_rows, KV_LORA), BF16),
        jax.ShapeDtypeStruct((n_rows, QK_ROPE), BF16),
        jax.ShapeDtypeStruct((1, Q_LORA), F32),
        jax.ShapeDtypeStruct((1, KV_LORA), F32),
    ]
    tiles = [dk, dv, cq, ckv]
    whole = [gq, gkv, wq_t, w_ukv]
    return pl.pallas_call(
        body, name="bwd_qkv", grid=(nt,),
        in_specs=[_lane_tile(dq_t.shape)] + [_tile_spec(a.shape) for a in tiles] + [_whole_spec(a.shape) for a in whole]
        + [_tile_spec(cos.shape), _tile_spec(sin.shape), _lane_tile(cos_t.shape), _lane_tile(sin_t.shape)],
        out_specs=[_lane_tile(out_shapes[0].shape)] + [_tile_spec(s.shape) for s in out_shapes[1:5]]
        + [_acc_spec(s.shape) for s in out_shapes[5:]],
        out_shape=out_shapes,
        compiler_params=_params("arbitrary"),
    )(dq_t, *tiles, *whole, cos, sin, cos_t, sin_t)


def _bwd_conv(du1, ag, conv_w, dcq, dckv, dkr, n_rows):
    nt = n_rows // ROW_TILE

    last_tap = CONV_WIDTH - 1

    def body(du1_ref, dnext_ref, ag_ref, w_ref, dcq_ref, dckv_ref, dkr_ref, dz_ref, dw_ref,
             dext_ref, uext_ref, conv_ref, sums_ref):
        i = pl.program_id(0)

        @pl.when(i == 0)
        def _():
            sums_ref[...] = jnp.zeros_like(sums_ref)

        _to_planes(dext_ref, (), slice(0, ROW_TILE), du1_ref[...])
        _to_planes(dext_ref, (), slice(ROW_TILE, None), jnp.where(i == nt - 1, 0.0, dnext_ref[...]))
        ag_t = ag_ref[...]
        live = _row_ids(i, ROW_TILE) >= DEAD
        sg = _sigmoid(ag_t[:, D_CONV:])
        _to_planes(uext_ref, (), slice(None), jnp.where(live, ag_t[:, :D_CONV] * sg, 0.0))
        for c in range(CONV_PLANES):
            taps = w_ref[:, c * _LANES:(c + 1) * _LANES]
            for half in range(0, PHASES, PHASES // 2):
                phases = range(half, half + PHASES // 2)
                us = {p: uext_ref[c, _phase(p), :] for p in phases}
                accs = {p: jnp.zeros((PHASE_ROWS, _LANES), F32) for p in phases}
                for k in range(CONV_WIDTH):
                    tap_sum = jnp.zeros((PHASE_ROWS, _LANES), F32)
                    for p in phases:
                        shifted = dext_ref[c, _phase(p + last_tap - k), :]
                        accs[p] = accs[p] + taps[k:k + 1, :] * shifted
                        tap_sum = tap_sum + shifted * us[p]
                    sums_ref[c, k] += tap_sum
                for p in phases:
                    conv_ref[c, _phase(p), :] = accs[p]
        du0 = jnp.where(live, _from_planes(conv_ref, (), D_CONV), 0.0)
        da = du0 * sg
        dgate = du0 * ag_t[:, :D_CONV] * sg * (1.0 - sg)
        dz_ref[...] = jnp.concatenate(
            [da.astype(BF16), dgate.astype(BF16), dcq_ref[...], dckv_ref[...], dkr_ref[...]], axis=-1)

        @pl.when(i == nt - 1)
        def _():
            for c in range(CONV_PLANES):
                for k in range(CONV_WIDTH):
                    dw_ref[k:k + 1, c * _LANES:(c + 1) * _LANES] = jnp.sum(sums_ref[c, k], axis=0, keepdims=True)

    out_shapes = [jax.ShapeDtypeStruct((n_rows, D_IN), BF16), jax.ShapeDtypeStruct((CONV_WIDTH, D_CONV), F32)]
    return pl.pallas_call(
        body, name="bwd_conv", grid=(nt,),
        in_specs=[_tile_spec(du1.shape), _halo_after(du1.shape, CONV_HALO, n_rows), _tile_spec(ag.shape),
                  _whole_spec(conv_w.shape), _tile_spec(dcq.shape), _tile_spec(dckv.shape), _tile_spec(dkr.shape)],
        out_specs=[_tile_spec(out_shapes[0].shape), _acc_spec(out_shapes[1].shape)],
        out_shape=out_shapes,
        scratch_shapes=[pltpu.VMEM((CONV_PLANES, ROW_TILE + CONV_HALO, _LANES), F32),
                        pltpu.VMEM((CONV_PLANES, ROW_TILE, _LANES), F32), pltpu.VMEM((CONV_PLANES, ROW_TILE, _LANES), F32),
                        pltpu.VMEM((CONV_PLANES, CONV_WIDTH, PHASE_ROWS, _LANES), F32)],
        compiler_params=_params("arbitrary"),
    )(du1, du1, ag, conv_w, dcq, dckv, dkr)


def _bwd_in(dz, x, meta_pad, dh1, g1, w_in, n_rows):
    nt = n_rows // ROW_TILE

    def body(dz_ref, x_ref, meta_ref, dh1_ref, g_ref, w_ref, gx_ref, gmeta_ref, dg1_ref):
        i = pl.program_id(0)
        h0 = jnp.where(i == 0, meta_ref[...], x_ref[...])
        dx, dg1 = _rms_bwd(_dot(dz_ref[...], w_ref[...]), h0, g_ref[...])
        dh0 = dh1_ref[...] + dx
        gx_ref[...] = dh0

        @pl.when(i == 0)
        def _():
            gmeta_ref[...] = dh0

        _accumulate(dg1_ref, i == 0, dg1)

    out_shapes = [
        jax.ShapeDtypeStruct((n_rows - ROW_TILE, D_MODEL), F32),
        jax.ShapeDtypeStruct((ROW_TILE, D_MODEL), F32),
        jax.ShapeDtypeStruct((1, D_MODEL), F32),
    ]
    return pl.pallas_call(
        body, name="bwd_in", grid=(nt,),
        in_specs=[_tile_spec(dz.shape), _real_spec(D_MODEL), _whole_spec(meta_pad.shape), _tile_spec(dh1.shape),
                  _whole_spec(g1.shape), _whole_spec(w_in.shape)],
        out_specs=[_real_spec(D_MODEL), _acc_spec(out_shapes[1].shape), _acc_spec(out_shapes[2].shape)],
        out_shape=out_shapes,
        compiler_params=_params("arbitrary"),
    )(dz, x, meta_pad, dh1, g1, w_in)


def _contraction_tile(n_rows):
    return next(t for t in range(n_rows // 2 // _LANES * _LANES, 0, -_LANES) if n_rows % t == 0)


def _weight_grad(a, b, name, a_transposed=False):
    groups = max(a.shape[0] if a.ndim == 3 else 1, b.shape[0] if b.ndim == 3 else 1)
    n_rows, n = b.shape[-2], b.shape[-1]
    m = a.shape[-2] if a_transposed else a.shape[-1]
    kt = _contraction_tile(n_rows)
    steps = n_rows // kt

    def body(a_ref, b_ref, out_ref, acc_ref):
        i = pl.program_id(1)
        a_t, b_t = a_ref[...].astype(BF16), b_ref[...].astype(BF16)
        part = _dot(a_t, b_t) if a_transposed else _dot_tn(a_t, b_t)
        _accumulate(acc_ref, i == 0, part)

        @pl.when(i == steps - 1)
        def _():
            out_ref[...] = acc_ref[...].astype(out_ref.dtype)

    def spec(arr, rows_last):
        block = (arr.shape[-2], kt) if rows_last else (kt, arr.shape[-1])
        at = (lambda i: (0, i)) if rows_last else (lambda i: (i, 0))
        if arr.ndim == 3:
            return pl.BlockSpec((None,) + block, lambda g, i: (g,) + at(i))
        return pl.BlockSpec(block, lambda g, i: at(i))

    return pl.pallas_call(
        body, name=name, grid=(groups, steps),
        in_specs=[spec(a, a_transposed), spec(b, False)],
        out_specs=pl.BlockSpec((None, m, n), lambda g, i: (g, 0, 0)),
        out_shape=jax.ShapeDtypeStruct((groups, m, n), BF16),
        scratch_shapes=[pltpu.VMEM((m, n), F32)],
        compiler_params=_params("parallel", "arbitrary"),
    )(a, b)


def _my_index():
    return 4 * lax.axis_index("x") + 2 * lax.axis_index("y") + lax.axis_index("c")


def _peer(k):
    flip = lambda v, bit: 1 - v if bit else v
    px = flip(lax.axis_index("x"), k & 4)
    py = flip(lax.axis_index("y"), k & 2)
    pc = flip(lax.axis_index("c"), k & 1)
    return (px, py, pc), 4 * px + 2 * py + pc


def _all_gather(shards, dtypes):
    n = len(shards)
    sibling, chips = 1, (2, 4, 6)

    def body(*refs):
        ins, outs, stages = refs[:n], refs[n:2 * n], refs[2 * n:3 * n]
        send_sems, recv_sems, local_sems = refs[3 * n:]
        me = _my_index()
        for a in range(n):
            stages[a][...] = ins[a][...].astype(stages[a].dtype)
        local = [pltpu.make_async_copy(stages[a], outs[a].at[me], local_sems.at[a]) for a in range(n)]
        for cp in local:
            cp.start()

        def copy(a, k, src, slot, to):
            return pltpu.make_async_remote_copy(
                src_ref=src, dst_ref=outs[a].at[slot], send_sem=send_sems.at[a, k - 1],
                recv_sem=recv_sems.at[a, k - 1], device_id=_peer(to)[0], device_id_type=MESH)

        def own(a, k):
            return copy(a, k, stages[a], me, k)

        def passed(a, k):
            slot = _peer(k)[1]
            return copy(a, k ^ sibling, outs[a].at[slot], slot, sibling)

        def arrival(a, k):
            return copy(a, k, stages[a], _peer(k)[1], k)

        for k in (sibling,) + chips:
            for a in range(n):
                own(a, k).start()
        for k in chips:
            for a in range(n):
                arrival(a, k).wait_recv()
                passed(a, k).start()
        for a in range(n):
            arrival(a, sibling).wait_recv()
            for k in chips:
                arrival(a, k ^ sibling).wait_recv()
        for a in range(n):
            for k in (sibling,) + chips:
                own(a, k).wait_send()
            for k in chips:
                passed(a, k).wait_send()
        for cp in local:
            cp.wait()

    return pl.pallas_call(
        body, name="gather_weights",
        in_specs=[pl.BlockSpec(memory_space=pltpu.VMEM)] * n,
        out_specs=[pl.BlockSpec(memory_space=pl.ANY)] * n,
        out_shape=[jax.ShapeDtypeStruct((N_DEV,) + s.shape, dt) for s, dt in zip(shards, dtypes)],
        scratch_shapes=[pltpu.VMEM(s.shape, dt) for s, dt in zip(shards, dtypes)]
        + [pltpu.SemaphoreType.DMA((n, N_DEV - 1)), pltpu.SemaphoreType.DMA((n, N_DEV - 1)), pltpu.SemaphoreType.DMA((n,))],
        compiler_params=pltpu.CompilerParams(vmem_limit_bytes=VMEM_LIMIT),
    )(*shards)


def _exchange(parts, whole):
    n = len(parts)

    def body(*refs):
        ins, outs = refs[:n], refs[n:2 * n]
        send_sems, recv_sems, local_sems = refs[2 * n:]
        me = _my_index()

        def src(a, slab):
            return ins[a] if whole[a] else ins[a].at[slab]

        local = [pltpu.make_async_copy(src(a, me), outs[a].at[me], local_sems.at[a]) for a in range(n)]
        for cp in local:
            cp.start()

        def copy(a, k, slab, slot):
            peer, _ = _peer(k)
            return pltpu.make_async_remote_copy(
                src_ref=src(a, slab), dst_ref=outs[a].at[slot], send_sem=send_sems.at[a, k - 1],
                recv_sem=recv_sems.at[a, k - 1], device_id=peer, device_id_type=MESH)

        for k in range(1, N_DEV):
            for a in range(n):
                copy(a, k, _peer(k)[1], me).start()
        for k in range(1, N_DEV):
            for a in range(n):
                copy(a, k, _peer(k)[1], _peer(k)[1]).wait()
        for cp in local:
            cp.wait()

    return pl.pallas_call(
        body, name="exchange_grads",
        in_specs=[pl.BlockSpec(memory_space=pl.ANY)] * n,
        out_specs=[pl.BlockSpec(memory_space=pl.ANY)] * n,
        out_shape=[jax.ShapeDtypeStruct(((N_DEV,) + p.shape) if w else p.shape, p.dtype) for p, w in zip(parts, whole)],
        scratch_shapes=[pltpu.SemaphoreType.DMA((n, N_DEV - 1)), pltpu.SemaphoreType.DMA((n, N_DEV - 1)),
                        pltpu.SemaphoreType.DMA((n,))],
    )(*parts)


def _sequencer_exchange(parts, whole, name, collective_id):
    n = len(parts)
    srcs = [jax.new_ref(p, memory_space=pltpu.MemorySpace.HBM) for p in parts]
    lands = [jax.empty_ref(jax.ShapeDtypeStruct(((N_DEV,) + p.shape) if w else p.shape, p.dtype),
                           memory_space=pltpu.MemorySpace.HBM) for p, w in zip(parts, whole)]

    @pl.kernel(mesh=plsc.ScalarSubcoreMesh(axis_name="sequencer", num_cores=1), name=name,
               scratch_types=(pltpu.SemaphoreType.DMA((n, N_DEV - 1)), pltpu.SemaphoreType.DMA((n, N_DEV - 1)),
                              pltpu.SemaphoreType.DMA((n,))),
               compiler_params=pltpu.CompilerParams(collective_id=collective_id))
    def launch(send_sems, recv_sems, local_sems):
        barrier = pltpu.get_barrier_semaphore()
        for k in range(1, N_DEV):
            pl.semaphore_signal(barrier, inc=1, device_id=_peer(k)[0], device_id_type=MESH)
        pl.semaphore_wait(barrier, N_DEV - 1)
        me = _my_index()

        def src(a, slab):
            return srcs[a] if whole[a] else srcs[a].at[slab]

        local = [pltpu.make_async_copy(src(a, me), lands[a].at[me], local_sems.at[a]) for a in range(n)]
        for cp in local:
            cp.start()

        def copy(a, k, slab, slot):
            return pltpu.make_async_remote_copy(
                src_ref=src(a, slab), dst_ref=lands[a].at[slot], send_sem=send_sems.at[a, k - 1],
                recv_sem=recv_sems.at[a, k - 1], device_id=_peer(k)[0], device_id_type=MESH)

        for k in range(1, N_DEV):
            for a in range(n):
                copy(a, k, _peer(k)[1], me).start()
        for k in range(1, N_DEV):
            for a in range(n):
                copy(a, k, _peer(k)[1], _peer(k)[1]).wait()
        for cp in local:
            cp.wait()

    launch()
    return [land[...] for land in lands]


def _row_block(rows):
    if rows <= ROW_TILE:
        return rows
    return next(rb for rb in range(ROW_TILE, 0, -16) if rows % rb == 0)


def _adamw(landing, w, m, v, name):
    rows, cols = w.shape
    rb = _row_block(rows)

    def body(l_ref, w_ref, m_ref, v_ref, g_ref, d_ref, m2_ref, v2_ref):
        g = l_ref[0].astype(F32)
        for p in range(1, N_DEV):
            g = g + l_ref[p].astype(F32)
        g_ref[...] = g
        d_ref[...], m2_ref[...], v2_ref[...] = _adamw_step(g, w_ref[...], m_ref[...], v_ref[...])

    flat = pl.BlockSpec((rb, cols), lambda i: (i, 0))
    return pl.pallas_call(
        body, name=name, grid=(rows // rb,),
        in_specs=[pl.BlockSpec((N_DEV, rb, cols), lambda i: (0, i, 0)), flat, flat, flat],
        out_specs=[flat] * 4,
        out_shape=[jax.ShapeDtypeStruct((rows, cols), F32)] * 4,
        compiler_params=_params("parallel"),
    )(landing, w, m, v)


def _adamw_step(g, w, m, v):
    m2 = ADAM_B1 * m + (1.0 - ADAM_B1) * g
    v2 = ADAM_B2 * v + (1.0 - ADAM_B2) * (g * g)
    m_hat = m2 / (1.0 - ADAM_B1 ** ADAM_STEP)
    v_hat = v2 / (1.0 - ADAM_B2 ** ADAM_STEP)
    return -ADAM_LR * (m_hat / (jnp.sqrt(v_hat) + ADAM_EPS) + ADAM_WD * w), m2, v2


_REPLICATED = (
    ("mix_norm_g", D_MODEL), ("q_norm_g", Q_LORA), ("kv_norm_g", KV_LORA), ("conv_b", D_CONV), ("conv_ln_g", D_CONV),
    ("conv_ln_b", D_CONV), ("conv_out_g", D_CONV), ("attn_out_g", D_CONV), ("ffn_norm_g", D_MODEL),
    ("ffn_conv_b", D_UP), ("final_norm_g", D_MODEL),
)
_REPLICATED_WIDTH = sum(size for _, size in _REPLICATED) + _LANES

_WEIGHT_ORDER = (
    "meta_tokens", "mix_norm_g", "w_in", "q_norm_g", "w_uq", "kv_norm_g", "w_ukv", "conv_w", "conv_b", "conv_ln_g",
    "conv_ln_b", "conv_out_g", "attn_out_g", "w_out", "ffn_norm_g", "w_ffn_up", "ffn_conv_w", "ffn_conv_b",
    "w_ffn_down", "final_norm_g",
)


def _pack_replicated(grads, loss):
    rows = [grads[name].reshape(1, size) for name, size in _REPLICATED]
    return jnp.concatenate(rows + [jnp.broadcast_to(loss.reshape(1, 1), (1, _LANES))], axis=-1)


def _adamw_replicated(landing, weights, moments_m, moments_v):
    n = len(_REPLICATED)

    def body(*refs):
        l_ref, ins, outs = refs[0], refs[1:1 + 3 * n], refs[1 + 3 * n:]
        total = l_ref[0]
        for p in range(1, N_DEV):
            total = total + l_ref[p]
        at = 0
        for a, (_, size) in enumerate(_REPLICATED):
            g = total[:, at:at + size]
            w_ref, m_ref, v_ref = ins[3 * a:3 * a + 3]
            g_ref, d_ref, m2_ref, v2_ref = outs[4 * a:4 * a + 4]
            g_ref[...] = g
            d_ref[...], m2_ref[...], v2_ref[...] = _adamw_step(g, w_ref[...], m_ref[...], v_ref[...])
            at += size
        outs[-1][...] = total[:, at:at + _LANES]

    operands, out_shapes = [], []
    for name, size in _REPLICATED:
        operands += [weights[name].reshape(1, size), moments_m[name].reshape(1, size), moments_v[name].reshape(1, size)]
        out_shapes += [jax.ShapeDtypeStruct((1, size), F32)] * 4
    out_shapes.append(jax.ShapeDtypeStruct((1, _LANES), F32))
    outs = pl.pallas_call(body, name="adamw_replicated", out_shape=out_shapes)(landing, *operands)
    return outs[-1][0, 0], {name: outs[4 * a:4 * a + 4] for a, (name, _) in enumerate(_REPLICATED)}


def _pad_rows(a, rows):
    return jnp.pad(a, ((0, rows - a.shape[0]), (0, 0)))


def _slabs(a):
    r, c = a.shape
    return a.reshape(r, N_DEV, c // N_DEV).transpose(1, 0, 2)


def _unslab(a):
    g, r, c = a.shape
    return a.transpose(1, 0, 2).reshape(r, g * c)


def _local_step(x, target, w, n_rows, ffn_weights, send_grads):
    cos_t, sin_t = lax.optimization_barrier(_rope_tables(n_rows))
    cos, sin = cos_t.T, sin_t.T
    meta_pad, g1, gf = w["meta_pad"], w["mix_norm_g"], w["final_norm_g"]
    gq, gkv, gb_col = w["q_norm_g"], w["kv_norm_g"], w["attn_out_g"].reshape(D_ATTN, 1)
    nb, ag, cq, ckv, kr = _fwd_in(x, meta_pad, g1, w["w_in"], n_rows)
    mix_a, u1 = _fwd_conv(ag, w["conv_w"], w["conv_b"], w["conv_ln_g"], w["conv_ln_b"], w["conv_out_g"], n_rows)
    q_t, k, v, v_t, cqn, ckvn = _fwd_qkv(cq, ckv, kr, gq, gkv, w["wq_t"], w["w_ukv"], w["wv_t"], cos, sin, cos_t, sin_t, n_rows)
    o_t, lse = _attn_fwd(q_t, k, v_t, n_rows)
    w_out, w_up, w_down = ffn_weights()
    mix_bt, h1 = _fwd_out(x, meta_pad, mix_a, o_t, gb_col, w_out, n_rows)
    n2, up0, act, da, db, dh2, loss, dgf = _fwd_ffn(
        h1, target, w["ffn_norm_g"], w_up, w["fw"], w["fb"], w_down, gf, n_rows)

    dup, dfb = _bwd_ffn_act(dh2, da, db, w_down, n_rows)
    dup0, dh1, dfw, dg2 = _bwd_ffn_up(dup, up0, h1, dh2, w["ffn_norm_g"], w_up, w["fw"], n_rows)
    grad_w_out = jnp.concatenate([_weight_grad(mix_a, dh1, "grad_w_out_conv")[0],
                                  _weight_grad(mix_bt, dh1, "grad_w_out_attn", a_transposed=True)[0]], axis=0)
    stage0 = {
        "w_ffn_up": _weight_grad(dup0, n2, "grad_w_ffn_up"),
        "w_ffn_down": _weight_grad(act, dh2, "grad_w_ffn_down").reshape(N_DEV, D_FF // N_DEV, D_MODEL),
        "w_out": grad_w_out.reshape(N_DEV, D_MODEL // N_DEV, D_MODEL),
    }
    stage0, dh1 = lax.optimization_barrier((stage0, dh1))
    send_grads(0, stage0)
    do_t, delta, du1, dgb, dga, dlg, dlb, dcb = _bwd_out(
        dh1, o_t, u1, w_out, gb_col, w["conv_ln_g"], w["conv_ln_b"], w["conv_out_g"], n_rows)
    dq_t, dk, dv = _attn_bwd(q_t, k, v, do_t, lse, delta, n_rows)
    dqraw_t, dkv, dcq, dckv, dkr, dgq, dgkv = _bwd_qkv(
        dq_t, dk, dv, cq, ckv, gq, gkv, w["wq_t"], w["w_ukv"], cos, sin, cos_t, sin_t, n_rows)
    dz, dcw = _bwd_conv(du1, ag, w["conv_w"], dcq, dckv, dkr, n_rows)
    stage1 = {
        "w_in": _weight_grad(dz, nb, "grad_w_in")[0].reshape(N_DEV, D_IN // N_DEV, D_MODEL),
        "w_uq": _weight_grad(dqraw_t.reshape(N_HEADS * QK_DIM, n_rows), cqn, "grad_w_uq", a_transposed=True)[0].reshape(
            N_HEADS, QK_DIM, Q_LORA),
        "w_ukv": _slabs(_weight_grad(ckvn, dkv, "grad_w_ukv")[0]),
        "conv_w": _slabs(dcw),
        "ffn_conv_w": dfw[:, :, :UP_SLAB],
    }
    stage1, dz = lax.optimization_barrier((stage1, dz))
    send_grads(1, stage1)
    gx, gmeta, dg1 = _bwd_in(dz, x, meta_pad, dh1, g1, w["w_in"], n_rows)

    sharded = {"meta_tokens": _slabs(gmeta[DEAD:])}
    replicated = {
        "mix_norm_g": dg1, "q_norm_g": dgq, "kv_norm_g": dgkv, "conv_b": dcb, "conv_ln_g": dlg, "conv_ln_b": dlb,
        "conv_out_g": dga, "attn_out_g": dgb, "ffn_norm_g": dg2, "ffn_conv_b": dfb, "final_norm_g": dgf,
    }
    return loss[0, 0], gx, sharded, replicated


_SHARDED = (
    ("w_in", None, BF16), ("w_uq", None, BF16), ("w_ukv", None, BF16), ("w_out", None, BF16), ("w_ffn_up", None, BF16),
    ("w_ffn_down", None, BF16), ("conv_w", 32, F32), ("ffn_conv_w", 8, F32), ("meta_tokens", None, F32),
)
GATHER_LATE_ID = 3
EXCHANGE_STAGE_IDS = (4, 5)
_LATE_WEIGHTS = ("w_out", "w_ffn_up", "w_ffn_down")
_COLUMN_SHARDS = ("w_in", "w_uq", "w_ffn_up")


def kernel(x, meta_tokens, mix_norm_g, w_in, q_norm_g, w_uq, kv_norm_g, w_ukv, conv_w, conv_b, conv_ln_g, conv_ln_b, conv_out_g, attn_out_g, w_out, ffn_norm_g, w_ffn_up, ffn_conv_w, ffn_conv_b, w_ffn_down, final_norm_g, loss_target, m_meta_tokens, m_mix_norm_g, m_w_in, m_q_norm_g, m_w_uq, m_kv_norm_g, m_w_ukv, m_conv_w, m_conv_b, m_conv_ln_g, m_conv_ln_b, m_conv_out_g, m_attn_out_g, m_w_out, m_ffn_norm_g, m_w_ffn_up, m_ffn_conv_w, m_ffn_conv_b, m_w_ffn_down, m_final_norm_g, v_meta_tokens, v_mix_norm_g, v_w_in, v_q_norm_g, v_w_uq, v_kv_norm_g, v_w_ukv, v_conv_w, v_conv_b, v_conv_ln_g, v_conv_ln_b, v_conv_out_g, v_attn_out_g, v_w_out, v_ffn_norm_g, v_w_ffn_up, v_ffn_conv_w, v_ffn_conv_b, v_w_ffn_down, v_final_norm_g):
    given = dict(locals())
    weights = {name: given[name] for name in _WEIGHT_ORDER}
    moments_m = {name: given["m_" + name] for name in _WEIGHT_ORDER}
    moments_v = {name: given["v_" + name] for name in _WEIGHT_ORDER}
    seq = x.shape[1]
    n_rows = ROW_TILE + seq

    def shard2d(name, a):
        a = a.reshape(a.shape[-2], a.shape[-1])
        return a.T if name in _COLUMN_SHARDS else a

    early = [entry for entry in _SHARDED if entry[0] not in _LATE_WEIGHTS]
    shards = []
    for name, pad_to, _ in early:
        s = shard2d(name, weights[name])
        shards.append(s if pad_to is None else _pad_rows(s, pad_to))
    gathered = dict(zip([name for name, _, _ in early], _all_gather(shards, [dt for _, _, dt in early])))
    late_shards, gathered["meta_tokens"] = lax.optimization_barrier(
        ([shard2d(name, weights[name]) for name in _LATE_WEIGHTS], gathered["meta_tokens"]))
    late_parts = [s.astype(BF16) for s in late_shards]
    late = _sequencer_exchange(late_parts, [True] * len(late_parts), "gather_late", GATHER_LATE_ID)
    meta_full = _unslab(gathered["meta_tokens"])
    full = {
        "meta_pad": jnp.concatenate([jnp.zeros((DEAD, D_MODEL), F32), meta_full], axis=0),
        "w_in": gathered["w_in"].reshape(D_IN, D_MODEL),
        "wq_t": gathered["w_uq"],
        "w_ukv": gathered["w_ukv"],
        "wv_t": gathered["w_ukv"][:, :, QK_NOPE:].transpose(0, 2, 1),
        "conv_w": _unslab(gathered["conv_w"][:, :CONV_WIDTH]),
        "fw": jnp.pad(gathered["ffn_conv_w"][:, :FFN_CONV_WIDTH], ((0, 0), (0, 0), (0, UP_PAD - UP_SLAB))),
        "fb": jnp.pad(ffn_conv_b.reshape(N_DEV, 1, UP_SLAB), ((0, 0), (0, 0), (0, UP_PAD - UP_SLAB))),
        "final_norm_g": final_norm_g.reshape(1, D_MODEL),
    }
    for name in ("mix_norm_g", "q_norm_g", "kv_norm_g", "conv_b", "conv_ln_g", "conv_ln_b", "conv_out_g", "attn_out_g",
                 "ffn_norm_g"):
        full[name] = weights[name]

    def ffn_weights():
        w_out_all, w_up_all, w_down_all = late
        return (w_out_all.reshape(D_MODEL, D_MODEL), w_up_all, w_down_all.reshape(N_ACT_SLAB, UP_SLAB, D_MODEL))

    wire = {name: (pad_to, dt) for name, pad_to, dt in _SHARDED}
    landing = {}

    def on_the_wire(name, slabs):
        pad_to, dt = wire[name]
        slabs = slabs.astype(dt)
        return slabs if pad_to is None else jnp.pad(slabs, ((0, 0), (0, pad_to - slabs.shape[1]), (0, 0)))

    def send_grads(stage, grads):
        parts = [on_the_wire(name, slabs) for name, slabs in grads.items()]
        if landing:
            arrived = list(landing)
            parts, held = lax.optimization_barrier((parts, [landing[name] for name in arrived]))
            landing.update(zip(arrived, held))
        landed = _sequencer_exchange(parts, [False] * len(parts), f"exchange_stage{stage}", EXCHANGE_STAGE_IDS[stage])
        landing.update(zip(grads, landed))

    loss, gx, sharded, replicated = _local_step(x[0], loss_target[0], full, n_rows, ffn_weights, send_grads)

    parts = [on_the_wire(name, slabs) for name, slabs in sharded.items()] + [_pack_replicated(replicated, loss)]
    landed = _exchange(parts, [False] * len(sharded) + [True])
    landing.update(zip(sharded, landed[:-1]))

    grad, delta, new_m, new_v = {}, {}, {}, {}
    for name, pad_to, _ in _SHARDED:
        land = landing[name]
        ws, ms, vs = (shard2d(name, a[name]) for a in (weights, moments_m, moments_v))
        rows = ws.shape[0]
        if pad_to is not None:
            ws, ms, vs = _pad_rows(ws, pad_to), _pad_rows(ms, pad_to), _pad_rows(vs, pad_to)
        outs = _adamw(land, ws, ms, vs, "adamw_" + name)
        shape = weights[name].shape
        grad[name], delta[name], new_m[name], new_v[name] = (
            (o.T if name in _COLUMN_SHARDS else o[:rows]).reshape(shape) for o in outs)
    loss, updates = _adamw_replicated(landed[-1], weights, moments_m, moments_v)
    for name, outs in updates.items():
        grad[name], delta[name], new_m[name], new_v[name] = (o.reshape(weights[name].shape) for o in outs)

    return (loss, gx[None], *[grad[n] for n in _WEIGHT_ORDER], *[delta[n] for n in _WEIGHT_ORDER],
            *[new_m[n] for n in _WEIGHT_ORDER], *[new_v[n] for n in _WEIGHT_ORDER])
```

```python
import jax
import jax.numpy as jnp
from jax import lax
from jax.experimental import pallas as pl
from jax.experimental.pallas import tpu as pltpu
from jax.experimental.pallas import tpu_sc as plsc

F32 = jnp.float32
BF16 = jnp.bfloat16

N_DEV = 8
D_MODEL = 1024
CHUNK = 64
CHUNK_SHIFT = 6
N_META = 16
D_CONV = 512
CONV_WIDTH = 31
N_HEADS = 8
QK_NOPE = 64
QK_ROPE = 32
QK_DIM = QK_NOPE + QK_ROPE
V_HEAD = 64
KV_HEAD = QK_NOPE + V_HEAD
D_ATTN = N_HEADS * V_HEAD
Q_LORA = 384
KV_LORA = 256
ROPE_THETA = 10000.0
D_IN = 2 * D_CONV + Q_LORA + KV_LORA + QK_ROPE
D_FF = 2816
D_UP = 2 * D_FF
FFN_CONV_WIDTH = 3
UP_SLAB = D_UP // N_DEV
N_ACT_SLAB = D_FF // UP_SLAB
EPS = 1e-6
NEG = -1e30
_LN2 = 0.6931471805599453
QK_LOGIT_SCALE = QK_DIM ** -0.5 / _LN2
ADAM_LR = 0.001
ADAM_B1 = 0.9
ADAM_B2 = 0.999
ADAM_EPS = 1e-08
ADAM_WD = 0.01
ADAM_STEP = 10

ROW_TILE = 256
DEAD = ROW_TILE - N_META
CONV_HALO = 32
FFN_HALO = 16
assert FFN_HALO == N_META
VMEM_LIMIT = 56 * 1024 * 1024
_LANES = 128

MESH = pl.DeviceIdType.MESH


def _dot(a, b):
    return jnp.dot(a, b, preferred_element_type=F32)


def _dot_nt(a, b):
    return lax.dot_general(a, b, (((1,), (1,)), ((), ())), preferred_element_type=F32)


def _dot_tn(a, b):
    return lax.dot_general(a, b, (((0,), (0,)), ((), ())), preferred_element_type=F32)


def _sigmoid(x):
    return 1.0 / (1.0 + jnp.exp2(x * (-1.0 / _LN2)))


def _rms_fwd(x, g):
    r = lax.rsqrt(jnp.mean(x * x, axis=-1, keepdims=True) + EPS)
    return x * r * g


def _rms_bwd(dy, x, g):
    r = lax.rsqrt(jnp.mean(x * x, axis=-1, keepdims=True) + EPS)
    w = dy * g
    dx = r * w - x * (r * r * r) * jnp.mean(w * x, axis=-1, keepdims=True)
    return dx, jnp.sum(dy * x * r, axis=0, keepdims=True)


def _rope(x, cos, sin):
    half = QK_ROPE // 2
    x1, x2 = x[:, :half], x[:, half:]
    return jnp.concatenate([x1 * cos - x2 * sin, x2 * cos + x1 * sin], axis=-1)


def _rope_t(dy, cos, sin):
    half = QK_ROPE // 2
    d1, d2 = dy[:, :half], dy[:, half:]
    return jnp.concatenate([d1 * cos + d2 * sin, d2 * cos - d1 * sin], axis=-1)


def _row_ids(i, rows):
    return i * rows + lax.broadcasted_iota(jnp.int32, (rows, 1), 0)


def _accumulate(ref, first, value):
    @pl.when(first)
    def _():
        ref[...] = value

    @pl.when(jnp.logical_not(first))
    def _():
        ref[...] += value


def _tile_spec(shape):
    nd = len(shape)
    if nd == 2:
        return pl.BlockSpec((ROW_TILE, shape[1]), lambda i: (i, 0))
    return pl.BlockSpec((shape[0], ROW_TILE, shape[2]), lambda i: (0, i, 0))


def _whole_spec(shape):
    nd = len(shape)
    return pl.BlockSpec(tuple(shape), lambda i: (0,) * nd, pipeline_mode=pl.Buffered(1))


def _acc_spec(shape):
    nd = len(shape)
    return pl.BlockSpec(tuple(shape), lambda i: (0,) * nd)


def _real_spec(width):
    return pl.BlockSpec((ROW_TILE, width), lambda i: (jnp.maximum(i - 1, 0), 0))


def _params(*semantics):
    return pltpu.CompilerParams(dimension_semantics=semantics, vmem_limit_bytes=VMEM_LIMIT)


def _fwd_in(x, meta_pad, g1, w_in, n_rows):
    nt = n_rows // ROW_TILE

    def body(x_ref, meta_ref, g_ref, w_ref, nb_ref, ag_ref, cq_ref, ckv_ref, kr_ref):
        i = pl.program_id(0)
        h0 = jnp.where(i == 0, meta_ref[...], x_ref[...])
        nb = _rms_fwd(h0, g_ref[...]).astype(BF16)
        nb_ref[...] = nb
        z = _dot_nt(nb, w_ref[...])
        ag_ref[...] = z[:, :2 * D_CONV]
        cq_ref[...] = z[:, 2 * D_CONV:2 * D_CONV + Q_LORA]
        ckv_ref[...] = z[:, 2 * D_CONV + Q_LORA:2 * D_CONV + Q_LORA + KV_LORA]
        kr_ref[...] = z[:, 2 * D_CONV + Q_LORA + KV_LORA:]

    out_shapes = [
        jax.ShapeDtypeStruct((n_rows, D_MODEL), BF16),
        jax.ShapeDtypeStruct((n_rows, 2 * D_CONV), F32),
        jax.ShapeDtypeStruct((n_rows, Q_LORA), F32),
        jax.ShapeDtypeStruct((n_rows, KV_LORA), F32),
        jax.ShapeDtypeStruct((n_rows, QK_ROPE), F32),
    ]
    return pl.pallas_call(
        body, name="fwd_in", grid=(nt,),
        in_specs=[_real_spec(D_MODEL), _whole_spec(meta_pad.shape), _whole_spec(g1.shape), _whole_spec(w_in.shape)],
        out_specs=[_tile_spec(s.shape) for s in out_shapes],
        out_shape=out_shapes,
        compiler_params=_params("parallel"),
    )(x, meta_pad, g1, w_in)


def _conv_chain(u1, ln_g, ln_b):
    mu = jnp.mean(u1, axis=-1, keepdims=True)
    xc = u1 - mu
    rstd = lax.rsqrt(jnp.mean(xc * xc, axis=-1, keepdims=True) + EPS)
    xh = xc * rstd
    u2 = xh * ln_g + ln_b
    return xh, u2, u2 * _sigmoid(u2), rstd


def _fwd_conv(ag, conv_w, conv_b, ln_g, ln_b, out_g, n_rows):
    nt = n_rows // ROW_TILE

    def body(ag_ref, w_ref, b_ref, lg_ref, lb_ref, og_ref, mix_ref, u1_ref, ext_ref, conv_ref):
        i = pl.program_id(0)

        @pl.when(i == 0)
        def _():
            ext_ref[:, 0:CONV_HALO, :] = jnp.zeros((CONV_PLANES, CONV_HALO, _LANES), F32)

        ag_t = ag_ref[...]
        live = _row_ids(i, ROW_TILE) >= DEAD
        u0 = jnp.where(live, ag_t[:, :D_CONV] * _sigmoid(ag_t[:, D_CONV:]), 0.0)
        _to_planes(ext_ref, (), slice(CONV_HALO, None), u0)
        first = CONV_HALO - (CONV_WIDTH - 1)
        for c in range(CONV_PLANES):
            taps = w_ref[:, c * _LANES:(c + 1) * _LANES]
            for p in range(PHASES):
                acc = jnp.zeros((PHASE_ROWS, _LANES), F32)
                for k in range(CONV_WIDTH):
                    acc = acc + taps[k:k + 1, :] * ext_ref[c, _phase(first + k + p), :]
                conv_ref[c, _phase(p), :] = acc
        ext_ref[:, 0:CONV_HALO, :] = ext_ref[:, ROW_TILE:ROW_TILE + CONV_HALO, :]
        u1 = _from_planes(conv_ref, (), D_CONV) + b_ref[...]
        u1_ref[...] = u1
        _, _, u3, _ = _conv_chain(u1, lg_ref[...], lb_ref[...])
        mix_ref[...] = _rms_fwd(u3, og_ref[...]).astype(BF16)

    out_shapes = [jax.ShapeDtypeStruct((n_rows, D_CONV), BF16), jax.ShapeDtypeStruct((n_rows, D_CONV), F32)]
    small = [conv_w, conv_b, ln_g, ln_b, out_g]
    return pl.pallas_call(
        body, name="fwd_conv", grid=(nt,),
        in_specs=[_tile_spec(ag.shape)] + [_whole_spec(a.shape) for a in small],
        out_specs=[_tile_spec(s.shape) for s in out_shapes],
        out_shape=out_shapes,
        scratch_shapes=[pltpu.VMEM((CONV_PLANES, ROW_TILE + CONV_HALO, _LANES), F32),
                        pltpu.VMEM((CONV_PLANES, ROW_TILE, _LANES), F32)],
        compiler_params=_params("arbitrary"),
    )(ag, *small)


def _lane_tile(shape):
    if len(shape) == 2:
        return pl.BlockSpec((shape[0], ROW_TILE), lambda i: (0, i))
    return pl.BlockSpec((shape[0], shape[1], ROW_TILE), lambda i: (0, 0, i))


def _rope_rows(x, cos, sin):
    half = QK_ROPE // 2
    x1, x2 = x[:half], x[half:]
    return jnp.concatenate([x1 * cos - x2 * sin, x2 * cos + x1 * sin], axis=0)


def _rope_rows_t(dy, cos, sin):
    half = QK_ROPE // 2
    d1, d2 = dy[:half], dy[half:]
    return jnp.concatenate([d1 * cos + d2 * sin, d2 * cos - d1 * sin], axis=0)


def _fwd_qkv(cq, ckv, kr, gq, gkv, wq_t, w_ukv, wv_t, cos, sin, cos_t, sin_t, n_rows):
    nt = n_rows // ROW_TILE

    def body(cq_ref, ckv_ref, kr_ref, gq_ref, gkv_ref, wqt_ref, wkv_ref, wvt_ref, cos_ref, sin_ref, cost_ref, sint_ref,
             qt_ref, k_ref, v_ref, vt_ref, cqn_ref, ckvn_ref):
        cqn = _rms_fwd(cq_ref[...], gq_ref[...]).astype(BF16)
        ckvn = _rms_fwd(ckv_ref[...], gkv_ref[...]).astype(BF16)
        cqn_ref[...] = cqn
        ckvn_ref[...] = ckvn
        k_rot = _rope(kr_ref[...], cos_ref[...], sin_ref[...])
        cos_rows, sin_rows = cost_ref[...], sint_ref[...]
        q_all = _dot_nt(wqt_ref[...].reshape(N_HEADS * QK_DIM, Q_LORA), cqn)
        vt_all = _dot_nt(wvt_ref[...].reshape(N_HEADS * V_HEAD, KV_LORA), ckvn).astype(BF16)
        for h in range(N_HEADS):
            q_raw = q_all[h * QK_DIM:(h + 1) * QK_DIM]
            q_h = jnp.concatenate([q_raw[:QK_NOPE], _rope_rows(q_raw[QK_NOPE:], cos_rows, sin_rows)], axis=0)
            qt_ref[h] = (q_h * QK_LOGIT_SCALE).astype(BF16)
            kv = _dot(ckvn, wkv_ref[h])
            k_ref[h] = jnp.concatenate([kv[:, :QK_NOPE], k_rot], axis=-1).astype(BF16)
            v_ref[h] = kv[:, QK_NOPE:].astype(BF16)
            vt_ref[h] = vt_all[h * V_HEAD:(h + 1) * V_HEAD]

    out_shapes = [
        jax.ShapeDtypeStruct((N_HEADS, QK_DIM, n_rows), BF16),
        jax.ShapeDtypeStruct((N_HEADS, n_rows, QK_DIM), BF16),
        jax.ShapeDtypeStruct((N_HEADS, n_rows, V_HEAD), BF16),
        jax.ShapeDtypeStruct((N_HEADS, V_HEAD, n_rows), BF16),
        jax.ShapeDtypeStruct((n_rows, Q_LORA), BF16),
        jax.ShapeDtypeStruct((n_rows, KV_LORA), BF16),
    ]
    tiles = [cq, ckv, kr]
    whole = [gq, gkv, wq_t, w_ukv, wv_t]
    out_specs = [_lane_tile(out_shapes[0].shape), _tile_spec(out_shapes[1].shape), _tile_spec(out_shapes[2].shape),
                 _lane_tile(out_shapes[3].shape), _tile_spec(out_shapes[4].shape), _tile_spec(out_shapes[5].shape)]
    return pl.pallas_call(
        body, name="fwd_qkv", grid=(nt,),
        in_specs=[_tile_spec(a.shape) for a in tiles] + [_whole_spec(a.shape) for a in whole]
        + [_tile_spec(cos.shape), _tile_spec(sin.shape), _lane_tile(cos_t.shape), _lane_tile(sin_t.shape)],
        out_specs=out_specs,
        out_shape=out_shapes,
        compiler_params=_params("parallel"),
    )(*tiles, *whole, cos, sin, cos_t, sin_t)


def _chunk_of(rows):
    return jnp.where(rows >= ROW_TILE, lax.shift_right_arithmetic(rows - ROW_TILE, CHUNK_SHIFT) + 1, 0)


def _visible(i, j):
    k_rows = j * ROW_TILE + lax.broadcasted_iota(jnp.int32, (ROW_TILE, 1), 0)
    q_rows = i * ROW_TILE + lax.broadcasted_iota(jnp.int32, (1, ROW_TILE), 1)
    return jnp.logical_and(_chunk_of(q_rows) >= _chunk_of(k_rows), k_rows >= DEAD)


def _attn_fwd(q_t, k, v_t, n_rows):
    nt = n_rows // ROW_TILE

    def body(qt_ref, k_ref, vt_ref, ot_ref, lse_ref, max_ref, sum_ref):
        i = pl.program_id(0)
        q_ts = [qt_ref[h] for h in range(N_HEADS)]

        def key_rows(j):
            return pl.ds(pl.multiple_of(j * ROW_TILE, ROW_TILE), ROW_TILE)

        def make_step(masked, tiles, first=0):
            def step(t, carry):
                js = [first + tiles * t + u for u in range(tiles)]
                scores = [[_dot(k_ref[h, key_rows(j), :], q_ts[h]) for h in range(N_HEADS)] for j in js]
                for j, tile_scores in zip(js, scores):
                    visible = _visible(i, j) if masked else None
                    probs, alphas = [], []
                    for h in range(N_HEADS):
                        m = max_ref[h]
                        s = jnp.where(visible, tile_scores[h], NEG) if masked else tile_scores[h]
                        m_new = jnp.maximum(m, jnp.max(s, axis=0, keepdims=True))
                        alpha = jnp.exp2(m - m_new)
                        p = jnp.exp2(s - m_new)
                        probs.append(p.astype(BF16))
                        alphas.append(alpha)
                        max_ref[h] = m_new
                        sum_ref[h] = alpha * sum_ref[h] + jnp.sum(p, axis=0, keepdims=True)
                    for h in range(N_HEADS):
                        ot_ref[h] = alphas[h] * ot_ref[h] + _dot(vt_ref[h, :, key_rows(j)], probs[h])
                return carry
            return step

        max_ref[...] = jnp.full(max_ref.shape, NEG, F32)
        sum_ref[...] = jnp.zeros_like(sum_ref)
        ot_ref[...] = jnp.zeros_like(ot_ref)
        between = jnp.maximum(i - 1, 0)
        quads = lax.shift_right_logical(between, 2)
        pairs = jnp.bitwise_and(lax.shift_right_logical(between, 1), 1)
        make_step(True, 1)(0, 0)
        lax.fori_loop(0, quads, make_step(False, 4, first=1), 0)
        lax.fori_loop(0, pairs, make_step(False, 2, first=1 + 4 * quads), 0)
        lax.fori_loop(1 + 4 * quads + 2 * pairs, i, make_step(False, 1), 0)
        lax.fori_loop(jnp.maximum(i, 1), i + 1, make_step(True, 1), 0)
        for h in range(N_HEADS):
            l = sum_ref[h]
            ot_ref[h] = ot_ref[h] / l
            lse_ref[h] = max_ref[h] + jnp.log2(l)

    out_shapes = [jax.ShapeDtypeStruct((N_HEADS, V_HEAD, n_rows), F32), jax.ShapeDtypeStruct((N_HEADS, 1, n_rows), F32)]
    return pl.pallas_call(
        body, name="attn_fwd", grid=(nt,),
        in_specs=[_lane_tile(q_t.shape), _whole_spec(k.shape), _whole_spec(v_t.shape)],
        out_specs=[_lane_tile(s.shape) for s in out_shapes],
        out_shape=out_shapes,
        scratch_shapes=[pltpu.VMEM((N_HEADS, 1, ROW_TILE), F32), pltpu.VMEM((N_HEADS, 1, ROW_TILE), F32)],
        compiler_params=_params("parallel"),
    )(q_t, k, v_t)


def _heads_to_rows(ref):
    return jnp.concatenate([ref[h] for h in range(N_HEADS)], axis=0)


def _rms_cols(x, g_col):
    r = lax.rsqrt(jnp.mean(x * x, axis=0, keepdims=True) + EPS)
    return x * r * g_col


def _fwd_out(x, meta_pad, mix_a, o_t, gb_col, w_out, n_rows):
    nt = n_rows // ROW_TILE

    def body(x_ref, meta_ref, mixa_ref, ot_ref, gb_ref, w_ref, mixbt_ref, h1_ref):
        i = pl.program_id(0)
        h0 = jnp.where(i == 0, meta_ref[...], x_ref[...])
        mix_bt = _rms_cols(_heads_to_rows(ot_ref), gb_ref[...]).astype(BF16)
        mixbt_ref[...] = mix_bt
        h1_ref[...] = h0 + _dot(mixa_ref[...], w_ref[:D_CONV, :]) + _dot_tn(mix_bt, w_ref[D_CONV:, :])

    out_shapes = [jax.ShapeDtypeStruct((D_ATTN, n_rows), BF16), jax.ShapeDtypeStruct((n_rows, D_MODEL), F32)]
    return pl.pallas_call(
        body, name="fwd_out", grid=(nt,),
        in_specs=[_real_spec(D_MODEL), _whole_spec(meta_pad.shape), _tile_spec(mix_a.shape), _lane_tile(o_t.shape),
                  _whole_spec(gb_col.shape), _whole_spec(w_out.shape)],
        out_specs=[_lane_tile(out_shapes[0].shape), _tile_spec(out_shapes[1].shape)],
        out_shape=out_shapes,
        compiler_params=_params("parallel"),
    )(x, meta_pad, mix_a, o_t, gb_col, w_out)


PHASES = 8
PHASE_ROWS = ROW_TILE // PHASES
UP_PLANES = -(-UP_SLAB // _LANES)
UP_PAD = UP_PLANES * _LANES
CONV_PLANES = D_CONV // _LANES


def _phase(start):
    return pl.ds(start, PHASE_ROWS, stride=PHASES)


def _to_planes(ref, lead, rows, value):
    width = value.shape[-1]
    for c in range(-(-width // _LANES)):
        part = value[:, c * _LANES:min((c + 1) * _LANES, width)]
        if part.shape[-1] < _LANES:
            part = jnp.concatenate([part, jnp.zeros((part.shape[0], _LANES - part.shape[-1]), part.dtype)], axis=-1)
        ref[(*lead, c, rows, slice(None))] = part


def _from_planes(ref, lead, width):
    planes = [ref[(*lead, c)] for c in range(-(-width // _LANES))]
    last = width - (len(planes) - 1) * _LANES
    return jnp.concatenate(planes[:-1] + [planes[-1][:, :last]], axis=-1)


def _fwd_ffn(h1, target, g2, w_up, fw, fb, w_down, gf, n_rows):
    nt = n_rows // ROW_TILE

    def body(h1_ref, t_ref, g2_ref, wup_ref, fw_ref, fb_ref, wdn_ref, gf_ref,
             n2_ref, up0_ref, act_ref, da_ref, db_ref, dh2_ref, loss_ref, dgf_ref, ext_ref):
        i = pl.program_id(0)

        @pl.when(i == 0)
        def _():
            n2_m = _rms_fwd(h1_ref[DEAD:, :], g2_ref[...]).astype(BF16)
            n2_ref[0:DEAD, :] = jnp.zeros((DEAD, D_MODEL), BF16)
            n2_ref[DEAD:, :] = n2_m
            for s in range(N_DEV):
                up0_m = _dot_nt(n2_m, wup_ref[s])
                up0_ref[s, 0:DEAD, :] = jnp.zeros((DEAD, UP_SLAB), BF16)
                up0_ref[s, DEAD:, :] = up0_m.astype(BF16)
                ext_ref[s, 0:FFN_HALO, :] = up0_m
            act_ref[...] = jnp.zeros_like(act_ref)
            da_ref[...] = jnp.zeros_like(da_ref)
            db_ref[...] = jnp.zeros_like(db_ref)
            dh2_ref[...] = jnp.zeros_like(dh2_ref)
            loss_ref[...] = jnp.zeros_like(loss_ref)
            dgf_ref[...] = jnp.zeros_like(dgf_ref)

        @pl.when(i > 0)
        def _():
            h1_t = h1_ref[...]
            n2 = _rms_fwd(h1_t, g2_ref[...]).astype(BF16)
            n2_ref[...] = n2
            for s in range(N_DEV):
                up0 = _dot_nt(n2, wup_ref[s])
                up0_ref[s] = up0.astype(BF16)
                ext_ref[s, FFN_HALO:, :] = up0

            def conv(s):
                block = ext_ref[s]
                acc = fb_ref[s, :, :UP_SLAB] + fw_ref[s, FFN_CONV_WIDTH - 1:FFN_CONV_WIDTH, :UP_SLAB] * block[FFN_HALO:]
                for back in range(1, FFN_CONV_WIDTH):
                    k = FFN_CONV_WIDTH - 1 - back
                    acc = acc + fw_ref[s, k:k + 1, :UP_SLAB] * pltpu.roll(block, back, 0)[FFN_HALO:]
                return acc

            h2 = h1_t
            for s in range(N_ACT_SLAB):
                gate = conv(s)
                val = conv(s + N_ACT_SLAB)
                sg = _sigmoid(gate)
                silu = gate * sg
                act = (silu * val).astype(BF16)
                act_ref[s] = act
                da_ref[s] = (val * sg * (1.0 + gate * (1.0 - sg))).astype(BF16)
                db_ref[s] = silu.astype(BF16)
                h2 = h2 + _dot(act, wdn_ref[s])
            ext_ref[:, 0:FFN_HALO, :] = ext_ref[:, ROW_TILE:ROW_TILE + FFN_HALO, :]

            gf_t = gf_ref[...]
            diff = _rms_fwd(h2, gf_t) - t_ref[...]
            tile_loss = 0.5 * jnp.sum(jnp.sum(diff * diff, axis=-1, keepdims=True), axis=0, keepdims=True) / D_MODEL
            dh2, dgf = _rms_bwd(diff / D_MODEL, h2, gf_t)
            dh2_ref[...] = dh2
            loss_ref[...] += jnp.broadcast_to(tile_loss, loss_ref.shape)
            dgf_ref[...] += dgf

    act_like = jax.ShapeDtypeStruct((N_ACT_SLAB, n_rows, UP_SLAB), BF16)
    out_shapes = [
        jax.ShapeDtypeStruct((n_rows, D_MODEL), BF16),
        jax.ShapeDtypeStruct((N_DEV, n_rows, UP_SLAB), BF16),
        act_like, act_like, act_like,
        jax.ShapeDtypeStruct((n_rows, D_MODEL), F32),
        jax.ShapeDtypeStruct((8, 128), F32),
        jax.ShapeDtypeStruct((1, D_MODEL), F32),
    ]
    whole = [g2, w_up, fw, fb, w_down, gf]
    return pl.pallas_call(
        body, name="fwd_ffn", grid=(nt,),
        in_specs=[_tile_spec(h1.shape), _real_spec(D_MODEL)] + [_whole_spec(a.shape) for a in whole],
        out_specs=[_tile_spec(s.shape) for s in out_shapes[:6]] + [_acc_spec(s.shape) for s in out_shapes[6:]],
        out_shape=out_shapes,
        scratch_shapes=[pltpu.VMEM((N_DEV, ROW_TILE + FFN_HALO, UP_SLAB), F32)],
        compiler_params=_params("arbitrary"),
    )(h1, target, *whole)


def _rope_tables(n_rows):
    pos = jnp.maximum(jnp.arange(n_rows, dtype=jnp.int32) - DEAD, 0)
    inv_freq = 1.0 / (ROPE_THETA ** (jnp.arange(0, QK_ROPE, 2, dtype=F32) / QK_ROPE))
    ang_t = inv_freq[:, None] * pos.astype(F32)[None, :]
    return jnp.cos(ang_t), jnp.sin(ang_t)


def _halo_after(shape, halo, n_rows):
    last = n_rows // halo - 1
    step = ROW_TILE // halo
    if len(shape) == 2:
        return pl.BlockSpec((halo, shape[1]), lambda i: (jnp.minimum((i + 1) * step, last), 0))
    return pl.BlockSpec((shape[0], halo, shape[2]), lambda i: (0, jnp.minimum((i + 1) * step, last), 0))


def _bwd_ffn_act(dh2, da, db, w_down, n_rows):
    nt = n_rows // ROW_TILE

    def body(dh2_ref, da_ref, db_ref, wdn_ref, dup_ref, dfb_ref):
        i = pl.program_id(0)

        @pl.when(i == 0)
        def _():
            dfb_ref[...] = jnp.zeros_like(dfb_ref)
            dup_ref[...] = jnp.zeros_like(dup_ref)

        @pl.when(i > 0)
        def _():
            dh2_b = dh2_ref[...].astype(BF16)
            for s in range(N_ACT_SLAB):
                d_act = _dot_nt(dh2_b, wdn_ref[s])
                d_gate = d_act * da_ref[s].astype(F32)
                d_val = d_act * db_ref[s].astype(F32)
                dup_ref[s] = d_gate.astype(BF16)
                dup_ref[s + N_ACT_SLAB] = d_val.astype(BF16)
                dfb_ref[s] += jnp.sum(d_gate, axis=0, keepdims=True)
                dfb_ref[s + N_ACT_SLAB] += jnp.sum(d_val, axis=0, keepdims=True)

    out_shapes = [jax.ShapeDtypeStruct((N_DEV, n_rows, UP_SLAB), BF16), jax.ShapeDtypeStruct((N_DEV, 1, UP_SLAB), F32)]
    return pl.pallas_call(
        body, name="bwd_ffn_act", grid=(nt,),
        in_specs=[_tile_spec(dh2.shape), _tile_spec(da.shape), _tile_spec(db.shape), _whole_spec(w_down.shape)],
        out_specs=[_tile_spec(out_shapes[0].shape), _acc_spec(out_shapes[1].shape)],
        out_shape=out_shapes,
        compiler_params=_params("arbitrary"),
    )(dh2, da, db, w_down)


def _bwd_ffn_up(dup, up0, h1, dh2, g2, w_up, fw, n_rows):
    nt = n_rows // ROW_TILE
    last_tap = FFN_CONV_WIDTH - 1

    def body(dup_ref, dnext_ref, up0_ref, h1_ref, dh2_ref, g2_ref, wup_ref, fw_ref,
             dup0_ref, dh1_ref, dfw_ref, dg2_ref):
        i = pl.program_id(0)

        def conv_transpose(s, rows, d, after, u):
            block = jnp.concatenate([d, after], axis=0)
            dup0 = fw_ref[s, last_tap:last_tap + 1, :UP_SLAB] * d
            dfw_ref[s, last_tap:last_tap + 1, :UP_SLAB] += jnp.sum(d * u, axis=0, keepdims=True)
            for ahead in range(1, FFN_CONV_WIDTH):
                k = last_tap - ahead
                shifted = pltpu.roll(block, rows + FFN_HALO - ahead, 0)[:rows]
                dup0 = dup0 + fw_ref[s, k:k + 1, :UP_SLAB] * shifted
                dfw_ref[s, k:k + 1, :UP_SLAB] += jnp.sum(shifted * u, axis=0, keepdims=True)
            return dup0.astype(BF16)

        @pl.when(i == 0)
        def _():
            dfw_ref[...] = jnp.zeros_like(dfw_ref)
            dn2_m = jnp.zeros((N_META, D_MODEL), F32)
            for s in range(N_DEV):
                dup0_m = conv_transpose(s, N_META, dup_ref[s, DEAD:, :].astype(F32), dnext_ref[s].astype(F32),
                                        up0_ref[s, DEAD:, :].astype(F32))
                dup0_ref[s, 0:DEAD, :] = jnp.zeros((DEAD, UP_SLAB), BF16)
                dup0_ref[s, DEAD:, :] = dup0_m
                dn2_m = dn2_m + _dot(dup0_m, wup_ref[s])
            dx_m, dg2 = _rms_bwd(dn2_m, h1_ref[DEAD:, :], g2_ref[...])
            dh1_ref[0:DEAD, :] = jnp.zeros((DEAD, D_MODEL), F32)
            dh1_ref[DEAD:, :] = dh2_ref[DEAD:, :] + dx_m
            dg2_ref[...] = dg2

        @pl.when(i > 0)
        def _():
            dn2 = jnp.zeros((ROW_TILE, D_MODEL), F32)
            for s in range(N_DEV):
                after = jnp.where(i == nt - 1, 0.0, dnext_ref[s].astype(F32))
                dup0_b = conv_transpose(s, ROW_TILE, dup_ref[s].astype(F32), after, up0_ref[s].astype(F32))
                dup0_ref[s] = dup0_b
                dn2 = dn2 + _dot(dup0_b, wup_ref[s])
            dx, dg2 = _rms_bwd(dn2, h1_ref[...], g2_ref[...])
            dh1_ref[...] = dh2_ref[...] + dx
            dg2_ref[...] += dg2

    out_shapes = [
        jax.ShapeDtypeStruct((N_DEV, n_rows, UP_SLAB), BF16),
        jax.ShapeDtypeStruct((n_rows, D_MODEL), F32),
        jax.ShapeDtypeStruct((N_DEV, FFN_CONV_WIDTH, UP_PAD), F32),
        jax.ShapeDtypeStruct((1, D_MODEL), F32),
    ]
    return pl.pallas_call(
        body, name="bwd_ffn_up", grid=(nt,),
        in_specs=[_tile_spec(dup.shape), _halo_after(dup.shape, FFN_HALO, n_rows), _tile_spec(up0.shape),
                  _tile_spec(h1.shape), _tile_spec(dh2.shape),
                  _whole_spec(g2.shape), _whole_spec(w_up.shape), _whole_spec(fw.shape)],
        out_specs=[_tile_spec(s.shape) for s in out_shapes[:2]] + [_acc_spec(s.shape) for s in out_shapes[2:]],
        out_shape=out_shapes,
        compiler_params=_params("arbitrary"),
    )(dup, dup, up0, h1, dh2, g2, w_up, fw)


def _bwd_out(dh1, o_t, u1, w_out, gb_col, ln_g, ln_b, ga, n_rows):
    nt = n_rows // ROW_TILE

    def body(dh1_ref, ot_ref, u1_ref, w_ref, gb_ref, lg_ref, lb_ref, ga_ref,
             dot_ref, delta_ref, du1_ref, dgb_ref, dga_ref, dlg_ref, dlb_ref, dcb_ref):
        i = pl.program_id(0)
        dh1_b = dh1_ref[...].astype(BF16)
        o_t = _heads_to_rows(ot_ref)
        gb = gb_ref[...]
        r = lax.rsqrt(jnp.mean(o_t * o_t, axis=0, keepdims=True) + EPS)
        dmix_bt = _dot_nt(w_ref[D_CONV:, :], dh1_b)
        wgt = dmix_bt * gb
        do_t = r * wgt - o_t * (r * r * r) * jnp.mean(wgt * o_t, axis=0, keepdims=True)
        dgb = jnp.sum(dmix_bt * o_t * r, axis=1, keepdims=True)
        for h in range(N_HEADS):
            do_h = do_t[h * V_HEAD:(h + 1) * V_HEAD]
            dot_ref[h] = do_h.astype(BF16)
            delta_ref[h] = jnp.sum(do_h * ot_ref[h], axis=0, keepdims=True)
        lg = lg_ref[...]
        xh, u2, u3, rstd = _conv_chain(u1_ref[...], lg, lb_ref[...])
        du3, dga = _rms_bwd(_dot_nt(dh1_b, w_ref[:D_CONV, :]), u3, ga_ref[...])
        sg = _sigmoid(u2)
        du2 = du3 * sg * (1.0 + u2 * (1.0 - sg))
        dxh = du2 * lg
        du1 = rstd * (dxh - jnp.mean(dxh, axis=-1, keepdims=True) - xh * jnp.mean(dxh * xh, axis=-1, keepdims=True))
        du1_ref[...] = du1
        first = i == 0
        _accumulate(dgb_ref, first, dgb)
        _accumulate(dga_ref, first, dga)
        _accumulate(dlg_ref, first, jnp.sum(du2 * xh, axis=0, keepdims=True))
        _accumulate(dlb_ref, first, jnp.sum(du2, axis=0, keepdims=True))
        _accumulate(dcb_ref, first, jnp.sum(du1, axis=0, keepdims=True))

    out_shapes = [
        jax.ShapeDtypeStruct((N_HEADS, V_HEAD, n_rows), BF16),
        jax.ShapeDtypeStruct((N_HEADS, 1, n_rows), F32),
        jax.ShapeDtypeStruct((n_rows, D_CONV), F32),
        jax.ShapeDtypeStruct((D_ATTN, 1), F32),
    ] + [jax.ShapeDtypeStruct((1, D_CONV), F32)] * 4
    whole = [w_out, gb_col, ln_g, ln_b, ga]
    return pl.pallas_call(
        body, name="bwd_out", grid=(nt,),
        in_specs=[_tile_spec(dh1.shape), _lane_tile(o_t.shape), _tile_spec(u1.shape)] + [_whole_spec(a.shape) for a in whole],
        out_specs=[_lane_tile(out_shapes[0].shape), _lane_tile(out_shapes[1].shape), _tile_spec(out_shapes[2].shape)]
        + [_acc_spec(s.shape) for s in out_shapes[3:]],
        out_shape=out_shapes,
        compiler_params=_params("arbitrary"),
    )(dh1, o_t, u1, *whole)


ATTN_BWD_HEADS = 8


def _attn_bwd(q_t, k, v, do_t, lse, delta, n_rows):
    nt = n_rows // ROW_TILE
    hp = ATTN_BWD_HEADS

    def body(k_ref, v_ref, qt_ref, dot_ref, lse_ref, delta_ref, dqt_ref, dk_ref, dv_ref):
        j = pl.program_id(1)

        @pl.when(j == 0)
        def _():
            dqt_ref[...] = jnp.zeros_like(dqt_ref)

        k_ts = [k_ref[h] for h in range(hp)]
        v_ts = [v_ref[h] for h in range(hp)]

        def make_step(masked, tiles, first=0):
            def step(t, carry):
                tiles_of_step = []
                for u in range(tiles):
                    i = first + tiles * t + u
                    cols = pl.ds(pl.multiple_of(i * ROW_TILE, ROW_TILE), ROW_TILE)
                    q_is = [qt_ref[h, :, cols] for h in range(hp)]
                    do_is = [dot_ref[h, :, cols] for h in range(hp)]
                    scores = [_dot(k_ts[h], q_is[h]) for h in range(hp)]
                    dps = [_dot(v_ts[h], do_is[h]) for h in range(hp)]
                    tiles_of_step.append((i, cols, q_is, do_is, scores, dps))
                for i, cols, q_is, do_is, scores, dps in tiles_of_step:
                    visible = _visible(i, j) if masked else None
                    probs, dss = [], []
                    for h in range(hp):
                        s = jnp.where(visible, scores[h], NEG) if masked else scores[h]
                        p = jnp.exp2(s - lse_ref[h, :, cols])
                        probs.append(p.astype(BF16))
                        dss.append((p * (dps[h] - delta_ref[h, :, cols])).astype(BF16))
                    for h in range(hp):
                        dv_ref[h] += _dot_nt(probs[h], do_is[h])
                        dk_ref[h] += _dot_nt(dss[h], q_is[h])
                        dqt_ref[h, :, cols] += _dot_tn(k_ts[h], dss[h])
                return carry
            return step

        dk_ref[...] = jnp.zeros_like(dk_ref)
        dv_ref[...] = jnp.zeros_like(dv_ref)
        make_step(True, 1)(j, 0)
        lax.fori_loop(jnp.where(j == 0, j + 1, nt), nt, make_step(True, 1), 0)
        unmasked = jnp.where(j == 0, 0, nt - 1 - j)
        quads = lax.shift_right_logical(unmasked, 2)
        pairs = jnp.bitwise_and(lax.shift_right_logical(unmasked, 1), 1)
        lax.fori_loop(0, quads, make_step(False, 4, first=j + 1), 0)
        lax.fori_loop(0, pairs, make_step(False, 2, first=j + 1 + 4 * quads), 0)
        lax.fori_loop(jnp.where(j == 0, nt, j + 1 + 4 * quads + 2 * pairs), nt, make_step(False, 1), 0)
        dk_ref[...] = dk_ref[...] * _LN2

    key_tile = lambda w: pl.BlockSpec((hp, ROW_TILE, w), lambda g, j: (g, j, 0))
    all_cols = lambda w: pl.BlockSpec((hp, w, n_rows), lambda g, j: (g, 0, 0))
    resident = lambda w: pl.BlockSpec((hp, w, n_rows), lambda g, j: (g, 0, 0), pipeline_mode=pl.Buffered(1))
    out_shapes = [
        jax.ShapeDtypeStruct((N_HEADS, QK_DIM, n_rows), F32),
        jax.ShapeDtypeStruct((N_HEADS, n_rows, QK_DIM), F32),
        jax.ShapeDtypeStruct((N_HEADS, n_rows, V_HEAD), F32),
    ]
    return pl.pallas_call(
        body, name="attn_bwd", grid=(N_HEADS // hp, nt),
        in_specs=[key_tile(QK_DIM), key_tile(V_HEAD), resident(QK_DIM), resident(V_HEAD), resident(1), resident(1)],
        out_specs=[all_cols(QK_DIM), key_tile(QK_DIM), key_tile(V_HEAD)],
        out_shape=out_shapes,
        compiler_params=_params("parallel", "arbitrary"),
    )(k, v, q_t, do_t, lse, delta)


def _bwd_qkv(dq_t, dk, dv, cq, ckv, gq, gkv, wq_t, w_ukv, cos, sin, cos_t, sin_t, n_rows):
    nt = n_rows // ROW_TILE

    def body(dqt_ref, dk_ref, dv_ref, cq_ref, ckv_ref, gq_ref, gkv_ref, wqt_ref, wkv_ref, cos_ref, sin_ref,
             cost_ref, sint_ref, dqraw_ref, dkv_ref, dcq_ref, dckv_ref, dkr_ref, dgq_ref, dgkv_ref):
        i = pl.program_id(0)
        cos_rows, sin_rows = cost_ref[...], sint_ref[...]
        dcqn = jnp.zeros((ROW_TILE, Q_LORA), F32)
        dckvn = jnp.zeros((ROW_TILE, KV_LORA), F32)
        dk_rot = jnp.zeros((ROW_TILE, QK_ROPE), F32)
        for h in range(N_HEADS):
            dq_h, dk_h = dqt_ref[h] * QK_DIM ** -0.5, dk_ref[h]
            dq_raw = jnp.concatenate(
                [dq_h[:QK_NOPE], _rope_rows_t(dq_h[QK_NOPE:], cos_rows, sin_rows)], axis=0).astype(BF16)
            dqraw_ref[h] = dq_raw
            dcqn = dcqn + _dot_tn(dq_raw, wqt_ref[h])
            dkv = jnp.concatenate([dk_h[:, :QK_NOPE], dv_ref[h]], axis=-1).astype(BF16)
            dkv_ref[:, h * KV_HEAD:(h + 1) * KV_HEAD] = dkv
            dckvn = dckvn + _dot_nt(dkv, wkv_ref[h])
            dk_rot = dk_rot + dk_h[:, QK_NOPE:]
        dkr_ref[...] = _rope_t(dk_rot, cos_ref[...], sin_ref[...]).astype(BF16)
        dcq, dgq = _rms_bwd(dcqn, cq_ref[...], gq_ref[...])
        dckv, dgkv = _rms_bwd(dckvn, ckv_ref[...], gkv_ref[...])
        dcq_ref[...] = dcq.astype(BF16)
        dckv_ref[...] = dckv.astype(BF16)
        _accumulate(dgq_ref, i == 0, dgq)
        _accumulate(dgkv_ref, i == 0, dgkv)

    out_shapes = [
        jax.ShapeDtypeStruct((N_HEADS, QK_DIM, n_rows), BF16),
        jax.ShapeDtypeStruct((n_rows, N_HEADS * KV_HEAD), BF16),
        jax.ShapeDtypeStruct((n_rows, Q_LORA), BF16),
        jax.ShapeDtypeStruct((n_rows, KV_LORA), BF16),
        jax.ShapeDtypeStruct((n_rows, QK_ROPE), BF16),
        jax.ShapeDtypeStruct((1, Q_LORA), F32),
        jax.ShapeDtypeStruct((1, KV_LORA), F32),
    ]
    tiles = [dk, dv, cq, ckv]
    whole = [gq, gkv, wq_t, w_ukv]
    return pl.pallas_call(
        body, name="bwd_qkv", grid=(nt,),
        in_specs=[_lane_tile(dq_t.shape)] + [_tile_spec(a.shape) for a in tiles] + [_whole_spec(a.shape) for a in whole]
        + [_tile_spec(cos.shape), _tile_spec(sin.shape), _lane_tile(cos_t.shape), _lane_tile(sin_t.shape)],
        out_specs=[_lane_tile(out_shapes[0].shape)] + [_tile_spec(s.shape) for s in out_shapes[1:5]]
        + [_acc_spec(s.shape) for s in out_shapes[5:]],
        out_shape=out_shapes,
        compiler_params=_params("arbitrary"),
    )(dq_t, *tiles, *whole, cos, sin, cos_t, sin_t)


def _bwd_conv(du1, ag, conv_w, dcq, dckv, dkr, n_rows):
    nt = n_rows // ROW_TILE

    last_tap = CONV_WIDTH - 1

    def body(du1_ref, dnext_ref, ag_ref, w_ref, dcq_ref, dckv_ref, dkr_ref, dz_ref, dw_ref,
             dext_ref, uext_ref, conv_ref, sums_ref):
        i = pl.program_id(0)

        @pl.when(i == 0)
        def _():
            sums_ref[...] = jnp.zeros_like(sums_ref)

        _to_planes(dext_ref, (), slice(0, ROW_TILE), du1_ref[...])
        _to_planes(dext_ref, (), slice(ROW_TILE, None), jnp.where(i == nt - 1, 0.0, dnext_ref[...]))
        ag_t = ag_ref[...]
        live = _row_ids(i, ROW_TILE) >= DEAD
        sg = _sigmoid(ag_t[:, D_CONV:])
        _to_planes(uext_ref, (), slice(None), jnp.where(live, ag_t[:, :D_CONV] * sg, 0.0))
        for c in range(CONV_PLANES):
            taps = w_ref[:, c * _LANES:(c + 1) * _LANES]
            for half in range(0, PHASES, PHASES // 2):
                phases = range(half, half + PHASES // 2)
                us = {p: uext_ref[c, _phase(p), :] for p in phases}
                accs = {p: jnp.zeros((PHASE_ROWS, _LANES), F32) for p in phases}
                for k in range(CONV_WIDTH):
                    tap_sum = jnp.zeros((PHASE_ROWS, _LANES), F32)
                    for p in phases:
                        shifted = dext_ref[c, _phase(p + last_tap - k), :]
                        accs[p] = accs[p] + taps[k:k + 1, :] * shifted
                        tap_sum = tap_sum + shifted * us[p]
                    sums_ref[c, k] += tap_sum
                for p in phases:
                    conv_ref[c, _phase(p), :] = accs[p]
        du0 = jnp.where(live, _from_planes(conv_ref, (), D_CONV), 0.0)
        da = du0 * sg
        dgate = du0 * ag_t[:, :D_CONV] * sg * (1.0 - sg)
        dz_ref[...] = jnp.concatenate(
            [da.astype(BF16), dgate.astype(BF16), dcq_ref[...], dckv_ref[...], dkr_ref[...]], axis=-1)

        @pl.when(i == nt - 1)
        def _():
            for c in range(CONV_PLANES):
                for k in range(CONV_WIDTH):
                    dw_ref[k:k + 1, c * _LANES:(c + 1) * _LANES] = jnp.sum(sums_ref[c, k], axis=0, keepdims=True)

    out_shapes = [jax.ShapeDtypeStruct((n_rows, D_IN), BF16), jax.ShapeDtypeStruct((CONV_WIDTH, D_CONV), F32)]
    return pl.pallas_call(
        body, name="bwd_conv", grid=(nt,),
        in_specs=[_tile_spec(du1.shape), _halo_after(du1.shape, CONV_HALO, n_rows), _tile_spec(ag.shape),
                  _whole_spec(conv_w.shape), _tile_spec(dcq.shape), _tile_spec(dckv.shape), _tile_spec(dkr.shape)],
        out_specs=[_tile_spec(out_shapes[0].shape), _acc_spec(out_shapes[1].shape)],
        out_shape=out_shapes,
        scratch_shapes=[pltpu.VMEM((CONV_PLANES, ROW_TILE + CONV_HALO, _LANES), F32),
                        pltpu.VMEM((CONV_PLANES, ROW_TILE, _LANES), F32), pltpu.VMEM((CONV_PLANES, ROW_TILE, _LANES), F32),
                        pltpu.VMEM((CONV_PLANES, CONV_WIDTH, PHASE_ROWS, _LANES), F32)],
        compiler_params=_params("arbitrary"),
    )(du1, du1, ag, conv_w, dcq, dckv, dkr)


def _bwd_in(dz, x, meta_pad, dh1, g1, w_in, n_rows):
    nt = n_rows // ROW_TILE

    def body(dz_ref, x_ref, meta_ref, dh1_ref, g_ref, w_ref, gx_ref, gmeta_ref, dg1_ref):
        i = pl.program_id(0)
        h0 = jnp.where(i == 0, meta_ref[...], x_ref[...])
        dx, dg1 = _rms_bwd(_dot(dz_ref[...], w_ref[...]), h0, g_ref[...])
        dh0 = dh1_ref[...] + dx
        gx_ref[...] = dh0

        @pl.when(i == 0)
        def _():
            gmeta_ref[...] = dh0

        _accumulate(dg1_ref, i == 0, dg1)

    out_shapes = [
        jax.ShapeDtypeStruct((n_rows - ROW_TILE, D_MODEL), F32),
        jax.ShapeDtypeStruct((ROW_TILE, D_MODEL), F32),
        jax.ShapeDtypeStruct((1, D_MODEL), F32),
    ]
    return pl.pallas_call(
        body, name="bwd_in", grid=(nt,),
        in_specs=[_tile_spec(dz.shape), _real_spec(D_MODEL), _whole_spec(meta_pad.shape), _tile_spec(dh1.shape),
                  _whole_spec(g1.shape), _whole_spec(w_in.shape)],
        out_specs=[_real_spec(D_MODEL), _acc_spec(out_shapes[1].shape), _acc_spec(out_shapes[2].shape)],
        out_shape=out_shapes,
        compiler_params=_params("arbitrary"),
    )(dz, x, meta_pad, dh1, g1, w_in)


def _contraction_tile(n_rows):
    return next(t for t in range(n_rows // 2 // _LANES * _LANES, 0, -_LANES) if n_rows % t == 0)


def _weight_grad(a, b, name, a_transposed=False):
    groups = max(a.shape[0] if a.ndim == 3 else 1, b.shape[0] if b.ndim == 3 else 1)
    n_rows, n = b.shape[-2], b.shape[-1]
    m = a.shape[-2] if a_transposed else a.shape[-1]
    kt = _contraction_tile(n_rows)
    steps = n_rows // kt

    def body(a_ref, b_ref, out_ref, acc_ref):
        i = pl.program_id(1)
        a_t, b_t = a_ref[...].astype(BF16), b_ref[...].astype(BF16)
        part = _dot(a_t, b_t) if a_transposed else _dot_tn(a_t, b_t)
        _accumulate(acc_ref, i == 0, part)

        @pl.when(i == steps - 1)
        def _():
            out_ref[...] = acc_ref[...].astype(out_ref.dtype)

    def spec(arr, rows_last):
        block = (arr.shape[-2], kt) if rows_last else (kt, arr.shape[-1])
        at = (lambda i: (0, i)) if rows_last else (lambda i: (i, 0))
        if arr.ndim == 3:
            return pl.BlockSpec((None,) + block, lambda g, i: (g,) + at(i))
        return pl.BlockSpec(block, lambda g, i: at(i))

    return pl.pallas_call(
        body, name=name, grid=(groups, steps),
        in_specs=[spec(a, a_transposed), spec(b, False)],
        out_specs=pl.BlockSpec((None, m, n), lambda g, i: (g, 0, 0)),
        out_shape=jax.ShapeDtypeStruct((groups, m, n), BF16),
        scratch_shapes=[pltpu.VMEM((m, n), F32)],
        compiler_params=_params("parallel", "arbitrary"),
    )(a, b)


def _my_index():
    return 4 * lax.axis_index("x") + 2 * lax.axis_index("y") + lax.axis_index("c")


def _peer(k):
    flip = lambda v, bit: 1 - v if bit else v
    px = flip(lax.axis_index("x"), k & 4)
    py = flip(lax.axis_index("y"), k & 2)
    pc = flip(lax.axis_index("c"), k & 1)
    return (px, py, pc), 4 * px + 2 * py + pc


def _all_gather(shards, dtypes):
    n = len(shards)
    sibling, chips = 1, (2, 4, 6)

    def body(*refs):
        ins, outs, stages = refs[:n], refs[n:2 * n], refs[2 * n:3 * n]
        send_sems, recv_sems, local_sems = refs[3 * n:]
        me = _my_index()
        for a in range(n):
            stages[a][...] = ins[a][...].astype(stages[a].dtype)
        local = [pltpu.make_async_copy(stages[a], outs[a].at[me], local_sems.at[a]) for a in range(n)]
        for cp in local:
            cp.start()

        def copy(a, k, src, slot, to):
            return pltpu.make_async_remote_copy(
                src_ref=src, dst_ref=outs[a].at[slot], send_sem=send_sems.at[a, k - 1],
                recv_sem=recv_sems.at[a, k - 1], device_id=_peer(to)[0], device_id_type=MESH)

        def own(a, k):
            return copy(a, k, stages[a], me, k)

        def passed(a, k):
            slot = _peer(k)[1]
            return copy(a, k ^ sibling, outs[a].at[slot], slot, sibling)

        def arrival(a, k):
            return copy(a, k, stages[a], _peer(k)[1], k)

        for k in (sibling,) + chips:
            for a in range(n):
                own(a, k).start()
        for k in chips:
            for a in range(n):
                arrival(a, k).wait_recv()
                passed(a, k).start()
        for a in range(n):
            arrival(a, sibling).wait_recv()
            for k in chips:
                arrival(a, k ^ sibling).wait_recv()
        for a in range(n):
            for k in (sibling,) + chips:
                own(a, k).wait_send()
            for k in chips:
                passed(a, k).wait_send()
        for cp in local:
            cp.wait()

    return pl.pallas_call(
        body, name="gather_weights",
        in_specs=[pl.BlockSpec(memory_space=pltpu.VMEM)] * n,
        out_specs=[pl.BlockSpec(memory_space=pl.ANY)] * n,
        out_shape=[jax.ShapeDtypeStruct((N_DEV,) + s.shape, dt) for s, dt in zip(shards, dtypes)],
        scratch_shapes=[pltpu.VMEM(s.shape, dt) for s, dt in zip(shards, dtypes)]
        + [pltpu.SemaphoreType.DMA((n, N_DEV - 1)), pltpu.SemaphoreType.DMA((n, N_DEV - 1)), pltpu.SemaphoreType.DMA((n,))],
        compiler_params=pltpu.CompilerParams(vmem_limit_bytes=VMEM_LIMIT),
    )(*shards)


def _exchange(parts, whole):
    n = len(parts)

    def body(*refs):
        ins, outs = refs[:n], refs[n:2 * n]
        send_sems, recv_sems, local_sems = refs[2 * n:]
        me = _my_index()

        def src(a, slab):
            return ins[a] if whole[a] else ins[a].at[slab]

        local = [pltpu.make_async_copy(src(a, me), outs[a].at[me], local_sems.at[a]) for a in range(n)]
        for cp in local:
            cp.start()

        def copy(a, k, slab, slot):
            peer, _ = _peer(k)
            return pltpu.make_async_remote_copy(
                src_ref=src(a, slab), dst_ref=outs[a].at[slot], send_sem=send_sems.at[a, k - 1],
                recv_sem=recv_sems.at[a, k - 1], device_id=peer, device_id_type=MESH)

        for k in range(1, N_DEV):
            for a in range(n):
                copy(a, k, _peer(k)[1], me).start()
        for k in range(1, N_DEV):
            for a in range(n):
                copy(a, k, _peer(k)[1], _peer(k)[1]).wait()
        for cp in local:
            cp.wait()

    return pl.pallas_call(
        body, name="exchange_grads",
        in_specs=[pl.BlockSpec(memory_space=pl.ANY)] * n,
        out_specs=[pl.BlockSpec(memory_space=pl.ANY)] * n,
        out_shape=[jax.ShapeDtypeStruct(((N_DEV,) + p.shape) if w else p.shape, p.dtype) for p, w in zip(parts, whole)],
        scratch_shapes=[pltpu.SemaphoreType.DMA((n, N_DEV - 1)), pltpu.SemaphoreType.DMA((n, N_DEV - 1)),
                        pltpu.SemaphoreType.DMA((n,))],
    )(*parts)


def _sequencer_exchange(parts, whole, name, collective_id):
    n = len(parts)
    srcs = [jax.new_ref(p, memory_space=pltpu.MemorySpace.HBM) for p in parts]
    lands = [jax.empty_ref(jax.ShapeDtypeStruct(((N_DEV,) + p.shape) if w else p.shape, p.dtype),
                           memory_space=pltpu.MemorySpace.HBM) for p, w in zip(parts, whole)]

    @pl.kernel(mesh=plsc.ScalarSubcoreMesh(axis_name="sequencer", num_cores=1), name=name,
               scratch_types=(pltpu.SemaphoreType.DMA((n, N_DEV - 1)), pltpu.SemaphoreType.DMA((n, N_DEV - 1)),
                              pltpu.SemaphoreType.DMA((n,))),
               compiler_params=pltpu.CompilerParams(collective_id=collective_id))
    def launch(send_sems, recv_sems, local_sems):
        barrier = pltpu.get_barrier_semaphore()
        for k in range(1, N_DEV):
            pl.semaphore_signal(barrier, inc=1, device_id=_peer(k)[0], device_id_type=MESH)
        pl.semaphore_wait(barrier, N_DEV - 1)
        me = _my_index()

        def src(a, slab):
            return srcs[a] if whole[a] else srcs[a].at[slab]

        local = [pltpu.make_async_copy(src(a, me), lands[a].at[me], local_sems.at[a]) for a in range(n)]
        for cp in local:
            cp.start()

        def copy(a, k, slab, slot):
            return pltpu.make_async_remote_copy(
                src_ref=src(a, slab), dst_ref=lands[a].at[slot], send_sem=send_sems.at[a, k - 1],
                recv_sem=recv_sems.at[a, k - 1], device_id=_peer(k)[0], device_id_type=MESH)

        for k in range(1, N_DEV):
            for a in range(n):
                copy(a, k, _peer(k)[1], me).start()
        for k in range(1, N_DEV):
            for a in range(n):
                copy(a, k, _peer(k)[1], _peer(k)[1]).wait()
        for cp in local:
            cp.wait()

    launch()
    return [land[...] for land in lands]


def _row_block(rows):
    if rows <= ROW_TILE:
        return rows
    return next(rb for rb in range(ROW_TILE, 0, -16) if rows % rb == 0)


def _adamw(landing, w, m, v, name):
    rows, cols = w.shape
    rb = _row_block(rows)

    def body(l_ref, w_ref, m_ref, v_ref, g_ref, d_ref, m2_ref, v2_ref):
        g = l_ref[0].astype(F32)
        for p in range(1, N_DEV):
            g = g + l_ref[p].astype(F32)
        g_ref[...] = g
        d_ref[...], m2_ref[...], v2_ref[...] = _adamw_step(g, w_ref[...], m_ref[...], v_ref[...])

    flat = pl.BlockSpec((rb, cols), lambda i: (i, 0))
    return pl.pallas_call(
        body, name=name, grid=(rows // rb,),
        in_specs=[pl.BlockSpec((N_DEV, rb, cols), lambda i: (0, i, 0)), flat, flat, flat],
        out_specs=[flat] * 4,
        out_shape=[jax.ShapeDtypeStruct((rows, cols), F32)] * 4,
        compiler_params=_params("parallel"),
    )(landing, w, m, v)


def _adamw_step(g, w, m, v):
    m2 = ADAM_B1 * m + (1.0 - ADAM_B1) * g
    v2 = ADAM_B2 * v + (1.0 - ADAM_B2) * (g * g)
    m_hat = m2 / (1.0 - ADAM_B1 ** ADAM_STEP)
    v_hat = v2 / (1.0 - ADAM_B2 ** ADAM_STEP)
    return -ADAM_LR * (m_hat / (jnp.sqrt(v_hat) + ADAM_EPS) + ADAM_WD * w), m2, v2


_REPLICATED = (
    ("mix_norm_g", D_MODEL), ("q_norm_g", Q_LORA), ("kv_norm_g", KV_LORA), ("conv_b", D_CONV), ("conv_ln_g", D_CONV),
    ("conv_ln_b", D_CONV), ("conv_out_g", D_CONV), ("attn_out_g", D_CONV), ("ffn_norm_g", D_MODEL),
    ("ffn_conv_b", D_UP), ("final_norm_g", D_MODEL),
)
_REPLICATED_WIDTH = sum(size for _, size in _REPLICATED) + _LANES

_WEIGHT_ORDER = (
    "meta_tokens", "mix_norm_g", "w_in", "q_norm_g", "w_uq", "kv_norm_g", "w_ukv", "conv_w", "conv_b", "conv_ln_g",
    "conv_ln_b", "conv_out_g", "attn_out_g", "w_out", "ffn_norm_g", "w_ffn_up", "ffn_conv_w", "ffn_conv_b",
    "w_ffn_down", "final_norm_g",
)


def _pack_replicated(grads, loss):
    rows = [grads[name].reshape(1, size) for name, size in _REPLICATED]
    return jnp.concatenate(rows + [jnp.broadcast_to(loss.reshape(1, 1), (1, _LANES))], axis=-1)


def _adamw_replicated(landing, weights, moments_m, moments_v):
    n = len(_REPLICATED)

    def body(*refs):
        l_ref, ins, outs = refs[0], refs[1:1 + 3 * n], refs[1 + 3 * n:]
        total = l_ref[0]
        for p in range(1, N_DEV):
            total = total + l_ref[p]
        at = 0
        for a, (_, size) in enumerate(_REPLICATED):
            g = total[:, at:at + size]
            w_ref, m_ref, v_ref = ins[3 * a:3 * a + 3]
            g_ref, d_ref, m2_ref, v2_ref = outs[4 * a:4 * a + 4]
            g_ref[...] = g
            d_ref[...], m2_ref[...], v2_ref[...] = _adamw_step(g, w_ref[...], m_ref[...], v_ref[...])
            at += size
        outs[-1][...] = total[:, at:at + _LANES]

    operands, out_shapes = [], []
    for name, size in _REPLICATED:
        operands += [weights[name].reshape(1, size), moments_m[name].reshape(1, size), moments_v[name].reshape(1, size)]
        out_shapes += [jax.ShapeDtypeStruct((1, size), F32)] * 4
    out_shapes.append(jax.ShapeDtypeStruct((1, _LANES), F32))
    outs = pl.pallas_call(body, name="adamw_replicated", out_shape=out_shapes)(landing, *operands)
    return outs[-1][0, 0], {name: outs[4 * a:4 * a + 4] for a, (name, _) in enumerate(_REPLICATED)}


def _pad_rows(a, rows):
    return jnp.pad(a, ((0, rows - a.shape[0]), (0, 0)))


def _slabs(a):
    r, c = a.shape
    return a.reshape(r, N_DEV, c // N_DEV).transpose(1, 0, 2)


def _unslab(a):
    g, r, c = a.shape
    return a.transpose(1, 0, 2).reshape(r, g * c)


def _local_step(x, target, w, n_rows, ffn_weights, send_grads):
    cos_t, sin_t = lax.optimization_barrier(_rope_tables(n_rows))
    cos, sin = cos_t.T, sin_t.T
    meta_pad, g1, gf = w["meta_pad"], w["mix_norm_g"], w["final_norm_g"]
    gq, gkv, gb_col = w["q_norm_g"], w["kv_norm_g"], w["attn_out_g"].reshape(D_ATTN, 1)
    nb, ag, cq, ckv, kr = _fwd_in(x, meta_pad, g1, w["w_in"], n_rows)
    mix_a, u1 = _fwd_conv(ag, w["conv_w"], w["conv_b"], w["conv_ln_g"], w["conv_ln_b"], w["conv_out_g"], n_rows)
    q_t, k, v, v_t, cqn, ckvn = _fwd_qkv(cq, ckv, kr, gq, gkv, w["wq_t"], w["w_ukv"], w["wv_t"], cos, sin, cos_t, sin_t, n_rows)
    o_t, lse = _attn_fwd(q_t, k, v_t, n_rows)
    w_out, w_up, w_down = ffn_weights()
    mix_bt, h1 = _fwd_out(x, meta_pad, mix_a, o_t, gb_col, w_out, n_rows)
    n2, up0, act, da, db, dh2, loss, dgf = _fwd_ffn(
        h1, target, w["ffn_norm_g"], w_up, w["fw"], w["fb"], w_down, gf, n_rows)

    dup, dfb = _bwd_ffn_act(dh2, da, db, w_down, n_rows)
    dup0, dh1, dfw, dg2 = _bwd_ffn_up(dup, up0, h1, dh2, w["ffn_norm_g"], w_up, w["fw"], n_rows)
    grad_w_out = jnp.concatenate([_weight_grad(mix_a, dh1, "grad_w_out_conv")[0],
                                  _weight_grad(mix_bt, dh1, "grad_w_out_attn", a_transposed=True)[0]], axis=0)
    stage0 = {
        "w_ffn_up": _weight_grad(dup0, n2, "grad_w_ffn_up"),
        "w_ffn_down": _weight_grad(act, dh2, "grad_w_ffn_down").reshape(N_DEV, D_FF // N_DEV, D_MODEL),
        "w_out": grad_w_out.reshape(N_DEV, D_MODEL // N_DEV, D_MODEL),
    }
    stage0, dh1 = lax.optimization_barrier((stage0, dh1))
    send_grads(0, stage0)
    do_t, delta, du1, dgb, dga, dlg, dlb, dcb = _bwd_out(
        dh1, o_t, u1, w_out, gb_col, w["conv_ln_g"], w["conv_ln_b"], w["conv_out_g"], n_rows)
    dq_t, dk, dv = _attn_bwd(q_t, k, v, do_t, lse, delta, n_rows)
    dqraw_t, dkv, dcq, dckv, dkr, dgq, dgkv = _bwd_qkv(
        dq_t, dk, dv, cq, ckv, gq, gkv, w["wq_t"], w["w_ukv"], cos, sin, cos_t, sin_t, n_rows)
    dz, dcw = _bwd_conv(du1, ag, w["conv_w"], dcq, dckv, dkr, n_rows)
    stage1 = {
        "w_in": _weight_grad(dz, nb, "grad_w_in")[0].reshape(N_DEV, D_IN // N_DEV, D_MODEL),
        "w_uq": _weight_grad(dqraw_t.reshape(N_HEADS * QK_DIM, n_rows), cqn, "grad_w_uq", a_transposed=True)[0].reshape(
            N_HEADS, QK_DIM, Q_LORA),
        "w_ukv": _slabs(_weight_grad(ckvn, dkv, "grad_w_ukv")[0]),
        "conv_w": _slabs(dcw),
        "ffn_conv_w": dfw[:, :, :UP_SLAB],
    }
    stage1, dz = lax.optimization_barrier((stage1, dz))
    send_grads(1, stage1)
    gx, gmeta, dg1 = _bwd_in(dz, x, meta_pad, dh1, g1, w["w_in"], n_rows)

    sharded = {"meta_tokens": _slabs(gmeta[DEAD:])}
    replicated = {
        "mix_norm_g": dg1, "q_norm_g": dgq, "kv_norm_g": dgkv, "conv_b": dcb, "conv_ln_g": dlg, "conv_ln_b": dlb,
        "conv_out_g": dga, "attn_out_g": dgb, "ffn_norm_g": dg2, "ffn_conv_b": dfb, "final_norm_g": dgf,
    }
    return loss[0, 0], gx, sharded, replicated


_SHARDED = (
    ("w_in", None, BF16), ("w_uq", None, BF16), ("w_ukv", None, BF16), ("w_out", None, BF16), ("w_ffn_up", None, BF16),
    ("w_ffn_down", None, BF16), ("conv_w", 32, F32), ("ffn_conv_w", 8, F32), ("meta_tokens", None, F32),
)
GATHER_LATE_ID = 3
EXCHANGE_STAGE_IDS = (4, 5)
_LATE_WEIGHTS = ("w_out", "w_ffn_up", "w_ffn_down")
_COLUMN_SHARDS = ("w_in", "w_uq", "w_ffn_up")


def kernel(x, meta_tokens, mix_norm_g, w_in, q_norm_g, w_uq, kv_norm_g, w_ukv, conv_w, conv_b, conv_ln_g, conv_ln_b, conv_out_g, attn_out_g, w_out, ffn_norm_g, w_ffn_up, ffn_conv_w, ffn_conv_b, w_ffn_down, final_norm_g, loss_target, m_meta_tokens, m_mix_norm_g, m_w_in, m_q_norm_g, m_w_uq, m_kv_norm_g, m_w_ukv, m_conv_w, m_conv_b, m_conv_ln_g, m_conv_ln_b, m_conv_out_g, m_attn_out_g, m_w_out, m_ffn_norm_g, m_w_ffn_up, m_ffn_conv_w, m_ffn_conv_b, m_w_ffn_down, m_final_norm_g, v_meta_tokens, v_mix_norm_g, v_w_in, v_q_norm_g, v_w_uq, v_kv_norm_g, v_w_ukv, v_conv_w, v_conv_b, v_conv_ln_g, v_conv_ln_b, v_conv_out_g, v_attn_out_g, v_w_out, v_ffn_norm_g, v_w_ffn_up, v_ffn_conv_w, v_ffn_conv_b, v_w_ffn_down, v_final_norm_g):
    given = dict(locals())
    weights = {name: given[name] for name in _WEIGHT_ORDER}
    moments_m = {name: given["m_" + name] for name in _WEIGHT_ORDER}
    moments_v = {name: given["v_" + name] for name in _WEIGHT_ORDER}
    seq = x.shape[1]
    n_rows = ROW_TILE + seq

    def shard2d(name, a):
        a = a.reshape(a.shape[-2], a.shape[-1])
        return a.T if name in _COLUMN_SHARDS else a

    early = [entry for entry in _SHARDED if entry[0] not in _LATE_WEIGHTS]
    shards = []
    for name, pad_to, _ in early:
        s = shard2d(name, weights[name])
        shards.append(s if pad_to is None else _pad_rows(s, pad_to))
    gathered = dict(zip([name for name, _, _ in early], _all_gather(shards, [dt for _, _, dt in early])))
    late_shards, gathered["meta_tokens"] = lax.optimization_barrier(
        ([shard2d(name, weights[name]) for name in _LATE_WEIGHTS], gathered["meta_tokens"]))
    late_parts = [s.astype(BF16) for s in late_shards]
    late = _sequencer_exchange(late_parts, [True] * len(late_parts), "gather_late", GATHER_LATE_ID)
    meta_full = _unslab(gathered["meta_tokens"])
    full = {
        "meta_pad": jnp.concatenate([jnp.zeros((DEAD, D_MODEL), F32), meta_full], axis=0),
        "w_in": gathered["w_in"].reshape(D_IN, D_MODEL),
        "wq_t": gathered["w_uq"],
        "w_ukv": gathered["w_ukv"],
        "wv_t": gathered["w_ukv"][:, :, QK_NOPE:].transpose(0, 2, 1),
        "conv_w": _unslab(gathered["conv_w"][:, :CONV_WIDTH]),
        "fw": jnp.pad(gathered["ffn_conv_w"][:, :FFN_CONV_WIDTH], ((0, 0), (0, 0), (0, UP_PAD - UP_SLAB))),
        "fb": jnp.pad(ffn_conv_b.reshape(N_DEV, 1, UP_SLAB), ((0, 0), (0, 0), (0, UP_PAD - UP_SLAB))),
        "final_norm_g": final_norm_g.reshape(1, D_MODEL),
    }
    for name in ("mix_norm_g", "q_norm_g", "kv_norm_g", "conv_b", "conv_ln_g", "conv_ln_b", "conv_out_g", "attn_out_g",
                 "ffn_norm_g"):
        full[name] = weights[name]

    def ffn_weights():
        w_out_all, w_up_all, w_down_all = late
        return (w_out_all.reshape(D_MODEL, D_MODEL), w_up_all, w_down_all.reshape(N_ACT_SLAB, UP_SLAB, D_MODEL))

    wire = {name: (pad_to, dt) for name, pad_to, dt in _SHARDED}
    landing = {}

    def on_the_wire(name, slabs):
        pad_to, dt = wire[name]
        slabs = slabs.astype(dt)
        return slabs if pad_to is None else jnp.pad(slabs, ((0, 0), (0, pad_to - slabs.shape[1]), (0, 0)))

    def send_grads(stage, grads):
        parts = [on_the_wire(name, slabs) for name, slabs in grads.items()]
        if landing:
            arrived = list(landing)
            parts, held = lax.optimization_barrier((parts, [landing[name] for name in arrived]))
            landing.update(zip(arrived, held))
        landed = _sequencer_exchange(parts, [False] * len(parts), f"exchange_stage{stage}", EXCHANGE_STAGE_IDS[stage])
        landing.update(zip(grads, landed))

    loss, gx, sharded, replicated = _local_step(x[0], loss_target[0], full, n_rows, ffn_weights, send_grads)

    parts = [on_the_wire(name, slabs) for name, slabs in sharded.items()] + [_pack_replicated(replicated, loss)]
    landed = _exchange(parts, [False] * len(sharded) + [True])
    landing.update(zip(sharded, landed[:-1]))

    grad, delta, new_m, new_v = {}, {}, {}, {}
    for name, pad_to, _ in _SHARDED:
        land = landing[name]
        ws, ms, vs = (shard2d(name, a[name]) for a in (weights, moments_m, moments_v))
        rows = ws.shape[0]
        if pad_to is not None:
            ws, ms, vs = _pad_rows(ws, pad_to), _pad_rows(ms, pad_to), _pad_rows(vs, pad_to)
        outs = _adamw(land, ws, ms, vs, "adamw_" + name)
        shape = weights[name].shape
        grad[name], delta[name], new_m[name], new_v[name] = (
            (o.T if name in _COLUMN_SHARDS else o[:rows]).reshape(shape) for o in outs)
    loss, updates = _adamw_replicated(landed[-1], weights, moments_m, moments_v)
    for name, outs in updates.items():
        grad[name], delta[name], new_m[name], new_v[name] = (o.reshape(weights[name].shape) for o in outs)

    return (loss, gx[None], *[grad[n] for n in _WEIGHT_ORDER], *[delta[n] for n in _WEIGHT_ORDER],
            *[new_m[n] for n in _WEIGHT_ORDER], *[new_v[n] for n in _WEIGHT_ORDER])
```

```python
import jax
import jax.numpy as jnp
from jax import lax
from jax.experimental import pallas as pl
from jax.experimental.pallas import tpu as pltpu
from jax.experimental.pallas import tpu_sc as plsc

F32 = jnp.float32
BF16 = jnp.bfloat16

N_DEV = 8
D_MODEL = 1024
CHUNK = 64
CHUNK_SHIFT = 6
N_META = 16
D_CONV = 512
CONV_WIDTH = 31
N_HEADS = 8
QK_NOPE = 64
QK_ROPE = 32
QK_DIM = QK_NOPE + QK_ROPE
V_HEAD = 64
KV_HEAD = QK_NOPE + V_HEAD
D_ATTN = N_HEADS * V_HEAD
Q_LORA = 384
KV_LORA = 256
ROPE_THETA = 10000.0
D_IN = 2 * D_CONV + Q_LORA + KV_LORA + QK_ROPE
D_FF = 2816
D_UP = 2 * D_FF
FFN_CONV_WIDTH = 3
UP_SLAB = D_UP // N_DEV
N_ACT_SLAB = D_FF // UP_SLAB
EPS = 1e-6
NEG = -1e30
_LN2 = 0.6931471805599453
QK_LOGIT_SCALE = QK_DIM ** -0.5 / _LN2
ADAM_LR = 0.001
ADAM_B1 = 0.9
ADAM_B2 = 0.999
ADAM_EPS = 1e-08
ADAM_WD = 0.01
ADAM_STEP = 10

ROW_TILE = 256
DEAD = ROW_TILE - N_META
CONV_HALO = 32
FFN_HALO = 16
assert FFN_HALO == N_META
VMEM_LIMIT = 56 * 1024 * 1024
_LANES = 128

MESH = pl.DeviceIdType.MESH


def _dot(a, b):
    return jnp.dot(a, b, preferred_element_type=F32)


def _dot_nt(a, b):
    return lax.dot_general(a, b, (((1,), (1,)), ((), ())), preferred_element_type=F32)


def _dot_tn(a, b):
    return lax.dot_general(a, b, (((0,), (0,)), ((), ())), preferred_element_type=F32)


def _sigmoid(x):
    return 1.0 / (1.0 + jnp.exp2(x * (-1.0 / _LN2)))


def _rms_fwd(x, g):
    r = lax.rsqrt(jnp.mean(x * x, axis=-1, keepdims=True) + EPS)
    return x * r * g


def _rms_bwd(dy, x, g):
    r = lax.rsqrt(jnp.mean(x * x, axis=-1, keepdims=True) + EPS)
    w = dy * g
    dx = r * w - x * (r * r * r) * jnp.mean(w * x, axis=-1, keepdims=True)
    return dx, jnp.sum(dy * x * r, axis=0, keepdims=True)


def _rope(x, cos, sin):
    half = QK_ROPE // 2
    x1, x2 = x[:, :half], x[:, half:]
    return jnp.concatenate([x1 * cos - x2 * sin, x2 * cos + x1 * sin], axis=-1)


def _rope_t(dy, cos, sin):
    half = QK_ROPE // 2
    d1, d2 = dy[:, :half], dy[:, half:]
    return jnp.concatenate([d1 * cos + d2 * sin, d2 * cos - d1 * sin], axis=-1)


def _row_ids(i, rows):
    return i * rows + lax.broadcasted_iota(jnp.int32, (rows, 1), 0)


def _accumulate(ref, first, value):
    @pl.when(first)
    def _():
        ref[...] = value

    @pl.when(jnp.logical_not(first))
    def _():
        ref[...] += value


def _tile_spec(shape):
    nd = len(shape)
    if nd == 2:
        return pl.BlockSpec((ROW_TILE, shape[1]), lambda i: (i, 0))
    return pl.BlockSpec((shape[0], ROW_TILE, shape[2]), lambda i: (0, i, 0))


def _whole_spec(shape):
    nd = len(shape)
    return pl.BlockSpec(tuple(shape), lambda i: (0,) * nd, pipeline_mode=pl.Buffered(1))


def _acc_spec(shape):
    nd = len(shape)
    return pl.BlockSpec(tuple(shape), lambda i: (0,) * nd)


def _real_spec(width):
    return pl.BlockSpec((ROW_TILE, width), lambda i: (jnp.maximum(i - 1, 0), 0))


def _params(*semantics):
    return pltpu.CompilerParams(dimension_semantics=semantics, vmem_limit_bytes=VMEM_LIMIT)


def _fwd_in(x, meta_pad, g1, w_in, n_rows):
    nt = n_rows // ROW_TILE

    def body(x_ref, meta_ref, g_ref, w_ref, nb_ref, ag_ref, cq_ref, ckv_ref, kr_ref):
        i = pl.program_id(0)
        h0 = jnp.where(i == 0, meta_ref[...], x_ref[...])
        nb = _rms_fwd(h0, g_ref[...]).astype(BF16)
        nb_ref[...] = nb
        z = _dot_nt(nb, w_ref[...])
        ag_ref[...] = z[:, :2 * D_CONV]
        cq_ref[...] = z[:, 2 * D_CONV:2 * D_CONV + Q_LORA]
        ckv_ref[...] = z[:, 2 * D_CONV + Q_LORA:2 * D_CONV + Q_LORA + KV_LORA]
        kr_ref[...] = z[:, 2 * D_CONV + Q_LORA + KV_LORA:]

    out_shapes = [
        jax.ShapeDtypeStruct((n_rows, D_MODEL), BF16),
        jax.ShapeDtypeStruct((n_rows, 2 * D_CONV), F32),
        jax.ShapeDtypeStruct((n_rows, Q_LORA), F32),
        jax.ShapeDtypeStruct((n_rows, KV_LORA), F32),
        jax.ShapeDtypeStruct((n_rows, QK_ROPE), F32),
    ]
    return pl.pallas_call(
        body, name="fwd_in", grid=(nt,),
        in_specs=[_real_spec(D_MODEL), _whole_spec(meta_pad.shape), _whole_spec(g1.shape), _whole_spec(w_in.shape)],
        out_specs=[_tile_spec(s.shape) for s in out_shapes],
        out_shape=out_shapes,
        compiler_params=_params("parallel"),
    )(x, meta_pad, g1, w_in)


def _conv_chain(u1, ln_g, ln_b):
    mu = jnp.mean(u1, axis=-1, keepdims=True)
    xc = u1 - mu
    rstd = lax.rsqrt(jnp.mean(xc * xc, axis=-1, keepdims=True) + EPS)
    xh = xc * rstd
    u2 = xh * ln_g + ln_b
    return xh, u2, u2 * _sigmoid(u2), rstd


def _fwd_conv(ag, conv_w, conv_b, ln_g, ln_b, out_g, n_rows):
    nt = n_rows // ROW_TILE

    def body(ag_ref, w_ref, b_ref, lg_ref, lb_ref, og_ref, mix_ref, u1_ref, ext_ref, conv_ref):
        i = pl.program_id(0)

        @pl.when(i == 0)
        def _():
            ext_ref[:, 0:CONV_HALO, :] = jnp.zeros((CONV_PLANES, CONV_HALO, _LANES), F32)

        ag_t = ag_ref[...]
        live = _row_ids(i, ROW_TILE) >= DEAD
        u0 = jnp.where(live, ag_t[:, :D_CONV] * _sigmoid(ag_t[:, D_CONV:]), 0.0)
        _to_planes(ext_ref, (), slice(CONV_HALO, None), u0)
        first = CONV_HALO - (CONV_WIDTH - 1)
        for c in range(CONV_PLANES):
            taps = w_ref[:, c * _LANES:(c + 1) * _LANES]
            for p in range(PHASES):
                acc = jnp.zeros((PHASE_ROWS, _LANES), F32)
                for k in range(CONV_WIDTH):
                    acc = acc + taps[k:k + 1, :] * ext_ref[c, _phase(first + k + p), :]
                conv_ref[c, _phase(p), :] = acc
        ext_ref[:, 0:CONV_HALO, :] = ext_ref[:, ROW_TILE:ROW_TILE + CONV_HALO, :]
        u1 = _from_planes(conv_ref, (), D_CONV) + b_ref[...]
        u1_ref[...] = u1
        _, _, u3, _ = _conv_chain(u1, lg_ref[...], lb_ref[...])
        mix_ref[...] = _rms_fwd(u3, og_ref[...]).astype(BF16)

    out_shapes = [jax.ShapeDtypeStruct((n_rows, D_CONV), BF16), jax.ShapeDtypeStruct((n_rows, D_CONV), F32)]
    small = [conv_w, conv_b, ln_g, ln_b, out_g]
    return pl.pallas_call(
        body, name="fwd_conv", grid=(nt,),
        in_specs=[_tile_spec(ag.shape)] + [_whole_spec(a.shape) for a in small],
        out_specs=[_tile_spec(s.shape) for s in out_shapes],
        out_shape=out_shapes,
        scratch_shapes=[pltpu.VMEM((CONV_PLANES, ROW_TILE + CONV_HALO, _LANES), F32),
                        pltpu.VMEM((CONV_PLANES, ROW_TILE, _LANES), F32)],
        compiler_params=_params("arbitrary"),
    )(ag, *small)


def _lane_tile(shape):
    if len(shape) == 2:
        return pl.BlockSpec((shape[0], ROW_TILE), lambda i: (0, i))
    return pl.BlockSpec((shape[0], shape[1], ROW_TILE), lambda i: (0, 0, i))


def _rope_rows(x, cos, sin):
    half = QK_ROPE // 2
    x1, x2 = x[:half], x[half:]
    return jnp.concatenate([x1 * cos - x2 * sin, x2 * cos + x1 * sin], axis=0)


def _rope_rows_t(dy, cos, sin):
    half = QK_ROPE // 2
    d1, d2 = dy[:half], dy[half:]
    return jnp.concatenate([d1 * cos + d2 * sin, d2 * cos - d1 * sin], axis=0)


def _fwd_qkv(cq, ckv, kr, gq, gkv, wq_t, w_ukv, wv_t, cos, sin, cos_t, sin_t, n_rows):
    nt = n_rows // ROW_TILE

    def body(cq_ref, ckv_ref, kr_ref, gq_ref, gkv_ref, wqt_ref, wkv_ref, wvt_ref, cos_ref, sin_ref, cost_ref, sint_ref,
             qt_ref, k_ref, v_ref, vt_ref, cqn_ref, ckvn_ref):
        cqn = _rms_fwd(cq_ref[...], gq_ref[...]).astype(BF16)
        ckvn = _rms_fwd(ckv_ref[...], gkv_ref[...]).astype(BF16)
        cqn_ref[...] = cqn
        ckvn_ref[...] = ckvn
        k_rot = _rope(kr_ref[...], cos_ref[...], sin_ref[...])
        cos_rows, sin_rows = cost_ref[...], sint_ref[...]
        q_all = _dot_nt(wqt_ref[...].reshape(N_HEADS * QK_DIM, Q_LORA), cqn)
        vt_all = _dot_nt(wvt_ref[...].reshape(N_HEADS * V_HEAD, KV_LORA), ckvn).astype(BF16)
        for h in range(N_HEADS):
            q_raw = q_all[h * QK_DIM:(h + 1) * QK_DIM]
            q_h = jnp.concatenate([q_raw[:QK_NOPE], _rope_rows(q_raw[QK_NOPE:], cos_rows, sin_rows)], axis=0)
            qt_ref[h] = (q_h * QK_LOGIT_SCALE).astype(BF16)
            kv = _dot(ckvn, wkv_ref[h])
            k_ref[h] = jnp.concatenate([kv[:, :QK_NOPE], k_rot], axis=-1).astype(BF16)
            v_ref[h] = kv[:, QK_NOPE:].astype(BF16)
            vt_ref[h] = vt_all[h * V_HEAD:(h + 1) * V_HEAD]

    out_shapes = [
        jax.ShapeDtypeStruct((N_HEADS, QK_DIM, n_rows), BF16),
        jax.ShapeDtypeStruct((N_HEADS, n_rows, QK_DIM), BF16),
        jax.ShapeDtypeStruct((N_HEADS, n_rows, V_HEAD), BF16),
        jax.ShapeDtypeStruct((N_HEADS, V_HEAD, n_rows), BF16),
        jax.ShapeDtypeStruct((n_rows, Q_LORA), BF16),
        jax.ShapeDtypeStruct((n_rows, KV_LORA), BF16),
    ]
    tiles = [cq, ckv, kr]
    whole = [gq, gkv, wq_t, w_ukv, wv_t]
    out_specs = [_lane_tile(out_shapes[0].shape), _tile_spec(out_shapes[1].shape), _tile_spec(out_shapes[2].shape),
                 _lane_tile(out_shapes[3].shape), _tile_spec(out_shapes[4].shape), _tile_spec(out_shapes[5].shape)]
    return pl.pallas_call(
        body, name="fwd_qkv", grid=(nt,),
        in_specs=[_tile_spec(a.shape) for a in tiles] + [_whole_spec(a.shape) for a in whole]
        + [_tile_spec(cos.shape), _tile_spec(sin.shape), _lane_tile(cos_t.shape), _lane_tile(sin_t.shape)],
        out_specs=out_specs,
        out_shape=out_shapes,
        compiler_params=_params("parallel"),
    )(*tiles, *whole, cos, sin, cos_t, sin_t)


def _chunk_of(rows):
    return jnp.where(rows >= ROW_TILE, lax.shift_right_arithmetic(rows - ROW_TILE, CHUNK_SHIFT) + 1, 0)


def _visible(i, j):
    k_rows = j * ROW_TILE + lax.broadcasted_iota(jnp.int32, (ROW_TILE, 1), 0)
    q_rows = i * ROW_TILE + lax.broadcasted_iota(jnp.int32, (1, ROW_TILE), 1)
    return jnp.logical_and(_chunk_of(q_rows) >= _chunk_of(k_rows), k_rows >= DEAD)


def _attn_fwd(q_t, k, v_t, n_rows):
    nt = n_rows // ROW_TILE

    def body(qt_ref, k_ref, vt_ref, ot_ref, lse_ref, max_ref, sum_ref):
        i = pl.program_id(0)
        q_ts = [qt_ref[h] for h in range(N_HEADS)]

        def key_rows(j):
            return pl.ds(pl.multiple_of(j * ROW_TILE, ROW_TILE), ROW_TILE)

        def make_step(masked, tiles, first=0):
            def step(t, carry):
                js = [first + tiles * t + u for u in range(tiles)]
                scores = [[_dot(k_ref[h, key_rows(j), :], q_ts[h]) for h in range(N_HEADS)] for j in js]
                for j, tile_scores in zip(js, scores):
                    visible = _visible(i, j) if masked else None
                    probs, alphas = [], []
                    for h in range(N_HEADS):
                        m = max_ref[h]
                        s = jnp.where(visible, tile_scores[h], NEG) if masked else tile_scores[h]
                        m_new = jnp.maximum(m, jnp.max(s, axis=0, keepdims=True))
                        alpha = jnp.exp2(m - m_new)
                        p = jnp.exp2(s - m_new)
                        probs.append(p.astype(BF16))
                        alphas.append(alpha)
                        max_ref[h] = m_new
                        sum_ref[h] = alpha * sum_ref[h] + jnp.sum(p, axis=0, keepdims=True)
                    for h in range(N_HEADS):
                        ot_ref[h] = alphas[h] * ot_ref[h] + _dot(vt_ref[h, :, key_rows(j)], probs[h])
                return carry
            return step

        max_ref[...] = jnp.full(max_ref.shape, NEG, F32)
        sum_ref[...] = jnp.zeros_like(sum_ref)
        ot_ref[...] = jnp.zeros_like(ot_ref)
        between = jnp.maximum(i - 1, 0)
        quads = lax.shift_right_logical(between, 2)
        pairs = jnp.bitwise_and(lax.shift_right_logical(between, 1), 1)
        make_step(True, 1)(0, 0)
        lax.fori_loop(0, quads, make_step(False, 4, first=1), 0)
        lax.fori_loop(0, pairs, make_step(False, 2, first=1 + 4 * quads), 0)
        lax.fori_loop(1 + 4 * quads + 2 * pairs, i, make_step(False, 1), 0)
        lax.fori_loop(jnp.maximum(i, 1), i + 1, make_step(True, 1), 0)
        for h in range(N_HEADS):
            l = sum_ref[h]
            ot_ref[h] = ot_ref[h] / l
            lse_ref[h] = max_ref[h] + jnp.log2(l)

    out_shapes = [jax.ShapeDtypeStruct((N_HEADS, V_HEAD, n_rows), F32), jax.ShapeDtypeStruct((N_HEADS, 1, n_rows), F32)]
    return pl.pallas_call(
        body, name="attn_fwd", grid=(nt,),
        in_specs=[_lane_tile(q_t.shape), _whole_spec(k.shape), _whole_spec(v_t.shape)],
        out_specs=[_lane_tile(s.shape) for s in out_shapes],
        out_shape=out_shapes,
        scratch_shapes=[pltpu.VMEM((N_HEADS, 1, ROW_TILE), F32), pltpu.VMEM((N_HEADS, 1, ROW_TILE), F32)],
        compiler_params=_params("parallel"),
    )(q_t, k, v_t)


def _heads_to_rows(ref):
    return jnp.concatenate([ref[h] for h in range(N_HEADS)], axis=0)


def _rms_cols(x, g_col):
    r = lax.rsqrt(jnp.mean(x * x, axis=0, keepdims=True) + EPS)
    return x * r * g_col


def _fwd_out(x, meta_pad, mix_a, o_t, gb_col, w_out, n_rows):
    nt = n_rows // ROW_TILE

    def body(x_ref, meta_ref, mixa_ref, ot_ref, gb_ref, w_ref, mixbt_ref, h1_ref):
        i = pl.program_id(0)
        h0 = jnp.where(i == 0, meta_ref[...], x_ref[...])
        mix_bt = _rms_cols(_heads_to_rows(ot_ref), gb_ref[...]).astype(BF16)
        mixbt_ref[...] = mix_bt
        h1_ref[...] = h0 + _dot(mixa_ref[...], w_ref[:D_CONV, :]) + _dot_tn(mix_bt, w_ref[D_CONV:, :])

    out_shapes = [jax.ShapeDtypeStruct((D_ATTN, n_rows), BF16), jax.ShapeDtypeStruct((n_rows, D_MODEL), F32)]
    return pl.pallas_call(
        body, name="fwd_out", grid=(nt,),
        in_specs=[_real_spec(D_MODEL), _whole_spec(meta_pad.shape), _tile_spec(mix_a.shape), _lane_tile(o_t.shape),
                  _whole_spec(gb_col.shape), _whole_spec(w_out.shape)],
        out_specs=[_lane_tile(out_shapes[0].shape), _tile_spec(out_shapes[1].shape)],
        out_shape=out_shapes,
        compiler_params=_params("parallel"),
    )(x, meta_pad, mix_a, o_t, gb_col, w_out)


PHASES = 8
PHASE_ROWS = ROW_TILE // PHASES
UP_PLANES = -(-UP_SLAB // _LANES)
UP_PAD = UP_PLANES * _LANES
CONV_PLANES = D_CONV // _LANES


def _phase(start):
    return pl.ds(start, PHASE_ROWS, stride=PHASES)


def _to_planes(ref, lead, rows, value):
    width = value.shape[-1]
    for c in range(-(-width // _LANES)):
        part = value[:, c * _LANES:min((c + 1) * _LANES, width)]
        if part.shape[-1] < _LANES:
            part = jnp.concatenate([part, jnp.zeros((part.shape[0], _LANES - part.shape[-1]), part.dtype)], axis=-1)
        ref[(*lead, c, rows, slice(None))] = part


def _from_planes(ref, lead, width):
    planes = [ref[(*lead, c)] for c in range(-(-width // _LANES))]
    last = width - (len(planes) - 1) * _LANES
    return jnp.concatenate(planes[:-1] + [planes[-1][:, :last]], axis=-1)


def _fwd_ffn(h1, target, g2, w_up, fw, fb, w_down, gf, n_rows):
    nt = n_rows // ROW_TILE

    def body(h1_ref, t_ref, g2_ref, wup_ref, fw_ref, fb_ref, wdn_ref, gf_ref,
             n2_ref, up0_ref, act_ref, da_ref, db_ref, dh2_ref, loss_ref, dgf_ref, ext_ref):
        i = pl.program_id(0)

        @pl.when(i == 0)
        def _():
            n2_m = _rms_fwd(h1_ref[DEAD:, :], g2_ref[...]).astype(BF16)
            n2_ref[0:DEAD, :] = jnp.zeros((DEAD, D_MODEL), BF16)
            n2_ref[DEAD:, :] = n2_m
            for s in range(N_DEV):
                up0_m = _dot_nt(n2_m, wup_ref[s])
                up0_ref[s, 0:DEAD, :] = jnp.zeros((DEAD, UP_SLAB), BF16)
                up0_ref[s, DEAD:, :] = up0_m.astype(BF16)
                ext_ref[s, 0:FFN_HALO, :] = up0_m
            act_ref[...] = jnp.zeros_like(act_ref)
            da_ref[...] = jnp.zeros_like(da_ref)
            db_ref[...] = jnp.zeros_like(db_ref)
            dh2_ref[...] = jnp.zeros_like(dh2_ref)
            loss_ref[...] = jnp.zeros_like(loss_ref)
            dgf_ref[...] = jnp.zeros_like(dgf_ref)

        @pl.when(i > 0)
        def _():
            h1_t = h1_ref[...]
            n2 = _rms_fwd(h1_t, g2_ref[...]).astype(BF16)
            n2_ref[...] = n2
            for s in range(N_DEV):
                up0 = _dot_nt(n2, wup_ref[s])
                up0_ref[s] = up0.astype(BF16)
                ext_ref[s, FFN_HALO:, :] = up0

            def conv(s):
                block = ext_ref[s]
                acc = fb_ref[s, :, :UP_SLAB] + fw_ref[s, FFN_CONV_WIDTH - 1:FFN_CONV_WIDTH, :UP_SLAB] * block[FFN_HALO:]
                for back in range(1, FFN_CONV_WIDTH):
                    k = FFN_CONV_WIDTH - 1 - back
                    acc = acc + fw_ref[s, k:k + 1, :UP_SLAB] * pltpu.roll(block, back, 0)[FFN_HALO:]
                return acc

            h2 = h1_t
            for s in range(N_ACT_SLAB):
                gate = conv(s)
                val = conv(s + N_ACT_SLAB)
                sg = _sigmoid(gate)
                silu = gate * sg
                act = (silu * val).astype(BF16)
                act_ref[s] = act
                da_ref[s] = (val * sg * (1.0 + gate * (1.0 - sg))).astype(BF16)
                db_ref[s] = silu.astype(BF16)
                h2 = h2 + _dot(act, wdn_ref[s])
            ext_ref[:, 0:FFN_HALO, :] = ext_ref[:, ROW_TILE:ROW_TILE + FFN_HALO, :]

            gf_t = gf_ref[...]
            diff = _rms_fwd(h2, gf_t) - t_ref[...]
            tile_loss = 0.5 * jnp.sum(jnp.sum(diff * diff, axis=-1, keepdims=True), axis=0, keepdims=True) / D_MODEL
            dh2, dgf = _rms_bwd(diff / D_MODEL, h2, gf_t)
            dh2_ref[...] = dh2
            loss_ref[...] += jnp.broadcast_to(tile_loss, loss_ref.shape)
            dgf_ref[...] += dgf

    act_like = jax.ShapeDtypeStruct((N_ACT_SLAB, n_rows, UP_SLAB), BF16)
    out_shapes = [
        jax.ShapeDtypeStruct((n_rows, D_MODEL), BF16),
        jax.ShapeDtypeStruct((N_DEV, n_rows, UP_SLAB), BF16),
        act_like, act_like, act_like,
        jax.ShapeDtypeStruct((n_rows, D_MODEL), F32),
        jax.ShapeDtypeStruct((8, 128), F32),
        jax.ShapeDtypeStruct((1, D_MODEL), F32),
    ]
    whole = [g2, w_up, fw, fb, w_down, gf]
    return pl.pallas_call(
        body, name="fwd_ffn", grid=(nt,),
        in_specs=[_tile_spec(h1.shape), _real_spec(D_MODEL)] + [_whole_spec(a.shape) for a in whole],
        out_specs=[_tile_spec(s.shape) for s in out_shapes[:6]] + [_acc_spec(s.shape) for s in out_shapes[6:]],
        out_shape=out_shapes,
        scratch_shapes=[pltpu.VMEM((N_DEV, ROW_TILE + FFN_HALO, UP_SLAB), F32)],
        compiler_params=_params("arbitrary"),
    )(h1, target, *whole)


def _rope_tables(n_rows):
    pos = jnp.maximum(jnp.arange(n_rows, dtype=jnp.int32) - DEAD, 0)
    inv_freq = 1.0 / (ROPE_THETA ** (jnp.arange(0, QK_ROPE, 2, dtype=F32) / QK_ROPE))
    ang_t = inv_freq[:, None] * pos.astype(F32)[None, :]
    return jnp.cos(ang_t), jnp.sin(ang_t)


def _halo_after(shape, halo, n_rows):
    last = n_rows // halo - 1
    step = ROW_TILE // halo
    if len(shape) == 2:
        return pl.BlockSpec((halo, shape[1]), lambda i: (jnp.minimum((i + 1) * step, last), 0))
    return pl.BlockSpec((shape[0], halo, shape[2]), lambda i: (0, jnp.minimum((i + 1) * step, last), 0))


def _bwd_ffn_act(dh2, da, db, act, w_down, n_rows):
    nt = n_rows // ROW_TILE

    def body(dh2_ref, da_ref, db_ref, act_ref, wdn_ref, dup_ref, dfb_ref, dwd_ref, acc_ref):
        i = pl.program_id(0)

        @pl.when(i == 0)
        def _():
            dfb_ref[...] = jnp.zeros_like(dfb_ref)
            dup_ref[...] = jnp.zeros_like(dup_ref)
            acc_ref[...] = jnp.zeros_like(acc_ref)

        @pl.when(i > 0)
        def _():
            dh2_b = dh2_ref[...].astype(BF16)
            for s in range(N_ACT_SLAB):
                d_act = _dot_nt(dh2_b, wdn_ref[s])
                d_gate = d_act * da_ref[s].astype(F32)
                d_val = d_act * db_ref[s].astype(F32)
                dup_ref[s] = d_gate.astype(BF16)
                dup_ref[s + N_ACT_SLAB] = d_val.astype(BF16)
                dfb_ref[s] += jnp.sum(d_gate, axis=0, keepdims=True)
                dfb_ref[s + N_ACT_SLAB] += jnp.sum(d_val, axis=0, keepdims=True)
                acc_ref[s] += _dot_tn(act_ref[s], dh2_b)

        @pl.when(i == nt - 1)
        def _():
            dwd_ref[...] = acc_ref[...].astype(BF16)

    out_shapes = [jax.ShapeDtypeStruct((N_DEV, n_rows, UP_SLAB), BF16), jax.ShapeDtypeStruct((N_DEV, 1, UP_SLAB), F32),
                  jax.ShapeDtypeStruct((N_ACT_SLAB, UP_SLAB, D_MODEL), BF16)]
    return pl.pallas_call(
        body, name="bwd_ffn_act", grid=(nt,),
        in_specs=[_tile_spec(dh2.shape), _tile_spec(da.shape), _tile_spec(db.shape), _tile_spec(act.shape),
                  _whole_spec(w_down.shape)],
        out_specs=[_tile_spec(out_shapes[0].shape), _acc_spec(out_shapes[1].shape), _acc_spec(out_shapes[2].shape)],
        out_shape=out_shapes,
        scratch_shapes=[pltpu.VMEM((N_ACT_SLAB, UP_SLAB, D_MODEL), F32)],
        compiler_params=_params("arbitrary"),
    )(dh2, da, db, act, w_down)


def _bwd_ffn_up(dup, up0, h1, dh2, g2, w_up, fw, n_rows):
    nt = n_rows // ROW_TILE
    last_tap = FFN_CONV_WIDTH - 1

    def body(dup_ref, dnext_ref, up0_ref, h1_ref, dh2_ref, g2_ref, wup_ref, fw_ref,
             dup0_ref, dh1_ref, dfw_ref, dg2_ref):
        i = pl.program_id(0)

        def conv_transpose(s, rows, d, after, u):
            block = jnp.concatenate([d, after], axis=0)
            dup0 = fw_ref[s, last_tap:last_tap + 1, :UP_SLAB] * d
            dfw_ref[s, last_tap:last_tap + 1, :UP_SLAB] += jnp.sum(d * u, axis=0, keepdims=True)
            for ahead in range(1, FFN_CONV_WIDTH):
                k = last_tap - ahead
                shifted = pltpu.roll(block, rows + FFN_HALO - ahead, 0)[:rows]
                dup0 = dup0 + fw_ref[s, k:k + 1, :UP_SLAB] * shifted
                dfw_ref[s, k:k + 1, :UP_SLAB] += jnp.sum(shifted * u, axis=0, keepdims=True)
            return dup0.astype(BF16)

        @pl.when(i == 0)
        def _():
            dfw_ref[...] = jnp.zeros_like(dfw_ref)
            dn2_m = jnp.zeros((N_META, D_MODEL), F32)
            for s in range(N_DEV):
                dup0_m = conv_transpose(s, N_META, dup_ref[s, DEAD:, :].astype(F32), dnext_ref[s].astype(F32),
                                        up0_ref[s, DEAD:, :].astype(F32))
                dup0_ref[s, 0:DEAD, :] = jnp.zeros((DEAD, UP_SLAB), BF16)
                dup0_ref[s, DEAD:, :] = dup0_m
                dn2_m = dn2_m + _dot(dup0_m, wup_ref[s])
            dx_m, dg2 = _rms_bwd(dn2_m, h1_ref[DEAD:, :], g2_ref[...])
            dh1_ref[0:DEAD, :] = jnp.zeros((DEAD, D_MODEL), F32)
            dh1_ref[DEAD:, :] = dh2_ref[DEAD:, :] + dx_m
            dg2_ref[...] = dg2

        @pl.when(i > 0)
        def _():
            dn2 = jnp.zeros((ROW_TILE, D_MODEL), F32)
            for s in range(N_DEV):
                after = jnp.where(i == nt - 1, 0.0, dnext_ref[s].astype(F32))
                dup0_b = conv_transpose(s, ROW_TILE, dup_ref[s].astype(F32), after, up0_ref[s].astype(F32))
                dup0_ref[s] = dup0_b
                dn2 = dn2 + _dot(dup0_b, wup_ref[s])
            dx, dg2 = _rms_bwd(dn2, h1_ref[...], g2_ref[...])
            dh1_ref[...] = dh2_ref[...] + dx
            dg2_ref[...] += dg2

    out_shapes = [
        jax.ShapeDtypeStruct((N_DEV, n_rows, UP_SLAB), BF16),
        jax.ShapeDtypeStruct((n_rows, D_MODEL), F32),
        jax.ShapeDtypeStruct((N_DEV, FFN_CONV_WIDTH, UP_PAD), F32),
        jax.ShapeDtypeStruct((1, D_MODEL), F32),
    ]
    return pl.pallas_call(
        body, name="bwd_ffn_up", grid=(nt,),
        in_specs=[_tile_spec(dup.shape), _halo_after(dup.shape, FFN_HALO, n_rows), _tile_spec(up0.shape),
                  _tile_spec(h1.shape), _tile_spec(dh2.shape),
                  _whole_spec(g2.shape), _whole_spec(w_up.shape), _whole_spec(fw.shape)],
        out_specs=[_tile_spec(s.shape) for s in out_shapes[:2]] + [_acc_spec(s.shape) for s in out_shapes[2:]],
        out_shape=out_shapes,
        compiler_params=_params("arbitrary"),
    )(dup, dup, up0, h1, dh2, g2, w_up, fw)


def _bwd_out(dh1, o_t, u1, w_out, gb_col, ln_g, ln_b, ga, n_rows):
    nt = n_rows // ROW_TILE

    def body(dh1_ref, ot_ref, u1_ref, w_ref, gb_ref, lg_ref, lb_ref, ga_ref,
             dot_ref, delta_ref, du1_ref, dgb_ref, dga_ref, dlg_ref, dlb_ref, dcb_ref):
        i = pl.program_id(0)
        dh1_b = dh1_ref[...].astype(BF16)
        o_t = _heads_to_rows(ot_ref)
        gb = gb_ref[...]
        r = lax.rsqrt(jnp.mean(o_t * o_t, axis=0, keepdims=True) + EPS)
        dmix_bt = _dot_nt(w_ref[D_CONV:, :], dh1_b)
        wgt = dmix_bt * gb
        do_t = r * wgt - o_t * (r * r * r) * jnp.mean(wgt * o_t, axis=0, keepdims=True)
        dgb = jnp.sum(dmix_bt * o_t * r, axis=1, keepdims=True)
        for h in range(N_HEADS):
            do_h = do_t[h * V_HEAD:(h + 1) * V_HEAD]
            dot_ref[h] = do_h.astype(BF16)
            delta_ref[h] = jnp.sum(do_h * ot_ref[h], axis=0, keepdims=True)
        lg = lg_ref[...]
        xh, u2, u3, rstd = _conv_chain(u1_ref[...], lg, lb_ref[...])
        du3, dga = _rms_bwd(_dot_nt(dh1_b, w_ref[:D_CONV, :]), u3, ga_ref[...])
        sg = _sigmoid(u2)
        du2 = du3 * sg * (1.0 + u2 * (1.0 - sg))
        dxh = du2 * lg
        du1 = rstd * (dxh - jnp.mean(dxh, axis=-1, keepdims=True) - xh * jnp.mean(dxh * xh, axis=-1, keepdims=True))
        du1_ref[...] = du1
        first = i == 0
        _accumulate(dgb_ref, first, dgb)
        _accumulate(dga_ref, first, dga)
        _accumulate(dlg_ref, first, jnp.sum(du2 * xh, axis=0, keepdims=True))
        _accumulate(dlb_ref, first, jnp.sum(du2, axis=0, keepdims=True))
        _accumulate(dcb_ref, first, jnp.sum(du1, axis=0, keepdims=True))

    out_shapes = [
        jax.ShapeDtypeStruct((N_HEADS, V_HEAD, n_rows), BF16),
        jax.ShapeDtypeStruct((N_HEADS, 1, n_rows), F32),
        jax.ShapeDtypeStruct((n_rows, D_CONV), F32),
        jax.ShapeDtypeStruct((D_ATTN, 1), F32),
    ] + [jax.ShapeDtypeStruct((1, D_CONV), F32)] * 4
    whole = [w_out, gb_col, ln_g, ln_b, ga]
    return pl.pallas_call(
        body, name="bwd_out", grid=(nt,),
        in_specs=[_tile_spec(dh1.shape), _lane_tile(o_t.shape), _tile_spec(u1.shape)] + [_whole_spec(a.shape) for a in whole],
        out_specs=[_lane_tile(out_shapes[0].shape), _lane_tile(out_shapes[1].shape), _tile_spec(out_shapes[2].shape)]
        + [_acc_spec(s.shape) for s in out_shapes[3:]],
        out_shape=out_shapes,
        compiler_params=_params("arbitrary"),
    )(dh1, o_t, u1, *whole)


ATTN_BWD_HEADS = 8


def _attn_bwd(q_t, k, v, do_t, lse, delta, n_rows):
    nt = n_rows // ROW_TILE
    hp = ATTN_BWD_HEADS

    def body(k_ref, v_ref, qt_ref, dot_ref, lse_ref, delta_ref, dqt_ref, dk_ref, dv_ref):
        j = pl.program_id(1)

        @pl.when(j == 0)
        def _():
            dqt_ref[...] = jnp.zeros_like(dqt_ref)

        k_ts = [k_ref[h] for h in range(hp)]
        v_ts = [v_ref[h] for h in range(hp)]

        def make_step(masked, tiles, first=0):
            def step(t, carry):
                tiles_of_step = []
                for u in range(tiles):
                    i = first + tiles * t + u
                    cols = pl.ds(pl.multiple_of(i * ROW_TILE, ROW_TILE), ROW_TILE)
                    q_is = [qt_ref[h, :, cols] for h in range(hp)]
                    do_is = [dot_ref[h, :, cols] for h in range(hp)]
                    scores = [_dot(k_ts[h], q_is[h]) for h in range(hp)]
                    dps = [_dot(v_ts[h], do_is[h]) for h in range(hp)]
                    tiles_of_step.append((i, cols, q_is, do_is, scores, dps))
                for i, cols, q_is, do_is, scores, dps in tiles_of_step:
                    visible = _visible(i, j) if masked else None
                    probs, dss = [], []
                    for h in range(hp):
                        s = jnp.where(visible, scores[h], NEG) if masked else scores[h]
                        p = jnp.exp2(s - lse_ref[h, :, cols])
                        probs.append(p.astype(BF16))
                        dss.append((p * (dps[h] - delta_ref[h, :, cols])).astype(BF16))
                    for h in range(hp):
                        dv_ref[h] += _dot_nt(probs[h], do_is[h])
                        dk_ref[h] += _dot_nt(dss[h], q_is[h])
                        dqt_ref[h, :, cols] += _dot_tn(k_ts[h], dss[h])
                return carry
            return step

        dk_ref[...] = jnp.zeros_like(dk_ref)
        dv_ref[...] = jnp.zeros_like(dv_ref)
        make_step(True, 1)(j, 0)
        lax.fori_loop(jnp.where(j == 0, j + 1, nt), nt, make_step(True, 1), 0)
        unmasked = jnp.where(j == 0, 0, nt - 1 - j)
        quads = lax.shift_right_logical(unmasked, 2)
        pairs = jnp.bitwise_and(lax.shift_right_logical(unmasked, 1), 1)
        lax.fori_loop(0, quads, make_step(False, 4, first=j + 1), 0)
        lax.fori_loop(0, pairs, make_step(False, 2, first=j + 1 + 4 * quads), 0)
        lax.fori_loop(jnp.where(j == 0, nt, j + 1 + 4 * quads + 2 * pairs), nt, make_step(False, 1), 0)
        dk_ref[...] = dk_ref[...] * _LN2

    key_tile = lambda w: pl.BlockSpec((hp, ROW_TILE, w), lambda g, j: (g, j, 0))
    all_cols = lambda w: pl.BlockSpec((hp, w, n_rows), lambda g, j: (g, 0, 0))
    resident = lambda w: pl.BlockSpec((hp, w, n_rows), lambda g, j: (g, 0, 0), pipeline_mode=pl.Buffered(1))
    out_shapes = [
        jax.ShapeDtypeStruct((N_HEADS, QK_DIM, n_rows), F32),
        jax.ShapeDtypeStruct((N_HEADS, n_rows, QK_DIM), F32),
        jax.ShapeDtypeStruct((N_HEADS, n_rows, V_HEAD), F32),
    ]
    return pl.pallas_call(
        body, name="attn_bwd", grid=(N_HEADS // hp, nt),
        in_specs=[key_tile(QK_DIM), key_tile(V_HEAD), resident(QK_DIM), resident(V_HEAD), resident(1), resident(1)],
        out_specs=[all_cols(QK_DIM), key_tile(QK_DIM), key_tile(V_HEAD)],
        out_shape=out_shapes,
        compiler_params=_params("parallel", "arbitrary"),
    )(k, v, q_t, do_t, lse, delta)


def _bwd_qkv(dq_t, dk, dv, cq, ckv, gq, gkv, wq_t, w_ukv, cos, sin, cos_t, sin_t, n_rows):
    nt = n_rows // ROW_TILE

    def body(dqt_ref, dk_ref, dv_ref, cq_ref, ckv_ref, gq_ref, gkv_ref, wqt_ref, wkv_ref, cos_ref, sin_ref,
             cost_ref, sint_ref, dqraw_ref, dkv_ref, dcq_ref, dckv_ref, dkr_ref, dgq_ref, dgkv_ref):
        i = pl.program_id(0)
        cos_rows, sin_rows = cost_ref[...], sint_ref[...]
        dcqn = jnp.zeros((ROW_TILE, Q_LORA), F32)
        dckvn = jnp.zeros((ROW_TILE, KV_LORA), F32)
        dk_rot = jnp.zeros((ROW_TILE, QK_ROPE), F32)
        for h in range(N_HEADS):
            dq_h, dk_h = dqt_ref[h] * QK_DIM ** -0.5, dk_ref[h]
            dq_raw = jnp.concatenate(
                [dq_h[:QK_NOPE], _rope_rows_t(dq_h[QK_NOPE:], cos_rows, sin_rows)], axis=0).astype(BF16)
            dqraw_ref[h] = dq_raw
            dcqn = dcqn + _dot_tn(dq_raw, wqt_ref[h])
            dkv = jnp.concatenate([dk_h[:, :QK_NOPE], dv_ref[h]], axis=-1).astype(BF16)
            dkv_ref[:, h * KV_HEAD:(h + 1) * KV_HEAD] = dkv
            dckvn = dckvn + _dot_nt(dkv, wkv_ref[h])
            dk_rot = dk_rot + dk_h[:, QK_NOPE:]
        dkr_ref[...] = _rope_t(dk_rot, cos_ref[...], sin_ref[...]).astype(BF16)
        dcq, dgq = _rms_bwd(dcqn, cq_ref[...], gq_ref[...])
        dckv, dgkv = _rms_bwd(dckvn, ckv_ref[...], gkv_ref[...])
        dcq_ref[...] = dcq.astype(BF16)
        dckv_ref[...] = dckv.astype(BF16)
        _accumulate(dgq_ref, i == 0, dgq)
        _accumulate(dgkv_ref, i == 0, dgkv)

    out_shapes = [
        jax.ShapeDtypeStruct((N_HEADS, QK_DIM, n_rows), BF16),
        jax.ShapeDtypeStruct((n_rows, N_HEADS * KV_HEAD), BF16),
        jax.ShapeDtypeStruct((n_rows, Q_LORA), BF16),
        jax.ShapeDtypeStruct((n_rows, KV_LORA), BF16),
        jax.ShapeDtypeStruct((n_rows, QK_ROPE), BF16),
        jax.ShapeDtypeStruct((1, Q_LORA), F32),
        jax.ShapeDtypeStruct((1, KV_LORA), F32),
    ]
    tiles = [dk, dv, cq, ckv]
    whole = [gq, gkv, wq_t, w_ukv]
    return pl.pallas_call(
        body, name="bwd_qkv", grid=(nt,),
        in_specs=[_lane_tile(dq_t.shape)] + [_tile_spec(a.shape) for a in tiles] + [_whole_spec(a.shape) for a in whole]
        + [_tile_spec(cos.shape), _tile_spec(sin.shape), _lane_tile(cos_t.shape), _lane_tile(sin_t.shape)],
        out_specs=[_lane_tile(out_shapes[0].shape)] + [_tile_spec(s.shape) for s in out_shapes[1:5]]
        + [_acc_spec(s.shape) for s in out_shapes[5:]],
        out_shape=out_shapes,
        compiler_params=_params("arbitrary"),
    )(dq_t, *tiles, *whole, cos, sin, cos_t, sin_t)


def _bwd_conv(du1, ag, conv_w, dcq, dckv, dkr, n_rows):
    nt = n_rows // ROW_TILE

    last_tap = CONV_WIDTH - 1

    def body(du1_ref, dnext_ref, ag_ref, w_ref, dcq_ref, dckv_ref, dkr_ref, dz_ref, dw_ref,
             dext_ref, uext_ref, conv_ref, sums_ref):
        i = pl.program_id(0)

        @pl.when(i == 0)
        def _():
            sums_ref[...] = jnp.zeros_like(sums_ref)

        _to_planes(dext_ref, (), slice(0, ROW_TILE), du1_ref[...])
        _to_planes(dext_ref, (), slice(ROW_TILE, None), jnp.where(i == nt - 1, 0.0, dnext_ref[...]))
        ag_t = ag_ref[...]
        live = _row_ids(i, ROW_TILE) >= DEAD
        sg = _sigmoid(ag_t[:, D_CONV:])
        _to_planes(uext_ref, (), slice(None), jnp.where(live, ag_t[:, :D_CONV] * sg, 0.0))
        for c in range(CONV_PLANES):
            taps = w_ref[:, c * _LANES:(c + 1) * _LANES]
            for half in range(0, PHASES, PHASES // 2):
                phases = range(half, half + PHASES // 2)
                us = {p: uext_ref[c, _phase(p), :] for p in phases}
                accs = {p: jnp.zeros((PHASE_ROWS, _LANES), F32) for p in phases}
                for k in range(CONV_WIDTH):
                    tap_sum = jnp.zeros((PHASE_ROWS, _LANES), F32)
                    for p in phases:
                        shifted = dext_ref[c, _phase(p + last_tap - k), :]
                        accs[p] = accs[p] + taps[k:k + 1, :] * shifted
                        tap_sum = tap_sum + shifted * us[p]
                    sums_ref[c, k] += tap_sum
                for p in phases:
                    conv_ref[c, _phase(p), :] = accs[p]
        du0 = jnp.where(live, _from_planes(conv_ref, (), D_CONV), 0.0)
        da = du0 * sg
        dgate = du0 * ag_t[:, :D_CONV] * sg * (1.0 - sg)
        dz_ref[...] = jnp.concatenate(
            [da.astype(BF16), dgate.astype(BF16), dcq_ref[...], dckv_ref[...], dkr_ref[...]], axis=-1)

        @pl.when(i == nt - 1)
        def _():
            for c in range(CONV_PLANES):
                for k in range(CONV_WIDTH):
                    dw_ref[k:k + 1, c * _LANES:(c + 1) * _LANES] = jnp.sum(sums_ref[c, k], axis=0, keepdims=True)

    out_shapes = [jax.ShapeDtypeStruct((n_rows, D_IN), BF16), jax.ShapeDtypeStruct((CONV_WIDTH, D_CONV), F32)]
    return pl.pallas_call(
        body, name="bwd_conv", grid=(nt,),
        in_specs=[_tile_spec(du1.shape), _halo_after(du1.shape, CONV_HALO, n_rows), _tile_spec(ag.shape),
                  _whole_spec(conv_w.shape), _tile_spec(dcq.shape), _tile_spec(dckv.shape), _tile_spec(dkr.shape)],
        out_specs=[_tile_spec(out_shapes[0].shape), _acc_spec(out_shapes[1].shape)],
        out_shape=out_shapes,
        scratch_shapes=[pltpu.VMEM((CONV_PLANES, ROW_TILE + CONV_HALO, _LANES), F32),
                        pltpu.VMEM((CONV_PLANES, ROW_TILE, _LANES), F32), pltpu.VMEM((CONV_PLANES, ROW_TILE, _LANES), F32),
                        pltpu.VMEM((CONV_PLANES, CONV_WIDTH, PHASE_ROWS, _LANES), F32)],
        compiler_params=_params("arbitrary"),
    )(du1, du1, ag, conv_w, dcq, dckv, dkr)


def _bwd_in(dz, x, meta_pad, dh1, g1, w_in, n_rows):
    nt = n_rows // ROW_TILE

    def body(dz_ref, x_ref, meta_ref, dh1_ref, g_ref, w_ref, gx_ref, gmeta_ref, dg1_ref):
        i = pl.program_id(0)
        h0 = jnp.where(i == 0, meta_ref[...], x_ref[...])
        dx, dg1 = _rms_bwd(_dot(dz_ref[...], w_ref[...]), h0, g_ref[...])
        dh0 = dh1_ref[...] + dx
        gx_ref[...] = dh0

        @pl.when(i == 0)
        def _():
            gmeta_ref[...] = dh0

        _accumulate(dg1_ref, i == 0, dg1)

    out_shapes = [
        jax.ShapeDtypeStruct((n_rows - ROW_TILE, D_MODEL), F32),
        jax.ShapeDtypeStruct((ROW_TILE, D_MODEL), F32),
        jax.ShapeDtypeStruct((1, D_MODEL), F32),
    ]
    return pl.pallas_call(
        body, name="bwd_in", grid=(nt,),
        in_specs=[_tile_spec(dz.shape), _real_spec(D_MODEL), _whole_spec(meta_pad.shape), _tile_spec(dh1.shape),
                  _whole_spec(g1.shape), _whole_spec(w_in.shape)],
        out_specs=[_real_spec(D_MODEL), _acc_spec(out_shapes[1].shape), _acc_spec(out_shapes[2].shape)],
        out_shape=out_shapes,
        compiler_params=_params("arbitrary"),
    )(dz, x, meta_pad, dh1, g1, w_in)


def _contraction_tile(n_rows):
    return next(t for t in range(n_rows // 2 // _LANES * _LANES, 0, -_LANES) if n_rows % t == 0)


def _weight_grad(a, b, name, a_transposed=False):
    groups = max(a.shape[0] if a.ndim == 3 else 1, b.shape[0] if b.ndim == 3 else 1)
    n_rows, n = b.shape[-2], b.shape[-1]
    m = a.shape[-2] if a_transposed else a.shape[-1]
    kt = _contraction_tile(n_rows)
    steps = n_rows // kt

    def body(a_ref, b_ref, out_ref, acc_ref):
        i = pl.program_id(1)
        a_t, b_t = a_ref[...].astype(BF16), b_ref[...].astype(BF16)
        part = _dot(a_t, b_t) if a_transposed else _dot_tn(a_t, b_t)
        _accumulate(acc_ref, i == 0, part)

        @pl.when(i == steps - 1)
        def _():
            out_ref[...] = acc_ref[...].astype(out_ref.dtype)

    def spec(arr, rows_last):
        block = (arr.shape[-2], kt) if rows_last else (kt, arr.shape[-1])
        at = (lambda i: (0, i)) if rows_last else (lambda i: (i, 0))
        if arr.ndim == 3:
            return pl.BlockSpec((None,) + block, lambda g, i: (g,) + at(i))
        return pl.BlockSpec(block, lambda g, i: at(i))

    return pl.pallas_call(
        body, name=name, grid=(groups, steps),
        in_specs=[spec(a, a_transposed), spec(b, False)],
        out_specs=pl.BlockSpec((None, m, n), lambda g, i: (g, 0, 0)),
        out_shape=jax.ShapeDtypeStruct((groups, m, n), BF16),
        scratch_shapes=[pltpu.VMEM((m, n), F32)],
        compiler_params=_params("parallel", "arbitrary"),
    )(a, b)


def _my_index():
    return 4 * lax.axis_index("x") + 2 * lax.axis_index("y") + lax.axis_index("c")


def _peer(k):
    flip = lambda v, bit: 1 - v if bit else v
    px = flip(lax.axis_index("x"), k & 4)
    py = flip(lax.axis_index("y"), k & 2)
    pc = flip(lax.axis_index("c"), k & 1)
    return (px, py, pc), 4 * px + 2 * py + pc


def _all_gather(shards, dtypes):
    n = len(shards)
    sibling, chips = 1, (2, 4, 6)

    def body(*refs):
        ins, outs, stages = refs[:n], refs[n:2 * n], refs[2 * n:3 * n]
        send_sems, recv_sems, local_sems = refs[3 * n:]
        me = _my_index()
        for a in range(n):
            stages[a][...] = ins[a][...].astype(stages[a].dtype)
        local = [pltpu.make_async_copy(stages[a], outs[a].at[me], local_sems.at[a]) for a in range(n)]
        for cp in local:
            cp.start()

        def copy(a, k, src, slot, to):
            return pltpu.make_async_remote_copy(
                src_ref=src, dst_ref=outs[a].at[slot], send_sem=send_sems.at[a, k - 1],
                recv_sem=recv_sems.at[a, k - 1], device_id=_peer(to)[0], device_id_type=MESH)

        def own(a, k):
            return copy(a, k, stages[a], me, k)

        def passed(a, k):
            slot = _peer(k)[1]
            return copy(a, k ^ sibling, outs[a].at[slot], slot, sibling)

        def arrival(a, k):
            return copy(a, k, stages[a], _peer(k)[1], k)

        for k in (sibling,) + chips:
            for a in range(n):
                own(a, k).start()
        for k in chips:
            for a in range(n):
                arrival(a, k).wait_recv()
                passed(a, k).start()
        for a in range(n):
            arrival(a, sibling).wait_recv()
            for k in chips:
                arrival(a, k ^ sibling).wait_recv()
        for a in range(n):
            for k in (sibling,) + chips:
                own(a, k).wait_send()
            for k in chips:
                passed(a, k).wait_send()
        for cp in local:
            cp.wait()

    return pl.pallas_call(
        body, name="gather_weights",
        in_specs=[pl.BlockSpec(memory_space=pltpu.VMEM)] * n,
        out_specs=[pl.BlockSpec(memory_space=pl.ANY)] * n,
        out_shape=[jax.ShapeDtypeStruct((N_DEV,) + s.shape, dt) for s, dt in zip(shards, dtypes)],
        scratch_shapes=[pltpu.VMEM(s.shape, dt) for s, dt in zip(shards, dtypes)]
        + [pltpu.SemaphoreType.DMA((n, N_DEV - 1)), pltpu.SemaphoreType.DMA((n, N_DEV - 1)), pltpu.SemaphoreType.DMA((n,))],
        compiler_params=pltpu.CompilerParams(vmem_limit_bytes=VMEM_LIMIT),
    )(*shards)


def _exchange(parts, whole):
    n = len(parts)

    def body(*refs):
        ins, outs = refs[:n], refs[n:2 * n]
        send_sems, recv_sems, local_sems = refs[2 * n:]
        me = _my_index()

        def src(a, slab):
            return ins[a] if whole[a] else ins[a].at[slab]

        local = [pltpu.make_async_copy(src(a, me), outs[a].at[me], local_sems.at[a]) for a in range(n)]
        for cp in local:
            cp.start()

        def copy(a, k, slab, slot):
            peer, _ = _peer(k)
            return pltpu.make_async_remote_copy(
                src_ref=src(a, slab), dst_ref=outs[a].at[slot], send_sem=send_sems.at[a, k - 1],
                recv_sem=recv_sems.at[a, k - 1], device_id=peer, device_id_type=MESH)

        for k in range(1, N_DEV):
            for a in range(n):
                copy(a, k, _peer(k)[1], me).start()
        for k in range(1, N_DEV):
            for a in range(n):
                copy(a, k, _peer(k)[1], _peer(k)[1]).wait()
        for cp in local:
            cp.wait()

    return pl.pallas_call(
        body, name="exchange_grads",
        in_specs=[pl.BlockSpec(memory_space=pl.ANY)] * n,
        out_specs=[pl.BlockSpec(memory_space=pl.ANY)] * n,
        out_shape=[jax.ShapeDtypeStruct(((N_DEV,) + p.shape) if w else p.shape, p.dtype) for p, w in zip(parts, whole)],
        scratch_shapes=[pltpu.SemaphoreType.DMA((n, N_DEV - 1)), pltpu.SemaphoreType.DMA((n, N_DEV - 1)),
                        pltpu.SemaphoreType.DMA((n,))],
    )(*parts)


def _sequencer_exchange(parts, whole, name, collective_id):
    n = len(parts)
    srcs = [jax.new_ref(p, memory_space=pltpu.MemorySpace.HBM) for p in parts]
    lands = [jax.empty_ref(jax.ShapeDtypeStruct(((N_DEV,) + p.shape) if w else p.shape, p.dtype),
                           memory_space=pltpu.MemorySpace.HBM) for p, w in zip(parts, whole)]

    @pl.kernel(mesh=plsc.ScalarSubcoreMesh(axis_name="sequencer", num_cores=1), name=name,
               scratch_types=(pltpu.SemaphoreType.DMA((n, N_DEV - 1)), pltpu.SemaphoreType.DMA((n, N_DEV - 1)),
                              pltpu.SemaphoreType.DMA((n,))),
               compiler_params=pltpu.CompilerParams(collective_id=collective_id))
    def launch(send_sems, recv_sems, local_sems):
        barrier = pltpu.get_barrier_semaphore()
        for k in range(1, N_DEV):
            pl.semaphore_signal(barrier, inc=1, device_id=_peer(k)[0], device_id_type=MESH)
        pl.semaphore_wait(barrier, N_DEV - 1)
        me = _my_index()

        def src(a, slab):
            return srcs[a] if whole[a] else srcs[a].at[slab]

        local = [pltpu.make_async_copy(src(a, me), lands[a].at[me], local_sems.at[a]) for a in range(n)]
        for cp in local:
            cp.start()

        def copy(a, k, slab, slot):
            return pltpu.make_async_remote_copy(
                src_ref=src(a, slab), dst_ref=lands[a].at[slot], send_sem=send_sems.at[a, k - 1],
                recv_sem=recv_sems.at[a, k - 1], device_id=_peer(k)[0], device_id_type=MESH)

        for k in range(1, N_DEV):
            for a in range(n):
                copy(a, k, _peer(k)[1], me).start()
        for k in range(1, N_DEV):
            for a in range(n):
                copy(a, k, _peer(k)[1], _peer(k)[1]).wait()
        for cp in local:
            cp.wait()

    launch()
    return [land[...] for land in lands]


def _row_block(rows):
    if rows <= ROW_TILE:
        return rows
    return next(rb for rb in range(ROW_TILE, 0, -16) if rows % rb == 0)


def _adamw(landing, w, m, v, name):
    rows, cols = w.shape
    rb = _row_block(rows)

    def body(l_ref, w_ref, m_ref, v_ref, g_ref, d_ref, m2_ref, v2_ref):
        g = l_ref[0].astype(F32)
        for p in range(1, N_DEV):
            g = g + l_ref[p].astype(F32)
        g_ref[...] = g
        d_ref[...], m2_ref[...], v2_ref[...] = _adamw_step(g, w_ref[...], m_ref[...], v_ref[...])

    flat = pl.BlockSpec((rb, cols), lambda i: (i, 0))
    return pl.pallas_call(
        body, name=name, grid=(rows // rb,),
        in_specs=[pl.BlockSpec((N_DEV, rb, cols), lambda i: (0, i, 0)), flat, flat, flat],
        out_specs=[flat] * 4,
        out_shape=[jax.ShapeDtypeStruct((rows, cols), F32)] * 4,
        compiler_params=_params("parallel"),
    )(landing, w, m, v)


def _adamw_step(g, w, m, v):
    m2 = ADAM_B1 * m + (1.0 - ADAM_B1) * g
    v2 = ADAM_B2 * v + (1.0 - ADAM_B2) * (g * g)
    m_hat = m2 / (1.0 - ADAM_B1 ** ADAM_STEP)
    v_hat = v2 / (1.0 - ADAM_B2 ** ADAM_STEP)
    return -ADAM_LR * (m_hat / (jnp.sqrt(v_hat) + ADAM_EPS) + ADAM_WD * w), m2, v2


_REPLICATED = (
    ("mix_norm_g", D_MODEL), ("q_norm_g", Q_LORA), ("kv_norm_g", KV_LORA), ("conv_b", D_CONV), ("conv_ln_g", D_CONV),
    ("conv_ln_b", D_CONV), ("conv_out_g", D_CONV), ("attn_out_g", D_CONV), ("ffn_norm_g", D_MODEL),
    ("ffn_conv_b", D_UP), ("final_norm_g", D_MODEL),
)
_REPLICATED_WIDTH = sum(size for _, size in _REPLICATED) + _LANES

_WEIGHT_ORDER = (
    "meta_tokens", "mix_norm_g", "w_in", "q_norm_g", "w_uq", "kv_norm_g", "w_ukv", "conv_w", "conv_b", "conv_ln_g",
    "conv_ln_b", "conv_out_g", "attn_out_g", "w_out", "ffn_norm_g", "w_ffn_up", "ffn_conv_w", "ffn_conv_b",
    "w_ffn_down", "final_norm_g",
)


def _pack_replicated(grads, loss):
    rows = [grads[name].reshape(1, size) for name, size in _REPLICATED]
    return jnp.concatenate(rows + [jnp.broadcast_to(loss.reshape(1, 1), (1, _LANES))], axis=-1)


def _adamw_replicated(landing, weights, moments_m, moments_v):
    n = len(_REPLICATED)

    def body(*refs):
        l_ref, ins, outs = refs[0], refs[1:1 + 3 * n], refs[1 + 3 * n:]
        total = l_ref[0]
        for p in range(1, N_DEV):
            total = total + l_ref[p]
        at = 0
        for a, (_, size) in enumerate(_REPLICATED):
            g = total[:, at:at + size]
            w_ref, m_ref, v_ref = ins[3 * a:3 * a + 3]
            g_ref, d_ref, m2_ref, v2_ref = outs[4 * a:4 * a + 4]
            g_ref[...] = g
            d_ref[...], m2_ref[...], v2_ref[...] = _adamw_step(g, w_ref[...], m_ref[...], v_ref[...])
            at += size
        outs[-1][...] = total[:, at:at + _LANES]

    operands, out_shapes = [], []
    for name, size in _REPLICATED:
        operands += [weights[name].reshape(1, size), moments_m[name].reshape(1, size), moments_v[name].reshape(1, size)]
        out_shapes += [jax.ShapeDtypeStruct((1, size), F32)] * 4
    out_shapes.append(jax.ShapeDtypeStruct((1, _LANES), F32))
    outs = pl.pallas_call(body, name="adamw_replicated", out_shape=out_shapes)(landing, *operands)
    return outs[-1][0, 0], {name: outs[4 * a:4 * a + 4] for a, (name, _) in enumerate(_REPLICATED)}


def _pad_rows(a, rows):
    return jnp.pad(a, ((0, rows - a.shape[0]), (0, 0)))


def _slabs(a):
    r, c = a.shape
    return a.reshape(r, N_DEV, c // N_DEV).transpose(1, 0, 2)


def _unslab(a):
    g, r, c = a.shape
    return a.transpose(1, 0, 2).reshape(r, g * c)


def _local_step(x, target, w, n_rows, ffn_weights, send_grads):
    cos_t, sin_t = lax.optimization_barrier(_rope_tables(n_rows))
    cos, sin = cos_t.T, sin_t.T
    meta_pad, g1, gf = w["meta_pad"], w["mix_norm_g"], w["final_norm_g"]
    gq, gkv, gb_col = w["q_norm_g"], w["kv_norm_g"], w["attn_out_g"].reshape(D_ATTN, 1)
    nb, ag, cq, ckv, kr = _fwd_in(x, meta_pad, g1, w["w_in"], n_rows)
    mix_a, u1 = _fwd_conv(ag, w["conv_w"], w["conv_b"], w["conv_ln_g"], w["conv_ln_b"], w["conv_out_g"], n_rows)
    q_t, k, v, v_t, cqn, ckvn = _fwd_qkv(cq, ckv, kr, gq, gkv, w["wq_t"], w["w_ukv"], w["wv_t"], cos, sin, cos_t, sin_t, n_rows)
    o_t, lse = _attn_fwd(q_t, k, v_t, n_rows)
    w_out, w_up, w_down = ffn_weights()
    mix_bt, h1 = _fwd_out(x, meta_pad, mix_a, o_t, gb_col, w_out, n_rows)
    n2, up0, act, da, db, dh2, loss, dgf = _fwd_ffn(
        h1, target, w["ffn_norm_g"], w_up, w["fw"], w["fb"], w_down, gf, n_rows)

    dup, dfb, grad_w_down = _bwd_ffn_act(dh2, da, db, act, w_down, n_rows)
    dup0, dh1, dfw, dg2 = _bwd_ffn_up(dup, up0, h1, dh2, w["ffn_norm_g"], w_up, w["fw"], n_rows)
    grad_w_out = jnp.concatenate([_weight_grad(mix_a, dh1, "grad_w_out_conv")[0],
                                  _weight_grad(mix_bt, dh1, "grad_w_out_attn", a_transposed=True)[0]], axis=0)
    stage0 = {
        "w_ffn_up": _weight_grad(dup0, n2, "grad_w_ffn_up"),
        "w_ffn_down": grad_w_down.reshape(N_DEV, D_FF // N_DEV, D_MODEL),
        "w_out": grad_w_out.reshape(N_DEV, D_MODEL // N_DEV, D_MODEL),
    }
    stage0, dh1 = lax.optimization_barrier((stage0, dh1))
    send_grads(0, stage0)
    do_t, delta, du1, dgb, dga, dlg, dlb, dcb = _bwd_out(
        dh1, o_t, u1, w_out, gb_col, w["conv_ln_g"], w["conv_ln_b"], w["conv_out_g"], n_rows)
    dq_t, dk, dv = _attn_bwd(q_t, k, v, do_t, lse, delta, n_rows)
    dqraw_t, dkv, dcq, dckv, dkr, dgq, dgkv = _bwd_qkv(
        dq_t, dk, dv, cq, ckv, gq, gkv, w["wq_t"], w["w_ukv"], cos, sin, cos_t, sin_t, n_rows)
    dz, dcw = _bwd_conv(du1, ag, w["conv_w"], dcq, dckv, dkr, n_rows)
    stage1 = {
        "w_in": _weight_grad(dz, nb, "grad_w_in")[0].reshape(N_DEV, D_IN // N_DEV, D_MODEL),
        "w_uq": _weight_grad(dqraw_t.reshape(N_HEADS * QK_DIM, n_rows), cqn, "grad_w_uq", a_transposed=True)[0].reshape(
            N_HEADS, QK_DIM, Q_LORA),
        "w_ukv": _slabs(_weight_grad(ckvn, dkv, "grad_w_ukv")[0]),
        "conv_w": _slabs(dcw),
        "ffn_conv_w": dfw[:, :, :UP_SLAB],
    }
    stage1, dz = lax.optimization_barrier((stage1, dz))
    send_grads(1, stage1)
    gx, gmeta, dg1 = _bwd_in(dz, x, meta_pad, dh1, g1, w["w_in"], n_rows)

    sharded = {"meta_tokens": _slabs(gmeta[DEAD:])}
    replicated = {
        "mix_norm_g": dg1, "q_norm_g": dgq, "kv_norm_g": dgkv, "conv_b": dcb, "conv_ln_g": dlg, "conv_ln_b": dlb,
        "conv_out_g": dga, "attn_out_g": dgb, "ffn_norm_g": dg2, "ffn_conv_b": dfb, "final_norm_g": dgf,
    }
    return loss[0, 0], gx, sharded, replicated


_SHARDED = (
    ("w_in", None, BF16), ("w_uq", None, BF16), ("w_ukv", None, BF16), ("w_out", None, BF16), ("w_ffn_up", None, BF16),
    ("w_ffn_down", None, BF16), ("conv_w", 32, F32), ("ffn_conv_w", 8, F32), ("meta_tokens", None, F32),
)
GATHER_LATE_ID = 3
EXCHANGE_STAGE_IDS = (4, 5)
_LATE_WEIGHTS = ("w_out", "w_ffn_up", "w_ffn_down")
_COLUMN_SHARDS = ("w_in", "w_uq", "w_ffn_up")


def kernel(x, meta_tokens, mix_norm_g, w_in, q_norm_g, w_uq, kv_norm_g, w_ukv, conv_w, conv_b, conv_ln_g, conv_ln_b, conv_out_g, attn_out_g, w_out, ffn_norm_g, w_ffn_up, ffn_conv_w, ffn_conv_b, w_ffn_down, final_norm_g, loss_target, m_meta_tokens, m_mix_norm_g, m_w_in, m_q_norm_g, m_w_uq, m_kv_norm_g, m_w_ukv, m_conv_w, m_conv_b, m_conv_ln_g, m_conv_ln_b, m_conv_out_g, m_attn_out_g, m_w_out, m_ffn_norm_g, m_w_ffn_up, m_ffn_conv_w, m_ffn_conv_b, m_w_ffn_down, m_final_norm_g, v_meta_tokens, v_mix_norm_g, v_w_in, v_q_norm_g, v_w_uq, v_kv_norm_g, v_w_ukv, v_conv_w, v_conv_b, v_conv_ln_g, v_conv_ln_b, v_conv_out_g, v_attn_out_g, v_w_out, v_ffn_norm_g, v_w_ffn_up, v_ffn_conv_w, v_ffn_conv_b, v_w_ffn_down, v_final_norm_g):
    given = dict(locals())
    weights = {name: given[name] for name in _WEIGHT_ORDER}
    moments_m = {name: given["m_" + name] for name in _WEIGHT_ORDER}
    moments_v = {name: given["v_" + name] for name in _WEIGHT_ORDER}
    seq = x.shape[1]
    n_rows = ROW_TILE + seq

    def shard2d(name, a):
        a = a.reshape(a.shape[-2], a.shape[-1])
        return a.T if name in _COLUMN_SHARDS else a

    early = [entry for entry in _SHARDED if entry[0] not in _LATE_WEIGHTS]
    shards = []
    for name, pad_to, _ in early:
        s = shard2d(name, weights[name])
        shards.append(s if pad_to is None else _pad_rows(s, pad_to))
    gathered = dict(zip([name for name, _, _ in early], _all_gather(shards, [dt for _, _, dt in early])))
    late_shards, gathered["meta_tokens"] = lax.optimization_barrier(
        ([shard2d(name, weights[name]) for name in _LATE_WEIGHTS], gathered["meta_tokens"]))
    late_parts = [s.astype(BF16) for s in late_shards]
    late = _sequencer_exchange(late_parts, [True] * len(late_parts), "gather_late", GATHER_LATE_ID)
    meta_full = _unslab(gathered["meta_tokens"])
    full = {
        "meta_pad": jnp.concatenate([jnp.zeros((DEAD, D_MODEL), F32), meta_full], axis=0),
        "w_in": gathered["w_in"].reshape(D_IN, D_MODEL),
        "wq_t": gathered["w_uq"],
        "w_ukv": gathered["w_ukv"],
        "wv_t": gathered["w_ukv"][:, :, QK_NOPE:].transpose(0, 2, 1),
        "conv_w": _unslab(gathered["conv_w"][:, :CONV_WIDTH]),
        "fw": jnp.pad(gathered["ffn_conv_w"][:, :FFN_CONV_WIDTH], ((0, 0), (0, 0), (0, UP_PAD - UP_SLAB))),
        "fb": jnp.pad(ffn_conv_b.reshape(N_DEV, 1, UP_SLAB), ((0, 0), (0, 0), (0, UP_PAD - UP_SLAB))),
        "final_norm_g": final_norm_g.reshape(1, D_MODEL),
    }
    for name in ("mix_norm_g", "q_norm_g", "kv_norm_g", "conv_b", "conv_ln_g", "conv_ln_b", "conv_out_g", "attn_out_g",
                 "ffn_norm_g"):
        full[name] = weights[name]

    def ffn_weights():
        w_out_all, w_up_all, w_down_all = late
        return (w_out_all.reshape(D_MODEL, D_MODEL), w_up_all, w_down_all.reshape(N_ACT_SLAB, UP_SLAB, D_MODEL))

    wire = {name: (pad_to, dt) for name, pad_to, dt in _SHARDED}
    landing = {}

    def on_the_wire(name, slabs):
        pad_to, dt = wire[name]
        slabs = slabs.astype(dt)
        return slabs if pad_to is None else jnp.pad(slabs, ((0, 0), (0, pad_to - slabs.shape[1]), (0, 0)))

    def send_grads(stage, grads):
        parts = [on_the_wire(name, slabs) for name, slabs in grads.items()]
        if landing:
            arrived = list(landing)
            parts, held = lax.optimization_barrier((parts, [landing[name] for name in arrived]))
            landing.update(zip(arrived, held))
        landed = _sequencer_exchange(parts, [False] * len(parts), f"exchange_stage{stage}", EXCHANGE_STAGE_IDS[stage])
        landing.update(zip(grads, landed))

    loss, gx, sharded, replicated = _local_step(x[0], loss_target[0], full, n_rows, ffn_weights, send_grads)

    parts = [on_the_wire(name, slabs) for name, slabs in sharded.items()] + [_pack_replicated(replicated, loss)]
    landed = _exchange(parts, [False] * len(sharded) + [True])
    landing.update(zip(sharded, landed[:-1]))

    grad, delta, new_m, new_v = {}, {}, {}, {}
    for name, pad_to, _ in _SHARDED:
        land = landing[name]
        ws, ms, vs = (shard2d(name, a[name]) for a in (weights, moments_m, moments_v))
        rows = ws.shape[0]
        if pad_to is not None:
            ws, ms, vs = _pad_rows(ws, pad_to), _pad_rows(ms, pad_to), _pad_rows(vs, pad_to)
        outs = _adamw(land, ws, ms, vs, "adamw_" + name)
        shape = weights[name].shape
        grad[name], delta[name], new_m[name], new_v[name] = (
            (o.T if name in _COLUMN_SHARDS else o[:rows]).reshape(shape) for o in outs)
    loss, updates = _adamw_replicated(landed[-1], weights, moments_m, moments_v)
    for name, outs in updates.items():
        grad[name], delta[name], new_m[name], new_v[name] = (o.reshape(weights[name].shape) for o in outs)

    return (loss, gx[None], *[grad[n] for n in _WEIGHT_ORDER], *[delta[n] for n in _WEIGHT_ORDER],
            *[new_m[n] for n in _WEIGHT_ORDER], *[new_v[n] for n in _WEIGHT_ORDER])
```

```python
import jax
import jax.numpy as jnp
from jax import lax
from jax.experimental import pallas as pl
from jax.experimental.pallas import tpu as pltpu
from jax.experimental.pallas import tpu_sc as plsc

F32 = jnp.float32
BF16 = jnp.bfloat16

N_DEV = 8
D_MODEL = 1024
CHUNK = 64
CHUNK_SHIFT = 6
N_META = 16
D_CONV = 512
CONV_WIDTH = 31
N_HEADS = 8
QK_NOPE = 64
QK_ROPE = 32
QK_DIM = QK_NOPE + QK_ROPE
V_HEAD = 64
KV_HEAD = QK_NOPE + V_HEAD
D_ATTN = N_HEADS * V_HEAD
Q_LORA = 384
KV_LORA = 256
ROPE_THETA = 10000.0
D_IN = 2 * D_CONV + Q_LORA + KV_LORA + QK_ROPE
D_FF = 2816
D_UP = 2 * D_FF
FFN_CONV_WIDTH = 3
UP_SLAB = D_UP // N_DEV
N_ACT_SLAB = D_FF // UP_SLAB
EPS = 1e-6
NEG = -1e30
_LN2 = 0.6931471805599453
QK_LOGIT_SCALE = QK_DIM ** -0.5 / _LN2
ADAM_LR = 0.001
ADAM_B1 = 0.9
ADAM_B2 = 0.999
ADAM_EPS = 1e-08
ADAM_WD = 0.01
ADAM_STEP = 10

ROW_TILE = 256
DEAD = ROW_TILE - N_META
CONV_HALO = 32
FFN_HALO = 16
assert FFN_HALO == N_META
VMEM_LIMIT = 56 * 1024 * 1024
_LANES = 128

MESH = pl.DeviceIdType.MESH


def _dot(a, b):
    return jnp.dot(a, b, preferred_element_type=F32)


def _dot_nt(a, b):
    return lax.dot_general(a, b, (((1,), (1,)), ((), ())), preferred_element_type=F32)


def _dot_tn(a, b):
    return lax.dot_general(a, b, (((0,), (0,)), ((), ())), preferred_element_type=F32)


def _sigmoid(x):
    return 1.0 / (1.0 + jnp.exp2(x * (-1.0 / _LN2)))


def _rms_fwd(x, g):
    r = lax.rsqrt(jnp.mean(x * x, axis=-1, keepdims=True) + EPS)
    return x * r * g


def _rms_bwd(dy, x, g):
    r = lax.rsqrt(jnp.mean(x * x, axis=-1, keepdims=True) + EPS)
    w = dy * g
    dx = r * w - x * (r * r * r) * jnp.mean(w * x, axis=-1, keepdims=True)
    return dx, jnp.sum(dy * x * r, axis=0, keepdims=True)


def _rope(x, cos, sin):
    half = QK_ROPE // 2
    x1, x2 = x[:, :half], x[:, half:]
    return jnp.concatenate([x1 * cos - x2 * sin, x2 * cos + x1 * sin], axis=-1)


def _rope_t(dy, cos, sin):
    half = QK_ROPE // 2
    d1, d2 = dy[:, :half], dy[:, half:]
    return jnp.concatenate([d1 * cos + d2 * sin, d2 * cos - d1 * sin], axis=-1)


def _row_ids(i, rows):
    return i * rows + lax.broadcasted_iota(jnp.int32, (rows, 1), 0)


def _accumulate(ref, first, value):
    @pl.when(first)
    def _():
        ref[...] = value

    @pl.when(jnp.logical_not(first))
    def _():
        ref[...] += value


def _tile_spec(shape):
    nd = len(shape)
    if nd == 2:
        return pl.BlockSpec((ROW_TILE, shape[1]), lambda i: (i, 0))
    return pl.BlockSpec((shape[0], ROW_TILE, shape[2]), lambda i: (0, i, 0))


def _whole_spec(shape):
    nd = len(shape)
    return pl.BlockSpec(tuple(shape), lambda i: (0,) * nd, pipeline_mode=pl.Buffered(1))


def _acc_spec(shape):
    nd = len(shape)
    return pl.BlockSpec(tuple(shape), lambda i: (0,) * nd)


def _real_spec(width):
    return pl.BlockSpec((ROW_TILE, width), lambda i: (jnp.maximum(i - 1, 0), 0))


def _params(*semantics):
    return pltpu.CompilerParams(dimension_semantics=semantics, vmem_limit_bytes=VMEM_LIMIT)


def _fwd_in(x, meta_pad, g1, w_in, n_rows):
    nt = n_rows // ROW_TILE

    def body(x_ref, meta_ref, g_ref, w_ref, nb_ref, ag_ref, cq_ref, ckv_ref, kr_ref):
        i = pl.program_id(0)
        h0 = jnp.where(i == 0, meta_ref[...], x_ref[...])
        nb = _rms_fwd(h0, g_ref[...]).astype(BF16)
        nb_ref[...] = nb
        z = _dot_nt(nb, w_ref[...])
        ag_ref[...] = z[:, :2 * D_CONV]
        cq_ref[...] = z[:, 2 * D_CONV:2 * D_CONV + Q_LORA]
        ckv_ref[...] = z[:, 2 * D_CONV + Q_LORA:2 * D_CONV + Q_LORA + KV_LORA]
        kr_ref[...] = z[:, 2 * D_CONV + Q_LORA + KV_LORA:]

    out_shapes = [
        jax.ShapeDtypeStruct((n_rows, D_MODEL), BF16),
        jax.ShapeDtypeStruct((n_rows, 2 * D_CONV), F32),
        jax.ShapeDtypeStruct((n_rows, Q_LORA), F32),
        jax.ShapeDtypeStruct((n_rows, KV_LORA), F32),
        jax.ShapeDtypeStruct((n_rows, QK_ROPE), F32),
    ]
    return pl.pallas_call(
        body, name="fwd_in", grid=(nt,),
        in_specs=[_real_spec(D_MODEL), _whole_spec(meta_pad.shape), _whole_spec(g1.shape), _whole_spec(w_in.shape)],
        out_specs=[_tile_spec(s.shape) for s in out_shapes],
        out_shape=out_shapes,
        compiler_params=_params("parallel"),
    )(x, meta_pad, g1, w_in)


def _conv_chain(u1, ln_g, ln_b):
    mu = jnp.mean(u1, axis=-1, keepdims=True)
    xc = u1 - mu
    rstd = lax.rsqrt(jnp.mean(xc * xc, axis=-1, keepdims=True) + EPS)
    xh = xc * rstd
    u2 = xh * ln_g + ln_b
    return xh, u2, u2 * _sigmoid(u2), rstd


def _fwd_conv(ag, conv_w, conv_b, ln_g, ln_b, out_g, n_rows):
    nt = n_rows // ROW_TILE

    def body(ag_ref, w_ref, b_ref, lg_ref, lb_ref, og_ref, mix_ref, u1_ref, ext_ref, conv_ref):
        i = pl.program_id(0)

        @pl.when(i == 0)
        def _():
            ext_ref[:, 0:CONV_HALO, :] = jnp.zeros((CONV_PLANES, CONV_HALO, _LANES), F32)

        ag_t = ag_ref[...]
        live = _row_ids(i, ROW_TILE) >= DEAD
        u0 = jnp.where(live, ag_t[:, :D_CONV] * _sigmoid(ag_t[:, D_CONV:]), 0.0)
        _to_planes(ext_ref, (), slice(CONV_HALO, None), u0)
        first = CONV_HALO - (CONV_WIDTH - 1)
        for c in range(CONV_PLANES):
            taps = w_ref[:, c * _LANES:(c + 1) * _LANES]
            for p in range(PHASES):
                acc = jnp.zeros((PHASE_ROWS, _LANES), F32)
                for k in range(CONV_WIDTH):
                    acc = acc + taps[k:k + 1, :] * ext_ref[c, _phase(first + k + p), :]
                conv_ref[c, _phase(p), :] = acc
        ext_ref[:, 0:CONV_HALO, :] = ext_ref[:, ROW_TILE:ROW_TILE + CONV_HALO, :]
        u1 = _from_planes(conv_ref, (), D_CONV) + b_ref[...]
        u1_ref[...] = u1
        _, _, u3, _ = _conv_chain(u1, lg_ref[...], lb_ref[...])
        mix_ref[...] = _rms_fwd(u3, og_ref[...]).astype(BF16)

    out_shapes = [jax.ShapeDtypeStruct((n_rows, D_CONV), BF16), jax.ShapeDtypeStruct((n_rows, D_CONV), F32)]
    small = [conv_w, conv_b, ln_g, ln_b, out_g]
    return pl.pallas_call(
        body, name="fwd_conv", grid=(nt,),
        in_specs=[_tile_spec(ag.shape)] + [_whole_spec(a.shape) for a in small],
        out_specs=[_tile_spec(s.shape) for s in out_shapes],
        out_shape=out_shapes,
        scratch_shapes=[pltpu.VMEM((CONV_PLANES, ROW_TILE + CONV_HALO, _LANES), F32),
                        pltpu.VMEM((CONV_PLANES, ROW_TILE, _LANES), F32)],
        compiler_params=_params("arbitrary"),
    )(ag, *small)


def _lane_tile(shape):
    if len(shape) == 2:
        return pl.BlockSpec((shape[0], ROW_TILE), lambda i: (0, i))
    return pl.BlockSpec((shape[0], shape[1], ROW_TILE), lambda i: (0, 0, i))


def _rope_rows(x, cos, sin):
    half = QK_ROPE // 2
    x1, x2 = x[:half], x[half:]
    return jnp.concatenate([x1 * cos - x2 * sin, x2 * cos + x1 * sin], axis=0)


def _rope_rows_t(dy, cos, sin):
    half = QK_ROPE // 2
    d1, d2 = dy[:half], dy[half:]
    return jnp.concatenate([d1 * cos + d2 * sin, d2 * cos - d1 * sin], axis=0)


def _fwd_qkv(cq, ckv, kr, gq, gkv, wq_t, w_ukv, wv_t, cos, sin, cos_t, sin_t, n_rows):
    nt = n_rows // ROW_TILE

    def body(cq_ref, ckv_ref, kr_ref, gq_ref, gkv_ref, wqt_ref, wkv_ref, wvt_ref, cos_ref, sin_ref, cost_ref, sint_ref,
             qt_ref, k_ref, v_ref, vt_ref, cqn_ref, ckvn_ref):
        cqn = _rms_fwd(cq_ref[...], gq_ref[...]).astype(BF16)
        ckvn = _rms_fwd(ckv_ref[...], gkv_ref[...]).astype(BF16)
        cqn_ref[...] = cqn
        ckvn_ref[...] = ckvn
        k_rot = _rope(kr_ref[...], cos_ref[...], sin_ref[...])
        cos_rows, sin_rows = cost_ref[...], sint_ref[...]
        q_all = _dot_nt(wqt_ref[...].reshape(N_HEADS * QK_DIM, Q_LORA), cqn)
        vt_all = _dot_nt(wvt_ref[...].reshape(N_HEADS * V_HEAD, KV_LORA), ckvn).astype(BF16)
        for h in range(N_HEADS):
            q_raw = q_all[h * QK_DIM:(h + 1) * QK_DIM]
            q_h = jnp.concatenate([q_raw[:QK_NOPE], _rope_rows(q_raw[QK_NOPE:], cos_rows, sin_rows)], axis=0)
            qt_ref[h] = (q_h * QK_LOGIT_SCALE).astype(BF16)
            kv = _dot(ckvn, wkv_ref[h])
            k_ref[h] = jnp.concatenate([kv[:, :QK_NOPE], k_rot], axis=-1).astype(BF16)
            v_ref[h] = kv[:, QK_NOPE:].astype(BF16)
            vt_ref[h] = vt_all[h * V_HEAD:(h + 1) * V_HEAD]

    out_shapes = [
        jax.ShapeDtypeStruct((N_HEADS, QK_DIM, n_rows), BF16),
        jax.ShapeDtypeStruct((N_HEADS, n_rows, QK_DIM), BF16),
        jax.ShapeDtypeStruct((N_HEADS, n_rows, V_HEAD), BF16),
        jax.ShapeDtypeStruct((N_HEADS, V_HEAD, n_rows), BF16),
        jax.ShapeDtypeStruct((n_rows, Q_LORA), BF16),
        jax.ShapeDtypeStruct((n_rows, KV_LORA), BF16),
    ]
    tiles = [cq, ckv, kr]
    whole = [gq, gkv, wq_t, w_ukv, wv_t]
    out_specs = [_lane_tile(out_shapes[0].shape), _tile_spec(out_shapes[1].shape), _tile_spec(out_shapes[2].shape),
                 _lane_tile(out_shapes[3].shape), _tile_spec(out_shapes[4].shape), _tile_spec(out_shapes[5].shape)]
    return pl.pallas_call(
        body, name="fwd_qkv", grid=(nt,),
        in_specs=[_tile_spec(a.shape) for a in tiles] + [_whole_spec(a.shape) for a in whole]
        + [_tile_spec(cos.shape), _tile_spec(sin.shape), _lane_tile(cos_t.shape), _lane_tile(sin_t.shape)],
        out_specs=out_specs,
        out_shape=out_shapes,
        compiler_params=_params("parallel"),
    )(*tiles, *whole, cos, sin, cos_t, sin_t)


def _chunk_of(rows):
    return jnp.where(rows >= ROW_TILE, lax.shift_right_arithmetic(rows - ROW_TILE, CHUNK_SHIFT) + 1, 0)


def _visible(i, j):
    k_rows = j * ROW_TILE + lax.broadcasted_iota(jnp.int32, (ROW_TILE, 1), 0)
    q_rows = i * ROW_TILE + lax.broadcasted_iota(jnp.int32, (1, ROW_TILE), 1)
    return jnp.logical_and(_chunk_of(q_rows) >= _chunk_of(k_rows), k_rows >= DEAD)


def _attn_fwd(q_t, k, v_t, n_rows):
    nt = n_rows // ROW_TILE

    def body(qt_ref, k_ref, vt_ref, ot_ref, lse_ref, max_ref, sum_ref):
        i = pl.program_id(0)
        q_ts = [qt_ref[h] for h in range(N_HEADS)]

        def key_rows(j):
            return pl.ds(pl.multiple_of(j * ROW_TILE, ROW_TILE), ROW_TILE)

        def make_step(masked, tiles, first=0):
            def step(t, carry):
                js = [first + tiles * t + u for u in range(tiles)]
                scores = [[_dot(k_ref[h, key_rows(j), :], q_ts[h]) for h in range(N_HEADS)] for j in js]
                for j, tile_scores in zip(js, scores):
                    visible = _visible(i, j) if masked else None
                    probs, alphas = [], []
                    for h in range(N_HEADS):
                        m = max_ref[h]
                        s = jnp.where(visible, tile_scores[h], NEG) if masked else tile_scores[h]
                        m_new = jnp.maximum(m, jnp.max(s, axis=0, keepdims=True))
                        alpha = jnp.exp2(m - m_new)
                        p = jnp.exp2(s - m_new)
                        probs.append(p.astype(BF16))
                        alphas.append(alpha)
                        max_ref[h] = m_new
                        sum_ref[h] = alpha * sum_ref[h] + jnp.sum(p, axis=0, keepdims=True)
                    for h in range(N_HEADS):
                        ot_ref[h] = alphas[h] * ot_ref[h] + _dot(vt_ref[h, :, key_rows(j)], probs[h])
                return carry
            return step

        max_ref[...] = jnp.full(max_ref.shape, NEG, F32)
        sum_ref[...] = jnp.zeros_like(sum_ref)
        ot_ref[...] = jnp.zeros_like(ot_ref)
        between = jnp.maximum(i - 1, 0)
        quads = lax.shift_right_logical(between, 2)
        pairs = jnp.bitwise_and(lax.shift_right_logical(between, 1), 1)
        make_step(True, 1)(0, 0)
        lax.fori_loop(0, quads, make_step(False, 4, first=1), 0)
        lax.fori_loop(0, pairs, make_step(False, 2, first=1 + 4 * quads), 0)
        lax.fori_loop(1 + 4 * quads + 2 * pairs, i, make_step(False, 1), 0)
        lax.fori_loop(jnp.maximum(i, 1), i + 1, make_step(True, 1), 0)
        for h in range(N_HEADS):
            l = sum_ref[h]
            ot_ref[h] = ot_ref[h] / l
            lse_ref[h] = max_ref[h] + jnp.log2(l)

    out_shapes = [jax.ShapeDtypeStruct((N_HEADS, V_HEAD, n_rows), F32), jax.ShapeDtypeStruct((N_HEADS, 1, n_rows), F32)]
    return pl.pallas_call(
        body, name="attn_fwd", grid=(nt,),
        in_specs=[_lane_tile(q_t.shape), _whole_spec(k.shape), _whole_spec(v_t.shape)],
        out_specs=[_lane_tile(s.shape) for s in out_shapes],
        out_shape=out_shapes,
        scratch_shapes=[pltpu.VMEM((N_HEADS, 1, ROW_TILE), F32), pltpu.VMEM((N_HEADS, 1, ROW_TILE), F32)],
        compiler_params=_params("parallel"),
    )(q_t, k, v_t)


def _heads_to_rows(ref):
    return jnp.concatenate([ref[h] for h in range(N_HEADS)], axis=0)


def _rms_cols(x, g_col):
    r = lax.rsqrt(jnp.mean(x * x, axis=0, keepdims=True) + EPS)
    return x * r * g_col


def _fwd_out(x, meta_pad, mix_a, o_t, gb_col, w_out, n_rows):
    nt = n_rows // ROW_TILE

    def body(x_ref, meta_ref, mixa_ref, ot_ref, gb_ref, w_ref, mixbt_ref, h1_ref):
        i = pl.program_id(0)
        h0 = jnp.where(i == 0, meta_ref[...], x_ref[...])
        mix_bt = _rms_cols(_heads_to_rows(ot_ref), gb_ref[...]).astype(BF16)
        mixbt_ref[...] = mix_bt
        h1_ref[...] = h0 + _dot(mixa_ref[...], w_ref[:D_CONV, :]) + _dot_tn(mix_bt, w_ref[D_CONV:, :])

    out_shapes = [jax.ShapeDtypeStruct((D_ATTN, n_rows), BF16), jax.ShapeDtypeStruct((n_rows, D_MODEL), F32)]
    return pl.pallas_call(
        body, name="fwd_out", grid=(nt,),
        in_specs=[_real_spec(D_MODEL), _whole_spec(meta_pad.shape), _tile_spec(mix_a.shape), _lane_tile(o_t.shape),
                  _whole_spec(gb_col.shape), _whole_spec(w_out.shape)],
        out_specs=[_lane_tile(out_shapes[0].shape), _tile_spec(out_shapes[1].shape)],
        out_shape=out_shapes,
        compiler_params=_params("parallel"),
    )(x, meta_pad, mix_a, o_t, gb_col, w_out)


PHASES = 8
PHASE_ROWS = ROW_TILE // PHASES
UP_PLANES = -(-UP_SLAB // _LANES)
UP_PAD = UP_PLANES * _LANES
CONV_PLANES = D_CONV // _LANES


def _phase(start):
    return pl.ds(start, PHASE_ROWS, stride=PHASES)


def _to_planes(ref, lead, rows, value):
    width = value.shape[-1]
    for c in range(-(-width // _LANES)):
        part = value[:, c * _LANES:min((c + 1) * _LANES, width)]
        if part.shape[-1] < _LANES:
            part = jnp.concatenate([part, jnp.zeros((part.shape[0], _LANES - part.shape[-1]), part.dtype)], axis=-1)
        ref[(*lead, c, rows, slice(None))] = part


def _from_planes(ref, lead, width):
    planes = [ref[(*lead, c)] for c in range(-(-width // _LANES))]
    last = width - (len(planes) - 1) * _LANES
    return jnp.concatenate(planes[:-1] + [planes[-1][:, :last]], axis=-1)


def _fwd_ffn(h1, target, g2, w_up, fw, fb, w_down, gf, n_rows):
    nt = n_rows // ROW_TILE

    def body(h1_ref, t_ref, g2_ref, wup_ref, fw_ref, fb_ref, wdn_ref, gf_ref,
             n2_ref, up0_ref, act_ref, da_ref, db_ref, dh2_ref, loss_ref, dgf_ref, ext_ref):
        i = pl.program_id(0)

        @pl.when(i == 0)
        def _():
            n2_m = _rms_fwd(h1_ref[DEAD:, :], g2_ref[...]).astype(BF16)
            n2_ref[0:DEAD, :] = jnp.zeros((DEAD, D_MODEL), BF16)
            n2_ref[DEAD:, :] = n2_m
            for s in range(N_DEV):
                up0_m = _dot_nt(n2_m, wup_ref[s])
                up0_ref[s, 0:DEAD, :] = jnp.zeros((DEAD, UP_SLAB), BF16)
                up0_ref[s, DEAD:, :] = up0_m.astype(BF16)
                ext_ref[s, 0:FFN_HALO, :] = up0_m
            act_ref[...] = jnp.zeros_like(act_ref)
            da_ref[...] = jnp.zeros_like(da_ref)
            db_ref[...] = jnp.zeros_like(db_ref)
            dh2_ref[...] = jnp.zeros_like(dh2_ref)
            loss_ref[...] = jnp.zeros_like(loss_ref)
            dgf_ref[...] = jnp.zeros_like(dgf_ref)

        @pl.when(i > 0)
        def _():
            h1_t = h1_ref[...]
            n2 = _rms_fwd(h1_t, g2_ref[...]).astype(BF16)
            n2_ref[...] = n2
            for s in range(N_DEV):
                up0 = _dot_nt(n2, wup_ref[s])
                up0_ref[s] = up0.astype(BF16)
                ext_ref[s, FFN_HALO:, :] = up0

            def conv(s):
                block = ext_ref[s]
                acc = fb_ref[s, :, :UP_SLAB] + fw_ref[s, FFN_CONV_WIDTH - 1:FFN_CONV_WIDTH, :UP_SLAB] * block[FFN_HALO:]
                for back in range(1, FFN_CONV_WIDTH):
                    k = FFN_CONV_WIDTH - 1 - back
                    acc = acc + fw_ref[s, k:k + 1, :UP_SLAB] * pltpu.roll(block, back, 0)[FFN_HALO:]
                return acc

            h2 = h1_t
            for s in range(N_ACT_SLAB):
                gate = conv(s)
                val = conv(s + N_ACT_SLAB)
                sg = _sigmoid(gate)
                silu = gate * sg
                act = (silu * val).astype(BF16)
                act_ref[s] = act
                da_ref[s] = (val * sg * (1.0 + gate * (1.0 - sg))).astype(BF16)
                db_ref[s] = silu.astype(BF16)
                h2 = h2 + _dot(act, wdn_ref[s])
            ext_ref[:, 0:FFN_HALO, :] = ext_ref[:, ROW_TILE:ROW_TILE + FFN_HALO, :]

            gf_t = gf_ref[...]
            diff = _rms_fwd(h2, gf_t) - t_ref[...]
            tile_loss = 0.5 * jnp.sum(jnp.sum(diff * diff, axis=-1, keepdims=True), axis=0, keepdims=True) / D_MODEL
            dh2, dgf = _rms_bwd(diff / D_MODEL, h2, gf_t)
            dh2_ref[...] = dh2
            loss_ref[...] += jnp.broadcast_to(tile_loss, loss_ref.shape)
            dgf_ref[...] += dgf

    act_like = jax.ShapeDtypeStruct((N_ACT_SLAB, n_rows, UP_SLAB), BF16)
    out_shapes = [
        jax.ShapeDtypeStruct((n_rows, D_MODEL), BF16),
        jax.ShapeDtypeStruct((N_DEV, n_rows, UP_SLAB), BF16),
        act_like, act_like, act_like,
        jax.ShapeDtypeStruct((n_rows, D_MODEL), F32),
        jax.ShapeDtypeStruct((8, 128), F32),
        jax.ShapeDtypeStruct((1, D_MODEL), F32),
    ]
    whole = [g2, w_up, fw, fb, w_down, gf]
    return pl.pallas_call(
        body, name="fwd_ffn", grid=(nt,),
        in_specs=[_tile_spec(h1.shape), _real_spec(D_MODEL)] + [_whole_spec(a.shape) for a in whole],
        out_specs=[_tile_spec(s.shape) for s in out_shapes[:6]] + [_acc_spec(s.shape) for s in out_shapes[6:]],
        out_shape=out_shapes,
        scratch_shapes=[pltpu.VMEM((N_DEV, ROW_TILE + FFN_HALO, UP_SLAB), F32)],
        compiler_params=_params("arbitrary"),
    )(h1, target, *whole)


def _rope_tables(n_rows):
    pos = jnp.maximum(jnp.arange(n_rows, dtype=jnp.int32) - DEAD, 0)
    inv_freq = 1.0 / (ROPE_THETA ** (jnp.arange(0, QK_ROPE, 2, dtype=F32) / QK_ROPE))
    ang_t = inv_freq[:, None] * pos.astype(F32)[None, :]
    return jnp.cos(ang_t), jnp.sin(ang_t)


def _halo_after(shape, halo, n_rows):
    last = n_rows // halo - 1
    step = ROW_TILE // halo
    if len(shape) == 2:
        return pl.BlockSpec((halo, shape[1]), lambda i: (jnp.minimum((i + 1) * step, last), 0))
    return pl.BlockSpec((shape[0], halo, shape[2]), lambda i: (0, jnp.minimum((i + 1) * step, last), 0))


def _bwd_ffn_act(dh2, da, db, act, w_down, n_rows):
    nt = n_rows // ROW_TILE

    def body(dh2_ref, da_ref, db_ref, act_ref, wdn_ref, dup_ref, dfb_ref, dwd_ref, acc_ref):
        i = pl.program_id(0)

        @pl.when(i == 0)
        def _():
            dfb_ref[...] = jnp.zeros_like(dfb_ref)
            dup_ref[...] = jnp.zeros_like(dup_ref)
            acc_ref[...] = jnp.zeros_like(acc_ref)

        @pl.when(i > 0)
        def _():
            dh2_b = dh2_ref[...].astype(BF16)
            for s in range(N_ACT_SLAB):
                d_act = _dot_nt(dh2_b, wdn_ref[s])
                d_gate = d_act * da_ref[s].astype(F32)
                d_val = d_act * db_ref[s].astype(F32)
                dup_ref[s] = d_gate.astype(BF16)
                dup_ref[s + N_ACT_SLAB] = d_val.astype(BF16)
                dfb_ref[s] += jnp.sum(d_gate, axis=0, keepdims=True)
                dfb_ref[s + N_ACT_SLAB] += jnp.sum(d_val, axis=0, keepdims=True)
                acc_ref[s] += _dot_tn(act_ref[s], dh2_b)

        @pl.when(i == nt - 1)
        def _():
            dwd_ref[...] = acc_ref[...].astype(BF16)

    out_shapes = [jax.ShapeDtypeStruct((N_DEV, n_rows, UP_SLAB), BF16), jax.ShapeDtypeStruct((N_DEV, 1, UP_SLAB), F32),
                  jax.ShapeDtypeStruct((N_ACT_SLAB, UP_SLAB, D_MODEL), BF16)]
    return pl.pallas_call(
        body, name="bwd_ffn_act", grid=(nt,),
        in_specs=[_tile_spec(dh2.shape), _tile_spec(da.shape), _tile_spec(db.shape), _tile_spec(act.shape),
                  _whole_spec(w_down.shape)],
        out_specs=[_tile_spec(out_shapes[0].shape), _acc_spec(out_shapes[1].shape), _acc_spec(out_shapes[2].shape)],
        out_shape=out_shapes,
        scratch_shapes=[pltpu.VMEM((N_ACT_SLAB, UP_SLAB, D_MODEL), F32)],
        compiler_params=_params("arbitrary"),
    )(dh2, da, db, act, w_down)


def _bwd_ffn_up(dup, up0, h1, dh2, g2, w_up, fw, mix_a, mix_bt, n_rows):
    nt = n_rows // ROW_TILE
    last_tap = FFN_CONV_WIDTH - 1
    half = mix_a.shape[1]

    def body(dup_ref, dnext_ref, up0_ref, h1_ref, dh2_ref, g2_ref, wup_ref, fw_ref, mixa_ref, mixbt_ref,
             dup0_ref, dh1_ref, dfw_ref, dg2_ref, dwo_ref, acc_ref):
        i = pl.program_id(0)

        def w_out_grad(dh1_b):
            return _dot_tn(mixa_ref[...].astype(BF16), dh1_b), _dot(mixbt_ref[...].astype(BF16), dh1_b)

        def conv_transpose(s, rows, d, after, u):
            block = jnp.concatenate([d, after], axis=0)
            dup0 = fw_ref[s, last_tap:last_tap + 1, :UP_SLAB] * d
            dfw_ref[s, last_tap:last_tap + 1, :UP_SLAB] += jnp.sum(d * u, axis=0, keepdims=True)
            for ahead in range(1, FFN_CONV_WIDTH):
                k = last_tap - ahead
                shifted = pltpu.roll(block, rows + FFN_HALO - ahead, 0)[:rows]
                dup0 = dup0 + fw_ref[s, k:k + 1, :UP_SLAB] * shifted
                dfw_ref[s, k:k + 1, :UP_SLAB] += jnp.sum(shifted * u, axis=0, keepdims=True)
            return dup0.astype(BF16)

        @pl.when(i == 0)
        def _():
            dfw_ref[...] = jnp.zeros_like(dfw_ref)
            dn2_m = jnp.zeros((N_META, D_MODEL), F32)
            for s in range(N_DEV):
                dup0_m = conv_transpose(s, N_META, dup_ref[s, DEAD:, :].astype(F32), dnext_ref[s].astype(F32),
                                        up0_ref[s, DEAD:, :].astype(F32))
                dup0_ref[s, 0:DEAD, :] = jnp.zeros((DEAD, UP_SLAB), BF16)
                dup0_ref[s, DEAD:, :] = dup0_m
                dn2_m = dn2_m + _dot(dup0_m, wup_ref[s])
            dx_m, dg2 = _rms_bwd(dn2_m, h1_ref[DEAD:, :], g2_ref[...])
            dh1_m = dh2_ref[DEAD:, :] + dx_m
            dh1_ref[0:DEAD, :] = jnp.zeros((DEAD, D_MODEL), F32)
            dh1_ref[DEAD:, :] = dh1_m
            dg2_ref[...] = dg2
            conv_half, attn_half = w_out_grad(
                jnp.concatenate([jnp.zeros((DEAD, D_MODEL), BF16), dh1_m.astype(BF16)], axis=0))
            acc_ref[0:half, :] = conv_half
            acc_ref[half:, :] = attn_half

        @pl.when(i > 0)
        def _():
            dn2 = jnp.zeros((ROW_TILE, D_MODEL), F32)
            for s in range(N_DEV):
                after = jnp.where(i == nt - 1, 0.0, dnext_ref[s].astype(F32))
                dup0_b = conv_transpose(s, ROW_TILE, dup_ref[s].astype(F32), after, up0_ref[s].astype(F32))
                dup0_ref[s] = dup0_b
                dn2 = dn2 + _dot(dup0_b, wup_ref[s])
            dx, dg2 = _rms_bwd(dn2, h1_ref[...], g2_ref[...])
            dh1 = dh2_ref[...] + dx
            dh1_ref[...] = dh1
            dg2_ref[...] += dg2
            conv_half, attn_half = w_out_grad(dh1.astype(BF16))
            acc_ref[0:half, :] += conv_half
            acc_ref[half:, :] += attn_half

        @pl.when(i == nt - 1)
        def _():
            dwo_ref[...] = acc_ref[...].astype(BF16)

    out_shapes = [
        jax.ShapeDtypeStruct((N_DEV, n_rows, UP_SLAB), BF16),
        jax.ShapeDtypeStruct((n_rows, D_MODEL), F32),
        jax.ShapeDtypeStruct((N_DEV, FFN_CONV_WIDTH, UP_PAD), F32),
        jax.ShapeDtypeStruct((1, D_MODEL), F32),
        jax.ShapeDtypeStruct((half + mix_bt.shape[0], D_MODEL), BF16),
    ]
    return pl.pallas_call(
        body, name="bwd_ffn_up", grid=(nt,),
        in_specs=[_tile_spec(dup.shape), _halo_after(dup.shape, FFN_HALO, n_rows), _tile_spec(up0.shape),
                  _tile_spec(h1.shape), _tile_spec(dh2.shape),
                  _whole_spec(g2.shape), _whole_spec(w_up.shape), _whole_spec(fw.shape),
                  _tile_spec(mix_a.shape), _lane_tile(mix_bt.shape)],
        out_specs=[_tile_spec(s.shape) for s in out_shapes[:2]] + [_acc_spec(s.shape) for s in out_shapes[2:]],
        out_shape=out_shapes,
        scratch_shapes=[pltpu.VMEM((half + mix_bt.shape[0], D_MODEL), F32)],
        compiler_params=_params("arbitrary"),
    )(dup, dup, up0, h1, dh2, g2, w_up, fw, mix_a, mix_bt)


def _bwd_out(dh1, o_t, u1, w_out, gb_col, ln_g, ln_b, ga, n_rows):
    nt = n_rows // ROW_TILE

    def body(dh1_ref, ot_ref, u1_ref, w_ref, gb_ref, lg_ref, lb_ref, ga_ref,
             dot_ref, delta_ref, du1_ref, dgb_ref, dga_ref, dlg_ref, dlb_ref, dcb_ref):
        i = pl.program_id(0)
        dh1_b = dh1_ref[...].astype(BF16)
        o_t = _heads_to_rows(ot_ref)
        gb = gb_ref[...]
        r = lax.rsqrt(jnp.mean(o_t * o_t, axis=0, keepdims=True) + EPS)
        dmix_bt = _dot_nt(w_ref[D_CONV:, :], dh1_b)
        wgt = dmix_bt * gb
        do_t = r * wgt - o_t * (r * r * r) * jnp.mean(wgt * o_t, axis=0, keepdims=True)
        dgb = jnp.sum(dmix_bt * o_t * r, axis=1, keepdims=True)
        for h in range(N_HEADS):
            do_h = do_t[h * V_HEAD:(h + 1) * V_HEAD]
            dot_ref[h] = do_h.astype(BF16)
            delta_ref[h] = jnp.sum(do_h * ot_ref[h], axis=0, keepdims=True)
        lg = lg_ref[...]
        xh, u2, u3, rstd = _conv_chain(u1_ref[...], lg, lb_ref[...])
        du3, dga = _rms_bwd(_dot_nt(dh1_b, w_ref[:D_CONV, :]), u3, ga_ref[...])
        sg = _sigmoid(u2)
        du2 = du3 * sg * (1.0 + u2 * (1.0 - sg))
        dxh = du2 * lg
        du1 = rstd * (dxh - jnp.mean(dxh, axis=-1, keepdims=True) - xh * jnp.mean(dxh * xh, axis=-1, keepdims=True))
        du1_ref[...] = du1
        first = i == 0
        _accumulate(dgb_ref, first, dgb)
        _accumulate(dga_ref, first, dga)
        _accumulate(dlg_ref, first, jnp.sum(du2 * xh, axis=0, keepdims=True))
        _accumulate(dlb_ref, first, jnp.sum(du2, axis=0, keepdims=True))
        _accumulate(dcb_ref, first, jnp.sum(du1, axis=0, keepdims=True))

    out_shapes = [
        jax.ShapeDtypeStruct((N_HEADS, V_HEAD, n_rows), BF16),
        jax.ShapeDtypeStruct((N_HEADS, 1, n_rows), F32),
        jax.ShapeDtypeStruct((n_rows, D_CONV), F32),
        jax.ShapeDtypeStruct((D_ATTN, 1), F32),
    ] + [jax.ShapeDtypeStruct((1, D_CONV), F32)] * 4
    whole = [w_out, gb_col, ln_g, ln_b, ga]
    return pl.pallas_call(
        body, name="bwd_out", grid=(nt,),
        in_specs=[_tile_spec(dh1.shape), _lane_tile(o_t.shape), _tile_spec(u1.shape)] + [_whole_spec(a.shape) for a in whole],
        out_specs=[_lane_tile(out_shapes[0].shape), _lane_tile(out_shapes[1].shape), _tile_spec(out_shapes[2].shape)]
        + [_acc_spec(s.shape) for s in out_shapes[3:]],
        out_shape=out_shapes,
        compiler_params=_params("arbitrary"),
    )(dh1, o_t, u1, *whole)


ATTN_BWD_HEADS = 8


def _attn_bwd(q_t, k, v, do_t, lse, delta, n_rows):
    nt = n_rows // ROW_TILE
    hp = ATTN_BWD_HEADS

    def body(k_ref, v_ref, qt_ref, dot_ref, lse_ref, delta_ref, dqt_ref, dk_ref, dv_ref):
        j = pl.program_id(1)

        @pl.when(j == 0)
        def _():
            dqt_ref[...] = jnp.zeros_like(dqt_ref)

        k_ts = [k_ref[h] for h in range(hp)]
        v_ts = [v_ref[h] for h in range(hp)]

        def make_step(masked, tiles, first=0):
            def step(t, carry):
                tiles_of_step = []
                for u in range(tiles):
                    i = first + tiles * t + u
                    cols = pl.ds(pl.multiple_of(i * ROW_TILE, ROW_TILE), ROW_TILE)
                    q_is = [qt_ref[h, :, cols] for h in range(hp)]
                    do_is = [dot_ref[h, :, cols] for h in range(hp)]
                    scores = [_dot(k_ts[h], q_is[h]) for h in range(hp)]
                    dps = [_dot(v_ts[h], do_is[h]) for h in range(hp)]
                    tiles_of_step.append((i, cols, q_is, do_is, scores, dps))
                for i, cols, q_is, do_is, scores, dps in tiles_of_step:
                    visible = _visible(i, j) if masked else None
                    probs, dss = [], []
                    for h in range(hp):
                        s = jnp.where(visible, scores[h], NEG) if masked else scores[h]
                        p = jnp.exp2(s - lse_ref[h, :, cols])
                        probs.append(p.astype(BF16))
                        dss.append((p * (dps[h] - delta_ref[h, :, cols])).astype(BF16))
                    for h in range(hp):
                        dv_ref[h] += _dot_nt(probs[h], do_is[h])
                        dk_ref[h] += _dot_nt(dss[h], q_is[h])
                        dqt_ref[h, :, cols] += _dot_tn(k_ts[h], dss[h])
                return carry
            return step

        dk_ref[...] = jnp.zeros_like(dk_ref)
        dv_ref[...] = jnp.zeros_like(dv_ref)
        make_step(True, 1)(j, 0)
        lax.fori_loop(jnp.where(j == 0, j + 1, nt), nt, make_step(True, 1), 0)
        unmasked = jnp.where(j == 0, 0, nt - 1 - j)
        quads = lax.shift_right_logical(unmasked, 2)
        pairs = jnp.bitwise_and(lax.shift_right_logical(unmasked, 1), 1)
        lax.fori_loop(0, quads, make_step(False, 4, first=j + 1), 0)
        lax.fori_loop(0, pairs, make_step(False, 2, first=j + 1 + 4 * quads), 0)
        lax.fori_loop(jnp.where(j == 0, nt, j + 1 + 4 * quads + 2 * pairs), nt, make_step(False, 1), 0)
        dk_ref[...] = dk_ref[...] * _LN2

    key_tile = lambda w: pl.BlockSpec((hp, ROW_TILE, w), lambda g, j: (g, j, 0))
    all_cols = lambda w: pl.BlockSpec((hp, w, n_rows), lambda g, j: (g, 0, 0))
    resident = lambda w: pl.BlockSpec((hp, w, n_rows), lambda g, j: (g, 0, 0), pipeline_mode=pl.Buffered(1))
    out_shapes = [
        jax.ShapeDtypeStruct((N_HEADS, QK_DIM, n_rows), F32),
        jax.ShapeDtypeStruct((N_HEADS, n_rows, QK_DIM), F32),
        jax.ShapeDtypeStruct((N_HEADS, n_rows, V_HEAD), F32),
    ]
    return pl.pallas_call(
        body, name="attn_bwd", grid=(N_HEADS // hp, nt),
        in_specs=[key_tile(QK_DIM), key_tile(V_HEAD), resident(QK_DIM), resident(V_HEAD), resident(1), resident(1)],
        out_specs=[all_cols(QK_DIM), key_tile(QK_DIM), key_tile(V_HEAD)],
        out_shape=out_shapes,
        compiler_params=_params("parallel", "arbitrary"),
    )(k, v, q_t, do_t, lse, delta)


def _bwd_qkv(dq_t, dk, dv, cq, ckv, gq, gkv, wq_t, w_ukv, cos, sin, cos_t, sin_t, n_rows):
    nt = n_rows // ROW_TILE

    def body(dqt_ref, dk_ref, dv_ref, cq_ref, ckv_ref, gq_ref, gkv_ref, wqt_ref, wkv_ref, cos_ref, sin_ref,
             cost_ref, sint_ref, dqraw_ref, dkv_ref, dcq_ref, dckv_ref, dkr_ref, dgq_ref, dgkv_ref):
        i = pl.program_id(0)
        cos_rows, sin_rows = cost_ref[...], sint_ref[...]
        dcqn = jnp.zeros((ROW_TILE, Q_LORA), F32)
        dckvn = jnp.zeros((ROW_TILE, KV_LORA), F32)
        dk_rot = jnp.zeros((ROW_TILE, QK_ROPE), F32)
        for h in range(N_HEADS):
            dq_h, dk_h = dqt_ref[h] * QK_DIM ** -0.5, dk_ref[h]
            dq_raw = jnp.concatenate(
                [dq_h[:QK_NOPE], _rope_rows_t(dq_h[QK_NOPE:], cos_rows, sin_rows)], axis=0).astype(BF16)
            dqraw_ref[h] = dq_raw
            dcqn = dcqn + _dot_tn(dq_raw, wqt_ref[h])
            dkv = jnp.concatenate([dk_h[:, :QK_NOPE], dv_ref[h]], axis=-1).astype(BF16)
            dkv_ref[:, h * KV_HEAD:(h + 1) * KV_HEAD] = dkv
            dckvn = dckvn + _dot_nt(dkv, wkv_ref[h])
            dk_rot = dk_rot + dk_h[:, QK_NOPE:]
        dkr_ref[...] = _rope_t(dk_rot, cos_ref[...], sin_ref[...]).astype(BF16)
        dcq, dgq = _rms_bwd(dcqn, cq_ref[...], gq_ref[...])
        dckv, dgkv = _rms_bwd(dckvn, ckv_ref[...], gkv_ref[...])
        dcq_ref[...] = dcq.astype(BF16)
        dckv_ref[...] = dckv.astype(BF16)
        _accumulate(dgq_ref, i == 0, dgq)
        _accumulate(dgkv_ref, i == 0, dgkv)

    out_shapes = [
        jax.ShapeDtypeStruct((N_HEADS, QK_DIM, n_rows), BF16),
        jax.ShapeDtypeStruct((n_rows, N_HEADS * KV_HEAD), BF16),
        jax.ShapeDtypeStruct((n_rows, Q_LORA), BF16),
        jax.ShapeDtypeStruct((n_rows, KV_LORA), BF16),
        jax.ShapeDtypeStruct((n_rows, QK_ROPE), BF16),
        jax.ShapeDtypeStruct((1, Q_LORA), F32),
        jax.ShapeDtypeStruct((1, KV_LORA), F32),
    ]
    tiles = [dk, dv, cq, ckv]
    whole = [gq, gkv, wq_t, w_ukv]
    return pl.pallas_call(
        body, name="bwd_qkv", grid=(nt,),
        in_specs=[_lane_tile(dq_t.shape)] + [_tile_spec(a.shape) for a in tiles] + [_whole_spec(a.shape) for a in whole]
        + [_tile_spec(cos.shape), _tile_spec(sin.shape), _lane_tile(cos_t.shape), _lane_tile(sin_t.shape)],
        out_specs=[_lane_tile(out_shapes[0].shape)] + [_tile_spec(s.shape) for s in out_shapes[1:5]]
        + [_acc_spec(s.shape) for s in out_shapes[5:]],
        out_shape=out_shapes,
        compiler_params=_params("arbitrary"),
    )(dq_t, *tiles, *whole, cos, sin, cos_t, sin_t)


def _bwd_conv(du1, ag, conv_w, dcq, dckv, dkr, n_rows):
    nt = n_rows // ROW_TILE

    last_tap = CONV_WIDTH - 1

    def body(du1_ref, dnext_ref, ag_ref, w_ref, dcq_ref, dckv_ref, dkr_ref, dz_ref, dw_ref,
             dext_ref, uext_ref, conv_ref, sums_ref):
        i = pl.program_id(0)

        @pl.when(i == 0)
        def _():
            sums_ref[...] = jnp.zeros_like(sums_ref)

        _to_planes(dext_ref, (), slice(0, ROW_TILE), du1_ref[...])
        _to_planes(dext_ref, (), slice(ROW_TILE, None), jnp.where(i == nt - 1, 0.0, dnext_ref[...]))
        ag_t = ag_ref[...]
        live = _row_ids(i, ROW_TILE) >= DEAD
        sg = _sigmoid(ag_t[:, D_CONV:])
        _to_planes(uext_ref, (), slice(None), jnp.where(live, ag_t[:, :D_CONV] * sg, 0.0))
        for c in range(CONV_PLANES):
            taps = w_ref[:, c * _LANES:(c + 1) * _LANES]
            for half in range(0, PHASES, PHASES // 2):
                phases = range(half, half + PHASES // 2)
                us = {p: uext_ref[c, _phase(p), :] for p in phases}
                accs = {p: jnp.zeros((PHASE_ROWS, _LANES), F32) for p in phases}
                for k in range(CONV_WIDTH):
                    tap_sum = jnp.zeros((PHASE_ROWS, _LANES), F32)
                    for p in phases:
                        shifted = dext_ref[c, _phase(p + last_tap - k), :]
                        accs[p] = accs[p] + taps[k:k + 1, :] * shifted
                        tap_sum = tap_sum + shifted * us[p]
                    sums_ref[c, k] += tap_sum
                for p in phases:
                    conv_ref[c, _phase(p), :] = accs[p]
        du0 = jnp.where(live, _from_planes(conv_ref, (), D_CONV), 0.0)
        da = du0 * sg
        dgate = du0 * ag_t[:, :D_CONV] * sg * (1.0 - sg)
        dz_ref[...] = jnp.concatenate(
            [da.astype(BF16), dgate.astype(BF16), dcq_ref[...], dckv_ref[...], dkr_ref[...]], axis=-1)

        @pl.when(i == nt - 1)
        def _():
            for c in range(CONV_PLANES):
                for k in range(CONV_WIDTH):
                    dw_ref[k:k + 1, c * _LANES:(c + 1) * _LANES] = jnp.sum(sums_ref[c, k], axis=0, keepdims=True)

    out_shapes = [jax.ShapeDtypeStruct((n_rows, D_IN), BF16), jax.ShapeDtypeStruct((CONV_WIDTH, D_CONV), F32)]
    return pl.pallas_call(
        body, name="bwd_conv", grid=(nt,),
        in_specs=[_tile_spec(du1.shape), _halo_after(du1.shape, CONV_HALO, n_rows), _tile_spec(ag.shape),
                  _whole_spec(conv_w.shape), _tile_spec(dcq.shape), _tile_spec(dckv.shape), _tile_spec(dkr.shape)],
        out_specs=[_tile_spec(out_shapes[0].shape), _acc_spec(out_shapes[1].shape)],
        out_shape=out_shapes,
        scratch_shapes=[pltpu.VMEM((CONV_PLANES, ROW_TILE + CONV_HALO, _LANES), F32),
                        pltpu.VMEM((CONV_PLANES, ROW_TILE, _LANES), F32), pltpu.VMEM((CONV_PLANES, ROW_TILE, _LANES), F32),
                        pltpu.VMEM((CONV_PLANES, CONV_WIDTH, PHASE_ROWS, _LANES), F32)],
        compiler_params=_params("arbitrary"),
    )(du1, du1, ag, conv_w, dcq, dckv, dkr)


def _bwd_in(dz, x, meta_pad, dh1, g1, w_in, n_rows):
    nt = n_rows // ROW_TILE

    def body(dz_ref, x_ref, meta_ref, dh1_ref, g_ref, w_ref, gx_ref, gmeta_ref, dg1_ref):
        i = pl.program_id(0)
        h0 = jnp.where(i == 0, meta_ref[...], x_ref[...])
        dx, dg1 = _rms_bwd(_dot(dz_ref[...], w_ref[...]), h0, g_ref[...])
        dh0 = dh1_ref[...] + dx
        gx_ref[...] = dh0

        @pl.when(i == 0)
        def _():
            gmeta_ref[...] = dh0

        _accumulate(dg1_ref, i == 0, dg1)

    out_shapes = [
        jax.ShapeDtypeStruct((n_rows - ROW_TILE, D_MODEL), F32),
        jax.ShapeDtypeStruct((ROW_TILE, D_MODEL), F32),
        jax.ShapeDtypeStruct((1, D_MODEL), F32),
    ]
    return pl.pallas_call(
        body, name="bwd_in", grid=(nt,),
        in_specs=[_tile_spec(dz.shape), _real_spec(D_MODEL), _whole_spec(meta_pad.shape), _tile_spec(dh1.shape),
                  _whole_spec(g1.shape), _whole_spec(w_in.shape)],
        out_specs=[_real_spec(D_MODEL), _acc_spec(out_shapes[1].shape), _acc_spec(out_shapes[2].shape)],
        out_shape=out_shapes,
        compiler_params=_params("arbitrary"),
    )(dz, x, meta_pad, dh1, g1, w_in)


def _contraction_tile(n_rows):
    return next(t for t in range(n_rows // 2 // _LANES * _LANES, 0, -_LANES) if n_rows % t == 0)


def _weight_grad(a, b, name, a_transposed=False):
    groups = max(a.shape[0] if a.ndim == 3 else 1, b.shape[0] if b.ndim == 3 else 1)
    n_rows, n = b.shape[-2], b.shape[-1]
    m = a.shape[-2] if a_transposed else a.shape[-1]
    kt = _contraction_tile(n_rows)
    steps = n_rows // kt

    def body(a_ref, b_ref, out_ref, acc_ref):
        i = pl.program_id(1)
        a_t, b_t = a_ref[...].astype(BF16), b_ref[...].astype(BF16)
        part = _dot(a_t, b_t) if a_transposed else _dot_tn(a_t, b_t)
        _accumulate(acc_ref, i == 0, part)

        @pl.when(i == steps - 1)
        def _():
            out_ref[...] = acc_ref[...].astype(out_ref.dtype)

    def spec(arr, rows_last):
        block = (arr.shape[-2], kt) if rows_last else (kt, arr.shape[-1])
        at = (lambda i: (0, i)) if rows_last else (lambda i: (i, 0))
        if arr.ndim == 3:
            return pl.BlockSpec((None,) + block, lambda g, i: (g,) + at(i))
        return pl.BlockSpec(block, lambda g, i: at(i))

    return pl.pallas_call(
        body, name=name, grid=(groups, steps),
        in_specs=[spec(a, a_transposed), spec(b, False)],
        out_specs=pl.BlockSpec((None, m, n), lambda g, i: (g, 0, 0)),
        out_shape=jax.ShapeDtypeStruct((groups, m, n), BF16),
        scratch_shapes=[pltpu.VMEM((m, n), F32)],
        compiler_params=_params("parallel", "arbitrary"),
    )(a, b)


def _my_index():
    return 4 * lax.axis_index("x") + 2 * lax.axis_index("y") + lax.axis_index("c")


def _peer(k):
    flip = lambda v, bit: 1 - v if bit else v
    px = flip(lax.axis_index("x"), k & 4)
    py = flip(lax.axis_index("y"), k & 2)
    pc = flip(lax.axis_index("c"), k & 1)
    return (px, py, pc), 4 * px + 2 * py + pc


def _all_gather(shards, dtypes):
    n = len(shards)
    sibling, chips = 1, (2, 4, 6)

    def body(*refs):
        ins, outs, stages = refs[:n], refs[n:2 * n], refs[2 * n:3 * n]
        send_sems, recv_sems, local_sems = refs[3 * n:]
        me = _my_index()
        for a in range(n):
            stages[a][...] = ins[a][...].astype(stages[a].dtype)
        local = [pltpu.make_async_copy(stages[a], outs[a].at[me], local_sems.at[a]) for a in range(n)]
        for cp in local:
            cp.start()

        def copy(a, k, src, slot, to):
            return pltpu.make_async_remote_copy(
                src_ref=src, dst_ref=outs[a].at[slot], send_sem=send_sems.at[a, k - 1],
                recv_sem=recv_sems.at[a, k - 1], device_id=_peer(to)[0], device_id_type=MESH)

        def own(a, k):
            return copy(a, k, stages[a], me, k)

        def passed(a, k):
            slot = _peer(k)[1]
            return copy(a, k ^ sibling, outs[a].at[slot], slot, sibling)

        def arrival(a, k):
            return copy(a, k, stages[a], _peer(k)[1], k)

        for k in (sibling,) + chips:
            for a in range(n):
                own(a, k).start()
        for k in chips:
            for a in range(n):
                arrival(a, k).wait_recv()
                passed(a, k).start()
        for a in range(n):
            arrival(a, sibling).wait_recv()
            for k in chips:
                arrival(a, k ^ sibling).wait_recv()
        for a in range(n):
            for k in (sibling,) + chips:
                own(a, k).wait_send()
            for k in chips:
                passed(a, k).wait_send()
        for cp in local:
            cp.wait()

    return pl.pallas_call(
        body, name="gather_weights",
        in_specs=[pl.BlockSpec(memory_space=pltpu.VMEM)] * n,
        out_specs=[pl.BlockSpec(memory_space=pl.ANY)] * n,
        out_shape=[jax.ShapeDtypeStruct((N_DEV,) + s.shape, dt) for s, dt in zip(shards, dtypes)],
        scratch_shapes=[pltpu.VMEM(s.shape, dt) for s, dt in zip(shards, dtypes)]
        + [pltpu.SemaphoreType.DMA((n, N_DEV - 1)), pltpu.SemaphoreType.DMA((n, N_DEV - 1)), pltpu.SemaphoreType.DMA((n,))],
        compiler_params=pltpu.CompilerParams(vmem_limit_bytes=VMEM_LIMIT),
    )(*shards)


def _exchange(parts, whole):
    n = len(parts)

    def body(*refs):
        ins, outs = refs[:n], refs[n:2 * n]
        send_sems, recv_sems, local_sems = refs[2 * n:]
        me = _my_index()

        def src(a, slab):
            return ins[a] if whole[a] else ins[a].at[slab]

        local = [pltpu.make_async_copy(src(a, me), outs[a].at[me], local_sems.at[a]) for a in range(n)]
        for cp in local:
            cp.start()

        def copy(a, k, slab, slot):
            peer, _ = _peer(k)
            return pltpu.make_async_remote_copy(
                src_ref=src(a, slab), dst_ref=outs[a].at[slot], send_sem=send_sems.at[a, k - 1],
                recv_sem=recv_sems.at[a, k - 1], device_id=peer, device_id_type=MESH)

        for k in range(1, N_DEV):
            for a in range(n):
                copy(a, k, _peer(k)[1], me).start()
        for k in range(1, N_DEV):
            for a in range(n):
                copy(a, k, _peer(k)[1], _peer(k)[1]).wait()
        for cp in local:
            cp.wait()

    return pl.pallas_call(
        body, name="exchange_grads",
        in_specs=[pl.BlockSpec(memory_space=pl.ANY)] * n,
        out_specs=[pl.BlockSpec(memory_space=pl.ANY)] * n,
        out_shape=[jax.ShapeDtypeStruct(((N_DEV,) + p.shape) if w else p.shape, p.dtype) for p, w in zip(parts, whole)],
        scratch_shapes=[pltpu.SemaphoreType.DMA((n, N_DEV - 1)), pltpu.SemaphoreType.DMA((n, N_DEV - 1)),
                        pltpu.SemaphoreType.DMA((n,))],
    )(*parts)


def _sequencer_exchange(parts, whole, name, collective_id):
    n = len(parts)
    srcs = [jax.new_ref(p, memory_space=pltpu.MemorySpace.HBM) for p in parts]
    lands = [jax.empty_ref(jax.ShapeDtypeStruct(((N_DEV,) + p.shape) if w else p.shape, p.dtype),
                           memory_space=pltpu.MemorySpace.HBM) for p, w in zip(parts, whole)]

    @pl.kernel(mesh=plsc.ScalarSubcoreMesh(axis_name="sequencer", num_cores=1), name=name,
               scratch_types=(pltpu.SemaphoreType.DMA((n, N_DEV - 1)), pltpu.SemaphoreType.DMA((n, N_DEV - 1)),
                              pltpu.SemaphoreType.DMA((n,))),
               compiler_params=pltpu.CompilerParams(collective_id=collective_id))
    def launch(send_sems, recv_sems, local_sems):
        barrier = pltpu.get_barrier_semaphore()
        for k in range(1, N_DEV):
            pl.semaphore_signal(barrier, inc=1, device_id=_peer(k)[0], device_id_type=MESH)
        pl.semaphore_wait(barrier, N_DEV - 1)
        me = _my_index()

        def src(a, slab):
            return srcs[a] if whole[a] else srcs[a].at[slab]

        local = [pltpu.make_async_copy(src(a, me), lands[a].at[me], local_sems.at[a]) for a in range(n)]
        for cp in local:
            cp.start()

        def copy(a, k, slab, slot):
            return pltpu.make_async_remote_copy(
                src_ref=src(a, slab), dst_ref=lands[a].at[slot], send_sem=send_sems.at[a, k - 1],
                recv_sem=recv_sems.at[a, k - 1], device_id=_peer(k)[0], device_id_type=MESH)

        for k in range(1, N_DEV):
            for a in range(n):
                copy(a, k, _peer(k)[1], me).start()
        for k in range(1, N_DEV):
            for a in range(n):
                copy(a, k, _peer(k)[1], _peer(k)[1]).wait()
        for cp in local:
            cp.wait()

    launch()
    return [land[...] for land in lands]


def _row_block(rows):
    if rows <= ROW_TILE:
        return rows
    return next(rb for rb in range(ROW_TILE, 0, -16) if rows % rb == 0)


def _adamw(landing, w, m, v, name):
    rows, cols = w.shape
    rb = _row_block(rows)

    def body(l_ref, w_ref, m_ref, v_ref, g_ref, d_ref, m2_ref, v2_ref):
        g = l_ref[0].astype(F32)
        for p in range(1, N_DEV):
            g = g + l_ref[p].astype(F32)
        g_ref[...] = g
        d_ref[...], m2_ref[...], v2_ref[...] = _adamw_step(g, w_ref[...], m_ref[...], v_ref[...])

    flat = pl.BlockSpec((rb, cols), lambda i: (i, 0))
    return pl.pallas_call(
        body, name=name, grid=(rows // rb,),
        in_specs=[pl.BlockSpec((N_DEV, rb, cols), lambda i: (0, i, 0)), flat, flat, flat],
        out_specs=[flat] * 4,
        out_shape=[jax.ShapeDtypeStruct((rows, cols), F32)] * 4,
        compiler_params=_params("parallel"),
    )(landing, w, m, v)


def _adamw_step(g, w, m, v):
    m2 = ADAM_B1 * m + (1.0 - ADAM_B1) * g
    v2 = ADAM_B2 * v + (1.0 - ADAM_B2) * (g * g)
    m_hat = m2 / (1.0 - ADAM_B1 ** ADAM_STEP)
    v_hat = v2 / (1.0 - ADAM_B2 ** ADAM_STEP)
    return -ADAM_LR * (m_hat / (jnp.sqrt(v_hat) + ADAM_EPS) + ADAM_WD * w), m2, v2


_REPLICATED = (
    ("mix_norm_g", D_MODEL), ("q_norm_g", Q_LORA), ("kv_norm_g", KV_LORA), ("conv_b", D_CONV), ("conv_ln_g", D_CONV),
    ("conv_ln_b", D_CONV), ("conv_out_g", D_CONV), ("attn_out_g", D_CONV), ("ffn_norm_g", D_MODEL),
    ("ffn_conv_b", D_UP), ("final_norm_g", D_MODEL),
)
_REPLICATED_WIDTH = sum(size for _, size in _REPLICATED) + _LANES

_WEIGHT_ORDER = (
    "meta_tokens", "mix_norm_g", "w_in", "q_norm_g", "w_uq", "kv_norm_g", "w_ukv", "conv_w", "conv_b", "conv_ln_g",
    "conv_ln_b", "conv_out_g", "attn_out_g", "w_out", "ffn_norm_g", "w_ffn_up", "ffn_conv_w", "ffn_conv_b",
    "w_ffn_down", "final_norm_g",
)


def _pack_replicated(grads, loss):
    rows = [grads[name].reshape(1, size) for name, size in _REPLICATED]
    return jnp.concatenate(rows + [jnp.broadcast_to(loss.reshape(1, 1), (1, _LANES))], axis=-1)


def _adamw_replicated(landing, weights, moments_m, moments_v):
    n = len(_REPLICATED)

    def body(*refs):
        l_ref, ins, outs = refs[0], refs[1:1 + 3 * n], refs[1 + 3 * n:]
        total = l_ref[0]
        for p in range(1, N_DEV):
            total = total + l_ref[p]
        at = 0
        for a, (_, size) in enumerate(_REPLICATED):
            g = total[:, at:at + size]
            w_ref, m_ref, v_ref = ins[3 * a:3 * a + 3]
            g_ref, d_ref, m2_ref, v2_ref = outs[4 * a:4 * a + 4]
            g_ref[...] = g
            d_ref[...], m2_ref[...], v2_ref[...] = _adamw_step(g, w_ref[...], m_ref[...], v_ref[...])
            at += size
        outs[-1][...] = total[:, at:at + _LANES]

    operands, out_shapes = [], []
    for name, size in _REPLICATED:
        operands += [weights[name].reshape(1, size), moments_m[name].reshape(1, size), moments_v[name].reshape(1, size)]
        out_shapes += [jax.ShapeDtypeStruct((1, size), F32)] * 4
    out_shapes.append(jax.ShapeDtypeStruct((1, _LANES), F32))
    outs = pl.pallas_call(body, name="adamw_replicated", out_shape=out_shapes)(landing, *operands)
    return outs[-1][0, 0], {name: outs[4 * a:4 * a + 4] for a, (name, _) in enumerate(_REPLICATED)}


def _pad_rows(a, rows):
    return jnp.pad(a, ((0, rows - a.shape[0]), (0, 0)))


def _slabs(a):
    r, c = a.shape
    return a.reshape(r, N_DEV, c // N_DEV).transpose(1, 0, 2)


def _unslab(a):
    g, r, c = a.shape
    return a.transpose(1, 0, 2).reshape(r, g * c)


def _local_step(x, target, w, n_rows, ffn_weights, send_grads):
    cos_t, sin_t = lax.optimization_barrier(_rope_tables(n_rows))
    cos, sin = cos_t.T, sin_t.T
    meta_pad, g1, gf = w["meta_pad"], w["mix_norm_g"], w["final_norm_g"]
    gq, gkv, gb_col = w["q_norm_g"], w["kv_norm_g"], w["attn_out_g"].reshape(D_ATTN, 1)
    nb, ag, cq, ckv, kr = _fwd_in(x, meta_pad, g1, w["w_in"], n_rows)
    mix_a, u1 = _fwd_conv(ag, w["conv_w"], w["conv_b"], w["conv_ln_g"], w["conv_ln_b"], w["conv_out_g"], n_rows)
    q_t, k, v, v_t, cqn, ckvn = _fwd_qkv(cq, ckv, kr, gq, gkv, w["wq_t"], w["w_ukv"], w["wv_t"], cos, sin, cos_t, sin_t, n_rows)
    o_t, lse = _attn_fwd(q_t, k, v_t, n_rows)
    w_out, w_up, w_down = ffn_weights()
    mix_bt, h1 = _fwd_out(x, meta_pad, mix_a, o_t, gb_col, w_out, n_rows)
    n2, up0, act, da, db, dh2, loss, dgf = _fwd_ffn(
        h1, target, w["ffn_norm_g"], w_up, w["fw"], w["fb"], w_down, gf, n_rows)

    dup, dfb, grad_w_down = _bwd_ffn_act(dh2, da, db, act, w_down, n_rows)
    dup0, dh1, dfw, dg2, grad_w_out = _bwd_ffn_up(
        dup, up0, h1, dh2, w["ffn_norm_g"], w_up, w["fw"], mix_a, mix_bt, n_rows)
    stage0 = {
        "w_ffn_up": _weight_grad(dup0, n2, "grad_w_ffn_up"),
        "w_ffn_down": grad_w_down.reshape(N_DEV, D_FF // N_DEV, D_MODEL),
        "w_out": grad_w_out.reshape(N_DEV, D_MODEL // N_DEV, D_MODEL),
    }
    stage0, dh1 = lax.optimization_barrier((stage0, dh1))
    send_grads(0, stage0)
    do_t, delta, du1, dgb, dga, dlg, dlb, dcb = _bwd_out(
        dh1, o_t, u1, w_out, gb_col, w["conv_ln_g"], w["conv_ln_b"], w["conv_out_g"], n_rows)
    dq_t, dk, dv = _attn_bwd(q_t, k, v, do_t, lse, delta, n_rows)
    dqraw_t, dkv, dcq, dckv, dkr, dgq, dgkv = _bwd_qkv(
        dq_t, dk, dv, cq, ckv, gq, gkv, w["wq_t"], w["w_ukv"], cos, sin, cos_t, sin_t, n_rows)
    dz, dcw = _bwd_conv(du1, ag, w["conv_w"], dcq, dckv, dkr, n_rows)
    stage1 = {
        "w_in": _weight_grad(dz, nb, "grad_w_in")[0].reshape(N_DEV, D_IN // N_DEV, D_MODEL),
        "w_uq": _weight_grad(dqraw_t.reshape(N_HEADS * QK_DIM, n_rows), cqn, "grad_w_uq", a_transposed=True)[0].reshape(
            N_HEADS, QK_DIM, Q_LORA),
        "w_ukv": _slabs(_weight_grad(ckvn, dkv, "grad_w_ukv")[0]),
        "conv_w": _slabs(dcw),
        "ffn_conv_w": dfw[:, :, :UP_SLAB],
    }
    stage1, dz = lax.optimization_barrier((stage1, dz))
    send_grads(1, stage1)
    gx, gmeta, dg1 = _bwd_in(dz, x, meta_pad, dh1, g1, w["w_in"], n_rows)

    sharded = {"meta_tokens": _slabs(gmeta[DEAD:])}
    replicated = {
        "mix_norm_g": dg1, "q_norm_g": dgq, "kv_norm_g": dgkv, "conv_b": dcb, "conv_ln_g": dlg, "conv_ln_b": dlb,
        "conv_out_g": dga, "attn_out_g": dgb, "ffn_norm_g": dg2, "ffn_conv_b": dfb, "final_norm_g": dgf,
    }
    return loss[0, 0], gx, sharded, replicated


_SHARDED = (
    ("w_in", None, BF16), ("w_uq", None, BF16), ("w_ukv", None, BF16), ("w_out", None, BF16), ("w_ffn_up", None, BF16),
    ("w_ffn_down", None, BF16), ("conv_w", 32, F32), ("ffn_conv_w", 8, F32), ("meta_tokens", None, F32),
)
GATHER_LATE_ID = 3
EXCHANGE_STAGE_IDS = (4, 5)
_LATE_WEIGHTS = ("w_out", "w_ffn_up", "w_ffn_down")
_COLUMN_SHARDS = ("w_in", "w_uq", "w_ffn_up")


def kernel(x, meta_tokens, mix_norm_g, w_in, q_norm_g, w_uq, kv_norm_g, w_ukv, conv_w, conv_b, conv_ln_g, conv_ln_b, conv_out_g, attn_out_g, w_out, ffn_norm_g, w_ffn_up, ffn_conv_w, ffn_conv_b, w_ffn_down, final_norm_g, loss_target, m_meta_tokens, m_mix_norm_g, m_w_in, m_q_norm_g, m_w_uq, m_kv_norm_g, m_w_ukv, m_conv_w, m_conv_b, m_conv_ln_g, m_conv_ln_b, m_conv_out_g, m_attn_out_g, m_w_out, m_ffn_norm_g, m_w_ffn_up, m_ffn_conv_w, m_ffn_conv_b, m_w_ffn_down, m_final_norm_g, v_meta_tokens, v_mix_norm_g, v_w_in, v_q_norm_g, v_w_uq, v_kv_norm_g, v_w_ukv, v_conv_w, v_conv_b, v_conv_ln_g, v_conv_ln_b, v_conv_out_g, v_attn_out_g, v_w_out, v_ffn_norm_g, v_w_ffn_up, v_ffn_conv_w, v_ffn_conv_b, v_w_ffn_down, v_final_norm_g):
    given = dict(locals())
    weights = {name: given[name] for name in _WEIGHT_ORDER}
    moments_m = {name: given["m_" + name] for name in _WEIGHT_ORDER}
    moments_v = {name: given["v_" + name] for name in _WEIGHT_ORDER}
    seq = x.shape[1]
    n_rows = ROW_TILE + seq

    def shard2d(name, a):
        a = a.reshape(a.shape[-2], a.shape[-1])
        return a.T if name in _COLUMN_SHARDS else a

    early = [entry for entry in _SHARDED if entry[0] not in _LATE_WEIGHTS]
    shards = []
    for name, pad_to, _ in early:
        s = shard2d(name, weights[name])
        shards.append(s if pad_to is None else _pad_rows(s, pad_to))
    gathered = dict(zip([name for name, _, _ in early], _all_gather(shards, [dt for _, _, dt in early])))
    late_shards, gathered["meta_tokens"] = lax.optimization_barrier(
        ([shard2d(name, weights[name]) for name in _LATE_WEIGHTS], gathered["meta_tokens"]))
    late_parts = [s.astype(BF16) for s in late_shards]
    late = _sequencer_exchange(late_parts, [True] * len(late_parts), "gather_late", GATHER_LATE_ID)
    meta_full = _unslab(gathered["meta_tokens"])
    full = {
        "meta_pad": jnp.concatenate([jnp.zeros((DEAD, D_MODEL), F32), meta_full], axis=0),
        "w_in": gathered["w_in"].reshape(D_IN, D_MODEL),
        "wq_t": gathered["w_uq"],
        "w_ukv": gathered["w_ukv"],
        "wv_t": gathered["w_ukv"][:, :, QK_NOPE:].transpose(0, 2, 1),
        "conv_w": _unslab(gathered["conv_w"][:, :CONV_WIDTH]),
        "fw": jnp.pad(gathered["ffn_conv_w"][:, :FFN_CONV_WIDTH], ((0, 0), (0, 0), (0, UP_PAD - UP_SLAB))),
        "fb": jnp.pad(ffn_conv_b.reshape(N_DEV, 1, UP_SLAB), ((0, 0), (0, 0), (0, UP_PAD - UP_SLAB))),
        "final_norm_g": final_norm_g.reshape(1, D_MODEL),
    }
    for name in ("mix_norm_g", "q_norm_g", "kv_norm_g", "conv_b", "conv_ln_g", "conv_ln_b", "conv_out_g", "attn_out_g",
                 "ffn_norm_g"):
        full[name] = weights[name]

    def ffn_weights():
        w_out_all, w_up_all, w_down_all = late
        return (w_out_all.reshape(D_MODEL, D_MODEL), w_up_all, w_down_all.reshape(N_ACT_SLAB, UP_SLAB, D_MODEL))

    wire = {name: (pad_to, dt) for name, pad_to, dt in _SHARDED}
    landing = {}

    def on_the_wire(name, slabs):
        pad_to, dt = wire[name]
        slabs = slabs.astype(dt)
        return slabs if pad_to is None else jnp.pad(slabs, ((0, 0), (0, pad_to - slabs.shape[1]), (0, 0)))

    def send_grads(stage, grads):
        parts = [on_the_wire(name, slabs) for name, slabs in grads.items()]
        if landing:
            arrived = list(landing)
            parts, held = lax.optimization_barrier((parts, [landing[name] for name in arrived]))
            landing.update(zip(arrived, held))
        landed = _sequencer_exchange(parts, [False] * len(parts), f"exchange_stage{stage}", EXCHANGE_STAGE_IDS[stage])
        landing.update(zip(grads, landed))

    loss, gx, sharded, replicated = _local_step(x[0], loss_target[0], full, n_rows, ffn_weights, send_grads)

    parts = [on_the_wire(name, slabs) for name, slabs in sharded.items()] + [_pack_replicated(replicated, loss)]
    landed = _exchange(parts, [False] * len(sharded) + [True])
    landing.update(zip(sharded, landed[:-1]))

    grad, delta, new_m, new_v = {}, {}, {}, {}
    for name, pad_to, _ in _SHARDED:
        land = landing[name]
        ws, ms, vs = (shard2d(name, a[name]) for a in (weights, moments_m, moments_v))
        rows = ws.shape[0]
        if pad_to is not None:
            ws, ms, vs = _pad_rows(ws, pad_to), _pad_rows(ms, pad_to), _pad_rows(vs, pad_to)
        outs = _adamw(land, ws, ms, vs, "adamw_" + name)
        shape = weights[name].shape
        grad[name], delta[name], new_m[name], new_v[name] = (
            (o.T if name in _COLUMN_SHARDS else o[:rows]).reshape(shape) for o in outs)
    loss, updates = _adamw_replicated(landed[-1], weights, moments_m, moments_v)
    for name, outs in updates.items():
        grad[name], delta[name], new_m[name], new_v[name] = (o.reshape(weights[name].shape) for o in outs)

    return (loss, gx[None], *[grad[n] for n in _WEIGHT_ORDER], *[delta[n] for n in _WEIGHT_ORDER],
            *[new_m[n] for n in _WEIGHT_ORDER], *[new_v[n] for n in _WEIGHT_ORDER])
```

```python
import jax
import jax.numpy as jnp
from jax import lax
from jax.experimental import pallas as pl
from jax.experimental.pallas import tpu as pltpu
from jax.experimental.pallas import tpu_sc as plsc

F32 = jnp.float32
BF16 = jnp.bfloat16

N_DEV = 8
D_MODEL = 1024
CHUNK = 64
CHUNK_SHIFT = 6
N_META = 16
D_CONV = 512
CONV_WIDTH = 31
N_HEADS = 8
QK_NOPE = 64
QK_ROPE = 32
QK_DIM = QK_NOPE + QK_ROPE
V_HEAD = 64
KV_HEAD = QK_NOPE + V_HEAD
D_ATTN = N_HEADS * V_HEAD
Q_LORA = 384
KV_LORA = 256
ROPE_THETA = 10000.0
D_IN = 2 * D_CONV + Q_LORA + KV_LORA + QK_ROPE
D_FF = 2816
D_UP = 2 * D_FF
FFN_CONV_WIDTH = 3
UP_SLAB = D_UP // N_DEV
N_ACT_SLAB = D_FF // UP_SLAB
EPS = 1e-6
NEG = -1e30
_LN2 = 0.6931471805599453
QK_LOGIT_SCALE = QK_DIM ** -0.5 / _LN2
ADAM_LR = 0.001
ADAM_B1 = 0.9
ADAM_B2 = 0.999
ADAM_EPS = 1e-08
ADAM_WD = 0.01
ADAM_STEP = 10

ROW_TILE = 256
DEAD = ROW_TILE - N_META
CONV_HALO = 32
FFN_HALO = 16
assert FFN_HALO == N_META
VMEM_LIMIT = 56 * 1024 * 1024
_LANES = 128

MESH = pl.DeviceIdType.MESH


def _dot(a, b):
    return jnp.dot(a, b, preferred_element_type=F32)


def _dot_nt(a, b):
    return lax.dot_general(a, b, (((1,), (1,)), ((), ())), preferred_element_type=F32)


def _dot_tn(a, b):
    return lax.dot_general(a, b, (((0,), (0,)), ((), ())), preferred_element_type=F32)


def _sigmoid(x):
    return 1.0 / (1.0 + jnp.exp2(x * (-1.0 / _LN2)))


def _rms_fwd(x, g):
    r = lax.rsqrt(jnp.mean(x * x, axis=-1, keepdims=True) + EPS)
    return x * r * g


def _rms_bwd(dy, x, g):
    r = lax.rsqrt(jnp.mean(x * x, axis=-1, keepdims=True) + EPS)
    w = dy * g
    dx = r * w - x * (r * r * r) * jnp.mean(w * x, axis=-1, keepdims=True)
    return dx, jnp.sum(dy * x * r, axis=0, keepdims=True)


def _rope(x, cos, sin):
    half = QK_ROPE // 2
    x1, x2 = x[:, :half], x[:, half:]
    return jnp.concatenate([x1 * cos - x2 * sin, x2 * cos + x1 * sin], axis=-1)


def _rope_t(dy, cos, sin):
    half = QK_ROPE // 2
    d1, d2 = dy[:, :half], dy[:, half:]
    return jnp.concatenate([d1 * cos + d2 * sin, d2 * cos - d1 * sin], axis=-1)


def _row_ids(i, rows):
    return i * rows + lax.broadcasted_iota(jnp.int32, (rows, 1), 0)


def _accumulate(ref, first, value):
    @pl.when(first)
    def _():
        ref[...] = value

    @pl.when(jnp.logical_not(first))
    def _():
        ref[...] += value


def _tile_spec(shape):
    nd = len(shape)
    if nd == 2:
        return pl.BlockSpec((ROW_TILE, shape[1]), lambda i: (i, 0))
    return pl.BlockSpec((shape[0], ROW_TILE, shape[2]), lambda i: (0, i, 0))


def _whole_spec(shape):
    nd = len(shape)
    return pl.BlockSpec(tuple(shape), lambda i: (0,) * nd, pipeline_mode=pl.Buffered(1))


def _acc_spec(shape):
    nd = len(shape)
    return pl.BlockSpec(tuple(shape), lambda i: (0,) * nd)


def _real_spec(width):
    return pl.BlockSpec((ROW_TILE, width), lambda i: (jnp.maximum(i - 1, 0), 0))


def _params(*semantics):
    return pltpu.CompilerParams(dimension_semantics=semantics, vmem_limit_bytes=VMEM_LIMIT)


def _fwd_in(x, meta_pad, g1, w_in, n_rows):
    nt = n_rows // ROW_TILE

    def body(x_ref, meta_ref, g_ref, w_ref, nb_ref, ag_ref, cq_ref, ckv_ref, kr_ref):
        i = pl.program_id(0)
        h0 = jnp.where(i == 0, meta_ref[...], x_ref[...])
        nb = _rms_fwd(h0, g_ref[...]).astype(BF16)
        nb_ref[...] = nb
        z = _dot_nt(nb, w_ref[...])
        ag_ref[...] = z[:, :2 * D_CONV]
        cq_ref[...] = z[:, 2 * D_CONV:2 * D_CONV + Q_LORA]
        ckv_ref[...] = z[:, 2 * D_CONV + Q_LORA:2 * D_CONV + Q_LORA + KV_LORA]
        kr_ref[...] = z[:, 2 * D_CONV + Q_LORA + KV_LORA:]

    out_shapes = [
        jax.ShapeDtypeStruct((n_rows, D_MODEL), BF16),
        jax.ShapeDtypeStruct((n_rows, 2 * D_CONV), F32),
        jax.ShapeDtypeStruct((n_rows, Q_LORA), F32),
        jax.ShapeDtypeStruct((n_rows, KV_LORA), F32),
        jax.ShapeDtypeStruct((n_rows, QK_ROPE), F32),
    ]
    return pl.pallas_call(
        body, name="fwd_in", grid=(nt,),
        in_specs=[_real_spec(D_MODEL), _whole_spec(meta_pad.shape), _whole_spec(g1.shape), _whole_spec(w_in.shape)],
        out_specs=[_tile_spec(s.shape) for s in out_shapes],
        out_shape=out_shapes,
        compiler_params=_params("parallel"),
    )(x, meta_pad, g1, w_in)


def _conv_chain(u1, ln_g, ln_b):
    mu = jnp.mean(u1, axis=-1, keepdims=True)
    xc = u1 - mu
    rstd = lax.rsqrt(jnp.mean(xc * xc, axis=-1, keepdims=True) + EPS)
    xh = xc * rstd
    u2 = xh * ln_g + ln_b
    return xh, u2, u2 * _sigmoid(u2), rstd


def _fwd_conv(ag, conv_w, conv_b, ln_g, ln_b, out_g, n_rows):
    nt = n_rows // ROW_TILE

    def body(ag_ref, w_ref, b_ref, lg_ref, lb_ref, og_ref, mix_ref, u1_ref, ext_ref, conv_ref):
        i = pl.program_id(0)

        @pl.when(i == 0)
        def _():
            ext_ref[:, 0:CONV_HALO, :] = jnp.zeros((CONV_PLANES, CONV_HALO, _LANES), F32)

        ag_t = ag_ref[...]
        live = _row_ids(i, ROW_TILE) >= DEAD
        u0 = jnp.where(live, ag_t[:, :D_CONV] * _sigmoid(ag_t[:, D_CONV:]), 0.0)
        _to_planes(ext_ref, (), slice(CONV_HALO, None), u0)
        first = CONV_HALO - (CONV_WIDTH - 1)
        for c in range(CONV_PLANES):
            taps = w_ref[:, c * _LANES:(c + 1) * _LANES]
            for p in range(PHASES):
                acc = jnp.zeros((PHASE_ROWS, _LANES), F32)
                for k in range(CONV_WIDTH):
                    acc = acc + taps[k:k + 1, :] * ext_ref[c, _phase(first + k + p), :]
                conv_ref[c, _phase(p), :] = acc
        ext_ref[:, 0:CONV_HALO, :] = ext_ref[:, ROW_TILE:ROW_TILE + CONV_HALO, :]
        u1 = _from_planes(conv_ref, (), D_CONV) + b_ref[...]
        u1_ref[...] = u1
        _, _, u3, _ = _conv_chain(u1, lg_ref[...], lb_ref[...])
        mix_ref[...] = _rms_fwd(u3, og_ref[...]).astype(BF16)

    out_shapes = [jax.ShapeDtypeStruct((n_rows, D_CONV), BF16), jax.ShapeDtypeStruct((n_rows, D_CONV), F32)]
    small = [conv_w, conv_b, ln_g, ln_b, out_g]
    return pl.pallas_call(
        body, name="fwd_conv", grid=(nt,),
        in_specs=[_tile_spec(ag.shape)] + [_whole_spec(a.shape) for a in small],
        out_specs=[_tile_spec(s.shape) for s in out_shapes],
        out_shape=out_shapes,
        scratch_shapes=[pltpu.VMEM((CONV_PLANES, ROW_TILE + CONV_HALO, _LANES), F32),
                        pltpu.VMEM((CONV_PLANES, ROW_TILE, _LANES), F32)],
        compiler_params=_params("arbitrary"),
    )(ag, *small)


def _lane_tile(shape):
    if len(shape) == 2:
        return pl.BlockSpec((shape[0], ROW_TILE), lambda i: (0, i))
    return pl.BlockSpec((shape[0], shape[1], ROW_TILE), lambda i: (0, 0, i))


def _rope_rows(x, cos, sin):
    half = QK_ROPE // 2
    x1, x2 = x[:half], x[half:]
    return jnp.concatenate([x1 * cos - x2 * sin, x2 * cos + x1 * sin], axis=0)


def _rope_rows_t(dy, cos, sin):
    half = QK_ROPE // 2
    d1, d2 = dy[:half], dy[half:]
    return jnp.concatenate([d1 * cos + d2 * sin, d2 * cos - d1 * sin], axis=0)


def _fwd_qkv(cq, ckv, kr, gq, gkv, wq_t, w_ukv, wv_t, cos, sin, cos_t, sin_t, n_rows):
    nt = n_rows // ROW_TILE

    def body(cq_ref, ckv_ref, kr_ref, gq_ref, gkv_ref, wqt_ref, wkv_ref, wvt_ref, cos_ref, sin_ref, cost_ref, sint_ref,
             qt_ref, k_ref, v_ref, vt_ref, cqn_ref, ckvn_ref):
        cqn = _rms_fwd(cq_ref[...], gq_ref[...]).astype(BF16)
        ckvn = _rms_fwd(ckv_ref[...], gkv_ref[...]).astype(BF16)
        cqn_ref[...] = cqn
        ckvn_ref[...] = ckvn
        k_rot = _rope(kr_ref[...], cos_ref[...], sin_ref[...])
        cos_rows, sin_rows = cost_ref[...], sint_ref[...]
        q_all = _dot_nt(wqt_ref[...].reshape(N_HEADS * QK_DIM, Q_LORA), cqn)
        vt_all = _dot_nt(wvt_ref[...].reshape(N_HEADS * V_HEAD, KV_LORA), ckvn).astype(BF16)
        for h in range(N_HEADS):
            q_raw = q_all[h * QK_DIM:(h + 1) * QK_DIM]
            q_h = jnp.concatenate([q_raw[:QK_NOPE], _rope_rows(q_raw[QK_NOPE:], cos_rows, sin_rows)], axis=0)
            qt_ref[h] = (q_h * QK_LOGIT_SCALE).astype(BF16)
            kv = _dot(ckvn, wkv_ref[h])
            k_ref[h] = jnp.concatenate([kv[:, :QK_NOPE], k_rot], axis=-1).astype(BF16)
            v_ref[h] = kv[:, QK_NOPE:].astype(BF16)
            vt_ref[h] = vt_all[h * V_HEAD:(h + 1) * V_HEAD]

    out_shapes = [
        jax.ShapeDtypeStruct((N_HEADS, QK_DIM, n_rows), BF16),
        jax.ShapeDtypeStruct((N_HEADS, n_rows, QK_DIM), BF16),
        jax.ShapeDtypeStruct((N_HEADS, n_rows, V_HEAD), BF16),
        jax.ShapeDtypeStruct((N_HEADS, V_HEAD, n_rows), BF16),
        jax.ShapeDtypeStruct((n_rows, Q_LORA), BF16),
        jax.ShapeDtypeStruct((n_rows, KV_LORA), BF16),
    ]
    tiles = [cq, ckv, kr]
    whole = [gq, gkv, wq_t, w_ukv, wv_t]
    out_specs = [_lane_tile(out_shapes[0].shape), _tile_spec(out_shapes[1].shape), _tile_spec(out_shapes[2].shape),
                 _lane_tile(out_shapes[3].shape), _tile_spec(out_shapes[4].shape), _tile_spec(out_shapes[5].shape)]
    return pl.pallas_call(
        body, name="fwd_qkv", grid=(nt,),
        in_specs=[_tile_spec(a.shape) for a in tiles] + [_whole_spec(a.shape) for a in whole]
        + [_tile_spec(cos.shape), _tile_spec(sin.shape), _lane_tile(cos_t.shape), _lane_tile(sin_t.shape)],
        out_specs=out_specs,
        out_shape=out_shapes,
        compiler_params=_params("parallel"),
    )(*tiles, *whole, cos, sin, cos_t, sin_t)


def _chunk_of(rows):
    return jnp.where(rows >= ROW_TILE, lax.shift_right_arithmetic(rows - ROW_TILE, CHUNK_SHIFT) + 1, 0)


def _visible(i, j):
    k_rows = j * ROW_TILE + lax.broadcasted_iota(jnp.int32, (ROW_TILE, 1), 0)
    q_rows = i * ROW_TILE + lax.broadcasted_iota(jnp.int32, (1, ROW_TILE), 1)
    return jnp.logical_and(_chunk_of(q_rows) >= _chunk_of(k_rows), k_rows >= DEAD)


def _attn_fwd(q_t, k, v_t, n_rows):
    nt = n_rows // ROW_TILE

    def body(qt_ref, k_ref, vt_ref, ot_ref, lse_ref, max_ref, sum_ref):
        i = pl.program_id(0)
        q_ts = [qt_ref[h] for h in range(N_HEADS)]

        def key_rows(j):
            return pl.ds(pl.multiple_of(j * ROW_TILE, ROW_TILE), ROW_TILE)

        def make_step(masked, tiles, first=0):
            def step(t, carry):
                js = [first + tiles * t + u for u in range(tiles)]
                scores = [[_dot(k_ref[h, key_rows(j), :], q_ts[h]) for h in range(N_HEADS)] for j in js]
                for j, tile_scores in zip(js, scores):
                    visible = _visible(i, j) if masked else None
                    probs, alphas = [], []
                    for h in range(N_HEADS):
                        m = max_ref[h]
                        s = jnp.where(visible, tile_scores[h], NEG) if masked else tile_scores[h]
                        m_new = jnp.maximum(m, jnp.max(s, axis=0, keepdims=True))
                        alpha = jnp.exp2(m - m_new)
                        p = jnp.exp2(s - m_new)
                        probs.append(p.astype(BF16))
                        alphas.append(alpha)
                        max_ref[h] = m_new
                        sum_ref[h] = alpha * sum_ref[h] + jnp.sum(p, axis=0, keepdims=True)
                    for h in range(N_HEADS):
                        ot_ref[h] = alphas[h] * ot_ref[h] + _dot(vt_ref[h, :, key_rows(j)], probs[h])
                return carry
            return step

        max_ref[...] = jnp.full(max_ref.shape, NEG, F32)
        sum_ref[...] = jnp.zeros_like(sum_ref)
        ot_ref[...] = jnp.zeros_like(ot_ref)
        between = jnp.maximum(i - 1, 0)
        quads = lax.shift_right_logical(between, 2)
        pairs = jnp.bitwise_and(lax.shift_right_logical(between, 1), 1)
        make_step(True, 1)(0, 0)
        lax.fori_loop(0, quads, make_step(False, 4, first=1), 0)
        lax.fori_loop(0, pairs, make_step(False, 2, first=1 + 4 * quads), 0)
        lax.fori_loop(1 + 4 * quads + 2 * pairs, i, make_step(False, 1), 0)
        lax.fori_loop(jnp.maximum(i, 1), i + 1, make_step(True, 1), 0)
        for h in range(N_HEADS):
            l = sum_ref[h]
            ot_ref[h] = ot_ref[h] / l
            lse_ref[h] = max_ref[h] + jnp.log2(l)

    out_shapes = [jax.ShapeDtypeStruct((N_HEADS, V_HEAD, n_rows), F32), jax.ShapeDtypeStruct((N_HEADS, 1, n_rows), F32)]
    return pl.pallas_call(
        body, name="attn_fwd", grid=(nt,),
        in_specs=[_lane_tile(q_t.shape), _whole_spec(k.shape), _whole_spec(v_t.shape)],
        out_specs=[_lane_tile(s.shape) for s in out_shapes],
        out_shape=out_shapes,
        scratch_shapes=[pltpu.VMEM((N_HEADS, 1, ROW_TILE), F32), pltpu.VMEM((N_HEADS, 1, ROW_TILE), F32)],
        compiler_params=_params("parallel"),
    )(q_t, k, v_t)


def _heads_to_rows(ref):
    return jnp.concatenate([ref[h] for h in range(N_HEADS)], axis=0)


def _rms_cols(x, g_col):
    r = lax.rsqrt(jnp.mean(x * x, axis=0, keepdims=True) + EPS)
    return x * r * g_col


def _fwd_out(x, meta_pad, mix_a, o_t, gb_col, w_out, n_rows):
    nt = n_rows // ROW_TILE

    def body(x_ref, meta_ref, mixa_ref, ot_ref, gb_ref, w_ref, mixbt_ref, h1_ref):
        i = pl.program_id(0)
        h0 = jnp.where(i == 0, meta_ref[...], x_ref[...])
        mix_bt = _rms_cols(_heads_to_rows(ot_ref), gb_ref[...]).astype(BF16)
        mixbt_ref[...] = mix_bt
        h1_ref[...] = h0 + _dot(mixa_ref[...], w_ref[:D_CONV, :]) + _dot_tn(mix_bt, w_ref[D_CONV:, :])

    out_shapes = [jax.ShapeDtypeStruct((D_ATTN, n_rows), BF16), jax.ShapeDtypeStruct((n_rows, D_MODEL), F32)]
    return pl.pallas_call(
        body, name="fwd_out", grid=(nt,),
        in_specs=[_real_spec(D_MODEL), _whole_spec(meta_pad.shape), _tile_spec(mix_a.shape), _lane_tile(o_t.shape),
                  _whole_spec(gb_col.shape), _whole_spec(w_out.shape)],
        out_specs=[_lane_tile(out_shapes[0].shape), _tile_spec(out_shapes[1].shape)],
        out_shape=out_shapes,
        compiler_params=_params("parallel"),
    )(x, meta_pad, mix_a, o_t, gb_col, w_out)


PHASES = 8
PHASE_ROWS = ROW_TILE // PHASES
UP_PLANES = -(-UP_SLAB // _LANES)
UP_PAD = UP_PLANES * _LANES
CONV_PLANES = D_CONV // _LANES


def _phase(start):
    return pl.ds(start, PHASE_ROWS, stride=PHASES)


def _to_planes(ref, lead, rows, value):
    width = value.shape[-1]
    for c in range(-(-width // _LANES)):
        part = value[:, c * _LANES:min((c + 1) * _LANES, width)]
        if part.shape[-1] < _LANES:
            part = jnp.concatenate([part, jnp.zeros((part.shape[0], _LANES - part.shape[-1]), part.dtype)], axis=-1)
        ref[(*lead, c, rows, slice(None))] = part


def _from_planes(ref, lead, width):
    planes = [ref[(*lead, c)] for c in range(-(-width // _LANES))]
    last = width - (len(planes) - 1) * _LANES
    return jnp.concatenate(planes[:-1] + [planes[-1][:, :last]], axis=-1)


def _fwd_ffn(h1, target, g2, w_up, fw, fb, w_down, gf, n_rows):
    nt = n_rows // ROW_TILE

    def body(h1_ref, t_ref, g2_ref, wup_ref, fw_ref, fb_ref, wdn_ref, gf_ref,
             n2_ref, up0_ref, act_ref, da_ref, db_ref, dh2_ref, loss_ref, dgf_ref, ext_ref):
        i = pl.program_id(0)

        @pl.when(i == 0)
        def _():
            n2_m = _rms_fwd(h1_ref[DEAD:, :], g2_ref[...]).astype(BF16)
            n2_ref[0:DEAD, :] = jnp.zeros((DEAD, D_MODEL), BF16)
            n2_ref[DEAD:, :] = n2_m
            for s in range(N_DEV):
                up0_m = _dot_nt(n2_m, wup_ref[s])
                up0_ref[s, 0:DEAD, :] = jnp.zeros((DEAD, UP_SLAB), BF16)
                up0_ref[s, DEAD:, :] = up0_m.astype(BF16)
                ext_ref[s, 0:FFN_HALO, :] = up0_m
            act_ref[...] = jnp.zeros_like(act_ref)
            da_ref[...] = jnp.zeros_like(da_ref)
            db_ref[...] = jnp.zeros_like(db_ref)
            dh2_ref[...] = jnp.zeros_like(dh2_ref)
            loss_ref[...] = jnp.zeros_like(loss_ref)
            dgf_ref[...] = jnp.zeros_like(dgf_ref)

        @pl.when(i > 0)
        def _():
            h1_t = h1_ref[...]
            n2 = _rms_fwd(h1_t, g2_ref[...]).astype(BF16)
            n2_ref[...] = n2
            for s in range(N_DEV):
                up0 = _dot_nt(n2, wup_ref[s])
                up0_ref[s] = up0.astype(BF16)
                ext_ref[s, FFN_HALO:, :] = up0

            def conv(s):
                block = ext_ref[s]
                acc = fb_ref[s, :, :UP_SLAB] + fw_ref[s, FFN_CONV_WIDTH - 1:FFN_CONV_WIDTH, :UP_SLAB] * block[FFN_HALO:]
                for back in range(1, FFN_CONV_WIDTH):
                    k = FFN_CONV_WIDTH - 1 - back
                    acc = acc + fw_ref[s, k:k + 1, :UP_SLAB] * pltpu.roll(block, back, 0)[FFN_HALO:]
                return acc

            h2 = h1_t
            for s in range(N_ACT_SLAB):
                gate = conv(s)
                val = conv(s + N_ACT_SLAB)
                sg = _sigmoid(gate)
                silu = gate * sg
                act = (silu * val).astype(BF16)
                act_ref[s] = act
                da_ref[s] = (val * sg * (1.0 + gate * (1.0 - sg))).astype(BF16)
                db_ref[s] = silu.astype(BF16)
                h2 = h2 + _dot(act, wdn_ref[s])
            ext_ref[:, 0:FFN_HALO, :] = ext_ref[:, ROW_TILE:ROW_TILE + FFN_HALO, :]

            gf_t = gf_ref[...]
            diff = _rms_fwd(h2, gf_t) - t_ref[...]
            tile_loss = 0.5 * jnp.sum(jnp.sum(diff * diff, axis=-1, keepdims=True), axis=0, keepdims=True) / D_MODEL
            dh2, dgf = _rms_bwd(diff / D_MODEL, h2, gf_t)
            dh2_ref[...] = dh2
            loss_ref[...] += jnp.broadcast_to(tile_loss, loss_ref.shape)
            dgf_ref[...] += dgf

    act_like = jax.ShapeDtypeStruct((N_ACT_SLAB, n_rows, UP_SLAB), BF16)
    out_shapes = [
        jax.ShapeDtypeStruct((n_rows, D_MODEL), BF16),
        jax.ShapeDtypeStruct((N_DEV, n_rows, UP_SLAB), BF16),
        act_like, act_like, act_like,
        jax.ShapeDtypeStruct((n_rows, D_MODEL), F32),
        jax.ShapeDtypeStruct((8, 128), F32),
        jax.ShapeDtypeStruct((1, D_MODEL), F32),
    ]
    whole = [g2, w_up, fw, fb, w_down, gf]
    return pl.pallas_call(
        body, name="fwd_ffn", grid=(nt,),
        in_specs=[_tile_spec(h1.shape), _real_spec(D_MODEL)] + [_whole_spec(a.shape) for a in whole],
        out_specs=[_tile_spec(s.shape) for s in out_shapes[:6]] + [_acc_spec(s.shape) for s in out_shapes[6:]],
        out_shape=out_shapes,
        scratch_shapes=[pltpu.VMEM((N_DEV, ROW_TILE + FFN_HALO, UP_SLAB), F32)],
        compiler_params=_params("arbitrary"),
    )(h1, target, *whole)


def _rope_tables(n_rows):
    pos = jnp.maximum(jnp.arange(n_rows, dtype=jnp.int32) - DEAD, 0)
    inv_freq = 1.0 / (ROPE_THETA ** (jnp.arange(0, QK_ROPE, 2, dtype=F32) / QK_ROPE))
    ang_t = inv_freq[:, None] * pos.astype(F32)[None, :]
    return jnp.cos(ang_t), jnp.sin(ang_t)


def _halo_after(shape, halo, n_rows):
    last = n_rows // halo - 1
    step = ROW_TILE // halo
    if len(shape) == 2:
        return pl.BlockSpec((halo, shape[1]), lambda i: (jnp.minimum((i + 1) * step, last), 0))
    return pl.BlockSpec((shape[0], halo, shape[2]), lambda i: (0, jnp.minimum((i + 1) * step, last), 0))


def _bwd_ffn_act(dh2, da, db, act, w_down, n_rows):
    nt = n_rows // ROW_TILE

    def body(dh2_ref, da_ref, db_ref, act_ref, wdn_ref, dup_ref, dfb_ref, dwd_ref, acc_ref):
        i = pl.program_id(0)

        @pl.when(i == 0)
        def _():
            dfb_ref[...] = jnp.zeros_like(dfb_ref)
            dup_ref[...] = jnp.zeros_like(dup_ref)
            acc_ref[...] = jnp.zeros_like(acc_ref)

        @pl.when(i > 0)
        def _():
            dh2_b = dh2_ref[...].astype(BF16)
            for s in range(N_ACT_SLAB):
                d_act = _dot_nt(dh2_b, wdn_ref[s])
                d_gate = d_act * da_ref[s].astype(F32)
                d_val = d_act * db_ref[s].astype(F32)
                dup_ref[s] = d_gate.astype(BF16)
                dup_ref[s + N_ACT_SLAB] = d_val.astype(BF16)
                dfb_ref[s] += jnp.sum(d_gate, axis=0, keepdims=True)
                dfb_ref[s + N_ACT_SLAB] += jnp.sum(d_val, axis=0, keepdims=True)
                acc_ref[s] += _dot_tn(act_ref[s], dh2_b)

        @pl.when(i == nt - 1)
        def _():
            dwd_ref[...] = acc_ref[...].astype(BF16)

    out_shapes = [jax.ShapeDtypeStruct((N_DEV, n_rows, UP_SLAB), BF16), jax.ShapeDtypeStruct((N_DEV, 1, UP_SLAB), F32),
                  jax.ShapeDtypeStruct((N_ACT_SLAB, UP_SLAB, D_MODEL), BF16)]
    return pl.pallas_call(
        body, name="bwd_ffn_act", grid=(nt,),
        in_specs=[_tile_spec(dh2.shape), _tile_spec(da.shape), _tile_spec(db.shape), _tile_spec(act.shape),
                  _whole_spec(w_down.shape)],
        out_specs=[_tile_spec(out_shapes[0].shape), _acc_spec(out_shapes[1].shape), _acc_spec(out_shapes[2].shape)],
        out_shape=out_shapes,
        scratch_shapes=[pltpu.VMEM((N_ACT_SLAB, UP_SLAB, D_MODEL), F32)],
        compiler_params=_params("arbitrary"),
    )(dh2, da, db, act, w_down)


def _bwd_ffn_up(dup, up0, h1, dh2, g2, w_up, fw, mix_a, mix_bt, n_rows):
    nt = n_rows // ROW_TILE
    last_tap = FFN_CONV_WIDTH - 1
    half = mix_a.shape[1]

    def body(dup_ref, dnext_ref, up0_ref, h1_ref, dh2_ref, g2_ref, wup_ref, fw_ref, mixa_ref, mixbt_ref,
             dup0_ref, dh1_ref, dfw_ref, dg2_ref, dwo_ref, acc_ref):
        i = pl.program_id(0)

        def w_out_grad(dh1_b):
            return _dot_tn(mixa_ref[...].astype(BF16), dh1_b), _dot(mixbt_ref[...].astype(BF16), dh1_b)

        def conv_transpose(s, rows, d, after, u):
            block = jnp.concatenate([d, after], axis=0)
            dup0 = fw_ref[s, last_tap:last_tap + 1, :UP_SLAB] * d
            dfw_ref[s, last_tap:last_tap + 1, :UP_SLAB] += jnp.sum(d * u, axis=0, keepdims=True)
            for ahead in range(1, FFN_CONV_WIDTH):
                k = last_tap - ahead
                shifted = pltpu.roll(block, rows + FFN_HALO - ahead, 0)[:rows]
                dup0 = dup0 + fw_ref[s, k:k + 1, :UP_SLAB] * shifted
                dfw_ref[s, k:k + 1, :UP_SLAB] += jnp.sum(shifted * u, axis=0, keepdims=True)
            return dup0.astype(BF16)

        @pl.when(i == 0)
        def _():
            dfw_ref[...] = jnp.zeros_like(dfw_ref)
            dn2_m = jnp.zeros((N_META, D_MODEL), F32)
            for s in range(N_DEV):
                dup0_m = conv_transpose(s, N_META, dup_ref[s, DEAD:, :].astype(F32), dnext_ref[s].astype(F32),
                                        up0_ref[s, DEAD:, :].astype(F32))
                dup0_ref[s, 0:DEAD, :] = jnp.zeros((DEAD, UP_SLAB), BF16)
                dup0_ref[s, DEAD:, :] = dup0_m
                dn2_m = dn2_m + _dot(dup0_m, wup_ref[s])
            dx_m, dg2 = _rms_bwd(dn2_m, h1_ref[DEAD:, :], g2_ref[...])
            dh1_m = dh2_ref[DEAD:, :] + dx_m
            dh1_ref[0:DEAD, :] = jnp.zeros((DEAD, D_MODEL), F32)
            dh1_ref[DEAD:, :] = dh1_m
            dg2_ref[...] = dg2
            conv_half, attn_half = w_out_grad(
                jnp.concatenate([jnp.zeros((DEAD, D_MODEL), BF16), dh1_m.astype(BF16)], axis=0))
            acc_ref[0:half, :] = conv_half
            acc_ref[half:, :] = attn_half

        @pl.when(i > 0)
        def _():
            dn2 = jnp.zeros((ROW_TILE, D_MODEL), F32)
            for s in range(N_DEV):
                after = jnp.where(i == nt - 1, 0.0, dnext_ref[s].astype(F32))
                dup0_b = conv_transpose(s, ROW_TILE, dup_ref[s].astype(F32), after, up0_ref[s].astype(F32))
                dup0_ref[s] = dup0_b
                dn2 = dn2 + _dot(dup0_b, wup_ref[s])
            dx, dg2 = _rms_bwd(dn2, h1_ref[...], g2_ref[...])
            dh1 = dh2_ref[...] + dx
            dh1_ref[...] = dh1
            dg2_ref[...] += dg2
            conv_half, attn_half = w_out_grad(dh1.astype(BF16))
            acc_ref[0:half, :] += conv_half
            acc_ref[half:, :] += attn_half

        @pl.when(i == nt - 1)
        def _():
            dwo_ref[...] = acc_ref[...].astype(BF16)

    out_shapes = [
        jax.ShapeDtypeStruct((N_DEV, n_rows, UP_SLAB), BF16),
        jax.ShapeDtypeStruct((n_rows, D_MODEL), F32),
        jax.ShapeDtypeStruct((N_DEV, FFN_CONV_WIDTH, UP_PAD), F32),
        jax.ShapeDtypeStruct((1, D_MODEL), F32),
        jax.ShapeDtypeStruct((half + mix_bt.shape[0], D_MODEL), BF16),
    ]
    return pl.pallas_call(
        body, name="bwd_ffn_up", grid=(nt,),
        in_specs=[_tile_spec(dup.shape), _halo_after(dup.shape, FFN_HALO, n_rows), _tile_spec(up0.shape),
                  _tile_spec(h1.shape), _tile_spec(dh2.shape),
                  _whole_spec(g2.shape), _whole_spec(w_up.shape), _whole_spec(fw.shape),
                  _tile_spec(mix_a.shape), _lane_tile(mix_bt.shape)],
        out_specs=[_tile_spec(s.shape) for s in out_shapes[:2]] + [_acc_spec(s.shape) for s in out_shapes[2:]],
        out_shape=out_shapes,
        scratch_shapes=[pltpu.VMEM((half + mix_bt.shape[0], D_MODEL), F32)],
        compiler_params=_params("arbitrary"),
    )(dup, dup, up0, h1, dh2, g2, w_up, fw, mix_a, mix_bt)


def _bwd_out(dh1, o_t, u1, w_out, gb_col, ln_g, ln_b, ga, n_rows):
    nt = n_rows // ROW_TILE

    def body(dh1_ref, ot_ref, u1_ref, w_ref, gb_ref, lg_ref, lb_ref, ga_ref,
             dot_ref, delta_ref, du1_ref, dgb_ref, dga_ref, dlg_ref, dlb_ref, dcb_ref):
        i = pl.program_id(0)
        dh1_b = dh1_ref[...].astype(BF16)
        o_t = _heads_to_rows(ot_ref)
        gb = gb_ref[...]
        r = lax.rsqrt(jnp.mean(o_t * o_t, axis=0, keepdims=True) + EPS)
        dmix_bt = _dot_nt(w_ref[D_CONV:, :], dh1_b)
        wgt = dmix_bt * gb
        do_t = r * wgt - o_t * (r * r * r) * jnp.mean(wgt * o_t, axis=0, keepdims=True)
        dgb = jnp.sum(dmix_bt * o_t * r, axis=1, keepdims=True)
        for h in range(N_HEADS):
            do_h = do_t[h * V_HEAD:(h + 1) * V_HEAD]
            dot_ref[h] = do_h.astype(BF16)
            delta_ref[h] = jnp.sum(do_h * ot_ref[h], axis=0, keepdims=True)
        lg = lg_ref[...]
        xh, u2, u3, rstd = _conv_chain(u1_ref[...], lg, lb_ref[...])
        du3, dga = _rms_bwd(_dot_nt(dh1_b, w_ref[:D_CONV, :]), u3, ga_ref[...])
        sg = _sigmoid(u2)
        du2 = du3 * sg * (1.0 + u2 * (1.0 - sg))
        dxh = du2 * lg
        du1 = rstd * (dxh - jnp.mean(dxh, axis=-1, keepdims=True) - xh * jnp.mean(dxh * xh, axis=-1, keepdims=True))
        du1_ref[...] = du1
        first = i == 0
        _accumulate(dgb_ref, first, dgb)
        _accumulate(dga_ref, first, dga)
        _accumulate(dlg_ref, first, jnp.sum(du2 * xh, axis=0, keepdims=True))
        _accumulate(dlb_ref, first, jnp.sum(du2, axis=0, keepdims=True))
        _accumulate(dcb_ref, first, jnp.sum(du1, axis=0, keepdims=True))

    out_shapes = [
        jax.ShapeDtypeStruct((N_HEADS, V_HEAD, n_rows), BF16),
        jax.ShapeDtypeStruct((N_HEADS, 1, n_rows), F32),
        jax.ShapeDtypeStruct((n_rows, D_CONV), F32),
        jax.ShapeDtypeStruct((D_ATTN, 1), F32),
    ] + [jax.ShapeDtypeStruct((1, D_CONV), F32)] * 4
    whole = [w_out, gb_col, ln_g, ln_b, ga]
    return pl.pallas_call(
        body, name="bwd_out", grid=(nt,),
        in_specs=[_tile_spec(dh1.shape), _lane_tile(o_t.shape), _tile_spec(u1.shape)] + [_whole_spec(a.shape) for a in whole],
        out_specs=[_lane_tile(out_shapes[0].shape), _lane_tile(out_shapes[1].shape), _tile_spec(out_shapes[2].shape)]
        + [_acc_spec(s.shape) for s in out_shapes[3:]],
        out_shape=out_shapes,
        compiler_params=_params("arbitrary"),
    )(dh1, o_t, u1, *whole)


ATTN_BWD_HEADS = 8


def _attn_bwd(q_t, k, v, do_t, lse, delta, n_rows):
    nt = n_rows // ROW_TILE
    hp = ATTN_BWD_HEADS

    def body(k_ref, v_ref, qt_ref, dot_ref, lse_ref, delta_ref, dqt_ref, dk_ref, dv_ref):
        j = pl.program_id(1)

        @pl.when(j == 0)
        def _():
            dqt_ref[...] = jnp.zeros_like(dqt_ref)

        k_ts = [k_ref[h] for h in range(hp)]
        v_ts = [v_ref[h] for h in range(hp)]

        def make_step(masked, tiles, first=0):
            def step(t, carry):
                tiles_of_step = []
                for u in range(tiles):
                    i = first + tiles * t + u
                    cols = pl.ds(pl.multiple_of(i * ROW_TILE, ROW_TILE), ROW_TILE)
                    q_is = [qt_ref[h, :, cols] for h in range(hp)]
                    do_is = [dot_ref[h, :, cols] for h in range(hp)]
                    scores = [_dot(k_ts[h], q_is[h]) for h in range(hp)]
                    dps = [_dot(v_ts[h], do_is[h]) for h in range(hp)]
                    tiles_of_step.append((i, cols, q_is, do_is, scores, dps))
                for i, cols, q_is, do_is, scores, dps in tiles_of_step:
                    visible = _visible(i, j) if masked else None
                    probs, dss = [], []
                    for h in range(hp):
                        s = jnp.where(visible, scores[h], NEG) if masked else scores[h]
                        p = jnp.exp2(s - lse_ref[h, :, cols])
                        probs.append(p.astype(BF16))
                        dss.append((p * (dps[h] - delta_ref[h, :, cols])).astype(BF16))
                    for h in range(hp):
                        dv_ref[h] += _dot_nt(probs[h], do_is[h])
                        dk_ref[h] += _dot_nt(dss[h], q_is[h])
                        dqt_ref[h, :, cols] += _dot_tn(k_ts[h], dss[h])
                return carry
            return step

        dk_ref[...] = jnp.zeros_like(dk_ref)
        dv_ref[...] = jnp.zeros_like(dv_ref)
        make_step(True, 1)(j, 0)
        lax.fori_loop(jnp.where(j == 0, j + 1, nt), nt, make_step(True, 1), 0)
        unmasked = jnp.where(j == 0, 0, nt - 1 - j)
        quads = lax.shift_right_logical(unmasked, 2)
        pairs = jnp.bitwise_and(lax.shift_right_logical(unmasked, 1), 1)
        lax.fori_loop(0, quads, make_step(False, 4, first=j + 1), 0)
        lax.fori_loop(0, pairs, make_step(False, 2, first=j + 1 + 4 * quads), 0)
        lax.fori_loop(jnp.where(j == 0, nt, j + 1 + 4 * quads + 2 * pairs), nt, make_step(False, 1), 0)
        dk_ref[...] = dk_ref[...] * _LN2

    key_tile = lambda w: pl.BlockSpec((hp, ROW_TILE, w), lambda g, j: (g, j, 0))
    all_cols = lambda w: pl.BlockSpec((hp, w, n_rows), lambda g, j: (g, 0, 0))
    resident = lambda w: pl.BlockSpec((hp, w, n_rows), lambda g, j: (g, 0, 0), pipeline_mode=pl.Buffered(1))
    out_shapes = [
        jax.ShapeDtypeStruct((N_HEADS, QK_DIM, n_rows), F32),
        jax.ShapeDtypeStruct((N_HEADS, n_rows, QK_DIM), F32),
        jax.ShapeDtypeStruct((N_HEADS, n_rows, V_HEAD), F32),
    ]
    return pl.pallas_call(
        body, name="attn_bwd", grid=(N_HEADS // hp, nt),
        in_specs=[key_tile(QK_DIM), key_tile(V_HEAD), resident(QK_DIM), resident(V_HEAD), resident(1), resident(1)],
        out_specs=[all_cols(QK_DIM), key_tile(QK_DIM), key_tile(V_HEAD)],
        out_shape=out_shapes,
        compiler_params=_params("parallel", "arbitrary"),
    )(k, v, q_t, do_t, lse, delta)


def _bwd_qkv(dq_t, dk, dv, cq, ckv, ckvn, gq, gkv, wq_t, w_ukv, cos, sin, cos_t, sin_t, n_rows):
    nt = n_rows // ROW_TILE

    def body(dqt_ref, dk_ref, dv_ref, cq_ref, ckv_ref, ckvn_ref, gq_ref, gkv_ref, wqt_ref, wkv_ref, cos_ref, sin_ref,
             cost_ref, sint_ref, dqraw_ref, dkv_ref, dcq_ref, dckv_ref, dkr_ref, dgq_ref, dgkv_ref, dwkv_ref, acc_ref):
        i = pl.program_id(0)
        cos_rows, sin_rows = cost_ref[...], sint_ref[...]
        dcqn = jnp.zeros((ROW_TILE, Q_LORA), F32)
        dckvn = jnp.zeros((ROW_TILE, KV_LORA), F32)
        dk_rot = jnp.zeros((ROW_TILE, QK_ROPE), F32)
        for h in range(N_HEADS):
            dq_h, dk_h = dqt_ref[h] * QK_DIM ** -0.5, dk_ref[h]
            dq_raw = jnp.concatenate(
                [dq_h[:QK_NOPE], _rope_rows_t(dq_h[QK_NOPE:], cos_rows, sin_rows)], axis=0).astype(BF16)
            dqraw_ref[h] = dq_raw
            dcqn = dcqn + _dot_tn(dq_raw, wqt_ref[h])
            dkv = jnp.concatenate([dk_h[:, :QK_NOPE], dv_ref[h]], axis=-1).astype(BF16)
            dkv_ref[:, h * KV_HEAD:(h + 1) * KV_HEAD] = dkv
            dckvn = dckvn + _dot_nt(dkv, wkv_ref[h])
            dk_rot = dk_rot + dk_h[:, QK_NOPE:]
        dkr_ref[...] = _rope_t(dk_rot, cos_ref[...], sin_ref[...]).astype(BF16)
        dcq, dgq = _rms_bwd(dcqn, cq_ref[...], gq_ref[...])
        dckv, dgkv = _rms_bwd(dckvn, ckv_ref[...], gkv_ref[...])
        dcq_ref[...] = dcq.astype(BF16)
        dckv_ref[...] = dckv.astype(BF16)
        _accumulate(dgq_ref, i == 0, dgq)
        _accumulate(dgkv_ref, i == 0, dgkv)
        _accumulate(acc_ref, i == 0, _dot_tn(ckvn_ref[...].astype(BF16), dkv_ref[...]))

        @pl.when(i == nt - 1)
        def _():
            dwkv_ref[...] = acc_ref[...].astype(BF16)

    out_shapes = [
        jax.ShapeDtypeStruct((N_HEADS, QK_DIM, n_rows), BF16),
        jax.ShapeDtypeStruct((n_rows, N_HEADS * KV_HEAD), BF16),
        jax.ShapeDtypeStruct((n_rows, Q_LORA), BF16),
        jax.ShapeDtypeStruct((n_rows, KV_LORA), BF16),
        jax.ShapeDtypeStruct((n_rows, QK_ROPE), BF16),
        jax.ShapeDtypeStruct((1, Q_LORA), F32),
        jax.ShapeDtypeStruct((1, KV_LORA), F32),
        jax.ShapeDtypeStruct((KV_LORA, N_HEADS * KV_HEAD), BF16),
    ]
    tiles = [dk, dv, cq, ckv, ckvn]
    whole = [gq, gkv, wq_t, w_ukv]
    return pl.pallas_call(
        body, name="bwd_qkv", grid=(nt,),
        in_specs=[_lane_tile(dq_t.shape)] + [_tile_spec(a.shape) for a in tiles] + [_whole_spec(a.shape) for a in whole]
        + [_tile_spec(cos.shape), _tile_spec(sin.shape), _lane_tile(cos_t.shape), _lane_tile(sin_t.shape)],
        out_specs=[_lane_tile(out_shapes[0].shape)] + [_tile_spec(s.shape) for s in out_shapes[1:5]]
        + [_acc_spec(s.shape) for s in out_shapes[5:]],
        out_shape=out_shapes,
        scratch_shapes=[pltpu.VMEM((KV_LORA, N_HEADS * KV_HEAD), F32)],
        compiler_params=_params("arbitrary"),
    )(dq_t, *tiles, *whole, cos, sin, cos_t, sin_t)


def _bwd_conv(du1, ag, conv_w, dcq, dckv, dkr, n_rows):
    nt = n_rows // ROW_TILE

    last_tap = CONV_WIDTH - 1

    def body(du1_ref, dnext_ref, ag_ref, w_ref, dcq_ref, dckv_ref, dkr_ref, dz_ref, dw_ref,
             dext_ref, uext_ref, conv_ref, sums_ref):
        i = pl.program_id(0)

        @pl.when(i == 0)
        def _():
            sums_ref[...] = jnp.zeros_like(sums_ref)

        _to_planes(dext_ref, (), slice(0, ROW_TILE), du1_ref[...])
        _to_planes(dext_ref, (), slice(ROW_TILE, None), jnp.where(i == nt - 1, 0.0, dnext_ref[...]))
        ag_t = ag_ref[...]
        live = _row_ids(i, ROW_TILE) >= DEAD
        sg = _sigmoid(ag_t[:, D_CONV:])
        _to_planes(uext_ref, (), slice(None), jnp.where(live, ag_t[:, :D_CONV] * sg, 0.0))
        for c in range(CONV_PLANES):
            taps = w_ref[:, c * _LANES:(c + 1) * _LANES]
            for half in range(0, PHASES, PHASES // 2):
                phases = range(half, half + PHASES // 2)
                us = {p: uext_ref[c, _phase(p), :] for p in phases}
                accs = {p: jnp.zeros((PHASE_ROWS, _LANES), F32) for p in phases}
                for k in range(CONV_WIDTH):
                    tap_sum = jnp.zeros((PHASE_ROWS, _LANES), F32)
                    for p in phases:
                        shifted = dext_ref[c, _phase(p + last_tap - k), :]
                        accs[p] = accs[p] + taps[k:k + 1, :] * shifted
                        tap_sum = tap_sum + shifted * us[p]
                    sums_ref[c, k] += tap_sum
                for p in phases:
                    conv_ref[c, _phase(p), :] = accs[p]
        du0 = jnp.where(live, _from_planes(conv_ref, (), D_CONV), 0.0)
        da = du0 * sg
        dgate = du0 * ag_t[:, :D_CONV] * sg * (1.0 - sg)
        dz_ref[...] = jnp.concatenate(
            [da.astype(BF16), dgate.astype(BF16), dcq_ref[...], dckv_ref[...], dkr_ref[...]], axis=-1)

        @pl.when(i == nt - 1)
        def _():
            for c in range(CONV_PLANES):
                for k in range(CONV_WIDTH):
                    dw_ref[k:k + 1, c * _LANES:(c + 1) * _LANES] = jnp.sum(sums_ref[c, k], axis=0, keepdims=True)

    out_shapes = [jax.ShapeDtypeStruct((n_rows, D_IN), BF16), jax.ShapeDtypeStruct((CONV_WIDTH, D_CONV), F32)]
    return pl.pallas_call(
        body, name="bwd_conv", grid=(nt,),
        in_specs=[_tile_spec(du1.shape), _halo_after(du1.shape, CONV_HALO, n_rows), _tile_spec(ag.shape),
                  _whole_spec(conv_w.shape), _tile_spec(dcq.shape), _tile_spec(dckv.shape), _tile_spec(dkr.shape)],
        out_specs=[_tile_spec(out_shapes[0].shape), _acc_spec(out_shapes[1].shape)],
        out_shape=out_shapes,
        scratch_shapes=[pltpu.VMEM((CONV_PLANES, ROW_TILE + CONV_HALO, _LANES), F32),
                        pltpu.VMEM((CONV_PLANES, ROW_TILE, _LANES), F32), pltpu.VMEM((CONV_PLANES, ROW_TILE, _LANES), F32),
                        pltpu.VMEM((CONV_PLANES, CONV_WIDTH, PHASE_ROWS, _LANES), F32)],
        compiler_params=_params("arbitrary"),
    )(du1, du1, ag, conv_w, dcq, dckv, dkr)


def _bwd_in(dz, x, meta_pad, dh1, g1, w_in, n_rows):
    nt = n_rows // ROW_TILE

    def body(dz_ref, x_ref, meta_ref, dh1_ref, g_ref, w_ref, gx_ref, gmeta_ref, dg1_ref):
        i = pl.program_id(0)
        h0 = jnp.where(i == 0, meta_ref[...], x_ref[...])
        dx, dg1 = _rms_bwd(_dot(dz_ref[...], w_ref[...]), h0, g_ref[...])
        dh0 = dh1_ref[...] + dx
        gx_ref[...] = dh0

        @pl.when(i == 0)
        def _():
            gmeta_ref[...] = dh0

        _accumulate(dg1_ref, i == 0, dg1)

    out_shapes = [
        jax.ShapeDtypeStruct((n_rows - ROW_TILE, D_MODEL), F32),
        jax.ShapeDtypeStruct((ROW_TILE, D_MODEL), F32),
        jax.ShapeDtypeStruct((1, D_MODEL), F32),
    ]
    return pl.pallas_call(
        body, name="bwd_in", grid=(nt,),
        in_specs=[_tile_spec(dz.shape), _real_spec(D_MODEL), _whole_spec(meta_pad.shape), _tile_spec(dh1.shape),
                  _whole_spec(g1.shape), _whole_spec(w_in.shape)],
        out_specs=[_real_spec(D_MODEL), _acc_spec(out_shapes[1].shape), _acc_spec(out_shapes[2].shape)],
        out_shape=out_shapes,
        compiler_params=_params("arbitrary"),
    )(dz, x, meta_pad, dh1, g1, w_in)


def _contraction_tile(n_rows):
    return next(t for t in range(n_rows // 2 // _LANES * _LANES, 0, -_LANES) if n_rows % t == 0)


def _weight_grad(a, b, name, a_transposed=False):
    groups = max(a.shape[0] if a.ndim == 3 else 1, b.shape[0] if b.ndim == 3 else 1)
    n_rows, n = b.shape[-2], b.shape[-1]
    m = a.shape[-2] if a_transposed else a.shape[-1]
    kt = _contraction_tile(n_rows)
    steps = n_rows // kt

    def body(a_ref, b_ref, out_ref, acc_ref):
        i = pl.program_id(1)
        a_t, b_t = a_ref[...].astype(BF16), b_ref[...].astype(BF16)
        part = _dot(a_t, b_t) if a_transposed else _dot_tn(a_t, b_t)
        _accumulate(acc_ref, i == 0, part)

        @pl.when(i == steps - 1)
        def _():
            out_ref[...] = acc_ref[...].astype(out_ref.dtype)

    def spec(arr, rows_last):
        block = (arr.shape[-2], kt) if rows_last else (kt, arr.shape[-1])
        at = (lambda i: (0, i)) if rows_last else (lambda i: (i, 0))
        if arr.ndim == 3:
            return pl.BlockSpec((None,) + block, lambda g, i: (g,) + at(i))
        return pl.BlockSpec(block, lambda g, i: at(i))

    return pl.pallas_call(
        body, name=name, grid=(groups, steps),
        in_specs=[spec(a, a_transposed), spec(b, False)],
        out_specs=pl.BlockSpec((None, m, n), lambda g, i: (g, 0, 0)),
        out_shape=jax.ShapeDtypeStruct((groups, m, n), BF16),
        scratch_shapes=[pltpu.VMEM((m, n), F32)],
        compiler_params=_params("parallel", "arbitrary"),
    )(a, b)


def _my_index():
    return 4 * lax.axis_index("x") + 2 * lax.axis_index("y") + lax.axis_index("c")


def _peer(k):
    flip = lambda v, bit: 1 - v if bit else v
    px = flip(lax.axis_index("x"), k & 4)
    py = flip(lax.axis_index("y"), k & 2)
    pc = flip(lax.axis_index("c"), k & 1)
    return (px, py, pc), 4 * px + 2 * py + pc


def _all_gather(shards, dtypes):
    n = len(shards)
    sibling, chips = 1, (2, 4, 6)

    def body(*refs):
        ins, outs, stages = refs[:n], refs[n:2 * n], refs[2 * n:3 * n]
        send_sems, recv_sems, local_sems = refs[3 * n:]
        me = _my_index()
        for a in range(n):
            stages[a][...] = ins[a][...].astype(stages[a].dtype)
        local = [pltpu.make_async_copy(stages[a], outs[a].at[me], local_sems.at[a]) for a in range(n)]
        for cp in local:
            cp.start()

        def copy(a, k, src, slot, to):
            return pltpu.make_async_remote_copy(
                src_ref=src, dst_ref=outs[a].at[slot], send_sem=send_sems.at[a, k - 1],
                recv_sem=recv_sems.at[a, k - 1], device_id=_peer(to)[0], device_id_type=MESH)

        def own(a, k):
            return copy(a, k, stages[a], me, k)

        def passed(a, k):
            slot = _peer(k)[1]
            return copy(a, k ^ sibling, outs[a].at[slot], slot, sibling)

        def arrival(a, k):
            return copy(a, k, stages[a], _peer(k)[1], k)

        for k in (sibling,) + chips:
            for a in range(n):
                own(a, k).start()
        for k in chips:
            for a in range(n):
                arrival(a, k).wait_recv()
                passed(a, k).start()
        for a in range(n):
            arrival(a, sibling).wait_recv()
            for k in chips:
                arrival(a, k ^ sibling).wait_recv()
        for a in range(n):
            for k in (sibling,) + chips:
                own(a, k).wait_send()
            for k in chips:
                passed(a, k).wait_send()
        for cp in local:
            cp.wait()

    return pl.pallas_call(
        body, name="gather_weights",
        in_specs=[pl.BlockSpec(memory_space=pltpu.VMEM)] * n,
        out_specs=[pl.BlockSpec(memory_space=pl.ANY)] * n,
        out_shape=[jax.ShapeDtypeStruct((N_DEV,) + s.shape, dt) for s, dt in zip(shards, dtypes)],
        scratch_shapes=[pltpu.VMEM(s.shape, dt) for s, dt in zip(shards, dtypes)]
        + [pltpu.SemaphoreType.DMA((n, N_DEV - 1)), pltpu.SemaphoreType.DMA((n, N_DEV - 1)), pltpu.SemaphoreType.DMA((n,))],
        compiler_params=pltpu.CompilerParams(vmem_limit_bytes=VMEM_LIMIT),
    )(*shards)


def _exchange(parts, whole):
    n = len(parts)

    def body(*refs):
        ins, outs = refs[:n], refs[n:2 * n]
        send_sems, recv_sems, local_sems = refs[2 * n:]
        me = _my_index()

        def src(a, slab):
            return ins[a] if whole[a] else ins[a].at[slab]

        local = [pltpu.make_async_copy(src(a, me), outs[a].at[me], local_sems.at[a]) for a in range(n)]
        for cp in local:
            cp.start()

        def copy(a, k, slab, slot):
            peer, _ = _peer(k)
            return pltpu.make_async_remote_copy(
                src_ref=src(a, slab), dst_ref=outs[a].at[slot], send_sem=send_sems.at[a, k - 1],
                recv_sem=recv_sems.at[a, k - 1], device_id=peer, device_id_type=MESH)

        for k in range(1, N_DEV):
            for a in range(n):
                copy(a, k, _peer(k)[1], me).start()
        for k in range(1, N_DEV):
            for a in range(n):
                copy(a, k, _peer(k)[1], _peer(k)[1]).wait()
        for cp in local:
            cp.wait()

    return pl.pallas_call(
        body, name="exchange_grads",
        in_specs=[pl.BlockSpec(memory_space=pl.ANY)] * n,
        out_specs=[pl.BlockSpec(memory_space=pl.ANY)] * n,
        out_shape=[jax.ShapeDtypeStruct(((N_DEV,) + p.shape) if w else p.shape, p.dtype) for p, w in zip(parts, whole)],
        scratch_shapes=[pltpu.SemaphoreType.DMA((n, N_DEV - 1)), pltpu.SemaphoreType.DMA((n, N_DEV - 1)),
                        pltpu.SemaphoreType.DMA((n,))],
    )(*parts)


def _sequencer_exchange(parts, whole, name, collective_id):
    n = len(parts)
    srcs = [jax.new_ref(p, memory_space=pltpu.MemorySpace.HBM) for p in parts]
    lands = [jax.empty_ref(jax.ShapeDtypeStruct(((N_DEV,) + p.shape) if w else p.shape, p.dtype),
                           memory_space=pltpu.MemorySpace.HBM) for p, w in zip(parts, whole)]

    @pl.kernel(mesh=plsc.ScalarSubcoreMesh(axis_name="sequencer", num_cores=1), name=name,
               scratch_types=(pltpu.SemaphoreType.DMA((n, N_DEV - 1)), pltpu.SemaphoreType.DMA((n, N_DEV - 1)),
                              pltpu.SemaphoreType.DMA((n,))),
               compiler_params=pltpu.CompilerParams(collective_id=collective_id))
    def launch(send_sems, recv_sems, local_sems):
        barrier = pltpu.get_barrier_semaphore()
        for k in range(1, N_DEV):
            pl.semaphore_signal(barrier, inc=1, device_id=_peer(k)[0], device_id_type=MESH)
        pl.semaphore_wait(barrier, N_DEV - 1)
        me = _my_index()

        def src(a, slab):
            return srcs[a] if whole[a] else srcs[a].at[slab]

        local = [pltpu.make_async_copy(src(a, me), lands[a].at[me], local_sems.at[a]) for a in range(n)]
        for cp in local:
            cp.start()

        def copy(a, k, slab, slot):
            return pltpu.make_async_remote_copy(
                src_ref=src(a, slab), dst_ref=lands[a].at[slot], send_sem=send_sems.at[a, k - 1],
                recv_sem=recv_sems.at[a, k - 1], device_id=_peer(k)[0], device_id_type=MESH)

        for k in range(1, N_DEV):
            for a in range(n):
                copy(a, k, _peer(k)[1], me).start()
        for k in range(1, N_DEV):
            for a in range(n):
                copy(a, k, _peer(k)[1], _peer(k)[1]).wait()
        for cp in local:
            cp.wait()

    launch()
    return [land[...] for land in lands]


def _row_block(rows):
    if rows <= ROW_TILE:
        return rows
    return next(rb for rb in range(ROW_TILE, 0, -16) if rows % rb == 0)


def _adamw(landing, w, m, v, name):
    rows, cols = w.shape
    rb = _row_block(rows)

    def body(l_ref, w_ref, m_ref, v_ref, g_ref, d_ref, m2_ref, v2_ref):
        g = l_ref[0].astype(F32)
        for p in range(1, N_DEV):
            g = g + l_ref[p].astype(F32)
        g_ref[...] = g
        d_ref[...], m2_ref[...], v2_ref[...] = _adamw_step(g, w_ref[...], m_ref[...], v_ref[...])

    flat = pl.BlockSpec((rb, cols), lambda i: (i, 0))
    return pl.pallas_call(
        body, name=name, grid=(rows // rb,),
        in_specs=[pl.BlockSpec((N_DEV, rb, cols), lambda i: (0, i, 0)), flat, flat, flat],
        out_specs=[flat] * 4,
        out_shape=[jax.ShapeDtypeStruct((rows, cols), F32)] * 4,
        compiler_params=_params("parallel"),
    )(landing, w, m, v)


def _adamw_step(g, w, m, v):
    m2 = ADAM_B1 * m + (1.0 - ADAM_B1) * g
    v2 = ADAM_B2 * v + (1.0 - ADAM_B2) * (g * g)
    m_hat = m2 / (1.0 - ADAM_B1 ** ADAM_STEP)
    v_hat = v2 / (1.0 - ADAM_B2 ** ADAM_STEP)
    return -ADAM_LR * (m_hat / (jnp.sqrt(v_hat) + ADAM_EPS) + ADAM_WD * w), m2, v2


_REPLICATED = (
    ("mix_norm_g", D_MODEL), ("q_norm_g", Q_LORA), ("kv_norm_g", KV_LORA), ("conv_b", D_CONV), ("conv_ln_g", D_CONV),
    ("conv_ln_b", D_CONV), ("conv_out_g", D_CONV), ("attn_out_g", D_CONV), ("ffn_norm_g", D_MODEL),
    ("ffn_conv_b", D_UP), ("final_norm_g", D_MODEL),
)
_REPLICATED_WIDTH = sum(size for _, size in _REPLICATED) + _LANES

_WEIGHT_ORDER = (
    "meta_tokens", "mix_norm_g", "w_in", "q_norm_g", "w_uq", "kv_norm_g", "w_ukv", "conv_w", "conv_b", "conv_ln_g",
    "conv_ln_b", "conv_out_g", "attn_out_g", "w_out", "ffn_norm_g", "w_ffn_up", "ffn_conv_w", "ffn_conv_b",
    "w_ffn_down", "final_norm_g",
)


def _pack_replicated(grads, loss):
    rows = [grads[name].reshape(1, size) for name, size in _REPLICATED]
    return jnp.concatenate(rows + [jnp.broadcast_to(loss.reshape(1, 1), (1, _LANES))], axis=-1)


def _adamw_replicated(landing, weights, moments_m, moments_v):
    n = len(_REPLICATED)

    def body(*refs):
        l_ref, ins, outs = refs[0], refs[1:1 + 3 * n], refs[1 + 3 * n:]
        total = l_ref[0]
        for p in range(1, N_DEV):
            total = total + l_ref[p]
        at = 0
        for a, (_, size) in enumerate(_REPLICATED):
            g = total[:, at:at + size]
            w_ref, m_ref, v_ref = ins[3 * a:3 * a + 3]
            g_ref, d_ref, m2_ref, v2_ref = outs[4 * a:4 * a + 4]
            g_ref[...] = g
            d_ref[...], m2_ref[...], v2_ref[...] = _adamw_step(g, w_ref[...], m_ref[...], v_ref[...])
            at += size
        outs[-1][...] = total[:, at:at + _LANES]

    operands, out_shapes = [], []
    for name, size in _REPLICATED:
        operands += [weights[name].reshape(1, size), moments_m[name].reshape(1, size), moments_v[name].reshape(1, size)]
        out_shapes += [jax.ShapeDtypeStruct((1, size), F32)] * 4
    out_shapes.append(jax.ShapeDtypeStruct((1, _LANES), F32))
    outs = pl.pallas_call(body, name="adamw_replicated", out_shape=out_shapes)(landing, *operands)
    return outs[-1][0, 0], {name: outs[4 * a:4 * a + 4] for a, (name, _) in enumerate(_REPLICATED)}


def _pad_rows(a, rows):
    return jnp.pad(a, ((0, rows - a.shape[0]), (0, 0)))


def _slabs(a):
    r, c = a.shape
    return a.reshape(r, N_DEV, c // N_DEV).transpose(1, 0, 2)


def _unslab(a):
    g, r, c = a.shape
    return a.transpose(1, 0, 2).reshape(r, g * c)


def _local_step(x, target, w, n_rows, ffn_weights, send_grads):
    cos_t, sin_t = lax.optimization_barrier(_rope_tables(n_rows))
    cos, sin = cos_t.T, sin_t.T
    meta_pad, g1, gf = w["meta_pad"], w["mix_norm_g"], w["final_norm_g"]
    gq, gkv, gb_col = w["q_norm_g"], w["kv_norm_g"], w["attn_out_g"].reshape(D_ATTN, 1)
    nb, ag, cq, ckv, kr = _fwd_in(x, meta_pad, g1, w["w_in"], n_rows)
    mix_a, u1 = _fwd_conv(ag, w["conv_w"], w["conv_b"], w["conv_ln_g"], w["conv_ln_b"], w["conv_out_g"], n_rows)
    q_t, k, v, v_t, cqn, ckvn = _fwd_qkv(cq, ckv, kr, gq, gkv, w["wq_t"], w["w_ukv"], w["wv_t"], cos, sin, cos_t, sin_t, n_rows)
    o_t, lse = _attn_fwd(q_t, k, v_t, n_rows)
    w_out, w_up, w_down = ffn_weights()
    mix_bt, h1 = _fwd_out(x, meta_pad, mix_a, o_t, gb_col, w_out, n_rows)
    n2, up0, act, da, db, dh2, loss, dgf = _fwd_ffn(
        h1, target, w["ffn_norm_g"], w_up, w["fw"], w["fb"], w_down, gf, n_rows)

    dup, dfb, grad_w_down = _bwd_ffn_act(dh2, da, db, act, w_down, n_rows)
    dup0, dh1, dfw, dg2, grad_w_out = _bwd_ffn_up(
        dup, up0, h1, dh2, w["ffn_norm_g"], w_up, w["fw"], mix_a, mix_bt, n_rows)
    stage0 = {
        "w_ffn_up": _weight_grad(dup0, n2, "grad_w_ffn_up"),
        "w_ffn_down": grad_w_down.reshape(N_DEV, D_FF // N_DEV, D_MODEL),
        "w_out": grad_w_out.reshape(N_DEV, D_MODEL // N_DEV, D_MODEL),
    }
    stage0, dh1 = lax.optimization_barrier((stage0, dh1))
    send_grads(0, stage0)
    do_t, delta, du1, dgb, dga, dlg, dlb, dcb = _bwd_out(
        dh1, o_t, u1, w_out, gb_col, w["conv_ln_g"], w["conv_ln_b"], w["conv_out_g"], n_rows)
    dq_t, dk, dv = _attn_bwd(q_t, k, v, do_t, lse, delta, n_rows)
    dqraw_t, dkv, dcq, dckv, dkr, dgq, dgkv, grad_w_ukv = _bwd_qkv(
        dq_t, dk, dv, cq, ckv, ckvn, gq, gkv, w["wq_t"], w["w_ukv"], cos, sin, cos_t, sin_t, n_rows)
    dz, dcw = _bwd_conv(du1, ag, w["conv_w"], dcq, dckv, dkr, n_rows)
    stage1 = {
        "w_in": _weight_grad(dz, nb, "grad_w_in")[0].reshape(N_DEV, D_IN // N_DEV, D_MODEL),
        "w_uq": _weight_grad(dqraw_t.reshape(N_HEADS * QK_DIM, n_rows), cqn, "grad_w_uq", a_transposed=True)[0].reshape(
            N_HEADS, QK_DIM, Q_LORA),
        "w_ukv": _slabs(grad_w_ukv),
        "conv_w": _slabs(dcw),
        "ffn_conv_w": dfw[:, :, :UP_SLAB],
    }
    stage1, dz = lax.optimization_barrier((stage1, dz))
    send_grads(1, stage1)
    gx, gmeta, dg1 = _bwd_in(dz, x, meta_pad, dh1, g1, w["w_in"], n_rows)

    sharded = {"meta_tokens": _slabs(gmeta[DEAD:])}
    replicated = {
        "mix_norm_g": dg1, "q_norm_g": dgq, "kv_norm_g": dgkv, "conv_b": dcb, "conv_ln_g": dlg, "conv_ln_b": dlb,
        "conv_out_g": dga, "attn_out_g": dgb, "ffn_norm_g": dg2, "ffn_conv_b": dfb, "final_norm_g": dgf,
    }
    return loss[0, 0], gx, sharded, replicated


_SHARDED = (
    ("w_in", None, BF16), ("w_uq", None, BF16), ("w_ukv", None, BF16), ("w_out", None, BF16), ("w_ffn_up", None, BF16),
    ("w_ffn_down", None, BF16), ("conv_w", 32, F32), ("ffn_conv_w", 8, F32), ("meta_tokens", None, F32),
)
GATHER_LATE_ID = 3
EXCHANGE_STAGE_IDS = (4, 5)
_LATE_WEIGHTS = ("w_out", "w_ffn_up", "w_ffn_down")
_COLUMN_SHARDS = ("w_in", "w_uq", "w_ffn_up")


def kernel(x, meta_tokens, mix_norm_g, w_in, q_norm_g, w_uq, kv_norm_g, w_ukv, conv_w, conv_b, conv_ln_g, conv_ln_b, conv_out_g, attn_out_g, w_out, ffn_norm_g, w_ffn_up, ffn_conv_w, ffn_conv_b, w_ffn_down, final_norm_g, loss_target, m_meta_tokens, m_mix_norm_g, m_w_in, m_q_norm_g, m_w_uq, m_kv_norm_g, m_w_ukv, m_conv_w, m_conv_b, m_conv_ln_g, m_conv_ln_b, m_conv_out_g, m_attn_out_g, m_w_out, m_ffn_norm_g, m_w_ffn_up, m_ffn_conv_w, m_ffn_conv_b, m_w_ffn_down, m_final_norm_g, v_meta_tokens, v_mix_norm_g, v_w_in, v_q_norm_g, v_w_uq, v_kv_norm_g, v_w_ukv, v_conv_w, v_conv_b, v_conv_ln_g, v_conv_ln_b, v_conv_out_g, v_attn_out_g, v_w_out, v_ffn_norm_g, v_w_ffn_up, v_ffn_conv_w, v_ffn_conv_b, v_w_ffn_down, v_final_norm_g):
    given = dict(locals())
    weights = {name: given[name] for name in _WEIGHT_ORDER}
    moments_m = {name: given["m_" + name] for name in _WEIGHT_ORDER}
    moments_v = {name: given["v_" + name] for name in _WEIGHT_ORDER}
    seq = x.shape[1]
    n_rows = ROW_TILE + seq

    def shard2d(name, a):
        a = a.reshape(a.shape[-2], a.shape[-1])
        return a.T if name in _COLUMN_SHARDS else a

    early = [entry for entry in _SHARDED if entry[0] not in _LATE_WEIGHTS]
    shards = []
    for name, pad_to, _ in early:
        s = shard2d(name, weights[name])
        shards.append(s if pad_to is None else _pad_rows(s, pad_to))
    gathered = dict(zip([name for name, _, _ in early], _all_gather(shards, [dt for _, _, dt in early])))
    late_shards, gathered["meta_tokens"] = lax.optimization_barrier(
        ([shard2d(name, weights[name]) for name in _LATE_WEIGHTS], gathered["meta_tokens"]))
    late_parts = [s.astype(BF16) for s in late_shards]
    late = _sequencer_exchange(late_parts, [True] * len(late_parts), "gather_late", GATHER_LATE_ID)
    meta_full = _unslab(gathered["meta_tokens"])
    full = {
        "meta_pad": jnp.concatenate([jnp.zeros((DEAD, D_MODEL), F32), meta_full], axis=0),
        "w_in": gathered["w_in"].reshape(D_IN, D_MODEL),
        "wq_t": gathered["w_uq"],
        "w_ukv": gathered["w_ukv"],
        "wv_t": gathered["w_ukv"][:, :, QK_NOPE:].transpose(0, 2, 1),
        "conv_w": _unslab(gathered["conv_w"][:, :CONV_WIDTH]),
        "fw": jnp.pad(gathered["ffn_conv_w"][:, :FFN_CONV_WIDTH], ((0, 0), (0, 0), (0, UP_PAD - UP_SLAB))),
        "fb": jnp.pad(ffn_conv_b.reshape(N_DEV, 1, UP_SLAB), ((0, 0), (0, 0), (0, UP_PAD - UP_SLAB))),
        "final_norm_g": final_norm_g.reshape(1, D_MODEL),
    }
    for name in ("mix_norm_g", "q_norm_g", "kv_norm_g", "conv_b", "conv_ln_g", "conv_ln_b", "conv_out_g", "attn_out_g",
                 "ffn_norm_g"):
        full[name] = weights[name]

    def ffn_weights():
        w_out_all, w_up_all, w_down_all = late
        return (w_out_all.reshape(D_MODEL, D_MODEL), w_up_all, w_down_all.reshape(N_ACT_SLAB, UP_SLAB, D_MODEL))

    wire = {name: (pad_to, dt) for name, pad_to, dt in _SHARDED}
    landing = {}

    def on_the_wire(name, slabs):
        pad_to, dt = wire[name]
        slabs = slabs.astype(dt)
        return slabs if pad_to is None else jnp.pad(slabs, ((0, 0), (0, pad_to - slabs.shape[1]), (0, 0)))

    def send_grads(stage, grads):
        parts = [on_the_wire(name, slabs) for name, slabs in grads.items()]
        if landing:
            arrived = list(landing)
            parts, held = lax.optimization_barrier((parts, [landing[name] for name in arrived]))
            landing.update(zip(arrived, held))
        landed = _sequencer_exchange(parts, [False] * len(parts), f"exchange_stage{stage}", EXCHANGE_STAGE_IDS[stage])
        landing.update(zip(grads, landed))

    loss, gx, sharded, replicated = _local_step(x[0], loss_target[0], full, n_rows, ffn_weights, send_grads)

    parts = [on_the_wire(name, slabs) for name, slabs in sharded.items()] + [_pack_replicated(replicated, loss)]
    landed = _exchange(parts, [False] * len(sharded) + [True])
    landing.update(zip(sharded, landed[:-1]))

    grad, delta, new_m, new_v = {}, {}, {}, {}
    for name, pad_to, _ in _SHARDED:
        land = landing[name]
        ws, ms, vs = (shard2d(name, a[name]) for a in (weights, moments_m, moments_v))
        rows = ws.shape[0]
        if pad_to is not None:
            ws, ms, vs = _pad_rows(ws, pad_to), _pad_rows(ms, pad_to), _pad_rows(vs, pad_to)
        outs = _adamw(land, ws, ms, vs, "adamw_" + name)
        shape = weights[name].shape
        grad[name], delta[name], new_m[name], new_v[name] = (
            (o.T if name in _COLUMN_SHARDS else o[:rows]).reshape(shape) for o in outs)
    loss, updates = _adamw_replicated(landed[-1], weights, moments_m, moments_v)
    for name, outs in updates.items():
        grad[name], delta[name], new_m[name], new_v[name] = (o.reshape(weights[name].shape) for o in outs)

    return (loss, gx[None], *[grad[n] for n in _WEIGHT_ORDER], *[delta[n] for n in _WEIGHT_ORDER],
            *[new_m[n] for n in _WEIGHT_ORDER], *[new_v[n] for n in _WEIGHT_ORDER])
```
